```python
import jax, jax.numpy as jnp
from jax import lax
import numpy as np

D_MODEL = 1024
BATCH = 8
SEQ = 8192
DEPTH = 1

CHUNK = 64
N_LEFT_CHUNKS = 8
BAND = (N_LEFT_CHUNKS + 1) * CHUNK
ATT_HEADS = 8
ATT_HEAD_DIM = 64
D_ATT = ATT_HEADS * ATT_HEAD_DIM
D_CONV = D_MODEL // 2
CONV_WIDTH = 3
MAX_REL = 128
N_REL = 2 * MAX_REL + 1
EPS = 1e-6
NEG_BIG = -1e30
IN_SIZES = (D_ATT, D_ATT, D_ATT, D_ATT, D_CONV, D_CONV, D_CONV, D_CONV, D_MODEL, D_MODEL)
IN_COLS = sum(IN_SIZES)
IN_SPLITS = tuple(int(s) for s in np.cumsum(IN_SIZES)[:-1])

kernel_name = "hybrid_chunk_attn_shortconv_gated_block"


def rmsnorm(x, g):
    xf = x.astype(jnp.float32)
    r = lax.rsqrt(jnp.mean(xf * xf, axis=-1, keepdims=True) + EPS)
    return (xf * r).astype(x.dtype) * g


def chunked_rel_attention(q, k, v, rel_bias):
    b, s, h, dh = q.shape
    nc = s // CHUNK
    qc = q.reshape(b, nc, CHUNK, h, dh)
    kc = k.reshape(b, nc, CHUNK, h, dh)
    vc = v.reshape(b, nc, CHUNK, h, dh)
    pad = ((0, 0), (N_LEFT_CHUNKS, 0), (0, 0), (0, 0), (0, 0))
    kp = jnp.pad(kc, pad)
    vp = jnp.pad(vc, pad)
    kb = jnp.concatenate([kp[:, j:j + nc] for j in range(N_LEFT_CHUNKS + 1)], axis=2)
    vb = jnp.concatenate([vp[:, j:j + nc] for j in range(N_LEFT_CHUNKS + 1)], axis=2)
    scale = ATT_HEAD_DIM ** -0.5
    scores = jnp.einsum('bcqhd,bckhd->bhcqk', qc, kb).astype(jnp.float32) * scale
    rel = np.arange(CHUNK)[:, None] + N_LEFT_CHUNKS * CHUNK - np.arange(BAND)[None, :]
    idx = np.clip(rel, -MAX_REL, MAX_REL) + MAX_REL
    bias = rel_bias[:, idx].astype(jnp.float32)
    kpos = np.arange(nc)[:, None] * CHUNK - N_LEFT_CHUNKS * CHUNK + np.arange(BAND)[None, :]
    valid = jnp.asarray(kpos >= 0)
    scores = jnp.where(valid[None, None, :, None, :], scores + bias[None, :, None], NEG_BIG)
    p = jax.nn.softmax(scores, axis=-1).astype(v.dtype)
    out = jnp.einsum('bhcqk,bckhd->bcqhd', p, vb)
    return out.reshape(b, s, h * dh)


def causal_depthwise_conv(u, w, bias):
    s = u.shape[1]
    up = jnp.pad(u, ((0, 0), (CONV_WIDTH - 1, 0), (0, 0)))
    return sum(w[t] * up[:, t:t + s] for t in range(CONV_WIDTH)) + bias


def _fwd_setup_inputs(seed: int = 0) -> dict:
    key = jax.random.key(seed)
    ks = jax.random.split(key, 12)
    f32 = jnp.float32
    x = jax.random.normal(ks[0], (BATCH, SEQ, D_MODEL), f32)
    norm_g = 1.0 + 0.05 * jax.random.normal(ks[1], (DEPTH, D_MODEL), f32)
    w_in = jax.random.normal(ks[2], (DEPTH, D_MODEL, IN_COLS), f32) * D_MODEL ** -0.5
    rel_bias = 0.2 * jax.random.normal(ks[3], (DEPTH, ATT_HEADS, N_REL), f32)
    w_att_out = jax.random.normal(ks[4], (DEPTH, D_ATT, D_MODEL), f32) * D_ATT ** -0.5
    conv_w = jax.random.normal(ks[5], (DEPTH, CONV_WIDTH, D_CONV), f32) * CONV_WIDTH ** -0.5
    conv_b = 0.02 * jax.random.normal(ks[6], (DEPTH, D_CONV), f32)
    w_conv_out = jax.random.normal(ks[7], (DEPTH, D_CONV, D_MODEL), f32) * D_CONV ** -0.5
    w_out = jax.random.normal(ks[8], (DEPTH, D_MODEL, D_MODEL), f32) * D_MODEL ** -0.5
    final_norm_g = 1.0 + 0.05 * jax.random.normal(ks[9], (D_MODEL,), f32)
    return {"x": x, "norm_g": norm_g, "w_in": w_in, "rel_bias": rel_bias,
            "w_att_out": w_att_out, "conv_w": conv_w, "conv_b": conv_b,
            "w_conv_out": w_conv_out, "w_out": w_out, "final_norm_g": final_norm_g}


def _fwd_reference(x, norm_g, w_in, rel_bias, w_att_out, conv_w, conv_b, w_conv_out, w_out, final_norm_g):
    b, s, _ = x.shape
    for l in range(DEPTH):
        h = rmsnorm(x, norm_g[l])
        proj = jnp.einsum('bsd,de->bse', h, w_in[l])
        q, k, v, z_att, gb, gc, u, z_conv, g_att, g_conv = jnp.split(proj, IN_SPLITS, axis=-1)
        att = chunked_rel_attention(q.reshape(b, s, ATT_HEADS, ATT_HEAD_DIM),
                                    k.reshape(b, s, ATT_HEADS, ATT_HEAD_DIM),
                                    v.reshape(b, s, ATT_HEADS, ATT_HEAD_DIM),
                                    rel_bias[l])
        y_att = jnp.einsum('bsc,cd->bsd', att * jax.nn.silu(z_att), w_att_out[l])
        vconv = causal_depthwise_conv(gc * u, conv_w[l], conv_b[l])
        y_conv = jnp.einsum('bsc,cd->bsd', gb * vconv * jax.nn.silu(z_conv), w_conv_out[l])
        m = jax.nn.sigmoid(g_att) * y_att + jax.nn.sigmoid(g_conv) * y_conv
        x = x + jnp.einsum('bsd,de->bse', m, w_out[l])
    return rmsnorm(x, final_norm_g)


import jax as _jax
import jax.numpy as _jnp

TWIN_FORMAT = 'train_step'
FWD_PARAMS = ['x', 'norm_g', 'w_in', 'rel_bias', 'w_att_out', 'conv_w', 'conv_b', 'w_conv_out', 'w_out', 'final_norm_g']
TWIN_WEIGHTS = ['norm_g', 'w_in', 'rel_bias', 'w_att_out', 'conv_w', 'conv_b', 'w_conv_out', 'w_out', 'final_norm_g']
TWIN_DIFF_INPUT = 'x'
TWIN_INPUTS = ['x', 'norm_g', 'w_in', 'rel_bias', 'w_att_out', 'conv_w', 'conv_b', 'w_conv_out', 'w_out', 'final_norm_g', 'loss_target', 'm_norm_g', 'm_w_in', 'm_rel_bias', 'm_w_att_out', 'm_conv_w', 'm_conv_b', 'm_w_conv_out', 'm_w_out', 'm_final_norm_g', 'v_norm_g', 'v_w_in', 'v_rel_bias', 'v_w_att_out', 'v_conv_w', 'v_conv_b', 'v_w_conv_out', 'v_w_out', 'v_final_norm_g']
TWIN_OUTPUTS = ['loss', 'grad_x', 'grad_norm_g', 'grad_w_in', 'grad_rel_bias', 'grad_w_att_out', 'grad_conv_w', 'grad_conv_b', 'grad_w_conv_out', 'grad_w_out', 'grad_final_norm_g', 'delta_norm_g', 'delta_w_in', 'delta_rel_bias', 'delta_w_att_out', 'delta_conv_w', 'delta_conv_b', 'delta_w_conv_out', 'delta_w_out', 'delta_final_norm_g', 'new_m_norm_g', 'new_m_w_in', 'new_m_rel_bias', 'new_m_w_att_out', 'new_m_conv_w', 'new_m_conv_b', 'new_m_w_conv_out', 'new_m_w_out', 'new_m_final_norm_g', 'new_v_norm_g', 'new_v_w_in', 'new_v_rel_bias', 'new_v_w_att_out', 'new_v_conv_w', 'new_v_conv_b', 'new_v_w_conv_out', 'new_v_w_out', 'new_v_final_norm_g']
TWIN_LEAF_KINDS = {'loss': 'loss', 'grad_x': 'grad_x', 'grad_norm_g': 'grad_w', 'grad_w_in': 'grad_w', 'grad_rel_bias': 'grad_w', 'grad_w_att_out': 'grad_w', 'grad_conv_w': 'grad_w', 'grad_conv_b': 'grad_w', 'grad_w_conv_out': 'grad_w', 'grad_w_out': 'grad_w', 'grad_final_norm_g': 'grad_w', 'delta_norm_g': 'delta_w', 'delta_w_in': 'delta_w', 'delta_rel_bias': 'delta_w', 'delta_w_att_out': 'delta_w', 'delta_conv_w': 'delta_w', 'delta_conv_b': 'delta_w', 'delta_w_conv_out': 'delta_w', 'delta_w_out': 'delta_w', 'delta_final_norm_g': 'delta_w', 'new_m_norm_g': 'new_m', 'new_m_w_in': 'new_m', 'new_m_rel_bias': 'new_m', 'new_m_w_att_out': 'new_m', 'new_m_conv_w': 'new_m', 'new_m_conv_b': 'new_m', 'new_m_w_conv_out': 'new_m', 'new_m_w_out': 'new_m', 'new_m_final_norm_g': 'new_m', 'new_v_norm_g': 'new_v', 'new_v_w_in': 'new_v', 'new_v_rel_bias': 'new_v', 'new_v_w_att_out': 'new_v', 'new_v_conv_w': 'new_v', 'new_v_conv_b': 'new_v', 'new_v_w_conv_out': 'new_v', 'new_v_w_out': 'new_v', 'new_v_final_norm_g': 'new_v'}


def _forward(args):
    return _fwd_reference(*[args[k] for k in FWD_PARAMS])


def _output_shape():
    def fwd():
        inp = _fwd_setup_inputs(0)
        return _fwd_reference(*[inp[k] for k in FWD_PARAMS])
    out = _jax.eval_shape(fwd)
    return out.shape, out.dtype

N_MICROBATCH = 1
ADAM_LR = 0.001
ADAM_B1 = 0.9
ADAM_B2 = 0.999
ADAM_EPS = 1e-08
ADAM_WD = 0.01
ADAM_STEP = 10
PER_EXAMPLE_BATCH_AXIS = {'x': 0, 'loss_target': 0}
SHARED_INPUTS = []
_WEIGHT_DTYPES = {'norm_g': _jnp.float32, 'w_in': _jnp.float32, 'rel_bias': _jnp.float32, 'w_att_out': _jnp.float32, 'conv_w': _jnp.float32, 'conv_b': _jnp.float32, 'w_conv_out': _jnp.float32, 'w_out': _jnp.float32, 'final_norm_g': _jnp.float32}
MOMENT_SCALE = {'norm_g': 1.695038e-01, 'w_in': 6.544743e-02, 'rel_bias': 7.437401e-03, 'w_att_out': 1.151508e-02, 'conv_w': 1.070367e-01, 'conv_b': 1.112377e-01, 'w_conv_out': 7.895804e-02, 'w_out': 7.950127e-02, 'final_norm_g': 6.405586e+01}


def _to_microbatches(a, axis):
    t = _jnp.moveaxis(a, axis, 0)
    t = t.reshape((N_MICROBATCH, t.shape[0] // N_MICROBATCH) + t.shape[1:])
    return _jnp.moveaxis(t, 1, axis + 1)


def setup_inputs(seed: int = 0) -> dict:
    inp = _fwd_setup_inputs(seed)
    key = _jax.random.fold_in(_jax.random.key(seed), 7919)
    shape, _ = _output_shape()
    out = dict(inp)
    out["loss_target"] = _jax.random.normal(_jax.random.fold_in(key, 0), shape, _jnp.float32)
    for i, name in enumerate(TWIN_WEIGHTS):
        w = inp[name].astype(_jnp.float32)
        if MOMENT_SCALE is None:
            s = _jnp.sqrt(_jnp.mean(_jnp.square(w)) + 1e-30)
        else:
            s = MOMENT_SCALE[name]
        km, kv = _jax.random.split(_jax.random.fold_in(key, i + 1))
        out[name] = w
        out["m_" + name] = s * _jax.random.normal(km, w.shape, _jnp.float32)
        out["v_" + name] = (s * s) * _jax.random.uniform(kv, w.shape, _jnp.float32, 0.5, 1.5)
    if N_MICROBATCH > 1:
        for name, axis in PER_EXAMPLE_BATCH_AXIS.items():
            out[name] = _to_microbatches(out[name], axis)
    return {'x': out['x'], 'norm_g': out['norm_g'], 'w_in': out['w_in'], 'rel_bias': out['rel_bias'], 'w_att_out': out['w_att_out'], 'conv_w': out['conv_w'], 'conv_b': out['conv_b'], 'w_conv_out': out['w_conv_out'], 'w_out': out['w_out'], 'final_norm_g': out['final_norm_g'], 'loss_target': out['loss_target'], 'm_norm_g': out['m_norm_g'], 'm_w_in': out['m_w_in'], 'm_rel_bias': out['m_rel_bias'], 'm_w_att_out': out['m_w_att_out'], 'm_conv_w': out['m_conv_w'], 'm_conv_b': out['m_conv_b'], 'm_w_conv_out': out['m_w_conv_out'], 'm_w_out': out['m_w_out'], 'm_final_norm_g': out['m_final_norm_g'], 'v_norm_g': out['v_norm_g'], 'v_w_in': out['v_w_in'], 'v_rel_bias': out['v_rel_bias'], 'v_w_att_out': out['v_w_att_out'], 'v_conv_w': out['v_conv_w'], 'v_conv_b': out['v_conv_b'], 'v_w_conv_out': out['v_w_conv_out'], 'v_w_out': out['v_w_out'], 'v_final_norm_g': out['v_final_norm_g']}


def _loss(weights, diff, rest, loss_target):
    with _jax.named_scope("forward"):
        args = {**rest, TWIN_DIFF_INPUT: diff, **{k: w.astype(_WEIGHT_DTYPES[k]) for k, w in weights.items()}}
        y = _forward(args)
    with _jax.named_scope("loss_head"):
        err = _jnp.square(y.astype(_jnp.float32) - loss_target)
        return 0.5 * _jnp.sum(_jnp.mean(err, axis=-1)) if err.ndim else 0.5 * err


def _adamw(w, g, m, v):
    m = ADAM_B1 * m + (1.0 - ADAM_B1) * g
    v = ADAM_B2 * v + (1.0 - ADAM_B2) * _jnp.square(g)
    m_hat = m / (1.0 - ADAM_B1 ** ADAM_STEP)
    v_hat = v / (1.0 - ADAM_B2 ** ADAM_STEP)
    delta = -ADAM_LR * (m_hat / (_jnp.sqrt(v_hat) + ADAM_EPS) + ADAM_WD * w)
    return delta, m, v


def reference(x, norm_g, w_in, rel_bias, w_att_out, conv_w, conv_b, w_conv_out, w_out, final_norm_g, loss_target, m_norm_g, m_w_in, m_rel_bias, m_w_att_out, m_conv_w, m_conv_b, m_w_conv_out, m_w_out, m_final_norm_g, v_norm_g, v_w_in, v_rel_bias, v_w_att_out, v_conv_w, v_conv_b, v_w_conv_out, v_w_out, v_final_norm_g):
    given = dict(x=x, norm_g=norm_g, w_in=w_in, rel_bias=rel_bias, w_att_out=w_att_out, conv_w=conv_w, conv_b=conv_b, w_conv_out=w_conv_out, w_out=w_out, final_norm_g=final_norm_g, loss_target=loss_target, m_norm_g=m_norm_g, m_w_in=m_w_in, m_rel_bias=m_rel_bias, m_w_att_out=m_w_att_out, m_conv_w=m_conv_w, m_conv_b=m_conv_b, m_w_conv_out=m_w_conv_out, m_w_out=m_w_out, m_final_norm_g=m_final_norm_g, v_norm_g=v_norm_g, v_w_in=v_w_in, v_rel_bias=v_rel_bias, v_w_att_out=v_w_att_out, v_conv_w=v_conv_w, v_conv_b=v_conv_b, v_w_conv_out=v_w_conv_out, v_w_out=v_w_out, v_final_norm_g=v_final_norm_g)
    weights = {n: given[n] for n in TWIN_WEIGHTS}
    shared = {n: given[n] for n in SHARED_INPUTS}
    per_example = {n: given[n] for n in ['x']}
    grad_fn = _jax.value_and_grad(_loss, argnums=(0, 1))

    def one_microbatch(ex, loss_target):
        ex = dict(ex)
        diff = ex.pop(TWIN_DIFF_INPUT)
        return grad_fn(weights, diff, {**shared, **ex}, loss_target)

    if N_MICROBATCH == 1:
        loss, (grad_w, grad_x) = one_microbatch(per_example, given["loss_target"])
    else:
        def body(carry, xs):
            loss_sum, grad_sum = carry
            l_k, (gw_k, gx_k) = one_microbatch(xs[0], xs[1])
            with _jax.named_scope("update"):
                return (loss_sum + l_k, _jax.tree.map(_jnp.add, grad_sum, gw_k)), gx_k

        init = (_jnp.zeros((), _jnp.float32), _jax.tree.map(_jnp.zeros_like, weights))
        (loss, grad_w), grad_x = _jax.lax.scan(body, init, (per_example, given["loss_target"]))
    with _jax.named_scope("update"):
        delta_w, new_m, new_v = {}, {}, {}
        for n in TWIN_WEIGHTS:
            delta_w[n], new_m[n], new_v[n] = _adamw(weights[n], grad_w[n], given["m_" + n], given["v_" + n])
    return (loss, grad_x, *[grad_w[n] for n in TWIN_WEIGHTS], *[delta_w[n] for n in TWIN_WEIGHTS],
            *[new_m[n] for n in TWIN_WEIGHTS], *[new_v[n] for n in TWIN_WEIGHTS])
```

```python
import functools

import numpy as np
import jax
import jax.numpy as jnp
from jax import lax
from jax.experimental import pallas as pl
from jax.experimental.pallas import tpu as pltpu

F32 = jnp.float32
BF16 = jnp.bfloat16
MESH = pl.DeviceIdType.MESH

CHUNK = 64
N_LEFT = 8
HEADS = 8
HEAD_DIM = 64
D_ATT = HEADS * HEAD_DIM
MAX_REL = 128
N_REL = 2 * MAX_REL + 1
EPS = 1e-6
NEG_BIG = -1e30
ADAM_LR, ADAM_B1, ADAM_B2, ADAM_EPS, ADAM_WD, ADAM_STEP = 0.001, 0.9, 0.999, 1e-08, 0.01, 10

LANES = 128
SUBLANES = 8
VMEM_LIMIT = 56 * 1024 * 1024

QB = 2 * CHUNK
KW = N_LEFT * CHUNK + QB
DIAG = KW + QB
TQ = N_LEFT * CHUNK
TM_MID = 256
TM_MM = 512
SMALL_ROWS, SMALL_COLS = 16, 1024


def _params(*sem):
    return pltpu.CompilerParams(dimension_semantics=sem, vmem_limit_bytes=VMEM_LIMIT)


def _nt(a, b):
    return lax.dot_general(a, b, (((1,), (1,)), ((), ())), preferred_element_type=F32)


def _tn(a, b):
    return lax.dot_general(a, b, (((0,), (0,)), ((), ())), preferred_element_type=F32)


def _nn(a, b):
    return jnp.dot(a, b, preferred_element_type=F32)


def _diag_rel_index():
    d = np.arange(DIAG)
    diff = np.where(d < KW, d, d - DIAG)
    rel = N_LEFT * CHUNK - diff
    return np.clip(rel, -MAX_REL, MAX_REL) + MAX_REL


def _build_bias(diag_ref, bias_scr):
    r = lax.broadcasted_iota(jnp.int32, (QB, KW), 0) // CHUNK
    s = lax.broadcasted_iota(jnp.int32, (QB, KW), 1) // CHUNK
    allowed = (s >= r) & (s <= r + N_LEFT)
    for h in range(HEADS):
        row = jnp.broadcast_to(diag_ref[h:h + 1, :], (QB, DIAG))
        t = pltpu.roll(row, 0, 1, stride=1, stride_axis=0)
        bias_scr[h] = jnp.where(allowed, t[:, :KW], NEG_BIG)


def _rmsnorm_fwd(x, g):
    s, d = x.shape

    def body(x_ref, g_ref, h_ref):
        xv = x_ref[...]
        r = lax.rsqrt(jnp.mean(xv * xv, axis=-1, keepdims=True) + EPS)
        h_ref[...] = ((xv * r) * g_ref[...]).astype(BF16)

    return pl.pallas_call(
        body, name="rmsnorm_fwd", grid=(s // TM_MM,),
        in_specs=[pl.BlockSpec((TM_MM, d), lambda i: (i, 0)), pl.BlockSpec((1, d), lambda i: (0, 0))],
        out_specs=pl.BlockSpec((TM_MM, d), lambda i: (i, 0)),
        out_shape=jax.ShapeDtypeStruct((s, d), BF16),
        compiler_params=_params("parallel"),
    )(x, g)


def _in_proj(h, w, tn):
    s, d = h.shape
    e = w.shape[1]

    def body(h_ref, w_ref, o_ref):
        o_ref[...] = _nn(h_ref[...], w_ref[...]).astype(BF16)

    return pl.pallas_call(
        body, name="in_proj", grid=(e // tn, s // TM_MM),
        in_specs=[pl.BlockSpec((TM_MM, d), lambda j, i: (i, 0)), pl.BlockSpec((d, tn), lambda j, i: (0, j))],
        out_specs=pl.BlockSpec((TM_MM, tn), lambda j, i: (i, j)),
        out_shape=jax.ShapeDtypeStruct((s, e), BF16),
        compiler_params=_params("parallel", "parallel"),
    )(h, w)


def _attn_fwd(diag, proj):
    s = proj.shape[0]
    n = s // TQ
    scale = HEAD_DIM ** -0.5

    def body(diag_ref, q_ref, kp_ref, kc_ref, vp_ref, vc_ref, o_ref, lse_ref, bias_scr, kcat, vcat):
        i = pl.program_id(0)

        @pl.when(i == 0)
        def _():
            _build_bias(diag_ref, bias_scr)

        kcat[0:TQ, :] = kp_ref[...]
        kcat[TQ:2 * TQ, :] = kc_ref[...]
        vcat[0:TQ, :] = vp_ref[...]
        vcat[TQ:2 * TQ, :] = vc_ref[...]
        lane_hi = lax.broadcasted_iota(jnp.int32, (QB, LANES), 1) >= HEAD_DIM
        col = lax.broadcasted_iota(jnp.int32, (QB, KW), 1)

        def block(b, carry):
            r0 = pl.multiple_of(b * QB, QB)
            valid = col >= jnp.where(i == 0, TQ - r0, 0)
            for p in range(HEADS // 2):
                lanes = slice(LANES * p, LANES * (p + 1))
                qp = q_ref[pl.ds(r0, QB), lanes] * scale
                kw = kcat[pl.ds(r0, KW), lanes]
                vw = vcat[pl.ds(r0, KW), lanes]
                outs = []
                for e in range(2):
                    hd = 2 * p + e
                    mask = lane_hi if e else jnp.logical_not(lane_hi)
                    qm = jnp.where(mask, qp, jnp.zeros_like(qp))
                    sc = jnp.where(valid, _nt(qm, kw) + bias_scr[hd], NEG_BIG)
                    m = jnp.max(sc, axis=1, keepdims=True)
                    pe = jnp.exp(sc - m)
                    l = jnp.sum(pe, axis=1, keepdims=True)
                    outs.append(_nn(pe.astype(BF16), vw) / l)
                    lse_ref[pl.ds(r0, QB), hd:hd + 1] = m + jnp.log(l)
                o_ref[pl.ds(r0, QB), lanes] = jnp.where(lane_hi, outs[1], outs[0]).astype(BF16)
            return carry

        lax.fori_loop(0, TQ // QB, block, 0)

    blk = lambda col_blk, prev: pl.BlockSpec(
        (TQ, D_ATT), (lambda i: (jnp.maximum(i - 1, 0), col_blk)) if prev else (lambda i: (i, col_blk)))
    return pl.pallas_call(
        body, name="attn_fwd", grid=(n,),
        in_specs=[pl.BlockSpec((HEADS, DIAG), lambda i: (0, 0)),
                  blk(0, False), blk(1, True), blk(1, False), blk(2, True), blk(2, False)],
        out_specs=[pl.BlockSpec((TQ, D_ATT), lambda i: (i, 0)), pl.BlockSpec((TQ, HEADS), lambda i: (i, 0))],
        out_shape=[jax.ShapeDtypeStruct((s, D_ATT), BF16), jax.ShapeDtypeStruct((s, HEADS), F32)],
        scratch_shapes=[pltpu.VMEM((HEADS, QB, KW), F32), pltpu.VMEM((2 * TQ, D_ATT), BF16),
                        pltpu.VMEM((2 * TQ, D_ATT), BF16)],
        compiler_params=_params("arbitrary"),
    )(diag, proj, proj, proj, proj, proj)


def _attn_bwd(diag, proj, d_att, att, lse):
    s = proj.shape[0]
    n = s // TQ
    scale = HEAD_DIM ** -0.5
    rel_pad = 3 * LANES

    def body(diag_ref, q_ref, kp_ref, kc_ref, vp_ref, vc_ref, do_ref, o_ref, lse_ref, dqkv_ref, dbias_ref,
             bias_scr, dbias_acc, kcat, vcat, dk_acc, dv_acc, dq_scr):
        i = pl.program_id(0)

        @pl.when(i == 0)
        def _():
            _build_bias(diag_ref, bias_scr)
            dbias_acc[...] = jnp.zeros_like(dbias_acc)
            dk_acc[...] = jnp.zeros_like(dk_acc)
            dv_acc[...] = jnp.zeros_like(dv_acc)

        @pl.when(i > 0)
        def _():
            dqkv_ref[:, 0:D_ATT] = dq_scr[...]
            dk_acc[0:TQ, :] = dk_acc[TQ:2 * TQ, :]
            dk_acc[TQ:2 * TQ, :] = jnp.zeros((TQ, D_ATT), F32)
            dv_acc[0:TQ, :] = dv_acc[TQ:2 * TQ, :]
            dv_acc[TQ:2 * TQ, :] = jnp.zeros((TQ, D_ATT), F32)

        @pl.when(i < n)
        def _():
            kcat[0:TQ, :] = kp_ref[...]
            kcat[TQ:2 * TQ, :] = kc_ref[...]
            vcat[0:TQ, :] = vp_ref[...]
            vcat[TQ:2 * TQ, :] = vc_ref[...]
            lane_hi = lax.broadcasted_iota(jnp.int32, (QB, LANES), 1) >= HEAD_DIM
            col = lax.broadcasted_iota(jnp.int32, (QB, KW), 1)

            def block(b, carry):
                r0 = pl.multiple_of(b * QB, QB)
                valid = col >= jnp.where(i == 0, TQ - r0, 0)
                for p in range(HEADS // 2):
                    lanes = slice(LANES * p, LANES * (p + 1))
                    qp = q_ref[pl.ds(r0, QB), lanes] * scale
                    kw = kcat[pl.ds(r0, KW), lanes]
                    vw = vcat[pl.ds(r0, KW), lanes]
                    dop = do_ref[pl.ds(r0, QB), lanes]
                    prod = dop.astype(F32) * o_ref[pl.ds(r0, QB), lanes].astype(F32)
                    dqs, dkp, dvp = [], None, None
                    for e in range(2):
                        hd = 2 * p + e
                        mask = lane_hi if e else jnp.logical_not(lane_hi)
                        qm = jnp.where(mask, qp, jnp.zeros_like(qp))
                        dom = jnp.where(mask, dop, jnp.zeros_like(dop))
                        delta = jnp.sum(jnp.where(mask, prod, 0.0), axis=1, keepdims=True)
                        sc = jnp.where(valid, _nt(qm, kw) + bias_scr[hd], NEG_BIG)
                        pr = jnp.exp(sc - lse_ref[pl.ds(r0, QB), hd:hd + 1])
                        ds = pr * (_nt(dom, vw) - delta)
                        dbias_acc[hd] += ds
                        dsb = ds.astype(BF16)
                        dv_e = _tn(pr.astype(BF16), dom)
                        dk_e = _tn(dsb, qm)
                        dvp = dv_e if dvp is None else dvp + dv_e
                        dkp = dk_e if dkp is None else dkp + dk_e
                        dqs.append(_nn(dsb, kw))
                    dv_acc[pl.ds(r0, KW), lanes] += dvp
                    dk_acc[pl.ds(r0, KW), lanes] += dkp
                    dq = jnp.where(lane_hi, dqs[1], dqs[0]) * (HEAD_DIM ** -0.5)
                    dq_scr[pl.ds(r0, QB), lanes] = dq.astype(BF16)
                return carry

            lax.fori_loop(0, TQ // QB, block, 0)

        @pl.when(i > 0)
        def _():
            dqkv_ref[:, D_ATT:2 * D_ATT] = dk_acc[0:TQ, :].astype(BF16)
            dqkv_ref[:, 2 * D_ATT:3 * D_ATT] = dv_acc[0:TQ, :].astype(BF16)

        @pl.when(i == n)
        def _():
            d_iota = lax.broadcasted_iota(jnp.int32, (DIAG, rel_pad), 0)
            n_iota = lax.broadcasted_iota(jnp.int32, (DIAG, rel_pad), 1)
            diff = jnp.where(d_iota < KW, d_iota, d_iota - DIAG)
            idx = jnp.clip(N_LEFT * CHUNK - diff, -MAX_REL, MAX_REL) + MAX_REL
            onehot = (idx == n_iota).astype(F32)
            rows = []
            for hd in range(HEADS):
                a = jnp.concatenate([dbias_acc[hd], jnp.zeros((QB, DIAG - KW), F32)], axis=1)
                g8 = a[0:SUBLANES, :]
                for blk in range(1, QB // SUBLANES):
                    g8 = g8 + pltpu.roll(a[blk * SUBLANES:(blk + 1) * SUBLANES, :], DIAG - blk * SUBLANES, 1)
                g1 = g8[0:1, :]
                for r in range(1, SUBLANES):
                    g1 = g1 + pltpu.roll(g8[r:r + 1, :], DIAG - r, 1)
                rows.append(g1)
            g = jnp.concatenate(rows, axis=0)
            dbias_ref[...] = jnp.dot(g, onehot, preferred_element_type=F32, precision=lax.Precision.HIGHEST)

    last = n - 1
    cur = lambda col_blk: pl.BlockSpec((TQ, D_ATT), lambda i: (jnp.minimum(i, last), col_blk))
    prev = lambda col_blk: pl.BlockSpec((TQ, D_ATT), lambda i: (jnp.maximum(jnp.minimum(i, last) - 1, 0), col_blk))
    return pl.pallas_call(
        body, name="attn_bwd", grid=(n + 1,),
        in_specs=[pl.BlockSpec((HEADS, DIAG), lambda i: (0, 0)),
                  cur(0), prev(1), cur(1), prev(2), cur(2), cur(0), cur(0),
                  pl.BlockSpec((TQ, HEADS), lambda i: (jnp.minimum(i, last), 0))],
        out_specs=[pl.BlockSpec((TQ, 3 * D_ATT), lambda i: (jnp.maximum(i - 1, 0), 0)),
                   pl.BlockSpec((HEADS, rel_pad), lambda i: (0, 0))],
        out_shape=[jax.ShapeDtypeStruct((s, 3 * D_ATT), BF16), jax.ShapeDtypeStruct((HEADS, rel_pad), F32)],
        scratch_shapes=[pltpu.VMEM((HEADS, QB, KW), F32), pltpu.VMEM((HEADS, QB, KW), F32),
                        pltpu.VMEM((2 * TQ, D_ATT), BF16), pltpu.VMEM((2 * TQ, D_ATT), BF16),
                        pltpu.VMEM((2 * TQ, D_ATT), F32), pltpu.VMEM((2 * TQ, D_ATT), F32),
                        pltpu.VMEM((TQ, D_ATT), BF16)],
        compiler_params=_params("arbitrary"),
    )(diag, proj, proj, proj, proj, proj, d_att, att, lse)


def _shift_down(a, k, halo):
    rolled = pltpu.roll(a, k, 0)
    row = lax.broadcasted_iota(jnp.int32, halo.shape, 0)
    first = jnp.where(row < k, pltpu.roll(halo, k, 0), rolled[0:SUBLANES, :])
    return jnp.concatenate([first, rolled[SUBLANES:, :]], axis=0)


def _shift_up(a, k, nxt):
    tm = a.shape[0]
    rolled = pltpu.roll(a, tm - k, 0)
    row = lax.broadcasted_iota(jnp.int32, nxt.shape, 0)
    last = jnp.where(row >= SUBLANES - k, pltpu.roll(nxt, SUBLANES - k, 0), rolled[tm - SUBLANES:, :])
    return jnp.concatenate([rolled[:tm - SUBLANES, :], last], axis=0)


def _sigmoid(v):
    return 1.0 / (1.0 + jnp.exp(-v))


def _mixer_mid(att, proj, x, tgt, w_att, w_conv, w_out, conv_w8, conv_b, fin_g):
    s, d = x.shape
    dc = D_ATT
    n = s // TM_MID
    tm = TM_MID
    n_shards = 4

    def body(att_ref, za_ref, gb_ref, gc_ref, u_ref, zc_ref, hgc_ref, hu_ref, gatt_ref, gconv_ref, x_ref, t_ref,
             watt_ref, wconv_ref, wout_ref, cw_ref, cb_ref, fg_ref,
             dpb_ref, do_ref, dx2_ref, gatt_o, gconv_o, gout_o, loss_o, gfn_o, gcb_o, gcw_o,
             acc_att, acc_conv, acc_out, carry):
        i = pl.program_id(0)
        tile = n - 1 - i

        @pl.when(i == 0)
        def _():
            acc_att[...] = jnp.zeros_like(acc_att)
            acc_conv[...] = jnp.zeros_like(acc_conv)
            acc_out[...] = jnp.zeros_like(acc_out)
            carry[...] = jnp.zeros_like(carry)
            loss_o[...] = jnp.zeros_like(loss_o)
            gfn_o[...] = jnp.zeros_like(gfn_o)
            gcb_o[...] = jnp.zeros_like(gcb_o)
            gcw_o[...] = jnp.zeros_like(gcw_o)

        att_v = att_ref[...].astype(F32)
        za = za_ref[...].astype(F32)
        sa = _sigmoid(za)
        silu_a = za * sa
        a_b = (att_v * silu_a).astype(BF16)

        gb = gb_ref[...].astype(F32)
        gc = gc_ref[...].astype(F32)
        u = u_ref[...].astype(F32)
        zc = zc_ref[...].astype(F32)
        cu = gc * u
        halo = jnp.where(tile > 0, hgc_ref[...].astype(F32) * hu_ref[...].astype(F32), 0.0)
        cu1 = _shift_down(cu, 1, halo)
        cu2 = _shift_down(cu, 2, halo)
        w0, w1, w2 = cw_ref[0:1, :], cw_ref[1:2, :], cw_ref[2:3, :]
        vconv = w0 * cu2 + w1 * cu1 + w2 * cu + cb_ref[...]
        sc = _sigmoid(zc)
        silu_c = zc * sc
        c_b = (gb * vconv * silu_c).astype(BF16)

        y_att = _nn(a_b, watt_ref[...])
        y_conv = _nn(c_b, wconv_ref[...])
        ga = _sigmoid(gatt_ref[...].astype(F32))
        gv = _sigmoid(gconv_ref[...].astype(F32))
        m_b = (ga * y_att + gv * y_conv).astype(BF16)
        x2 = x_ref[...] + _nn(m_b, wout_ref[...])
        r2 = lax.rsqrt(jnp.mean(x2 * x2, axis=-1, keepdims=True) + EPS)
        x2n = x2 * r2
        fg = fg_ref[...]
        err = x2n * fg - t_ref[...]
        loss_o[...] += jnp.sum(err * err, axis=0, keepdims=True) * (0.5 / d)
        dy = err * (1.0 / d)
        gfn_o[...] += jnp.sum(dy * x2n, axis=0, keepdims=True)
        dyn = dy * fg
        dx2 = r2 * (dyn - x2n * jnp.mean(dyn * x2n, axis=-1, keepdims=True))
        dx2_ref[...] = dx2
        dx2_b = dx2.astype(BF16)

        dm = _nt(dx2_b, wout_ref[...])
        acc_out[...] += _tn(m_b, dx2_b)
        dy_att = dm * ga
        dy_conv = dm * gv
        dpb_ref[:, 5 * dc:5 * dc + d] = (dy_att * y_att * (1.0 - ga)).astype(BF16)
        dpb_ref[:, 5 * dc + d:5 * dc + 2 * d] = (dy_conv * y_conv * (1.0 - gv)).astype(BF16)
        dya_b = dy_att.astype(BF16)
        dyc_b = dy_conv.astype(BF16)
        da_in = _nt(dya_b, watt_ref[...])
        acc_att[...] += _tn(a_b, dya_b)
        dc_in = _nt(dyc_b, wconv_ref[...])
        acc_conv[...] += _tn(c_b, dyc_b)

        do_ref[...] = (da_in * silu_a).astype(BF16)
        dpb_ref[:, 0:dc] = (da_in * att_v * (sa * (1.0 + za * (1.0 - sa)))).astype(BF16)
        dpb_ref[:, dc:2 * dc] = (dc_in * vconv * silu_c).astype(BF16)
        dgs = dc_in * gb
        dvc = dgs * silu_c
        dpb_ref[:, 4 * dc:5 * dc] = (dgs * vconv * (sc * (1.0 + zc * (1.0 - sc)))).astype(BF16)
        gcb_o[...] += jnp.sum(dvc, axis=0, keepdims=True)
        gcw_o[0:1, :] += jnp.sum(dvc * cu2, axis=0, keepdims=True)
        gcw_o[1:2, :] += jnp.sum(dvc * cu1, axis=0, keepdims=True)
        gcw_o[2:3, :] += jnp.sum(dvc * cu, axis=0, keepdims=True)
        nxt = carry[...]
        dcu = w2 * dvc + w1 * _shift_up(dvc, 1, nxt) + w0 * _shift_up(dvc, 2, nxt)
        carry[...] = dvc[0:SUBLANES, :]
        dpb_ref[:, 2 * dc:3 * dc] = (dcu * u).astype(BF16)
        dpb_ref[:, 3 * dc:4 * dc] = (dcu * gc).astype(BF16)

        @pl.when(i == n - 1)
        def _():
            for j in range(n_shards):
                gatt_o[j] = acc_att[:, j * (d // n_shards):(j + 1) * (d // n_shards)].astype(BF16)
                gconv_o[j] = acc_conv[:, j * (d // n_shards):(j + 1) * (d // n_shards)].astype(BF16)
                gout_o[j] = acc_out[j * (d // n_shards):(j + 1) * (d // n_shards), :].astype(BF16)

    rev = lambda width, col_blk: pl.BlockSpec((tm, width), lambda i: (n - 1 - i, col_blk))
    halo_spec = lambda col_blk: pl.BlockSpec(
        (SUBLANES, dc), lambda i: (jnp.maximum((n - 1 - i) * (tm // SUBLANES) - 1, 0), col_blk))
    const = lambda shape: pl.BlockSpec(shape, lambda i: tuple(0 for _ in shape), pipeline_mode=pl.Buffered(1))
    q4 = d // n_shards
    return pl.pallas_call(
        body, name="mixer_mid", grid=(n,),
        in_specs=[rev(dc, 0), rev(dc, 3), rev(dc, 4), rev(dc, 5), rev(dc, 6), rev(dc, 7),
                  halo_spec(5), halo_spec(6), rev(d, 4), rev(d, 5), rev(d, 0), rev(d, 0),
                  const((dc, d)), const((dc, d)), const((d, d)), const((SUBLANES, dc)), const((1, dc)), const((1, d))],
        out_specs=[rev(5 * dc + 2 * d, 0), rev(dc, 0), rev(d, 0),
                   const((n_shards, dc, q4)), const((n_shards, dc, q4)), const((n_shards, q4, d)),
                   const((1, d)), const((1, d)), const((1, dc)), const((SUBLANES, dc))],
        out_shape=[jax.ShapeDtypeStruct((s, 5 * dc + 2 * d), BF16), jax.ShapeDtypeStruct((s, dc), BF16),
                   jax.ShapeDtypeStruct((s, d), F32),
                   jax.ShapeDtypeStruct((n_shards, dc, q4), BF16), jax.ShapeDtypeStruct((n_shards, dc, q4), BF16),
                   jax.ShapeDtypeStruct((n_shards, q4, d), BF16),
                   jax.ShapeDtypeStruct((1, d), F32), jax.ShapeDtypeStruct((1, d), F32),
                   jax.ShapeDtypeStruct((1, dc), F32), jax.ShapeDtypeStruct((SUBLANES, dc), F32)],
        scratch_shapes=[pltpu.VMEM((dc, d), F32), pltpu.VMEM((dc, d), F32), pltpu.VMEM((d, d), F32),
                        pltpu.VMEM((SUBLANES, dc), F32)],
        compiler_params=_params("arbitrary"),
    )(att, proj, proj, proj, proj, proj, proj, proj, proj, proj, x, tgt,
      w_att, w_conv, w_out, conv_w8, conv_b, fin_g)


def _in_proj_bwd_x(dqkv, dpb, w_in, x, dx2, g):
    s, d = x.shape
    tn = dqkv.shape[1]
    nb = dpb.shape[1] // tn
    n = s // TM_MM

    def body(*refs):
        dps, ws = refs[:nb + 1], refs[nb + 1:2 * nb + 2]
        x_ref, dx2_ref, g_ref, gx_ref, gng_ref = refs[2 * nb + 2:]
        i = pl.program_id(0)

        @pl.when(i == 0)
        def _():
            gng_ref[...] = jnp.zeros_like(gng_ref)

        dh = _nt(dps[0][...], ws[0][...])
        for j in range(1, nb + 1):
            dh = dh + _nt(dps[j][...], ws[j][...])
        xv = x_ref[...]
        r = lax.rsqrt(jnp.mean(xv * xv, axis=-1, keepdims=True) + EPS)
        xn = xv * r
        gng_ref[...] += jnp.sum(dh * xn, axis=0, keepdims=True)
        dhn = dh * g_ref[...]
        gx_ref[...] = dx2_ref[...] + r * (dhn - xn * jnp.mean(dhn * xn, axis=-1, keepdims=True))

    tile = lambda width, col_blk: pl.BlockSpec((TM_MM, width), lambda i: (i, col_blk))
    wspec = lambda col_blk: pl.BlockSpec((d, tn), lambda i: (0, col_blk), pipeline_mode=pl.Buffered(1))
    return pl.pallas_call(
        body, name="in_proj_bwd_x", grid=(n,),
        in_specs=[tile(tn, 0)] + [tile(tn, j) for j in range(nb)] + [wspec(j) for j in range(nb + 1)]
        + [tile(d, 0), tile(d, 0), pl.BlockSpec((1, d), lambda i: (0, 0))],
        out_specs=[tile(d, 0), pl.BlockSpec((1, d), lambda i: (0, 0))],
        out_shape=[jax.ShapeDtypeStruct((s, d), F32), jax.ShapeDtypeStruct((1, d), F32)],
        compiler_params=_params("arbitrary"),
    )(dqkv, *([dpb] * nb), *([w_in] * (nb + 1)), x, dx2, g)


def _in_proj_bwd_w(h, dqkv, dpb, tn):
    s, d = h.shape
    n = s // TM_MM
    nb = 1 + dpb.shape[1] // tn

    def body(h_ref, da_ref, db_ref, o_ref, acc):
        j, i = pl.program_id(0), pl.program_id(1)

        @pl.when(i == 0)
        def _():
            acc[...] = jnp.zeros_like(acc)

        @pl.when(j == 0)
        def _():
            acc[...] += _tn(h_ref[...], da_ref[...])

        @pl.when(j > 0)
        def _():
            acc[...] += _tn(h_ref[...], db_ref[...])

        @pl.when(i == n - 1)
        def _():
            o_ref[0] = acc[...].astype(BF16)

    return pl.pallas_call(
        body, name="in_proj_bwd_w", grid=(nb, n),
        in_specs=[pl.BlockSpec((TM_MM, d), lambda j, i: (i, 0)),
                  pl.BlockSpec((TM_MM, tn), lambda j, i: (jnp.where(j == 0, i, 0), 0)),
                  pl.BlockSpec((TM_MM, tn), lambda j, i: (jnp.where(j == 0, 0, i), jnp.maximum(j - 1, 0)))],
        out_specs=pl.BlockSpec((1, d, tn), lambda j, i: (j, 0, 0)),
        out_shape=jax.ShapeDtypeStruct((nb, d, tn), BF16),
        scratch_shapes=[pltpu.VMEM((d, tn), F32)],
        compiler_params=_params("arbitrary", "arbitrary"),
    )(h, dqkv, dpb)


def _sum_slots(recv, name, rows_per_step):
    k, r, c = recv.shape

    def body(r_ref, o_ref):
        total = r_ref[0].astype(F32)
        for slot in range(1, k):
            total = total + r_ref[slot].astype(F32)
        o_ref[...] = total

    return pl.pallas_call(
        body, name=name, grid=(r // rows_per_step,),
        in_specs=[pl.BlockSpec((k, rows_per_step, c), lambda i: (0, i, 0))],
        out_specs=pl.BlockSpec((rows_per_step, c), lambda i: (i, 0)),
        out_shape=jax.ShapeDtypeStruct((r, c), F32),
        compiler_params=_params("parallel"),
    )(recv)


LOSS_ROW = 6


def _sum_small(recv):
    k = recv.shape[0]

    def body(r_ref, o_ref):
        total = r_ref[0]
        for slot in range(1, k):
            total = total + r_ref[slot]
        o_ref[...] = total
        loss = jnp.sum(total[LOSS_ROW:LOSS_ROW + 1, :], axis=1, keepdims=True)
        o_ref[LOSS_ROW:LOSS_ROW + 1, :] = jnp.broadcast_to(loss, (1, SMALL_COLS))

    return pl.pallas_call(
        body, name="reduce_sum_small",
        out_shape=jax.ShapeDtypeStruct((SMALL_ROWS, SMALL_COLS), F32),
    )(recv)


def _adamw(w, g, m, v, name, rows_per_step):
    r, c = w.shape
    c1 = 1.0 / (1.0 - ADAM_B1 ** ADAM_STEP)
    c2 = 1.0 / (1.0 - ADAM_B2 ** ADAM_STEP)

    def body(w_ref, g_ref, m_ref, v_ref, d_ref, mo_ref, vo_ref):
        gv = g_ref[...]
        m2 = ADAM_B1 * m_ref[...] + (1.0 - ADAM_B1) * gv
        v2 = ADAM_B2 * v_ref[...] + (1.0 - ADAM_B2) * (gv * gv)
        mo_ref[...] = m2
        vo_ref[...] = v2
        d_ref[...] = -ADAM_LR * ((m2 * c1) / (jnp.sqrt(v2 * c2) + ADAM_EPS) + ADAM_WD * w_ref[...])

    spec = pl.BlockSpec((rows_per_step, c), lambda i: (i, 0))
    shape = jax.ShapeDtypeStruct((r, c), F32)
    return pl.pallas_call(
        body, name=name, grid=(r // rows_per_step,),
        in_specs=[spec] * 4, out_specs=[spec] * 3, out_shape=[shape] * 3,
        compiler_params=_params("parallel"),
    )(w, g, m, v)


ANY = pl.BlockSpec(memory_space=pl.ANY)


def _position():
    return lax.axis_index("x"), lax.axis_index("y"), lax.axis_index("c")


def _gather_weights(shards, kinds, cw8):
    n_chips = 4
    nw = len(shards)

    def full_shape(a, kind):
        r, c = a.shape
        return (r, c * n_chips) if kind == "cols" else (r * n_chips, c)

    def body(*refs):
        srcs, cw = refs[:nw], refs[nw]
        dsts, cw_all = refs[nw + 1:2 * nw + 1], refs[2 * nw + 1]
        send1, recv1, send2, recv2, ssend, srecv, lsem = refs[2 * nw + 2:]
        x, y, c = _position()
        mine = 2 * x + y
        chips = [(x, 1 - y), (1 - x, y), (1 - x, 1 - y)]

        def window(w, shard, half):
            r, cc = shards[w].shape
            hr = r // 2
            rows = pl.ds(0, r) if half is None else pl.ds(half * hr, hr)
            if kinds[w] == "cols":
                return dsts[w].at[rows, pl.ds(shard * cc, cc)]
            rows = pl.ds(shard * r, r) if half is None else pl.ds(shard * r + half * hr, hr)
            return dsts[w].at[rows, :]

        def my_half(w):
            hr = shards[w].shape[0] // 2
            return srcs[w].at[pl.ds(c * hr, hr), :]

        local = [pltpu.make_async_copy(srcs[w], window(w, mine, None), lsem.at[w]) for w in range(nw)]
        local.append(pltpu.make_async_copy(cw, cw_all.at[mine], lsem.at[nw]))
        for cp in local:
            cp.start()

        def ici(k, w, shard, src):
            kx, ky = chips[k]
            return pltpu.make_async_remote_copy(
                src_ref=src, dst_ref=window(w, shard, c), send_sem=send1.at[k, w], recv_sem=recv1.at[k, w],
                device_id=(kx, ky, c), device_id_type=MESH)

        def d2d(k, w, shard, half):
            return pltpu.make_async_remote_copy(
                src_ref=window(w, shard, half), dst_ref=window(w, shard, half),
                send_sem=send2.at[k, w], recv_sem=recv2.at[k, w],
                device_id=(x, y, 1 - c), device_id_type=MESH)

        def small(k):
            kx, ky = chips[k]
            return pltpu.make_async_remote_copy(
                src_ref=cw, dst_ref=cw_all.at[mine], send_sem=ssend.at[k], recv_sem=srecv.at[k],
                device_id=(kx, ky, c), device_id_type=MESH)

        sent = [ici(k, w, mine, my_half(w)) for k in range(3) for w in range(nw)]
        sent += [small(k) for k in range(3)]
        for cp in sent:
            cp.start()
        passed = []
        for k in range(3):
            kx, ky = chips[k]
            theirs = 2 * kx + ky
            for w in range(nw):
                ici(k, w, theirs, my_half(w)).wait_recv()
                fwd = d2d(k, w, theirs, c)
                fwd.start()
                passed.append(fwd)
        for k in range(3):
            kx, ky = chips[k]
            theirs = 2 * kx + ky
            for w in range(nw):
                d2d(k, w, theirs, 1 - c).wait_recv()
            pltpu.make_async_remote_copy(
                src_ref=cw, dst_ref=cw_all.at[theirs], send_sem=ssend.at[k], recv_sem=srecv.at[k],
                device_id=(kx, ky, c), device_id_type=MESH).wait_recv()
        for cp in sent + passed:
            cp.wait_send()
        for cp in local:
            cp.wait()

    out_shape = [jax.ShapeDtypeStruct(full_shape(a, k), a.dtype) for a, k in zip(shards, kinds)]
    out_shape.append(jax.ShapeDtypeStruct((n_chips,) + cw8.shape, cw8.dtype))
    return pl.pallas_call(
        body, name="gather_weights",
        in_specs=[ANY] * (nw + 1), out_specs=[ANY] * (nw + 1), out_shape=out_shape,
        scratch_shapes=[pltpu.SemaphoreType.DMA((3, nw)), pltpu.SemaphoreType.DMA((3, nw)),
                        pltpu.SemaphoreType.DMA((3, nw)), pltpu.SemaphoreType.DMA((3, nw)),
                        pltpu.SemaphoreType.DMA((3,)), pltpu.SemaphoreType.DMA((3,)),
                        pltpu.SemaphoreType.DMA((nw + 1,))],
    )(*shards, cw8)


def _reduce_send(parts, small):
    nw = len(parts)
    n_dev = 8

    def body(*refs):
        srcs, sm = refs[:nw], refs[nw]
        dsts, sm_all = refs[nw + 1:2 * nw + 1], refs[2 * nw + 1]
        send, recv, ssend, srecv, lsem = refs[2 * nw + 2:]
        x, y, c = _position()
        me = 4 * x + 2 * y + c

        def peer(k):
            return ((1 - x) if k & 4 else x, (1 - y) if k & 2 else y, (1 - c) if k & 1 else c)

        def piece(w, k):
            px, py, pc = peer(k)
            hr = parts[w].shape[1] // 2
            return srcs[w].at[2 * px + py, pl.ds(pc * hr, hr), :]

        local = [pltpu.make_async_copy(piece(w, 0), dsts[w].at[0], lsem.at[w]) for w in range(nw)]
        local.append(pltpu.make_async_copy(sm, sm_all.at[me], lsem.at[nw]))
        for cp in local:
            cp.start()
        sent = []
        for k in range(1, n_dev):
            for w in range(nw):
                sent.append(pltpu.make_async_remote_copy(
                    src_ref=piece(w, k), dst_ref=dsts[w].at[k], send_sem=send.at[k, w], recv_sem=recv.at[k, w],
                    device_id=peer(k), device_id_type=MESH))
            sent.append(pltpu.make_async_remote_copy(
                src_ref=sm, dst_ref=sm_all.at[me], send_sem=ssend.at[k], recv_sem=srecv.at[k],
                device_id=peer(k), device_id_type=MESH))
        for cp in sent:
            cp.start()
        for k in range(1, n_dev):
            px, py, pc = peer(k)
            for w in range(nw):
                pltpu.make_async_remote_copy(
                    src_ref=piece(w, k), dst_ref=dsts[w].at[k], send_sem=send.at[k, w], recv_sem=recv.at[k, w],
                    device_id=peer(k), device_id_type=MESH).wait_recv()
            pltpu.make_async_remote_copy(
                src_ref=sm, dst_ref=sm_all.at[4 * px + 2 * py + pc], send_sem=ssend.at[k], recv_sem=srecv.at[k],
                device_id=peer(k), device_id_type=MESH).wait_recv()
        for cp in sent:
            cp.wait_send()
        for cp in local:
            cp.wait()

    out_shape = [jax.ShapeDtypeStruct((n_dev, p.shape[1] // 2, p.shape[2]), p.dtype) for p in parts]
    out_shape.append(jax.ShapeDtypeStruct((n_dev,) + small.shape, small.dtype))
    return pl.pallas_call(
        body, name="reduce_send",
        in_specs=[ANY] * (nw + 1), out_specs=[ANY] * (nw + 1), out_shape=out_shape,
        scratch_shapes=[pltpu.SemaphoreType.DMA((n_dev, nw)), pltpu.SemaphoreType.DMA((n_dev, nw)),
                        pltpu.SemaphoreType.DMA((n_dev,)), pltpu.SemaphoreType.DMA((n_dev,)),
                        pltpu.SemaphoreType.DMA((nw + 1,))],
    )(*parts, small)


def _reduce_pair(halves):
    nw = len(halves)

    def body(*refs):
        srcs, dsts = refs[:nw], refs[nw:2 * nw]
        send, recv, lsem = refs[2 * nw:]
        x, y, c = _position()

        def rows(w, half):
            hr = halves[w].shape[0]
            return dsts[w].at[pl.ds(half * hr, hr), :]

        local = [pltpu.make_async_copy(srcs[w], rows(w, c), lsem.at[w]) for w in range(nw)]
        sent = [pltpu.make_async_remote_copy(
            src_ref=srcs[w], dst_ref=rows(w, c), send_sem=send.at[w], recv_sem=recv.at[w],
            device_id=(x, y, 1 - c), device_id_type=MESH) for w in range(nw)]
        for cp in local + sent:
            cp.start()
        for w in range(nw):
            pltpu.make_async_remote_copy(
                src_ref=srcs[w], dst_ref=rows(w, 1 - c), send_sem=send.at[w], recv_sem=recv.at[w],
                device_id=(x, y, 1 - c), device_id_type=MESH).wait_recv()
        for cp in sent:
            cp.wait_send()
        for cp in local:
            cp.wait()

    return pl.pallas_call(
        body, name="reduce_pair",
        in_specs=[ANY] * nw, out_specs=[ANY] * nw,
        out_shape=[jax.ShapeDtypeStruct((2 * hv.shape[0], hv.shape[1]), hv.dtype) for hv in halves],
        scratch_shapes=[pltpu.SemaphoreType.DMA((nw,)), pltpu.SemaphoreType.DMA((nw,)),
                        pltpu.SemaphoreType.DMA((nw,))],
    )(*halves)


def _pad_to(a, rows, cols):
    return jnp.pad(a, ((0, rows - a.shape[0]), (0, cols - a.shape[1])))


def _pack_small(norm_g, fin_g, conv_b, conv_w, loss_vec, rel):
    rows = [_pad_to(norm_g, 1, SMALL_COLS), _pad_to(fin_g, 1, SMALL_COLS), _pad_to(conv_b, 1, SMALL_COLS),
            _pad_to(conv_w, 3, SMALL_COLS), _pad_to(loss_vec, 2, SMALL_COLS), _pad_to(rel, HEADS, SMALL_COLS)]
    return jnp.concatenate(rows, axis=0)


def kernel(x, norm_g, w_in, rel_bias, w_att_out, conv_w, conv_b, w_conv_out, w_out, final_norm_g, loss_target, m_norm_g, m_w_in, m_rel_bias, m_w_att_out, m_conv_w, m_conv_b, m_w_conv_out, m_w_out, m_final_norm_g, v_norm_g, v_w_in, v_rel_bias, v_w_att_out, v_conv_w, v_conv_b, v_w_conv_out, v_w_out, v_final_norm_g):
    xs, tgt = x[0], loss_target[0]
    d = xs.shape[1]
    cshard = conv_w.shape[2]
    chip = 2 * lax.axis_index("x") + lax.axis_index("y")

    shards = [w_in[0].astype(BF16), w_att_out[0].astype(BF16), w_conv_out[0].astype(BF16), w_out[0].astype(BF16)]
    cw8 = _pad_to(conv_w[0], SUBLANES, cshard)
    wb_in, wb_att, wb_conv, wb_out, cw_all = _gather_weights(shards, ["cols", "cols", "cols", "rows"], cw8)
    conv_w_full = jnp.transpose(cw_all, (1, 0, 2)).reshape(SUBLANES, 4 * cshard)
    tn = shards[0].shape[1]

    h = _rmsnorm_fwd(xs, norm_g)
    proj = _in_proj(h, wb_in, tn)
    diag = jnp.take(rel_bias[0], _diag_rel_index(), axis=1)
    att, lse = _attn_fwd(diag, proj)
    (dpb, d_att, dx2, g_att_p, g_conv_p, g_out_p, loss_vec, g_fin, g_cb, g_cw) = _mixer_mid(
        att, proj, xs, tgt, wb_att, wb_conv, wb_out, conv_w_full, conv_b, final_norm_g[None, :])
    dqkv, g_rel = _attn_bwd(diag, proj, d_att, att, lse)
    grad_x, g_norm = _in_proj_bwd_x(dqkv, dpb, wb_in, xs, dx2, norm_g)
    g_in_p = _in_proj_bwd_w(h, dqkv, dpb, tn)

    small = _pack_small(g_norm, g_fin, g_cb, g_cw[0:3], loss_vec, g_rel)
    r_in, r_att, r_conv, r_out, r_small = _reduce_send([g_in_p, g_att_p, g_conv_p, g_out_p], small)
    halves = [_sum_slots(r_in, "reduce_sum_in", 64), _sum_slots(r_att, "reduce_sum_att", r_att.shape[1]),
              _sum_slots(r_conv, "reduce_sum_conv", r_conv.shape[1]), _sum_slots(r_out, "reduce_sum_out", r_out.shape[1])]
    gw_in, gw_att, gw_conv, gw_out = _reduce_pair(halves)
    gs = _sum_small(r_small)

    g_cw_mine = lax.dynamic_slice(gs, (3, chip * cshard), (3, cshard))
    zero2 = jnp.zeros((2, 1), F32)
    pack = lambda ng, fg, cb, cw, rb: _pack_small(ng, fg[None, :], cb, cw[0], zero2, rb[0])
    gs_mine = _pack_small(gs[0:1], gs[1:2], gs[2:3, :conv_b.shape[1]], g_cw_mine, zero2, gs[8:16, :N_REL])
    ds, ms, vs = _adamw(pack(norm_g, final_norm_g, conv_b, conv_w, rel_bias), gs_mine,
                        pack(m_norm_g, m_final_norm_g, m_conv_b, m_conv_w, m_rel_bias),
                        pack(v_norm_g, v_final_norm_g, v_conv_b, v_conv_w, v_rel_bias), "adamw_small", SMALL_ROWS)

    def unpack(a):
        return {"norm_g": a[0:1], "final_norm_g": a[1], "conv_b": a[2:3, :conv_b.shape[1]],
                "conv_w": a[3:6, :cshard][None], "rel_bias": a[8:16, :N_REL][None]}

    g_small, d_small, m_small, v_small = unpack(gs_mine), unpack(ds), unpack(ms), unpack(vs)

    big = {}
    for name, w, g, m, v, rows in (("w_in", w_in, gw_in, m_w_in, v_w_in, 128),
                                   ("w_att_out", w_att_out, gw_att, m_w_att_out, v_w_att_out, 256),
                                   ("w_conv_out", w_conv_out, gw_conv, m_w_conv_out, v_w_conv_out, 256),
                                   ("w_out", w_out, gw_out, m_w_out, v_w_out, 128)):
        dw, mw, vw = _adamw(w[0], g, m[0], v[0], "adamw_" + name, rows)
        big[name] = (g[None], dw[None], mw[None], vw[None])

    order = ["norm_g", "w_in", "rel_bias", "w_att_out", "conv_w", "conv_b", "w_conv_out", "w_out", "final_norm_g"]
    outs = [gs[LOSS_ROW, 0], grad_x[None]]
    for which, small_d in enumerate((g_small, d_small, m_small, v_small)):
        for name in order:
            outs.append(big[name][which] if name in big else small_d[name])
    return tuple(outs)
```

```python
import functools

import numpy as np
import jax
import jax.numpy as jnp
from jax import lax
from jax.experimental import pallas as pl
from jax.experimental.pallas import tpu as pltpu

F32 = jnp.float32
BF16 = jnp.bfloat16
MESH = pl.DeviceIdType.MESH

CHUNK = 64
N_LEFT = 8
HEADS = 8
HEAD_DIM = 64
D_ATT = HEADS * HEAD_DIM
MAX_REL = 128
N_REL = 2 * MAX_REL + 1
EPS = 1e-6
NEG_BIG = -1e30
ADAM_LR, ADAM_B1, ADAM_B2, ADAM_EPS, ADAM_WD, ADAM_STEP = 0.001, 0.9, 0.999, 1e-08, 0.01, 10

LANES = 128
SUBLANES = 8
VMEM_LIMIT = 56 * 1024 * 1024

QB = 2 * CHUNK
KW = N_LEFT * CHUNK + QB
DIAG = KW + QB
TQ = N_LEFT * CHUNK
TM_MID = 256
TM_MM = 512
SMALL_ROWS, SMALL_COLS = 16, 1024


def _params(*sem):
    return pltpu.CompilerParams(dimension_semantics=sem, vmem_limit_bytes=VMEM_LIMIT)


def _nt(a, b):
    return lax.dot_general(a, b, (((1,), (1,)), ((), ())), preferred_element_type=F32)


def _tn(a, b):
    return lax.dot_general(a, b, (((0,), (0,)), ((), ())), preferred_element_type=F32)


def _nn(a, b):
    return jnp.dot(a, b, preferred_element_type=F32)


def _diag_rel_index():
    d = np.arange(DIAG)
    diff = np.where(d < KW, d, d - DIAG)
    rel = N_LEFT * CHUNK - diff
    return np.clip(rel, -MAX_REL, MAX_REL) + MAX_REL


def _build_bias(diag_ref, bias_scr):
    r = lax.broadcasted_iota(jnp.int32, (QB, KW), 0) // CHUNK
    s = lax.broadcasted_iota(jnp.int32, (QB, KW), 1) // CHUNK
    allowed = (s >= r) & (s <= r + N_LEFT)
    for h in range(HEADS):
        row = jnp.broadcast_to(diag_ref[h:h + 1, :], (QB, DIAG))
        t = pltpu.roll(row, 0, 1, stride=1, stride_axis=0)
        bias_scr[h] = jnp.where(allowed, t[:, :KW], NEG_BIG)


def _rmsnorm_fwd(x, g):
    s, d = x.shape

    def body(x_ref, g_ref, h_ref):
        xv = x_ref[...]
        r = lax.rsqrt(jnp.mean(xv * xv, axis=-1, keepdims=True) + EPS)
        h_ref[...] = ((xv * r) * g_ref[...]).astype(BF16)

    return pl.pallas_call(
        body, name="rmsnorm_fwd", grid=(s // TM_MM,),
        in_specs=[pl.BlockSpec((TM_MM, d), lambda i: (i, 0)), pl.BlockSpec((1, d), lambda i: (0, 0))],
        out_specs=pl.BlockSpec((TM_MM, d), lambda i: (i, 0)),
        out_shape=jax.ShapeDtypeStruct((s, d), BF16),
        compiler_params=_params("parallel"),
    )(x, g)


def _in_proj(h, w, tn):
    s, d = h.shape
    e = w.shape[1]

    def body(h_ref, w_ref, o_ref):
        o_ref[...] = _nn(h_ref[...], w_ref[...]).astype(BF16)

    return pl.pallas_call(
        body, name="in_proj", grid=(e // tn, s // TM_MM),
        in_specs=[pl.BlockSpec((TM_MM, d), lambda j, i: (i, 0)), pl.BlockSpec((d, tn), lambda j, i: (0, j))],
        out_specs=pl.BlockSpec((TM_MM, tn), lambda j, i: (i, j)),
        out_shape=jax.ShapeDtypeStruct((s, e), BF16),
        compiler_params=_params("parallel", "parallel"),
    )(h, w)


def _attn_fwd(diag, proj):
    s = proj.shape[0]
    n = s // TQ
    scale = HEAD_DIM ** -0.5

    def body(diag_ref, q_ref, kp_ref, kc_ref, vp_ref, vc_ref, o_ref, lse_ref, bias_scr, kcat, vcat):
        i = pl.program_id(0)

        @pl.when(i == 0)
        def _():
            _build_bias(diag_ref, bias_scr)

        kcat[0:TQ, :] = kp_ref[...]
        kcat[TQ:2 * TQ, :] = kc_ref[...]
        vcat[0:TQ, :] = vp_ref[...]
        vcat[TQ:2 * TQ, :] = vc_ref[...]
        lane_hi = lax.broadcasted_iota(jnp.int32, (QB, LANES), 1) >= HEAD_DIM
        col = lax.broadcasted_iota(jnp.int32, (QB, KW), 1)

        def block(b, carry):
            r0 = pl.multiple_of(b * QB, QB)
            valid = col >= jnp.where(i == 0, TQ - r0, 0)
            for p in range(HEADS // 2):
                lanes = slice(LANES * p, LANES * (p + 1))
                qp = q_ref[pl.ds(r0, QB), lanes] * scale
                kw = kcat[pl.ds(r0, KW), lanes]
                vw = vcat[pl.ds(r0, KW), lanes]
                outs = []
                for e in range(2):
                    hd = 2 * p + e
                    mask = lane_hi if e else jnp.logical_not(lane_hi)
                    qm = jnp.where(mask, qp, jnp.zeros_like(qp))
                    sc = jnp.where(valid, _nt(qm, kw) + bias_scr[hd], NEG_BIG)
                    m = jnp.max(sc, axis=1, keepdims=True)
                    pe = jnp.exp(sc - m)
                    l = jnp.sum(pe, axis=1, keepdims=True)
                    outs.append(_nn(pe.astype(BF16), vw) / l)
                    lse_ref[pl.ds(r0, QB), hd:hd + 1] = m + jnp.log(l)
                o_ref[pl.ds(r0, QB), lanes] = jnp.where(lane_hi, outs[1], outs[0]).astype(BF16)
            return carry

        lax.fori_loop(0, TQ // QB, block, 0)

    blk = lambda col_blk, prev: pl.BlockSpec(
        (TQ, D_ATT), (lambda i: (jnp.maximum(i - 1, 0), col_blk)) if prev else (lambda i: (i, col_blk)))
    return pl.pallas_call(
        body, name="attn_fwd", grid=(n,),
        in_specs=[pl.BlockSpec((HEADS, DIAG), lambda i: (0, 0)),
                  blk(0, False), blk(1, True), blk(1, False), blk(2, True), blk(2, False)],
        out_specs=[pl.BlockSpec((TQ, D_ATT), lambda i: (i, 0)), pl.BlockSpec((TQ, HEADS), lambda i: (i, 0))],
        out_shape=[jax.ShapeDtypeStruct((s, D_ATT), BF16), jax.ShapeDtypeStruct((s, HEADS), F32)],
        scratch_shapes=[pltpu.VMEM((HEADS, QB, KW), F32), pltpu.VMEM((2 * TQ, D_ATT), BF16),
                        pltpu.VMEM((2 * TQ, D_ATT), BF16)],
        compiler_params=_params("arbitrary"),
    )(diag, proj, proj, proj, proj, proj)


def _attn_bwd(diag, proj, d_att, att, lse):
    s = proj.shape[0]
    n = s // TQ
    scale = HEAD_DIM ** -0.5
    rel_pad = 3 * LANES

    def body(diag_ref, q_ref, kp_ref, kc_ref, vp_ref, vc_ref, do_ref, o_ref, lse_ref, dqkv_ref, dbias_ref,
             bias_scr, dbias_acc, kcat, vcat, dk_acc, dv_acc, dq_scr):
        i = pl.program_id(0)

        @pl.when(i == 0)
        def _():
            _build_bias(diag_ref, bias_scr)
            dbias_acc[...] = jnp.zeros_like(dbias_acc)
            dk_acc[...] = jnp.zeros_like(dk_acc)
            dv_acc[...] = jnp.zeros_like(dv_acc)

        @pl.when(i > 0)
        def _():
            dqkv_ref[:, 0:D_ATT] = dq_scr[...]
            dk_acc[0:TQ, :] = dk_acc[TQ:2 * TQ, :]
            dk_acc[TQ:2 * TQ, :] = jnp.zeros((TQ, D_ATT), F32)
            dv_acc[0:TQ, :] = dv_acc[TQ:2 * TQ, :]
            dv_acc[TQ:2 * TQ, :] = jnp.zeros((TQ, D_ATT), F32)

        @pl.when(i < n)
        def _():
            kcat[0:TQ, :] = kp_ref[...]
            kcat[TQ:2 * TQ, :] = kc_ref[...]
            vcat[0:TQ, :] = vp_ref[...]
            vcat[TQ:2 * TQ, :] = vc_ref[...]
            lane_hi = lax.broadcasted_iota(jnp.int32, (QB, LANES), 1) >= HEAD_DIM
            col = lax.broadcasted_iota(jnp.int32, (QB, KW), 1)

            def block(b, carry):
                r0 = pl.multiple_of(b * QB, QB)
                valid = col >= jnp.where(i == 0, TQ - r0, 0)
                for p in range(HEADS // 2):
                    lanes = slice(LANES * p, LANES * (p + 1))
                    qp = q_ref[pl.ds(r0, QB), lanes] * scale
                    kw = kcat[pl.ds(r0, KW), lanes]
                    vw = vcat[pl.ds(r0, KW), lanes]
                    dop = do_ref[pl.ds(r0, QB), lanes]
                    prod = dop.astype(F32) * o_ref[pl.ds(r0, QB), lanes].astype(F32)
                    dqs, dkp, dvp = [], None, None
                    for e in range(2):
                        hd = 2 * p + e
                        mask = lane_hi if e else jnp.logical_not(lane_hi)
                        qm = jnp.where(mask, qp, jnp.zeros_like(qp))
                        dom = jnp.where(mask, dop, jnp.zeros_like(dop))
                        delta = jnp.sum(jnp.where(mask, prod, 0.0), axis=1, keepdims=True)
                        sc = jnp.where(valid, _nt(qm, kw) + bias_scr[hd], NEG_BIG)
                        pr = jnp.exp(sc - lse_ref[pl.ds(r0, QB), hd:hd + 1])
                        ds = pr * (_nt(dom, vw) - delta)
                        dbias_acc[hd] += ds
                        dsb = ds.astype(BF16)
                        dv_e = _tn(pr.astype(BF16), dom)
                        dk_e = _tn(dsb, qm)
                        dvp = dv_e if dvp is None else dvp + dv_e
                        dkp = dk_e if dkp is None else dkp + dk_e
                        dqs.append(_nn(dsb, kw))
                    dv_acc[pl.ds(r0, KW), lanes] += dvp
                    dk_acc[pl.ds(r0, KW), lanes] += dkp
                    dq = jnp.where(lane_hi, dqs[1], dqs[0]) * (HEAD_DIM ** -0.5)
                    dq_scr[pl.ds(r0, QB), lanes] = dq.astype(BF16)
                return carry

            lax.fori_loop(0, TQ // QB, block, 0)

        @pl.when(i > 0)
        def _():
            dqkv_ref[:, D_ATT:2 * D_ATT] = dk_acc[0:TQ, :].astype(BF16)
            dqkv_ref[:, 2 * D_ATT:3 * D_ATT] = dv_acc[0:TQ, :].astype(BF16)

        @pl.when(i == n)
        def _():
            d_iota = lax.broadcasted_iota(jnp.int32, (DIAG, rel_pad), 0)
            n_iota = lax.broadcasted_iota(jnp.int32, (DIAG, rel_pad), 1)
            diff = jnp.where(d_iota < KW, d_iota, d_iota - DIAG)
            idx = jnp.clip(N_LEFT * CHUNK - diff, -MAX_REL, MAX_REL) + MAX_REL
            onehot = (idx == n_iota).astype(F32)
            rows = []
            for hd in range(HEADS):
                a = jnp.concatenate([dbias_acc[hd], jnp.zeros((QB, DIAG - KW), F32)], axis=1)
                g8 = a[0:SUBLANES, :]
                for blk in range(1, QB // SUBLANES):
                    g8 = g8 + pltpu.roll(a[blk * SUBLANES:(blk + 1) * SUBLANES, :], DIAG - blk * SUBLANES, 1)
                g1 = g8[0:1, :]
                for r in range(1, SUBLANES):
                    g1 = g1 + pltpu.roll(g8[r:r + 1, :], DIAG - r, 1)
                rows.append(g1)
            g = jnp.concatenate(rows, axis=0)
            dbias_ref[...] = jnp.dot(g, onehot, preferred_element_type=F32, precision=lax.Precision.HIGHEST)

    last = n - 1
    cur = lambda col_blk: pl.BlockSpec((TQ, D_ATT), lambda i: (jnp.minimum(i, last), col_blk))
    prev = lambda col_blk: pl.BlockSpec((TQ, D_ATT), lambda i: (jnp.maximum(jnp.minimum(i, last) - 1, 0), col_blk))
    return pl.pallas_call(
        body, name="attn_bwd", grid=(n + 1,),
        in_specs=[pl.BlockSpec((HEADS, DIAG), lambda i: (0, 0)),
                  cur(0), prev(1), cur(1), prev(2), cur(2), cur(0), cur(0),
                  pl.BlockSpec((TQ, HEADS), lambda i: (jnp.minimum(i, last), 0))],
        out_specs=[pl.BlockSpec((TQ, 3 * D_ATT), lambda i: (jnp.maximum(i - 1, 0), 0)),
                   pl.BlockSpec((HEADS, rel_pad), lambda i: (0, 0))],
        out_shape=[jax.ShapeDtypeStruct((s, 3 * D_ATT), BF16), jax.ShapeDtypeStruct((HEADS, rel_pad), F32)],
        scratch_shapes=[pltpu.VMEM((HEADS, QB, KW), F32), pltpu.VMEM((HEADS, QB, KW), F32),
                        pltpu.VMEM((2 * TQ, D_ATT), BF16), pltpu.VMEM((2 * TQ, D_ATT), BF16),
                        pltpu.VMEM((2 * TQ, D_ATT), F32), pltpu.VMEM((2 * TQ, D_ATT), F32),
                        pltpu.VMEM((TQ, D_ATT), BF16)],
        compiler_params=_params("arbitrary"),
    )(diag, proj, proj, proj, proj, proj, d_att, att, lse)


def _shift_down(a, k, halo):
    rolled = pltpu.roll(a, k, 0)
    row = lax.broadcasted_iota(jnp.int32, halo.shape, 0)
    first = jnp.where(row < k, pltpu.roll(halo, k, 0), rolled[0:SUBLANES, :])
    return jnp.concatenate([first, rolled[SUBLANES:, :]], axis=0)


def _shift_up(a, k, nxt):
    tm = a.shape[0]
    rolled = pltpu.roll(a, tm - k, 0)
    row = lax.broadcasted_iota(jnp.int32, nxt.shape, 0)
    last = jnp.where(row >= SUBLANES - k, pltpu.roll(nxt, SUBLANES - k, 0), rolled[tm - SUBLANES:, :])
    return jnp.concatenate([rolled[:tm - SUBLANES, :], last], axis=0)


def _sigmoid(v):
    return 1.0 / (1.0 + jnp.exp(-v))


def _mixer_mid(att, proj, x, tgt, w_att, w_conv, w_out, conv_w8, conv_b, fin_g):
    s, d = x.shape
    dc = D_ATT
    n = s // TM_MID
    tm = TM_MID
    n_shards = 4

    def body(att_ref, za_ref, gb_ref, gc_ref, u_ref, zc_ref, hgc_ref, hu_ref, gatt_ref, gconv_ref, x_ref, t_ref,
             watt_ref, wconv_ref, wout_ref, cw_ref, cb_ref, fg_ref,
             dpb_ref, do_ref, dx2_ref, gatt_o, gconv_o, gout_o, loss_o, gfn_o, gcb_o, gcw_o,
             acc_att, acc_conv, acc_out, carry):
        i = pl.program_id(0)
        tile = n - 1 - i

        @pl.when(i == 0)
        def _():
            acc_att[...] = jnp.zeros_like(acc_att)
            acc_conv[...] = jnp.zeros_like(acc_conv)
            acc_out[...] = jnp.zeros_like(acc_out)
            carry[...] = jnp.zeros_like(carry)
            loss_o[...] = jnp.zeros_like(loss_o)
            gfn_o[...] = jnp.zeros_like(gfn_o)
            gcb_o[...] = jnp.zeros_like(gcb_o)
            gcw_o[...] = jnp.zeros_like(gcw_o)

        att_v = att_ref[...].astype(F32)
        za = za_ref[...].astype(F32)
        sa = _sigmoid(za)
        silu_a = za * sa
        a_b = (att_v * silu_a).astype(BF16)

        gb = gb_ref[...].astype(F32)
        gc = gc_ref[...].astype(F32)
        u = u_ref[...].astype(F32)
        zc = zc_ref[...].astype(F32)
        cu = gc * u
        halo = jnp.where(tile > 0, hgc_ref[...].astype(F32) * hu_ref[...].astype(F32), 0.0)
        cu1 = _shift_down(cu, 1, halo)
        cu2 = _shift_down(cu, 2, halo)
        w0, w1, w2 = cw_ref[0:1, :], cw_ref[1:2, :], cw_ref[2:3, :]
        vconv = w0 * cu2 + w1 * cu1 + w2 * cu + cb_ref[...]
        sc = _sigmoid(zc)
        silu_c = zc * sc
        c_b = (gb * vconv * silu_c).astype(BF16)

        y_att = _nn(a_b, watt_ref[...])
        y_conv = _nn(c_b, wconv_ref[...])
        ga = _sigmoid(gatt_ref[...].astype(F32))
        gv = _sigmoid(gconv_ref[...].astype(F32))
        m_b = (ga * y_att + gv * y_conv).astype(BF16)
        x2 = x_ref[...] + _nn(m_b, wout_ref[...])
        r2 = lax.rsqrt(jnp.mean(x2 * x2, axis=-1, keepdims=True) + EPS)
        x2n = x2 * r2
        fg = fg_ref[...]
        err = x2n * fg - t_ref[...]
        loss_o[...] += jnp.sum(err * err, axis=0, keepdims=True) * (0.5 / d)
        dy = err * (1.0 / d)
        gfn_o[...] += jnp.sum(dy * x2n, axis=0, keepdims=True)
        dyn = dy * fg
        dx2 = r2 * (dyn - x2n * jnp.mean(dyn * x2n, axis=-1, keepdims=True))
        dx2_ref[...] = dx2
        dx2_b = dx2.astype(BF16)

        dm = _nt(dx2_b, wout_ref[...])
        acc_out[...] += _tn(m_b, dx2_b)
        dy_att = dm * ga
        dy_conv = dm * gv
        dpb_ref[:, 5 * dc:5 * dc + d] = (dy_att * y_att * (1.0 - ga)).astype(BF16)
        dpb_ref[:, 5 * dc + d:5 * dc + 2 * d] = (dy_conv * y_conv * (1.0 - gv)).astype(BF16)
        dya_b = dy_att.astype(BF16)
        dyc_b = dy_conv.astype(BF16)
        da_in = _nt(dya_b, watt_ref[...])
        acc_att[...] += _tn(a_b, dya_b)
        dc_in = _nt(dyc_b, wconv_ref[...])
        acc_conv[...] += _tn(c_b, dyc_b)

        do_ref[...] = (da_in * silu_a).astype(BF16)
        dpb_ref[:, 0:dc] = (da_in * att_v * (sa * (1.0 + za * (1.0 - sa)))).astype(BF16)
        dpb_ref[:, dc:2 * dc] = (dc_in * vconv * silu_c).astype(BF16)
        dgs = dc_in * gb
        dvc = dgs * silu_c
        dpb_ref[:, 4 * dc:5 * dc] = (dgs * vconv * (sc * (1.0 + zc * (1.0 - sc)))).astype(BF16)
        gcb_o[...] += jnp.sum(dvc, axis=0, keepdims=True)
        gcw_o[0:1, :] += jnp.sum(dvc * cu2, axis=0, keepdims=True)
        gcw_o[1:2, :] += jnp.sum(dvc * cu1, axis=0, keepdims=True)
        gcw_o[2:3, :] += jnp.sum(dvc * cu, axis=0, keepdims=True)
        nxt = carry[...]
        dcu = w2 * dvc + w1 * _shift_up(dvc, 1, nxt) + w0 * _shift_up(dvc, 2, nxt)
        carry[...] = dvc[0:SUBLANES, :]
        dpb_ref[:, 2 * dc:3 * dc] = (dcu * u).astype(BF16)
        dpb_ref[:, 3 * dc:4 * dc] = (dcu * gc).astype(BF16)

        @pl.when(i == n - 1)
        def _():
            for j in range(n_shards):
                gatt_o[j] = acc_att[:, j * (d // n_shards):(j + 1) * (d // n_shards)].astype(BF16)
                gconv_o[j] = acc_conv[:, j * (d // n_shards):(j + 1) * (d // n_shards)].astype(BF16)
                gout_o[j] = acc_out[j * (d // n_shards):(j + 1) * (d // n_shards), :].astype(BF16)

    rev = lambda width, col_blk: pl.BlockSpec((tm, width), lambda i: (n - 1 - i, col_blk))
    halo_spec = lambda col_blk: pl.BlockSpec(
        (SUBLANES, dc), lambda i: (jnp.maximum((n - 1 - i) * (tm // SUBLANES) - 1, 0), col_blk))
    const = lambda shape: pl.BlockSpec(shape, lambda i: tuple(0 for _ in shape), pipeline_mode=pl.Buffered(1))
    q4 = d // n_shards
    return pl.pallas_call(
        body, name="mixer_mid", grid=(n,),
        in_specs=[rev(dc, 0), rev(dc, 3), rev(dc, 4), rev(dc, 5), rev(dc, 6), rev(dc, 7),
                  halo_spec(5), halo_spec(6), rev(d, 4), rev(d, 5), rev(d, 0), rev(d, 0),
                  const((dc, d)), const((dc, d)), const((d, d)), const((SUBLANES, dc)), const((1, dc)), const((1, d))],
        out_specs=[rev(5 * dc + 2 * d, 0), rev(dc, 0), rev(d, 0),
                   const((n_shards, dc, q4)), const((n_shards, dc, q4)), const((n_shards, q4, d)),
                   const((1, d)), const((1, d)), const((1, dc)), const((SUBLANES, dc))],
        out_shape=[jax.ShapeDtypeStruct((s, 5 * dc + 2 * d), BF16), jax.ShapeDtypeStruct((s, dc), BF16),
                   jax.ShapeDtypeStruct((s, d), F32),
                   jax.ShapeDtypeStruct((n_shards, dc, q4), BF16), jax.ShapeDtypeStruct((n_shards, dc, q4), BF16),
                   jax.ShapeDtypeStruct((n_shards, q4, d), BF16),
                   jax.ShapeDtypeStruct((1, d), F32), jax.ShapeDtypeStruct((1, d), F32),
                   jax.ShapeDtypeStruct((1, dc), F32), jax.ShapeDtypeStruct((SUBLANES, dc), F32)],
        scratch_shapes=[pltpu.VMEM((dc, d), F32), pltpu.VMEM((dc, d), F32), pltpu.VMEM((d, d), F32),
                        pltpu.VMEM((SUBLANES, dc), F32)],
        compiler_params=_params("arbitrary"),
    )(att, proj, proj, proj, proj, proj, proj, proj, proj, proj, x, tgt,
      w_att, w_conv, w_out, conv_w8, conv_b, fin_g)


def _in_proj_bwd_x(dqkv, dpb, w_in, x, dx2, g):
    s, d = x.shape
    tn = dqkv.shape[1]
    nb = dpb.shape[1] // tn
    n = s // TM_MM

    def body(*refs):
        dps, ws = refs[:nb + 1], refs[nb + 1:2 * nb + 2]
        x_ref, dx2_ref, g_ref, gx_ref, gng_ref = refs[2 * nb + 2:]
        i = pl.program_id(0)

        @pl.when(i == 0)
        def _():
            gng_ref[...] = jnp.zeros_like(gng_ref)

        dh = _nt(dps[0][...], ws[0][...])
        for j in range(1, nb + 1):
            dh = dh + _nt(dps[j][...], ws[j][...])
        xv = x_ref[...]
        r = lax.rsqrt(jnp.mean(xv * xv, axis=-1, keepdims=True) + EPS)
        xn = xv * r
        gng_ref[...] += jnp.sum(dh * xn, axis=0, keepdims=True)
        dhn = dh * g_ref[...]
        gx_ref[...] = dx2_ref[...] + r * (dhn - xn * jnp.mean(dhn * xn, axis=-1, keepdims=True))

    tile = lambda width, col_blk: pl.BlockSpec((TM_MM, width), lambda i: (i, col_blk))
    wspec = lambda col_blk: pl.BlockSpec((d, tn), lambda i: (0, col_blk), pipeline_mode=pl.Buffered(1))
    return pl.pallas_call(
        body, name="in_proj_bwd_x", grid=(n,),
        in_specs=[tile(tn, 0)] + [tile(tn, j) for j in range(nb)] + [wspec(j) for j in range(nb + 1)]
        + [tile(d, 0), tile(d, 0), pl.BlockSpec((1, d), lambda i: (0, 0))],
        out_specs=[tile(d, 0), pl.BlockSpec((1, d), lambda i: (0, 0))],
        out_shape=[jax.ShapeDtypeStruct((s, d), F32), jax.ShapeDtypeStruct((1, d), F32)],
        compiler_params=_params("arbitrary"),
    )(dqkv, *([dpb] * nb), *([w_in] * (nb + 1)), x, dx2, g)


def _in_proj_bwd_w(h, dqkv, dpb, tn):
    s, d = h.shape
    n = s // TM_MM
    nb = 1 + dpb.shape[1] // tn

    def body(h_ref, da_ref, db_ref, o_ref, acc):
        j, i = pl.program_id(0), pl.program_id(1)

        @pl.when(i == 0)
        def _():
            acc[...] = jnp.zeros_like(acc)

        @pl.when(j == 0)
        def _():
            acc[...] += _tn(h_ref[...], da_ref[...])

        @pl.when(j > 0)
        def _():
            acc[...] += _tn(h_ref[...], db_ref[...])

        @pl.when(i == n - 1)
        def _():
            o_ref[0] = acc[...].astype(BF16)

    return pl.pallas_call(
        body, name="in_proj_bwd_w", grid=(nb, n),
        in_specs=[pl.BlockSpec((TM_MM, d), lambda j, i: (i, 0)),
                  pl.BlockSpec((TM_MM, tn), lambda j, i: (jnp.where(j == 0, i, 0), 0)),
                  pl.BlockSpec((TM_MM, tn), lambda j, i: (jnp.where(j == 0, 0, i), jnp.maximum(j - 1, 0)))],
        out_specs=pl.BlockSpec((1, d, tn), lambda j, i: (j, 0, 0)),
        out_shape=jax.ShapeDtypeStruct((nb, d, tn), BF16),
        scratch_shapes=[pltpu.VMEM((d, tn), F32)],
        compiler_params=_params("arbitrary", "arbitrary"),
    )(h, dqkv, dpb)


def _sum_slots(recv, name, rows_per_step):
    k, r, c = recv.shape

    def body(r_ref, o_ref):
        total = r_ref[0].astype(F32)
        for slot in range(1, k):
            total = total + r_ref[slot].astype(F32)
        o_ref[...] = total

    return pl.pallas_call(
        body, name=name, grid=(r // rows_per_step,),
        in_specs=[pl.BlockSpec((k, rows_per_step, c), lambda i: (0, i, 0))],
        out_specs=pl.BlockSpec((rows_per_step, c), lambda i: (i, 0)),
        out_shape=jax.ShapeDtypeStruct((r, c), F32),
        compiler_params=_params("parallel"),
    )(recv)


LOSS_ROW = 6


def _sum_small(recv):
    k = recv.shape[0]

    def body(r_ref, o_ref):
        total = r_ref[0]
        for slot in range(1, k):
            total = total + r_ref[slot]
        o_ref[...] = total
        loss = jnp.sum(total[LOSS_ROW:LOSS_ROW + 1, :], axis=1, keepdims=True)
        o_ref[LOSS_ROW:LOSS_ROW + 1, :] = jnp.broadcast_to(loss, (1, SMALL_COLS))

    return pl.pallas_call(
        body, name="reduce_sum_small",
        out_shape=jax.ShapeDtypeStruct((SMALL_ROWS, SMALL_COLS), F32),
    )(recv)


def _adamw(w, g, m, v, name, rows_per_step):
    r, c = w.shape
    c1 = 1.0 / (1.0 - ADAM_B1 ** ADAM_STEP)
    c2 = 1.0 / (1.0 - ADAM_B2 ** ADAM_STEP)

    def body(w_ref, g_ref, m_ref, v_ref, d_ref, mo_ref, vo_ref):
        gv = g_ref[...]
        m2 = ADAM_B1 * m_ref[...] + (1.0 - ADAM_B1) * gv
        v2 = ADAM_B2 * v_ref[...] + (1.0 - ADAM_B2) * (gv * gv)
        mo_ref[...] = m2
        vo_ref[...] = v2
        d_ref[...] = -ADAM_LR * ((m2 * c1) / (jnp.sqrt(v2 * c2) + ADAM_EPS) + ADAM_WD * w_ref[...])

    spec = pl.BlockSpec((rows_per_step, c), lambda i: (i, 0))
    shape = jax.ShapeDtypeStruct((r, c), F32)
    return pl.pallas_call(
        body, name=name, grid=(r // rows_per_step,),
        in_specs=[spec] * 4, out_specs=[spec] * 3, out_shape=[shape] * 3,
        compiler_params=_params("parallel"),
    )(w, g, m, v)


ANY = pl.BlockSpec(memory_space=pl.ANY)


def _position():
    return lax.axis_index("x"), lax.axis_index("y"), lax.axis_index("c")


def _gather_weights(shards, kinds, cw8):
    n_chips = 4
    nw = len(shards)

    def full_shape(a, kind):
        r, c = a.shape
        return (r, c * n_chips) if kind == "cols" else (r * n_chips, c)

    def body(*refs):
        srcs, cw = refs[:nw], refs[nw]
        dsts, cw_all = refs[nw + 1:2 * nw + 1], refs[2 * nw + 1]
        send1, recv1, send2, recv2, ssend, srecv, lsem = refs[2 * nw + 2:]
        x, y, c = _position()
        mine = 2 * x + y
        chips = [(x, 1 - y), (1 - x, y), (1 - x, 1 - y)]

        def window(w, shard, half):
            r, cc = shards[w].shape
            hr = r // 2
            rows = pl.ds(0, r) if half is None else pl.ds(half * hr, hr)
            if kinds[w] == "cols":
                return dsts[w].at[rows, pl.ds(shard * cc, cc)]
            rows = pl.ds(shard * r, r) if half is None else pl.ds(shard * r + half * hr, hr)
            return dsts[w].at[rows, :]

        def my_half(w):
            hr = shards[w].shape[0] // 2
            return srcs[w].at[pl.ds(c * hr, hr), :]

        local = [pltpu.make_async_copy(srcs[w], window(w, mine, None), lsem.at[w]) for w in range(nw)]
        local.append(pltpu.make_async_copy(cw, cw_all.at[mine], lsem.at[nw]))
        for cp in local:
            cp.start()

        def ici(k, w, shard, src):
            kx, ky = chips[k]
            return pltpu.make_async_remote_copy(
                src_ref=src, dst_ref=window(w, shard, c), send_sem=send1.at[k, w], recv_sem=recv1.at[k, w],
                device_id=(kx, ky, c), device_id_type=MESH)

        def d2d(k, w, shard, half):
            return pltpu.make_async_remote_copy(
                src_ref=window(w, shard, half), dst_ref=window(w, shard, half),
                send_sem=send2.at[k, w], recv_sem=recv2.at[k, w],
                device_id=(x, y, 1 - c), device_id_type=MESH)

        def small(k):
            kx, ky = chips[k]
            return pltpu.make_async_remote_copy(
                src_ref=cw, dst_ref=cw_all.at[mine], send_sem=ssend.at[k], recv_sem=srecv.at[k],
                device_id=(kx, ky, c), device_id_type=MESH)

        sent = [ici(k, w, mine, my_half(w)) for k in range(3) for w in range(nw)]
        sent += [small(k) for k in range(3)]
        for cp in sent:
            cp.start()
        passed = []
        for k in range(3):
            kx, ky = chips[k]
            theirs = 2 * kx + ky
            for w in range(nw):
                ici(k, w, theirs, my_half(w)).wait_recv()
                fwd = d2d(k, w, theirs, c)
                fwd.start()
                passed.append(fwd)
        for k in range(3):
            kx, ky = chips[k]
            theirs = 2 * kx + ky
            for w in range(nw):
                d2d(k, w, theirs, 1 - c).wait_recv()
            pltpu.make_async_remote_copy(
                src_ref=cw, dst_ref=cw_all.at[theirs], send_sem=ssend.at[k], recv_sem=srecv.at[k],
                device_id=(kx, ky, c), device_id_type=MESH).wait_recv()
        for cp in sent + passed:
            cp.wait_send()
        for cp in local:
            cp.wait()

    out_shape = [jax.ShapeDtypeStruct(full_shape(a, k), a.dtype) for a, k in zip(shards, kinds)]
    out_shape.append(jax.ShapeDtypeStruct((n_chips,) + cw8.shape, cw8.dtype))
    return pl.pallas_call(
        body, name="gather_weights",
        in_specs=[pl.BlockSpec(memory_space=pltpu.VMEM)] * (nw + 1), out_specs=[ANY] * (nw + 1), out_shape=out_shape,
        scratch_shapes=[pltpu.SemaphoreType.DMA((3, nw)), pltpu.SemaphoreType.DMA((3, nw)),
                        pltpu.SemaphoreType.DMA((3, nw)), pltpu.SemaphoreType.DMA((3, nw)),
                        pltpu.SemaphoreType.DMA((3,)), pltpu.SemaphoreType.DMA((3,)),
                        pltpu.SemaphoreType.DMA((nw + 1,))],
    )(*shards, cw8)


def _reduce_send(parts, small):
    nw = len(parts)
    n_dev = 8

    def body(*refs):
        srcs, sm = refs[:nw], refs[nw]
        dsts, sm_all = refs[nw + 1:2 * nw + 1], refs[2 * nw + 1]
        send, recv, ssend, srecv, lsem = refs[2 * nw + 2:]
        x, y, c = _position()
        me = 4 * x + 2 * y + c

        def peer(k):
            return ((1 - x) if k & 4 else x, (1 - y) if k & 2 else y, (1 - c) if k & 1 else c)

        def piece(w, k):
            px, py, pc = peer(k)
            hr = parts[w].shape[1] // 2
            return srcs[w].at[2 * px + py, pl.ds(pc * hr, hr), :]

        local = [pltpu.make_async_copy(piece(w, 0), dsts[w].at[0], lsem.at[w]) for w in range(nw)]
        local.append(pltpu.make_async_copy(sm, sm_all.at[me], lsem.at[nw]))
        for cp in local:
            cp.start()
        sent = []
        for k in range(1, n_dev):
            for w in range(nw):
                sent.append(pltpu.make_async_remote_copy(
                    src_ref=piece(w, k), dst_ref=dsts[w].at[k], send_sem=send.at[k, w], recv_sem=recv.at[k, w],
                    device_id=peer(k), device_id_type=MESH))
            sent.append(pltpu.make_async_remote_copy(
                src_ref=sm, dst_ref=sm_all.at[me], send_sem=ssend.at[k], recv_sem=srecv.at[k],
                device_id=peer(k), device_id_type=MESH))
        for cp in sent:
            cp.start()
        for k in range(1, n_dev):
            px, py, pc = peer(k)
            for w in range(nw):
                pltpu.make_async_remote_copy(
                    src_ref=piece(w, k), dst_ref=dsts[w].at[k], send_sem=send.at[k, w], recv_sem=recv.at[k, w],
                    device_id=peer(k), device_id_type=MESH).wait_recv()
            pltpu.make_async_remote_copy(
                src_ref=sm, dst_ref=sm_all.at[4 * px + 2 * py + pc], send_sem=ssend.at[k], recv_sem=srecv.at[k],
                device_id=peer(k), device_id_type=MESH).wait_recv()
        for cp in sent:
            cp.wait_send()
        for cp in local:
            cp.wait()

    out_shape = [jax.ShapeDtypeStruct((n_dev, p.shape[1] // 2, p.shape[2]), p.dtype) for p in parts]
    out_shape.append(jax.ShapeDtypeStruct((n_dev,) + small.shape, small.dtype))
    return pl.pallas_call(
        body, name="reduce_send",
        in_specs=[ANY] * (nw + 1), out_specs=[ANY] * (nw + 1), out_shape=out_shape,
        scratch_shapes=[pltpu.SemaphoreType.DMA((n_dev, nw)), pltpu.SemaphoreType.DMA((n_dev, nw)),
                        pltpu.SemaphoreType.DMA((n_dev,)), pltpu.SemaphoreType.DMA((n_dev,)),
                        pltpu.SemaphoreType.DMA((nw + 1,))],
    )(*parts, small)


def _reduce_pair(halves):
    nw = len(halves)

    def body(*refs):
        srcs, dsts = refs[:nw], refs[nw:2 * nw]
        send, recv, lsem = refs[2 * nw:]
        x, y, c = _position()

        def rows(w, half):
            hr = halves[w].shape[0]
            return dsts[w].at[pl.ds(half * hr, hr), :]

        def remote(w, half):
            return pltpu.make_async_remote_copy(
                src_ref=srcs[w], dst_ref=rows(w, half), send_sem=send.at[w], recv_sem=recv.at[w],
                device_id=(x, y, 1 - c), device_id_type=MESH)

        local = [pltpu.make_async_copy(srcs[w], rows(w, c), lsem.at[w]) for w in range(nw)]
        sent = [remote(w, c) for w in range(nw)]
        for cp in sent + local:
            cp.start()
        for w in range(nw):
            remote(w, 1 - c).wait_recv()
        for cp in sent:
            cp.wait_send()
        for cp in local:
            cp.wait()

    return pl.pallas_call(
        body, name="reduce_pair",
        in_specs=[pl.BlockSpec(memory_space=pltpu.VMEM)] * nw, out_specs=[ANY] * nw,
        out_shape=[jax.ShapeDtypeStruct((2 * hv.shape[0], hv.shape[1]), hv.dtype) for hv in halves],
        scratch_shapes=[pltpu.SemaphoreType.DMA((nw,)), pltpu.SemaphoreType.DMA((nw,)),
                        pltpu.SemaphoreType.DMA((nw,))],
        compiler_params=pltpu.CompilerParams(vmem_limit_bytes=VMEM_LIMIT),
    )(*halves)


def _pad_to(a, rows, cols):
    return jnp.pad(a, ((0, rows - a.shape[0]), (0, cols - a.shape[1])))


def _pack_small(norm_g, fin_g, conv_b, conv_w, loss_vec, rel):
    rows = [_pad_to(norm_g, 1, SMALL_COLS), _pad_to(fin_g, 1, SMALL_COLS), _pad_to(conv_b, 1, SMALL_COLS),
            _pad_to(conv_w, 3, SMALL_COLS), _pad_to(loss_vec, 2, SMALL_COLS), _pad_to(rel, HEADS, SMALL_COLS)]
    return jnp.concatenate(rows, axis=0)


def kernel(x, norm_g, w_in, rel_bias, w_att_out, conv_w, conv_b, w_conv_out, w_out, final_norm_g, loss_target, m_norm_g, m_w_in, m_rel_bias, m_w_att_out, m_conv_w, m_conv_b, m_w_conv_out, m_w_out, m_final_norm_g, v_norm_g, v_w_in, v_rel_bias, v_w_att_out, v_conv_w, v_conv_b, v_w_conv_out, v_w_out, v_final_norm_g):
    xs, tgt = x[0], loss_target[0]
    d = xs.shape[1]
    cshard = conv_w.shape[2]
    chip = 2 * lax.axis_index("x") + lax.axis_index("y")

    shards = [w_in[0].astype(BF16), w_att_out[0].astype(BF16), w_conv_out[0].astype(BF16), w_out[0].astype(BF16)]
    cw8 = _pad_to(conv_w[0], SUBLANES, cshard)
    wb_in, wb_att, wb_conv, wb_out, cw_all = _gather_weights(shards, ["cols", "cols", "cols", "rows"], cw8)
    conv_w_full = jnp.transpose(cw_all, (1, 0, 2)).reshape(SUBLANES, 4 * cshard)
    tn = shards[0].shape[1]

    h = _rmsnorm_fwd(xs, norm_g)
    proj = _in_proj(h, wb_in, tn)
    diag = jnp.take(rel_bias[0], _diag_rel_index(), axis=1)
    att, lse = _attn_fwd(diag, proj)
    (dpb, d_att, dx2, g_att_p, g_conv_p, g_out_p, loss_vec, g_fin, g_cb, g_cw) = _mixer_mid(
        att, proj, xs, tgt, wb_att, wb_conv, wb_out, conv_w_full, conv_b, final_norm_g[None, :])
    dqkv, g_rel = _attn_bwd(diag, proj, d_att, att, lse)
    grad_x, g_norm = _in_proj_bwd_x(dqkv, dpb, wb_in, xs, dx2, norm_g)
    g_in_p = _in_proj_bwd_w(h, dqkv, dpb, tn)

    small = _pack_small(g_norm, g_fin, g_cb, g_cw[0:3], loss_vec, g_rel)
    r_in, r_att, r_conv, r_out, r_small = _reduce_send([g_in_p, g_att_p, g_conv_p, g_out_p], small)
    halves = [_sum_slots(r_in, "reduce_sum_in", 64), _sum_slots(r_att, "reduce_sum_att", r_att.shape[1]),
              _sum_slots(r_conv, "reduce_sum_conv", r_conv.shape[1]), _sum_slots(r_out, "reduce_sum_out", r_out.shape[1])]
    gw_in, gw_att, gw_conv, gw_out = _reduce_pair(halves)
    gs = _sum_small(r_small)

    g_cw_mine = lax.dynamic_slice(gs, (3, chip * cshard), (3, cshard))
    zero2 = jnp.zeros((2, 1), F32)
    pack = lambda ng, fg, cb, cw, rb: _pack_small(ng, fg[None, :], cb, cw[0], zero2, rb[0])
    gs_mine = _pack_small(gs[0:1], gs[1:2], gs[2:3, :conv_b.shape[1]], g_cw_mine, zero2, gs[8:16, :N_REL])
    ds, ms, vs = _adamw(pack(norm_g, final_norm_g, conv_b, conv_w, rel_bias), gs_mine,
                        pack(m_norm_g, m_final_norm_g, m_conv_b, m_conv_w, m_rel_bias),
                        pack(v_norm_g, v_final_norm_g, v_conv_b, v_conv_w, v_rel_bias), "adamw_small", SMALL_ROWS)

    def unpack(a):
        return {"norm_g": a[0:1], "final_norm_g": a[1], "conv_b": a[2:3, :conv_b.shape[1]],
                "conv_w": a[3:6, :cshard][None], "rel_bias": a[8:16, :N_REL][None]}

    g_small, d_small, m_small, v_small = unpack(gs_mine), unpack(ds), unpack(ms), unpack(vs)

    big = {}
    for name, w, g, m, v, rows in (("w_in", w_in, gw_in, m_w_in, v_w_in, 128),
                                   ("w_att_out", w_att_out, gw_att, m_w_att_out, v_w_att_out, 256),
                                   ("w_conv_out", w_conv_out, gw_conv, m_w_conv_out, v_w_conv_out, 256),
                                   ("w_out", w_out, gw_out, m_w_out, v_w_out, 128)):
        dw, mw, vw = _adamw(w[0], g, m[0], v[0], "adamw_" + name, rows)
        big[name] = (g[None], dw[None], mw[None], vw[None])

    order = ["norm_g", "w_in", "rel_bias", "w_att_out", "conv_w", "conv_b", "w_conv_out", "w_out", "final_norm_g"]
    outs = [gs[LOSS_ROW, 0], grad_x[None]]
    for which, small_d in enumerate((g_small, d_small, m_small, v_small)):
        for name in order:
            outs.append(big[name][which] if name in big else small_d[name])
    return tuple(outs)
```

```python
import numpy as np
import jax
import jax.numpy as jnp
from jax import lax
from jax.experimental import pallas as pl
from jax.experimental.pallas import tpu as pltpu

F32 = jnp.float32
BF16 = jnp.bfloat16
MESH = pl.DeviceIdType.MESH

CHUNK = 64
N_LEFT = 8
HEADS = 8
HEAD_DIM = 64
D_ATT = HEADS * HEAD_DIM
MAX_REL = 128
N_REL = 2 * MAX_REL + 1
EPS = 1e-6
NEG_BIG = -1e30
ADAM_LR, ADAM_B1, ADAM_B2, ADAM_EPS, ADAM_WD, ADAM_STEP = 0.001, 0.9, 0.999, 1e-08, 0.01, 10

LANES = 128
SUBLANES = 8
VMEM_LIMIT = 56 * 1024 * 1024

QB = 2 * CHUNK
KW = N_LEFT * CHUNK + QB
DIAG = KW + QB
TQ = N_LEFT * CHUNK
TM_MID = 256
TM_MM = 512
SMALL_ROWS, SMALL_COLS = 16, 1024


def _params(*sem):
    return pltpu.CompilerParams(dimension_semantics=sem, vmem_limit_bytes=VMEM_LIMIT)


def _nt(a, b):
    return lax.dot_general(a, b, (((1,), (1,)), ((), ())), preferred_element_type=F32)


def _tn(a, b):
    return lax.dot_general(a, b, (((0,), (0,)), ((), ())), preferred_element_type=F32)


def _nn(a, b):
    return jnp.dot(a, b, preferred_element_type=F32)


def _diag_rel_index():
    d = np.arange(DIAG)
    diff = np.where(d < KW, d, d - DIAG)
    rel = N_LEFT * CHUNK - diff
    return np.clip(rel, -MAX_REL, MAX_REL) + MAX_REL


def _build_bias(diag_ref, bias_scr):
    r = lax.broadcasted_iota(jnp.int32, (QB, KW), 0) // CHUNK
    s = lax.broadcasted_iota(jnp.int32, (QB, KW), 1) // CHUNK
    allowed = (s >= r) & (s <= r + N_LEFT)
    for h in range(HEADS):
        row = jnp.broadcast_to(diag_ref[h:h + 1, :], (QB, DIAG))
        t = pltpu.roll(row, 0, 1, stride=1, stride_axis=0)
        bias_scr[h // 2, (h % 2) * QB:(h % 2 + 1) * QB, :] = jnp.where(allowed, t[:, :KW], NEG_BIG)


def _stack_heads(a, lane_hi):
    zero = jnp.zeros_like(a)
    return jnp.concatenate([jnp.where(lane_hi, zero, a), jnp.where(lane_hi, a, zero)], axis=0)


def _rmsnorm_fwd(x, g):
    s, d = x.shape

    def body(x_ref, g_ref, h_ref):
        xv = x_ref[...]
        r = lax.rsqrt(jnp.mean(xv * xv, axis=-1, keepdims=True) + EPS)
        h_ref[...] = ((xv * r) * g_ref[...]).astype(BF16)

    return pl.pallas_call(
        body, name="rmsnorm_fwd", grid=(s // TM_MM,),
        in_specs=[pl.BlockSpec((TM_MM, d), lambda i: (i, 0)), pl.BlockSpec((1, d), lambda i: (0, 0))],
        out_specs=pl.BlockSpec((TM_MM, d), lambda i: (i, 0)),
        out_shape=jax.ShapeDtypeStruct((s, d), BF16),
        compiler_params=_params("parallel"),
    )(x, g)


def _in_proj(h, w, tn):
    s, d = h.shape
    e = w.shape[1]

    def body(h_ref, w_ref, o_ref):
        o_ref[...] = _nn(h_ref[...], w_ref[...]).astype(BF16)

    return pl.pallas_call(
        body, name="in_proj", grid=(e // tn, s // TM_MM),
        in_specs=[pl.BlockSpec((TM_MM, d), lambda j, i: (i, 0)), pl.BlockSpec((d, tn), lambda j, i: (0, j))],
        out_specs=pl.BlockSpec((TM_MM, tn), lambda j, i: (i, j)),
        out_shape=jax.ShapeDtypeStruct((s, e), BF16),
        compiler_params=_params("parallel", "parallel"),
    )(h, w)


def _attn_fwd(diag, proj, shards, kinds, cw8):
    s = proj.shape[0]
    n = s // TQ
    nw = len(shards)
    scale = HEAD_DIM ** -0.5

    def body(*refs):
        diag_ref, q_ref, kp_ref, kc_ref, vp_ref, vc_ref = refs[:6]
        srcs, cw = refs[6:6 + nw], refs[6 + nw]
        o_ref, lse_ref = refs[7 + nw:9 + nw]
        dsts, cw_all = refs[9 + nw:9 + 2 * nw], refs[9 + 2 * nw]
        bias_scr, kcat, vcat = refs[10 + 2 * nw:13 + 2 * nw]
        start, forward, finish = _gather_plan(kinds, srcs, dsts, cw, cw_all, *refs[13 + 2 * nw:])
        i = pl.program_id(0)

        @pl.when(i == 0)
        def _():
            start()
            _build_bias(diag_ref, bias_scr)

        @pl.when(i == n // 2)
        def _():
            forward()

        @pl.when(i == n - 1)
        def _():
            finish()

        kcat[0:TQ, :] = kp_ref[...]
        kcat[TQ:2 * TQ, :] = kc_ref[...]
        vcat[0:TQ, :] = vp_ref[...]
        vcat[TQ:2 * TQ, :] = vc_ref[...]
        lane_hi = lax.broadcasted_iota(jnp.int32, (QB, LANES), 1) >= HEAD_DIM
        col = lax.broadcasted_iota(jnp.int32, (2 * QB, KW), 1)

        def make_block(first_tile):
            def block(b, carry):
                r0 = pl.multiple_of(b * QB, QB)
                for p in range(HEADS // 2):
                    lanes = slice(LANES * p, LANES * (p + 1))
                    q2 = _stack_heads(q_ref[pl.ds(r0, QB), lanes] * scale, lane_hi)
                    kw = kcat[pl.ds(r0, KW), lanes]
                    vw = vcat[pl.ds(r0, KW), lanes]
                    sc = _nt(q2, kw) + bias_scr[p]
                    if first_tile:
                        sc = jnp.where(col >= TQ - r0, sc, NEG_BIG)
                    m = jnp.max(sc, axis=1, keepdims=True)
                    pe = jnp.exp(sc - m)
                    l = jnp.sum(pe, axis=1, keepdims=True)
                    o2 = _nn(pe.astype(BF16), vw) / l
                    lse2 = m + jnp.log(l)
                    lse_ref[pl.ds(r0, QB), 2 * p:2 * p + 1] = lse2[0:QB, :]
                    lse_ref[pl.ds(r0, QB), 2 * p + 1:2 * p + 2] = lse2[QB:2 * QB, :]
                    o_ref[pl.ds(r0, QB), lanes] = jnp.where(lane_hi, o2[QB:2 * QB, :], o2[0:QB, :]).astype(BF16)
                return carry
            return block

        @pl.when(i == 0)
        def _():
            lax.fori_loop(0, TQ // QB, make_block(True), 0)

        @pl.when(i > 0)
        def _():
            lax.fori_loop(0, TQ // QB, make_block(False), 0)

    blk = lambda col_blk, prev: pl.BlockSpec(
        (TQ, D_ATT), (lambda i: (jnp.maximum(i - 1, 0), col_blk)) if prev else (lambda i: (i, col_blk)))
    vmem = pl.BlockSpec(memory_space=pltpu.VMEM)
    return pl.pallas_call(
        body, name="attn_fwd", grid=(n,),
        in_specs=[pl.BlockSpec((HEADS, DIAG), lambda i: (0, 0)),
                  blk(0, False), blk(1, True), blk(1, False), blk(2, True), blk(2, False)] + [vmem] * (nw + 1),
        out_specs=[pl.BlockSpec((TQ, D_ATT), lambda i: (i, 0)), pl.BlockSpec((TQ, HEADS), lambda i: (i, 0))]
        + [ANY] * (nw + 1),
        out_shape=[jax.ShapeDtypeStruct((s, D_ATT), BF16), jax.ShapeDtypeStruct((s, HEADS), F32)]
        + _gather_out_shapes(shards, kinds, cw8),
        scratch_shapes=[pltpu.VMEM((HEADS // 2, 2 * QB, KW), F32), pltpu.VMEM((2 * TQ, D_ATT), BF16),
                        pltpu.VMEM((2 * TQ, D_ATT), BF16)] + _gather_sems(nw),
        compiler_params=_params("arbitrary"),
    )(diag, proj, proj, proj, proj, proj, *shards, cw8)


def _attn_bwd(diag, proj, d_att, att, lse, parts):
    s = proj.shape[0]
    n = s // TQ
    npart = len(parts)
    scale = HEAD_DIM ** -0.5
    rel_pad = 3 * LANES

    def body(*refs):
        diag_ref, q_ref, kp_ref, kc_ref, vp_ref, vc_ref, do_ref, o_ref, lse_ref = refs[:9]
        part_refs = refs[9:9 + npart]
        dqkv_ref, dbias_ref = refs[9 + npart:11 + npart]
        slot_refs = refs[11 + npart:11 + 2 * npart]
        bias_scr, dbias_acc, kcat, vcat, dk_acc, dv_acc, dq_scr = refs[11 + 2 * npart:18 + 2 * npart]
        start, finish = _scatter_plan(part_refs, slot_refs, *refs[18 + 2 * npart:])
        i = pl.program_id(0)

        @pl.when(i == 0)
        def _():
            start()
            _build_bias(diag_ref, bias_scr)
            dbias_acc[...] = jnp.zeros_like(dbias_acc)
            dk_acc[...] = jnp.zeros_like(dk_acc)
            dv_acc[...] = jnp.zeros_like(dv_acc)

        @pl.when(i > 0)
        def _():
            dqkv_ref[:, 0:D_ATT] = dq_scr[...]
            dk_acc[0:TQ, :] = dk_acc[TQ:2 * TQ, :]
            dk_acc[TQ:2 * TQ, :] = jnp.zeros((TQ, D_ATT), F32)
            dv_acc[0:TQ, :] = dv_acc[TQ:2 * TQ, :]
            dv_acc[TQ:2 * TQ, :] = jnp.zeros((TQ, D_ATT), F32)

        @pl.when(i < n)
        def _():
            kcat[0:TQ, :] = kp_ref[...]
            kcat[TQ:2 * TQ, :] = kc_ref[...]
            vcat[0:TQ, :] = vp_ref[...]
            vcat[TQ:2 * TQ, :] = vc_ref[...]

        lane_hi = lax.broadcasted_iota(jnp.int32, (QB, LANES), 1) >= HEAD_DIM
        col = lax.broadcasted_iota(jnp.int32, (2 * QB, KW), 1)

        def make_block(first_tile):
            def block(b, carry):
                r0 = pl.multiple_of(b * QB, QB)
                for p in range(HEADS // 2):
                    lanes = slice(LANES * p, LANES * (p + 1))
                    q2 = _stack_heads(q_ref[pl.ds(r0, QB), lanes] * scale, lane_hi)
                    kw = kcat[pl.ds(r0, KW), lanes]
                    vw = vcat[pl.ds(r0, KW), lanes]
                    dop = do_ref[pl.ds(r0, QB), lanes]
                    do2 = _stack_heads(dop, lane_hi)
                    prod = dop.astype(F32) * o_ref[pl.ds(r0, QB), lanes].astype(F32)
                    delta2 = jnp.concatenate(
                        [jnp.sum(jnp.where(lane_hi, 0.0, prod), axis=1, keepdims=True),
                         jnp.sum(jnp.where(lane_hi, prod, 0.0), axis=1, keepdims=True)], axis=0)
                    lse2 = jnp.concatenate([lse_ref[pl.ds(r0, QB), 2 * p:2 * p + 1],
                                            lse_ref[pl.ds(r0, QB), 2 * p + 1:2 * p + 2]], axis=0)
                    sc = _nt(q2, kw) + bias_scr[p]
                    if first_tile:
                        sc = jnp.where(col >= TQ - r0, sc, NEG_BIG)
                    pr = jnp.exp(sc - lse2)
                    ds = pr * (_nt(do2, vw) - delta2)
                    dbias_acc[p] += ds
                    dsb = ds.astype(BF16)
                    dv_acc[pl.ds(r0, KW), lanes] += _tn(pr.astype(BF16), do2)
                    dk_acc[pl.ds(r0, KW), lanes] += _tn(dsb, q2)
                    dq2 = _nn(dsb, kw)
                    dq = jnp.where(lane_hi, dq2[QB:2 * QB, :], dq2[0:QB, :]) * scale
                    dq_scr[pl.ds(r0, QB), lanes] = dq.astype(BF16)
                return carry
            return block

        @pl.when(i == 0)
        def _():
            lax.fori_loop(0, TQ // QB, make_block(True), 0)

        @pl.when((i > 0) & (i < n))
        def _():
            lax.fori_loop(0, TQ // QB, make_block(False), 0)

        @pl.when(i > 0)
        def _():
            dqkv_ref[:, D_ATT:2 * D_ATT] = dk_acc[0:TQ, :].astype(BF16)
            dqkv_ref[:, 2 * D_ATT:3 * D_ATT] = dv_acc[0:TQ, :].astype(BF16)

        @pl.when(i == n)
        def _():
            d_iota = lax.broadcasted_iota(jnp.int32, (DIAG, rel_pad), 0)
            n_iota = lax.broadcasted_iota(jnp.int32, (DIAG, rel_pad), 1)
            diff = jnp.where(d_iota < KW, d_iota, d_iota - DIAG)
            idx = jnp.clip(N_LEFT * CHUNK - diff, -MAX_REL, MAX_REL) + MAX_REL
            onehot = (idx == n_iota).astype(F32)
            rows = []
            for hd in range(HEADS):
                acc = dbias_acc[hd // 2, (hd % 2) * QB:(hd % 2 + 1) * QB, :]
                a = jnp.concatenate([acc, jnp.zeros((QB, DIAG - KW), F32)], axis=1)
                g8 = a[0:SUBLANES, :]
                for blk in range(1, QB // SUBLANES):
                    g8 = g8 + pltpu.roll(a[blk * SUBLANES:(blk + 1) * SUBLANES, :], DIAG - blk * SUBLANES, 1)
                g1 = g8[0:1, :]
                for r in range(1, SUBLANES):
                    g1 = g1 + pltpu.roll(g8[r:r + 1, :], DIAG - r, 1)
                rows.append(g1)
            g = jnp.concatenate(rows, axis=0)
            dbias_ref[...] = jnp.dot(g, onehot, preferred_element_type=F32, precision=lax.Precision.HIGHEST)
            finish()

    last = n - 1
    cur = lambda col_blk: pl.BlockSpec((TQ, D_ATT), lambda i: (jnp.minimum(i, last), col_blk))
    prev = lambda col_blk: pl.BlockSpec((TQ, D_ATT), lambda i: (jnp.maximum(jnp.minimum(i, last) - 1, 0), col_blk))
    return pl.pallas_call(
        body, name="attn_bwd", grid=(n + 1,),
        in_specs=[pl.BlockSpec((HEADS, DIAG), lambda i: (0, 0)),
                  cur(0), prev(1), cur(1), prev(2), cur(2), cur(0), cur(0),
                  pl.BlockSpec((TQ, HEADS), lambda i: (jnp.minimum(i, last), 0))] + [ANY] * npart,
        out_specs=[pl.BlockSpec((TQ, 3 * D_ATT), lambda i: (jnp.maximum(i - 1, 0), 0)),
                   pl.BlockSpec((HEADS, rel_pad), lambda i: (0, 0))] + [ANY] * npart,
        out_shape=[jax.ShapeDtypeStruct((s, 3 * D_ATT), BF16), jax.ShapeDtypeStruct((HEADS, rel_pad), F32)]
        + _scatter_out_shapes(parts),
        scratch_shapes=[pltpu.VMEM((HEADS // 2, 2 * QB, KW), F32), pltpu.VMEM((HEADS // 2, 2 * QB, KW), F32),
                        pltpu.VMEM((2 * TQ, D_ATT), BF16), pltpu.VMEM((2 * TQ, D_ATT), BF16),
                        pltpu.VMEM((2 * TQ, D_ATT), F32), pltpu.VMEM((2 * TQ, D_ATT), F32),
                        pltpu.VMEM((TQ, D_ATT), BF16)] + _scatter_sems(npart),
        compiler_params=_params("arbitrary"),
    )(diag, proj, proj, proj, proj, proj, d_att, att, lse, *parts)


def _shift_down(a, k, halo):
    rolled = pltpu.roll(a, k, 0)
    row = lax.broadcasted_iota(jnp.int32, halo.shape, 0)
    first = jnp.where(row < k, pltpu.roll(halo, k, 0), rolled[0:SUBLANES, :])
    return jnp.concatenate([first, rolled[SUBLANES:, :]], axis=0)


def _shift_up(a, k, nxt):
    tm = a.shape[0]
    rolled = pltpu.roll(a, tm - k, 0)
    row = lax.broadcasted_iota(jnp.int32, nxt.shape, 0)
    last = jnp.where(row >= SUBLANES - k, pltpu.roll(nxt, SUBLANES - k, 0), rolled[tm - SUBLANES:, :])
    return jnp.concatenate([rolled[:tm - SUBLANES, :], last], axis=0)


def _sigmoid(v):
    return 1.0 / (1.0 + jnp.exp(-v))


def _mixer_mid(att, proj, x, tgt, w_att, w_conv, w_out, conv_w8, conv_b, fin_g):
    s, d = x.shape
    dc = D_ATT
    n = s // TM_MID
    tm = TM_MID
    n_shards = 4

    def body(att_ref, za_ref, gb_ref, gc_ref, u_ref, zc_ref, hgc_ref, hu_ref, gatt_ref, gconv_ref, x_ref, t_ref,
             watt_ref, wconv_ref, wout_ref, cw_ref, cb_ref, fg_ref,
             dpb_ref, do_ref, dx2_ref, gatt_o, gconv_o, gout_o, loss_o, gfn_o, gcb_o, gcw_o,
             acc_att, acc_conv, acc_out, carry):
        i = pl.program_id(0)
        tile = n - 1 - i

        @pl.when(i == 0)
        def _():
            acc_att[...] = jnp.zeros_like(acc_att)
            acc_conv[...] = jnp.zeros_like(acc_conv)
            acc_out[...] = jnp.zeros_like(acc_out)
            carry[...] = jnp.zeros_like(carry)
            loss_o[...] = jnp.zeros_like(loss_o)
            gfn_o[...] = jnp.zeros_like(gfn_o)
            gcb_o[...] = jnp.zeros_like(gcb_o)
            gcw_o[...] = jnp.zeros_like(gcw_o)

        att_v = att_ref[...].astype(F32)
        za = za_ref[...].astype(F32)
        sa = _sigmoid(za)
        silu_a = za * sa
        a_b = (att_v * silu_a).astype(BF16)

        gb = gb_ref[...].astype(F32)
        gc = gc_ref[...].astype(F32)
        u = u_ref[...].astype(F32)
        zc = zc_ref[...].astype(F32)
        cu = gc * u
        halo = jnp.where(tile > 0, hgc_ref[...].astype(F32) * hu_ref[...].astype(F32), 0.0)
        cu1 = _shift_down(cu, 1, halo)
        cu2 = _shift_down(cu, 2, halo)
        w0, w1, w2 = cw_ref[0:1, :], cw_ref[1:2, :], cw_ref[2:3, :]
        vconv = w0 * cu2 + w1 * cu1 + w2 * cu + cb_ref[...]
        sc = _sigmoid(zc)
        silu_c = zc * sc
        c_b = (gb * vconv * silu_c).astype(BF16)

        y_att = _nn(a_b, watt_ref[...])
        y_conv = _nn(c_b, wconv_ref[...])
        ga = _sigmoid(gatt_ref[...].astype(F32))
        gv = _sigmoid(gconv_ref[...].astype(F32))
        m_b = (ga * y_att + gv * y_conv).astype(BF16)
        x2 = x_ref[...] + _nn(m_b, wout_ref[...])
        r2 = lax.rsqrt(jnp.mean(x2 * x2, axis=-1, keepdims=True) + EPS)
        x2n = x2 * r2
        fg = fg_ref[...]
        err = x2n * fg - t_ref[...]
        loss_o[...] += jnp.sum(err * err, axis=0, keepdims=True) * (0.5 / d)
        dy = err * (1.0 / d)
        gfn_o[...] += jnp.sum(dy * x2n, axis=0, keepdims=True)
        dyn = dy * fg
        dx2 = r2 * (dyn - x2n * jnp.mean(dyn * x2n, axis=-1, keepdims=True))
        dx2_ref[...] = dx2
        dx2_b = dx2.astype(BF16)

        dm = _nt(dx2_b, wout_ref[...])
        acc_out[...] += _tn(m_b, dx2_b)
        dy_att = dm * ga
        dy_conv = dm * gv
        dpb_ref[:, 5 * dc:5 * dc + d] = (dy_att * y_att * (1.0 - ga)).astype(BF16)
        dpb_ref[:, 5 * dc + d:5 * dc + 2 * d] = (dy_conv * y_conv * (1.0 - gv)).astype(BF16)
        dya_b = dy_att.astype(BF16)
        dyc_b = dy_conv.astype(BF16)
        da_in = _nt(dya_b, watt_ref[...])
        acc_att[...] += _tn(a_b, dya_b)
        dc_in = _nt(dyc_b, wconv_ref[...])
        acc_conv[...] += _tn(c_b, dyc_b)

        do_ref[...] = (da_in * silu_a).astype(BF16)
        dpb_ref[:, 0:dc] = (da_in * att_v * (sa * (1.0 + za * (1.0 - sa)))).astype(BF16)
        dpb_ref[:, dc:2 * dc] = (dc_in * vconv * silu_c).astype(BF16)
        dgs = dc_in * gb
        dvc = dgs * silu_c
        dpb_ref[:, 4 * dc:5 * dc] = (dgs * vconv * (sc * (1.0 + zc * (1.0 - sc)))).astype(BF16)
        gcb_o[...] += jnp.sum(dvc, axis=0, keepdims=True)
        gcw_o[0:1, :] += jnp.sum(dvc * cu2, axis=0, keepdims=True)
        gcw_o[1:2, :] += jnp.sum(dvc * cu1, axis=0, keepdims=True)
        gcw_o[2:3, :] += jnp.sum(dvc * cu, axis=0, keepdims=True)
        nxt = carry[...]
        dcu = w2 * dvc + w1 * _shift_up(dvc, 1, nxt) + w0 * _shift_up(dvc, 2, nxt)
        carry[...] = dvc[0:SUBLANES, :]
        dpb_ref[:, 2 * dc:3 * dc] = (dcu * u).astype(BF16)
        dpb_ref[:, 3 * dc:4 * dc] = (dcu * gc).astype(BF16)

        @pl.when(i == n - 1)
        def _():
            for j in range(n_shards):
                gatt_o[j] = acc_att[:, j * (d // n_shards):(j + 1) * (d // n_shards)].astype(BF16)
                gconv_o[j] = acc_conv[:, j * (d // n_shards):(j + 1) * (d // n_shards)].astype(BF16)
                gout_o[j] = acc_out[j * (d // n_shards):(j + 1) * (d // n_shards), :].astype(BF16)

    rev = lambda width, col_blk: pl.BlockSpec((tm, width), lambda i: (n - 1 - i, col_blk))
    halo_spec = lambda col_blk: pl.BlockSpec(
        (SUBLANES, dc), lambda i: (jnp.maximum((n - 1 - i) * (tm // SUBLANES) - 1, 0), col_blk))
    const = lambda shape: pl.BlockSpec(shape, lambda i: tuple(0 for _ in shape), pipeline_mode=pl.Buffered(1))
    q4 = d // n_shards
    return pl.pallas_call(
        body, name="mixer_mid", grid=(n,),
        in_specs=[rev(dc, 0), rev(dc, 3), rev(dc, 4), rev(dc, 5), rev(dc, 6), rev(dc, 7),
                  halo_spec(5), halo_spec(6), rev(d, 4), rev(d, 5), rev(d, 0), rev(d, 0),
                  const((dc, d)), const((dc, d)), const((d, d)), const((SUBLANES, dc)), const((1, dc)), const((1, d))],
        out_specs=[rev(5 * dc + 2 * d, 0), rev(dc, 0), rev(d, 0),
                   const((n_shards, dc, q4)), const((n_shards, dc, q4)), const((n_shards, q4, d)),
                   const((1, d)), const((1, d)), const((1, dc)), const((SUBLANES, dc))],
        out_shape=[jax.ShapeDtypeStruct((s, 5 * dc + 2 * d), BF16), jax.ShapeDtypeStruct((s, dc), BF16),
                   jax.ShapeDtypeStruct((s, d), F32),
                   jax.ShapeDtypeStruct((n_shards, dc, q4), BF16), jax.ShapeDtypeStruct((n_shards, dc, q4), BF16),
                   jax.ShapeDtypeStruct((n_shards, q4, d), BF16),
                   jax.ShapeDtypeStruct((1, d), F32), jax.ShapeDtypeStruct((1, d), F32),
                   jax.ShapeDtypeStruct((1, dc), F32), jax.ShapeDtypeStruct((SUBLANES, dc), F32)],
        scratch_shapes=[pltpu.VMEM((dc, d), F32), pltpu.VMEM((dc, d), F32), pltpu.VMEM((d, d), F32),
                        pltpu.VMEM((SUBLANES, dc), F32)],
        compiler_params=_params("arbitrary"),
    )(att, proj, proj, proj, proj, proj, proj, proj, proj, proj, x, tgt,
      w_att, w_conv, w_out, conv_w8, conv_b, fin_g)


def _in_proj_bwd_x(dqkv, dpb, w_in, x, dx2, g, part):
    s, d = x.shape
    tn = dqkv.shape[1]
    nb = dpb.shape[1] // tn
    n = s // TM_MM

    def body(*refs):
        dps, ws = refs[:nb + 1], refs[nb + 1:2 * nb + 2]
        x_ref, dx2_ref, g_ref, part_ref, gx_ref, gng_ref, slot_ref = refs[2 * nb + 2:2 * nb + 9]
        start, finish = _scatter_plan([part_ref], [slot_ref], *refs[2 * nb + 9:])
        i = pl.program_id(0)

        @pl.when(i == 0)
        def _():
            start()
            gng_ref[...] = jnp.zeros_like(gng_ref)

        @pl.when(i == n - 1)
        def _():
            finish()

        dh = _nt(dps[0][...], ws[0][...])
        for j in range(1, nb + 1):
            dh = dh + _nt(dps[j][...], ws[j][...])
        xv = x_ref[...]
        r = lax.rsqrt(jnp.mean(xv * xv, axis=-1, keepdims=True) + EPS)
        xn = xv * r
        gng_ref[...] += jnp.sum(dh * xn, axis=0, keepdims=True)
        dhn = dh * g_ref[...]
        gx_ref[...] = dx2_ref[...] + r * (dhn - xn * jnp.mean(dhn * xn, axis=-1, keepdims=True))

    tile = lambda width, col_blk: pl.BlockSpec((TM_MM, width), lambda i: (i, col_blk))
    wspec = lambda col_blk: pl.BlockSpec((d, tn), lambda i: (0, col_blk), pipeline_mode=pl.Buffered(1))
    return pl.pallas_call(
        body, name="in_proj_bwd_x", grid=(n,),
        in_specs=[tile(tn, 0)] + [tile(tn, j) for j in range(nb)] + [wspec(j) for j in range(nb + 1)]
        + [tile(d, 0), tile(d, 0), pl.BlockSpec((1, d), lambda i: (0, 0)), ANY],
        out_specs=[tile(d, 0), pl.BlockSpec((1, d), lambda i: (0, 0)), ANY],
        out_shape=[jax.ShapeDtypeStruct((s, d), F32), jax.ShapeDtypeStruct((1, d), F32)] + _scatter_out_shapes([part]),
        scratch_shapes=_scatter_sems(1),
        compiler_params=_params("arbitrary"),
    )(dqkv, *([dpb] * nb), *([w_in] * (nb + 1)), x, dx2, g, part)


def _in_proj_bwd_w(h, dqkv, dpb, tn):
    s, d = h.shape
    n = s // TM_MM
    nb = 1 + dpb.shape[1] // tn

    def body(h_ref, da_ref, db_ref, o_ref, acc):
        j, i = pl.program_id(0), pl.program_id(1)

        @pl.when(i == 0)
        def _():
            acc[...] = jnp.zeros_like(acc)

        @pl.when(j == 0)
        def _():
            acc[...] += _tn(h_ref[...], da_ref[...])

        @pl.when(j > 0)
        def _():
            acc[...] += _tn(h_ref[...], db_ref[...])

        @pl.when(i == n - 1)
        def _():
            o_ref[0] = acc[...].astype(BF16)

    return pl.pallas_call(
        body, name="in_proj_bwd_w", grid=(nb, n),
        in_specs=[pl.BlockSpec((TM_MM, d), lambda j, i: (i, 0)),
                  pl.BlockSpec((TM_MM, tn), lambda j, i: (jnp.where(j == 0, i, 0), 0)),
                  pl.BlockSpec((TM_MM, tn), lambda j, i: (jnp.where(j == 0, 0, i), jnp.maximum(j - 1, 0)))],
        out_specs=pl.BlockSpec((1, d, tn), lambda j, i: (j, 0, 0)),
        out_shape=jax.ShapeDtypeStruct((nb, d, tn), BF16),
        scratch_shapes=[pltpu.VMEM((d, tn), F32)],
        compiler_params=_params("arbitrary", "arbitrary"),
    )(h, dqkv, dpb)


def _sum_slots(recv, name, rows_per_step):
    k, r, c = recv.shape

    def body(r_ref, o_ref):
        total = r_ref[0].astype(F32)
        for slot in range(1, k):
            total = total + r_ref[slot].astype(F32)
        o_ref[...] = total

    return pl.pallas_call(
        body, name=name, grid=(r // rows_per_step,),
        in_specs=[pl.BlockSpec((k, rows_per_step, c), lambda i: (0, i, 0))],
        out_specs=pl.BlockSpec((rows_per_step, c), lambda i: (i, 0)),
        out_shape=jax.ShapeDtypeStruct((r, c), F32),
        compiler_params=_params("parallel"),
    )(recv)


LOSS_ROW = 6


def _sum_small(recv):
    k = recv.shape[0]

    def body(r_ref, o_ref):
        total = r_ref[0]
        for slot in range(1, k):
            total = total + r_ref[slot]
        o_ref[...] = total
        loss = jnp.sum(total[LOSS_ROW:LOSS_ROW + 1, :], axis=1, keepdims=True)
        o_ref[LOSS_ROW:LOSS_ROW + 1, :] = jnp.broadcast_to(loss, (1, SMALL_COLS))

    return pl.pallas_call(
        body, name="reduce_sum_small",
        out_shape=jax.ShapeDtypeStruct((SMALL_ROWS, SMALL_COLS), F32),
    )(recv)


def _adamw(w, g, m, v, name, rows_per_step):
    r, c = w.shape
    c1 = 1.0 / (1.0 - ADAM_B1 ** ADAM_STEP)
    c2 = 1.0 / (1.0 - ADAM_B2 ** ADAM_STEP)

    def body(w_ref, g_ref, m_ref, v_ref, d_ref, mo_ref, vo_ref):
        gv = g_ref[...]
        m2 = ADAM_B1 * m_ref[...] + (1.0 - ADAM_B1) * gv
        v2 = ADAM_B2 * v_ref[...] + (1.0 - ADAM_B2) * (gv * gv)
        mo_ref[...] = m2
        vo_ref[...] = v2
        d_ref[...] = -ADAM_LR * ((m2 * c1) / (jnp.sqrt(v2 * c2) + ADAM_EPS) + ADAM_WD * w_ref[...])

    spec = pl.BlockSpec((rows_per_step, c), lambda i: (i, 0))
    shape = jax.ShapeDtypeStruct((r, c), F32)
    return pl.pallas_call(
        body, name=name, grid=(r // rows_per_step,),
        in_specs=[spec] * 4, out_specs=[spec] * 3, out_shape=[shape] * 3,
        compiler_params=_params("parallel"),
    )(w, g, m, v)


ANY = pl.BlockSpec(memory_space=pl.ANY)
N_CHIPS = 4
N_DEV = 8


def _position():
    return lax.axis_index("x"), lax.axis_index("y"), lax.axis_index("c")


def _gather_out_shapes(shards, kinds, cw8):
    full = [(a.shape[0], a.shape[1] * N_CHIPS) if k == "cols" else (a.shape[0] * N_CHIPS, a.shape[1])
            for a, k in zip(shards, kinds)]
    return [jax.ShapeDtypeStruct(f, a.dtype) for f, a in zip(full, shards)] + [
        jax.ShapeDtypeStruct((N_CHIPS,) + cw8.shape, cw8.dtype)]


def _gather_sems(nw):
    return [pltpu.SemaphoreType.DMA((3, nw)), pltpu.SemaphoreType.DMA((3, nw)),
            pltpu.SemaphoreType.DMA((3, nw)), pltpu.SemaphoreType.DMA((3, nw)),
            pltpu.SemaphoreType.DMA((3,)), pltpu.SemaphoreType.DMA((3,)), pltpu.SemaphoreType.DMA((nw + 1,))]


def _gather_plan(kinds, srcs, dsts, cw, cw_all, send1, recv1, send2, recv2, ssend, srecv, lsem):
    nw = len(srcs)
    x, y, c = _position()
    mine = 2 * x + y
    chips = [(x, 1 - y), (1 - x, y), (1 - x, 1 - y)]

    def window(w, shard, half):
        r, cc = srcs[w].shape
        hr = r // 2
        if kinds[w] == "cols":
            rows = pl.ds(0, r) if half is None else pl.ds(half * hr, hr)
            return dsts[w].at[rows, pl.ds(shard * cc, cc)]
        rows = pl.ds(shard * r, r) if half is None else pl.ds(shard * r + half * hr, hr)
        return dsts[w].at[rows, :]

    def my_half(w):
        hr = srcs[w].shape[0] // 2
        return srcs[w].at[pl.ds(c * hr, hr), :]

    def local():
        return [pltpu.make_async_copy(srcs[w], window(w, mine, None), lsem.at[w]) for w in range(nw)] + [
            pltpu.make_async_copy(cw, cw_all.at[mine], lsem.at[nw])]

    def ici(k, w, shard):
        kx, ky = chips[k]
        return pltpu.make_async_remote_copy(
            src_ref=my_half(w), dst_ref=window(w, shard, c), send_sem=send1.at[k, w], recv_sem=recv1.at[k, w],
            device_id=(kx, ky, c), device_id_type=MESH)

    def d2d(k, w, shard, half):
        return pltpu.make_async_remote_copy(
            src_ref=window(w, shard, half), dst_ref=window(w, shard, half),
            send_sem=send2.at[k, w], recv_sem=recv2.at[k, w], device_id=(x, y, 1 - c), device_id_type=MESH)

    def small(k, shard):
        kx, ky = chips[k]
        return pltpu.make_async_remote_copy(
            src_ref=cw, dst_ref=cw_all.at[shard], send_sem=ssend.at[k], recv_sem=srecv.at[k],
            device_id=(kx, ky, c), device_id_type=MESH)

    def theirs(k):
        kx, ky = chips[k]
        return 2 * kx + ky

    def start():
        for cp in local():
            cp.start()
        for k in range(3):
            for w in range(nw):
                ici(k, w, mine).start()
            small(k, mine).start()

    def forward():
        for k in range(3):
            for w in range(nw):
                ici(k, w, theirs(k)).wait_recv()
                d2d(k, w, theirs(k), c).start()

    def finish():
        for k in range(3):
            for w in range(nw):
                d2d(k, w, theirs(k), 1 - c).wait_recv()
            small(k, theirs(k)).wait_recv()
        for k in range(3):
            for w in range(nw):
                ici(k, w, mine).wait_send()
                d2d(k, w, theirs(k), c).wait_send()
            small(k, mine).wait_send()
        for cp in local():
            cp.wait()

    return start, forward, finish


def _gather_w_in(shard, cw8):
    def body(src, cw, dst, cw_all, *sems):
        start, forward, finish = _gather_plan(["cols"], [src], [dst], cw, cw_all, *sems)
        start()
        forward()
        finish()

    vmem = pl.BlockSpec(memory_space=pltpu.VMEM)
    return pl.pallas_call(
        body, name="gather_w_in", in_specs=[vmem, vmem], out_specs=[ANY, ANY],
        out_shape=_gather_out_shapes([shard], ["cols"], cw8), scratch_shapes=_gather_sems(1),
    )(shard, cw8)


def _scatter_out_shapes(parts):
    return [jax.ShapeDtypeStruct((N_DEV, p.shape[1] // 2, p.shape[2]), p.dtype) for p in parts]


def _scatter_sems(nw):
    return [pltpu.SemaphoreType.DMA((N_DEV, nw)), pltpu.SemaphoreType.DMA((N_DEV, nw)), pltpu.SemaphoreType.DMA((nw,))]


def _peer(pos, k):
    x, y, c = pos
    return ((1 - x) if k & 4 else x, (1 - y) if k & 2 else y, (1 - c) if k & 1 else c)


def _scatter_plan(srcs, dsts, send, recv, lsem):
    nw = len(srcs)
    pos = _position()

    def piece(w, k):
        px, py, pc = _peer(pos, k)
        hr = srcs[w].shape[1] // 2
        return srcs[w].at[2 * px + py, pl.ds(pc * hr, hr), :]

    def remote(w, k):
        return pltpu.make_async_remote_copy(
            src_ref=piece(w, k), dst_ref=dsts[w].at[k], send_sem=send.at[k, w], recv_sem=recv.at[k, w],
            device_id=_peer(pos, k), device_id_type=MESH)

    def local(w):
        return pltpu.make_async_copy(piece(w, 0), dsts[w].at[0], lsem.at[w])

    def start():
        for w in range(nw):
            local(w).start()
        for k in range(1, N_DEV):
            for w in range(nw):
                remote(w, k).start()

    def finish():
        for k in range(1, N_DEV):
            for w in range(nw):
                remote(w, k).wait_recv()
        for k in range(1, N_DEV):
            for w in range(nw):
                remote(w, k).wait_send()
        for w in range(nw):
            local(w).wait()

    return start, finish


def _reduce_pair(halves, small):
    nw = len(halves)

    def body(*refs):
        srcs, sm = refs[:nw], refs[nw]
        dsts, sm_all = refs[nw + 1:2 * nw + 1], refs[2 * nw + 1]
        send, recv, ssend, srecv, lsem = refs[2 * nw + 2:]
        pos = _position()
        x, y, c = pos
        me = 4 * x + 2 * y + c

        def rows(w, half):
            hr = halves[w].shape[0]
            return dsts[w].at[pl.ds(half * hr, hr), :]

        def remote(w, half):
            return pltpu.make_async_remote_copy(
                src_ref=srcs[w], dst_ref=rows(w, half), send_sem=send.at[w], recv_sem=recv.at[w],
                device_id=(x, y, 1 - c), device_id_type=MESH)

        def bcast(k, slot):
            return pltpu.make_async_remote_copy(
                src_ref=sm, dst_ref=sm_all.at[slot], send_sem=ssend.at[k], recv_sem=srecv.at[k],
                device_id=_peer(pos, k), device_id_type=MESH)

        local = [pltpu.make_async_copy(srcs[w], rows(w, c), lsem.at[w]) for w in range(nw)]
        local.append(pltpu.make_async_copy(sm, sm_all.at[me], lsem.at[nw]))
        sent = [remote(w, c) for w in range(nw)] + [bcast(k, me) for k in range(1, N_DEV)]
        for cp in sent + local:
            cp.start()
        for w in range(nw):
            remote(w, 1 - c).wait_recv()
        for k in range(1, N_DEV):
            px, py, pc = _peer(pos, k)
            bcast(k, 4 * px + 2 * py + pc).wait_recv()
        for cp in sent:
            cp.wait_send()
        for cp in local:
            cp.wait()

    vmem = pl.BlockSpec(memory_space=pltpu.VMEM)
    return pl.pallas_call(
        body, name="reduce_pair",
        in_specs=[vmem] * (nw + 1), out_specs=[ANY] * (nw + 1),
        out_shape=[jax.ShapeDtypeStruct((2 * hv.shape[0], hv.shape[1]), hv.dtype) for hv in halves]
        + [jax.ShapeDtypeStruct((N_DEV,) + small.shape, small.dtype)],
        scratch_shapes=[pltpu.SemaphoreType.DMA((nw,)), pltpu.SemaphoreType.DMA((nw,)),
                        pltpu.SemaphoreType.DMA((N_DEV,)), pltpu.SemaphoreType.DMA((N_DEV,)),
                        pltpu.SemaphoreType.DMA((nw + 1,))],
        compiler_params=pltpu.CompilerParams(vmem_limit_bytes=VMEM_LIMIT),
    )(*halves, small)


def _pad_to(a, rows, cols):
    return jnp.pad(a, ((0, rows - a.shape[0]), (0, cols - a.shape[1])))


def _pack_small(norm_g, fin_g, conv_b, conv_w, loss_vec, rel):
    rows = [_pad_to(norm_g, 1, SMALL_COLS), _pad_to(fin_g, 1, SMALL_COLS), _pad_to(conv_b, 1, SMALL_COLS),
            _pad_to(conv_w, 3, SMALL_COLS), _pad_to(loss_vec, 2, SMALL_COLS), _pad_to(rel, HEADS, SMALL_COLS)]
    return jnp.concatenate(rows, axis=0)


def kernel(x, norm_g, w_in, rel_bias, w_att_out, conv_w, conv_b, w_conv_out, w_out, final_norm_g, loss_target, m_norm_g, m_w_in, m_rel_bias, m_w_att_out, m_conv_w, m_conv_b, m_w_conv_out, m_w_out, m_final_norm_g, v_norm_g, v_w_in, v_rel_bias, v_w_att_out, v_conv_w, v_conv_b, v_w_conv_out, v_w_out, v_final_norm_g):
    xs, tgt = x[0], loss_target[0]
    cshard = conv_w.shape[2]
    chip = 2 * lax.axis_index("x") + lax.axis_index("y")

    shards = [w_in[0].astype(BF16), w_att_out[0].astype(BF16), w_conv_out[0].astype(BF16), w_out[0].astype(BF16)]
    cw8 = _pad_to(conv_w[0], SUBLANES, cshard)
    wb_in, _ = _gather_w_in(shards[0], cw8)
    tn = shards[0].shape[1]

    h = _rmsnorm_fwd(xs, norm_g)
    proj = _in_proj(h, wb_in, tn)
    diag = jnp.take(rel_bias[0], _diag_rel_index(), axis=1)
    att, lse, wb_att, wb_conv, wb_out, cw_all = _attn_fwd(diag, proj, shards[1:], ["cols", "cols", "rows"], cw8)
    conv_w_full = jnp.transpose(cw_all, (1, 0, 2)).reshape(SUBLANES, N_CHIPS * cshard)
    (dpb, d_att, dx2, g_att_p, g_conv_p, g_out_p, loss_vec, g_fin, g_cb, g_cw) = _mixer_mid(
        att, proj, xs, tgt, wb_att, wb_conv, wb_out, conv_w_full, conv_b, final_norm_g[None, :])
    dqkv, g_rel, r_att, r_conv, r_out = _attn_bwd(diag, proj, d_att, att, lse, [g_att_p, g_conv_p, g_out_p])
    g_in_p = _in_proj_bwd_w(h, dqkv, dpb, tn)
    grad_x, g_norm, r_in = _in_proj_bwd_x(dqkv, dpb, wb_in, xs, dx2, norm_g, g_in_p)

    small = _pack_small(g_norm, g_fin, g_cb, g_cw[0:3], loss_vec, g_rel)
    halves = [_sum_slots(r_in, "reduce_sum_in", 64), _sum_slots(r_att, "reduce_sum_att", r_att.shape[1]),
              _sum_slots(r_conv, "reduce_sum_conv", r_conv.shape[1]), _sum_slots(r_out, "reduce_sum_out", r_out.shape[1])]
    gw_in, gw_att, gw_conv, gw_out, r_small = _reduce_pair(halves, small)
    gs = _sum_small(r_small)

    g_cw_mine = lax.dynamic_slice(gs, (3, chip * cshard), (3, cshard))
    zero2 = jnp.zeros((2, 1), F32)
    pack = lambda ng, fg, cb, cw, rb: _pack_small(ng, fg[None, :], cb, cw[0], zero2, rb[0])
    gs_mine = _pack_small(gs[0:1], gs[1:2], gs[2:3, :conv_b.shape[1]], g_cw_mine, zero2, gs[8:16, :N_REL])
    ds, ms, vs = _adamw(pack(norm_g, final_norm_g, conv_b, conv_w, rel_bias), gs_mine,
                        pack(m_norm_g, m_final_norm_g, m_conv_b, m_conv_w, m_rel_bias),
                        pack(v_norm_g, v_final_norm_g, v_conv_b, v_conv_w, v_rel_bias), "adamw_small", SMALL_ROWS)

    def unpack(a):
        return {"norm_g": a[0:1], "final_norm_g": a[1], "conv_b": a[2:3, :conv_b.shape[1]],
                "conv_w": a[3:6, :cshard][None], "rel_bias": a[8:16, :N_REL][None]}

    g_small, d_small, m_small, v_small = unpack(gs_mine), unpack(ds), unpack(ms), unpack(vs)

    big = {}
    for name, w, g, m, v, rows in (("w_in", w_in, gw_in, m_w_in, v_w_in, 128),
                                   ("w_att_out", w_att_out, gw_att, m_w_att_out, v_w_att_out, 256),
                                   ("w_conv_out", w_conv_out, gw_conv, m_w_conv_out, v_w_conv_out, 256),
                                   ("w_out", w_out, gw_out, m_w_out, v_w_out, 128)):
        dw, mw, vw = _adamw(w[0], g, m[0], v[0], "adamw_" + name, rows)
        big[name] = (g[None], dw[None], mw[None], vw[None])

    order = ["norm_g", "w_in", "rel_bias", "w_att_out", "conv_w", "conv_b", "w_conv_out", "w_out", "final_norm_g"]
    outs = [gs[LOSS_ROW, 0], grad_x[None]]
    for which, small_d in enumerate((g_small, d_small, m_small, v_small)):
        for name in order:
            outs.append(big[name][which] if name in big else small_d[name])
    return tuple(outs)
```

```python
import numpy as np
import jax
import jax.numpy as jnp
from jax import lax
from jax.experimental import pallas as pl
from jax.experimental.pallas import tpu as pltpu

F32 = jnp.float32
BF16 = jnp.bfloat16
MESH = pl.DeviceIdType.MESH

CHUNK = 64
N_LEFT = 8
HEADS = 8
HEAD_DIM = 64
D_ATT = HEADS * HEAD_DIM
MAX_REL = 128
N_REL = 2 * MAX_REL + 1
EPS = 1e-6
NEG_BIG = -1e30
ADAM_LR, ADAM_B1, ADAM_B2, ADAM_EPS, ADAM_WD, ADAM_STEP = 0.001, 0.9, 0.999, 1e-08, 0.01, 10

LANES = 128
SUBLANES = 8
VMEM_LIMIT = 56 * 1024 * 1024

QB = 2 * CHUNK
KW = N_LEFT * CHUNK + QB
DIAG = KW + QB
TQ = N_LEFT * CHUNK
TM_MID = 256
TM_MM = 512
SMALL_ROWS, SMALL_COLS = 16, 1024


def _params(*sem):
    return pltpu.CompilerParams(dimension_semantics=sem, vmem_limit_bytes=VMEM_LIMIT)


def _nt(a, b):
    return lax.dot_general(a, b, (((1,), (1,)), ((), ())), preferred_element_type=F32)


def _tn(a, b):
    return lax.dot_general(a, b, (((0,), (0,)), ((), ())), preferred_element_type=F32)


def _nn(a, b):
    return jnp.dot(a, b, preferred_element_type=F32)


def _diag_rel_index():
    d = np.arange(DIAG)
    diff = np.where(d < KW, d, d - DIAG)
    rel = N_LEFT * CHUNK - diff
    return np.clip(rel, -MAX_REL, MAX_REL) + MAX_REL


def _build_bias(diag_ref, bias_scr):
    r = lax.broadcasted_iota(jnp.int32, (QB, KW), 0) // CHUNK
    s = lax.broadcasted_iota(jnp.int32, (QB, KW), 1) // CHUNK
    allowed = (s >= r) & (s <= r + N_LEFT)
    for h in range(HEADS):
        row = jnp.broadcast_to(diag_ref[h:h + 1, :], (QB, DIAG))
        t = pltpu.roll(row, 0, 1, stride=1, stride_axis=0)
        bias_scr[h // 2, (h % 2) * QB:(h % 2 + 1) * QB, :] = jnp.where(allowed, t[:, :KW], NEG_BIG)


def _stack_heads(a, lane_hi):
    zero = jnp.zeros_like(a)
    return jnp.concatenate([jnp.where(lane_hi, zero, a), jnp.where(lane_hi, a, zero)], axis=0)


def _rmsnorm_fwd(x, g):
    s, d = x.shape

    def body(x_ref, g_ref, h_ref):
        xv = x_ref[...]
        r = lax.rsqrt(jnp.mean(xv * xv, axis=-1, keepdims=True) + EPS)
        h_ref[...] = ((xv * r) * g_ref[...]).astype(BF16)

    return pl.pallas_call(
        body, name="rmsnorm_fwd", grid=(s // TM_MM,),
        in_specs=[pl.BlockSpec((TM_MM, d), lambda i: (i, 0)), pl.BlockSpec((1, d), lambda i: (0, 0))],
        out_specs=pl.BlockSpec((TM_MM, d), lambda i: (i, 0)),
        out_shape=jax.ShapeDtypeStruct((s, d), BF16),
        compiler_params=_params("parallel"),
    )(x, g)


def _in_proj_gather(h, shard, order):
    s, d = h.shape
    tn = shard.shape[1]
    n = s // TM_MM
    hr = d // 2

    def body(order_ref, h_ref, shard_ref, proj_ref, wfull_ref, wbuf, send1, recv1, send2, recv2, lsem):
        del order_ref
        j, i = pl.program_id(0), pl.program_id(1)
        x, y, c = _position()
        mine = 2 * x + y
        chips = [(x, 1 - y), (1 - x, y), (1 - x, 1 - y)]

        def theirs(k):
            return 2 * chips[k][0] + chips[k][1]

        def half_rows(half):
            return pl.ds(half * hr, hr)

        def ici(k, shard_index):
            return pltpu.make_async_remote_copy(
                src_ref=shard_ref.at[half_rows(c), :], dst_ref=wfull_ref.at[shard_index, half_rows(c), :],
                send_sem=send1.at[k], recv_sem=recv1.at[k], device_id=(*chips[k], c), device_id_type=MESH)

        def d2d(k, half):
            return pltpu.make_async_remote_copy(
                src_ref=wbuf.at[k % 2, half_rows(half), :], dst_ref=wfull_ref.at[theirs(k), half_rows(half), :],
                send_sem=send2.at[k], recv_sem=recv2.at[k], device_id=(x, y, 1 - c), device_id_type=MESH)

        def load(k, half, sem):
            return pltpu.make_async_copy(wfull_ref.at[theirs(k), half_rows(half), :],
                                         wbuf.at[k % 2, half_rows(half), :], lsem.at[sem])

        own = pltpu.make_async_copy(shard_ref, wfull_ref.at[mine], lsem.at[0])

        @pl.when((j == 0) & (i == 0))
        def _():
            own.start()
            for k in range(3):
                ici(k, mine).start()

        for k in range(3):
            @pl.when((j == k) & (i == n // 2))
            def _(k=k):
                if k == 2:
                    d2d(0, c).wait_send()
                ici(k, theirs(k)).wait_recv()
                load(k, c, 1).start()
                load(k, c, 1).wait()
                d2d(k, c).start()
                d2d(k, 1 - c).wait_recv()
                load(k, 1 - c, 2).start()

            @pl.when((j == k + 1) & (i == 0))
            def _(k=k):
                load(k, 1 - c, 2).wait()

        @pl.when(j == 0)
        def _():
            proj_ref[...] = _nn(h_ref[...], shard_ref[...]).astype(BF16)

        for k in range(3):
            @pl.when(j == k + 1)
            def _(k=k):
                proj_ref[...] = _nn(h_ref[...], wbuf[k % 2]).astype(BF16)

        @pl.when((j == 3) & (i == n - 1))
        def _():
            for k in range(3):
                ici(k, mine).wait_send()
            d2d(1, c).wait_send()
            d2d(2, c).wait_send()
            own.wait()

    return pl.pallas_call(
        body, name="in_proj_gather",
        grid_spec=pltpu.PrefetchScalarGridSpec(
            num_scalar_prefetch=1, grid=(N_CHIPS, n),
            in_specs=[pl.BlockSpec((TM_MM, d), lambda j, i, order: (i, 0)), pl.BlockSpec(memory_space=pltpu.VMEM)],
            out_specs=[pl.BlockSpec((TM_MM, tn), lambda j, i, order: (i, order[j])), ANY],
            scratch_shapes=[pltpu.VMEM((2, d, tn), BF16), pltpu.SemaphoreType.DMA((3,)), pltpu.SemaphoreType.DMA((3,)),
                            pltpu.SemaphoreType.DMA((3,)), pltpu.SemaphoreType.DMA((3,)), pltpu.SemaphoreType.DMA((3,))]),
        out_shape=[jax.ShapeDtypeStruct((s, N_CHIPS * tn), BF16), jax.ShapeDtypeStruct((N_CHIPS, d, tn), BF16)],
        compiler_params=_params("arbitrary", "arbitrary"),
    )(order, h, shard)


def _attn_fwd(diag, proj, shards, kinds, cw8):
    s = proj.shape[0]
    n = s // TQ
    nw = len(shards)
    scale = HEAD_DIM ** -0.5

    def body(*refs):
        diag_ref, q_ref, kp_ref, kc_ref, vp_ref, vc_ref = refs[:6]
        srcs, cw = refs[6:6 + nw], refs[6 + nw]
        o_ref, lse_ref = refs[7 + nw:9 + nw]
        dsts, cw_all = refs[9 + nw:9 + 2 * nw], refs[9 + 2 * nw]
        bias_scr, kcat, vcat = refs[10 + 2 * nw:13 + 2 * nw]
        start, forward, finish = _gather_plan(kinds, srcs, dsts, cw, cw_all, *refs[13 + 2 * nw:])
        i = pl.program_id(0)

        @pl.when(i == 0)
        def _():
            start()
            _build_bias(diag_ref, bias_scr)

        @pl.when(i == n // 2)
        def _():
            forward()

        @pl.when(i == n - 1)
        def _():
            finish()

        kcat[0:TQ, :] = kp_ref[...]
        kcat[TQ:2 * TQ, :] = kc_ref[...]
        vcat[0:TQ, :] = vp_ref[...]
        vcat[TQ:2 * TQ, :] = vc_ref[...]
        lane_hi = lax.broadcasted_iota(jnp.int32, (QB, LANES), 1) >= HEAD_DIM
        col = lax.broadcasted_iota(jnp.int32, (2 * QB, KW), 1)

        def make_block(first_tile):
            def block(b, carry):
                r0 = pl.multiple_of(b * QB, QB)
                for p in range(HEADS // 2):
                    lanes = slice(LANES * p, LANES * (p + 1))
                    q2 = _stack_heads(q_ref[pl.ds(r0, QB), lanes] * scale, lane_hi)
                    kw = kcat[pl.ds(r0, KW), lanes]
                    vw = vcat[pl.ds(r0, KW), lanes]
                    sc = _nt(q2, kw) + bias_scr[p]
                    if first_tile:
                        sc = jnp.where(col >= TQ - r0, sc, NEG_BIG)
                    m = jnp.max(sc, axis=1, keepdims=True)
                    pe = jnp.exp(sc - m)
                    l = jnp.sum(pe, axis=1, keepdims=True)
                    o2 = _nn(pe.astype(BF16), vw) / l
                    lse2 = m + jnp.log(l)
                    lse_ref[pl.ds(r0, QB), 2 * p:2 * p + 1] = lse2[0:QB, :]
                    lse_ref[pl.ds(r0, QB), 2 * p + 1:2 * p + 2] = lse2[QB:2 * QB, :]
                    o_ref[pl.ds(r0, QB), lanes] = jnp.where(lane_hi, o2[QB:2 * QB, :], o2[0:QB, :]).astype(BF16)
                return carry
            return block

        @pl.when(i == 0)
        def _():
            lax.fori_loop(0, TQ // QB, make_block(True), 0)

        @pl.when(i > 0)
        def _():
            lax.fori_loop(0, TQ // QB, make_block(False), 0)

    blk = lambda col_blk, prev: pl.BlockSpec(
        (TQ, D_ATT), (lambda i: (jnp.maximum(i - 1, 0), col_blk)) if prev else (lambda i: (i, col_blk)))
    vmem = pl.BlockSpec(memory_space=pltpu.VMEM)
    return pl.pallas_call(
        body, name="attn_fwd", grid=(n,),
        in_specs=[pl.BlockSpec((HEADS, DIAG), lambda i: (0, 0)),
                  blk(0, False), blk(1, True), blk(1, False), blk(2, True), blk(2, False)] + [vmem] * (nw + 1),
        out_specs=[pl.BlockSpec((TQ, D_ATT), lambda i: (i, 0)), pl.BlockSpec((TQ, HEADS), lambda i: (i, 0))]
        + [ANY] * (nw + 1),
        out_shape=[jax.ShapeDtypeStruct((s, D_ATT), BF16), jax.ShapeDtypeStruct((s, HEADS), F32)]
        + _gather_out_shapes(shards, kinds, cw8),
        scratch_shapes=[pltpu.VMEM((HEADS // 2, 2 * QB, KW), F32), pltpu.VMEM((2 * TQ, D_ATT), BF16),
                        pltpu.VMEM((2 * TQ, D_ATT), BF16)] + _gather_sems(nw),
        compiler_params=_params("arbitrary"),
    )(diag, proj, proj, proj, proj, proj, *shards, cw8)


def _attn_bwd(diag, proj, d_att, att, lse, parts):
    s = proj.shape[0]
    n = s // TQ
    npart = len(parts)
    scale = HEAD_DIM ** -0.5
    rel_pad = 3 * LANES

    def body(*refs):
        diag_ref, q_ref, kp_ref, kc_ref, vp_ref, vc_ref, do_ref, o_ref, lse_ref = refs[:9]
        part_refs = refs[9:9 + npart]
        dqkv_ref, dbias_ref = refs[9 + npart:11 + npart]
        slot_refs = refs[11 + npart:11 + 2 * npart]
        bias_scr, dbias_acc, kcat, vcat, dk_acc, dv_acc, dq_scr = refs[11 + 2 * npart:18 + 2 * npart]
        start, finish = _scatter_plan(part_refs, slot_refs, *refs[18 + 2 * npart:])
        i = pl.program_id(0)

        @pl.when(i == 0)
        def _():
            start()
            _build_bias(diag_ref, bias_scr)
            dbias_acc[...] = jnp.zeros_like(dbias_acc)
            dk_acc[...] = jnp.zeros_like(dk_acc)
            dv_acc[...] = jnp.zeros_like(dv_acc)

        @pl.when(i > 0)
        def _():
            dqkv_ref[:, 0:D_ATT] = dq_scr[...]
            dk_acc[0:TQ, :] = dk_acc[TQ:2 * TQ, :]
            dk_acc[TQ:2 * TQ, :] = jnp.zeros((TQ, D_ATT), F32)
            dv_acc[0:TQ, :] = dv_acc[TQ:2 * TQ, :]
            dv_acc[TQ:2 * TQ, :] = jnp.zeros((TQ, D_ATT), F32)

        @pl.when(i < n)
        def _():
            kcat[0:TQ, :] = kp_ref[...]
            kcat[TQ:2 * TQ, :] = kc_ref[...]
            vcat[0:TQ, :] = vp_ref[...]
            vcat[TQ:2 * TQ, :] = vc_ref[...]

        lane_hi = lax.broadcasted_iota(jnp.int32, (QB, LANES), 1) >= HEAD_DIM
        col = lax.broadcasted_iota(jnp.int32, (2 * QB, KW), 1)

        def make_block(first_tile):
            def block(b, carry):
                r0 = pl.multiple_of(b * QB, QB)
                for p in range(HEADS // 2):
                    lanes = slice(LANES * p, LANES * (p + 1))
                    q2 = _stack_heads(q_ref[pl.ds(r0, QB), lanes] * scale, lane_hi)
                    kw = kcat[pl.ds(r0, KW), lanes]
                    vw = vcat[pl.ds(r0, KW), lanes]
                    dop = do_ref[pl.ds(r0, QB), lanes]
                    do2 = _stack_heads(dop, lane_hi)
                    prod = dop.astype(F32) * o_ref[pl.ds(r0, QB), lanes].astype(F32)
                    delta2 = jnp.concatenate(
                        [jnp.sum(jnp.where(lane_hi, 0.0, prod), axis=1, keepdims=True),
                         jnp.sum(jnp.where(lane_hi, prod, 0.0), axis=1, keepdims=True)], axis=0)
                    lse2 = jnp.concatenate([lse_ref[pl.ds(r0, QB), 2 * p:2 * p + 1],
                                            lse_ref[pl.ds(r0, QB), 2 * p + 1:2 * p + 2]], axis=0)
                    sc = _nt(q2, kw) + bias_scr[p]
                    if first_tile:
                        sc = jnp.where(col >= TQ - r0, sc, NEG_BIG)
                    pr = jnp.exp(sc - lse2)
                    ds = pr * (_nt(do2, vw) - delta2)
                    dbias_acc[p] += ds
                    dsb = ds.astype(BF16)
                    dv_acc[pl.ds(r0, KW), lanes] += _tn(pr.astype(BF16), do2)
                    dk_acc[pl.ds(r0, KW), lanes] += _tn(dsb, q2)
                    dq2 = _nn(dsb, kw)
                    dq = jnp.where(lane_hi, dq2[QB:2 * QB, :], dq2[0:QB, :]) * scale
                    dq_scr[pl.ds(r0, QB), lanes] = dq.astype(BF16)
                return carry
            return block

        @pl.when(i == 0)
        def _():
            lax.fori_loop(0, TQ // QB, make_block(True), 0)

        @pl.when((i > 0) & (i < n))
        def _():
            lax.fori_loop(0, TQ // QB, make_block(False), 0)

        @pl.when(i > 0)
        def _():
            dqkv_ref[:, D_ATT:2 * D_ATT] = dk_acc[0:TQ, :].astype(BF16)
            dqkv_ref[:, 2 * D_ATT:3 * D_ATT] = dv_acc[0:TQ, :].astype(BF16)

        @pl.when(i == n)
        def _():
            d_iota = lax.broadcasted_iota(jnp.int32, (DIAG, rel_pad), 0)
            n_iota = lax.broadcasted_iota(jnp.int32, (DIAG, rel_pad), 1)
            diff = jnp.where(d_iota < KW, d_iota, d_iota - DIAG)
            idx = jnp.clip(N_LEFT * CHUNK - diff, -MAX_REL, MAX_REL) + MAX_REL
            onehot = (idx == n_iota).astype(F32)
            rows = []
            for hd in range(HEADS):
                acc = dbias_acc[hd // 2, (hd % 2) * QB:(hd % 2 + 1) * QB, :]
                a = jnp.concatenate([acc, jnp.zeros((QB, DIAG - KW), F32)], axis=1)
                g8 = a[0:SUBLANES, :]
                for blk in range(1, QB // SUBLANES):
                    g8 = g8 + pltpu.roll(a[blk * SUBLANES:(blk + 1) * SUBLANES, :], DIAG - blk * SUBLANES, 1)
                g1 = g8[0:1, :]
                for r in range(1, SUBLANES):
                    g1 = g1 + pltpu.roll(g8[r:r + 1, :], DIAG - r, 1)
                rows.append(g1)
            g = jnp.concatenate(rows, axis=0)
            dbias_ref[...] = jnp.dot(g, onehot, preferred_element_type=F32, precision=lax.Precision.HIGHEST)
            finish()

    last = n - 1
    cur = lambda col_blk: pl.BlockSpec((TQ, D_ATT), lambda i: (jnp.minimum(i, last), col_blk))
    prev = lambda col_blk: pl.BlockSpec((TQ, D_ATT), lambda i: (jnp.maximum(jnp.minimum(i, last) - 1, 0), col_blk))
    return pl.pallas_call(
        body, name="attn_bwd", grid=(n + 1,),
        in_specs=[pl.BlockSpec((HEADS, DIAG), lambda i: (0, 0)),
                  cur(0), prev(1), cur(1), prev(2), cur(2), cur(0), cur(0),
                  pl.BlockSpec((TQ, HEADS), lambda i: (jnp.minimum(i, last), 0))] + [ANY] * npart,
        out_specs=[pl.BlockSpec((TQ, 3 * D_ATT), lambda i: (jnp.maximum(i - 1, 0), 0)),
                   pl.BlockSpec((HEADS, rel_pad), lambda i: (0, 0))] + [ANY] * npart,
        out_shape=[jax.ShapeDtypeStruct((s, 3 * D_ATT), BF16), jax.ShapeDtypeStruct((HEADS, rel_pad), F32)]
        + _scatter_out_shapes(parts),
        scratch_shapes=[pltpu.VMEM((HEADS // 2, 2 * QB, KW), F32), pltpu.VMEM((HEADS // 2, 2 * QB, KW), F32),
                        pltpu.VMEM((2 * TQ, D_ATT), BF16), pltpu.VMEM((2 * TQ, D_ATT), BF16),
                        pltpu.VMEM((2 * TQ, D_ATT), F32), pltpu.VMEM((2 * TQ, D_ATT), F32),
                        pltpu.VMEM((TQ, D_ATT), BF16)] + _scatter_sems(npart),
        compiler_params=_params("arbitrary"),
    )(diag, proj, proj, proj, proj, proj, d_att, att, lse, *parts)


def _shift_down(a, k, halo):
    rolled = pltpu.roll(a, k, 0)
    row = lax.broadcasted_iota(jnp.int32, halo.shape, 0)
    first = jnp.where(row < k, pltpu.roll(halo, k, 0), rolled[0:SUBLANES, :])
    return jnp.concatenate([first, rolled[SUBLANES:, :]], axis=0)


def _shift_up(a, k, nxt):
    tm = a.shape[0]
    rolled = pltpu.roll(a, tm - k, 0)
    row = lax.broadcasted_iota(jnp.int32, nxt.shape, 0)
    last = jnp.where(row >= SUBLANES - k, pltpu.roll(nxt, SUBLANES - k, 0), rolled[tm - SUBLANES:, :])
    return jnp.concatenate([rolled[:tm - SUBLANES, :], last], axis=0)


def _sigmoid(v):
    return 0.5 * jnp.tanh(0.5 * v) + 0.5


def _mixer_mid(att, proj, x, tgt, w_att, w_conv, w_out, conv_w8, conv_b, fin_g):
    s, d = x.shape
    dc = D_ATT
    n = s // TM_MID
    tm = TM_MID
    n_shards = 4

    def body(att_ref, za_ref, gb_ref, gc_ref, u_ref, zc_ref, hgc_ref, hu_ref, gatt_ref, gconv_ref, x_ref, t_ref,
             watt_ref, wconv_ref, wout_ref, cw_ref, cb_ref, fg_ref,
             dpb_ref, do_ref, dx2_ref, gatt_o, gconv_o, gout_o, loss_o, gfn_o, gcb_o, gcw_o,
             acc_att, acc_conv, acc_out, carry):
        i = pl.program_id(0)
        tile = n - 1 - i

        @pl.when(i == 0)
        def _():
            acc_att[...] = jnp.zeros_like(acc_att)
            acc_conv[...] = jnp.zeros_like(acc_conv)
            acc_out[...] = jnp.zeros_like(acc_out)
            carry[...] = jnp.zeros_like(carry)
            loss_o[...] = jnp.zeros_like(loss_o)
            gfn_o[...] = jnp.zeros_like(gfn_o)
            gcb_o[...] = jnp.zeros_like(gcb_o)
            gcw_o[...] = jnp.zeros_like(gcw_o)

        att_v = att_ref[...].astype(F32)
        za = za_ref[...].astype(F32)
        sa = _sigmoid(za)
        silu_a = za * sa
        a_b = (att_v * silu_a).astype(BF16)

        gb = gb_ref[...].astype(F32)
        gc = gc_ref[...].astype(F32)
        u = u_ref[...].astype(F32)
        zc = zc_ref[...].astype(F32)
        cu = gc * u
        halo = jnp.where(tile > 0, hgc_ref[...].astype(F32) * hu_ref[...].astype(F32), 0.0)
        cu1 = _shift_down(cu, 1, halo)
        cu2 = _shift_down(cu, 2, halo)
        w0, w1, w2 = cw_ref[0:1, :], cw_ref[1:2, :], cw_ref[2:3, :]
        vconv = w0 * cu2 + w1 * cu1 + w2 * cu + cb_ref[...]
        sc = _sigmoid(zc)
        silu_c = zc * sc
        c_b = (gb * vconv * silu_c).astype(BF16)

        y_att = _nn(a_b, watt_ref[...])
        y_conv = _nn(c_b, wconv_ref[...])
        ga = _sigmoid(gatt_ref[...].astype(F32))
        gv = _sigmoid(gconv_ref[...].astype(F32))
        m_b = (ga * y_att + gv * y_conv).astype(BF16)
        x2 = x_ref[...] + _nn(m_b, wout_ref[...])
        r2 = lax.rsqrt(jnp.mean(x2 * x2, axis=-1, keepdims=True) + EPS)
        x2n = x2 * r2
        fg = fg_ref[...]
        err = x2n * fg - t_ref[...]
        loss_o[...] += jnp.sum(err * err, axis=0, keepdims=True) * (0.5 / d)
        dy = err * (1.0 / d)
        gfn_o[...] += jnp.sum(dy * x2n, axis=0, keepdims=True)
        dyn = dy * fg
        dx2 = r2 * (dyn - x2n * jnp.mean(dyn * x2n, axis=-1, keepdims=True))
        dx2_ref[...] = dx2
        dx2_b = dx2.astype(BF16)

        dm = _nt(dx2_b, wout_ref[...])
        acc_out[...] += _tn(m_b, dx2_b)
        dy_att = dm * ga
        dy_conv = dm * gv
        dpb_ref[:, 5 * dc:5 * dc + d] = (dy_att * y_att * (1.0 - ga)).astype(BF16)
        dpb_ref[:, 5 * dc + d:5 * dc + 2 * d] = (dy_conv * y_conv * (1.0 - gv)).astype(BF16)
        dya_b = dy_att.astype(BF16)
        dyc_b = dy_conv.astype(BF16)
        da_in = _nt(dya_b, watt_ref[...])
        acc_att[...] += _tn(a_b, dya_b)
        dc_in = _nt(dyc_b, wconv_ref[...])
        acc_conv[...] += _tn(c_b, dyc_b)

        do_ref[...] = (da_in * silu_a).astype(BF16)
        dpb_ref[:, 0:dc] = (da_in * att_v * (sa * (1.0 + za * (1.0 - sa)))).astype(BF16)
        dpb_ref[:, dc:2 * dc] = (dc_in * vconv * silu_c).astype(BF16)
        dgs = dc_in * gb
        dvc = dgs * silu_c
        dpb_ref[:, 4 * dc:5 * dc] = (dgs * vconv * (sc * (1.0 + zc * (1.0 - sc)))).astype(BF16)
        gcb_o[...] += jnp.sum(dvc, axis=0, keepdims=True)
        gcw_o[0:1, :] += jnp.sum(dvc * cu2, axis=0, keepdims=True)
        gcw_o[1:2, :] += jnp.sum(dvc * cu1, axis=0, keepdims=True)
        gcw_o[2:3, :] += jnp.sum(dvc * cu, axis=0, keepdims=True)
        nxt = carry[...]
        dcu = w2 * dvc + w1 * _shift_up(dvc, 1, nxt) + w0 * _shift_up(dvc, 2, nxt)
        carry[...] = dvc[0:SUBLANES, :]
        dpb_ref[:, 2 * dc:3 * dc] = (dcu * u).astype(BF16)
        dpb_ref[:, 3 * dc:4 * dc] = (dcu * gc).astype(BF16)

        @pl.when(i == n - 1)
        def _():
            for j in range(n_shards):
                gatt_o[j] = acc_att[:, j * (d // n_shards):(j + 1) * (d // n_shards)].astype(BF16)
                gconv_o[j] = acc_conv[:, j * (d // n_shards):(j + 1) * (d // n_shards)].astype(BF16)
                gout_o[j] = acc_out[j * (d // n_shards):(j + 1) * (d // n_shards), :].astype(BF16)

    rev = lambda width, col_blk: pl.BlockSpec((tm, width), lambda i: (n - 1 - i, col_blk))
    halo_spec = lambda col_blk: pl.BlockSpec(
        (SUBLANES, dc), lambda i: (jnp.maximum((n - 1 - i) * (tm // SUBLANES) - 1, 0), col_blk))
    const = lambda shape: pl.BlockSpec(shape, lambda i: tuple(0 for _ in shape), pipeline_mode=pl.Buffered(1))
    q4 = d // n_shards
    return pl.pallas_call(
        body, name="mixer_mid", grid=(n,),
        in_specs=[rev(dc, 0), rev(dc, 3), rev(dc, 4), rev(dc, 5), rev(dc, 6), rev(dc, 7),
                  halo_spec(5), halo_spec(6), rev(d, 4), rev(d, 5), rev(d, 0), rev(d, 0),
                  const((dc, d)), const((dc, d)), const((d, d)), const((SUBLANES, dc)), const((1, dc)), const((1, d))],
        out_specs=[rev(5 * dc + 2 * d, 0), rev(dc, 0), rev(d, 0),
                   const((n_shards, dc, q4)), const((n_shards, dc, q4)), const((n_shards, q4, d)),
                   const((1, d)), const((1, d)), const((1, dc)), const((SUBLANES, dc))],
        out_shape=[jax.ShapeDtypeStruct((s, 5 * dc + 2 * d), BF16), jax.ShapeDtypeStruct((s, dc), BF16),
                   jax.ShapeDtypeStruct((s, d), F32),
                   jax.ShapeDtypeStruct((n_shards, dc, q4), BF16), jax.ShapeDtypeStruct((n_shards, dc, q4), BF16),
                   jax.ShapeDtypeStruct((n_shards, q4, d), BF16),
                   jax.ShapeDtypeStruct((1, d), F32), jax.ShapeDtypeStruct((1, d), F32),
                   jax.ShapeDtypeStruct((1, dc), F32), jax.ShapeDtypeStruct((SUBLANES, dc), F32)],
        scratch_shapes=[pltpu.VMEM((dc, d), F32), pltpu.VMEM((dc, d), F32), pltpu.VMEM((d, d), F32),
                        pltpu.VMEM((SUBLANES, dc), F32)],
        compiler_params=_params("arbitrary"),
    )(att, proj, proj, proj, proj, proj, proj, proj, proj, proj, x, tgt,
      w_att, w_conv, w_out, conv_w8, conv_b, fin_g)


def _in_proj_bwd_x(dqkv, dpb, w_in, x, dx2, g):
    s, d = x.shape
    tn = dqkv.shape[1]
    nb = dpb.shape[1] // tn
    n = s // TM_MM

    def body(*refs):
        dps, ws = refs[:nb + 1], refs[nb + 1:2 * nb + 2]
        x_ref, dx2_ref, g_ref, gx_ref, gng_ref = refs[2 * nb + 2:]
        i = pl.program_id(0)

        @pl.when(i == 0)
        def _():
            gng_ref[...] = jnp.zeros_like(gng_ref)

        dh = _nt(dps[0][...], ws[0][0])
        for j in range(1, nb + 1):
            dh = dh + _nt(dps[j][...], ws[j][0])
        xv = x_ref[...]
        r = lax.rsqrt(jnp.mean(xv * xv, axis=-1, keepdims=True) + EPS)
        xn = xv * r
        gng_ref[...] += jnp.sum(dh * xn, axis=0, keepdims=True)
        dhn = dh * g_ref[...]
        gx_ref[...] = dx2_ref[...] + r * (dhn - xn * jnp.mean(dhn * xn, axis=-1, keepdims=True))

    tile = lambda width, col_blk: pl.BlockSpec((TM_MM, width), lambda i: (i, col_blk))
    wspec = lambda blk: pl.BlockSpec((1, d, tn), lambda i: (blk, 0, 0), pipeline_mode=pl.Buffered(1))
    return pl.pallas_call(
        body, name="in_proj_bwd_x", grid=(n,),
        in_specs=[tile(tn, 0)] + [tile(tn, j) for j in range(nb)] + [wspec(j) for j in range(nb + 1)]
        + [tile(d, 0), tile(d, 0), pl.BlockSpec((1, d), lambda i: (0, 0))],
        out_specs=[tile(d, 0), pl.BlockSpec((1, d), lambda i: (0, 0))],
        out_shape=[jax.ShapeDtypeStruct((s, d), F32), jax.ShapeDtypeStruct((1, d), F32)],
        compiler_params=_params("arbitrary"),
    )(dqkv, *([dpb] * nb), *([w_in] * (nb + 1)), x, dx2, g)


def _in_proj_bwd_w(h, dqkv, dpb, order):
    s, d = h.shape
    tn = dqkv.shape[1]
    n = s // TM_MM
    hr = d // 2

    def body(order_ref, h_ref, da_ref, db_ref, slots_ref, acc, sendbuf, send, recv, lsem):
        j, i = pl.program_id(0), pl.program_id(1)
        blk = order_ref[j]
        pos = _position()
        c = pos[2]

        @pl.when(i == 0)
        def _():
            acc[...] = jnp.zeros_like(acc)

        @pl.when(blk == 0)
        def _():
            acc[...] += _tn(h_ref[...], da_ref[...])

        @pl.when(blk > 0)
        def _():
            acc[...] += _tn(h_ref[...], db_ref[...])

        def remote(step, kc):
            flip = (step + 1) % N_CHIPS
            k = 4 * (flip >> 1) + 2 * (flip & 1) + kc
            target_c = (1 - c) if kc else c
            return pltpu.make_async_remote_copy(
                src_ref=sendbuf.at[step, pl.ds(target_c * hr, hr), :], dst_ref=slots_ref.at[k],
                send_sem=send.at[k], recv_sem=recv.at[k], device_id=_peer(pos, k), device_id_type=MESH)

        local = pltpu.make_async_copy(sendbuf.at[N_CHIPS - 1, pl.ds(c * hr, hr), :], slots_ref.at[0], lsem.at[0])
        copies = [(step, kc) for step in range(N_CHIPS) for kc in range(2) if (step, kc) != (N_CHIPS - 1, 0)]

        for step in range(N_CHIPS):
            @pl.when((j == step) & (i == n - 1))
            def _(step=step):
                sendbuf[step] = acc[...].astype(BF16)
                for kc in range(2):
                    if (step, kc) == (N_CHIPS - 1, 0):
                        local.start()
                    else:
                        remote(step, kc).start()

        @pl.when((j == N_CHIPS - 1) & (i == n - 1))
        def _():
            for step, kc in copies:
                remote(step, kc).wait_recv()
            for step, kc in copies:
                remote(step, kc).wait_send()
            local.wait()

    return pl.pallas_call(
        body, name="in_proj_bwd_w",
        grid_spec=pltpu.PrefetchScalarGridSpec(
            num_scalar_prefetch=1, grid=(N_CHIPS, n),
            in_specs=[pl.BlockSpec((TM_MM, d), lambda j, i, order: (i, 0)),
                      pl.BlockSpec((TM_MM, tn), lambda j, i, order: (jnp.where(order[j] == 0, i, 0), 0)),
                      pl.BlockSpec((TM_MM, tn), lambda j, i, order: (jnp.where(order[j] == 0, 0, i),
                                                                     jnp.maximum(order[j] - 1, 0)))],
            out_specs=[ANY],
            scratch_shapes=[pltpu.VMEM((d, tn), F32), pltpu.VMEM((N_CHIPS, d, tn), BF16),
                            pltpu.SemaphoreType.DMA((N_DEV,)), pltpu.SemaphoreType.DMA((N_DEV,)),
                            pltpu.SemaphoreType.DMA((1,))]),
        out_shape=[jax.ShapeDtypeStruct((N_DEV, hr, tn), BF16)],
        compiler_params=_params("arbitrary", "arbitrary"),
    )(order, h, dqkv, dpb)[0]


def _sum_slots(recv, name, rows_per_step):
    k, r, c = recv.shape

    def body(r_ref, o_ref):
        total = r_ref[0].astype(F32)
        for slot in range(1, k):
            total = total + r_ref[slot].astype(F32)
        o_ref[...] = total

    return pl.pallas_call(
        body, name=name, grid=(r // rows_per_step,),
        in_specs=[pl.BlockSpec((k, rows_per_step, c), lambda i: (0, i, 0))],
        out_specs=pl.BlockSpec((rows_per_step, c), lambda i: (i, 0)),
        out_shape=jax.ShapeDtypeStruct((r, c), F32),
        compiler_params=_params("parallel"),
    )(recv)


LOSS_ROW = 6


def _sum_small(recv):
    k = recv.shape[0]

    def body(r_ref, o_ref):
        total = r_ref[0]
        for slot in range(1, k):
            total = total + r_ref[slot]
        o_ref[...] = total
        loss = jnp.sum(total[LOSS_ROW:LOSS_ROW + 1, :], axis=1, keepdims=True)
        o_ref[LOSS_ROW:LOSS_ROW + 1, :] = jnp.broadcast_to(loss, (1, SMALL_COLS))

    return pl.pallas_call(
        body, name="reduce_sum_small",
        out_shape=jax.ShapeDtypeStruct((SMALL_ROWS, SMALL_COLS), F32),
    )(recv)


def _adamw(w, g, m, v, name, rows_per_step):
    r, c = w.shape
    c1 = 1.0 / (1.0 - ADAM_B1 ** ADAM_STEP)
    c2 = 1.0 / (1.0 - ADAM_B2 ** ADAM_STEP)

    def body(w_ref, g_ref, m_ref, v_ref, d_ref, mo_ref, vo_ref):
        gv = g_ref[...]
        m2 = ADAM_B1 * m_ref[...] + (1.0 - ADAM_B1) * gv
        v2 = ADAM_B2 * v_ref[...] + (1.0 - ADAM_B2) * (gv * gv)
        mo_ref[...] = m2
        vo_ref[...] = v2
        d_ref[...] = -ADAM_LR * ((m2 * c1) / (jnp.sqrt(v2 * c2) + ADAM_EPS) + ADAM_WD * w_ref[...])

    spec = pl.BlockSpec((rows_per_step, c), lambda i: (i, 0))
    shape = jax.ShapeDtypeStruct((r, c), F32)
    return pl.pallas_call(
        body, name=name, grid=(r // rows_per_step,),
        in_specs=[spec] * 4, out_specs=[spec] * 3, out_shape=[shape] * 3,
        compiler_params=_params("parallel"),
    )(w, g, m, v)


ANY = pl.BlockSpec(memory_space=pl.ANY)
N_CHIPS = 4
N_DEV = 8


def _position():
    return lax.axis_index("x"), lax.axis_index("y"), lax.axis_index("c")


def _gather_out_shapes(shards, kinds, cw8):
    full = [(a.shape[0], a.shape[1] * N_CHIPS) if k == "cols" else (a.shape[0] * N_CHIPS, a.shape[1])
            for a, k in zip(shards, kinds)]
    return [jax.ShapeDtypeStruct(f, a.dtype) for f, a in zip(full, shards)] + [
        jax.ShapeDtypeStruct((N_CHIPS,) + cw8.shape, cw8.dtype)]


def _gather_sems(nw):
    return [pltpu.SemaphoreType.DMA((3, nw)), pltpu.SemaphoreType.DMA((3, nw)),
            pltpu.SemaphoreType.DMA((3, nw)), pltpu.SemaphoreType.DMA((3, nw)),
            pltpu.SemaphoreType.DMA((3,)), pltpu.SemaphoreType.DMA((3,)), pltpu.SemaphoreType.DMA((nw + 1,))]


def _gather_plan(kinds, srcs, dsts, cw, cw_all, send1, recv1, send2, recv2, ssend, srecv, lsem):
    nw = len(srcs)
    x, y, c = _position()
    mine = 2 * x + y
    chips = [(x, 1 - y), (1 - x, y), (1 - x, 1 - y)]

    def window(w, shard, half):
        r, cc = srcs[w].shape
        hr = r // 2
        if kinds[w] == "cols":
            rows = pl.ds(0, r) if half is None else pl.ds(half * hr, hr)
            return dsts[w].at[rows, pl.ds(shard * cc, cc)]
        rows = pl.ds(shard * r, r) if half is None else pl.ds(shard * r + half * hr, hr)
        return dsts[w].at[rows, :]

    def my_half(w):
        hr = srcs[w].shape[0] // 2
        return srcs[w].at[pl.ds(c * hr, hr), :]

    def local():
        return [pltpu.make_async_copy(srcs[w], window(w, mine, None), lsem.at[w]) for w in range(nw)] + [
            pltpu.make_async_copy(cw, cw_all.at[mine], lsem.at[nw])]

    def ici(k, w, shard):
        kx, ky = chips[k]
        return pltpu.make_async_remote_copy(
            src_ref=my_half(w), dst_ref=window(w, shard, c), send_sem=send1.at[k, w], recv_sem=recv1.at[k, w],
            device_id=(kx, ky, c), device_id_type=MESH)

    def d2d(k, w, shard, half):
        return pltpu.make_async_remote_copy(
            src_ref=window(w, shard, half), dst_ref=window(w, shard, half),
            send_sem=send2.at[k, w], recv_sem=recv2.at[k, w], device_id=(x, y, 1 - c), device_id_type=MESH)

    def small(k, shard):
        kx, ky = chips[k]
        return pltpu.make_async_remote_copy(
            src_ref=cw, dst_ref=cw_all.at[shard], send_sem=ssend.at[k], recv_sem=srecv.at[k],
            device_id=(kx, ky, c), device_id_type=MESH)

    def theirs(k):
        kx, ky = chips[k]
        return 2 * kx + ky

    def start():
        for cp in local():
            cp.start()
        for k in range(3):
            for w in range(nw):
                ici(k, w, mine).start()
            small(k, mine).start()

    def forward():
        for k in range(3):
            for w in range(nw):
                ici(k, w, theirs(k)).wait_recv()
                d2d(k, w, theirs(k), c).start()

    def finish():
        for k in range(3):
            for w in range(nw):
                d2d(k, w, theirs(k), 1 - c).wait_recv()
            small(k, theirs(k)).wait_recv()
        for k in range(3):
            for w in range(nw):
                ici(k, w, mine).wait_send()
                d2d(k, w, theirs(k), c).wait_send()
            small(k, mine).wait_send()
        for cp in local():
            cp.wait()

    return start, forward, finish


def _scatter_out_shapes(parts):
    return [jax.ShapeDtypeStruct((N_DEV, p.shape[1] // 2, p.shape[2]), p.dtype) for p in parts]


def _scatter_sems(nw):
    return [pltpu.SemaphoreType.DMA((N_DEV, nw)), pltpu.SemaphoreType.DMA((N_DEV, nw)), pltpu.SemaphoreType.DMA((nw,))]


def _peer(pos, k):
    x, y, c = pos
    return ((1 - x) if k & 4 else x, (1 - y) if k & 2 else y, (1 - c) if k & 1 else c)


def _scatter_plan(srcs, dsts, send, recv, lsem):
    nw = len(srcs)
    pos = _position()

    def piece(w, k):
        px, py, pc = _peer(pos, k)
        hr = srcs[w].shape[1] // 2
        return srcs[w].at[2 * px + py, pl.ds(pc * hr, hr), :]

    def remote(w, k):
        return pltpu.make_async_remote_copy(
            src_ref=piece(w, k), dst_ref=dsts[w].at[k], send_sem=send.at[k, w], recv_sem=recv.at[k, w],
            device_id=_peer(pos, k), device_id_type=MESH)

    def local(w):
        return pltpu.make_async_copy(piece(w, 0), dsts[w].at[0], lsem.at[w])

    def start():
        for w in range(nw):
            local(w).start()
        for k in range(1, N_DEV):
            for w in range(nw):
                remote(w, k).start()

    def finish():
        for k in range(1, N_DEV):
            for w in range(nw):
                remote(w, k).wait_recv()
        for k in range(1, N_DEV):
            for w in range(nw):
                remote(w, k).wait_send()
        for w in range(nw):
            local(w).wait()

    return start, finish


def _reduce_pair(halves, small):
    nw = len(halves)

    def body(*refs):
        srcs, sm = refs[:nw], refs[nw]
        dsts, sm_all = refs[nw + 1:2 * nw + 1], refs[2 * nw + 1]
        send, recv, ssend, srecv, lsem = refs[2 * nw + 2:]
        pos = _position()
        x, y, c = pos
        me = 4 * x + 2 * y + c

        def rows(w, half):
            hr = halves[w].shape[0]
            return dsts[w].at[pl.ds(half * hr, hr), :]

        def remote(w, half):
            return pltpu.make_async_remote_copy(
                src_ref=srcs[w], dst_ref=rows(w, half), send_sem=send.at[w], recv_sem=recv.at[w],
                device_id=(x, y, 1 - c), device_id_type=MESH)

        def bcast(k, slot):
            return pltpu.make_async_remote_copy(
                src_ref=sm, dst_ref=sm_all.at[slot], send_sem=ssend.at[k], recv_sem=srecv.at[k],
                device_id=_peer(pos, k), device_id_type=MESH)

        local = [pltpu.make_async_copy(srcs[w], rows(w, c), lsem.at[w]) for w in range(nw)]
        local.append(pltpu.make_async_copy(sm, sm_all.at[me], lsem.at[nw]))
        sent = [remote(w, c) for w in range(nw)] + [bcast(k, me) for k in range(1, N_DEV)]
        for cp in sent + local:
            cp.start()
        for w in range(nw):
            remote(w, 1 - c).wait_recv()
        for k in range(1, N_DEV):
            px, py, pc = _peer(pos, k)
            bcast(k, 4 * px + 2 * py + pc).wait_recv()
        for cp in sent:
            cp.wait_send()
        for cp in local:
            cp.wait()

    vmem = pl.BlockSpec(memory_space=pltpu.VMEM)
    return pl.pallas_call(
        body, name="reduce_pair",
        in_specs=[vmem] * (nw + 1), out_specs=[ANY] * (nw + 1),
        out_shape=[jax.ShapeDtypeStruct((2 * hv.shape[0], hv.shape[1]), hv.dtype) for hv in halves]
        + [jax.ShapeDtypeStruct((N_DEV,) + small.shape, small.dtype)],
        scratch_shapes=[pltpu.SemaphoreType.DMA((nw,)), pltpu.SemaphoreType.DMA((nw,)),
                        pltpu.SemaphoreType.DMA((N_DEV,)), pltpu.SemaphoreType.DMA((N_DEV,)),
                        pltpu.SemaphoreType.DMA((nw + 1,))],
        compiler_params=pltpu.CompilerParams(vmem_limit_bytes=VMEM_LIMIT),
    )(*halves, small)


def _pad_to(a, rows, cols):
    return jnp.pad(a, ((0, rows - a.shape[0]), (0, cols - a.shape[1])))


def _pack_small(norm_g, fin_g, conv_b, conv_w, loss_vec, rel):
    rows = [_pad_to(norm_g, 1, SMALL_COLS), _pad_to(fin_g, 1, SMALL_COLS), _pad_to(conv_b, 1, SMALL_COLS),
            _pad_to(conv_w, 3, SMALL_COLS), _pad_to(loss_vec, 2, SMALL_COLS), _pad_to(rel, HEADS, SMALL_COLS)]
    return jnp.concatenate(rows, axis=0)


def kernel(x, norm_g, w_in, rel_bias, w_att_out, conv_w, conv_b, w_conv_out, w_out, final_norm_g, loss_target, m_norm_g, m_w_in, m_rel_bias, m_w_att_out, m_conv_w, m_conv_b, m_w_conv_out, m_w_out, m_final_norm_g, v_norm_g, v_w_in, v_rel_bias, v_w_att_out, v_conv_w, v_conv_b, v_w_conv_out, v_w_out, v_final_norm_g):
    xs, tgt = x[0], loss_target[0]
    cshard = conv_w.shape[2]
    chip = 2 * lax.axis_index("x") + lax.axis_index("y")

    shards = [w_in[0].astype(BF16), w_att_out[0].astype(BF16), w_conv_out[0].astype(BF16), w_out[0].astype(BF16)]
    cw8 = _pad_to(conv_w[0], SUBLANES, cshard)
    flips = jnp.arange(N_CHIPS, dtype=jnp.int32)
    own_first = jnp.bitwise_xor(chip, flips)
    own_last = jnp.bitwise_xor(chip, (flips + 1) % N_CHIPS)

    h = _rmsnorm_fwd(xs, norm_g)
    proj, wb_in = _in_proj_gather(h, shards[0], own_first)
    diag = jnp.take(rel_bias[0], _diag_rel_index(), axis=1)
    att, lse, wb_att, wb_conv, wb_out, cw_all = _attn_fwd(diag, proj, shards[1:], ["cols", "cols", "rows"], cw8)
    conv_w_full = jnp.transpose(cw_all, (1, 0, 2)).reshape(SUBLANES, N_CHIPS * cshard)
    (dpb, d_att, dx2, g_att_p, g_conv_p, g_out_p, loss_vec, g_fin, g_cb, g_cw) = _mixer_mid(
        att, proj, xs, tgt, wb_att, wb_conv, wb_out, conv_w_full, conv_b, final_norm_g[None, :])
    dqkv, g_rel, r_att, r_conv, r_out = _attn_bwd(diag, proj, d_att, att, lse, [g_att_p, g_conv_p, g_out_p])
    grad_x, g_norm = _in_proj_bwd_x(dqkv, dpb, wb_in, xs, dx2, norm_g)
    r_in = _in_proj_bwd_w(h, dqkv, dpb, own_last)

    small = _pack_small(g_norm, g_fin, g_cb, g_cw[0:3], loss_vec, g_rel)
    halves = [_sum_slots(r_in, "reduce_sum_in", 64), _sum_slots(r_att, "reduce_sum_att", r_att.shape[1]),
              _sum_slots(r_conv, "reduce_sum_conv", r_conv.shape[1]), _sum_slots(r_out, "reduce_sum_out", r_out.shape[1])]
    gw_in, gw_att, gw_conv, gw_out, r_small = _reduce_pair(halves, small)
    gs = _sum_small(r_small)

    g_cw_mine = lax.dynamic_slice(gs, (3, chip * cshard), (3, cshard))
    zero2 = jnp.zeros((2, 1), F32)
    pack = lambda ng, fg, cb, cw, rb: _pack_small(ng, fg[None, :], cb, cw[0], zero2, rb[0])
    gs_mine = _pack_small(gs[0:1], gs[1:2], gs[2:3, :conv_b.shape[1]], g_cw_mine, zero2, gs[8:16, :N_REL])
    ds, ms, vs = _adamw(pack(norm_g, final_norm_g, conv_b, conv_w, rel_bias), gs_mine,
                        pack(m_norm_g, m_final_norm_g, m_conv_b, m_conv_w, m_rel_bias),
                        pack(v_norm_g, v_final_norm_g, v_conv_b, v_conv_w, v_rel_bias), "adamw_small", SMALL_ROWS)

    def unpack(a):
        return {"norm_g": a[0:1], "final_norm_g": a[1], "conv_b": a[2:3, :conv_b.shape[1]],
                "conv_w": a[3:6, :cshard][None], "rel_bias": a[8:16, :N_REL][None]}

    g_small, d_small, m_small, v_small = unpack(gs_mine), unpack(ds), unpack(ms), unpack(vs)

    big = {}
    for name, w, g, m, v, rows in (("w_in", w_in, gw_in, m_w_in, v_w_in, 128),
                                   ("w_att_out", w_att_out, gw_att, m_w_att_out, v_w_att_out, 256),
                                   ("w_conv_out", w_conv_out, gw_conv, m_w_conv_out, v_w_conv_out, 256),
                                   ("w_out", w_out, gw_out, m_w_out, v_w_out, 128)):
        dw, mw, vw = _adamw(w[0], g, m[0], v[0], "adamw_" + name, rows)
        big[name] = (g[None], dw[None], mw[None], vw[None])

    order = ["norm_g", "w_in", "rel_bias", "w_att_out", "conv_w", "conv_b", "w_conv_out", "w_out", "final_norm_g"]
    outs = [gs[LOSS_ROW, 0], grad_x[None]]
    for which, small_d in enumerate((g_small, d_small, m_small, v_small)):
        for name in order:
            outs.append(big[name][which] if name in big else small_d[name])
    return tuple(outs)
```

```python
import numpy as np
import jax
import jax.numpy as jnp
from jax import lax
from jax.experimental import pallas as pl
from jax.experimental.pallas import tpu as pltpu

F32 = jnp.float32
BF16 = jnp.bfloat16
MESH = pl.DeviceIdType.MESH

CHUNK = 64
N_LEFT = 8
HEADS = 8
HEAD_DIM = 64
D_ATT = HEADS * HEAD_DIM
MAX_REL = 128
N_REL = 2 * MAX_REL + 1
EPS = 1e-6
NEG_BIG = -1e30
ADAM_LR, ADAM_B1, ADAM_B2, ADAM_EPS, ADAM_WD, ADAM_STEP = 0.001, 0.9, 0.999, 1e-08, 0.01, 10

LANES = 128
SUBLANES = 8
VMEM_LIMIT = 56 * 1024 * 1024

QB = 2 * CHUNK
KW = N_LEFT * CHUNK + QB
DIAG = KW + QB
TQ = N_LEFT * CHUNK
TM_MID = 256
TM_MM = 512
SMALL_ROWS, SMALL_COLS = 16, 1024


def _params(*sem):
    return pltpu.CompilerParams(dimension_semantics=sem, vmem_limit_bytes=VMEM_LIMIT)


def _nt(a, b):
    return lax.dot_general(a, b, (((1,), (1,)), ((), ())), preferred_element_type=F32)


def _tn(a, b):
    return lax.dot_general(a, b, (((0,), (0,)), ((), ())), preferred_element_type=F32)


def _nn(a, b):
    return jnp.dot(a, b, preferred_element_type=F32)


def _diag_rel_index():
    d = np.arange(DIAG)
    diff = np.where(d < KW, d, d - DIAG)
    rel = N_LEFT * CHUNK - diff
    return np.clip(rel, -MAX_REL, MAX_REL) + MAX_REL


def _build_bias(diag_ref, bias_scr):
    r = lax.broadcasted_iota(jnp.int32, (QB, KW), 0) // CHUNK
    s = lax.broadcasted_iota(jnp.int32, (QB, KW), 1) // CHUNK
    allowed = (s >= r) & (s <= r + N_LEFT)
    for h in range(HEADS):
        row = jnp.broadcast_to(diag_ref[h:h + 1, :], (QB, DIAG))
        t = pltpu.roll(row, 0, 1, stride=1, stride_axis=0)
        bias_scr[h // 2, (h % 2) * QB:(h % 2 + 1) * QB, :] = jnp.where(allowed, t[:, :KW], NEG_BIG)


def _stack_heads(a, lane_hi):
    zero = jnp.zeros_like(a)
    return jnp.concatenate([jnp.where(lane_hi, zero, a), jnp.where(lane_hi, a, zero)], axis=0)


def _rmsnorm_fwd(x, g):
    s, d = x.shape

    def body(x_ref, g_ref, h_ref):
        xv = x_ref[...]
        r = lax.rsqrt(jnp.mean(xv * xv, axis=-1, keepdims=True) + EPS)
        h_ref[...] = ((xv * r) * g_ref[...]).astype(BF16)

    return pl.pallas_call(
        body, name="rmsnorm_fwd", grid=(s // TM_MM,),
        in_specs=[pl.BlockSpec((TM_MM, d), lambda i: (i, 0)), pl.BlockSpec((1, d), lambda i: (0, 0))],
        out_specs=pl.BlockSpec((TM_MM, d), lambda i: (i, 0)),
        out_shape=jax.ShapeDtypeStruct((s, d), BF16),
        compiler_params=_params("parallel"),
    )(x, g)


def _in_proj_gather(h, shard, order):
    s, d = h.shape
    tn = shard.shape[1]
    n = s // TM_MM
    hr = d // 2

    def body(order_ref, h_ref, shard_ref, proj_ref, wfull_ref, wbuf, send1, recv1, send2, recv2, lsem):
        del order_ref
        j, i = pl.program_id(0), pl.program_id(1)
        x, y, c = _position()
        mine = 2 * x + y
        chips = [(x, 1 - y), (1 - x, y), (1 - x, 1 - y)]

        def theirs(k):
            return 2 * chips[k][0] + chips[k][1]

        def half_rows(half):
            return pl.ds(half * hr, hr)

        def ici(k, shard_index):
            return pltpu.make_async_remote_copy(
                src_ref=shard_ref.at[half_rows(c), :], dst_ref=wfull_ref.at[shard_index, half_rows(c), :],
                send_sem=send1.at[k], recv_sem=recv1.at[k], device_id=(*chips[k], c), device_id_type=MESH)

        def d2d(k, half):
            return pltpu.make_async_remote_copy(
                src_ref=wbuf.at[k % 2, half_rows(half), :], dst_ref=wfull_ref.at[theirs(k), half_rows(half), :],
                send_sem=send2.at[k], recv_sem=recv2.at[k], device_id=(x, y, 1 - c), device_id_type=MESH)

        def load(k, half, sem):
            return pltpu.make_async_copy(wfull_ref.at[theirs(k), half_rows(half), :],
                                         wbuf.at[k % 2, half_rows(half), :], lsem.at[sem])

        own = pltpu.make_async_copy(shard_ref, wfull_ref.at[mine], lsem.at[0])

        @pl.when((j == 0) & (i == 0))
        def _():
            own.start()
            ici(0, mine).start()
            ici(1, mine).start()

        for k in range(3):
            @pl.when((j == k) & (i == (n - 1 if k == 0 else n // 2)))
            def _(k=k):
                if k == 0:
                    ici(0, mine).wait_send()
                    ici(1, mine).wait_send()
                    ici(2, mine).start()
                if k == 2:
                    d2d(0, c).wait_send()
                ici(k, theirs(k)).wait_recv()
                load(k, c, 1).start()
                load(k, c, 1).wait()
                d2d(k, c).start()
                d2d(k, 1 - c).wait_recv()
                load(k, 1 - c, 2).start()

            @pl.when((j == k + 1) & (i == 0))
            def _(k=k):
                load(k, 1 - c, 2).wait()

        @pl.when(j == 0)
        def _():
            proj_ref[...] = _nn(h_ref[...], shard_ref[...]).astype(BF16)

        for k in range(3):
            @pl.when(j == k + 1)
            def _(k=k):
                proj_ref[...] = _nn(h_ref[...], wbuf[k % 2]).astype(BF16)

        @pl.when((j == 3) & (i == n - 1))
        def _():
            ici(2, mine).wait_send()
            d2d(1, c).wait_send()
            d2d(2, c).wait_send()
            own.wait()

    return pl.pallas_call(
        body, name="in_proj_gather",
        grid_spec=pltpu.PrefetchScalarGridSpec(
            num_scalar_prefetch=1, grid=(N_CHIPS, n),
            in_specs=[pl.BlockSpec((TM_MM, d), lambda j, i, order: (i, 0)), pl.BlockSpec(memory_space=pltpu.VMEM)],
            out_specs=[pl.BlockSpec((TM_MM, tn), lambda j, i, order: (i, order[j])), ANY],
            scratch_shapes=[pltpu.VMEM((2, d, tn), BF16), pltpu.SemaphoreType.DMA((3,)), pltpu.SemaphoreType.DMA((3,)),
                            pltpu.SemaphoreType.DMA((3,)), pltpu.SemaphoreType.DMA((3,)), pltpu.SemaphoreType.DMA((3,))]),
        out_shape=[jax.ShapeDtypeStruct((s, N_CHIPS * tn), BF16), jax.ShapeDtypeStruct((N_CHIPS, d, tn), BF16)],
        compiler_params=_params("arbitrary", "arbitrary"),
    )(order, h, shard)


def _attn_fwd(diag, proj, shards, kinds, cw8):
    s = proj.shape[0]
    n = s // TQ
    nw = len(shards)
    scale = HEAD_DIM ** -0.5

    def body(*refs):
        diag_ref, q_ref, kp_ref, kc_ref, vp_ref, vc_ref = refs[:6]
        srcs, cw = refs[6:6 + nw], refs[6 + nw]
        o_ref, lse_ref = refs[7 + nw:9 + nw]
        dsts, cw_all = refs[9 + nw:9 + 2 * nw], refs[9 + 2 * nw]
        bias_scr, kcat, vcat = refs[10 + 2 * nw:13 + 2 * nw]
        start, forward, finish = _gather_plan(kinds, srcs, dsts, cw, cw_all, *refs[13 + 2 * nw:])
        i = pl.program_id(0)

        @pl.when(i == 0)
        def _():
            start()
            _build_bias(diag_ref, bias_scr)

        @pl.when(i == n // 2)
        def _():
            forward()

        @pl.when(i == n - 1)
        def _():
            finish()

        kcat[0:TQ, :] = kp_ref[...]
        kcat[TQ:2 * TQ, :] = kc_ref[...]
        vcat[0:TQ, :] = vp_ref[...]
        vcat[TQ:2 * TQ, :] = vc_ref[...]
        lane_hi = lax.broadcasted_iota(jnp.int32, (QB, LANES), 1) >= HEAD_DIM
        col = lax.broadcasted_iota(jnp.int32, (2 * QB, KW), 1)

        def make_block(first_tile):
            def block(b, carry):
                r0 = pl.multiple_of(b * QB, QB)
                for p in range(HEADS // 2):
                    lanes = slice(LANES * p, LANES * (p + 1))
                    q2 = _stack_heads(q_ref[pl.ds(r0, QB), lanes] * scale, lane_hi)
                    kw = kcat[pl.ds(r0, KW), lanes]
                    vw = vcat[pl.ds(r0, KW), lanes]
                    sc = _nt(q2, kw) + bias_scr[p]
                    if first_tile:
                        sc = jnp.where(col >= TQ - r0, sc, NEG_BIG)
                    m = jnp.max(sc, axis=1, keepdims=True)
                    pe = jnp.exp(sc - m)
                    l = jnp.sum(pe, axis=1, keepdims=True)
                    o2 = _nn(pe.astype(BF16), vw) / l
                    lse2 = m + jnp.log(l)
                    lse_ref[pl.ds(r0, QB), 2 * p:2 * p + 1] = lse2[0:QB, :]
                    lse_ref[pl.ds(r0, QB), 2 * p + 1:2 * p + 2] = lse2[QB:2 * QB, :]
                    o_ref[pl.ds(r0, QB), lanes] = jnp.where(lane_hi, o2[QB:2 * QB, :], o2[0:QB, :]).astype(BF16)
                return carry
            return block

        @pl.when(i == 0)
        def _():
            lax.fori_loop(0, TQ // QB, make_block(True), 0)

        @pl.when(i > 0)
        def _():
            lax.fori_loop(0, TQ // QB, make_block(False), 0)

    blk = lambda col_blk, prev: pl.BlockSpec(
        (TQ, D_ATT), (lambda i: (jnp.maximum(i - 1, 0), col_blk)) if prev else (lambda i: (i, col_blk)))
    vmem = pl.BlockSpec(memory_space=pltpu.VMEM)
    return pl.pallas_call(
        body, name="attn_fwd", grid=(n,),
        in_specs=[pl.BlockSpec((HEADS, DIAG), lambda i: (0, 0)),
                  blk(0, False), blk(1, True), blk(1, False), blk(2, True), blk(2, False)] + [vmem] * (nw + 1),
        out_specs=[pl.BlockSpec((TQ, D_ATT), lambda i: (i, 0)), pl.BlockSpec((TQ, HEADS), lambda i: (i, 0))]
        + [ANY] * (nw + 1),
        out_shape=[jax.ShapeDtypeStruct((s, D_ATT), BF16), jax.ShapeDtypeStruct((s, HEADS), F32)]
        + _gather_out_shapes(shards, kinds, cw8),
        scratch_shapes=[pltpu.VMEM((HEADS // 2, 2 * QB, KW), F32), pltpu.VMEM((2 * TQ, D_ATT), BF16),
                        pltpu.VMEM((2 * TQ, D_ATT), BF16)] + _gather_sems(nw),
        compiler_params=_params("arbitrary"),
    )(diag, proj, proj, proj, proj, proj, *shards, cw8)


def _attn_bwd(diag, proj, d_att, att, lse, parts):
    s = proj.shape[0]
    n = s // TQ
    npart = len(parts)
    scale = HEAD_DIM ** -0.5
    rel_pad = 3 * LANES

    def body(*refs):
        diag_ref, q_ref, kp_ref, kc_ref, vp_ref, vc_ref, do_ref, o_ref, lse_ref = refs[:9]
        part_refs = refs[9:9 + npart]
        dqkv_ref, dbias_ref = refs[9 + npart:11 + npart]
        slot_refs = refs[11 + npart:11 + 2 * npart]
        bias_scr, dbias_acc, kcat, vcat, dk_acc, dv_acc, dq_scr = refs[11 + 2 * npart:18 + 2 * npart]
        start, finish = _scatter_plan(part_refs, slot_refs, *refs[18 + 2 * npart:])
        i = pl.program_id(0)

        @pl.when(i == 0)
        def _():
            start()
            _build_bias(diag_ref, bias_scr)
            dbias_acc[...] = jnp.zeros_like(dbias_acc)
            dk_acc[...] = jnp.zeros_like(dk_acc)
            dv_acc[...] = jnp.zeros_like(dv_acc)

        @pl.when(i > 0)
        def _():
            dqkv_ref[:, 0:D_ATT] = dq_scr[...]
            dk_acc[0:TQ, :] = dk_acc[TQ:2 * TQ, :]
            dk_acc[TQ:2 * TQ, :] = jnp.zeros((TQ, D_ATT), F32)
            dv_acc[0:TQ, :] = dv_acc[TQ:2 * TQ, :]
            dv_acc[TQ:2 * TQ, :] = jnp.zeros((TQ, D_ATT), F32)

        @pl.when(i < n)
        def _():
            kcat[0:TQ, :] = kp_ref[...]
            kcat[TQ:2 * TQ, :] = kc_ref[...]
            vcat[0:TQ, :] = vp_ref[...]
            vcat[TQ:2 * TQ, :] = vc_ref[...]

        lane_hi = lax.broadcasted_iota(jnp.int32, (QB, LANES), 1) >= HEAD_DIM
        col = lax.broadcasted_iota(jnp.int32, (2 * QB, KW), 1)

        def make_block(first_tile):
            def block(b, carry):
                r0 = pl.multiple_of(b * QB, QB)
                for p in range(HEADS // 2):
                    lanes = slice(LANES * p, LANES * (p + 1))
                    q2 = _stack_heads(q_ref[pl.ds(r0, QB), lanes] * scale, lane_hi)
                    kw = kcat[pl.ds(r0, KW), lanes]
                    vw = vcat[pl.ds(r0, KW), lanes]
                    dop = do_ref[pl.ds(r0, QB), lanes]
                    do2 = _stack_heads(dop, lane_hi)
                    prod = dop.astype(F32) * o_ref[pl.ds(r0, QB), lanes].astype(F32)
                    delta2 = jnp.concatenate(
                        [jnp.sum(jnp.where(lane_hi, 0.0, prod), axis=1, keepdims=True),
                         jnp.sum(jnp.where(lane_hi, prod, 0.0), axis=1, keepdims=True)], axis=0)
                    lse2 = jnp.concatenate([lse_ref[pl.ds(r0, QB), 2 * p:2 * p + 1],
                                            lse_ref[pl.ds(r0, QB), 2 * p + 1:2 * p + 2]], axis=0)
                    sc = _nt(q2, kw) + bias_scr[p]
                    if first_tile:
                        sc = jnp.where(col >= TQ - r0, sc, NEG_BIG)
                    pr = jnp.exp(sc - lse2)
                    ds = pr * (_nt(do2, vw) - delta2)
                    dbias_acc[p] += ds
                    dsb = ds.astype(BF16)
                    dv_acc[pl.ds(r0, KW), lanes] += _tn(pr.astype(BF16), do2)
                    dk_acc[pl.ds(r0, KW), lanes] += _tn(dsb, q2)
                    dq2 = _nn(dsb, kw)
                    dq = jnp.where(lane_hi, dq2[QB:2 * QB, :], dq2[0:QB, :]) * scale
                    dq_scr[pl.ds(r0, QB), lanes] = dq.astype(BF16)
                return carry
            return block

        @pl.when(i == 0)
        def _():
            lax.fori_loop(0, TQ // QB, make_block(True), 0)

        @pl.when((i > 0) & (i < n))
        def _():
            lax.fori_loop(0, TQ // QB, make_block(False), 0)

        @pl.when(i > 0)
        def _():
            dqkv_ref[:, D_ATT:2 * D_ATT] = dk_acc[0:TQ, :].astype(BF16)
            dqkv_ref[:, 2 * D_ATT:3 * D_ATT] = dv_acc[0:TQ, :].astype(BF16)

        @pl.when(i == n)
        def _():
            d_iota = lax.broadcasted_iota(jnp.int32, (DIAG, rel_pad), 0)
            n_iota = lax.broadcasted_iota(jnp.int32, (DIAG, rel_pad), 1)
            diff = jnp.where(d_iota < KW, d_iota, d_iota - DIAG)
            idx = jnp.clip(N_LEFT * CHUNK - diff, -MAX_REL, MAX_REL) + MAX_REL
            onehot = (idx == n_iota).astype(F32)
            rows = []
            for hd in range(HEADS):
                acc = dbias_acc[hd // 2, (hd % 2) * QB:(hd % 2 + 1) * QB, :]
                a = jnp.concatenate([acc, jnp.zeros((QB, DIAG - KW), F32)], axis=1)
                g8 = a[0:SUBLANES, :]
                for blk in range(1, QB // SUBLANES):
                    g8 = g8 + pltpu.roll(a[blk * SUBLANES:(blk + 1) * SUBLANES, :], DIAG - blk * SUBLANES, 1)
                g1 = g8[0:1, :]
                for r in range(1, SUBLANES):
                    g1 = g1 + pltpu.roll(g8[r:r + 1, :], DIAG - r, 1)
                rows.append(g1)
            g = jnp.concatenate(rows, axis=0)
            dbias_ref[...] = jnp.dot(g, onehot, preferred_element_type=F32, precision=lax.Precision.HIGHEST)
            finish()

    last = n - 1
    cur = lambda col_blk: pl.BlockSpec((TQ, D_ATT), lambda i: (jnp.minimum(i, last), col_blk))
    prev = lambda col_blk: pl.BlockSpec((TQ, D_ATT), lambda i: (jnp.maximum(jnp.minimum(i, last) - 1, 0), col_blk))
    return pl.pallas_call(
        body, name="attn_bwd", grid=(n + 1,),
        in_specs=[pl.BlockSpec((HEADS, DIAG), lambda i: (0, 0)),
                  cur(0), prev(1), cur(1), prev(2), cur(2), cur(0), cur(0),
                  pl.BlockSpec((TQ, HEADS), lambda i: (jnp.minimum(i, last), 0))] + [ANY] * npart,
        out_specs=[pl.BlockSpec((TQ, 3 * D_ATT), lambda i: (jnp.maximum(i - 1, 0), 0)),
                   pl.BlockSpec((HEADS, rel_pad), lambda i: (0, 0))] + [ANY] * npart,
        out_shape=[jax.ShapeDtypeStruct((s, 3 * D_ATT), BF16), jax.ShapeDtypeStruct((HEADS, rel_pad), F32)]
        + _scatter_out_shapes(parts),
        scratch_shapes=[pltpu.VMEM((HEADS // 2, 2 * QB, KW), F32), pltpu.VMEM((HEADS // 2, 2 * QB, KW), F32),
                        pltpu.VMEM((2 * TQ, D_ATT), BF16), pltpu.VMEM((2 * TQ, D_ATT), BF16),
                        pltpu.VMEM((2 * TQ, D_ATT), F32), pltpu.VMEM((2 * TQ, D_ATT), F32),
                        pltpu.VMEM((TQ, D_ATT), BF16)] + _scatter_sems(npart),
        compiler_params=_params("arbitrary"),
    )(diag, proj, proj, proj, proj, proj, d_att, att, lse, *parts)


def _shift_down(a, k, halo):
    rolled = pltpu.roll(a, k, 0)
    row = lax.broadcasted_iota(jnp.int32, halo.shape, 0)
    first = jnp.where(row < k, pltpu.roll(halo, k, 0), rolled[0:SUBLANES, :])
    return jnp.concatenate([first, rolled[SUBLANES:, :]], axis=0)


def _shift_up(a, k, nxt):
    tm = a.shape[0]
    rolled = pltpu.roll(a, tm - k, 0)
    row = lax.broadcasted_iota(jnp.int32, nxt.shape, 0)
    last = jnp.where(row >= SUBLANES - k, pltpu.roll(nxt, SUBLANES - k, 0), rolled[tm - SUBLANES:, :])
    return jnp.concatenate([rolled[:tm - SUBLANES, :], last], axis=0)


def _sigmoid(v):
    return 0.5 * jnp.tanh(0.5 * v) + 0.5


def _mixer_mid(att, proj, x, tgt, w_att, w_conv, w_out, conv_w8, conv_b, fin_g):
    s, d = x.shape
    dc = D_ATT
    n = s // TM_MID
    tm = TM_MID
    n_shards = 4

    def body(att_ref, za_ref, gb_ref, gc_ref, u_ref, zc_ref, hgc_ref, hu_ref, gatt_ref, gconv_ref, x_ref, t_ref,
             watt_ref, wconv_ref, wout_ref, cw_ref, cb_ref, fg_ref,
             dpb_ref, do_ref, dx2_ref, gatt_o, gconv_o, gout_o, loss_o, gfn_o, gcb_o, gcw_o,
             acc_att, acc_conv, acc_out, carry):
        i = pl.program_id(0)
        tile = n - 1 - i

        @pl.when(i == 0)
        def _():
            acc_att[...] = jnp.zeros_like(acc_att)
            acc_conv[...] = jnp.zeros_like(acc_conv)
            acc_out[...] = jnp.zeros_like(acc_out)
            carry[...] = jnp.zeros_like(carry)
            loss_o[...] = jnp.zeros_like(loss_o)
            gfn_o[...] = jnp.zeros_like(gfn_o)
            gcb_o[...] = jnp.zeros_like(gcb_o)
            gcw_o[...] = jnp.zeros_like(gcw_o)

        att_v = att_ref[...].astype(F32)
        za = za_ref[...].astype(F32)
        sa = _sigmoid(za)
        silu_a = za * sa
        a_b = (att_v * silu_a).astype(BF16)

        gb = gb_ref[...].astype(F32)
        gc = gc_ref[...].astype(F32)
        u = u_ref[...].astype(F32)
        zc = zc_ref[...].astype(F32)
        cu = gc * u
        halo = jnp.where(tile > 0, hgc_ref[...].astype(F32) * hu_ref[...].astype(F32), 0.0)
        cu1 = _shift_down(cu, 1, halo)
        cu2 = _shift_down(cu, 2, halo)
        w0, w1, w2 = cw_ref[0:1, :], cw_ref[1:2, :], cw_ref[2:3, :]
        vconv = w0 * cu2 + w1 * cu1 + w2 * cu + cb_ref[...]
        sc = _sigmoid(zc)
        silu_c = zc * sc
        c_b = (gb * vconv * silu_c).astype(BF16)

        y_att = _nn(a_b, watt_ref[...])
        y_conv = _nn(c_b, wconv_ref[...])
        ga = _sigmoid(gatt_ref[...].astype(F32))
        gv = _sigmoid(gconv_ref[...].astype(F32))
        m_b = (ga * y_att + gv * y_conv).astype(BF16)
        x2 = x_ref[...] + _nn(m_b, wout_ref[...])
        r2 = lax.rsqrt(jnp.mean(x2 * x2, axis=-1, keepdims=True) + EPS)
        x2n = x2 * r2
        fg = fg_ref[...]
        err = x2n * fg - t_ref[...]
        loss_o[...] += jnp.sum(err * err, axis=0, keepdims=True) * (0.5 / d)
        dy = err * (1.0 / d)
        gfn_o[...] += jnp.sum(dy * x2n, axis=0, keepdims=True)
        dyn = dy * fg
        dx2 = r2 * (dyn - x2n * jnp.mean(dyn * x2n, axis=-1, keepdims=True))
        dx2_ref[...] = dx2
        dx2_b = dx2.astype(BF16)

        dm = _nt(dx2_b, wout_ref[...])
        acc_out[...] += _tn(m_b, dx2_b)
        dy_att = dm * ga
        dy_conv = dm * gv
        dpb_ref[:, 5 * dc:5 * dc + d] = (dy_att * y_att * (1.0 - ga)).astype(BF16)
        dpb_ref[:, 5 * dc + d:5 * dc + 2 * d] = (dy_conv * y_conv * (1.0 - gv)).astype(BF16)
        dya_b = dy_att.astype(BF16)
        dyc_b = dy_conv.astype(BF16)
        da_in = _nt(dya_b, watt_ref[...])
        acc_att[...] += _tn(a_b, dya_b)
        dc_in = _nt(dyc_b, wconv_ref[...])
        acc_conv[...] += _tn(c_b, dyc_b)

        do_ref[...] = (da_in * silu_a).astype(BF16)
        dpb_ref[:, 0:dc] = (da_in * att_v * (sa * (1.0 + za * (1.0 - sa)))).astype(BF16)
        dpb_ref[:, dc:2 * dc] = (dc_in * vconv * silu_c).astype(BF16)
        dgs = dc_in * gb
        dvc = dgs * silu_c
        dpb_ref[:, 4 * dc:5 * dc] = (dgs * vconv * (sc * (1.0 + zc * (1.0 - sc)))).astype(BF16)
        gcb_o[...] += jnp.sum(dvc, axis=0, keepdims=True)
        gcw_o[0:1, :] += jnp.sum(dvc * cu2, axis=0, keepdims=True)
        gcw_o[1:2, :] += jnp.sum(dvc * cu1, axis=0, keepdims=True)
        gcw_o[2:3, :] += jnp.sum(dvc * cu, axis=0, keepdims=True)
        nxt = carry[...]
        dcu = w2 * dvc + w1 * _shift_up(dvc, 1, nxt) + w0 * _shift_up(dvc, 2, nxt)
        carry[...] = dvc[0:SUBLANES, :]
        dpb_ref[:, 2 * dc:3 * dc] = (dcu * u).astype(BF16)
        dpb_ref[:, 3 * dc:4 * dc] = (dcu * gc).astype(BF16)

        @pl.when(i == n - 1)
        def _():
            for j in range(n_shards):
                gatt_o[j] = acc_att[:, j * (d // n_shards):(j + 1) * (d // n_shards)].astype(BF16)
                gconv_o[j] = acc_conv[:, j * (d // n_shards):(j + 1) * (d // n_shards)].astype(BF16)
                gout_o[j] = acc_out[j * (d // n_shards):(j + 1) * (d // n_shards), :].astype(BF16)

    rev = lambda width, col_blk: pl.BlockSpec((tm, width), lambda i: (n - 1 - i, col_blk))
    halo_spec = lambda col_blk: pl.BlockSpec(
        (SUBLANES, dc), lambda i: (jnp.maximum((n - 1 - i) * (tm // SUBLANES) - 1, 0), col_blk))
    const = lambda shape: pl.BlockSpec(shape, lambda i: tuple(0 for _ in shape), pipeline_mode=pl.Buffered(1))
    q4 = d // n_shards
    return pl.pallas_call(
        body, name="mixer_mid", grid=(n,),
        in_specs=[rev(dc, 0), rev(dc, 3), rev(dc, 4), rev(dc, 5), rev(dc, 6), rev(dc, 7),
                  halo_spec(5), halo_spec(6), rev(d, 4), rev(d, 5), rev(d, 0), rev(d, 0),
                  const((dc, d)), const((dc, d)), const((d, d)), const((SUBLANES, dc)), const((1, dc)), const((1, d))],
        out_specs=[rev(5 * dc + 2 * d, 0), rev(dc, 0), rev(d, 0),
                   const((n_shards, dc, q4)), const((n_shards, dc, q4)), const((n_shards, q4, d)),
                   const((1, d)), const((1, d)), const((1, dc)), const((SUBLANES, dc))],
        out_shape=[jax.ShapeDtypeStruct((s, 5 * dc + 2 * d), BF16), jax.ShapeDtypeStruct((s, dc), BF16),
                   jax.ShapeDtypeStruct((s, d), F32),
                   jax.ShapeDtypeStruct((n_shards, dc, q4), BF16), jax.ShapeDtypeStruct((n_shards, dc, q4), BF16),
                   jax.ShapeDtypeStruct((n_shards, q4, d), BF16),
                   jax.ShapeDtypeStruct((1, d), F32), jax.ShapeDtypeStruct((1, d), F32),
                   jax.ShapeDtypeStruct((1, dc), F32), jax.ShapeDtypeStruct((SUBLANES, dc), F32)],
        scratch_shapes=[pltpu.VMEM((dc, d), F32), pltpu.VMEM((dc, d), F32), pltpu.VMEM((d, d), F32),
                        pltpu.VMEM((SUBLANES, dc), F32)],
        compiler_params=_params("arbitrary"),
    )(att, proj, proj, proj, proj, proj, proj, proj, proj, proj, x, tgt,
      w_att, w_conv, w_out, conv_w8, conv_b, fin_g)


def _in_proj_bwd_x(dqkv, dpb, w_in, x, dx2, g):
    s, d = x.shape
    tn = dqkv.shape[1]
    nb = dpb.shape[1] // tn
    n = s // TM_MM

    def body(*refs):
        dps, ws = refs[:nb + 1], refs[nb + 1:2 * nb + 2]
        x_ref, dx2_ref, g_ref, gx_ref, gng_ref = refs[2 * nb + 2:]
        i = pl.program_id(0)

        @pl.when(i == 0)
        def _():
            gng_ref[...] = jnp.zeros_like(gng_ref)

        dh = _nt(dps[0][...], ws[0][0])
        for j in range(1, nb + 1):
            dh = dh + _nt(dps[j][...], ws[j][0])
        xv = x_ref[...]
        r = lax.rsqrt(jnp.mean(xv * xv, axis=-1, keepdims=True) + EPS)
        xn = xv * r
        gng_ref[...] += jnp.sum(dh * xn, axis=0, keepdims=True)
        dhn = dh * g_ref[...]
        gx_ref[...] = dx2_ref[...] + r * (dhn - xn * jnp.mean(dhn * xn, axis=-1, keepdims=True))

    tile = lambda width, col_blk: pl.BlockSpec((TM_MM, width), lambda i: (i, col_blk))
    wspec = lambda blk: pl.BlockSpec((1, d, tn), lambda i: (blk, 0, 0), pipeline_mode=pl.Buffered(1))
    return pl.pallas_call(
        body, name="in_proj_bwd_x", grid=(n,),
        in_specs=[tile(tn, 0)] + [tile(tn, j) for j in range(nb)] + [wspec(j) for j in range(nb + 1)]
        + [tile(d, 0), tile(d, 0), pl.BlockSpec((1, d), lambda i: (0, 0))],
        out_specs=[tile(d, 0), pl.BlockSpec((1, d), lambda i: (0, 0))],
        out_shape=[jax.ShapeDtypeStruct((s, d), F32), jax.ShapeDtypeStruct((1, d), F32)],
        compiler_params=_params("arbitrary"),
    )(dqkv, *([dpb] * nb), *([w_in] * (nb + 1)), x, dx2, g)


def _in_proj_bwd_w(h, dqkv, dpb, order):
    s, d = h.shape
    tn = dqkv.shape[1]
    n = s // TM_MM
    hr = d // 2
    settle = min(2, n - 1)

    def body(order_ref, h_ref, da_ref, db_ref, slots_ref, acc, sendbuf, pairbuf, chipbuf, psend, precv, send, recv, lsem):
        j, i = pl.program_id(0), pl.program_id(1)
        blk = order_ref[j]
        pos = _position()
        x, y, c = pos

        @pl.when(i == 0)
        def _():
            acc[...] = jnp.zeros_like(acc)

        @pl.when(blk == 0)
        def _():
            acc[...] += _tn(h_ref[...], da_ref[...])

        @pl.when(blk > 0)
        def _():
            acc[...] += _tn(h_ref[...], db_ref[...])

        def pair(step, half):
            return pltpu.make_async_remote_copy(
                src_ref=sendbuf.at[step, pl.ds(half * hr, hr), :], dst_ref=pairbuf.at[step],
                send_sem=psend.at[step], recv_sem=precv.at[step], device_id=(x, y, 1 - c), device_id_type=MESH)

        def ici(step):
            flip = step + 1
            return pltpu.make_async_remote_copy(
                src_ref=chipbuf.at[step], dst_ref=slots_ref.at[flip], send_sem=send.at[step], recv_sem=recv.at[step],
                device_id=_peer(pos, 4 * (flip >> 1) + 2 * (flip & 1)), device_id_type=MESH)

        local = pltpu.make_async_copy(chipbuf.at[N_CHIPS - 1], slots_ref.at[0], lsem.at[0])

        def combine(step):
            pair(step, c).wait_recv()
            mine = sendbuf[step, pl.ds(c * hr, hr), :].astype(F32)
            chipbuf[step] = (mine + pairbuf[step].astype(F32)).astype(BF16)

        for step in range(N_CHIPS):
            @pl.when((j == step) & (i == n - 1))
            def _(step=step):
                sendbuf[step] = acc[...].astype(BF16)
                pair(step, 1 - c).start()

        for step in range(N_CHIPS - 1):
            @pl.when((j == step + 1) & (i == settle))
            def _(step=step):
                combine(step)
                ici(step).start()

        @pl.when((j == N_CHIPS - 1) & (i == n - 1))
        def _():
            combine(N_CHIPS - 1)
            local.start()
            for step in range(N_CHIPS - 1):
                ici(step).wait_recv()
            for step in range(N_CHIPS - 1):
                ici(step).wait_send()
            for step in range(N_CHIPS):
                pair(step, 1 - c).wait_send()
            local.wait()

    return pl.pallas_call(
        body, name="in_proj_bwd_w",
        grid_spec=pltpu.PrefetchScalarGridSpec(
            num_scalar_prefetch=1, grid=(N_CHIPS, n),
            in_specs=[pl.BlockSpec((TM_MM, d), lambda j, i, order: (i, 0)),
                      pl.BlockSpec((TM_MM, tn), lambda j, i, order: (jnp.where(order[j] == 0, i, 0), 0)),
                      pl.BlockSpec((TM_MM, tn), lambda j, i, order: (jnp.where(order[j] == 0, 0, i),
                                                                     jnp.maximum(order[j] - 1, 0)))],
            out_specs=[ANY],
            scratch_shapes=[pltpu.VMEM((d, tn), F32), pltpu.VMEM((N_CHIPS, d, tn), BF16),
                            pltpu.VMEM((N_CHIPS, hr, tn), BF16), pltpu.VMEM((N_CHIPS, hr, tn), BF16),
                            pltpu.SemaphoreType.DMA((N_CHIPS,)), pltpu.SemaphoreType.DMA((N_CHIPS,)),
                            pltpu.SemaphoreType.DMA((N_CHIPS - 1,)), pltpu.SemaphoreType.DMA((N_CHIPS - 1,)),
                            pltpu.SemaphoreType.DMA((1,))]),
        out_shape=[jax.ShapeDtypeStruct((N_CHIPS, hr, tn), BF16)],
        compiler_params=_params("arbitrary", "arbitrary"),
    )(order, h, dqkv, dpb)[0]


def _sum_slots(recv, name, rows_per_step):
    k, r, c = recv.shape

    def body(r_ref, o_ref):
        total = r_ref[0].astype(F32)
        for slot in range(1, k):
            total = total + r_ref[slot].astype(F32)
        o_ref[...] = total

    return pl.pallas_call(
        body, name=name, grid=(r // rows_per_step,),
        in_specs=[pl.BlockSpec((k, rows_per_step, c), lambda i: (0, i, 0))],
        out_specs=pl.BlockSpec((rows_per_step, c), lambda i: (i, 0)),
        out_shape=jax.ShapeDtypeStruct((r, c), F32),
        compiler_params=_params("parallel"),
    )(recv)


LOSS_ROW = 6


def _sum_small(recv):
    k = recv.shape[0]

    def body(r_ref, o_ref):
        total = r_ref[0]
        for slot in range(1, k):
            total = total + r_ref[slot]
        o_ref[...] = total
        loss = jnp.sum(total[LOSS_ROW:LOSS_ROW + 1, :], axis=1, keepdims=True)
        o_ref[LOSS_ROW:LOSS_ROW + 1, :] = jnp.broadcast_to(loss, (1, SMALL_COLS))

    return pl.pallas_call(
        body, name="reduce_sum_small",
        out_shape=jax.ShapeDtypeStruct((SMALL_ROWS, SMALL_COLS), F32),
    )(recv)


def _adamw(w, g, m, v, name, rows_per_step):
    r, c = w.shape
    c1 = 1.0 / (1.0 - ADAM_B1 ** ADAM_STEP)
    c2 = 1.0 / (1.0 - ADAM_B2 ** ADAM_STEP)

    def body(w_ref, g_ref, m_ref, v_ref, d_ref, mo_ref, vo_ref):
        gv = g_ref[...]
        m2 = ADAM_B1 * m_ref[...] + (1.0 - ADAM_B1) * gv
        v2 = ADAM_B2 * v_ref[...] + (1.0 - ADAM_B2) * (gv * gv)
        mo_ref[...] = m2
        vo_ref[...] = v2
        d_ref[...] = -ADAM_LR * ((m2 * c1) / (jnp.sqrt(v2 * c2) + ADAM_EPS) + ADAM_WD * w_ref[...])

    spec = pl.BlockSpec((rows_per_step, c), lambda i: (i, 0))
    shape = jax.ShapeDtypeStruct((r, c), F32)
    return pl.pallas_call(
        body, name=name, grid=(r // rows_per_step,),
        in_specs=[spec] * 4, out_specs=[spec] * 3, out_shape=[shape] * 3,
        compiler_params=_params("parallel"),
    )(w, g, m, v)


ANY = pl.BlockSpec(memory_space=pl.ANY)
N_CHIPS = 4
N_DEV = 8


def _position():
    return lax.axis_index("x"), lax.axis_index("y"), lax.axis_index("c")


def _gather_out_shapes(shards, kinds, cw8):
    full = [(a.shape[0], a.shape[1] * N_CHIPS) if k == "cols" else (a.shape[0] * N_CHIPS, a.shape[1])
            for a, k in zip(shards, kinds)]
    return [jax.ShapeDtypeStruct(f, a.dtype) for f, a in zip(full, shards)] + [
        jax.ShapeDtypeStruct((N_CHIPS,) + cw8.shape, cw8.dtype)]


def _gather_sems(nw):
    return [pltpu.SemaphoreType.DMA((3, nw)), pltpu.SemaphoreType.DMA((3, nw)),
            pltpu.SemaphoreType.DMA((3, nw)), pltpu.SemaphoreType.DMA((3, nw)),
            pltpu.SemaphoreType.DMA((3,)), pltpu.SemaphoreType.DMA((3,)), pltpu.SemaphoreType.DMA((nw + 1,))]


def _gather_plan(kinds, srcs, dsts, cw, cw_all, send1, recv1, send2, recv2, ssend, srecv, lsem):
    nw = len(srcs)
    x, y, c = _position()
    mine = 2 * x + y
    chips = [(x, 1 - y), (1 - x, y), (1 - x, 1 - y)]

    def window(w, shard, half):
        r, cc = srcs[w].shape
        hr = r // 2
        if kinds[w] == "cols":
            rows = pl.ds(0, r) if half is None else pl.ds(half * hr, hr)
            return dsts[w].at[rows, pl.ds(shard * cc, cc)]
        rows = pl.ds(shard * r, r) if half is None else pl.ds(shard * r + half * hr, hr)
        return dsts[w].at[rows, :]

    def my_half(w):
        hr = srcs[w].shape[0] // 2
        return srcs[w].at[pl.ds(c * hr, hr), :]

    def local():
        return [pltpu.make_async_copy(srcs[w], window(w, mine, None), lsem.at[w]) for w in range(nw)] + [
            pltpu.make_async_copy(cw, cw_all.at[mine], lsem.at[nw])]

    def ici(k, w, shard):
        kx, ky = chips[k]
        return pltpu.make_async_remote_copy(
            src_ref=my_half(w), dst_ref=window(w, shard, c), send_sem=send1.at[k, w], recv_sem=recv1.at[k, w],
            device_id=(kx, ky, c), device_id_type=MESH)

    def d2d(k, w, shard, half):
        return pltpu.make_async_remote_copy(
            src_ref=window(w, shard, half), dst_ref=window(w, shard, half),
            send_sem=send2.at[k, w], recv_sem=recv2.at[k, w], device_id=(x, y, 1 - c), device_id_type=MESH)

    def small(k, shard):
        kx, ky = chips[k]
        return pltpu.make_async_remote_copy(
            src_ref=cw, dst_ref=cw_all.at[shard], send_sem=ssend.at[k], recv_sem=srecv.at[k],
            device_id=(kx, ky, c), device_id_type=MESH)

    def theirs(k):
        kx, ky = chips[k]
        return 2 * kx + ky

    def start():
        for cp in local():
            cp.start()
        for k in range(3):
            for w in range(nw):
                ici(k, w, mine).start()
            small(k, mine).start()

    def forward():
        for k in range(3):
            for w in range(nw):
                ici(k, w, theirs(k)).wait_recv()
                d2d(k, w, theirs(k), c).start()

    def finish():
        for k in range(3):
            for w in range(nw):
                d2d(k, w, theirs(k), 1 - c).wait_recv()
            small(k, theirs(k)).wait_recv()
        for k in range(3):
            for w in range(nw):
                ici(k, w, mine).wait_send()
                d2d(k, w, theirs(k), c).wait_send()
            small(k, mine).wait_send()
        for cp in local():
            cp.wait()

    return start, forward, finish


def _scatter_out_shapes(parts):
    return [jax.ShapeDtypeStruct((N_DEV, p.shape[1] // 2, p.shape[2]), p.dtype) for p in parts]


def _scatter_sems(nw):
    return [pltpu.SemaphoreType.DMA((N_DEV, nw)), pltpu.SemaphoreType.DMA((N_DEV, nw)), pltpu.SemaphoreType.DMA((nw,))]


def _peer(pos, k):
    x, y, c = pos
    return ((1 - x) if k & 4 else x, (1 - y) if k & 2 else y, (1 - c) if k & 1 else c)


def _scatter_plan(srcs, dsts, send, recv, lsem):
    nw = len(srcs)
    pos = _position()

    def piece(w, k):
        px, py, pc = _peer(pos, k)
        hr = srcs[w].shape[1] // 2
        return srcs[w].at[2 * px + py, pl.ds(pc * hr, hr), :]

    def remote(w, k):
        return pltpu.make_async_remote_copy(
            src_ref=piece(w, k), dst_ref=dsts[w].at[k], send_sem=send.at[k, w], recv_sem=recv.at[k, w],
            device_id=_peer(pos, k), device_id_type=MESH)

    def local(w):
        return pltpu.make_async_copy(piece(w, 0), dsts[w].at[0], lsem.at[w])

    def start():
        for w in range(nw):
            local(w).start()
        for k in range(1, N_DEV):
            for w in range(nw):
                remote(w, k).start()

    def finish():
        for k in range(1, N_DEV):
            for w in range(nw):
                remote(w, k).wait_recv()
        for k in range(1, N_DEV):
            for w in range(nw):
                remote(w, k).wait_send()
        for w in range(nw):
            local(w).wait()

    return start, finish


def _reduce_pair(halves, small):
    nw = len(halves)

    def body(*refs):
        srcs, sm = refs[:nw], refs[nw]
        dsts, sm_all = refs[nw + 1:2 * nw + 1], refs[2 * nw + 1]
        send, recv, ssend, srecv, lsem = refs[2 * nw + 2:]
        pos = _position()
        x, y, c = pos
        me = 4 * x + 2 * y + c

        def rows(w, half):
            hr = halves[w].shape[0]
            return dsts[w].at[pl.ds(half * hr, hr), :]

        def remote(w, half):
            return pltpu.make_async_remote_copy(
                src_ref=srcs[w], dst_ref=rows(w, half), send_sem=send.at[w], recv_sem=recv.at[w],
                device_id=(x, y, 1 - c), device_id_type=MESH)

        def bcast(k, slot):
            return pltpu.make_async_remote_copy(
                src_ref=sm, dst_ref=sm_all.at[slot], send_sem=ssend.at[k], recv_sem=srecv.at[k],
                device_id=_peer(pos, k), device_id_type=MESH)

        local = [pltpu.make_async_copy(srcs[w], rows(w, c), lsem.at[w]) for w in range(nw)]
        local.append(pltpu.make_async_copy(sm, sm_all.at[me], lsem.at[nw]))
        sent = [remote(w, c) for w in range(nw)] + [bcast(k, me) for k in range(1, N_DEV)]
        for cp in sent + local:
            cp.start()
        for w in range(nw):
            remote(w, 1 - c).wait_recv()
        for k in range(1, N_DEV):
            px, py, pc = _peer(pos, k)
            bcast(k, 4 * px + 2 * py + pc).wait_recv()
        for cp in sent:
            cp.wait_send()
        for cp in local:
            cp.wait()

    vmem = pl.BlockSpec(memory_space=pltpu.VMEM)
    return pl.pallas_call(
        body, name="reduce_pair",
        in_specs=[vmem] * (nw + 1), out_specs=[ANY] * (nw + 1),
        out_shape=[jax.ShapeDtypeStruct((2 * hv.shape[0], hv.shape[1]), hv.dtype) for hv in halves]
        + [jax.ShapeDtypeStruct((N_DEV,) + small.shape, small.dtype)],
        scratch_shapes=[pltpu.SemaphoreType.DMA((nw,)), pltpu.SemaphoreType.DMA((nw,)),
                        pltpu.SemaphoreType.DMA((N_DEV,)), pltpu.SemaphoreType.DMA((N_DEV,)),
                        pltpu.SemaphoreType.DMA((nw + 1,))],
        compiler_params=pltpu.CompilerParams(vmem_limit_bytes=VMEM_LIMIT),
    )(*halves, small)


def _pad_to(a, rows, cols):
    return jnp.pad(a, ((0, rows - a.shape[0]), (0, cols - a.shape[1])))


def _pack_small(norm_g, fin_g, conv_b, conv_w, loss_vec, rel):
    rows = [_pad_to(norm_g, 1, SMALL_COLS), _pad_to(fin_g, 1, SMALL_COLS), _pad_to(conv_b, 1, SMALL_COLS),
            _pad_to(conv_w, 3, SMALL_COLS), _pad_to(loss_vec, 2, SMALL_COLS), _pad_to(rel, HEADS, SMALL_COLS)]
    return jnp.concatenate(rows, axis=0)


def kernel(x, norm_g, w_in, rel_bias, w_att_out, conv_w, conv_b, w_conv_out, w_out, final_norm_g, loss_target, m_norm_g, m_w_in, m_rel_bias, m_w_att_out, m_conv_w, m_conv_b, m_w_conv_out, m_w_out, m_final_norm_g, v_norm_g, v_w_in, v_rel_bias, v_w_att_out, v_conv_w, v_conv_b, v_w_conv_out, v_w_out, v_final_norm_g):
    xs, tgt = x[0], loss_target[0]
    cshard = conv_w.shape[2]
    chip = 2 * lax.axis_index("x") + lax.axis_index("y")

    shards = [w_in[0].astype(BF16), w_att_out[0].astype(BF16), w_conv_out[0].astype(BF16), w_out[0].astype(BF16)]
    cw8 = _pad_to(conv_w[0], SUBLANES, cshard)
    flips = jnp.arange(N_CHIPS, dtype=jnp.int32)
    own_first = jnp.bitwise_xor(chip, flips)
    own_last = jnp.bitwise_xor(chip, (flips + 1) % N_CHIPS)

    h = _rmsnorm_fwd(xs, norm_g)
    proj, wb_in = _in_proj_gather(h, shards[0], own_first)
    diag = jnp.take(rel_bias[0], _diag_rel_index(), axis=1)
    att, lse, wb_att, wb_conv, wb_out, cw_all = _attn_fwd(diag, proj, shards[1:], ["cols", "cols", "rows"], cw8)
    conv_w_full = jnp.transpose(cw_all, (1, 0, 2)).reshape(SUBLANES, N_CHIPS * cshard)
    (dpb, d_att, dx2, g_att_p, g_conv_p, g_out_p, loss_vec, g_fin, g_cb, g_cw) = _mixer_mid(
        att, proj, xs, tgt, wb_att, wb_conv, wb_out, conv_w_full, conv_b, final_norm_g[None, :])
    dqkv, g_rel, r_att, r_conv, r_out = _attn_bwd(diag, proj, d_att, att, lse, [g_att_p, g_conv_p, g_out_p])
    grad_x, g_norm = _in_proj_bwd_x(dqkv, dpb, wb_in, xs, dx2, norm_g)
    r_in = _in_proj_bwd_w(h, dqkv, dpb, own_last)

    small = _pack_small(g_norm, g_fin, g_cb, g_cw[0:3], loss_vec, g_rel)
    halves = [_sum_slots(r_in, "reduce_sum_in", 64), _sum_slots(r_att, "reduce_sum_att", r_att.shape[1]),
              _sum_slots(r_conv, "reduce_sum_conv", r_conv.shape[1]), _sum_slots(r_out, "reduce_sum_out", r_out.shape[1])]
    gw_in, gw_att, gw_conv, gw_out, r_small = _reduce_pair(halves, small)
    gs = _sum_small(r_small)

    g_cw_mine = lax.dynamic_slice(gs, (3, chip * cshard), (3, cshard))
    zero2 = jnp.zeros((2, 1), F32)
    pack = lambda ng, fg, cb, cw, rb: _pack_small(ng, fg[None, :], cb, cw[0], zero2, rb[0])
    gs_mine = _pack_small(gs[0:1], gs[1:2], gs[2:3, :conv_b.shape[1]], g_cw_mine, zero2, gs[8:16, :N_REL])
    ds, ms, vs = _adamw(pack(norm_g, final_norm_g, conv_b, conv_w, rel_bias), gs_mine,
                        pack(m_norm_g, m_final_norm_g, m_conv_b, m_conv_w, m_rel_bias),
                        pack(v_norm_g, v_final_norm_g, v_conv_b, v_conv_w, v_rel_bias), "adamw_small", SMALL_ROWS)

    def unpack(a):
        return {"norm_g": a[0:1], "final_norm_g": a[1], "conv_b": a[2:3, :conv_b.shape[1]],
                "conv_w": a[3:6, :cshard][None], "rel_bias": a[8:16, :N_REL][None]}

    g_small, d_small, m_small, v_small = unpack(gs_mine), unpack(ds), unpack(ms), unpack(vs)

    big = {}
    for name, w, g, m, v, rows in (("w_in", w_in, gw_in, m_w_in, v_w_in, 128),
                                   ("w_att_out", w_att_out, gw_att, m_w_att_out, v_w_att_out, 256),
                                   ("w_conv_out", w_conv_out, gw_conv, m_w_conv_out, v_w_conv_out, 256),
                                   ("w_out", w_out, gw_out, m_w_out, v_w_out, 128)):
        dw, mw, vw = _adamw(w[0], g, m[0], v[0], "adamw_" + name, rows)
        big[name] = (g[None], dw[None], mw[None], vw[None])

    order = ["norm_g", "w_in", "rel_bias", "w_att_out", "conv_w", "conv_b", "w_conv_out", "w_out", "final_norm_g"]
    outs = [gs[LOSS_ROW, 0], grad_x[None]]
    for which, small_d in enumerate((g_small, d_small, m_small, v_small)):
        for name in order:
            outs.append(big[name][which] if name in big else small_d[name])
    return tuple(outs)
```

```python
import numpy as np
import jax
import jax.numpy as jnp
from jax import lax
from jax.experimental import pallas as pl
from jax.experimental.pallas import tpu as pltpu

F32 = jnp.float32
BF16 = jnp.bfloat16
MESH = pl.DeviceIdType.MESH

CHUNK = 64
N_LEFT = 8
HEADS = 8
HEAD_DIM = 64
D_ATT = HEADS * HEAD_DIM
MAX_REL = 128
N_REL = 2 * MAX_REL + 1
EPS = 1e-6
NEG_BIG = -1e30
ADAM_LR, ADAM_B1, ADAM_B2, ADAM_EPS, ADAM_WD, ADAM_STEP = 0.001, 0.9, 0.999, 1e-08, 0.01, 10

LANES = 128
SUBLANES = 8
VMEM_LIMIT = 56 * 1024 * 1024

QB = 2 * CHUNK
KW = N_LEFT * CHUNK + QB
DIAG = KW + QB
TQ = N_LEFT * CHUNK
TM_MID = 256
TM_MM = 512
SMALL_ROWS, SMALL_COLS = 16, 1024


def _params(*sem):
    return pltpu.CompilerParams(dimension_semantics=sem, vmem_limit_bytes=VMEM_LIMIT)


def _nt(a, b):
    return lax.dot_general(a, b, (((1,), (1,)), ((), ())), preferred_element_type=F32)


def _tn(a, b):
    return lax.dot_general(a, b, (((0,), (0,)), ((), ())), preferred_element_type=F32)


def _nn(a, b):
    return jnp.dot(a, b, preferred_element_type=F32)


def _diag_rel_index():
    d = np.arange(DIAG)
    diff = np.where(d < KW, d, d - DIAG)
    rel = N_LEFT * CHUNK - diff
    return np.clip(rel, -MAX_REL, MAX_REL) + MAX_REL


def _build_bias(diag_ref, bias_scr):
    r = lax.broadcasted_iota(jnp.int32, (QB, KW), 0) // CHUNK
    s = lax.broadcasted_iota(jnp.int32, (QB, KW), 1) // CHUNK
    allowed = (s >= r) & (s <= r + N_LEFT)
    for h in range(HEADS):
        row = jnp.broadcast_to(diag_ref[h:h + 1, :], (QB, DIAG))
        t = pltpu.roll(row, 0, 1, stride=1, stride_axis=0)
        bias_scr[h // 2, (h % 2) * QB:(h % 2 + 1) * QB, :] = jnp.where(allowed, t[:, :KW], NEG_BIG)


def _stack_heads(a, lane_hi):
    zero = jnp.zeros_like(a)
    return jnp.concatenate([jnp.where(lane_hi, zero, a), jnp.where(lane_hi, a, zero)], axis=0)


def _in_proj_gather(x, g, shard, order):
    s, d = x.shape
    tn = shard.shape[1]
    n = s // TM_MM
    hr = d // 2

    def body(order_ref, x_ref, g_ref, shard_ref, proj_ref, h_ref, wfull_ref, hbuf, wbuf, send1, recv1, send2, recv2, lsem):
        del order_ref
        j, i = pl.program_id(0), pl.program_id(1)
        x, y, c = _position()
        mine = 2 * x + y
        chips = [(x, 1 - y), (1 - x, y), (1 - x, 1 - y)]

        def theirs(k):
            return 2 * chips[k][0] + chips[k][1]

        def half_rows(half):
            return pl.ds(half * hr, hr)

        def ici(k, shard_index):
            return pltpu.make_async_remote_copy(
                src_ref=shard_ref.at[half_rows(c), :], dst_ref=wfull_ref.at[shard_index, half_rows(c), :],
                send_sem=send1.at[k], recv_sem=recv1.at[k], device_id=(*chips[k], c), device_id_type=MESH)

        def d2d(k, half):
            return pltpu.make_async_remote_copy(
                src_ref=wbuf.at[k % 2, half_rows(half), :], dst_ref=wfull_ref.at[theirs(k), half_rows(half), :],
                send_sem=send2.at[k], recv_sem=recv2.at[k], device_id=(x, y, 1 - c), device_id_type=MESH)

        def load(k, half, sem):
            return pltpu.make_async_copy(wfull_ref.at[theirs(k), half_rows(half), :],
                                         wbuf.at[k % 2, half_rows(half), :], lsem.at[sem])

        own = pltpu.make_async_copy(shard_ref, wfull_ref.at[mine], lsem.at[0])

        @pl.when((j == 0) & (i == 0))
        def _():
            own.start()
            ici(0, mine).start()
            ici(1, mine).start()

        for k in range(3):
            @pl.when((j == k) & (i == (n - 1 if k == 0 else n // 2)))
            def _(k=k):
                if k == 0:
                    ici(0, mine).wait_send()
                    ici(1, mine).wait_send()
                    ici(2, mine).start()
                if k == 2:
                    d2d(0, c).wait_send()
                ici(k, theirs(k)).wait_recv()
                load(k, c, 1).start()
                load(k, c, 1).wait()
                d2d(k, c).start()
                d2d(k, 1 - c).wait_recv()
                load(k, 1 - c, 2).start()

            @pl.when((j == k + 1) & (i == 0))
            def _(k=k):
                load(k, 1 - c, 2).wait()

        @pl.when(j == 0)
        def _():
            xv = x_ref[...]
            r = lax.rsqrt(jnp.mean(xv * xv, axis=-1, keepdims=True) + EPS)
            hv = ((xv * r) * g_ref[...]).astype(BF16)
            hbuf[i] = hv
            h_ref[...] = hv
            proj_ref[...] = _nn(hv, shard_ref[...]).astype(BF16)

        for k in range(3):
            @pl.when(j == k + 1)
            def _(k=k):
                proj_ref[...] = _nn(hbuf[i], wbuf[k % 2]).astype(BF16)

        @pl.when((j == 3) & (i == n - 1))
        def _():
            ici(2, mine).wait_send()
            d2d(1, c).wait_send()
            d2d(2, c).wait_send()
            own.wait()

    return pl.pallas_call(
        body, name="in_proj_gather",
        grid_spec=pltpu.PrefetchScalarGridSpec(
            num_scalar_prefetch=1, grid=(N_CHIPS, n),
            in_specs=[pl.BlockSpec((TM_MM, d), lambda j, i, order: (jnp.where(j == 0, i, n - 1), 0)),
                      pl.BlockSpec((1, d), lambda j, i, order: (0, 0)), pl.BlockSpec(memory_space=pltpu.VMEM)],
            out_specs=[pl.BlockSpec((TM_MM, tn), lambda j, i, order: (i, order[j])),
                       pl.BlockSpec((TM_MM, d), lambda j, i, order: (jnp.where(j == 0, i, n - 1), 0)), ANY],
            scratch_shapes=[pltpu.VMEM((n, TM_MM, d), BF16), pltpu.VMEM((2, d, tn), BF16),
                            pltpu.SemaphoreType.DMA((3,)), pltpu.SemaphoreType.DMA((3,)),
                            pltpu.SemaphoreType.DMA((3,)), pltpu.SemaphoreType.DMA((3,)), pltpu.SemaphoreType.DMA((3,))]),
        out_shape=[jax.ShapeDtypeStruct((s, N_CHIPS * tn), BF16), jax.ShapeDtypeStruct((s, d), BF16),
                   jax.ShapeDtypeStruct((N_CHIPS, d, tn), BF16)],
        compiler_params=_params("arbitrary", "arbitrary"),
    )(order, x, g, shard)


def _attn_fwd(diag, proj, shards, kinds, cw8):
    s = proj.shape[0]
    n = s // TQ
    nw = len(shards)
    scale = HEAD_DIM ** -0.5

    def body(*refs):
        diag_ref, q_ref, kp_ref, kc_ref, vp_ref, vc_ref = refs[:6]
        srcs, cw = refs[6:6 + nw], refs[6 + nw]
        o_ref, lse_ref = refs[7 + nw:9 + nw]
        dsts, cw_all = refs[9 + nw:9 + 2 * nw], refs[9 + 2 * nw]
        bias_scr, kcat, vcat = refs[10 + 2 * nw:13 + 2 * nw]
        start, forward, finish = _gather_plan(kinds, srcs, dsts, cw, cw_all, *refs[13 + 2 * nw:])
        i = pl.program_id(0)

        @pl.when(i == 0)
        def _():
            start()
            _build_bias(diag_ref, bias_scr)

        @pl.when(i == n // 2)
        def _():
            forward()

        @pl.when(i == n - 1)
        def _():
            finish()

        kcat[0:TQ, :] = kp_ref[...]
        kcat[TQ:2 * TQ, :] = kc_ref[...]
        vcat[0:TQ, :] = vp_ref[...]
        vcat[TQ:2 * TQ, :] = vc_ref[...]
        lane_hi = lax.broadcasted_iota(jnp.int32, (QB, LANES), 1) >= HEAD_DIM
        col = lax.broadcasted_iota(jnp.int32, (2 * QB, KW), 1)

        def make_block(first_tile):
            def block(b, carry):
                r0 = pl.multiple_of(b * QB, QB)
                for p in range(HEADS // 2):
                    lanes = slice(LANES * p, LANES * (p + 1))
                    q2 = _stack_heads(q_ref[pl.ds(r0, QB), lanes] * scale, lane_hi)
                    kw = kcat[pl.ds(r0, KW), lanes]
                    vw = vcat[pl.ds(r0, KW), lanes]
                    sc = _nt(q2, kw) + bias_scr[p]
                    if first_tile:
                        sc = jnp.where(col >= TQ - r0, sc, NEG_BIG)
                    m = jnp.max(sc, axis=1, keepdims=True)
                    pe = jnp.exp(sc - m)
                    l = jnp.sum(pe, axis=1, keepdims=True)
                    o2 = _nn(pe.astype(BF16), vw) / l
                    lse2 = m + jnp.log(l)
                    lse_ref[pl.ds(r0, QB), 2 * p:2 * p + 1] = lse2[0:QB, :]
                    lse_ref[pl.ds(r0, QB), 2 * p + 1:2 * p + 2] = lse2[QB:2 * QB, :]
                    o_ref[pl.ds(r0, QB), lanes] = jnp.where(lane_hi, o2[QB:2 * QB, :], o2[0:QB, :]).astype(BF16)
                return carry
            return block

        @pl.when(i == 0)
        def _():
            lax.fori_loop(0, TQ // QB, make_block(True), 0)

        @pl.when(i > 0)
        def _():
            lax.fori_loop(0, TQ // QB, make_block(False), 0)

    blk = lambda col_blk, prev: pl.BlockSpec(
        (TQ, D_ATT), (lambda i: (jnp.maximum(i - 1, 0), col_blk)) if prev else (lambda i: (i, col_blk)))
    vmem = pl.BlockSpec(memory_space=pltpu.VMEM)
    return pl.pallas_call(
        body, name="attn_fwd", grid=(n,),
        in_specs=[pl.BlockSpec((HEADS, DIAG), lambda i: (0, 0)),
                  blk(0, False), blk(1, True), blk(1, False), blk(2, True), blk(2, False)] + [vmem] * (nw + 1),
        out_specs=[pl.BlockSpec((TQ, D_ATT), lambda i: (i, 0)), pl.BlockSpec((TQ, HEADS), lambda i: (i, 0))]
        + [ANY] * (nw + 1),
        out_shape=[jax.ShapeDtypeStruct((s, D_ATT), BF16), jax.ShapeDtypeStruct((s, HEADS), F32)]
        + _gather_out_shapes(shards, kinds, cw8),
        scratch_shapes=[pltpu.VMEM((HEADS // 2, 2 * QB, KW), F32), pltpu.VMEM((2 * TQ, D_ATT), BF16),
                        pltpu.VMEM((2 * TQ, D_ATT), BF16)] + _gather_sems(nw),
        compiler_params=_params("arbitrary"),
    )(diag, proj, proj, proj, proj, proj, *shards, cw8)


def _attn_bwd(diag, proj, d_att, att, lse, parts):
    s = proj.shape[0]
    n = s // TQ
    npart = len(parts)
    scale = HEAD_DIM ** -0.5
    rel_pad = 3 * LANES

    def body(*refs):
        diag_ref, q_ref, kp_ref, kc_ref, vp_ref, vc_ref, do_ref, o_ref, lse_ref = refs[:9]
        part_refs = refs[9:9 + npart]
        dqkv_ref, dbias_ref = refs[9 + npart:11 + npart]
        slot_refs = refs[11 + npart:11 + 2 * npart]
        bias_scr, dbias_acc, kcat, vcat, dk_acc, dv_acc, dq_scr = refs[11 + 2 * npart:18 + 2 * npart]
        start, finish = _scatter_plan(part_refs, slot_refs, *refs[18 + 2 * npart:])
        i = pl.program_id(0)

        @pl.when(i == 0)
        def _():
            start()
            _build_bias(diag_ref, bias_scr)
            dbias_acc[...] = jnp.zeros_like(dbias_acc)
            dk_acc[...] = jnp.zeros_like(dk_acc)
            dv_acc[...] = jnp.zeros_like(dv_acc)

        @pl.when(i > 0)
        def _():
            dqkv_ref[:, 0:D_ATT] = dq_scr[...]
            dk_acc[0:TQ, :] = dk_acc[TQ:2 * TQ, :]
            dk_acc[TQ:2 * TQ, :] = jnp.zeros((TQ, D_ATT), F32)
            dv_acc[0:TQ, :] = dv_acc[TQ:2 * TQ, :]
            dv_acc[TQ:2 * TQ, :] = jnp.zeros((TQ, D_ATT), F32)

        @pl.when(i < n)
        def _():
            kcat[0:TQ, :] = kp_ref[...]
            kcat[TQ:2 * TQ, :] = kc_ref[...]
            vcat[0:TQ, :] = vp_ref[...]
            vcat[TQ:2 * TQ, :] = vc_ref[...]

        lane_hi = lax.broadcasted_iota(jnp.int32, (QB, LANES), 1) >= HEAD_DIM
        col = lax.broadcasted_iota(jnp.int32, (2 * QB, KW), 1)

        def make_block(first_tile):
            def block(b, carry):
                r0 = pl.multiple_of(b * QB, QB)
                for p in range(HEADS // 2):
                    lanes = slice(LANES * p, LANES * (p + 1))
                    q2 = _stack_heads(q_ref[pl.ds(r0, QB), lanes] * scale, lane_hi)
                    kw = kcat[pl.ds(r0, KW), lanes]
                    vw = vcat[pl.ds(r0, KW), lanes]
                    dop = do_ref[pl.ds(r0, QB), lanes]
                    do2 = _stack_heads(dop, lane_hi)
                    prod = dop.astype(F32) * o_ref[pl.ds(r0, QB), lanes].astype(F32)
                    delta2 = jnp.concatenate(
                        [jnp.sum(jnp.where(lane_hi, 0.0, prod), axis=1, keepdims=True),
                         jnp.sum(jnp.where(lane_hi, prod, 0.0), axis=1, keepdims=True)], axis=0)
                    lse2 = jnp.concatenate([lse_ref[pl.ds(r0, QB), 2 * p:2 * p + 1],
                                            lse_ref[pl.ds(r0, QB), 2 * p + 1:2 * p + 2]], axis=0)
                    sc = _nt(q2, kw) + bias_scr[p]
                    if first_tile:
                        sc = jnp.where(col >= TQ - r0, sc, NEG_BIG)
                    pr = jnp.exp(sc - lse2)
                    ds = pr * (_nt(do2, vw) - delta2)
                    dbias_acc[p] += ds
                    dsb = ds.astype(BF16)
                    dv_acc[pl.ds(r0, KW), lanes] += _tn(pr.astype(BF16), do2)
                    dk_acc[pl.ds(r0, KW), lanes] += _tn(dsb, q2)
                    dq2 = _nn(dsb, kw)
                    dq = jnp.where(lane_hi, dq2[QB:2 * QB, :], dq2[0:QB, :]) * scale
                    dq_scr[pl.ds(r0, QB), lanes] = dq.astype(BF16)
                return carry
            return block

        @pl.when(i == 0)
        def _():
            lax.fori_loop(0, TQ // QB, make_block(True), 0)

        @pl.when((i > 0) & (i < n))
        def _():
            lax.fori_loop(0, TQ // QB, make_block(False), 0)

        @pl.when(i > 0)
        def _():
            dqkv_ref[:, D_ATT:2 * D_ATT] = dk_acc[0:TQ, :].astype(BF16)
            dqkv_ref[:, 2 * D_ATT:3 * D_ATT] = dv_acc[0:TQ, :].astype(BF16)

        @pl.when(i == n)
        def _():
            d_iota = lax.broadcasted_iota(jnp.int32, (DIAG, rel_pad), 0)
            n_iota = lax.broadcasted_iota(jnp.int32, (DIAG, rel_pad), 1)
            diff = jnp.where(d_iota < KW, d_iota, d_iota - DIAG)
            idx = jnp.clip(N_LEFT * CHUNK - diff, -MAX_REL, MAX_REL) + MAX_REL
            onehot = (idx == n_iota).astype(F32)
            rows = []
            for hd in range(HEADS):
                acc = dbias_acc[hd // 2, (hd % 2) * QB:(hd % 2 + 1) * QB, :]
                a = jnp.concatenate([acc, jnp.zeros((QB, DIAG - KW), F32)], axis=1)
                g8 = a[0:SUBLANES, :]
                for blk in range(1, QB // SUBLANES):
                    g8 = g8 + pltpu.roll(a[blk * SUBLANES:(blk + 1) * SUBLANES, :], DIAG - blk * SUBLANES, 1)
                g1 = g8[0:1, :]
                for r in range(1, SUBLANES):
                    g1 = g1 + pltpu.roll(g8[r:r + 1, :], DIAG - r, 1)
                rows.append(g1)
            g = jnp.concatenate(rows, axis=0)
            dbias_ref[...] = jnp.dot(g, onehot, preferred_element_type=F32, precision=lax.Precision.HIGHEST)
            finish()

    last = n - 1
    cur = lambda col_blk: pl.BlockSpec((TQ, D_ATT), lambda i: (jnp.minimum(i, last), col_blk))
    prev = lambda col_blk: pl.BlockSpec((TQ, D_ATT), lambda i: (jnp.maximum(jnp.minimum(i, last) - 1, 0), col_blk))
    return pl.pallas_call(
        body, name="attn_bwd", grid=(n + 1,),
        in_specs=[pl.BlockSpec((HEADS, DIAG), lambda i: (0, 0)),
                  cur(0), prev(1), cur(1), prev(2), cur(2), cur(0), cur(0),
                  pl.BlockSpec((TQ, HEADS), lambda i: (jnp.minimum(i, last), 0))] + [ANY] * npart,
        out_specs=[pl.BlockSpec((TQ, 3 * D_ATT), lambda i: (jnp.maximum(i - 1, 0), 0)),
                   pl.BlockSpec((HEADS, rel_pad), lambda i: (0, 0))] + [ANY] * npart,
        out_shape=[jax.ShapeDtypeStruct((s, 3 * D_ATT), BF16), jax.ShapeDtypeStruct((HEADS, rel_pad), F32)]
        + _scatter_out_shapes(parts),
        scratch_shapes=[pltpu.VMEM((HEADS // 2, 2 * QB, KW), F32), pltpu.VMEM((HEADS // 2, 2 * QB, KW), F32),
                        pltpu.VMEM((2 * TQ, D_ATT), BF16), pltpu.VMEM((2 * TQ, D_ATT), BF16),
                        pltpu.VMEM((2 * TQ, D_ATT), F32), pltpu.VMEM((2 * TQ, D_ATT), F32),
                        pltpu.VMEM((TQ, D_ATT), BF16)] + _scatter_sems(npart),
        compiler_params=_params("arbitrary"),
    )(diag, proj, proj, proj, proj, proj, d_att, att, lse, *parts)


def _shift_down(a, k, halo):
    rolled = pltpu.roll(a, k, 0)
    row = lax.broadcasted_iota(jnp.int32, halo.shape, 0)
    first = jnp.where(row < k, pltpu.roll(halo, k, 0), rolled[0:SUBLANES, :])
    return jnp.concatenate([first, rolled[SUBLANES:, :]], axis=0)


def _shift_up(a, k, nxt):
    tm = a.shape[0]
    rolled = pltpu.roll(a, tm - k, 0)
    row = lax.broadcasted_iota(jnp.int32, nxt.shape, 0)
    last = jnp.where(row >= SUBLANES - k, pltpu.roll(nxt, SUBLANES - k, 0), rolled[tm - SUBLANES:, :])
    return jnp.concatenate([rolled[:tm - SUBLANES, :], last], axis=0)


def _sigmoid(v):
    return 0.5 * jnp.tanh(0.5 * v) + 0.5


def _mixer_mid(att, proj, x, tgt, w_att, w_conv, w_out, conv_w8, conv_b, fin_g):
    s, d = x.shape
    dc = D_ATT
    n = s // TM_MID
    tm = TM_MID
    n_shards = 4

    def body(att_ref, za_ref, gb_ref, gc_ref, u_ref, zc_ref, hgc_ref, hu_ref, gatt_ref, gconv_ref, x_ref, t_ref,
             watt_ref, wconv_ref, wout_ref, cw_ref, cb_ref, fg_ref,
             dpb_ref, do_ref, dx2_ref, gatt_o, gconv_o, gout_o, loss_o, gfn_o, gcb_o, gcw_o,
             acc_att, acc_conv, acc_out, carry):
        i = pl.program_id(0)
        tile = n - 1 - i

        @pl.when(i == 0)
        def _():
            acc_att[...] = jnp.zeros_like(acc_att)
            acc_conv[...] = jnp.zeros_like(acc_conv)
            acc_out[...] = jnp.zeros_like(acc_out)
            carry[...] = jnp.zeros_like(carry)
            loss_o[...] = jnp.zeros_like(loss_o)
            gfn_o[...] = jnp.zeros_like(gfn_o)
            gcb_o[...] = jnp.zeros_like(gcb_o)
            gcw_o[...] = jnp.zeros_like(gcw_o)

        att_v = att_ref[...].astype(F32)
        za = za_ref[...].astype(F32)
        sa = _sigmoid(za)
        silu_a = za * sa
        a_b = (att_v * silu_a).astype(BF16)

        gb = gb_ref[...].astype(F32)
        gc = gc_ref[...].astype(F32)
        u = u_ref[...].astype(F32)
        zc = zc_ref[...].astype(F32)
        cu = gc * u
        halo = jnp.where(tile > 0, hgc_ref[...].astype(F32) * hu_ref[...].astype(F32), 0.0)
        cu1 = _shift_down(cu, 1, halo)
        cu2 = _shift_down(cu, 2, halo)
        w0, w1, w2 = cw_ref[0:1, :], cw_ref[1:2, :], cw_ref[2:3, :]
        vconv = w0 * cu2 + w1 * cu1 + w2 * cu + cb_ref[...]
        sc = _sigmoid(zc)
        silu_c = zc * sc
        c_b = (gb * vconv * silu_c).astype(BF16)

        y_att = _nn(a_b, watt_ref[...])
        y_conv = _nn(c_b, wconv_ref[...])
        ga = _sigmoid(gatt_ref[...].astype(F32))
        gv = _sigmoid(gconv_ref[...].astype(F32))
        m_b = (ga * y_att + gv * y_conv).astype(BF16)
        x2 = x_ref[...] + _nn(m_b, wout_ref[...])
        r2 = lax.rsqrt(jnp.mean(x2 * x2, axis=-1, keepdims=True) + EPS)
        x2n = x2 * r2
        fg = fg_ref[...]
        err = x2n * fg - t_ref[...]
        loss_o[...] += jnp.sum(err * err, axis=0, keepdims=True) * (0.5 / d)
        dy = err * (1.0 / d)
        gfn_o[...] += jnp.sum(dy * x2n, axis=0, keepdims=True)
        dyn = dy * fg
        dx2 = r2 * (dyn - x2n * jnp.mean(dyn * x2n, axis=-1, keepdims=True))
        dx2_ref[...] = dx2
        dx2_b = dx2.astype(BF16)

        dm = _nt(dx2_b, wout_ref[...])
        acc_out[...] += _tn(m_b, dx2_b)
        dy_att = dm * ga
        dy_conv = dm * gv
        dpb_ref[:, 5 * dc:5 * dc + d] = (dy_att * y_att * (1.0 - ga)).astype(BF16)
        dpb_ref[:, 5 * dc + d:5 * dc + 2 * d] = (dy_conv * y_conv * (1.0 - gv)).astype(BF16)
        dya_b = dy_att.astype(BF16)
        dyc_b = dy_conv.astype(BF16)
        da_in = _nt(dya_b, watt_ref[...])
        acc_att[...] += _tn(a_b, dya_b)
        dc_in = _nt(dyc_b, wconv_ref[...])
        acc_conv[...] += _tn(c_b, dyc_b)

        do_ref[...] = (da_in * silu_a).astype(BF16)
        dpb_ref[:, 0:dc] = (da_in * att_v * (sa * (1.0 + za * (1.0 - sa)))).astype(BF16)
        dpb_ref[:, dc:2 * dc] = (dc_in * vconv * silu_c).astype(BF16)
        dgs = dc_in * gb
        dvc = dgs * silu_c
        dpb_ref[:, 4 * dc:5 * dc] = (dgs * vconv * (sc * (1.0 + zc * (1.0 - sc)))).astype(BF16)
        gcb_o[...] += jnp.sum(dvc, axis=0, keepdims=True)
        gcw_o[0:1, :] += jnp.sum(dvc * cu2, axis=0, keepdims=True)
        gcw_o[1:2, :] += jnp.sum(dvc * cu1, axis=0, keepdims=True)
        gcw_o[2:3, :] += jnp.sum(dvc * cu, axis=0, keepdims=True)
        nxt = carry[...]
        dcu = w2 * dvc + w1 * _shift_up(dvc, 1, nxt) + w0 * _shift_up(dvc, 2, nxt)
        carry[...] = dvc[0:SUBLANES, :]
        dpb_ref[:, 2 * dc:3 * dc] = (dcu * u).astype(BF16)
        dpb_ref[:, 3 * dc:4 * dc] = (dcu * gc).astype(BF16)

        @pl.when(i == n - 1)
        def _():
            for j in range(n_shards):
                gatt_o[j] = acc_att[:, j * (d // n_shards):(j + 1) * (d // n_shards)].astype(BF16)
                gconv_o[j] = acc_conv[:, j * (d // n_shards):(j + 1) * (d // n_shards)].astype(BF16)
                gout_o[j] = acc_out[j * (d // n_shards):(j + 1) * (d // n_shards), :].astype(BF16)

    rev = lambda width, col_blk: pl.BlockSpec((tm, width), lambda i: (n - 1 - i, col_blk))
    halo_spec = lambda col_blk: pl.BlockSpec(
        (SUBLANES, dc), lambda i: (jnp.maximum((n - 1 - i) * (tm // SUBLANES) - 1, 0), col_blk))
    const = lambda shape: pl.BlockSpec(shape, lambda i: tuple(0 for _ in shape), pipeline_mode=pl.Buffered(1))
    q4 = d // n_shards
    return pl.pallas_call(
        body, name="mixer_mid", grid=(n,),
        in_specs=[rev(dc, 0), rev(dc, 3), rev(dc, 4), rev(dc, 5), rev(dc, 6), rev(dc, 7),
                  halo_spec(5), halo_spec(6), rev(d, 4), rev(d, 5), rev(d, 0), rev(d, 0),
                  const((dc, d)), const((dc, d)), const((d, d)), const((SUBLANES, dc)), const((1, dc)), const((1, d))],
        out_specs=[rev(5 * dc + 2 * d, 0), rev(dc, 0), rev(d, 0),
                   const((n_shards, dc, q4)), const((n_shards, dc, q4)), const((n_shards, q4, d)),
                   const((1, d)), const((1, d)), const((1, dc)), const((SUBLANES, dc))],
        out_shape=[jax.ShapeDtypeStruct((s, 5 * dc + 2 * d), BF16), jax.ShapeDtypeStruct((s, dc), BF16),
                   jax.ShapeDtypeStruct((s, d), F32),
                   jax.ShapeDtypeStruct((n_shards, dc, q4), BF16), jax.ShapeDtypeStruct((n_shards, dc, q4), BF16),
                   jax.ShapeDtypeStruct((n_shards, q4, d), BF16),
                   jax.ShapeDtypeStruct((1, d), F32), jax.ShapeDtypeStruct((1, d), F32),
                   jax.ShapeDtypeStruct((1, dc), F32), jax.ShapeDtypeStruct((SUBLANES, dc), F32)],
        scratch_shapes=[pltpu.VMEM((dc, d), F32), pltpu.VMEM((dc, d), F32), pltpu.VMEM((d, d), F32),
                        pltpu.VMEM((SUBLANES, dc), F32)],
        compiler_params=_params("arbitrary"),
    )(att, proj, proj, proj, proj, proj, proj, proj, proj, proj, x, tgt,
      w_att, w_conv, w_out, conv_w8, conv_b, fin_g)


def _in_proj_bwd_x(dqkv, dpb, w_in, x, dx2, g):
    s, d = x.shape
    tn = dqkv.shape[1]
    nb = dpb.shape[1] // tn
    n = s // TM_MM

    def body(*refs):
        dps, ws = refs[:nb + 1], refs[nb + 1:2 * nb + 2]
        x_ref, dx2_ref, g_ref, gx_ref, gng_ref = refs[2 * nb + 2:]
        i = pl.program_id(0)

        @pl.when(i == 0)
        def _():
            gng_ref[...] = jnp.zeros_like(gng_ref)

        dh = _nt(dps[0][...], ws[0][0])
        for j in range(1, nb + 1):
            dh = dh + _nt(dps[j][...], ws[j][0])
        xv = x_ref[...]
        r = lax.rsqrt(jnp.mean(xv * xv, axis=-1, keepdims=True) + EPS)
        xn = xv * r
        gng_ref[...] += jnp.sum(dh * xn, axis=0, keepdims=True)
        dhn = dh * g_ref[...]
        gx_ref[...] = dx2_ref[...] + r * (dhn - xn * jnp.mean(dhn * xn, axis=-1, keepdims=True))

    tile = lambda width, col_blk: pl.BlockSpec((TM_MM, width), lambda i: (i, col_blk))
    wspec = lambda blk: pl.BlockSpec((1, d, tn), lambda i: (blk, 0, 0), pipeline_mode=pl.Buffered(1))
    return pl.pallas_call(
        body, name="in_proj_bwd_x", grid=(n,),
        in_specs=[tile(tn, 0)] + [tile(tn, j) for j in range(nb)] + [wspec(j) for j in range(nb + 1)]
        + [tile(d, 0), tile(d, 0), pl.BlockSpec((1, d), lambda i: (0, 0))],
        out_specs=[tile(d, 0), pl.BlockSpec((1, d), lambda i: (0, 0))],
        out_shape=[jax.ShapeDtypeStruct((s, d), F32), jax.ShapeDtypeStruct((1, d), F32)],
        compiler_params=_params("arbitrary"),
    )(dqkv, *([dpb] * nb), *([w_in] * (nb + 1)), x, dx2, g)


def _in_proj_bwd_w(h, dqkv, dpb, order):
    s, d = h.shape
    tn = dqkv.shape[1]
    n = s // TM_MM
    hr = d // 2
    settle = min(2, n - 1)

    def body(order_ref, h_ref, da_ref, db_ref, slots_ref, acc, sendbuf, pairbuf, chipbuf, psend, precv, send, recv, lsem):
        j, i = pl.program_id(0), pl.program_id(1)
        blk = order_ref[j]
        pos = _position()
        x, y, c = pos

        @pl.when(i == 0)
        def _():
            acc[...] = jnp.zeros_like(acc)

        @pl.when(blk == 0)
        def _():
            acc[...] += _tn(h_ref[...], da_ref[...])

        @pl.when(blk > 0)
        def _():
            acc[...] += _tn(h_ref[...], db_ref[...])

        def pair(step, half):
            return pltpu.make_async_remote_copy(
                src_ref=sendbuf.at[step, pl.ds(half * hr, hr), :], dst_ref=pairbuf.at[step],
                send_sem=psend.at[step], recv_sem=precv.at[step], device_id=(x, y, 1 - c), device_id_type=MESH)

        def ici(step):
            flip = step + 1
            return pltpu.make_async_remote_copy(
                src_ref=chipbuf.at[step], dst_ref=slots_ref.at[flip], send_sem=send.at[step], recv_sem=recv.at[step],
                device_id=_peer(pos, 4 * (flip >> 1) + 2 * (flip & 1)), device_id_type=MESH)

        local = pltpu.make_async_copy(chipbuf.at[N_CHIPS - 1], slots_ref.at[0], lsem.at[0])

        def combine(step):
            pair(step, c).wait_recv()
            mine = sendbuf[step, pl.ds(c * hr, hr), :].astype(F32)
            chipbuf[step] = (mine + pairbuf[step].astype(F32)).astype(BF16)

        for step in range(N_CHIPS):
            @pl.when((j == step) & (i == n - 1))
            def _(step=step):
                sendbuf[step] = acc[...].astype(BF16)
                pair(step, 1 - c).start()

        for step in range(N_CHIPS - 1):
            @pl.when((j == step + 1) & (i == settle))
            def _(step=step):
                combine(step)
                ici(step).start()

        @pl.when((j == N_CHIPS - 1) & (i == n - 1))
        def _():
            combine(N_CHIPS - 1)
            local.start()
            for step in range(N_CHIPS - 1):
                ici(step).wait_recv()
            for step in range(N_CHIPS - 1):
                ici(step).wait_send()
            for step in range(N_CHIPS):
                pair(step, 1 - c).wait_send()
            local.wait()

    return pl.pallas_call(
        body, name="in_proj_bwd_w",
        grid_spec=pltpu.PrefetchScalarGridSpec(
            num_scalar_prefetch=1, grid=(N_CHIPS, n),
            in_specs=[pl.BlockSpec((TM_MM, d), lambda j, i, order: (i, 0)),
                      pl.BlockSpec((TM_MM, tn), lambda j, i, order: (jnp.where(order[j] == 0, i, 0), 0)),
                      pl.BlockSpec((TM_MM, tn), lambda j, i, order: (jnp.where(order[j] == 0, 0, i),
                                                                     jnp.maximum(order[j] - 1, 0)))],
            out_specs=[ANY],
            scratch_shapes=[pltpu.VMEM((d, tn), F32), pltpu.VMEM((N_CHIPS, d, tn), BF16),
                            pltpu.VMEM((N_CHIPS, hr, tn), BF16), pltpu.VMEM((N_CHIPS, hr, tn), BF16),
                            pltpu.SemaphoreType.DMA((N_CHIPS,)), pltpu.SemaphoreType.DMA((N_CHIPS,)),
                            pltpu.SemaphoreType.DMA((N_CHIPS - 1,)), pltpu.SemaphoreType.DMA((N_CHIPS - 1,)),
                            pltpu.SemaphoreType.DMA((1,))]),
        out_shape=[jax.ShapeDtypeStruct((N_CHIPS, hr, tn), BF16)],
        compiler_params=_params("arbitrary", "arbitrary"),
    )(order, h, dqkv, dpb)[0]


LOSS_ROW = 6


def _adam_update(w, g, m, v):
    c1 = 1.0 / (1.0 - ADAM_B1 ** ADAM_STEP)
    c2 = 1.0 / (1.0 - ADAM_B2 ** ADAM_STEP)
    m2 = ADAM_B1 * m + (1.0 - ADAM_B1) * g
    v2 = ADAM_B2 * v + (1.0 - ADAM_B2) * (g * g)
    return -ADAM_LR * ((m2 * c1) / (jnp.sqrt(v2 * c2) + ADAM_EPS) + ADAM_WD * w), m2, v2


def _adamw_small(recv, params, moments_m, moments_v):
    k = recv.shape[0]
    n_par = len(params)
    cshard = params[3].shape[1]

    def body(*refs):
        r_ref = refs[0]
        ws, ms, vs = refs[1:1 + n_par], refs[1 + n_par:1 + 2 * n_par], refs[1 + 2 * n_par:1 + 3 * n_par]
        loss_ref = refs[1 + 3 * n_par]
        outs = refs[2 + 3 * n_par:]
        total = r_ref[0]
        for slot in range(1, k):
            total = total + r_ref[slot]
        loss_ref[...] = jnp.sum(total[LOSS_ROW:LOSS_ROW + 1, :], axis=1, keepdims=True)
        chip = 2 * lax.axis_index("x") + lax.axis_index("y")
        g_cw = jnp.zeros((3, cshard), F32)
        for sh in range(N_CHIPS):
            g_cw = g_cw + jnp.where(chip == sh, total[3:6, sh * cshard:(sh + 1) * cshard], 0.0)
        grads = [total[0:1, :], total[1:2, :], total[2:3, :ws[2].shape[1]], g_cw, total[8:16, :ws[4].shape[1]]]
        for p in range(n_par):
            delta, m2, v2 = _adam_update(ws[p][...], grads[p], ms[p][...], vs[p][...])
            for q, val in enumerate((grads[p], delta, m2, v2)):
                outs[4 * p + q][...] = val

    shapes = [jax.ShapeDtypeStruct((1, 1), F32)]
    for p in params:
        shapes += [jax.ShapeDtypeStruct(p.shape, F32)] * 4
    return pl.pallas_call(body, name="adamw_small", out_shape=shapes)(recv, *params, *moments_m, *moments_v)


def _adamw(w, g, m, v, name, rows_per_step):
    r, c = w.shape

    def body(w_ref, g_ref, m_ref, v_ref, d_ref, mo_ref, vo_ref):
        d_ref[...], mo_ref[...], vo_ref[...] = _adam_update(w_ref[...], g_ref[...], m_ref[...], v_ref[...])

    spec = pl.BlockSpec((rows_per_step, c), lambda i: (i, 0))
    shape = jax.ShapeDtypeStruct((r, c), F32)
    return pl.pallas_call(
        body, name=name, grid=(r // rows_per_step,),
        in_specs=[spec] * 4, out_specs=[spec] * 3, out_shape=[shape] * 3,
        compiler_params=_params("parallel"),
    )(w, g, m, v)


ANY = pl.BlockSpec(memory_space=pl.ANY)
N_CHIPS = 4
N_DEV = 8


def _position():
    return lax.axis_index("x"), lax.axis_index("y"), lax.axis_index("c")


def _gather_out_shapes(shards, kinds, cw8):
    full = [(a.shape[0], a.shape[1] * N_CHIPS) if k == "cols" else (a.shape[0] * N_CHIPS, a.shape[1])
            for a, k in zip(shards, kinds)]
    return [jax.ShapeDtypeStruct(f, a.dtype) for f, a in zip(full, shards)] + [
        jax.ShapeDtypeStruct((N_CHIPS,) + cw8.shape, cw8.dtype)]


def _gather_sems(nw):
    return [pltpu.SemaphoreType.DMA((3, nw)), pltpu.SemaphoreType.DMA((3, nw)),
            pltpu.SemaphoreType.DMA((3, nw)), pltpu.SemaphoreType.DMA((3, nw)),
            pltpu.SemaphoreType.DMA((3,)), pltpu.SemaphoreType.DMA((3,)), pltpu.SemaphoreType.DMA((nw + 1,))]


def _gather_plan(kinds, srcs, dsts, cw, cw_all, send1, recv1, send2, recv2, ssend, srecv, lsem):
    nw = len(srcs)
    x, y, c = _position()
    mine = 2 * x + y
    chips = [(x, 1 - y), (1 - x, y), (1 - x, 1 - y)]

    def window(w, shard, half):
        r, cc = srcs[w].shape
        hr = r // 2
        if kinds[w] == "cols":
            rows = pl.ds(0, r) if half is None else pl.ds(half * hr, hr)
            return dsts[w].at[rows, pl.ds(shard * cc, cc)]
        rows = pl.ds(shard * r, r) if half is None else pl.ds(shard * r + half * hr, hr)
        return dsts[w].at[rows, :]

    def my_half(w):
        hr = srcs[w].shape[0] // 2
        return srcs[w].at[pl.ds(c * hr, hr), :]

    def local():
        return [pltpu.make_async_copy(srcs[w], window(w, mine, None), lsem.at[w]) for w in range(nw)] + [
            pltpu.make_async_copy(cw, cw_all.at[mine], lsem.at[nw])]

    def ici(k, w, shard):
        kx, ky = chips[k]
        return pltpu.make_async_remote_copy(
            src_ref=my_half(w), dst_ref=window(w, shard, c), send_sem=send1.at[k, w], recv_sem=recv1.at[k, w],
            device_id=(kx, ky, c), device_id_type=MESH)

    def d2d(k, w, shard, half):
        return pltpu.make_async_remote_copy(
            src_ref=window(w, shard, half), dst_ref=window(w, shard, half),
            send_sem=send2.at[k, w], recv_sem=recv2.at[k, w], device_id=(x, y, 1 - c), device_id_type=MESH)

    def small(k, shard):
        kx, ky = chips[k]
        return pltpu.make_async_remote_copy(
            src_ref=cw, dst_ref=cw_all.at[shard], send_sem=ssend.at[k], recv_sem=srecv.at[k],
            device_id=(kx, ky, c), device_id_type=MESH)

    def theirs(k):
        kx, ky = chips[k]
        return 2 * kx + ky

    def start():
        for cp in local():
            cp.start()
        for k in range(3):
            for w in range(nw):
                ici(k, w, mine).start()
            small(k, mine).start()

    def forward():
        for k in range(3):
            for w in range(nw):
                ici(k, w, theirs(k)).wait_recv()
                d2d(k, w, theirs(k), c).start()

    def finish():
        for k in range(3):
            for w in range(nw):
                d2d(k, w, theirs(k), 1 - c).wait_recv()
            small(k, theirs(k)).wait_recv()
        for k in range(3):
            for w in range(nw):
                ici(k, w, mine).wait_send()
                d2d(k, w, theirs(k), c).wait_send()
            small(k, mine).wait_send()
        for cp in local():
            cp.wait()

    return start, forward, finish


def _scatter_out_shapes(parts):
    return [jax.ShapeDtypeStruct((N_DEV, p.shape[1] // 2, p.shape[2]), p.dtype) for p in parts]


def _scatter_sems(nw):
    return [pltpu.SemaphoreType.DMA((N_DEV, nw)), pltpu.SemaphoreType.DMA((N_DEV, nw)), pltpu.SemaphoreType.DMA((nw,))]


def _peer(pos, k):
    x, y, c = pos
    return ((1 - x) if k & 4 else x, (1 - y) if k & 2 else y, (1 - c) if k & 1 else c)


def _scatter_plan(srcs, dsts, send, recv, lsem):
    nw = len(srcs)
    pos = _position()

    def piece(w, k):
        px, py, pc = _peer(pos, k)
        hr = srcs[w].shape[1] // 2
        return srcs[w].at[2 * px + py, pl.ds(pc * hr, hr), :]

    def remote(w, k):
        return pltpu.make_async_remote_copy(
            src_ref=piece(w, k), dst_ref=dsts[w].at[k], send_sem=send.at[k, w], recv_sem=recv.at[k, w],
            device_id=_peer(pos, k), device_id_type=MESH)

    def local(w):
        return pltpu.make_async_copy(piece(w, 0), dsts[w].at[0], lsem.at[w])

    def start():
        for w in range(nw):
            local(w).start()
        for k in range(1, N_DEV):
            for w in range(nw):
                remote(w, k).start()

    def finish():
        for k in range(1, N_DEV):
            for w in range(nw):
                remote(w, k).wait_recv()
        for k in range(1, N_DEV):
            for w in range(nw):
                remote(w, k).wait_send()
        for w in range(nw):
            local(w).wait()

    return start, finish


def _reduce_pair(slots, small):
    nw = len(slots)

    def body(*refs):
        srcs, sm = refs[:nw], refs[nw]
        dsts, sm_all = refs[nw + 1:2 * nw + 1], refs[2 * nw + 1]
        halves = refs[2 * nw + 2:3 * nw + 2]
        send, recv, ssend, srecv, lsem = refs[3 * nw + 2:]
        pos = _position()
        x, y, c = pos
        me = 4 * x + 2 * y + c

        def rows(w, half):
            hr = halves[w].shape[0]
            return dsts[w].at[pl.ds(half * hr, hr), :]

        def remote(w, half):
            return pltpu.make_async_remote_copy(
                src_ref=halves[w], dst_ref=rows(w, half), send_sem=send.at[w], recv_sem=recv.at[w],
                device_id=(x, y, 1 - c), device_id_type=MESH)

        def bcast(k, slot):
            return pltpu.make_async_remote_copy(
                src_ref=sm, dst_ref=sm_all.at[slot], send_sem=ssend.at[k], recv_sem=srecv.at[k],
                device_id=_peer(pos, k), device_id_type=MESH)

        small_copies = [bcast(k, me) for k in range(1, N_DEV)]
        own_small = pltpu.make_async_copy(sm, sm_all.at[me], lsem.at[nw])
        for cp in small_copies + [own_small]:
            cp.start()
        big = []
        for w in range(nw):
            total = srcs[w][0].astype(F32)
            for k in range(1, srcs[w].shape[0]):
                total = total + srcs[w][k].astype(F32)
            halves[w][...] = total
            big += [remote(w, c), pltpu.make_async_copy(halves[w], rows(w, c), lsem.at[w])]
            big[-2].start()
            big[-1].start()
        for w in range(nw):
            remote(w, 1 - c).wait_recv()
        for k in range(1, N_DEV):
            px, py, pc = _peer(pos, k)
            bcast(k, 4 * px + 2 * py + pc).wait_recv()
        for w in range(nw):
            big[2 * w].wait_send()
            big[2 * w + 1].wait()
        for cp in small_copies:
            cp.wait_send()
        own_small.wait()

    vmem = pl.BlockSpec(memory_space=pltpu.VMEM)
    half_shapes = [(sl.shape[1], sl.shape[2]) for sl in slots]
    return pl.pallas_call(
        body, name="reduce_pair",
        in_specs=[vmem] * (nw + 1), out_specs=[ANY] * (nw + 1),
        out_shape=[jax.ShapeDtypeStruct((2 * r, cc), F32) for r, cc in half_shapes]
        + [jax.ShapeDtypeStruct((N_DEV,) + small.shape, small.dtype)],
        scratch_shapes=[pltpu.VMEM(hs, F32) for hs in half_shapes]
        + [pltpu.SemaphoreType.DMA((nw,)), pltpu.SemaphoreType.DMA((nw,)),
           pltpu.SemaphoreType.DMA((N_DEV,)), pltpu.SemaphoreType.DMA((N_DEV,)),
           pltpu.SemaphoreType.DMA((nw + 1,))],
        compiler_params=pltpu.CompilerParams(vmem_limit_bytes=VMEM_LIMIT),
    )(*slots, small)


def _pad_to(a, rows, cols):
    return jnp.pad(a, ((0, rows - a.shape[0]), (0, cols - a.shape[1])))


def _pack_small(norm_g, fin_g, conv_b, conv_w, loss_vec, rel):
    rows = [_pad_to(norm_g, 1, SMALL_COLS), _pad_to(fin_g, 1, SMALL_COLS), _pad_to(conv_b, 1, SMALL_COLS),
            _pad_to(conv_w, 3, SMALL_COLS), _pad_to(loss_vec, 2, SMALL_COLS), _pad_to(rel, HEADS, SMALL_COLS)]
    return jnp.concatenate(rows, axis=0)


def kernel(x, norm_g, w_in, rel_bias, w_att_out, conv_w, conv_b, w_conv_out, w_out, final_norm_g, loss_target, m_norm_g, m_w_in, m_rel_bias, m_w_att_out, m_conv_w, m_conv_b, m_w_conv_out, m_w_out, m_final_norm_g, v_norm_g, v_w_in, v_rel_bias, v_w_att_out, v_conv_w, v_conv_b, v_w_conv_out, v_w_out, v_final_norm_g):
    xs, tgt = x[0], loss_target[0]
    cshard = conv_w.shape[2]
    chip = 2 * lax.axis_index("x") + lax.axis_index("y")

    shards = [w_in[0].astype(BF16), w_att_out[0].astype(BF16), w_conv_out[0].astype(BF16), w_out[0].astype(BF16)]
    cw8 = _pad_to(conv_w[0], SUBLANES, cshard)
    flips = jnp.arange(N_CHIPS, dtype=jnp.int32)
    own_first = jnp.bitwise_xor(chip, flips)
    own_last = jnp.bitwise_xor(chip, (flips + 1) % N_CHIPS)

    proj, h, wb_in = _in_proj_gather(xs, norm_g, shards[0], own_first)
    diag = jnp.take(rel_bias[0], _diag_rel_index(), axis=1)
    att, lse, wb_att, wb_conv, wb_out, cw_all = _attn_fwd(diag, proj, shards[1:], ["cols", "cols", "rows"], cw8)
    conv_w_full = jnp.transpose(cw_all, (1, 0, 2)).reshape(SUBLANES, N_CHIPS * cshard)
    (dpb, d_att, dx2, g_att_p, g_conv_p, g_out_p, loss_vec, g_fin, g_cb, g_cw) = _mixer_mid(
        att, proj, xs, tgt, wb_att, wb_conv, wb_out, conv_w_full, conv_b, final_norm_g[None, :])
    dqkv, g_rel, r_att, r_conv, r_out = _attn_bwd(diag, proj, d_att, att, lse, [g_att_p, g_conv_p, g_out_p])
    grad_x, g_norm = _in_proj_bwd_x(dqkv, dpb, wb_in, xs, dx2, norm_g)
    r_in = _in_proj_bwd_w(h, dqkv, dpb, own_last)

    small = _pack_small(g_norm, g_fin, g_cb, g_cw[0:3], loss_vec, g_rel)
    gw_in, gw_att, gw_conv, gw_out, r_small = _reduce_pair([r_in, r_att, r_conv, r_out], small)
    rel_pad = g_rel.shape[1]
    pad_rel = lambda a: _pad_to(a[0], HEADS, rel_pad)
    small_out = _adamw_small(
        r_small,
        [norm_g, final_norm_g[None, :], conv_b, conv_w[0], pad_rel(rel_bias)],
        [m_norm_g, m_final_norm_g[None, :], m_conv_b, m_conv_w[0], pad_rel(m_rel_bias)],
        [v_norm_g, v_final_norm_g[None, :], v_conv_b, v_conv_w[0], pad_rel(v_rel_bias)])
    loss = small_out[0][0, 0]
    small_names = ["norm_g", "final_norm_g", "conv_b", "conv_w", "rel_bias"]
    fix = {"norm_g": lambda a: a, "final_norm_g": lambda a: a[0], "conv_b": lambda a: a,
           "conv_w": lambda a: a[None], "rel_bias": lambda a: a[None, :, :N_REL]}
    small_res = {name: [fix[name](small_out[1 + 4 * p + q]) for q in range(4)] for p, name in enumerate(small_names)}

    big = {}
    for name, w, g, m, v, rows in (("w_in", w_in, gw_in, m_w_in, v_w_in, 128),
                                   ("w_att_out", w_att_out, gw_att, m_w_att_out, v_w_att_out, 256),
                                   ("w_conv_out", w_conv_out, gw_conv, m_w_conv_out, v_w_conv_out, 256),
                                   ("w_out", w_out, gw_out, m_w_out, v_w_out, 128)):
        dw, mw, vw = _adamw(w[0], g, m[0], v[0], "adamw_" + name, rows)
        big[name] = (g[None], dw[None], mw[None], vw[None])

    order = ["norm_g", "w_in", "rel_bias", "w_att_out", "conv_w", "conv_b", "w_conv_out", "w_out", "final_norm_g"]
    outs = [loss, grad_x[None]]
    for which in range(4):
        for name in order:
            outs.append(big[name][which] if name in big else small_res[name][which])
    return tuple(outs)
```

```python
import numpy as np
import jax
import jax.numpy as jnp
from jax import lax
from jax.experimental import pallas as pl
from jax.experimental.pallas import tpu as pltpu

F32 = jnp.float32
BF16 = jnp.bfloat16
MESH = pl.DeviceIdType.MESH

CHUNK = 64
N_LEFT = 8
HEADS = 8
HEAD_DIM = 64
D_ATT = HEADS * HEAD_DIM
MAX_REL = 128
N_REL = 2 * MAX_REL + 1
EPS = 1e-6
NEG_BIG = -1e30
ADAM_LR, ADAM_B1, ADAM_B2, ADAM_EPS, ADAM_WD, ADAM_STEP = 0.001, 0.9, 0.999, 1e-08, 0.01, 10

LANES = 128
SUBLANES = 8
VMEM_LIMIT = 56 * 1024 * 1024

QB = 2 * CHUNK
KW = N_LEFT * CHUNK + QB
DIAG = KW + QB
TQ = N_LEFT * CHUNK
TM_MID = 256
TM_MM = 512
TM_BLK = 1024
SMALL_ROWS, SMALL_COLS = 16, 1024


def _params(*sem):
    return pltpu.CompilerParams(dimension_semantics=sem, vmem_limit_bytes=VMEM_LIMIT)


def _nt(a, b):
    return lax.dot_general(a, b, (((1,), (1,)), ((), ())), preferred_element_type=F32)


def _tn(a, b):
    return lax.dot_general(a, b, (((0,), (0,)), ((), ())), preferred_element_type=F32)


def _nn(a, b):
    return jnp.dot(a, b, preferred_element_type=F32)


def _diag_rel_index():
    d = np.arange(DIAG)
    diff = np.where(d < KW, d, d - DIAG)
    rel = N_LEFT * CHUNK - diff
    return np.clip(rel, -MAX_REL, MAX_REL) + MAX_REL


def _build_bias(diag_ref, bias_scr):
    r = lax.broadcasted_iota(jnp.int32, (QB, KW), 0) // CHUNK
    s = lax.broadcasted_iota(jnp.int32, (QB, KW), 1) // CHUNK
    allowed = (s >= r) & (s <= r + N_LEFT)
    for h in range(HEADS):
        row = jnp.broadcast_to(diag_ref[h:h + 1, :], (QB, DIAG))
        t = pltpu.roll(row, 0, 1, stride=1, stride_axis=0)
        bias_scr[h // 2, (h % 2) * QB:(h % 2 + 1) * QB, :] = jnp.where(allowed, t[:, :KW], NEG_BIG)


def _stack_heads(a, lane_hi):
    zero = jnp.zeros_like(a)
    return jnp.concatenate([jnp.where(lane_hi, zero, a), jnp.where(lane_hi, a, zero)], axis=0)


def _in_proj_gather(x, g, shard, order):
    s, d = x.shape
    tn = shard.shape[1]
    n = s // TM_BLK
    hr = d // 2

    def body(order_ref, x_ref, g_ref, shard_ref, proj_ref, h_ref, wfull_ref, hbuf, wbuf, send1, recv1, send2, recv2, lsem):
        del order_ref
        j, i = pl.program_id(0), pl.program_id(1)
        x, y, c = _position()
        mine = 2 * x + y
        chips = [(x, 1 - y), (1 - x, y), (1 - x, 1 - y)]

        def theirs(k):
            return 2 * chips[k][0] + chips[k][1]

        def half_rows(half):
            return pl.ds(half * hr, hr)

        def ici(k, shard_index):
            return pltpu.make_async_remote_copy(
                src_ref=shard_ref.at[half_rows(c), :], dst_ref=wfull_ref.at[shard_index, half_rows(c), :],
                send_sem=send1.at[k], recv_sem=recv1.at[k], device_id=(*chips[k], c), device_id_type=MESH)

        def d2d(k, half):
            return pltpu.make_async_remote_copy(
                src_ref=wbuf.at[k % 2, half_rows(half), :], dst_ref=wfull_ref.at[theirs(k), half_rows(half), :],
                send_sem=send2.at[k], recv_sem=recv2.at[k], device_id=(x, y, 1 - c), device_id_type=MESH)

        def load(k, half, sem):
            return pltpu.make_async_copy(wfull_ref.at[theirs(k), half_rows(half), :],
                                         wbuf.at[k % 2, half_rows(half), :], lsem.at[sem])

        own = pltpu.make_async_copy(shard_ref, wfull_ref.at[mine], lsem.at[0])

        @pl.when((j == 0) & (i == 0))
        def _():
            own.start()
            ici(0, mine).start()
            ici(1, mine).start()

        for k in range(3):
            first = max(n - 3, 0) if k == 0 else min(n // 2, n - 1)

            @pl.when((j == k) & (i == first))
            def _(k=k):
                if k == 0:
                    ici(0, mine).wait_send()
                    ici(1, mine).wait_send()
                    ici(2, mine).start()
                if k == 2:
                    d2d(0, c).wait_send()
                ici(k, theirs(k)).wait_recv()
                load(k, c, 1).start()

            @pl.when((j == k) & (i == min(first + 1, n - 1)))
            def _(k=k):
                load(k, c, 1).wait()
                d2d(k, c).start()

            @pl.when((j == k) & (i == min(first + 2, n - 1)))
            def _(k=k):
                d2d(k, 1 - c).wait_recv()
                load(k, 1 - c, 2).start()

            @pl.when((j == k + 1) & (i == 0))
            def _(k=k):
                load(k, 1 - c, 2).wait()

        @pl.when(j == 0)
        def _():
            xv = x_ref[...]
            r = lax.rsqrt(jnp.mean(xv * xv, axis=-1, keepdims=True) + EPS)
            hv = ((xv * r) * g_ref[...]).astype(BF16)
            hbuf[i] = hv
            h_ref[...] = hv
            proj_ref[...] = _nn(hv, shard_ref[...]).astype(BF16)

        for k in range(3):
            @pl.when(j == k + 1)
            def _(k=k):
                proj_ref[...] = _nn(hbuf[i], wbuf[k % 2]).astype(BF16)

        @pl.when((j == 3) & (i == n - 1))
        def _():
            ici(2, mine).wait_send()
            d2d(1, c).wait_send()
            d2d(2, c).wait_send()
            own.wait()

    return pl.pallas_call(
        body, name="in_proj_gather",
        grid_spec=pltpu.PrefetchScalarGridSpec(
            num_scalar_prefetch=1, grid=(N_CHIPS, n),
            in_specs=[pl.BlockSpec((TM_BLK, d), lambda j, i, order: (jnp.where(j == 0, i, n - 1), 0)),
                      pl.BlockSpec((1, d), lambda j, i, order: (0, 0)), pl.BlockSpec(memory_space=pltpu.VMEM)],
            out_specs=[pl.BlockSpec((TM_BLK, tn), lambda j, i, order: (i, order[j])),
                       pl.BlockSpec((TM_BLK, d), lambda j, i, order: (jnp.where(j == 0, i, n - 1), 0)), ANY],
            scratch_shapes=[pltpu.VMEM((n, TM_BLK, d), BF16), pltpu.VMEM((2, d, tn), BF16),
                            pltpu.SemaphoreType.DMA((3,)), pltpu.SemaphoreType.DMA((3,)),
                            pltpu.SemaphoreType.DMA((3,)), pltpu.SemaphoreType.DMA((3,)), pltpu.SemaphoreType.DMA((3,))]),
        out_shape=[jax.ShapeDtypeStruct((s, N_CHIPS * tn), BF16), jax.ShapeDtypeStruct((s, d), BF16),
                   jax.ShapeDtypeStruct((N_CHIPS, d, tn), BF16)],
        compiler_params=_params("arbitrary", "arbitrary"),
    )(order, x, g, shard)


def _attn_fwd(diag, proj, shards, kinds, cw8):
    s = proj.shape[0]
    n = s // TQ
    nw = len(shards)
    scale = HEAD_DIM ** -0.5

    def body(*refs):
        diag_ref, q_ref, kp_ref, kc_ref, vp_ref, vc_ref = refs[:6]
        srcs, cw = refs[6:6 + nw], refs[6 + nw]
        o_ref, lse_ref = refs[7 + nw:9 + nw]
        dsts, cw_all = refs[9 + nw:9 + 2 * nw], refs[9 + 2 * nw]
        bias_scr, kcat, vcat = refs[10 + 2 * nw:13 + 2 * nw]
        start, forward, finish = _gather_plan(kinds, srcs, dsts, cw, cw_all, *refs[13 + 2 * nw:])
        i = pl.program_id(0)

        @pl.when(i == 0)
        def _():
            start()
            _build_bias(diag_ref, bias_scr)

        @pl.when(i == n // 2)
        def _():
            forward()

        @pl.when(i == n - 1)
        def _():
            finish()

        kcat[0:TQ, :] = kp_ref[...]
        kcat[TQ:2 * TQ, :] = kc_ref[...]
        vcat[0:TQ, :] = vp_ref[...]
        vcat[TQ:2 * TQ, :] = vc_ref[...]
        lane_hi = lax.broadcasted_iota(jnp.int32, (QB, LANES), 1) >= HEAD_DIM
        col = lax.broadcasted_iota(jnp.int32, (2 * QB, KW), 1)

        def make_block(first_tile):
            def block(b, carry):
                r0 = pl.multiple_of(b * QB, QB)
                for p in range(HEADS // 2):
                    lanes = slice(LANES * p, LANES * (p + 1))
                    q2 = _stack_heads(q_ref[pl.ds(r0, QB), lanes] * scale, lane_hi)
                    kw = kcat[pl.ds(r0, KW), lanes]
                    vw = vcat[pl.ds(r0, KW), lanes]
                    sc = _nt(q2, kw) + bias_scr[p]
                    if first_tile:
                        sc = jnp.where(col >= TQ - r0, sc, NEG_BIG)
                    m = jnp.max(sc, axis=1, keepdims=True)
                    pe = jnp.exp(sc - m)
                    l = jnp.sum(pe, axis=1, keepdims=True)
                    o2 = _nn(pe.astype(BF16), vw) / l
                    lse2 = m + jnp.log(l)
                    lse_ref[pl.ds(r0, QB), 2 * p:2 * p + 1] = lse2[0:QB, :]
                    lse_ref[pl.ds(r0, QB), 2 * p + 1:2 * p + 2] = lse2[QB:2 * QB, :]
                    o_ref[pl.ds(r0, QB), lanes] = jnp.where(lane_hi, o2[QB:2 * QB, :], o2[0:QB, :]).astype(BF16)
                return carry
            return block

        @pl.when(i == 0)
        def _():
            lax.fori_loop(0, TQ // QB, make_block(True), 0)

        @pl.when(i > 0)
        def _():
            lax.fori_loop(0, TQ // QB, make_block(False), 0)

    blk = lambda col_blk, prev: pl.BlockSpec(
        (TQ, D_ATT), (lambda i: (jnp.maximum(i - 1, 0), col_blk)) if prev else (lambda i: (i, col_blk)))
    vmem = pl.BlockSpec(memory_space=pltpu.VMEM)
    return pl.pallas_call(
        body, name="attn_fwd", grid=(n,),
        in_specs=[pl.BlockSpec((HEADS, DIAG), lambda i: (0, 0)),
                  blk(0, False), blk(1, True), blk(1, False), blk(2, True), blk(2, False)] + [vmem] * (nw + 1),
        out_specs=[pl.BlockSpec((TQ, D_ATT), lambda i: (i, 0)), pl.BlockSpec((TQ, HEADS), lambda i: (i, 0))]
        + [ANY] * (nw + 1),
        out_shape=[jax.ShapeDtypeStruct((s, D_ATT), BF16), jax.ShapeDtypeStruct((s, HEADS), F32)]
        + _gather_out_shapes(shards, kinds, cw8),
        scratch_shapes=[pltpu.VMEM((HEADS // 2, 2 * QB, KW), F32), pltpu.VMEM((2 * TQ, D_ATT), BF16),
                        pltpu.VMEM((2 * TQ, D_ATT), BF16)] + _gather_sems(nw),
        compiler_params=_params("arbitrary"),
    )(diag, proj, proj, proj, proj, proj, *shards, cw8)


def _attn_bwd(diag, proj, d_att, att, lse, parts):
    s = proj.shape[0]
    n = s // TQ
    npart = len(parts)
    scale = HEAD_DIM ** -0.5
    rel_pad = 3 * LANES

    def body(*refs):
        diag_ref, q_ref, kp_ref, kc_ref, vp_ref, vc_ref, do_ref, o_ref, lse_ref = refs[:9]
        part_refs = refs[9:9 + npart]
        dqkv_ref, dbias_ref = refs[9 + npart:11 + npart]
        slot_refs = refs[11 + npart:11 + 2 * npart]
        bias_scr, dbias_acc, kcat, vcat, dk_acc, dv_acc, dq_scr = refs[11 + 2 * npart:18 + 2 * npart]
        start, finish = _scatter_plan(part_refs, slot_refs, *refs[18 + 2 * npart:])
        i = pl.program_id(0)

        @pl.when(i == 0)
        def _():
            start()
            _build_bias(diag_ref, bias_scr)
            dbias_acc[...] = jnp.zeros_like(dbias_acc)
            dk_acc[...] = jnp.zeros_like(dk_acc)
            dv_acc[...] = jnp.zeros_like(dv_acc)

        @pl.when(i > 0)
        def _():
            dqkv_ref[:, 0:D_ATT] = dq_scr[...]
            dk_acc[0:TQ, :] = dk_acc[TQ:2 * TQ, :]
            dk_acc[TQ:2 * TQ, :] = jnp.zeros((TQ, D_ATT), F32)
            dv_acc[0:TQ, :] = dv_acc[TQ:2 * TQ, :]
            dv_acc[TQ:2 * TQ, :] = jnp.zeros((TQ, D_ATT), F32)

        @pl.when(i < n)
        def _():
            kcat[0:TQ, :] = kp_ref[...]
            kcat[TQ:2 * TQ, :] = kc_ref[...]
            vcat[0:TQ, :] = vp_ref[...]
            vcat[TQ:2 * TQ, :] = vc_ref[...]

        lane_hi = lax.broadcasted_iota(jnp.int32, (QB, LANES), 1) >= HEAD_DIM
        col = lax.broadcasted_iota(jnp.int32, (2 * QB, KW), 1)

        def make_block(first_tile):
            def block(b, carry):
                r0 = pl.multiple_of(b * QB, QB)
                for p in range(HEADS // 2):
                    lanes = slice(LANES * p, LANES * (p + 1))
                    q2 = _stack_heads(q_ref[pl.ds(r0, QB), lanes] * scale, lane_hi)
                    kw = kcat[pl.ds(r0, KW), lanes]
                    vw = vcat[pl.ds(r0, KW), lanes]
                    dop = do_ref[pl.ds(r0, QB), lanes]
                    do2 = _stack_heads(dop, lane_hi)
                    prod = dop.astype(F32) * o_ref[pl.ds(r0, QB), lanes].astype(F32)
                    delta2 = jnp.concatenate(
                        [jnp.sum(jnp.where(lane_hi, 0.0, prod), axis=1, keepdims=True),
                         jnp.sum(jnp.where(lane_hi, prod, 0.0), axis=1, keepdims=True)], axis=0)
                    lse2 = jnp.concatenate([lse_ref[pl.ds(r0, QB), 2 * p:2 * p + 1],
                                            lse_ref[pl.ds(r0, QB), 2 * p + 1:2 * p + 2]], axis=0)
                    sc = _nt(q2, kw) + bias_scr[p]
                    if first_tile:
                        sc = jnp.where(col >= TQ - r0, sc, NEG_BIG)
                    pr = jnp.exp(sc - lse2)
                    ds = pr * (_nt(do2, vw) - delta2)
                    dbias_acc[p] += ds
                    dsb = ds.astype(BF16)
                    dv_acc[pl.ds(r0, KW), lanes] += _tn(pr.astype(BF16), do2)
                    dk_acc[pl.ds(r0, KW), lanes] += _tn(dsb, q2)
                    dq2 = _nn(dsb, kw)
                    dq = jnp.where(lane_hi, dq2[QB:2 * QB, :], dq2[0:QB, :]) * scale
                    dq_scr[pl.ds(r0, QB), lanes] = dq.astype(BF16)
                return carry
            return block

        @pl.when(i == 0)
        def _():
            lax.fori_loop(0, TQ // QB, make_block(True), 0)

        @pl.when((i > 0) & (i < n))
        def _():
            lax.fori_loop(0, TQ // QB, make_block(False), 0)

        @pl.when(i > 0)
        def _():
            dqkv_ref[:, D_ATT:2 * D_ATT] = dk_acc[0:TQ, :].astype(BF16)
            dqkv_ref[:, 2 * D_ATT:3 * D_ATT] = dv_acc[0:TQ, :].astype(BF16)

        @pl.when(i == n)
        def _():
            d_iota = lax.broadcasted_iota(jnp.int32, (DIAG, rel_pad), 0)
            n_iota = lax.broadcasted_iota(jnp.int32, (DIAG, rel_pad), 1)
            diff = jnp.where(d_iota < KW, d_iota, d_iota - DIAG)
            idx = jnp.clip(N_LEFT * CHUNK - diff, -MAX_REL, MAX_REL) + MAX_REL
            onehot = (idx == n_iota).astype(F32)
            rows = []
            for hd in range(HEADS):
                acc = dbias_acc[hd // 2, (hd % 2) * QB:(hd % 2 + 1) * QB, :]
                a = jnp.concatenate([acc, jnp.zeros((QB, DIAG - KW), F32)], axis=1)
                g8 = a[0:SUBLANES, :]
                for blk in range(1, QB // SUBLANES):
                    g8 = g8 + pltpu.roll(a[blk * SUBLANES:(blk + 1) * SUBLANES, :], DIAG - blk * SUBLANES, 1)
                g1 = g8[0:1, :]
                for r in range(1, SUBLANES):
                    g1 = g1 + pltpu.roll(g8[r:r + 1, :], DIAG - r, 1)
                rows.append(g1)
            g = jnp.concatenate(rows, axis=0)
            dbias_ref[...] = jnp.dot(g, onehot, preferred_element_type=F32, precision=lax.Precision.HIGHEST)
            finish()

    last = n - 1
    cur = lambda col_blk: pl.BlockSpec((TQ, D_ATT), lambda i: (jnp.minimum(i, last), col_blk))
    prev = lambda col_blk: pl.BlockSpec((TQ, D_ATT), lambda i: (jnp.maximum(jnp.minimum(i, last) - 1, 0), col_blk))
    return pl.pallas_call(
        body, name="attn_bwd", grid=(n + 1,),
        in_specs=[pl.BlockSpec((HEADS, DIAG), lambda i: (0, 0)),
                  cur(0), prev(1), cur(1), prev(2), cur(2), cur(0), cur(0),
                  pl.BlockSpec((TQ, HEADS), lambda i: (jnp.minimum(i, last), 0))] + [ANY] * npart,
        out_specs=[pl.BlockSpec((TQ, 3 * D_ATT), lambda i: (jnp.maximum(i - 1, 0), 0)),
                   pl.BlockSpec((HEADS, rel_pad), lambda i: (0, 0))] + [ANY] * npart,
        out_shape=[jax.ShapeDtypeStruct((s, 3 * D_ATT), BF16), jax.ShapeDtypeStruct((HEADS, rel_pad), F32)]
        + _scatter_out_shapes(parts),
        scratch_shapes=[pltpu.VMEM((HEADS // 2, 2 * QB, KW), F32), pltpu.VMEM((HEADS // 2, 2 * QB, KW), F32),
                        pltpu.VMEM((2 * TQ, D_ATT), BF16), pltpu.VMEM((2 * TQ, D_ATT), BF16),
                        pltpu.VMEM((2 * TQ, D_ATT), F32), pltpu.VMEM((2 * TQ, D_ATT), F32),
                        pltpu.VMEM((TQ, D_ATT), BF16)] + _scatter_sems(npart),
        compiler_params=_params("arbitrary"),
    )(diag, proj, proj, proj, proj, proj, d_att, att, lse, *parts)


def _shift_down(a, k, halo):
    rolled = pltpu.roll(a, k, 0)
    row = lax.broadcasted_iota(jnp.int32, halo.shape, 0)
    first = jnp.where(row < k, pltpu.roll(halo, k, 0), rolled[0:SUBLANES, :])
    return jnp.concatenate([first, rolled[SUBLANES:, :]], axis=0)


def _shift_up(a, k, nxt):
    tm = a.shape[0]
    rolled = pltpu.roll(a, tm - k, 0)
    row = lax.broadcasted_iota(jnp.int32, nxt.shape, 0)
    last = jnp.where(row >= SUBLANES - k, pltpu.roll(nxt, SUBLANES - k, 0), rolled[tm - SUBLANES:, :])
    return jnp.concatenate([rolled[:tm - SUBLANES, :], last], axis=0)


def _sigmoid(v):
    return 0.5 * jnp.tanh(0.5 * v) + 0.5


def _mixer_mid(att, proj, x, tgt, w_att, w_conv, w_out, conv_w8, conv_b, fin_g):
    s, d = x.shape
    dc = D_ATT
    n = s // TM_MID
    tm = TM_MID
    n_shards = 4

    def body(att_ref, za_ref, gb_ref, gc_ref, u_ref, zc_ref, hgc_ref, hu_ref, gatt_ref, gconv_ref, x_ref, t_ref,
             watt_ref, wconv_ref, wout_ref, cw_ref, cb_ref, fg_ref,
             dpb_ref, do_ref, dx2_ref, gatt_o, gconv_o, gout_o, loss_o, gfn_o, gcb_o, gcw_o,
             acc_att, acc_conv, acc_out, carry):
        i = pl.program_id(0)
        tile = n - 1 - i

        @pl.when(i == 0)
        def _():
            acc_att[...] = jnp.zeros_like(acc_att)
            acc_conv[...] = jnp.zeros_like(acc_conv)
            acc_out[...] = jnp.zeros_like(acc_out)
            carry[...] = jnp.zeros_like(carry)
            loss_o[...] = jnp.zeros_like(loss_o)
            gfn_o[...] = jnp.zeros_like(gfn_o)
            gcb_o[...] = jnp.zeros_like(gcb_o)
            gcw_o[...] = jnp.zeros_like(gcw_o)

        att_v = att_ref[...].astype(F32)
        za = za_ref[...].astype(F32)
        sa = _sigmoid(za)
        silu_a = za * sa
        a_b = (att_v * silu_a).astype(BF16)

        gb = gb_ref[...].astype(F32)
        gc = gc_ref[...].astype(F32)
        u = u_ref[...].astype(F32)
        zc = zc_ref[...].astype(F32)
        cu = gc * u
        halo = jnp.where(tile > 0, hgc_ref[...].astype(F32) * hu_ref[...].astype(F32), 0.0)
        cu1 = _shift_down(cu, 1, halo)
        cu2 = _shift_down(cu, 2, halo)
        w0, w1, w2 = cw_ref[0:1, :], cw_ref[1:2, :], cw_ref[2:3, :]
        vconv = w0 * cu2 + w1 * cu1 + w2 * cu + cb_ref[...]
        sc = _sigmoid(zc)
        silu_c = zc * sc
        c_b = (gb * vconv * silu_c).astype(BF16)

        y_att = _nn(a_b, watt_ref[...])
        y_conv = _nn(c_b, wconv_ref[...])
        ga = _sigmoid(gatt_ref[...].astype(F32))
        gv = _sigmoid(gconv_ref[...].astype(F32))
        m_b = (ga * y_att + gv * y_conv).astype(BF16)
        x2 = x_ref[...] + _nn(m_b, wout_ref[...])
        r2 = lax.rsqrt(jnp.mean(x2 * x2, axis=-1, keepdims=True) + EPS)
        x2n = x2 * r2
        fg = fg_ref[...]
        err = x2n * fg - t_ref[...]
        loss_o[...] += jnp.sum(err * err, axis=0, keepdims=True) * (0.5 / d)
        dy = err * (1.0 / d)
        gfn_o[...] += jnp.sum(dy * x2n, axis=0, keepdims=True)
        dyn = dy * fg
        dx2 = r2 * (dyn - x2n * jnp.mean(dyn * x2n, axis=-1, keepdims=True))
        dx2_ref[...] = dx2
        dx2_b = dx2.astype(BF16)

        dm = _nt(dx2_b, wout_ref[...])
        acc_out[...] += _tn(m_b, dx2_b)
        dy_att = dm * ga
        dy_conv = dm * gv
        dpb_ref[:, 5 * dc:5 * dc + d] = (dy_att * y_att * (1.0 - ga)).astype(BF16)
        dpb_ref[:, 5 * dc + d:5 * dc + 2 * d] = (dy_conv * y_conv * (1.0 - gv)).astype(BF16)
        dya_b = dy_att.astype(BF16)
        dyc_b = dy_conv.astype(BF16)
        da_in = _nt(dya_b, watt_ref[...])
        acc_att[...] += _tn(a_b, dya_b)
        dc_in = _nt(dyc_b, wconv_ref[...])
        acc_conv[...] += _tn(c_b, dyc_b)

        do_ref[...] = (da_in * silu_a).astype(BF16)
        dpb_ref[:, 0:dc] = (da_in * att_v * (sa * (1.0 + za * (1.0 - sa)))).astype(BF16)
        dpb_ref[:, dc:2 * dc] = (dc_in * vconv * silu_c).astype(BF16)
        dgs = dc_in * gb
        dvc = dgs * silu_c
        dpb_ref[:, 4 * dc:5 * dc] = (dgs * vconv * (sc * (1.0 + zc * (1.0 - sc)))).astype(BF16)
        gcb_o[...] += jnp.sum(dvc, axis=0, keepdims=True)
        gcw_o[0:1, :] += jnp.sum(dvc * cu2, axis=0, keepdims=True)
        gcw_o[1:2, :] += jnp.sum(dvc * cu1, axis=0, keepdims=True)
        gcw_o[2:3, :] += jnp.sum(dvc * cu, axis=0, keepdims=True)
        nxt = carry[...]
        dcu = w2 * dvc + w1 * _shift_up(dvc, 1, nxt) + w0 * _shift_up(dvc, 2, nxt)
        carry[...] = dvc[0:SUBLANES, :]
        dpb_ref[:, 2 * dc:3 * dc] = (dcu * u).astype(BF16)
        dpb_ref[:, 3 * dc:4 * dc] = (dcu * gc).astype(BF16)

        @pl.when(i == n - 1)
        def _():
            for j in range(n_shards):
                gatt_o[j] = acc_att[:, j * (d // n_shards):(j + 1) * (d // n_shards)].astype(BF16)
                gconv_o[j] = acc_conv[:, j * (d // n_shards):(j + 1) * (d // n_shards)].astype(BF16)
                gout_o[j] = acc_out[j * (d // n_shards):(j + 1) * (d // n_shards), :].astype(BF16)

    rev = lambda width, col_blk: pl.BlockSpec((tm, width), lambda i: (n - 1 - i, col_blk))
    halo_spec = lambda col_blk: pl.BlockSpec(
        (SUBLANES, dc), lambda i: (jnp.maximum((n - 1 - i) * (tm // SUBLANES) - 1, 0), col_blk))
    const = lambda shape: pl.BlockSpec(shape, lambda i: tuple(0 for _ in shape), pipeline_mode=pl.Buffered(1))
    q4 = d // n_shards
    return pl.pallas_call(
        body, name="mixer_mid", grid=(n,),
        in_specs=[rev(dc, 0), rev(dc, 3), rev(dc, 4), rev(dc, 5), rev(dc, 6), rev(dc, 7),
                  halo_spec(5), halo_spec(6), rev(d, 4), rev(d, 5), rev(d, 0), rev(d, 0),
                  const((dc, d)), const((dc, d)), const((d, d)), const((SUBLANES, dc)), const((1, dc)), const((1, d))],
        out_specs=[rev(5 * dc + 2 * d, 0), rev(dc, 0), rev(d, 0),
                   const((n_shards, dc, q4)), const((n_shards, dc, q4)), const((n_shards, q4, d)),
                   const((1, d)), const((1, d)), const((1, dc)), const((SUBLANES, dc))],
        out_shape=[jax.ShapeDtypeStruct((s, 5 * dc + 2 * d), BF16), jax.ShapeDtypeStruct((s, dc), BF16),
                   jax.ShapeDtypeStruct((s, d), F32),
                   jax.ShapeDtypeStruct((n_shards, dc, q4), BF16), jax.ShapeDtypeStruct((n_shards, dc, q4), BF16),
                   jax.ShapeDtypeStruct((n_shards, q4, d), BF16),
                   jax.ShapeDtypeStruct((1, d), F32), jax.ShapeDtypeStruct((1, d), F32),
                   jax.ShapeDtypeStruct((1, dc), F32), jax.ShapeDtypeStruct((SUBLANES, dc), F32)],
        scratch_shapes=[pltpu.VMEM((dc, d), F32), pltpu.VMEM((dc, d), F32), pltpu.VMEM((d, d), F32),
                        pltpu.VMEM((SUBLANES, dc), F32)],
        compiler_params=_params("arbitrary"),
    )(att, proj, proj, proj, proj, proj, proj, proj, proj, proj, x, tgt,
      w_att, w_conv, w_out, conv_w8, conv_b, fin_g)


def _in_proj_bwd_x(dqkv, dpb, w_in, x, dx2, g):
    s, d = x.shape
    tn = dqkv.shape[1]
    nb = dpb.shape[1] // tn
    n = s // TM_MM

    def body(*refs):
        dps, ws = refs[:nb + 1], refs[nb + 1:2 * nb + 2]
        x_ref, dx2_ref, g_ref, gx_ref, gng_ref = refs[2 * nb + 2:]
        i = pl.program_id(0)

        @pl.when(i == 0)
        def _():
            gng_ref[...] = jnp.zeros_like(gng_ref)

        dh = _nt(dps[0][...], ws[0][0])
        for j in range(1, nb + 1):
            dh = dh + _nt(dps[j][...], ws[j][0])
        xv = x_ref[...]
        r = lax.rsqrt(jnp.mean(xv * xv, axis=-1, keepdims=True) + EPS)
        xn = xv * r
        gng_ref[...] += jnp.sum(dh * xn, axis=0, keepdims=True)
        dhn = dh * g_ref[...]
        gx_ref[...] = dx2_ref[...] + r * (dhn - xn * jnp.mean(dhn * xn, axis=-1, keepdims=True))

    tile = lambda width, col_blk: pl.BlockSpec((TM_MM, width), lambda i: (i, col_blk))
    wspec = lambda blk: pl.BlockSpec((1, d, tn), lambda i: (blk, 0, 0), pipeline_mode=pl.Buffered(1))
    return pl.pallas_call(
        body, name="in_proj_bwd_x", grid=(n,),
        in_specs=[tile(tn, 0)] + [tile(tn, j) for j in range(nb)] + [wspec(j) for j in range(nb + 1)]
        + [tile(d, 0), tile(d, 0), pl.BlockSpec((1, d), lambda i: (0, 0))],
        out_specs=[tile(d, 0), pl.BlockSpec((1, d), lambda i: (0, 0))],
        out_shape=[jax.ShapeDtypeStruct((s, d), F32), jax.ShapeDtypeStruct((1, d), F32)],
        compiler_params=_params("arbitrary"),
    )(dqkv, *([dpb] * nb), *([w_in] * (nb + 1)), x, dx2, g)


def _in_proj_bwd_w(h, dqkv, dpb, order):
    s, d = h.shape
    tn = dqkv.shape[1]
    n = s // TM_BLK
    hr = d // 2
    settle = min(2, n - 1)

    def body(order_ref, h_ref, da_ref, db_ref, slots_ref, acc, sendbuf, pairbuf, chipbuf, psend, precv, send, recv, lsem):
        j, i = pl.program_id(0), pl.program_id(1)
        blk = order_ref[j]
        pos = _position()
        x, y, c = pos

        @pl.when(i == 0)
        def _():
            acc[...] = jnp.zeros_like(acc)

        @pl.when(blk == 0)
        def _():
            acc[...] += _tn(h_ref[...], da_ref[...])

        @pl.when(blk > 0)
        def _():
            acc[...] += _tn(h_ref[...], db_ref[...])

        def pair(step, half):
            return pltpu.make_async_remote_copy(
                src_ref=sendbuf.at[step, pl.ds(half * hr, hr), :], dst_ref=pairbuf.at[step],
                send_sem=psend.at[step], recv_sem=precv.at[step], device_id=(x, y, 1 - c), device_id_type=MESH)

        def ici(step):
            flip = step + 1
            return pltpu.make_async_remote_copy(
                src_ref=chipbuf.at[step], dst_ref=slots_ref.at[flip], send_sem=send.at[step], recv_sem=recv.at[step],
                device_id=_peer(pos, 4 * (flip >> 1) + 2 * (flip & 1)), device_id_type=MESH)

        local = pltpu.make_async_copy(chipbuf.at[N_CHIPS - 1], slots_ref.at[0], lsem.at[0])

        def combine(step):
            pair(step, c).wait_recv()
            mine = sendbuf[step, pl.ds(c * hr, hr), :].astype(F32)
            chipbuf[step] = (mine + pairbuf[step].astype(F32)).astype(BF16)

        for step in range(N_CHIPS):
            @pl.when((j == step) & (i == n - 1))
            def _(step=step):
                sendbuf[step] = acc[...].astype(BF16)
                pair(step, 1 - c).start()

        for step in range(N_CHIPS - 1):
            @pl.when((j == step + 1) & (i == settle))
            def _(step=step):
                combine(step)
                ici(step).start()

        @pl.when((j == N_CHIPS - 1) & (i == n - 1))
        def _():
            combine(N_CHIPS - 1)
            local.start()
            for step in range(N_CHIPS - 1):
                ici(step).wait_recv()
            for step in range(N_CHIPS - 1):
                ici(step).wait_send()
            for step in range(N_CHIPS):
                pair(step, 1 - c).wait_send()
            local.wait()

    return pl.pallas_call(
        body, name="in_proj_bwd_w",
        grid_spec=pltpu.PrefetchScalarGridSpec(
            num_scalar_prefetch=1, grid=(N_CHIPS, n),
            in_specs=[pl.BlockSpec((TM_BLK, d), lambda j, i, order: (i, 0)),
                      pl.BlockSpec((TM_BLK, tn), lambda j, i, order: (jnp.where(order[j] == 0, i, 0), 0)),
                      pl.BlockSpec((TM_BLK, tn), lambda j, i, order: (jnp.where(order[j] == 0, 0, i),
                                                                     jnp.maximum(order[j] - 1, 0)))],
            out_specs=[ANY],
            scratch_shapes=[pltpu.VMEM((d, tn), F32), pltpu.VMEM((N_CHIPS, d, tn), BF16),
                            pltpu.VMEM((N_CHIPS, hr, tn), BF16), pltpu.VMEM((N_CHIPS, hr, tn), BF16),
                            pltpu.SemaphoreType.DMA((N_CHIPS,)), pltpu.SemaphoreType.DMA((N_CHIPS,)),
                            pltpu.SemaphoreType.DMA((N_CHIPS - 1,)), pltpu.SemaphoreType.DMA((N_CHIPS - 1,)),
                            pltpu.SemaphoreType.DMA((1,))]),
        out_shape=[jax.ShapeDtypeStruct((N_CHIPS, hr, tn), BF16)],
        compiler_params=_params("arbitrary", "arbitrary"),
    )(order, h, dqkv, dpb)[0]


LOSS_ROW = 6


def _adam_update(w, g, m, v):
    c1 = 1.0 / (1.0 - ADAM_B1 ** ADAM_STEP)
    c2 = 1.0 / (1.0 - ADAM_B2 ** ADAM_STEP)
    m2 = ADAM_B1 * m + (1.0 - ADAM_B1) * g
    v2 = ADAM_B2 * v + (1.0 - ADAM_B2) * (g * g)
    return -ADAM_LR * ((m2 * c1) / (jnp.sqrt(v2 * c2) + ADAM_EPS) + ADAM_WD * w), m2, v2


def _adamw_small(recv, params, moments_m, moments_v):
    k = recv.shape[0]
    n_par = len(params)
    cshard = params[3].shape[1]

    def body(*refs):
        r_ref = refs[0]
        ws, ms, vs = refs[1:1 + n_par], refs[1 + n_par:1 + 2 * n_par], refs[1 + 2 * n_par:1 + 3 * n_par]
        loss_ref = refs[1 + 3 * n_par]
        outs = refs[2 + 3 * n_par:]
        total = r_ref[0]
        for slot in range(1, k):
            total = total + r_ref[slot]
        loss_ref[...] = jnp.sum(total[LOSS_ROW:LOSS_ROW + 1, :], axis=1, keepdims=True)
        chip = 2 * lax.axis_index("x") + lax.axis_index("y")
        g_cw = jnp.zeros((3, cshard), F32)
        for sh in range(N_CHIPS):
            g_cw = g_cw + jnp.where(chip == sh, total[3:6, sh * cshard:(sh + 1) * cshard], 0.0)
        grads = [total[0:1, :], total[1:2, :], total[2:3, :ws[2].shape[1]], g_cw, total[8:16, :ws[4].shape[1]]]
        for p in range(n_par):
            delta, m2, v2 = _adam_update(ws[p][...], grads[p], ms[p][...], vs[p][...])
            for q, val in enumerate((grads[p], delta, m2, v2)):
                outs[4 * p + q][...] = val

    shapes = [jax.ShapeDtypeStruct((1, 1), F32)]
    for p in params:
        shapes += [jax.ShapeDtypeStruct(p.shape, F32)] * 4
    return pl.pallas_call(body, name="adamw_small", out_shape=shapes)(recv, *params, *moments_m, *moments_v)


def _adamw(w, g, m, v, name, rows_per_step):
    r, c = w.shape

    def body(w_ref, g_ref, m_ref, v_ref, d_ref, mo_ref, vo_ref):
        d_ref[...], mo_ref[...], vo_ref[...] = _adam_update(w_ref[...], g_ref[...], m_ref[...], v_ref[...])

    spec = pl.BlockSpec((rows_per_step, c), lambda i: (i, 0))
    shape = jax.ShapeDtypeStruct((r, c), F32)
    return pl.pallas_call(
        body, name=name, grid=(r // rows_per_step,),
        in_specs=[spec] * 4, out_specs=[spec] * 3, out_shape=[shape] * 3,
        compiler_params=_params("parallel"),
    )(w, g, m, v)


ANY = pl.BlockSpec(memory_space=pl.ANY)
N_CHIPS = 4
N_DEV = 8


def _position():
    return lax.axis_index("x"), lax.axis_index("y"), lax.axis_index("c")


def _gather_out_shapes(shards, kinds, cw8):
    full = [(a.shape[0], a.shape[1] * N_CHIPS) if k == "cols" else (a.shape[0] * N_CHIPS, a.shape[1])
            for a, k in zip(shards, kinds)]
    return [jax.ShapeDtypeStruct(f, a.dtype) for f, a in zip(full, shards)] + [
        jax.ShapeDtypeStruct((N_CHIPS,) + cw8.shape, cw8.dtype)]


def _gather_sems(nw):
    return [pltpu.SemaphoreType.DMA((3, nw)), pltpu.SemaphoreType.DMA((3, nw)),
            pltpu.SemaphoreType.DMA((3, nw)), pltpu.SemaphoreType.DMA((3, nw)),
            pltpu.SemaphoreType.DMA((3,)), pltpu.SemaphoreType.DMA((3,)), pltpu.SemaphoreType.DMA((nw + 1,))]


def _gather_plan(kinds, srcs, dsts, cw, cw_all, send1, recv1, send2, recv2, ssend, srecv, lsem):
    nw = len(srcs)
    x, y, c = _position()
    mine = 2 * x + y
    chips = [(x, 1 - y), (1 - x, y), (1 - x, 1 - y)]

    def window(w, shard, half):
        r, cc = srcs[w].shape
        hr = r // 2
        if kinds[w] == "cols":
            rows = pl.ds(0, r) if half is None else pl.ds(half * hr, hr)
            return dsts[w].at[rows, pl.ds(shard * cc, cc)]
        rows = pl.ds(shard * r, r) if half is None else pl.ds(shard * r + half * hr, hr)
        return dsts[w].at[rows, :]

    def my_half(w):
        hr = srcs[w].shape[0] // 2
        return srcs[w].at[pl.ds(c * hr, hr), :]

    def local():
        return [pltpu.make_async_copy(srcs[w], window(w, mine, None), lsem.at[w]) for w in range(nw)] + [
            pltpu.make_async_copy(cw, cw_all.at[mine], lsem.at[nw])]

    def ici(k, w, shard):
        kx, ky = chips[k]
        return pltpu.make_async_remote_copy(
            src_ref=my_half(w), dst_ref=window(w, shard, c), send_sem=send1.at[k, w], recv_sem=recv1.at[k, w],
            device_id=(kx, ky, c), device_id_type=MESH)

    def d2d(k, w, shard, half):
        return pltpu.make_async_remote_copy(
            src_ref=window(w, shard, half), dst_ref=window(w, shard, half),
            send_sem=send2.at[k, w], recv_sem=recv2.at[k, w], device_id=(x, y, 1 - c), device_id_type=MESH)

    def small(k, shard):
        kx, ky = chips[k]
        return pltpu.make_async_remote_copy(
            src_ref=cw, dst_ref=cw_all.at[shard], send_sem=ssend.at[k], recv_sem=srecv.at[k],
            device_id=(kx, ky, c), device_id_type=MESH)

    def theirs(k):
        kx, ky = chips[k]
        return 2 * kx + ky

    def start():
        for cp in local():
            cp.start()
        for k in range(3):
            for w in range(nw):
                ici(k, w, mine).start()
            small(k, mine).start()

    def forward():
        for k in range(3):
            for w in range(nw):
                ici(k, w, theirs(k)).wait_recv()
                d2d(k, w, theirs(k), c).start()

    def finish():
        for k in range(3):
            for w in range(nw):
                d2d(k, w, theirs(k), 1 - c).wait_recv()
            small(k, theirs(k)).wait_recv()
        for k in range(3):
            for w in range(nw):
                ici(k, w, mine).wait_send()
                d2d(k, w, theirs(k), c).wait_send()
            small(k, mine).wait_send()
        for cp in local():
            cp.wait()

    return start, forward, finish


def _scatter_out_shapes(parts):
    return [jax.ShapeDtypeStruct((N_DEV, p.shape[1] // 2, p.shape[2]), p.dtype) for p in parts]


def _scatter_sems(nw):
    return [pltpu.SemaphoreType.DMA((N_DEV, nw)), pltpu.SemaphoreType.DMA((N_DEV, nw)), pltpu.SemaphoreType.DMA((nw,))]


def _peer(pos, k):
    x, y, c = pos
    return ((1 - x) if k & 4 else x, (1 - y) if k & 2 else y, (1 - c) if k & 1 else c)


def _scatter_plan(srcs, dsts, send, recv, lsem):
    nw = len(srcs)
    pos = _position()

    def piece(w, k):
        px, py, pc = _peer(pos, k)
        hr = srcs[w].shape[1] // 2
        return srcs[w].at[2 * px + py, pl.ds(pc * hr, hr), :]

    def remote(w, k):
        return pltpu.make_async_remote_copy(
            src_ref=piece(w, k), dst_ref=dsts[w].at[k], send_sem=send.at[k, w], recv_sem=recv.at[k, w],
            device_id=_peer(pos, k), device_id_type=MESH)

    def local(w):
        return pltpu.make_async_copy(piece(w, 0), dsts[w].at[0], lsem.at[w])

    def start():
        for w in range(nw):
            local(w).start()
        for k in range(1, N_DEV):
            for w in range(nw):
                remote(w, k).start()

    def finish():
        for k in range(1, N_DEV):
            for w in range(nw):
                remote(w, k).wait_recv()
        for k in range(1, N_DEV):
            for w in range(nw):
                remote(w, k).wait_send()
        for w in range(nw):
            local(w).wait()

    return start, finish


def _reduce_pair(slots, small):
    nw = len(slots)

    def body(*refs):
        srcs, sm = refs[:nw], refs[nw]
        dsts, sm_all = refs[nw + 1:2 * nw + 1], refs[2 * nw + 1]
        halves = refs[2 * nw + 2:3 * nw + 2]
        send, recv, ssend, srecv, lsem = refs[3 * nw + 2:]
        pos = _position()
        x, y, c = pos
        me = 4 * x + 2 * y + c

        def rows(w, half):
            hr = halves[w].shape[0]
            return dsts[w].at[pl.ds(half * hr, hr), :]

        def remote(w, half):
            return pltpu.make_async_remote_copy(
                src_ref=halves[w], dst_ref=rows(w, half), send_sem=send.at[w], recv_sem=recv.at[w],
                device_id=(x, y, 1 - c), device_id_type=MESH)

        def bcast(k, slot):
            return pltpu.make_async_remote_copy(
                src_ref=sm, dst_ref=sm_all.at[slot], send_sem=ssend.at[k], recv_sem=srecv.at[k],
                device_id=_peer(pos, k), device_id_type=MESH)

        small_copies = [bcast(k, me) for k in range(1, N_DEV)]
        own_small = pltpu.make_async_copy(sm, sm_all.at[me], lsem.at[nw])
        for cp in small_copies + [own_small]:
            cp.start()
        big = []
        for w in range(nw):
            total = srcs[w][0].astype(F32)
            for k in range(1, srcs[w].shape[0]):
                total = total + srcs[w][k].astype(F32)
            halves[w][...] = total
            big += [remote(w, c), pltpu.make_async_copy(halves[w], rows(w, c), lsem.at[w])]
            big[-2].start()
            big[-1].start()
        for w in range(nw):
            remote(w, 1 - c).wait_recv()
        for k in range(1, N_DEV):
            px, py, pc = _peer(pos, k)
            bcast(k, 4 * px + 2 * py + pc).wait_recv()
        for w in range(nw):
            big[2 * w].wait_send()
            big[2 * w + 1].wait()
        for cp in small_copies:
            cp.wait_send()
        own_small.wait()

    vmem = pl.BlockSpec(memory_space=pltpu.VMEM)
    half_shapes = [(sl.shape[1], sl.shape[2]) for sl in slots]
    return pl.pallas_call(
        body, name="reduce_pair",
        in_specs=[vmem] * (nw + 1), out_specs=[ANY] * (nw + 1),
        out_shape=[jax.ShapeDtypeStruct((2 * r, cc), F32) for r, cc in half_shapes]
        + [jax.ShapeDtypeStruct((N_DEV,) + small.shape, small.dtype)],
        scratch_shapes=[pltpu.VMEM(hs, F32) for hs in half_shapes]
        + [pltpu.SemaphoreType.DMA((nw,)), pltpu.SemaphoreType.DMA((nw,)),
           pltpu.SemaphoreType.DMA((N_DEV,)), pltpu.SemaphoreType.DMA((N_DEV,)),
           pltpu.SemaphoreType.DMA((nw + 1,))],
        compiler_params=pltpu.CompilerParams(vmem_limit_bytes=VMEM_LIMIT),
    )(*slots, small)


def _pad_to(a, rows, cols):
    return jnp.pad(a, ((0, rows - a.shape[0]), (0, cols - a.shape[1])))


def _pack_small(norm_g, fin_g, conv_b, conv_w, loss_vec, rel):
    rows = [_pad_to(norm_g, 1, SMALL_COLS), _pad_to(fin_g, 1, SMALL_COLS), _pad_to(conv_b, 1, SMALL_COLS),
            _pad_to(conv_w, 3, SMALL_COLS), _pad_to(loss_vec, 2, SMALL_COLS), _pad_to(rel, HEADS, SMALL_COLS)]
    return jnp.concatenate(rows, axis=0)


def kernel(x, norm_g, w_in, rel_bias, w_att_out, conv_w, conv_b, w_conv_out, w_out, final_norm_g, loss_target, m_norm_g, m_w_in, m_rel_bias, m_w_att_out, m_conv_w, m_conv_b, m_w_conv_out, m_w_out, m_final_norm_g, v_norm_g, v_w_in, v_rel_bias, v_w_att_out, v_conv_w, v_conv_b, v_w_conv_out, v_w_out, v_final_norm_g):
    xs, tgt = x[0], loss_target[0]
    cshard = conv_w.shape[2]
    chip = 2 * lax.axis_index("x") + lax.axis_index("y")

    shards = [w_in[0].astype(BF16), w_att_out[0].astype(BF16), w_conv_out[0].astype(BF16), w_out[0].astype(BF16)]
    cw8 = _pad_to(conv_w[0], SUBLANES, cshard)
    flips = jnp.arange(N_CHIPS, dtype=jnp.int32)
    own_first = jnp.bitwise_xor(chip, flips)
    own_last = jnp.bitwise_xor(chip, (flips + 1) % N_CHIPS)

    proj, h, wb_in = _in_proj_gather(xs, norm_g, shards[0], own_first)
    diag = jnp.take(rel_bias[0], _diag_rel_index(), axis=1)
    att, lse, wb_att, wb_conv, wb_out, cw_all = _attn_fwd(diag, proj, shards[1:], ["cols", "cols", "rows"], cw8)
    conv_w_full = jnp.transpose(cw_all, (1, 0, 2)).reshape(SUBLANES, N_CHIPS * cshard)
    (dpb, d_att, dx2, g_att_p, g_conv_p, g_out_p, loss_vec, g_fin, g_cb, g_cw) = _mixer_mid(
        att, proj, xs, tgt, wb_att, wb_conv, wb_out, conv_w_full, conv_b, final_norm_g[None, :])
    dqkv, g_rel, r_att, r_conv, r_out = _attn_bwd(diag, proj, d_att, att, lse, [g_att_p, g_conv_p, g_out_p])
    grad_x, g_norm = _in_proj_bwd_x(dqkv, dpb, wb_in, xs, dx2, norm_g)
    r_in = _in_proj_bwd_w(h, dqkv, dpb, own_last)

    small = _pack_small(g_norm, g_fin, g_cb, g_cw[0:3], loss_vec, g_rel)
    gw_in, gw_att, gw_conv, gw_out, r_small = _reduce_pair([r_in, r_att, r_conv, r_out], small)
    rel_pad = g_rel.shape[1]
    pad_rel = lambda a: _pad_to(a[0], HEADS, rel_pad)
    small_out = _adamw_small(
        r_small,
        [norm_g, final_norm_g[None, :], conv_b, conv_w[0], pad_rel(rel_bias)],
        [m_norm_g, m_final_norm_g[None, :], m_conv_b, m_conv_w[0], pad_rel(m_rel_bias)],
        [v_norm_g, v_final_norm_g[None, :], v_conv_b, v_conv_w[0], pad_rel(v_rel_bias)])
    loss = small_out[0][0, 0]
    small_names = ["norm_g", "final_norm_g", "conv_b", "conv_w", "rel_bias"]
    fix = {"norm_g": lambda a: a, "final_norm_g": lambda a: a[0], "conv_b": lambda a: a,
           "conv_w": lambda a: a[None], "rel_bias": lambda a: a[None, :, :N_REL]}
    small_res = {name: [fix[name](small_out[1 + 4 * p + q]) for q in range(4)] for p, name in enumerate(small_names)}

    big = {}
    for name, w, g, m, v, rows in (("w_in", w_in, gw_in, m_w_in, v_w_in, 128),
                                   ("w_att_out", w_att_out, gw_att, m_w_att_out, v_w_att_out, 256),
                                   ("w_conv_out", w_conv_out, gw_conv, m_w_conv_out, v_w_conv_out, 256),
                                   ("w_out", w_out, gw_out, m_w_out, v_w_out, 128)):
        dw, mw, vw = _adamw(w[0], g, m[0], v[0], "adamw_" + name, rows)
        big[name] = (g[None], dw[None], mw[None], vw[None])

    order = ["norm_g", "w_in", "rel_bias", "w_att_out", "conv_w", "conv_b", "w_conv_out", "w_out", "final_norm_g"]
    outs = [loss, grad_x[None]]
    for which in range(4):
        for name in order:
            outs.append(big[name][which] if name in big else small_res[name][which])
    return tuple(outs)
```

```python
import numpy as np
import jax
import jax.numpy as jnp
from jax import lax
from jax.experimental import pallas as pl
from jax.experimental.pallas import tpu as pltpu

F32 = jnp.float32
BF16 = jnp.bfloat16
MESH = pl.DeviceIdType.MESH

CHUNK = 64
N_LEFT = 8
HEADS = 8
HEAD_DIM = 64
D_ATT = HEADS * HEAD_DIM
MAX_REL = 128
N_REL = 2 * MAX_REL + 1
EPS = 1e-6
NEG_BIG = -1e30
ADAM_LR, ADAM_B1, ADAM_B2, ADAM_EPS, ADAM_WD, ADAM_STEP = 0.001, 0.9, 0.999, 1e-08, 0.01, 10

LANES = 128
SUBLANES = 8
VMEM_LIMIT = 56 * 1024 * 1024

QB = 2 * CHUNK
KW = N_LEFT * CHUNK + QB
DIAG = KW + QB
TQ = N_LEFT * CHUNK
TM_MID = 256
TM_MM = 512
TM_BLK = 1024
SMALL_ROWS, SMALL_COLS = 16, 1024


def _params(*sem):
    return pltpu.CompilerParams(dimension_semantics=sem, vmem_limit_bytes=VMEM_LIMIT)


def _nt(a, b):
    return lax.dot_general(a, b, (((1,), (1,)), ((), ())), preferred_element_type=F32)


def _tn(a, b):
    return lax.dot_general(a, b, (((0,), (0,)), ((), ())), preferred_element_type=F32)


def _nn(a, b):
    return jnp.dot(a, b, preferred_element_type=F32)


def _diag_rel_index():
    d = np.arange(DIAG)
    diff = np.where(d < KW, d, d - DIAG)
    rel = N_LEFT * CHUNK - diff
    return np.clip(rel, -MAX_REL, MAX_REL) + MAX_REL


def _build_bias(diag_ref, bias_scr):
    r = lax.broadcasted_iota(jnp.int32, (QB, KW), 0) // CHUNK
    s = lax.broadcasted_iota(jnp.int32, (QB, KW), 1) // CHUNK
    allowed = (s >= r) & (s <= r + N_LEFT)
    for h in range(HEADS):
        row = jnp.broadcast_to(diag_ref[h:h + 1, :], (QB, DIAG))
        t = pltpu.roll(row, 0, 1, stride=1, stride_axis=0)
        bias_scr[h // 2, (h % 2) * QB:(h % 2 + 1) * QB, :] = jnp.where(allowed, t[:, :KW], NEG_BIG)


def _stack_heads(a, lane_hi):
    zero = jnp.zeros_like(a)
    return jnp.concatenate([jnp.where(lane_hi, zero, a), jnp.where(lane_hi, a, zero)], axis=0)


def _in_proj_gather(x, g, shard, order):
    s, d = x.shape
    tn = shard.shape[1]
    n = s // TM_BLK
    hr = d // 2

    def body(order_ref, x_ref, g_ref, shard32_ref, proj_ref, h_ref, wfull_ref, shard_ref, hbuf, wbuf,
             send1, recv1, send2, recv2, lsem):
        del order_ref
        j, i = pl.program_id(0), pl.program_id(1)
        x, y, c = _position()
        mine = 2 * x + y
        chips = [(x, 1 - y), (1 - x, y), (1 - x, 1 - y)]

        def theirs(k):
            return 2 * chips[k][0] + chips[k][1]

        def half_rows(half):
            return pl.ds(half * hr, hr)

        def ici(k, shard_index):
            return pltpu.make_async_remote_copy(
                src_ref=shard_ref.at[half_rows(c), :], dst_ref=wfull_ref.at[shard_index, half_rows(c), :],
                send_sem=send1.at[k], recv_sem=recv1.at[k], device_id=(*chips[k], c), device_id_type=MESH)

        def d2d(k, half):
            return pltpu.make_async_remote_copy(
                src_ref=wbuf.at[k % 2, half_rows(half), :], dst_ref=wfull_ref.at[theirs(k), half_rows(half), :],
                send_sem=send2.at[k], recv_sem=recv2.at[k], device_id=(x, y, 1 - c), device_id_type=MESH)

        def load(k, half, sem):
            return pltpu.make_async_copy(wfull_ref.at[theirs(k), half_rows(half), :],
                                         wbuf.at[k % 2, half_rows(half), :], lsem.at[sem])

        own = pltpu.make_async_copy(shard_ref, wfull_ref.at[mine], lsem.at[0])

        @pl.when((j == 0) & (i == 0))
        def _():
            shard_ref[...] = shard32_ref[...].astype(BF16)
            own.start()
            ici(0, mine).start()
            ici(1, mine).start()

        for k in range(3):
            first = max(n - 3, 0) if k == 0 else min(n // 2, n - 1)

            @pl.when((j == k) & (i == first))
            def _(k=k):
                if k == 0:
                    ici(0, mine).wait_send()
                    ici(1, mine).wait_send()
                    ici(2, mine).start()
                if k == 2:
                    d2d(0, c).wait_send()
                ici(k, theirs(k)).wait_recv()
                load(k, c, 1).start()

            @pl.when((j == k) & (i == min(first + 1, n - 1)))
            def _(k=k):
                load(k, c, 1).wait()
                d2d(k, c).start()

            @pl.when((j == k) & (i == min(first + 2, n - 1)))
            def _(k=k):
                d2d(k, 1 - c).wait_recv()
                load(k, 1 - c, 2).start()

            @pl.when((j == k + 1) & (i == 0))
            def _(k=k):
                load(k, 1 - c, 2).wait()

        @pl.when(j == 0)
        def _():
            xv = x_ref[...]
            r = lax.rsqrt(jnp.mean(xv * xv, axis=-1, keepdims=True) + EPS)
            hv = ((xv * r) * g_ref[...]).astype(BF16)
            hbuf[i] = hv
            h_ref[...] = hv
            proj_ref[...] = _nn(hv, shard_ref[...]).astype(BF16)

        for k in range(3):
            @pl.when(j == k + 1)
            def _(k=k):
                proj_ref[...] = _nn(hbuf[i], wbuf[k % 2]).astype(BF16)

        @pl.when((j == 3) & (i == n - 1))
        def _():
            ici(2, mine).wait_send()
            d2d(1, c).wait_send()
            d2d(2, c).wait_send()
            own.wait()

    return pl.pallas_call(
        body, name="in_proj_gather",
        grid_spec=pltpu.PrefetchScalarGridSpec(
            num_scalar_prefetch=1, grid=(N_CHIPS, n),
            in_specs=[pl.BlockSpec((TM_BLK, d), lambda j, i, order: (jnp.where(j == 0, i, n - 1), 0)),
                      pl.BlockSpec((1, d), lambda j, i, order: (0, 0)), pl.BlockSpec(memory_space=pltpu.VMEM)],
            out_specs=[pl.BlockSpec((TM_BLK, tn), lambda j, i, order: (i, order[j])),
                       pl.BlockSpec((TM_BLK, d), lambda j, i, order: (jnp.where(j == 0, i, n - 1), 0)), ANY],
            scratch_shapes=[pltpu.VMEM((d, tn), BF16), pltpu.VMEM((n, TM_BLK, d), BF16), pltpu.VMEM((2, d, tn), BF16),
                            pltpu.SemaphoreType.DMA((3,)), pltpu.SemaphoreType.DMA((3,)),
                            pltpu.SemaphoreType.DMA((3,)), pltpu.SemaphoreType.DMA((3,)), pltpu.SemaphoreType.DMA((3,))]),
        out_shape=[jax.ShapeDtypeStruct((s, N_CHIPS * tn), BF16), jax.ShapeDtypeStruct((s, d), BF16),
                   jax.ShapeDtypeStruct((N_CHIPS, d, tn), BF16)],
        compiler_params=_params("arbitrary", "arbitrary"),
    )(order, x, g, shard)


def _attn_fwd(diag, proj, shards, kinds, cw8):
    s = proj.shape[0]
    n = s // TQ
    nw = len(shards)
    scale = HEAD_DIM ** -0.5

    def body(*refs):
        diag_ref, q_ref, kp_ref, kc_ref, vp_ref, vc_ref = refs[:6]
        srcs, cw = refs[6:6 + nw], refs[6 + nw]
        o_ref, lse_ref = refs[7 + nw:9 + nw]
        dsts, cw_all = refs[9 + nw:9 + 2 * nw], refs[9 + 2 * nw]
        bias_scr, kcat, vcat = refs[10 + 2 * nw:13 + 2 * nw]
        casts = refs[13 + 2 * nw:13 + 3 * nw]
        start, forward, finish = _gather_plan(kinds, casts, dsts, cw, cw_all, *refs[13 + 3 * nw:])
        i = pl.program_id(0)

        @pl.when(i == 0)
        def _():
            for w in range(nw):
                casts[w][...] = srcs[w][...].astype(BF16)
            start()
            _build_bias(diag_ref, bias_scr)

        @pl.when(i == n // 2)
        def _():
            forward()

        @pl.when(i == n - 1)
        def _():
            finish()

        kcat[0:TQ, :] = kp_ref[...]
        kcat[TQ:2 * TQ, :] = kc_ref[...]
        vcat[0:TQ, :] = vp_ref[...]
        vcat[TQ:2 * TQ, :] = vc_ref[...]
        lane_hi = lax.broadcasted_iota(jnp.int32, (QB, LANES), 1) >= HEAD_DIM
        col = lax.broadcasted_iota(jnp.int32, (2 * QB, KW), 1)

        def make_block(first_tile):
            def block(b, carry):
                r0 = pl.multiple_of(b * QB, QB)
                for p in range(HEADS // 2):
                    lanes = slice(LANES * p, LANES * (p + 1))
                    q2 = _stack_heads(q_ref[pl.ds(r0, QB), lanes] * scale, lane_hi)
                    kw = kcat[pl.ds(r0, KW), lanes]
                    vw = vcat[pl.ds(r0, KW), lanes]
                    sc = _nt(q2, kw) + bias_scr[p]
                    if first_tile:
                        sc = jnp.where(col >= TQ - r0, sc, NEG_BIG)
                    m = jnp.max(sc, axis=1, keepdims=True)
                    pe = jnp.exp(sc - m)
                    l = jnp.sum(pe, axis=1, keepdims=True)
                    o2 = _nn(pe.astype(BF16), vw) / l
                    lse2 = m + jnp.log(l)
                    lse_ref[pl.ds(r0, QB), 2 * p:2 * p + 1] = lse2[0:QB, :]
                    lse_ref[pl.ds(r0, QB), 2 * p + 1:2 * p + 2] = lse2[QB:2 * QB, :]
                    o_ref[pl.ds(r0, QB), lanes] = jnp.where(lane_hi, o2[QB:2 * QB, :], o2[0:QB, :]).astype(BF16)
                return carry
            return block

        @pl.when(i == 0)
        def _():
            lax.fori_loop(0, TQ // QB, make_block(True), 0)

        @pl.when(i > 0)
        def _():
            lax.fori_loop(0, TQ // QB, make_block(False), 0)

    blk = lambda col_blk, prev: pl.BlockSpec(
        (TQ, D_ATT), (lambda i: (jnp.maximum(i - 1, 0), col_blk)) if prev else (lambda i: (i, col_blk)))
    vmem = pl.BlockSpec(memory_space=pltpu.VMEM)
    return pl.pallas_call(
        body, name="attn_fwd", grid=(n,),
        in_specs=[pl.BlockSpec((HEADS, DIAG), lambda i: (0, 0)),
                  blk(0, False), blk(1, True), blk(1, False), blk(2, True), blk(2, False)] + [vmem] * (nw + 1),
        out_specs=[pl.BlockSpec((TQ, D_ATT), lambda i: (i, 0)), pl.BlockSpec((TQ, HEADS), lambda i: (i, 0))]
        + [ANY] * (nw + 1),
        out_shape=[jax.ShapeDtypeStruct((s, D_ATT), BF16), jax.ShapeDtypeStruct((s, HEADS), F32)]
        + _gather_out_shapes(shards, kinds, cw8),
        scratch_shapes=[pltpu.VMEM((HEADS // 2, 2 * QB, KW), F32), pltpu.VMEM((2 * TQ, D_ATT), BF16),
                        pltpu.VMEM((2 * TQ, D_ATT), BF16)] + [pltpu.VMEM(a.shape, BF16) for a in shards]
        + _gather_sems(nw),
        compiler_params=_params("arbitrary"),
    )(diag, proj, proj, proj, proj, proj, *shards, cw8)


def _attn_bwd(diag, proj, d_att, att, lse, parts):
    s = proj.shape[0]
    n = s // TQ
    npart = len(parts)
    scale = HEAD_DIM ** -0.5
    rel_pad = 3 * LANES

    def body(*refs):
        diag_ref, q_ref, kp_ref, kc_ref, vp_ref, vc_ref, do_ref, o_ref, lse_ref = refs[:9]
        part_refs = refs[9:9 + npart]
        dqkv_ref, dbias_ref = refs[9 + npart:11 + npart]
        slot_refs = refs[11 + npart:11 + 2 * npart]
        bias_scr, dbias_acc, kcat, vcat, dk_acc, dv_acc, dq_scr = refs[11 + 2 * npart:18 + 2 * npart]
        start, finish = _scatter_plan(part_refs, slot_refs, *refs[18 + 2 * npart:])
        i = pl.program_id(0)

        @pl.when(i == 0)
        def _():
            start()
            _build_bias(diag_ref, bias_scr)
            dbias_acc[...] = jnp.zeros_like(dbias_acc)
            dk_acc[...] = jnp.zeros_like(dk_acc)
            dv_acc[...] = jnp.zeros_like(dv_acc)

        @pl.when(i > 0)
        def _():
            dqkv_ref[:, 0:D_ATT] = dq_scr[...]
            dk_acc[0:TQ, :] = dk_acc[TQ:2 * TQ, :]
            dk_acc[TQ:2 * TQ, :] = jnp.zeros((TQ, D_ATT), F32)
            dv_acc[0:TQ, :] = dv_acc[TQ:2 * TQ, :]
            dv_acc[TQ:2 * TQ, :] = jnp.zeros((TQ, D_ATT), F32)

        @pl.when(i < n)
        def _():
            kcat[0:TQ, :] = kp_ref[...]
            kcat[TQ:2 * TQ, :] = kc_ref[...]
            vcat[0:TQ, :] = vp_ref[...]
            vcat[TQ:2 * TQ, :] = vc_ref[...]

        lane_hi = lax.broadcasted_iota(jnp.int32, (QB, LANES), 1) >= HEAD_DIM
        col = lax.broadcasted_iota(jnp.int32, (2 * QB, KW), 1)

        def make_block(first_tile):
            def block(b, carry):
                r0 = pl.multiple_of(b * QB, QB)
                for p in range(HEADS // 2):
                    lanes = slice(LANES * p, LANES * (p + 1))
                    q2 = _stack_heads(q_ref[pl.ds(r0, QB), lanes] * scale, lane_hi)
                    kw = kcat[pl.ds(r0, KW), lanes]
                    vw = vcat[pl.ds(r0, KW), lanes]
                    dop = do_ref[pl.ds(r0, QB), lanes]
                    do2 = _stack_heads(dop, lane_hi)
                    prod = dop.astype(F32) * o_ref[pl.ds(r0, QB), lanes].astype(F32)
                    delta2 = jnp.concatenate(
                        [jnp.sum(jnp.where(lane_hi, 0.0, prod), axis=1, keepdims=True),
                         jnp.sum(jnp.where(lane_hi, prod, 0.0), axis=1, keepdims=True)], axis=0)
                    lse2 = jnp.concatenate([lse_ref[pl.ds(r0, QB), 2 * p:2 * p + 1],
                                            lse_ref[pl.ds(r0, QB), 2 * p + 1:2 * p + 2]], axis=0)
                    sc = _nt(q2, kw) + bias_scr[p]
                    if first_tile:
                        sc = jnp.where(col >= TQ - r0, sc, NEG_BIG)
                    pr = jnp.exp(sc - lse2)
                    ds = pr * (_nt(do2, vw) - delta2)
                    dbias_acc[p] += ds
                    dsb = ds.astype(BF16)
                    dv_acc[pl.ds(r0, KW), lanes] += _tn(pr.astype(BF16), do2)
                    dk_acc[pl.ds(r0, KW), lanes] += _tn(dsb, q2)
                    dq2 = _nn(dsb, kw)
                    dq = jnp.where(lane_hi, dq2[QB:2 * QB, :], dq2[0:QB, :]) * scale
                    dq_scr[pl.ds(r0, QB), lanes] = dq.astype(BF16)
                return carry
            return block

        @pl.when(i == 0)
        def _():
            lax.fori_loop(0, TQ // QB, make_block(True), 0)

        @pl.when((i > 0) & (i < n))
        def _():
            lax.fori_loop(0, TQ // QB, make_block(False), 0)

        @pl.when(i > 0)
        def _():
            dqkv_ref[:, D_ATT:2 * D_ATT] = dk_acc[0:TQ, :].astype(BF16)
            dqkv_ref[:, 2 * D_ATT:3 * D_ATT] = dv_acc[0:TQ, :].astype(BF16)

        @pl.when(i == n)
        def _():
            d_iota = lax.broadcasted_iota(jnp.int32, (DIAG, rel_pad), 0)
            n_iota = lax.broadcasted_iota(jnp.int32, (DIAG, rel_pad), 1)
            diff = jnp.where(d_iota < KW, d_iota, d_iota - DIAG)
            idx = jnp.clip(N_LEFT * CHUNK - diff, -MAX_REL, MAX_REL) + MAX_REL
            onehot = (idx == n_iota).astype(F32)
            rows = []
            for hd in range(HEADS):
                acc = dbias_acc[hd // 2, (hd % 2) * QB:(hd % 2 + 1) * QB, :]
                a = jnp.concatenate([acc, jnp.zeros((QB, DIAG - KW), F32)], axis=1)
                g8 = a[0:SUBLANES, :]
                for blk in range(1, QB // SUBLANES):
                    g8 = g8 + pltpu.roll(a[blk * SUBLANES:(blk + 1) * SUBLANES, :], DIAG - blk * SUBLANES, 1)
                g1 = g8[0:1, :]
                for r in range(1, SUBLANES):
                    g1 = g1 + pltpu.roll(g8[r:r + 1, :], DIAG - r, 1)
                rows.append(g1)
            g = jnp.concatenate(rows, axis=0)
            dbias_ref[...] = jnp.dot(g, onehot, preferred_element_type=F32, precision=lax.Precision.HIGHEST)
            finish()

    last = n - 1
    cur = lambda col_blk: pl.BlockSpec((TQ, D_ATT), lambda i: (jnp.minimum(i, last), col_blk))
    prev = lambda col_blk: pl.BlockSpec((TQ, D_ATT), lambda i: (jnp.maximum(jnp.minimum(i, last) - 1, 0), col_blk))
    return pl.pallas_call(
        body, name="attn_bwd", grid=(n + 1,),
        in_specs=[pl.BlockSpec((HEADS, DIAG), lambda i: (0, 0)),
                  cur(0), prev(1), cur(1), prev(2), cur(2), cur(0), cur(0),
                  pl.BlockSpec((TQ, HEADS), lambda i: (jnp.minimum(i, last), 0))] + [ANY] * npart,
        out_specs=[pl.BlockSpec((TQ, 3 * D_ATT), lambda i: (jnp.maximum(i - 1, 0), 0)),
                   pl.BlockSpec((HEADS, rel_pad), lambda i: (0, 0))] + [ANY] * npart,
        out_shape=[jax.ShapeDtypeStruct((s, 3 * D_ATT), BF16), jax.ShapeDtypeStruct((HEADS, rel_pad), F32)]
        + _scatter_out_shapes(parts),
        scratch_shapes=[pltpu.VMEM((HEADS // 2, 2 * QB, KW), F32), pltpu.VMEM((HEADS // 2, 2 * QB, KW), F32),
                        pltpu.VMEM((2 * TQ, D_ATT), BF16), pltpu.VMEM((2 * TQ, D_ATT), BF16),
                        pltpu.VMEM((2 * TQ, D_ATT), F32), pltpu.VMEM((2 * TQ, D_ATT), F32),
                        pltpu.VMEM((TQ, D_ATT), BF16)] + _scatter_sems(npart),
        compiler_params=_params("arbitrary"),
    )(diag, proj, proj, proj, proj, proj, d_att, att, lse, *parts)


def _shift_down(a, k, halo):
    rolled = pltpu.roll(a, k, 0)
    row = lax.broadcasted_iota(jnp.int32, halo.shape, 0)
    first = jnp.where(row < k, pltpu.roll(halo, k, 0), rolled[0:SUBLANES, :])
    return jnp.concatenate([first, rolled[SUBLANES:, :]], axis=0)


def _shift_up(a, k, nxt):
    tm = a.shape[0]
    rolled = pltpu.roll(a, tm - k, 0)
    row = lax.broadcasted_iota(jnp.int32, nxt.shape, 0)
    last = jnp.where(row >= SUBLANES - k, pltpu.roll(nxt, SUBLANES - k, 0), rolled[tm - SUBLANES:, :])
    return jnp.concatenate([rolled[:tm - SUBLANES, :], last], axis=0)


def _sigmoid(v):
    return 0.5 * jnp.tanh(0.5 * v) + 0.5


def _mixer_mid(att, proj, x, tgt, w_att, w_conv, w_out, conv_w8, conv_b, fin_g):
    s, d = x.shape
    dc = D_ATT
    n = s // TM_MID
    tm = TM_MID
    n_shards = 4

    def body(att_ref, za_ref, gb_ref, gc_ref, u_ref, zc_ref, hgc_ref, hu_ref, gatt_ref, gconv_ref, x_ref, t_ref,
             watt_ref, wconv_ref, wout_ref, cw_ref, cb_ref, fg_ref,
             dpb_ref, do_ref, dx2_ref, gatt_o, gconv_o, gout_o, loss_o, gfn_o, gcb_o, gcw_o,
             acc_att, acc_conv, acc_out, carry):
        i = pl.program_id(0)
        tile = n - 1 - i

        @pl.when(i == 0)
        def _():
            acc_att[...] = jnp.zeros_like(acc_att)
            acc_conv[...] = jnp.zeros_like(acc_conv)
            acc_out[...] = jnp.zeros_like(acc_out)
            carry[...] = jnp.zeros_like(carry)
            loss_o[...] = jnp.zeros_like(loss_o)
            gfn_o[...] = jnp.zeros_like(gfn_o)
            gcb_o[...] = jnp.zeros_like(gcb_o)
            gcw_o[...] = jnp.zeros_like(gcw_o)

        att_v = att_ref[...].astype(F32)
        za = za_ref[...].astype(F32)
        sa = _sigmoid(za)
        silu_a = za * sa
        a_b = (att_v * silu_a).astype(BF16)

        gb = gb_ref[...].astype(F32)
        gc = gc_ref[...].astype(F32)
        u = u_ref[...].astype(F32)
        zc = zc_ref[...].astype(F32)
        cu = gc * u
        halo = jnp.where(tile > 0, hgc_ref[...].astype(F32) * hu_ref[...].astype(F32), 0.0)
        cu1 = _shift_down(cu, 1, halo)
        cu2 = _shift_down(cu, 2, halo)
        w0, w1, w2 = cw_ref[0:1, :], cw_ref[1:2, :], cw_ref[2:3, :]
        vconv = w0 * cu2 + w1 * cu1 + w2 * cu + cb_ref[...]
        sc = _sigmoid(zc)
        silu_c = zc * sc
        c_b = (gb * vconv * silu_c).astype(BF16)

        y_att = _nn(a_b, watt_ref[...])
        y_conv = _nn(c_b, wconv_ref[...])
        ga = _sigmoid(gatt_ref[...].astype(F32))
        gv = _sigmoid(gconv_ref[...].astype(F32))
        m_b = (ga * y_att + gv * y_conv).astype(BF16)
        x2 = x_ref[...] + _nn(m_b, wout_ref[...])
        r2 = lax.rsqrt(jnp.mean(x2 * x2, axis=-1, keepdims=True) + EPS)
        x2n = x2 * r2
        fg = fg_ref[...]
        err = x2n * fg - t_ref[...]
        loss_o[...] += jnp.sum(err * err, axis=0, keepdims=True) * (0.5 / d)
        dy = err * (1.0 / d)
        gfn_o[...] += jnp.sum(dy * x2n, axis=0, keepdims=True)
        dyn = dy * fg
        dx2 = r2 * (dyn - x2n * jnp.mean(dyn * x2n, axis=-1, keepdims=True))
        dx2_ref[...] = dx2
        dx2_b = dx2.astype(BF16)

        dm = _nt(dx2_b, wout_ref[...])
        acc_out[...] += _tn(m_b, dx2_b)
        dy_att = dm * ga
        dy_conv = dm * gv
        dpb_ref[:, 5 * dc:5 * dc + d] = (dy_att * y_att * (1.0 - ga)).astype(BF16)
        dpb_ref[:, 5 * dc + d:5 * dc + 2 * d] = (dy_conv * y_conv * (1.0 - gv)).astype(BF16)
        dya_b = dy_att.astype(BF16)
        dyc_b = dy_conv.astype(BF16)
        da_in = _nt(dya_b, watt_ref[...])
        acc_att[...] += _tn(a_b, dya_b)
        dc_in = _nt(dyc_b, wconv_ref[...])
        acc_conv[...] += _tn(c_b, dyc_b)

        do_ref[...] = (da_in * silu_a).astype(BF16)
        dpb_ref[:, 0:dc] = (da_in * att_v * (sa * (1.0 + za * (1.0 - sa)))).astype(BF16)
        dpb_ref[:, dc:2 * dc] = (dc_in * vconv * silu_c).astype(BF16)
        dgs = dc_in * gb
        dvc = dgs * silu_c
        dpb_ref[:, 4 * dc:5 * dc] = (dgs * vconv * (sc * (1.0 + zc * (1.0 - sc)))).astype(BF16)
        gcb_o[...] += jnp.sum(dvc, axis=0, keepdims=True)
        gcw_o[0:1, :] += jnp.sum(dvc * cu2, axis=0, keepdims=True)
        gcw_o[1:2, :] += jnp.sum(dvc * cu1, axis=0, keepdims=True)
        gcw_o[2:3, :] += jnp.sum(dvc * cu, axis=0, keepdims=True)
        nxt = carry[...]
        dcu = w2 * dvc + w1 * _shift_up(dvc, 1, nxt) + w0 * _shift_up(dvc, 2, nxt)
        carry[...] = dvc[0:SUBLANES, :]
        dpb_ref[:, 2 * dc:3 * dc] = (dcu * u).astype(BF16)
        dpb_ref[:, 3 * dc:4 * dc] = (dcu * gc).astype(BF16)

        @pl.when(i == n - 1)
        def _():
            for j in range(n_shards):
                gatt_o[j] = acc_att[:, j * (d // n_shards):(j + 1) * (d // n_shards)].astype(BF16)
                gconv_o[j] = acc_conv[:, j * (d // n_shards):(j + 1) * (d // n_shards)].astype(BF16)
                gout_o[j] = acc_out[j * (d // n_shards):(j + 1) * (d // n_shards), :].astype(BF16)

    rev = lambda width, col_blk: pl.BlockSpec((tm, width), lambda i: (n - 1 - i, col_blk))
    halo_spec = lambda col_blk: pl.BlockSpec(
        (SUBLANES, dc), lambda i: (jnp.maximum((n - 1 - i) * (tm // SUBLANES) - 1, 0), col_blk))
    const = lambda shape: pl.BlockSpec(shape, lambda i: tuple(0 for _ in shape), pipeline_mode=pl.Buffered(1))
    q4 = d // n_shards
    return pl.pallas_call(
        body, name="mixer_mid", grid=(n,),
        in_specs=[rev(dc, 0), rev(dc, 3), rev(dc, 4), rev(dc, 5), rev(dc, 6), rev(dc, 7),
                  halo_spec(5), halo_spec(6), rev(d, 4), rev(d, 5), rev(d, 0), rev(d, 0),
                  const((dc, d)), const((dc, d)), const((d, d)), const((SUBLANES, dc)), const((1, dc)), const((1, d))],
        out_specs=[rev(5 * dc + 2 * d, 0), rev(dc, 0), rev(d, 0),
                   const((n_shards, dc, q4)), const((n_shards, dc, q4)), const((n_shards, q4, d)),
                   const((1, d)), const((1, d)), const((1, dc)), const((SUBLANES, dc))],
        out_shape=[jax.ShapeDtypeStruct((s, 5 * dc + 2 * d), BF16), jax.ShapeDtypeStruct((s, dc), BF16),
                   jax.ShapeDtypeStruct((s, d), F32),
                   jax.ShapeDtypeStruct((n_shards, dc, q4), BF16), jax.ShapeDtypeStruct((n_shards, dc, q4), BF16),
                   jax.ShapeDtypeStruct((n_shards, q4, d), BF16),
                   jax.ShapeDtypeStruct((1, d), F32), jax.ShapeDtypeStruct((1, d), F32),
                   jax.ShapeDtypeStruct((1, dc), F32), jax.ShapeDtypeStruct((SUBLANES, dc), F32)],
        scratch_shapes=[pltpu.VMEM((dc, d), F32), pltpu.VMEM((dc, d), F32), pltpu.VMEM((d, d), F32),
                        pltpu.VMEM((SUBLANES, dc), F32)],
        compiler_params=_params("arbitrary"),
    )(att, proj, proj, proj, proj, proj, proj, proj, proj, proj, x, tgt,
      w_att, w_conv, w_out, conv_w8, conv_b, fin_g)


def _in_proj_bwd_x(dqkv, dpb, w_in, x, dx2, g):
    s, d = x.shape
    tn = dqkv.shape[1]
    nb = dpb.shape[1] // tn
    n = s // TM_MM

    def body(*refs):
        dps, ws = refs[:nb + 1], refs[nb + 1:2 * nb + 2]
        x_ref, dx2_ref, g_ref, gx_ref, gng_ref = refs[2 * nb + 2:]
        i = pl.program_id(0)

        @pl.when(i == 0)
        def _():
            gng_ref[...] = jnp.zeros_like(gng_ref)

        dh = _nt(dps[0][...], ws[0][0])
        for j in range(1, nb + 1):
            dh = dh + _nt(dps[j][...], ws[j][0])
        xv = x_ref[...]
        r = lax.rsqrt(jnp.mean(xv * xv, axis=-1, keepdims=True) + EPS)
        xn = xv * r
        gng_ref[...] += jnp.sum(dh * xn, axis=0, keepdims=True)
        dhn = dh * g_ref[...]
        gx_ref[...] = dx2_ref[...] + r * (dhn - xn * jnp.mean(dhn * xn, axis=-1, keepdims=True))

    tile = lambda width, col_blk: pl.BlockSpec((TM_MM, width), lambda i: (i, col_blk))
    wspec = lambda blk: pl.BlockSpec((1, d, tn), lambda i: (blk, 0, 0), pipeline_mode=pl.Buffered(1))
    return pl.pallas_call(
        body, name="in_proj_bwd_x", grid=(n,),
        in_specs=[tile(tn, 0)] + [tile(tn, j) for j in range(nb)] + [wspec(j) for j in range(nb + 1)]
        + [tile(d, 0), tile(d, 0), pl.BlockSpec((1, d), lambda i: (0, 0))],
        out_specs=[tile(d, 0), pl.BlockSpec((1, d), lambda i: (0, 0))],
        out_shape=[jax.ShapeDtypeStruct((s, d), F32), jax.ShapeDtypeStruct((1, d), F32)],
        compiler_params=_params("arbitrary"),
    )(dqkv, *([dpb] * nb), *([w_in] * (nb + 1)), x, dx2, g)


def _in_proj_bwd_w(h, dqkv, dpb, order):
    s, d = h.shape
    tn = dqkv.shape[1]
    n = s // TM_BLK
    hr = d // 2
    settle = min(2, n - 1)

    def body(order_ref, h_ref, da_ref, db_ref, slots_ref, acc, sendbuf, pairbuf, chipbuf, psend, precv, send, recv, lsem):
        j, i = pl.program_id(0), pl.program_id(1)
        blk = order_ref[j]
        pos = _position()
        x, y, c = pos

        @pl.when(i == 0)
        def _():
            acc[...] = jnp.zeros_like(acc)

        @pl.when(blk == 0)
        def _():
            acc[...] += _tn(h_ref[...], da_ref[...])

        @pl.when(blk > 0)
        def _():
            acc[...] += _tn(h_ref[...], db_ref[...])

        def pair(step, half):
            return pltpu.make_async_remote_copy(
                src_ref=sendbuf.at[step, pl.ds(half * hr, hr), :], dst_ref=pairbuf.at[step],
                send_sem=psend.at[step], recv_sem=precv.at[step], device_id=(x, y, 1 - c), device_id_type=MESH)

        def ici(step):
            flip = step + 1
            return pltpu.make_async_remote_copy(
                src_ref=chipbuf.at[step], dst_ref=slots_ref.at[flip], send_sem=send.at[step], recv_sem=recv.at[step],
                device_id=_peer(pos, 4 * (flip >> 1) + 2 * (flip & 1)), device_id_type=MESH)

        local = pltpu.make_async_copy(chipbuf.at[N_CHIPS - 1], slots_ref.at[0], lsem.at[0])

        def combine(step):
            pair(step, c).wait_recv()
            mine = sendbuf[step, pl.ds(c * hr, hr), :].astype(F32)
            chipbuf[step] = (mine + pairbuf[step].astype(F32)).astype(BF16)

        for step in range(N_CHIPS):
            @pl.when((j == step) & (i == n - 1))
            def _(step=step):
                sendbuf[step] = acc[...].astype(BF16)
                pair(step, 1 - c).start()

        for step in range(N_CHIPS - 1):
            @pl.when((j == step + 1) & (i == settle))
            def _(step=step):
                combine(step)
                ici(step).start()

        @pl.when((j == N_CHIPS - 1) & (i == n - 1))
        def _():
            combine(N_CHIPS - 1)
            local.start()
            for step in range(N_CHIPS - 1):
                ici(step).wait_recv()
            for step in range(N_CHIPS - 1):
                ici(step).wait_send()
            for step in range(N_CHIPS):
                pair(step, 1 - c).wait_send()
            local.wait()

    return pl.pallas_call(
        body, name="in_proj_bwd_w",
        grid_spec=pltpu.PrefetchScalarGridSpec(
            num_scalar_prefetch=1, grid=(N_CHIPS, n),
            in_specs=[pl.BlockSpec((TM_BLK, d), lambda j, i, order: (i, 0)),
                      pl.BlockSpec((TM_BLK, tn), lambda j, i, order: (jnp.where(order[j] == 0, i, 0), 0)),
                      pl.BlockSpec((TM_BLK, tn), lambda j, i, order: (jnp.where(order[j] == 0, 0, i),
                                                                     jnp.maximum(order[j] - 1, 0)))],
            out_specs=[ANY],
            scratch_shapes=[pltpu.VMEM((d, tn), F32), pltpu.VMEM((N_CHIPS, d, tn), BF16),
                            pltpu.VMEM((N_CHIPS, hr, tn), BF16), pltpu.VMEM((N_CHIPS, hr, tn), BF16),
                            pltpu.SemaphoreType.DMA((N_CHIPS,)), pltpu.SemaphoreType.DMA((N_CHIPS,)),
                            pltpu.SemaphoreType.DMA((N_CHIPS - 1,)), pltpu.SemaphoreType.DMA((N_CHIPS - 1,)),
                            pltpu.SemaphoreType.DMA((1,))]),
        out_shape=[jax.ShapeDtypeStruct((N_CHIPS, hr, tn), BF16)],
        compiler_params=_params("arbitrary", "arbitrary"),
    )(order, h, dqkv, dpb)[0]


LOSS_ROW = 6


def _adam_update(w, g, m, v):
    c1 = 1.0 / (1.0 - ADAM_B1 ** ADAM_STEP)
    c2 = 1.0 / (1.0 - ADAM_B2 ** ADAM_STEP)
    m2 = ADAM_B1 * m + (1.0 - ADAM_B1) * g
    v2 = ADAM_B2 * v + (1.0 - ADAM_B2) * (g * g)
    return -ADAM_LR * ((m2 * c1) / (jnp.sqrt(v2 * c2) + ADAM_EPS) + ADAM_WD * w), m2, v2


def _adamw_small(recv, params, moments_m, moments_v):
    k = recv.shape[0]
    n_par = len(params)
    cshard = params[3].shape[1]

    def body(*refs):
        r_ref = refs[0]
        ws, ms, vs = refs[1:1 + n_par], refs[1 + n_par:1 + 2 * n_par], refs[1 + 2 * n_par:1 + 3 * n_par]
        loss_ref = refs[1 + 3 * n_par]
        outs = refs[2 + 3 * n_par:]
        total = r_ref[0]
        for slot in range(1, k):
            total = total + r_ref[slot]
        loss_ref[...] = jnp.sum(total[LOSS_ROW:LOSS_ROW + 1, :], axis=1, keepdims=True)
        chip = 2 * lax.axis_index("x") + lax.axis_index("y")
        g_cw = jnp.zeros((3, cshard), F32)
        for sh in range(N_CHIPS):
            g_cw = g_cw + jnp.where(chip == sh, total[3:6, sh * cshard:(sh + 1) * cshard], 0.0)
        grads = [total[0:1, :], total[1:2, :], total[2:3, :ws[2].shape[1]], g_cw, total[8:16, :ws[4].shape[1]]]
        for p in range(n_par):
            delta, m2, v2 = _adam_update(ws[p][...], grads[p], ms[p][...], vs[p][...])
            for q, val in enumerate((grads[p], delta, m2, v2)):
                outs[4 * p + q][...] = val

    shapes = [jax.ShapeDtypeStruct((1, 1), F32)]
    for p in params:
        shapes += [jax.ShapeDtypeStruct(p.shape, F32)] * 4
    return pl.pallas_call(body, name="adamw_small", out_shape=shapes)(recv, *params, *moments_m, *moments_v)


def _adamw(w, g, m, v, name, rows_per_step):
    r, c = w.shape

    def body(w_ref, g_ref, m_ref, v_ref, d_ref, mo_ref, vo_ref):
        d_ref[...], mo_ref[...], vo_ref[...] = _adam_update(w_ref[...], g_ref[...], m_ref[...], v_ref[...])

    spec = pl.BlockSpec((rows_per_step, c), lambda i: (i, 0))
    shape = jax.ShapeDtypeStruct((r, c), F32)
    return pl.pallas_call(
        body, name=name, grid=(r // rows_per_step,),
        in_specs=[spec] * 4, out_specs=[spec] * 3, out_shape=[shape] * 3,
        compiler_params=_params("parallel"),
    )(w, g, m, v)


ANY = pl.BlockSpec(memory_space=pl.ANY)
N_CHIPS = 4
N_DEV = 8


def _position():
    return lax.axis_index("x"), lax.axis_index("y"), lax.axis_index("c")


def _gather_out_shapes(shards, kinds, cw8):
    full = [(a.shape[0], a.shape[1] * N_CHIPS) if k == "cols" else (a.shape[0] * N_CHIPS, a.shape[1])
            for a, k in zip(shards, kinds)]
    return [jax.ShapeDtypeStruct(f, BF16) for f in full] + [
        jax.ShapeDtypeStruct((N_CHIPS,) + cw8.shape, cw8.dtype)]


def _gather_sems(nw):
    return [pltpu.SemaphoreType.DMA((3, nw)), pltpu.SemaphoreType.DMA((3, nw)),
            pltpu.SemaphoreType.DMA((3, nw)), pltpu.SemaphoreType.DMA((3, nw)),
            pltpu.SemaphoreType.DMA((3,)), pltpu.SemaphoreType.DMA((3,)), pltpu.SemaphoreType.DMA((nw + 1,))]


def _gather_plan(kinds, srcs, dsts, cw, cw_all, send1, recv1, send2, recv2, ssend, srecv, lsem):
    nw = len(srcs)
    x, y, c = _position()
    mine = 2 * x + y
    chips = [(x, 1 - y), (1 - x, y), (1 - x, 1 - y)]

    def window(w, shard, half):
        r, cc = srcs[w].shape
        hr = r // 2
        if kinds[w] == "cols":
            rows = pl.ds(0, r) if half is None else pl.ds(half * hr, hr)
            return dsts[w].at[rows, pl.ds(shard * cc, cc)]
        rows = pl.ds(shard * r, r) if half is None else pl.ds(shard * r + half * hr, hr)
        return dsts[w].at[rows, :]

    def my_half(w):
        hr = srcs[w].shape[0] // 2
        return srcs[w].at[pl.ds(c * hr, hr), :]

    def local():
        return [pltpu.make_async_copy(srcs[w], window(w, mine, None), lsem.at[w]) for w in range(nw)] + [
            pltpu.make_async_copy(cw, cw_all.at[mine], lsem.at[nw])]

    def ici(k, w, shard):
        kx, ky = chips[k]
        return pltpu.make_async_remote_copy(
            src_ref=my_half(w), dst_ref=window(w, shard, c), send_sem=send1.at[k, w], recv_sem=recv1.at[k, w],
            device_id=(kx, ky, c), device_id_type=MESH)

    def d2d(k, w, shard, half):
        return pltpu.make_async_remote_copy(
            src_ref=window(w, shard, half), dst_ref=window(w, shard, half),
            send_sem=send2.at[k, w], recv_sem=recv2.at[k, w], device_id=(x, y, 1 - c), device_id_type=MESH)

    def small(k, shard):
        kx, ky = chips[k]
        return pltpu.make_async_remote_copy(
            src_ref=cw, dst_ref=cw_all.at[shard], send_sem=ssend.at[k], recv_sem=srecv.at[k],
            device_id=(kx, ky, c), device_id_type=MESH)

    def theirs(k):
        kx, ky = chips[k]
        return 2 * kx + ky

    def start():
        for cp in local():
            cp.start()
        for k in range(3):
            for w in range(nw):
                ici(k, w, mine).start()
            small(k, mine).start()

    def forward():
        for k in range(3):
            for w in range(nw):
                ici(k, w, theirs(k)).wait_recv()
                d2d(k, w, theirs(k), c).start()

    def finish():
        for k in range(3):
            for w in range(nw):
                d2d(k, w, theirs(k), 1 - c).wait_recv()
            small(k, theirs(k)).wait_recv()
        for k in range(3):
            for w in range(nw):
                ici(k, w, mine).wait_send()
                d2d(k, w, theirs(k), c).wait_send()
            small(k, mine).wait_send()
        for cp in local():
            cp.wait()

    return start, forward, finish


def _scatter_out_shapes(parts):
    return [jax.ShapeDtypeStruct((N_DEV, p.shape[1] // 2, p.shape[2]), p.dtype) for p in parts]


def _scatter_sems(nw):
    return [pltpu.SemaphoreType.DMA((N_DEV, nw)), pltpu.SemaphoreType.DMA((N_DEV, nw)), pltpu.SemaphoreType.DMA((nw,))]


def _peer(pos, k):
    x, y, c = pos
    return ((1 - x) if k & 4 else x, (1 - y) if k & 2 else y, (1 - c) if k & 1 else c)


def _scatter_plan(srcs, dsts, send, recv, lsem):
    nw = len(srcs)
    pos = _position()

    def piece(w, k):
        px, py, pc = _peer(pos, k)
        hr = srcs[w].shape[1] // 2
        return srcs[w].at[2 * px + py, pl.ds(pc * hr, hr), :]

    def remote(w, k):
        return pltpu.make_async_remote_copy(
            src_ref=piece(w, k), dst_ref=dsts[w].at[k], send_sem=send.at[k, w], recv_sem=recv.at[k, w],
            device_id=_peer(pos, k), device_id_type=MESH)

    def local(w):
        return pltpu.make_async_copy(piece(w, 0), dsts[w].at[0], lsem.at[w])

    def start():
        for w in range(nw):
            local(w).start()
        for k in range(1, N_DEV):
            for w in range(nw):
                remote(w, k).start()

    def finish():
        for k in range(1, N_DEV):
            for w in range(nw):
                remote(w, k).wait_recv()
        for k in range(1, N_DEV):
            for w in range(nw):
                remote(w, k).wait_send()
        for w in range(nw):
            local(w).wait()

    return start, finish


def _reduce_pair(slots, small):
    nw = len(slots)

    def body(*refs):
        srcs, sm = refs[:nw], refs[nw]
        dsts, sm_all = refs[nw + 1:2 * nw + 1], refs[2 * nw + 1]
        halves = refs[2 * nw + 2:3 * nw + 2]
        send, recv, ssend, srecv, lsem = refs[3 * nw + 2:]
        pos = _position()
        x, y, c = pos
        me = 4 * x + 2 * y + c

        def rows(w, half):
            hr = halves[w].shape[0]
            return dsts[w].at[pl.ds(half * hr, hr), :]

        def remote(w, half):
            return pltpu.make_async_remote_copy(
                src_ref=halves[w], dst_ref=rows(w, half), send_sem=send.at[w], recv_sem=recv.at[w],
                device_id=(x, y, 1 - c), device_id_type=MESH)

        def bcast(k, slot):
            return pltpu.make_async_remote_copy(
                src_ref=sm, dst_ref=sm_all.at[slot], send_sem=ssend.at[k], recv_sem=srecv.at[k],
                device_id=_peer(pos, k), device_id_type=MESH)

        small_copies = [bcast(k, me) for k in range(1, N_DEV)]
        own_small = pltpu.make_async_copy(sm, sm_all.at[me], lsem.at[nw])
        for cp in small_copies + [own_small]:
            cp.start()
        big = []
        for w in range(nw):
            total = srcs[w][0].astype(F32)
            for k in range(1, srcs[w].shape[0]):
                total = total + srcs[w][k].astype(F32)
            halves[w][...] = total
            big += [remote(w, c), pltpu.make_async_copy(halves[w], rows(w, c), lsem.at[w])]
            big[-2].start()
            big[-1].start()
        for w in range(nw):
            remote(w, 1 - c).wait_recv()
        for k in range(1, N_DEV):
            px, py, pc = _peer(pos, k)
            bcast(k, 4 * px + 2 * py + pc).wait_recv()
        for w in range(nw):
            big[2 * w].wait_send()
            big[2 * w + 1].wait()
        for cp in small_copies:
            cp.wait_send()
        own_small.wait()

    vmem = pl.BlockSpec(memory_space=pltpu.VMEM)
    half_shapes = [(sl.shape[1], sl.shape[2]) for sl in slots]
    return pl.pallas_call(
        body, name="reduce_pair",
        in_specs=[vmem] * (nw + 1), out_specs=[ANY] * (nw + 1),
        out_shape=[jax.ShapeDtypeStruct((2 * r, cc), F32) for r, cc in half_shapes]
        + [jax.ShapeDtypeStruct((N_DEV,) + small.shape, small.dtype)],
        scratch_shapes=[pltpu.VMEM(hs, F32) for hs in half_shapes]
        + [pltpu.SemaphoreType.DMA((nw,)), pltpu.SemaphoreType.DMA((nw,)),
           pltpu.SemaphoreType.DMA((N_DEV,)), pltpu.SemaphoreType.DMA((N_DEV,)),
           pltpu.SemaphoreType.DMA((nw + 1,))],
        compiler_params=pltpu.CompilerParams(vmem_limit_bytes=VMEM_LIMIT),
    )(*slots, small)


def _pad_to(a, rows, cols):
    return jnp.pad(a, ((0, rows - a.shape[0]), (0, cols - a.shape[1])))


def _pack_small(norm_g, fin_g, conv_b, conv_w, loss_vec, rel):
    rows = [_pad_to(norm_g, 1, SMALL_COLS), _pad_to(fin_g, 1, SMALL_COLS), _pad_to(conv_b, 1, SMALL_COLS),
            _pad_to(conv_w, 3, SMALL_COLS), _pad_to(loss_vec, 2, SMALL_COLS), _pad_to(rel, HEADS, SMALL_COLS)]
    return jnp.concatenate(rows, axis=0)


def kernel(x, norm_g, w_in, rel_bias, w_att_out, conv_w, conv_b, w_conv_out, w_out, final_norm_g, loss_target, m_norm_g, m_w_in, m_rel_bias, m_w_att_out, m_conv_w, m_conv_b, m_w_conv_out, m_w_out, m_final_norm_g, v_norm_g, v_w_in, v_rel_bias, v_w_att_out, v_conv_w, v_conv_b, v_w_conv_out, v_w_out, v_final_norm_g):
    xs, tgt = x[0], loss_target[0]
    cshard = conv_w.shape[2]
    chip = 2 * lax.axis_index("x") + lax.axis_index("y")

    shards = [w_in[0], w_att_out[0], w_conv_out[0], w_out[0]]
    cw8 = _pad_to(conv_w[0], SUBLANES, cshard)
    flips = jnp.arange(N_CHIPS, dtype=jnp.int32)
    own_first = jnp.bitwise_xor(chip, flips)
    own_last = jnp.bitwise_xor(chip, (flips + 1) % N_CHIPS)

    proj, h, wb_in = _in_proj_gather(xs, norm_g, shards[0], own_first)
    diag = jnp.take(rel_bias[0], _diag_rel_index(), axis=1)
    att, lse, wb_att, wb_conv, wb_out, cw_all = _attn_fwd(diag, proj, shards[1:], ["cols", "cols", "rows"], cw8)
    conv_w_full = jnp.transpose(cw_all, (1, 0, 2)).reshape(SUBLANES, N_CHIPS * cshard)
    (dpb, d_att, dx2, g_att_p, g_conv_p, g_out_p, loss_vec, g_fin, g_cb, g_cw) = _mixer_mid(
        att, proj, xs, tgt, wb_att, wb_conv, wb_out, conv_w_full, conv_b, final_norm_g[None, :])
    dqkv, g_rel, r_att, r_conv, r_out = _attn_bwd(diag, proj, d_att, att, lse, [g_att_p, g_conv_p, g_out_p])
    grad_x, g_norm = _in_proj_bwd_x(dqkv, dpb, wb_in, xs, dx2, norm_g)
    r_in = _in_proj_bwd_w(h, dqkv, dpb, own_last)

    small = _pack_small(g_norm, g_fin, g_cb, g_cw[0:3], loss_vec, g_rel)
    gw_in, gw_att, gw_conv, gw_out, r_small = _reduce_pair([r_in, r_att, r_conv, r_out], small)
    rel_pad = g_rel.shape[1]
    pad_rel = lambda a: _pad_to(a[0], HEADS, rel_pad)
    small_out = _adamw_small(
        r_small,
        [norm_g, final_norm_g[None, :], conv_b, conv_w[0], pad_rel(rel_bias)],
        [m_norm_g, m_final_norm_g[None, :], m_conv_b, m_conv_w[0], pad_rel(m_rel_bias)],
        [v_norm_g, v_final_norm_g[None, :], v_conv_b, v_conv_w[0], pad_rel(v_rel_bias)])
    loss = small_out[0][0, 0]
    small_names = ["norm_g", "final_norm_g", "conv_b", "conv_w", "rel_bias"]
    fix = {"norm_g": lambda a: a, "final_norm_g": lambda a: a[0], "conv_b": lambda a: a,
           "conv_w": lambda a: a[None], "rel_bias": lambda a: a[None, :, :N_REL]}
    small_res = {name: [fix[name](small_out[1 + 4 * p + q]) for q in range(4)] for p, name in enumerate(small_names)}

    big = {}
    for name, w, g, m, v, rows in (("w_in", w_in, gw_in, m_w_in, v_w_in, 128),
                                   ("w_att_out", w_att_out, gw_att, m_w_att_out, v_w_att_out, 256),
                                   ("w_conv_out", w_conv_out, gw_conv, m_w_conv_out, v_w_conv_out, 256),
                                   ("w_out", w_out, gw_out, m_w_out, v_w_out, 128)):
        dw, mw, vw = _adamw(w[0], g, m[0], v[0], "adamw_" + name, rows)
        big[name] = (g[None], dw[None], mw[None], vw[None])

    order = ["norm_g", "w_in", "rel_bias", "w_att_out", "conv_w", "conv_b", "w_conv_out", "w_out", "final_norm_g"]
    outs = [loss, grad_x[None]]
    for which in range(4):
        for name in order:
            outs.append(big[name][which] if name in big else small_res[name][which])
    return tuple(outs)
```

```python
import numpy as np
import jax
import jax.numpy as jnp
from jax import lax
from jax.experimental import pallas as pl
from jax.experimental.pallas import tpu as pltpu

F32 = jnp.float32
BF16 = jnp.bfloat16
MESH = pl.DeviceIdType.MESH

CHUNK = 64
N_LEFT = 8
HEADS = 8
HEAD_DIM = 64
D_ATT = HEADS * HEAD_DIM
MAX_REL = 128
N_REL = 2 * MAX_REL + 1
EPS = 1e-6
NEG_BIG = -1e30
ADAM_LR, ADAM_B1, ADAM_B2, ADAM_EPS, ADAM_WD, ADAM_STEP = 0.001, 0.9, 0.999, 1e-08, 0.01, 10

LANES = 128
SUBLANES = 8
VMEM_LIMIT = 56 * 1024 * 1024

QB = 2 * CHUNK
KW = N_LEFT * CHUNK + QB
DIAG = KW + QB
TQ = N_LEFT * CHUNK
TM_MID = 256
TM_MM = 512
TM_BLK = 1024
SMALL_ROWS, SMALL_COLS = 16, 1024


def _params(*sem):
    return pltpu.CompilerParams(dimension_semantics=sem, vmem_limit_bytes=VMEM_LIMIT)


def _nt(a, b):
    return lax.dot_general(a, b, (((1,), (1,)), ((), ())), preferred_element_type=F32)


def _tn(a, b):
    return lax.dot_general(a, b, (((0,), (0,)), ((), ())), preferred_element_type=F32)


def _nn(a, b):
    return jnp.dot(a, b, preferred_element_type=F32)


def _diag_rel_index():
    d = np.arange(DIAG)
    diff = np.where(d < KW, d, d - DIAG)
    rel = N_LEFT * CHUNK - diff
    return np.clip(rel, -MAX_REL, MAX_REL) + MAX_REL


def _build_bias(diag_ref, bias_scr):
    r = lax.broadcasted_iota(jnp.int32, (QB, KW), 0) // CHUNK
    s = lax.broadcasted_iota(jnp.int32, (QB, KW), 1) // CHUNK
    allowed = (s >= r) & (s <= r + N_LEFT)
    for h in range(HEADS):
        row = jnp.broadcast_to(diag_ref[h:h + 1, :], (QB, DIAG))
        t = pltpu.roll(row, 0, 1, stride=1, stride_axis=0)
        bias_scr[h // 2, (h % 2) * QB:(h % 2 + 1) * QB, :] = jnp.where(allowed, t[:, :KW], NEG_BIG)


def _stack_heads(a, lane_hi):
    zero = jnp.zeros_like(a)
    return jnp.concatenate([jnp.where(lane_hi, zero, a), jnp.where(lane_hi, a, zero)], axis=0)


def _in_proj_gather(x, g, shard, order):
    s, d = x.shape
    tn = shard.shape[1]
    n = s // TM_BLK
    hr = d // 2

    def body(order_ref, x_ref, g_ref, shard32_ref, proj_ref, h_ref, wfull_ref, shard_ref, hbuf, wbuf,
             send1, recv1, send2, recv2, lsem):
        del order_ref
        j, i = pl.program_id(0), pl.program_id(1)
        x, y, c = _position()
        mine = 2 * x + y
        chips = [(x, 1 - y), (1 - x, y), (1 - x, 1 - y)]

        def theirs(k):
            return 2 * chips[k][0] + chips[k][1]

        def half_rows(half):
            return pl.ds(half * hr, hr)

        def ici(k, shard_index):
            return pltpu.make_async_remote_copy(
                src_ref=shard_ref.at[half_rows(c), :], dst_ref=wfull_ref.at[shard_index, half_rows(c), :],
                send_sem=send1.at[k], recv_sem=recv1.at[k], device_id=(*chips[k], c), device_id_type=MESH)

        def d2d(k, half):
            return pltpu.make_async_remote_copy(
                src_ref=wbuf.at[k % 2, half_rows(half), :], dst_ref=wfull_ref.at[theirs(k), half_rows(half), :],
                send_sem=send2.at[k], recv_sem=recv2.at[k], device_id=(x, y, 1 - c), device_id_type=MESH)

        def load(k, half, sem):
            return pltpu.make_async_copy(wfull_ref.at[theirs(k), half_rows(half), :],
                                         wbuf.at[k % 2, half_rows(half), :], lsem.at[sem])

        own = pltpu.make_async_copy(shard_ref, wfull_ref.at[mine], lsem.at[0])

        @pl.when((j == 0) & (i == 0))
        def _():
            shard_ref[...] = shard32_ref[...].astype(BF16)
            own.start()
            ici(0, mine).start()
            ici(1, mine).start()

        for k in range(3):
            first = max(n - 3, 0) if k == 0 else min(n // 2, n - 1)

            @pl.when((j == k) & (i == first))
            def _(k=k):
                if k == 0:
                    ici(0, mine).wait_send()
                    ici(1, mine).wait_send()
                    ici(2, mine).start()
                if k == 2:
                    d2d(0, c).wait_send()
                ici(k, theirs(k)).wait_recv()
                load(k, c, 1).start()

            @pl.when((j == k) & (i == min(first + 1, n - 1)))
            def _(k=k):
                load(k, c, 1).wait()
                d2d(k, c).start()

            @pl.when((j == k) & (i == min(first + 2, n - 1)))
            def _(k=k):
                d2d(k, 1 - c).wait_recv()
                load(k, 1 - c, 2).start()

            @pl.when((j == k + 1) & (i == 0))
            def _(k=k):
                load(k, 1 - c, 2).wait()

        @pl.when(j == 0)
        def _():
            xv = x_ref[...]
            r = lax.rsqrt(jnp.mean(xv * xv, axis=-1, keepdims=True) + EPS)
            hv = ((xv * r) * g_ref[...]).astype(BF16)
            hbuf[i] = hv
            h_ref[...] = hv
            proj_ref[...] = _nn(hv, shard_ref[...]).astype(BF16)

        for k in range(3):
            @pl.when(j == k + 1)
            def _(k=k):
                proj_ref[...] = _nn(hbuf[i], wbuf[k % 2]).astype(BF16)

        @pl.when((j == 3) & (i == n - 1))
        def _():
            ici(2, mine).wait_send()
            d2d(1, c).wait_send()
            d2d(2, c).wait_send()
            own.wait()

    return pl.pallas_call(
        body, name="in_proj_gather",
        grid_spec=pltpu.PrefetchScalarGridSpec(
            num_scalar_prefetch=1, grid=(N_CHIPS, n),
            in_specs=[pl.BlockSpec((TM_BLK, d), lambda j, i, order: (jnp.where(j == 0, i, n - 1), 0)),
                      pl.BlockSpec((1, d), lambda j, i, order: (0, 0)), pl.BlockSpec(memory_space=pltpu.VMEM)],
            out_specs=[pl.BlockSpec((TM_BLK, tn), lambda j, i, order: (i, order[j])),
                       pl.BlockSpec((TM_BLK, d), lambda j, i, order: (jnp.where(j == 0, i, n - 1), 0)), ANY],
            scratch_shapes=[pltpu.VMEM((d, tn), BF16), pltpu.VMEM((n, TM_BLK, d), BF16), pltpu.VMEM((2, d, tn), BF16),
                            pltpu.SemaphoreType.DMA((3,)), pltpu.SemaphoreType.DMA((3,)),
                            pltpu.SemaphoreType.DMA((3,)), pltpu.SemaphoreType.DMA((3,)), pltpu.SemaphoreType.DMA((3,))]),
        out_shape=[jax.ShapeDtypeStruct((s, N_CHIPS * tn), BF16), jax.ShapeDtypeStruct((s, d), BF16),
                   jax.ShapeDtypeStruct((N_CHIPS, d, tn), BF16)],
        compiler_params=_params("arbitrary", "arbitrary"),
    )(order, x, g, shard)


def _attn_fwd(diag, proj, shards, kinds, cw8):
    s = proj.shape[0]
    n = s // TQ
    nw = len(shards)
    scale = HEAD_DIM ** -0.5

    def body(*refs):
        diag_ref, q_ref, kp_ref, kc_ref, vp_ref, vc_ref = refs[:6]
        srcs, cw = refs[6:6 + nw], refs[6 + nw]
        o_ref, lse_ref = refs[7 + nw:9 + nw]
        dsts, cw_all = refs[9 + nw:9 + 2 * nw], refs[9 + 2 * nw]
        bias_scr, kcat, vcat = refs[10 + 2 * nw:13 + 2 * nw]
        casts = refs[13 + 2 * nw:13 + 3 * nw]
        start, forward, finish = _gather_plan(kinds, casts, dsts, cw, cw_all, *refs[13 + 3 * nw:])
        i = pl.program_id(0)

        @pl.when(i == 0)
        def _():
            for w in range(nw):
                casts[w][...] = srcs[w][...].astype(BF16)
            start()
            _build_bias(diag_ref, bias_scr)

        @pl.when(i == n // 2)
        def _():
            forward()

        @pl.when(i == n - 1)
        def _():
            finish()

        kcat[0:TQ, :] = kp_ref[...]
        kcat[TQ:2 * TQ, :] = kc_ref[...]
        vcat[0:TQ, :] = vp_ref[...]
        vcat[TQ:2 * TQ, :] = vc_ref[...]
        lane_hi = lax.broadcasted_iota(jnp.int32, (QB, LANES), 1) >= HEAD_DIM
        col = lax.broadcasted_iota(jnp.int32, (2 * QB, KW), 1)

        def make_block(first_tile):
            def block(b, carry):
                r0 = pl.multiple_of(b * QB, QB)
                for p in range(HEADS // 2):
                    lanes = slice(LANES * p, LANES * (p + 1))
                    q2 = _stack_heads(q_ref[pl.ds(r0, QB), lanes] * scale, lane_hi)
                    kw = kcat[pl.ds(r0, KW), lanes]
                    vw = vcat[pl.ds(r0, KW), lanes]
                    sc = _nt(q2, kw) + bias_scr[p]
                    if first_tile:
                        sc = jnp.where(col >= TQ - r0, sc, NEG_BIG)
                    m = jnp.max(sc, axis=1, keepdims=True)
                    pe = jnp.exp(sc - m)
                    l = jnp.sum(pe, axis=1, keepdims=True)
                    o2 = _nn(pe.astype(BF16), vw) / l
                    lse2 = m + jnp.log(l)
                    lse_ref[pl.ds(r0, QB), 2 * p:2 * p + 1] = lse2[0:QB, :]
                    lse_ref[pl.ds(r0, QB), 2 * p + 1:2 * p + 2] = lse2[QB:2 * QB, :]
                    o_ref[pl.ds(r0, QB), lanes] = jnp.where(lane_hi, o2[QB:2 * QB, :], o2[0:QB, :]).astype(BF16)
                return carry
            return block

        @pl.when(i == 0)
        def _():
            lax.fori_loop(0, TQ // QB, make_block(True), 0)

        @pl.when(i > 0)
        def _():
            lax.fori_loop(0, TQ // QB, make_block(False), 0)

    blk = lambda col_blk, prev: pl.BlockSpec(
        (TQ, D_ATT), (lambda i: (jnp.maximum(i - 1, 0), col_blk)) if prev else (lambda i: (i, col_blk)))
    vmem = pl.BlockSpec(memory_space=pltpu.VMEM)
    return pl.pallas_call(
        body, name="attn_fwd", grid=(n,),
        in_specs=[pl.BlockSpec((HEADS, DIAG), lambda i: (0, 0)),
                  blk(0, False), blk(1, True), blk(1, False), blk(2, True), blk(2, False)] + [vmem] * (nw + 1),
        out_specs=[pl.BlockSpec((TQ, D_ATT), lambda i: (i, 0)), pl.BlockSpec((TQ, HEADS), lambda i: (i, 0))]
        + [ANY] * (nw + 1),
        out_shape=[jax.ShapeDtypeStruct((s, D_ATT), BF16), jax.ShapeDtypeStruct((s, HEADS), F32)]
        + _gather_out_shapes(shards, kinds, cw8),
        scratch_shapes=[pltpu.VMEM((HEADS // 2, 2 * QB, KW), F32), pltpu.VMEM((2 * TQ, D_ATT), BF16),
                        pltpu.VMEM((2 * TQ, D_ATT), BF16)] + [pltpu.VMEM(a.shape, BF16) for a in shards]
        + _gather_sems(nw),
        compiler_params=_params("arbitrary"),
    )(diag, proj, proj, proj, proj, proj, *shards, cw8)


def _attn_bwd(diag, proj, d_att, att, lse, parts):
    s = proj.shape[0]
    n = s // TQ
    npart = len(parts)
    scale = HEAD_DIM ** -0.5
    rel_pad = 3 * LANES

    def body(*refs):
        diag_ref, q_ref, kp_ref, kc_ref, vp_ref, vc_ref, do_ref, o_ref, lse_ref = refs[:9]
        part_refs = refs[9:9 + npart]
        dqkv_ref, dbias_ref = refs[9 + npart:11 + npart]
        slot_refs = refs[11 + npart:11 + 2 * npart]
        bias_scr, dbias_acc, kcat, vcat, dk_acc, dv_acc, dq_scr = refs[11 + 2 * npart:18 + 2 * npart]
        start, finish = _scatter_plan(part_refs, slot_refs, *refs[18 + 2 * npart:])
        i = pl.program_id(0)

        @pl.when(i == 0)
        def _():
            start()
            _build_bias(diag_ref, bias_scr)
            dbias_acc[...] = jnp.zeros_like(dbias_acc)
            dk_acc[...] = jnp.zeros_like(dk_acc)
            dv_acc[...] = jnp.zeros_like(dv_acc)

        @pl.when(i > 0)
        def _():
            dqkv_ref[:, 0:D_ATT] = dq_scr[...]
            dk_acc[0:TQ, :] = dk_acc[TQ:2 * TQ, :]
            dk_acc[TQ:2 * TQ, :] = jnp.zeros((TQ, D_ATT), F32)
            dv_acc[0:TQ, :] = dv_acc[TQ:2 * TQ, :]
            dv_acc[TQ:2 * TQ, :] = jnp.zeros((TQ, D_ATT), F32)

        @pl.when(i < n)
        def _():
            kcat[0:TQ, :] = kp_ref[...]
            kcat[TQ:2 * TQ, :] = kc_ref[...]
            vcat[0:TQ, :] = vp_ref[...]
            vcat[TQ:2 * TQ, :] = vc_ref[...]

        lane_hi = lax.broadcasted_iota(jnp.int32, (QB, LANES), 1) >= HEAD_DIM
        col = lax.broadcasted_iota(jnp.int32, (2 * QB, KW), 1)

        def make_block(first_tile):
            def block(b, carry):
                r0 = pl.multiple_of(b * QB, QB)
                for p in range(HEADS // 2):
                    lanes = slice(LANES * p, LANES * (p + 1))
                    q2 = _stack_heads(q_ref[pl.ds(r0, QB), lanes] * scale, lane_hi)
                    kw = kcat[pl.ds(r0, KW), lanes]
                    vw = vcat[pl.ds(r0, KW), lanes]
                    dop = do_ref[pl.ds(r0, QB), lanes]
                    do2 = _stack_heads(dop, lane_hi)
                    prod = dop.astype(F32) * o_ref[pl.ds(r0, QB), lanes].astype(F32)
                    delta2 = jnp.concatenate(
                        [jnp.sum(jnp.where(lane_hi, 0.0, prod), axis=1, keepdims=True),
                         jnp.sum(jnp.where(lane_hi, prod, 0.0), axis=1, keepdims=True)], axis=0)
                    lse2 = jnp.concatenate([lse_ref[pl.ds(r0, QB), 2 * p:2 * p + 1],
                                            lse_ref[pl.ds(r0, QB), 2 * p + 1:2 * p + 2]], axis=0)
                    sc = _nt(q2, kw) + bias_scr[p]
                    if first_tile:
                        sc = jnp.where(col >= TQ - r0, sc, NEG_BIG)
                    pr = jnp.exp(sc - lse2)
                    ds = pr * (_nt(do2, vw) - delta2)
                    dbias_acc[p] += ds
                    dsb = ds.astype(BF16)
                    dv_acc[pl.ds(r0, KW), lanes] += _tn(pr.astype(BF16), do2)
                    dk_acc[pl.ds(r0, KW), lanes] += _tn(dsb, q2)
                    dq2 = _nn(dsb, kw)
                    dq = jnp.where(lane_hi, dq2[QB:2 * QB, :], dq2[0:QB, :]) * scale
                    dq_scr[pl.ds(r0, QB), lanes] = dq.astype(BF16)
                return carry
            return block

        @pl.when(i == 0)
        def _():
            lax.fori_loop(0, TQ // QB, make_block(True), 0)

        @pl.when((i > 0) & (i < n))
        def _():
            lax.fori_loop(0, TQ // QB, make_block(False), 0)

        @pl.when(i > 0)
        def _():
            dqkv_ref[:, D_ATT:2 * D_ATT] = dk_acc[0:TQ, :].astype(BF16)
            dqkv_ref[:, 2 * D_ATT:3 * D_ATT] = dv_acc[0:TQ, :].astype(BF16)

        @pl.when(i == n)
        def _():
            d_iota = lax.broadcasted_iota(jnp.int32, (DIAG, rel_pad), 0)
            n_iota = lax.broadcasted_iota(jnp.int32, (DIAG, rel_pad), 1)
            diff = jnp.where(d_iota < KW, d_iota, d_iota - DIAG)
            idx = jnp.clip(N_LEFT * CHUNK - diff, -MAX_REL, MAX_REL) + MAX_REL
            onehot = (idx == n_iota).astype(F32)
            rows = []
            for hd in range(HEADS):
                acc = dbias_acc[hd // 2, (hd % 2) * QB:(hd % 2 + 1) * QB, :]
                a = jnp.concatenate([acc, jnp.zeros((QB, DIAG - KW), F32)], axis=1)
                g8 = a[0:SUBLANES, :]
                for blk in range(1, QB // SUBLANES):
                    g8 = g8 + pltpu.roll(a[blk * SUBLANES:(blk + 1) * SUBLANES, :], DIAG - blk * SUBLANES, 1)
                g1 = g8[0:1, :]
                for r in range(1, SUBLANES):
                    g1 = g1 + pltpu.roll(g8[r:r + 1, :], DIAG - r, 1)
                rows.append(g1)
            g = jnp.concatenate(rows, axis=0)
            dbias_ref[...] = jnp.dot(g, onehot, preferred_element_type=F32, precision=lax.Precision.HIGHEST)
            finish()

    last = n - 1
    cur = lambda col_blk: pl.BlockSpec((TQ, D_ATT), lambda i: (jnp.minimum(i, last), col_blk))
    prev = lambda col_blk: pl.BlockSpec((TQ, D_ATT), lambda i: (jnp.maximum(jnp.minimum(i, last) - 1, 0), col_blk))
    return pl.pallas_call(
        body, name="attn_bwd", grid=(n + 1,),
        in_specs=[pl.BlockSpec((HEADS, DIAG), lambda i: (0, 0)),
                  cur(0), prev(1), cur(1), prev(2), cur(2), cur(0), cur(0),
                  pl.BlockSpec((TQ, HEADS), lambda i: (jnp.minimum(i, last), 0))] + [ANY] * npart,
        out_specs=[pl.BlockSpec((TQ, 3 * D_ATT), lambda i: (jnp.maximum(i - 1, 0), 0)),
                   pl.BlockSpec((HEADS, rel_pad), lambda i: (0, 0))] + [ANY] * npart,
        out_shape=[jax.ShapeDtypeStruct((s, 3 * D_ATT), BF16), jax.ShapeDtypeStruct((HEADS, rel_pad), F32)]
        + _scatter_out_shapes(parts),
        scratch_shapes=[pltpu.VMEM((HEADS // 2, 2 * QB, KW), F32), pltpu.VMEM((HEADS // 2, 2 * QB, KW), F32),
                        pltpu.VMEM((2 * TQ, D_ATT), BF16), pltpu.VMEM((2 * TQ, D_ATT), BF16),
                        pltpu.VMEM((2 * TQ, D_ATT), F32), pltpu.VMEM((2 * TQ, D_ATT), F32),
                        pltpu.VMEM((TQ, D_ATT), BF16)] + _scatter_sems(npart),
        compiler_params=_params("arbitrary"),
    )(diag, proj, proj, proj, proj, proj, d_att, att, lse, *parts)


def _shift_down(a, k, halo):
    rolled = pltpu.roll(a, k, 0)
    row = lax.broadcasted_iota(jnp.int32, halo.shape, 0)
    first = jnp.where(row < k, pltpu.roll(halo, k, 0), rolled[0:SUBLANES, :])
    return jnp.concatenate([first, rolled[SUBLANES:, :]], axis=0)


def _shift_up(a, k, nxt):
    tm = a.shape[0]
    rolled = pltpu.roll(a, tm - k, 0)
    row = lax.broadcasted_iota(jnp.int32, nxt.shape, 0)
    last = jnp.where(row >= SUBLANES - k, pltpu.roll(nxt, SUBLANES - k, 0), rolled[tm - SUBLANES:, :])
    return jnp.concatenate([rolled[:tm - SUBLANES, :], last], axis=0)


def _sigmoid(v):
    return 0.5 * jnp.tanh(0.5 * v) + 0.5


def _mixer_mid(att, proj, x, tgt, w_att, w_conv, w_out, conv_w8, conv_b, fin_g):
    s, d = x.shape
    dc = D_ATT
    n = s // TM_MID
    tm = TM_MID
    n_shards = 4

    def body(att_ref, za_ref, gb_ref, gc_ref, u_ref, zc_ref, hgc_ref, hu_ref, gatt_ref, gconv_ref, x_ref, t_ref,
             watt_ref, wconv_ref, wout_ref, cw_ref, cb_ref, fg_ref,
             dpb_ref, do_ref, dx2_ref, gatt_o, gconv_o, gout_o, loss_o, gfn_o, gcb_o, gcw_o,
             acc_att, acc_conv, acc_out, carry):
        i = pl.program_id(0)
        tile = n - 1 - i

        @pl.when(i == 0)
        def _():
            acc_att[...] = jnp.zeros_like(acc_att)
            acc_conv[...] = jnp.zeros_like(acc_conv)
            acc_out[...] = jnp.zeros_like(acc_out)
            carry[...] = jnp.zeros_like(carry)
            loss_o[...] = jnp.zeros_like(loss_o)
            gfn_o[...] = jnp.zeros_like(gfn_o)
            gcb_o[...] = jnp.zeros_like(gcb_o)
            gcw_o[...] = jnp.zeros_like(gcw_o)

        att_v = att_ref[...].astype(F32)
        za = za_ref[...].astype(F32)
        sa = _sigmoid(za)
        silu_a = za * sa
        a_b = (att_v * silu_a).astype(BF16)

        gb = gb_ref[...].astype(F32)
        gc = gc_ref[...].astype(F32)
        u = u_ref[...].astype(F32)
        zc = zc_ref[...].astype(F32)
        cu = gc * u
        halo = jnp.where(tile > 0, hgc_ref[...].astype(F32) * hu_ref[...].astype(F32), 0.0)
        cu1 = _shift_down(cu, 1, halo)
        cu2 = _shift_down(cu, 2, halo)
        w0, w1, w2 = cw_ref[0:1, :], cw_ref[1:2, :], cw_ref[2:3, :]
        vconv = w0 * cu2 + w1 * cu1 + w2 * cu + cb_ref[...]
        sc = _sigmoid(zc)
        silu_c = zc * sc
        c_b = (gb * vconv * silu_c).astype(BF16)

        y_att = _nn(a_b, watt_ref[...])
        y_conv = _nn(c_b, wconv_ref[...])
        ga = _sigmoid(gatt_ref[...].astype(F32))
        gv = _sigmoid(gconv_ref[...].astype(F32))
        m_b = (ga * y_att + gv * y_conv).astype(BF16)
        x2 = x_ref[...] + _nn(m_b, wout_ref[...])
        r2 = lax.rsqrt(jnp.mean(x2 * x2, axis=-1, keepdims=True) + EPS)
        x2n = x2 * r2
        fg = fg_ref[...]
        err = x2n * fg - t_ref[...]
        loss_o[...] += jnp.sum(err * err, axis=0, keepdims=True) * (0.5 / d)
        dy = err * (1.0 / d)
        gfn_o[...] += jnp.sum(dy * x2n, axis=0, keepdims=True)
        dyn = dy * fg
        dx2 = r2 * (dyn - x2n * jnp.mean(dyn * x2n, axis=-1, keepdims=True))
        dx2_ref[...] = dx2
        dx2_b = dx2.astype(BF16)

        dm = _nt(dx2_b, wout_ref[...])
        acc_out[...] += _tn(m_b, dx2_b)
        dy_att = dm * ga
        dy_conv = dm * gv
        dpb_ref[:, 5 * dc:5 * dc + d] = (dy_att * y_att * (1.0 - ga)).astype(BF16)
        dpb_ref[:, 5 * dc + d:5 * dc + 2 * d] = (dy_conv * y_conv * (1.0 - gv)).astype(BF16)
        dya_b = dy_att.astype(BF16)
        dyc_b = dy_conv.astype(BF16)
        da_in = _nt(dya_b, watt_ref[...])
        acc_att[...] += _tn(a_b, dya_b)
        dc_in = _nt(dyc_b, wconv_ref[...])
        acc_conv[...] += _tn(c_b, dyc_b)

        do_ref[...] = (da_in * silu_a).astype(BF16)
        dpb_ref[:, 0:dc] = (da_in * att_v * (sa * (1.0 + za * (1.0 - sa)))).astype(BF16)
        dpb_ref[:, dc:2 * dc] = (dc_in * vconv * silu_c).astype(BF16)
        dgs = dc_in * gb
        dvc = dgs * silu_c
        dpb_ref[:, 4 * dc:5 * dc] = (dgs * vconv * (sc * (1.0 + zc * (1.0 - sc)))).astype(BF16)
        gcb_o[...] += jnp.sum(dvc, axis=0, keepdims=True)
        gcw_o[0:1, :] += jnp.sum(dvc * cu2, axis=0, keepdims=True)
        gcw_o[1:2, :] += jnp.sum(dvc * cu1, axis=0, keepdims=True)
        gcw_o[2:3, :] += jnp.sum(dvc * cu, axis=0, keepdims=True)
        nxt = carry[...]
        dcu = w2 * dvc + w1 * _shift_up(dvc, 1, nxt) + w0 * _shift_up(dvc, 2, nxt)
        carry[...] = dvc[0:SUBLANES, :]
        dpb_ref[:, 2 * dc:3 * dc] = (dcu * u).astype(BF16)
        dpb_ref[:, 3 * dc:4 * dc] = (dcu * gc).astype(BF16)

        @pl.when(i == n - 1)
        def _():
            for j in range(n_shards):
                gatt_o[j] = acc_att[:, j * (d // n_shards):(j + 1) * (d // n_shards)].astype(BF16)
                gconv_o[j] = acc_conv[:, j * (d // n_shards):(j + 1) * (d // n_shards)].astype(BF16)
                gout_o[j] = acc_out[j * (d // n_shards):(j + 1) * (d // n_shards), :].astype(BF16)

    rev = lambda width, col_blk: pl.BlockSpec((tm, width), lambda i: (n - 1 - i, col_blk))
    halo_spec = lambda col_blk: pl.BlockSpec(
        (SUBLANES, dc), lambda i: (jnp.maximum((n - 1 - i) * (tm // SUBLANES) - 1, 0), col_blk))
    const = lambda shape: pl.BlockSpec(shape, lambda i: tuple(0 for _ in shape), pipeline_mode=pl.Buffered(1))
    q4 = d // n_shards
    return pl.pallas_call(
        body, name="mixer_mid", grid=(n,),
        in_specs=[rev(dc, 0), rev(dc, 3), rev(dc, 4), rev(dc, 5), rev(dc, 6), rev(dc, 7),
                  halo_spec(5), halo_spec(6), rev(d, 4), rev(d, 5), rev(d, 0), rev(d, 0),
                  const((dc, d)), const((dc, d)), const((d, d)), const((SUBLANES, dc)), const((1, dc)), const((1, d))],
        out_specs=[rev(5 * dc + 2 * d, 0), rev(dc, 0), rev(d, 0),
                   const((n_shards, dc, q4)), const((n_shards, dc, q4)), const((n_shards, q4, d)),
                   const((1, d)), const((1, d)), const((1, dc)), const((SUBLANES, dc))],
        out_shape=[jax.ShapeDtypeStruct((s, 5 * dc + 2 * d), BF16), jax.ShapeDtypeStruct((s, dc), BF16),
                   jax.ShapeDtypeStruct((s, d), F32),
                   jax.ShapeDtypeStruct((n_shards, dc, q4), BF16), jax.ShapeDtypeStruct((n_shards, dc, q4), BF16),
                   jax.ShapeDtypeStruct((n_shards, q4, d), BF16),
                   jax.ShapeDtypeStruct((1, d), F32), jax.ShapeDtypeStruct((1, d), F32),
                   jax.ShapeDtypeStruct((1, dc), F32), jax.ShapeDtypeStruct((SUBLANES, dc), F32)],
        scratch_shapes=[pltpu.VMEM((dc, d), F32), pltpu.VMEM((dc, d), F32), pltpu.VMEM((d, d), F32),
                        pltpu.VMEM((SUBLANES, dc), F32)],
        compiler_params=_params("arbitrary"),
    )(att, proj, proj, proj, proj, proj, proj, proj, proj, proj, x, tgt,
      w_att, w_conv, w_out, conv_w8, conv_b, fin_g)


def _in_proj_bwd_x(dqkv, dpb, w_in, x, dx2, g):
    s, d = x.shape
    tn = dqkv.shape[1]
    nb = dpb.shape[1] // tn
    n = s // TM_MM

    def body(*refs):
        dps, ws = refs[:nb + 1], refs[nb + 1:2 * nb + 2]
        x_ref, dx2_ref, g_ref, gx_ref, gng_ref = refs[2 * nb + 2:]
        i = pl.program_id(0)

        @pl.when(i == 0)
        def _():
            gng_ref[...] = jnp.zeros_like(gng_ref)

        dh = _nt(dps[0][...], ws[0][0])
        for j in range(1, nb + 1):
            dh = dh + _nt(dps[j][...], ws[j][0])
        xv = x_ref[...]
        r = lax.rsqrt(jnp.mean(xv * xv, axis=-1, keepdims=True) + EPS)
        xn = xv * r
        gng_ref[...] += jnp.sum(dh * xn, axis=0, keepdims=True)
        dhn = dh * g_ref[...]
        gx_ref[...] = dx2_ref[...] + r * (dhn - xn * jnp.mean(dhn * xn, axis=-1, keepdims=True))

    tile = lambda width, col_blk: pl.BlockSpec((TM_MM, width), lambda i: (i, col_blk))
    wspec = lambda blk: pl.BlockSpec((1, d, tn), lambda i: (blk, 0, 0), pipeline_mode=pl.Buffered(1))
    return pl.pallas_call(
        body, name="in_proj_bwd_x", grid=(n,),
        in_specs=[tile(tn, 0)] + [tile(tn, j) for j in range(nb)] + [wspec(j) for j in range(nb + 1)]
        + [tile(d, 0), tile(d, 0), pl.BlockSpec((1, d), lambda i: (0, 0))],
        out_specs=[tile(d, 0), pl.BlockSpec((1, d), lambda i: (0, 0))],
        out_shape=[jax.ShapeDtypeStruct((s, d), F32), jax.ShapeDtypeStruct((1, d), F32)],
        compiler_params=_params("arbitrary"),
    )(dqkv, *([dpb] * nb), *([w_in] * (nb + 1)), x, dx2, g)


def _in_proj_bwd_w(h, dqkv, dpb, order):
    s, d = h.shape
    tn = dqkv.shape[1]
    n = s // TM_BLK
    hr = d // 2
    settle = min(2, n - 1)

    def body(order_ref, h_ref, da_ref, db_ref, slots_ref, acc, sendbuf, pairbuf, chipbuf, psend, precv, send, recv, lsem):
        j, i = pl.program_id(0), pl.program_id(1)
        blk = order_ref[j]
        pos = _position()
        x, y, c = pos

        @pl.when(i == 0)
        def _():
            acc[...] = jnp.zeros_like(acc)

        @pl.when(blk == 0)
        def _():
            acc[...] += _tn(h_ref[...], da_ref[...])

        @pl.when(blk > 0)
        def _():
            acc[...] += _tn(h_ref[...], db_ref[...])

        def pair(step, half):
            return pltpu.make_async_remote_copy(
                src_ref=sendbuf.at[step, pl.ds(half * hr, hr), :], dst_ref=pairbuf.at[step],
                send_sem=psend.at[step], recv_sem=precv.at[step], device_id=(x, y, 1 - c), device_id_type=MESH)

        def ici(step):
            flip = OWNER_FLIPS[step]
            return pltpu.make_async_remote_copy(
                src_ref=chipbuf.at[step], dst_ref=slots_ref.at[flip], send_sem=send.at[step], recv_sem=recv.at[step],
                device_id=_peer(pos, 4 * (flip >> 1) + 2 * (flip & 1)), device_id_type=MESH)

        local = pltpu.make_async_copy(chipbuf.at[N_CHIPS - 1], slots_ref.at[0], lsem.at[0])

        def combine(step):
            pair(step, c).wait_recv()
            mine = sendbuf[step, pl.ds(c * hr, hr), :].astype(F32)
            chipbuf[step] = (mine + pairbuf[step].astype(F32)).astype(BF16)

        for step in range(N_CHIPS):
            @pl.when((j == step) & (i == n - 1))
            def _(step=step):
                sendbuf[step] = acc[...].astype(BF16)
                pair(step, 1 - c).start()

        for step in range(N_CHIPS - 1):
            @pl.when((j == step + 1) & (i == settle))
            def _(step=step):
                combine(step)
                ici(step).start()

        @pl.when((j == N_CHIPS - 1) & (i == n - 1))
        def _():
            combine(N_CHIPS - 1)
            local.start()
            for step in range(N_CHIPS - 1):
                ici(step).wait_recv()
            for step in range(N_CHIPS - 1):
                ici(step).wait_send()
            for step in range(N_CHIPS):
                pair(step, 1 - c).wait_send()
            local.wait()

    return pl.pallas_call(
        body, name="in_proj_bwd_w",
        grid_spec=pltpu.PrefetchScalarGridSpec(
            num_scalar_prefetch=1, grid=(N_CHIPS, n),
            in_specs=[pl.BlockSpec((TM_BLK, d), lambda j, i, order: (i, 0)),
                      pl.BlockSpec((TM_BLK, tn), lambda j, i, order: (jnp.where(order[j] == 0, i, 0), 0)),
                      pl.BlockSpec((TM_BLK, tn), lambda j, i, order: (jnp.where(order[j] == 0, 0, i),
                                                                     jnp.maximum(order[j] - 1, 0)))],
            out_specs=[ANY],
            scratch_shapes=[pltpu.VMEM((d, tn), F32), pltpu.VMEM((N_CHIPS, d, tn), BF16),
                            pltpu.VMEM((N_CHIPS, hr, tn), BF16), pltpu.VMEM((N_CHIPS, hr, tn), BF16),
                            pltpu.SemaphoreType.DMA((N_CHIPS,)), pltpu.SemaphoreType.DMA((N_CHIPS,)),
                            pltpu.SemaphoreType.DMA((N_CHIPS - 1,)), pltpu.SemaphoreType.DMA((N_CHIPS - 1,)),
                            pltpu.SemaphoreType.DMA((1,))]),
        out_shape=[jax.ShapeDtypeStruct((N_CHIPS, hr, tn), BF16)],
        compiler_params=_params("arbitrary", "arbitrary"),
    )(order, h, dqkv, dpb)[0]


LOSS_ROW = 6


def _adam_update(w, g, m, v):
    c1 = 1.0 / (1.0 - ADAM_B1 ** ADAM_STEP)
    c2 = 1.0 / (1.0 - ADAM_B2 ** ADAM_STEP)
    m2 = ADAM_B1 * m + (1.0 - ADAM_B1) * g
    v2 = ADAM_B2 * v + (1.0 - ADAM_B2) * (g * g)
    return -ADAM_LR * ((m2 * c1) / (jnp.sqrt(v2 * c2) + ADAM_EPS) + ADAM_WD * w), m2, v2


def _adamw_small(recv, params, moments_m, moments_v):
    k = recv.shape[0]
    n_par = len(params)
    cshard = params[3].shape[1]

    def body(*refs):
        r_ref = refs[0]
        ws, ms, vs = refs[1:1 + n_par], refs[1 + n_par:1 + 2 * n_par], refs[1 + 2 * n_par:1 + 3 * n_par]
        loss_ref = refs[1 + 3 * n_par]
        outs = refs[2 + 3 * n_par:]
        total = r_ref[0]
        for slot in range(1, k):
            total = total + r_ref[slot]
        loss_ref[...] = jnp.sum(total[LOSS_ROW:LOSS_ROW + 1, :], axis=1, keepdims=True)
        chip = 2 * lax.axis_index("x") + lax.axis_index("y")
        g_cw = jnp.zeros((3, cshard), F32)
        for sh in range(N_CHIPS):
            g_cw = g_cw + jnp.where(chip == sh, total[3:6, sh * cshard:(sh + 1) * cshard], 0.0)
        grads = [total[0:1, :], total[1:2, :], total[2:3, :ws[2].shape[1]], g_cw, total[8:16, :ws[4].shape[1]]]
        for p in range(n_par):
            delta, m2, v2 = _adam_update(ws[p][...], grads[p], ms[p][...], vs[p][...])
            for q, val in enumerate((grads[p], delta, m2, v2)):
                outs[4 * p + q][...] = val

    shapes = [jax.ShapeDtypeStruct((1, 1), F32)]
    for p in params:
        shapes += [jax.ShapeDtypeStruct(p.shape, F32)] * 4
    return pl.pallas_call(body, name="adamw_small", out_shape=shapes)(recv, *params, *moments_m, *moments_v)


def _adamw(w, g, m, v, name, rows_per_step):
    r, c = w.shape

    def body(w_ref, g_ref, m_ref, v_ref, d_ref, mo_ref, vo_ref):
        d_ref[...], mo_ref[...], vo_ref[...] = _adam_update(w_ref[...], g_ref[...], m_ref[...], v_ref[...])

    spec = pl.BlockSpec((rows_per_step, c), lambda i: (i, 0))
    shape = jax.ShapeDtypeStruct((r, c), F32)
    return pl.pallas_call(
        body, name=name, grid=(r // rows_per_step,),
        in_specs=[spec] * 4, out_specs=[spec] * 3, out_shape=[shape] * 3,
        compiler_params=_params("parallel"),
    )(w, g, m, v)


ANY = pl.BlockSpec(memory_space=pl.ANY)
N_CHIPS = 4
N_DEV = 8
OWNER_FLIPS = (3, 1, 2, 0)


def _position():
    return lax.axis_index("x"), lax.axis_index("y"), lax.axis_index("c")


def _gather_out_shapes(shards, kinds, cw8):
    full = [(a.shape[0], a.shape[1] * N_CHIPS) if k == "cols" else (a.shape[0] * N_CHIPS, a.shape[1])
            for a, k in zip(shards, kinds)]
    return [jax.ShapeDtypeStruct(f, BF16) for f in full] + [
        jax.ShapeDtypeStruct((N_CHIPS,) + cw8.shape, cw8.dtype)]


def _gather_sems(nw):
    return [pltpu.SemaphoreType.DMA((3, nw)), pltpu.SemaphoreType.DMA((3, nw)),
            pltpu.SemaphoreType.DMA((3, nw)), pltpu.SemaphoreType.DMA((3, nw)),
            pltpu.SemaphoreType.DMA((3,)), pltpu.SemaphoreType.DMA((3,)), pltpu.SemaphoreType.DMA((nw + 1,))]


def _gather_plan(kinds, srcs, dsts, cw, cw_all, send1, recv1, send2, recv2, ssend, srecv, lsem):
    nw = len(srcs)
    x, y, c = _position()
    mine = 2 * x + y
    chips = [(x, 1 - y), (1 - x, y), (1 - x, 1 - y)]

    def window(w, shard, half):
        r, cc = srcs[w].shape
        hr = r // 2
        if kinds[w] == "cols":
            rows = pl.ds(0, r) if half is None else pl.ds(half * hr, hr)
            return dsts[w].at[rows, pl.ds(shard * cc, cc)]
        rows = pl.ds(shard * r, r) if half is None else pl.ds(shard * r + half * hr, hr)
        return dsts[w].at[rows, :]

    def my_half(w):
        hr = srcs[w].shape[0] // 2
        return srcs[w].at[pl.ds(c * hr, hr), :]

    def local():
        return [pltpu.make_async_copy(srcs[w], window(w, mine, None), lsem.at[w]) for w in range(nw)] + [
            pltpu.make_async_copy(cw, cw_all.at[mine], lsem.at[nw])]

    def ici(k, w, shard):
        kx, ky = chips[k]
        return pltpu.make_async_remote_copy(
            src_ref=my_half(w), dst_ref=window(w, shard, c), send_sem=send1.at[k, w], recv_sem=recv1.at[k, w],
            device_id=(kx, ky, c), device_id_type=MESH)

    def d2d(k, w, shard, half):
        return pltpu.make_async_remote_copy(
            src_ref=window(w, shard, half), dst_ref=window(w, shard, half),
            send_sem=send2.at[k, w], recv_sem=recv2.at[k, w], device_id=(x, y, 1 - c), device_id_type=MESH)

    def small(k, shard):
        kx, ky = chips[k]
        return pltpu.make_async_remote_copy(
            src_ref=cw, dst_ref=cw_all.at[shard], send_sem=ssend.at[k], recv_sem=srecv.at[k],
            device_id=(kx, ky, c), device_id_type=MESH)

    def theirs(k):
        kx, ky = chips[k]
        return 2 * kx + ky

    def start():
        for cp in local():
            cp.start()
        for k in range(3):
            for w in range(nw):
                ici(k, w, mine).start()
            small(k, mine).start()

    def forward():
        for k in range(3):
            for w in range(nw):
                ici(k, w, theirs(k)).wait_recv()
                d2d(k, w, theirs(k), c).start()

    def finish():
        for k in range(3):
            for w in range(nw):
                d2d(k, w, theirs(k), 1 - c).wait_recv()
            small(k, theirs(k)).wait_recv()
        for k in range(3):
            for w in range(nw):
                ici(k, w, mine).wait_send()
                d2d(k, w, theirs(k), c).wait_send()
            small(k, mine).wait_send()
        for cp in local():
            cp.wait()

    return start, forward, finish


def _scatter_out_shapes(parts):
    return [jax.ShapeDtypeStruct((N_DEV, p.shape[1] // 2, p.shape[2]), p.dtype) for p in parts]


def _scatter_sems(nw):
    return [pltpu.SemaphoreType.DMA((N_DEV, nw)), pltpu.SemaphoreType.DMA((N_DEV, nw)), pltpu.SemaphoreType.DMA((nw,))]


def _peer(pos, k):
    x, y, c = pos
    return ((1 - x) if k & 4 else x, (1 - y) if k & 2 else y, (1 - c) if k & 1 else c)


def _scatter_plan(srcs, dsts, send, recv, lsem):
    nw = len(srcs)
    pos = _position()

    def piece(w, k):
        px, py, pc = _peer(pos, k)
        hr = srcs[w].shape[1] // 2
        return srcs[w].at[2 * px + py, pl.ds(pc * hr, hr), :]

    def remote(w, k):
        return pltpu.make_async_remote_copy(
            src_ref=piece(w, k), dst_ref=dsts[w].at[k], send_sem=send.at[k, w], recv_sem=recv.at[k, w],
            device_id=_peer(pos, k), device_id_type=MESH)

    def local(w):
        return pltpu.make_async_copy(piece(w, 0), dsts[w].at[0], lsem.at[w])

    def start():
        for w in range(nw):
            local(w).start()
        for k in range(1, N_DEV):
            for w in range(nw):
                remote(w, k).start()

    def finish():
        for k in range(1, N_DEV):
            for w in range(nw):
                remote(w, k).wait_recv()
        for k in range(1, N_DEV):
            for w in range(nw):
                remote(w, k).wait_send()
        for w in range(nw):
            local(w).wait()

    return start, finish


def _reduce_pair(slots, small):
    nw = len(slots)

    def body(*refs):
        srcs, sm = refs[:nw], refs[nw]
        dsts, sm_all = refs[nw + 1:2 * nw + 1], refs[2 * nw + 1]
        halves = refs[2 * nw + 2:3 * nw + 2]
        send, recv, ssend, srecv, lsem = refs[3 * nw + 2:]
        pos = _position()
        x, y, c = pos
        me = 4 * x + 2 * y + c

        def rows(w, half):
            hr = halves[w].shape[0]
            return dsts[w].at[pl.ds(half * hr, hr), :]

        def remote(w, half):
            return pltpu.make_async_remote_copy(
                src_ref=halves[w], dst_ref=rows(w, half), send_sem=send.at[w], recv_sem=recv.at[w],
                device_id=(x, y, 1 - c), device_id_type=MESH)

        def bcast(k, slot):
            return pltpu.make_async_remote_copy(
                src_ref=sm, dst_ref=sm_all.at[slot], send_sem=ssend.at[k], recv_sem=srecv.at[k],
                device_id=_peer(pos, k), device_id_type=MESH)

        small_copies = [bcast(k, me) for k in range(1, N_DEV)]
        own_small = pltpu.make_async_copy(sm, sm_all.at[me], lsem.at[nw])
        for cp in small_copies + [own_small]:
            cp.start()
        big = []
        for w in range(nw):
            total = srcs[w][0].astype(F32)
            for k in range(1, srcs[w].shape[0]):
                total = total + srcs[w][k].astype(F32)
            halves[w][...] = total
            big += [remote(w, c), pltpu.make_async_copy(halves[w], rows(w, c), lsem.at[w])]
            big[-2].start()
            big[-1].start()
        for w in range(nw):
            remote(w, 1 - c).wait_recv()
        for k in range(1, N_DEV):
            px, py, pc = _peer(pos, k)
            bcast(k, 4 * px + 2 * py + pc).wait_recv()
        for w in range(nw):
            big[2 * w].wait_send()
            big[2 * w + 1].wait()
        for cp in small_copies:
            cp.wait_send()
        own_small.wait()

    vmem = pl.BlockSpec(memory_space=pltpu.VMEM)
    half_shapes = [(sl.shape[1], sl.shape[2]) for sl in slots]
    return pl.pallas_call(
        body, name="reduce_pair",
        in_specs=[vmem] * (nw + 1), out_specs=[ANY] * (nw + 1),
        out_shape=[jax.ShapeDtypeStruct((2 * r, cc), F32) for r, cc in half_shapes]
        + [jax.ShapeDtypeStruct((N_DEV,) + small.shape, small.dtype)],
        scratch_shapes=[pltpu.VMEM(hs, F32) for hs in half_shapes]
        + [pltpu.SemaphoreType.DMA((nw,)), pltpu.SemaphoreType.DMA((nw,)),
           pltpu.SemaphoreType.DMA((N_DEV,)), pltpu.SemaphoreType.DMA((N_DEV,)),
           pltpu.SemaphoreType.DMA((nw + 1,))],
        compiler_params=pltpu.CompilerParams(vmem_limit_bytes=VMEM_LIMIT),
    )(*slots, small)


def _pad_to(a, rows, cols):
    return jnp.pad(a, ((0, rows - a.shape[0]), (0, cols - a.shape[1])))


def _pack_small(norm_g, fin_g, conv_b, conv_w, loss_vec, rel):
    rows = [_pad_to(norm_g, 1, SMALL_COLS), _pad_to(fin_g, 1, SMALL_COLS), _pad_to(conv_b, 1, SMALL_COLS),
            _pad_to(conv_w, 3, SMALL_COLS), _pad_to(loss_vec, 2, SMALL_COLS), _pad_to(rel, HEADS, SMALL_COLS)]
    return jnp.concatenate(rows, axis=0)


def kernel(x, norm_g, w_in, rel_bias, w_att_out, conv_w, conv_b, w_conv_out, w_out, final_norm_g, loss_target, m_norm_g, m_w_in, m_rel_bias, m_w_att_out, m_conv_w, m_conv_b, m_w_conv_out, m_w_out, m_final_norm_g, v_norm_g, v_w_in, v_rel_bias, v_w_att_out, v_conv_w, v_conv_b, v_w_conv_out, v_w_out, v_final_norm_g):
    xs, tgt = x[0], loss_target[0]
    cshard = conv_w.shape[2]
    chip = 2 * lax.axis_index("x") + lax.axis_index("y")

    shards = [w_in[0], w_att_out[0], w_conv_out[0], w_out[0]]
    cw8 = _pad_to(conv_w[0], SUBLANES, cshard)
    flips = jnp.arange(N_CHIPS, dtype=jnp.int32)
    own_first = jnp.bitwise_xor(chip, flips)
    own_last = jnp.bitwise_xor(chip, jnp.asarray(OWNER_FLIPS, jnp.int32))

    proj, h, wb_in = _in_proj_gather(xs, norm_g, shards[0], own_first)
    diag = jnp.take(rel_bias[0], _diag_rel_index(), axis=1)
    att, lse, wb_att, wb_conv, wb_out, cw_all = _attn_fwd(diag, proj, shards[1:], ["cols", "cols", "rows"], cw8)
    conv_w_full = jnp.transpose(cw_all, (1, 0, 2)).reshape(SUBLANES, N_CHIPS * cshard)
    (dpb, d_att, dx2, g_att_p, g_conv_p, g_out_p, loss_vec, g_fin, g_cb, g_cw) = _mixer_mid(
        att, proj, xs, tgt, wb_att, wb_conv, wb_out, conv_w_full, conv_b, final_norm_g[None, :])
    dqkv, g_rel, r_att, r_conv, r_out = _attn_bwd(diag, proj, d_att, att, lse, [g_att_p, g_conv_p, g_out_p])
    grad_x, g_norm = _in_proj_bwd_x(dqkv, dpb, wb_in, xs, dx2, norm_g)
    r_in = _in_proj_bwd_w(h, dqkv, dpb, own_last)

    small = _pack_small(g_norm, g_fin, g_cb, g_cw[0:3], loss_vec, g_rel)
    gw_in, gw_att, gw_conv, gw_out, r_small = _reduce_pair([r_in, r_att, r_conv, r_out], small)
    small_out = _adamw_small(
        r_small,
        [norm_g, final_norm_g[None, :], conv_b, conv_w[0], rel_bias[0]],
        [m_norm_g, m_final_norm_g[None, :], m_conv_b, m_conv_w[0], m_rel_bias[0]],
        [v_norm_g, v_final_norm_g[None, :], v_conv_b, v_conv_w[0], v_rel_bias[0]])
    loss = small_out[0][0, 0]
    small_names = ["norm_g", "final_norm_g", "conv_b", "conv_w", "rel_bias"]
    fix = {"norm_g": lambda a: a, "final_norm_g": lambda a: a[0], "conv_b": lambda a: a,
           "conv_w": lambda a: a[None], "rel_bias": lambda a: a[None]}
    small_res = {name: [fix[name](small_out[1 + 4 * p + q]) for q in range(4)] for p, name in enumerate(small_names)}

    big = {}
    for name, w, g, m, v, rows in (("w_in", w_in, gw_in, m_w_in, v_w_in, 128),
                                   ("w_att_out", w_att_out, gw_att, m_w_att_out, v_w_att_out, 256),
                                   ("w_conv_out", w_conv_out, gw_conv, m_w_conv_out, v_w_conv_out, 256),
                                   ("w_out", w_out, gw_out, m_w_out, v_w_out, 128)):
        dw, mw, vw = _adamw(w[0], g, m[0], v[0], "adamw_" + name, rows)
        big[name] = (g[None], dw[None], mw[None], vw[None])

    order = ["norm_g", "w_in", "rel_bias", "w_att_out", "conv_w", "conv_b", "w_conv_out", "w_out", "final_norm_g"]
    outs = [loss, grad_x[None]]
    for which in range(4):
        for name in order:
            outs.append(big[name][which] if name in big else small_res[name][which])
    return tuple(outs)
```

```python
import numpy as np
import jax
import jax.numpy as jnp
from jax import lax
from jax.experimental import pallas as pl
from jax.experimental.pallas import tpu as pltpu

F32 = jnp.float32
BF16 = jnp.bfloat16
MESH = pl.DeviceIdType.MESH

CHUNK = 64
N_LEFT = 8
HEADS = 8
HEAD_DIM = 64
D_ATT = HEADS * HEAD_DIM
MAX_REL = 128
N_REL = 2 * MAX_REL + 1
EPS = 1e-6
NEG_BIG = -1e30
ADAM_LR, ADAM_B1, ADAM_B2, ADAM_EPS, ADAM_WD, ADAM_STEP = 0.001, 0.9, 0.999, 1e-08, 0.01, 10

LANES = 128
SUBLANES = 8
VMEM_LIMIT = 56 * 1024 * 1024

QB = 2 * CHUNK
KW = N_LEFT * CHUNK + QB
DIAG = KW + QB
TQ = N_LEFT * CHUNK
TM_MID = 256
TM_MM = 512
TM_BLK = 1024
SMALL_ROWS, SMALL_COLS = 16, 1024


def _params(*sem):
    return pltpu.CompilerParams(dimension_semantics=sem, vmem_limit_bytes=VMEM_LIMIT)


def _nt(a, b):
    return lax.dot_general(a, b, (((1,), (1,)), ((), ())), preferred_element_type=F32)


def _tn(a, b):
    return lax.dot_general(a, b, (((0,), (0,)), ((), ())), preferred_element_type=F32)


def _nn(a, b):
    return jnp.dot(a, b, preferred_element_type=F32)


def _diag_rel_index():
    d = np.arange(DIAG)
    diff = np.where(d < KW, d, d - DIAG)
    rel = N_LEFT * CHUNK - diff
    return np.clip(rel, -MAX_REL, MAX_REL) + MAX_REL


def _build_bias(diag_ref, bias_scr):
    r = lax.broadcasted_iota(jnp.int32, (QB, KW), 0) // CHUNK
    s = lax.broadcasted_iota(jnp.int32, (QB, KW), 1) // CHUNK
    allowed = (s >= r) & (s <= r + N_LEFT)
    for h in range(HEADS):
        row = jnp.broadcast_to(diag_ref[h:h + 1, :], (QB, DIAG))
        t = pltpu.roll(row, 0, 1, stride=1, stride_axis=0)
        bias_scr[h // 2, (h % 2) * QB:(h % 2 + 1) * QB, :] = jnp.where(allowed, t[:, :KW], NEG_BIG)


def _stack_heads(a, lane_hi):
    zero = jnp.zeros_like(a)
    return jnp.concatenate([jnp.where(lane_hi, zero, a), jnp.where(lane_hi, a, zero)], axis=0)


def _in_proj_gather(x, g, shard, order):
    s, d = x.shape
    tn = shard.shape[1]
    n = s // TM_BLK
    hr = d // 2

    def body(order_ref, x_ref, g_ref, shard32_ref, proj_ref, h_ref, wfull_ref, shard_ref, hbuf, wbuf,
             send1, recv1, send2, recv2, lsem):
        del order_ref
        j, i = pl.program_id(0), pl.program_id(1)
        x, y, c = _position()
        mine = 2 * x + y
        chips = [(x, 1 - y), (1 - x, y), (1 - x, 1 - y)]

        def theirs(k):
            return 2 * chips[k][0] + chips[k][1]

        def half_rows(half):
            return pl.ds(half * hr, hr)

        def ici(k, shard_index):
            return pltpu.make_async_remote_copy(
                src_ref=shard_ref.at[half_rows(c), :], dst_ref=wfull_ref.at[shard_index, half_rows(c), :],
                send_sem=send1.at[k], recv_sem=recv1.at[k], device_id=(*chips[k], c), device_id_type=MESH)

        def d2d(k, half):
            return pltpu.make_async_remote_copy(
                src_ref=wbuf.at[k % 2, half_rows(half), :], dst_ref=wfull_ref.at[theirs(k), half_rows(half), :],
                send_sem=send2.at[k], recv_sem=recv2.at[k], device_id=(x, y, 1 - c), device_id_type=MESH)

        def load(k, half, sem):
            return pltpu.make_async_copy(wfull_ref.at[theirs(k), half_rows(half), :],
                                         wbuf.at[k % 2, half_rows(half), :], lsem.at[sem])

        own = pltpu.make_async_copy(shard_ref, wfull_ref.at[mine], lsem.at[0])

        def put_proj(block):
            for grp in range(tn // D_ATT):
                proj_ref[grp] = block[:, grp * D_ATT:(grp + 1) * D_ATT].astype(BF16)

        @pl.when((j == 0) & (i == 0))
        def _():
            shard_ref[...] = shard32_ref[...].astype(BF16)
            own.start()
            ici(0, mine).start()
            ici(1, mine).start()

        for k in range(3):
            first = max(n - 3, 0) if k == 0 else min(n // 2, n - 1)

            @pl.when((j == k) & (i == first))
            def _(k=k):
                if k == 0:
                    ici(0, mine).wait_send()
                    ici(1, mine).wait_send()
                    ici(2, mine).start()
                if k == 2:
                    d2d(0, c).wait_send()
                ici(k, theirs(k)).wait_recv()
                load(k, c, 1).start()

            @pl.when((j == k) & (i == min(first + 1, n - 1)))
            def _(k=k):
                load(k, c, 1).wait()
                d2d(k, c).start()

            @pl.when((j == k) & (i == min(first + 2, n - 1)))
            def _(k=k):
                d2d(k, 1 - c).wait_recv()
                load(k, 1 - c, 2).start()

            @pl.when((j == k + 1) & (i == 0))
            def _(k=k):
                load(k, 1 - c, 2).wait()

        @pl.when(j == 0)
        def _():
            xv = x_ref[...]
            r = lax.rsqrt(jnp.mean(xv * xv, axis=-1, keepdims=True) + EPS)
            hv = ((xv * r) * g_ref[...]).astype(BF16)
            hbuf[i] = hv
            h_ref[...] = hv
            put_proj(_nn(hv, shard_ref[...]))

        for k in range(3):
            @pl.when(j == k + 1)
            def _(k=k):
                put_proj(_nn(hbuf[i], wbuf[k % 2]))

        @pl.when((j == 3) & (i == n - 1))
        def _():
            ici(2, mine).wait_send()
            d2d(1, c).wait_send()
            d2d(2, c).wait_send()
            own.wait()

    return pl.pallas_call(
        body, name="in_proj_gather",
        grid_spec=pltpu.PrefetchScalarGridSpec(
            num_scalar_prefetch=1, grid=(N_CHIPS, n),
            in_specs=[pl.BlockSpec((TM_BLK, d), lambda j, i, order: (jnp.where(j == 0, i, n - 1), 0)),
                      pl.BlockSpec((1, d), lambda j, i, order: (0, 0)), pl.BlockSpec(memory_space=pltpu.VMEM)],
            out_specs=[pl.BlockSpec((tn // D_ATT, TM_BLK, D_ATT), lambda j, i, order: (order[j], i, 0)),
                       pl.BlockSpec((TM_BLK, d), lambda j, i, order: (jnp.where(j == 0, i, n - 1), 0)), ANY],
            scratch_shapes=[pltpu.VMEM((d, tn), BF16), pltpu.VMEM((n, TM_BLK, d), BF16), pltpu.VMEM((2, d, tn), BF16),
                            pltpu.SemaphoreType.DMA((3,)), pltpu.SemaphoreType.DMA((3,)),
                            pltpu.SemaphoreType.DMA((3,)), pltpu.SemaphoreType.DMA((3,)), pltpu.SemaphoreType.DMA((3,))]),
        out_shape=[jax.ShapeDtypeStruct((N_CHIPS * tn // D_ATT, s, D_ATT), BF16), jax.ShapeDtypeStruct((s, d), BF16),
                   jax.ShapeDtypeStruct((N_CHIPS, d, tn), BF16)],
        compiler_params=_params("arbitrary", "arbitrary"),
    )(order, x, g, shard)


def _attn_fwd(diag, proj, shards, kinds, cw8):
    s = proj.shape[1]
    n = s // TQ
    nw = len(shards)
    scale = HEAD_DIM ** -0.5

    def body(*refs):
        diag_ref, q_ref, kp_ref, kc_ref, vp_ref, vc_ref = refs[:6]
        srcs, cw = refs[6:6 + nw], refs[6 + nw]
        o_ref, lse_ref = refs[7 + nw:9 + nw]
        dsts, cw_all = refs[9 + nw:9 + 2 * nw], refs[9 + 2 * nw]
        bias_scr, kcat, vcat = refs[10 + 2 * nw:13 + 2 * nw]
        casts = refs[13 + 2 * nw:13 + 3 * nw]
        start, forward, finish = _gather_plan(kinds, casts, dsts, cw, cw_all, *refs[13 + 3 * nw:])
        i = pl.program_id(0)

        @pl.when(i == 0)
        def _():
            for w in range(nw):
                casts[w][...] = srcs[w][...].astype(BF16)
            start()
            _build_bias(diag_ref, bias_scr)

        @pl.when(i == n // 2)
        def _():
            forward()

        @pl.when(i == n - 1)
        def _():
            finish()

        kcat[0:TQ, :] = kp_ref[...]
        kcat[TQ:2 * TQ, :] = kc_ref[...]
        vcat[0:TQ, :] = vp_ref[...]
        vcat[TQ:2 * TQ, :] = vc_ref[...]
        lane_hi = lax.broadcasted_iota(jnp.int32, (QB, LANES), 1) >= HEAD_DIM
        col = lax.broadcasted_iota(jnp.int32, (2 * QB, KW), 1)

        def make_block(first_tile):
            def block(b, carry):
                r0 = pl.multiple_of(b * QB, QB)
                for p in range(HEADS // 2):
                    lanes = slice(LANES * p, LANES * (p + 1))
                    q2 = _stack_heads(q_ref[pl.ds(r0, QB), lanes] * scale, lane_hi)
                    kw = kcat[pl.ds(r0, KW), lanes]
                    vw = vcat[pl.ds(r0, KW), lanes]
                    sc = _nt(q2, kw) + bias_scr[p]
                    if first_tile:
                        sc = jnp.where(col >= TQ - r0, sc, NEG_BIG)
                    m = jnp.max(sc, axis=1, keepdims=True)
                    pe = jnp.exp(sc - m)
                    l = jnp.sum(pe, axis=1, keepdims=True)
                    o2 = _nn(pe.astype(BF16), vw) / l
                    lse2 = m + jnp.log(l)
                    lse_ref[pl.ds(r0, QB), 2 * p:2 * p + 1] = lse2[0:QB, :]
                    lse_ref[pl.ds(r0, QB), 2 * p + 1:2 * p + 2] = lse2[QB:2 * QB, :]
                    o_ref[pl.ds(r0, QB), lanes] = jnp.where(lane_hi, o2[QB:2 * QB, :], o2[0:QB, :]).astype(BF16)
                return carry
            return block

        @pl.when(i == 0)
        def _():
            lax.fori_loop(0, TQ // QB, make_block(True), 0)

        @pl.when(i > 0)
        def _():
            lax.fori_loop(0, TQ // QB, make_block(False), 0)

    blk = lambda grp, prev: pl.BlockSpec(
        (None, TQ, D_ATT), (lambda i: (grp, jnp.maximum(i - 1, 0), 0)) if prev else (lambda i: (grp, i, 0)))
    vmem = pl.BlockSpec(memory_space=pltpu.VMEM)
    return pl.pallas_call(
        body, name="attn_fwd", grid=(n,),
        in_specs=[pl.BlockSpec((HEADS, DIAG), lambda i: (0, 0)),
                  blk(0, False), blk(1, True), blk(1, False), blk(2, True), blk(2, False)] + [vmem] * (nw + 1),
        out_specs=[pl.BlockSpec((TQ, D_ATT), lambda i: (i, 0)), pl.BlockSpec((TQ, HEADS), lambda i: (i, 0))]
        + [ANY] * (nw + 1),
        out_shape=[jax.ShapeDtypeStruct((s, D_ATT), BF16), jax.ShapeDtypeStruct((s, HEADS), F32)]
        + _gather_out_shapes(shards, kinds, cw8),
        scratch_shapes=[pltpu.VMEM((HEADS // 2, 2 * QB, KW), F32), pltpu.VMEM((2 * TQ, D_ATT), BF16),
                        pltpu.VMEM((2 * TQ, D_ATT), BF16)] + [pltpu.VMEM(a.shape, BF16) for a in shards]
        + _gather_sems(nw),
        compiler_params=_params("arbitrary"),
    )(diag, proj, proj, proj, proj, proj, *shards, cw8)


def _attn_bwd(diag, proj, d_att, att, lse, parts):
    s = proj.shape[1]
    n = s // TQ
    npart = len(parts)
    scale = HEAD_DIM ** -0.5
    rel_pad = 3 * LANES

    def body(*refs):
        diag_ref, q_ref, kp_ref, kc_ref, vp_ref, vc_ref, do_ref, o_ref, lse_ref = refs[:9]
        part_refs = refs[9:9 + npart]
        dqkv_ref, dbias_ref = refs[9 + npart:11 + npart]
        slot_refs = refs[11 + npart:11 + 2 * npart]
        bias_scr, dbias_acc, kcat, vcat, dk_acc, dv_acc, dq_scr = refs[11 + 2 * npart:18 + 2 * npart]
        start, finish = _scatter_plan(part_refs, slot_refs, *refs[18 + 2 * npart:])
        i = pl.program_id(0)

        @pl.when(i == 0)
        def _():
            start()
            _build_bias(diag_ref, bias_scr)
            dbias_acc[...] = jnp.zeros_like(dbias_acc)
            dk_acc[...] = jnp.zeros_like(dk_acc)
            dv_acc[...] = jnp.zeros_like(dv_acc)

        @pl.when(i > 0)
        def _():
            dqkv_ref[:, 0:D_ATT] = dq_scr[...]
            dk_acc[0:TQ, :] = dk_acc[TQ:2 * TQ, :]
            dk_acc[TQ:2 * TQ, :] = jnp.zeros((TQ, D_ATT), F32)
            dv_acc[0:TQ, :] = dv_acc[TQ:2 * TQ, :]
            dv_acc[TQ:2 * TQ, :] = jnp.zeros((TQ, D_ATT), F32)

        @pl.when(i < n)
        def _():
            kcat[0:TQ, :] = kp_ref[...]
            kcat[TQ:2 * TQ, :] = kc_ref[...]
            vcat[0:TQ, :] = vp_ref[...]
            vcat[TQ:2 * TQ, :] = vc_ref[...]

        lane_hi = lax.broadcasted_iota(jnp.int32, (QB, LANES), 1) >= HEAD_DIM
        col = lax.broadcasted_iota(jnp.int32, (2 * QB, KW), 1)

        def make_block(first_tile):
            def block(b, carry):
                r0 = pl.multiple_of(b * QB, QB)
                for p in range(HEADS // 2):
                    lanes = slice(LANES * p, LANES * (p + 1))
                    q2 = _stack_heads(q_ref[pl.ds(r0, QB), lanes] * scale, lane_hi)
                    kw = kcat[pl.ds(r0, KW), lanes]
                    vw = vcat[pl.ds(r0, KW), lanes]
                    dop = do_ref[pl.ds(r0, QB), lanes]
                    do2 = _stack_heads(dop, lane_hi)
                    prod = dop.astype(F32) * o_ref[pl.ds(r0, QB), lanes].astype(F32)
                    delta2 = jnp.concatenate(
                        [jnp.sum(jnp.where(lane_hi, 0.0, prod), axis=1, keepdims=True),
                         jnp.sum(jnp.where(lane_hi, prod, 0.0), axis=1, keepdims=True)], axis=0)
                    lse2 = jnp.concatenate([lse_ref[pl.ds(r0, QB), 2 * p:2 * p + 1],
                                            lse_ref[pl.ds(r0, QB), 2 * p + 1:2 * p + 2]], axis=0)
                    sc = _nt(q2, kw) + bias_scr[p]
                    if first_tile:
                        sc = jnp.where(col >= TQ - r0, sc, NEG_BIG)
                    pr = jnp.exp(sc - lse2)
                    ds = pr * (_nt(do2, vw) - delta2)
                    dbias_acc[p] += ds
                    dsb = ds.astype(BF16)
                    dv_acc[pl.ds(r0, KW), lanes] += _tn(pr.astype(BF16), do2)
                    dk_acc[pl.ds(r0, KW), lanes] += _tn(dsb, q2)
                    dq2 = _nn(dsb, kw)
                    dq = jnp.where(lane_hi, dq2[QB:2 * QB, :], dq2[0:QB, :]) * scale
                    dq_scr[pl.ds(r0, QB), lanes] = dq.astype(BF16)
                return carry
            return block

        @pl.when(i == 0)
        def _():
            lax.fori_loop(0, TQ // QB, make_block(True), 0)

        @pl.when((i > 0) & (i < n))
        def _():
            lax.fori_loop(0, TQ // QB, make_block(False), 0)

        @pl.when(i > 0)
        def _():
            dqkv_ref[:, D_ATT:2 * D_ATT] = dk_acc[0:TQ, :].astype(BF16)
            dqkv_ref[:, 2 * D_ATT:3 * D_ATT] = dv_acc[0:TQ, :].astype(BF16)

        @pl.when(i == n)
        def _():
            d_iota = lax.broadcasted_iota(jnp.int32, (DIAG, rel_pad), 0)
            n_iota = lax.broadcasted_iota(jnp.int32, (DIAG, rel_pad), 1)
            diff = jnp.where(d_iota < KW, d_iota, d_iota - DIAG)
            idx = jnp.clip(N_LEFT * CHUNK - diff, -MAX_REL, MAX_REL) + MAX_REL
            onehot = (idx == n_iota).astype(F32)
            rows = []
            for hd in range(HEADS):
                acc = dbias_acc[hd // 2, (hd % 2) * QB:(hd % 2 + 1) * QB, :]
                a = jnp.concatenate([acc, jnp.zeros((QB, DIAG - KW), F32)], axis=1)
                g8 = a[0:SUBLANES, :]
                for blk in range(1, QB // SUBLANES):
                    g8 = g8 + pltpu.roll(a[blk * SUBLANES:(blk + 1) * SUBLANES, :], DIAG - blk * SUBLANES, 1)
                g1 = g8[0:1, :]
                for r in range(1, SUBLANES):
                    g1 = g1 + pltpu.roll(g8[r:r + 1, :], DIAG - r, 1)
                rows.append(g1)
            g = jnp.concatenate(rows, axis=0)
            dbias_ref[...] = jnp.dot(g, onehot, preferred_element_type=F32, precision=lax.Precision.HIGHEST)
            finish()

    last = n - 1
    cur = lambda grp: pl.BlockSpec((None, TQ, D_ATT), lambda i: (grp, jnp.minimum(i, last), 0))
    prev = lambda grp: pl.BlockSpec((None, TQ, D_ATT), lambda i: (grp, jnp.maximum(jnp.minimum(i, last) - 1, 0), 0))
    tile = pl.BlockSpec((TQ, D_ATT), lambda i: (jnp.minimum(i, last), 0))
    return pl.pallas_call(
        body, name="attn_bwd", grid=(n + 1,),
        in_specs=[pl.BlockSpec((HEADS, DIAG), lambda i: (0, 0)),
                  cur(0), prev(1), cur(1), prev(2), cur(2), tile, tile,
                  pl.BlockSpec((TQ, HEADS), lambda i: (jnp.minimum(i, last), 0))] + [ANY] * npart,
        out_specs=[pl.BlockSpec((TQ, 3 * D_ATT), lambda i: (jnp.maximum(i - 1, 0), 0)),
                   pl.BlockSpec((HEADS, rel_pad), lambda i: (0, 0))] + [ANY] * npart,
        out_shape=[jax.ShapeDtypeStruct((s, 3 * D_ATT), BF16), jax.ShapeDtypeStruct((HEADS, rel_pad), F32)]
        + _scatter_out_shapes(parts),
        scratch_shapes=[pltpu.VMEM((HEADS // 2, 2 * QB, KW), F32), pltpu.VMEM((HEADS // 2, 2 * QB, KW), F32),
                        pltpu.VMEM((2 * TQ, D_ATT), BF16), pltpu.VMEM((2 * TQ, D_ATT), BF16),
                        pltpu.VMEM((2 * TQ, D_ATT), F32), pltpu.VMEM((2 * TQ, D_ATT), F32),
                        pltpu.VMEM((TQ, D_ATT), BF16)] + _scatter_sems(npart),
        compiler_params=_params("arbitrary"),
    )(diag, proj, proj, proj, proj, proj, d_att, att, lse, *parts)


def _shift_down(a, k, halo):
    rolled = pltpu.roll(a, k, 0)
    row = lax.broadcasted_iota(jnp.int32, halo.shape, 0)
    first = jnp.where(row < k, pltpu.roll(halo, k, 0), rolled[0:SUBLANES, :])
    return jnp.concatenate([first, rolled[SUBLANES:, :]], axis=0)


def _shift_up(a, k, nxt):
    tm = a.shape[0]
    rolled = pltpu.roll(a, tm - k, 0)
    row = lax.broadcasted_iota(jnp.int32, nxt.shape, 0)
    last = jnp.where(row >= SUBLANES - k, pltpu.roll(nxt, SUBLANES - k, 0), rolled[tm - SUBLANES:, :])
    return jnp.concatenate([rolled[:tm - SUBLANES, :], last], axis=0)


def _sigmoid(v):
    return 0.5 * jnp.tanh(0.5 * v) + 0.5


def _mixer_mid(att, proj, x, tgt, w_att, w_conv, w_out, conv_w8, conv_b, fin_g):
    s, d = x.shape
    dc = D_ATT
    n = s // TM_MID
    tm = TM_MID
    n_shards = 4

    def body(att_ref, za_ref, gb_ref, gc_ref, u_ref, zc_ref, hgc_ref, hu_ref, gatt_ref, gconv_ref, x_ref, t_ref,
             watt_ref, wconv_ref, wout_ref, cw_ref, cb_ref, fg_ref,
             dpb_ref, do_ref, dx2_ref, gatt_o, gconv_o, gout_o, loss_o, gfn_o, gcb_o, gcw_o,
             acc_att, acc_conv, acc_out, carry):
        i = pl.program_id(0)
        tile = n - 1 - i

        @pl.when(i == 0)
        def _():
            acc_att[...] = jnp.zeros_like(acc_att)
            acc_conv[...] = jnp.zeros_like(acc_conv)
            acc_out[...] = jnp.zeros_like(acc_out)
            carry[...] = jnp.zeros_like(carry)
            loss_o[...] = jnp.zeros_like(loss_o)
            gfn_o[...] = jnp.zeros_like(gfn_o)
            gcb_o[...] = jnp.zeros_like(gcb_o)
            gcw_o[...] = jnp.zeros_like(gcw_o)

        att_v = att_ref[...].astype(F32)
        za = za_ref[...].astype(F32)
        sa = _sigmoid(za)
        silu_a = za * sa
        a_b = (att_v * silu_a).astype(BF16)

        gb = gb_ref[...].astype(F32)
        gc = gc_ref[...].astype(F32)
        u = u_ref[...].astype(F32)
        zc = zc_ref[...].astype(F32)
        cu = gc * u
        halo = jnp.where(tile > 0, hgc_ref[...].astype(F32) * hu_ref[...].astype(F32), 0.0)
        cu1 = _shift_down(cu, 1, halo)
        cu2 = _shift_down(cu, 2, halo)
        w0, w1, w2 = cw_ref[0:1, :], cw_ref[1:2, :], cw_ref[2:3, :]
        vconv = w0 * cu2 + w1 * cu1 + w2 * cu + cb_ref[...]
        sc = _sigmoid(zc)
        silu_c = zc * sc
        c_b = (gb * vconv * silu_c).astype(BF16)

        y_att = _nn(a_b, watt_ref[...])
        y_conv = _nn(c_b, wconv_ref[...])
        ga = _sigmoid(jnp.concatenate([gatt_ref[0], gatt_ref[1]], axis=1).astype(F32))
        gv = _sigmoid(jnp.concatenate([gconv_ref[0], gconv_ref[1]], axis=1).astype(F32))
        m_b = (ga * y_att + gv * y_conv).astype(BF16)
        x2 = x_ref[...] + _nn(m_b, wout_ref[...])
        r2 = lax.rsqrt(jnp.mean(x2 * x2, axis=-1, keepdims=True) + EPS)
        x2n = x2 * r2
        fg = fg_ref[...]
        err = x2n * fg - t_ref[...]
        loss_o[...] += jnp.sum(err * err, axis=0, keepdims=True) * (0.5 / d)
        dy = err * (1.0 / d)
        gfn_o[...] += jnp.sum(dy * x2n, axis=0, keepdims=True)
        dyn = dy * fg
        dx2 = r2 * (dyn - x2n * jnp.mean(dyn * x2n, axis=-1, keepdims=True))
        dx2_ref[...] = dx2
        dx2_b = dx2.astype(BF16)

        dm = _nt(dx2_b, wout_ref[...])
        acc_out[...] += _tn(m_b, dx2_b)
        dy_att = dm * ga
        dy_conv = dm * gv
        dpb_ref[:, 5 * dc:5 * dc + d] = (dy_att * y_att * (1.0 - ga)).astype(BF16)
        dpb_ref[:, 5 * dc + d:5 * dc + 2 * d] = (dy_conv * y_conv * (1.0 - gv)).astype(BF16)
        dya_b = dy_att.astype(BF16)
        dyc_b = dy_conv.astype(BF16)
        da_in = _nt(dya_b, watt_ref[...])
        acc_att[...] += _tn(a_b, dya_b)
        dc_in = _nt(dyc_b, wconv_ref[...])
        acc_conv[...] += _tn(c_b, dyc_b)

        do_ref[...] = (da_in * silu_a).astype(BF16)
        dpb_ref[:, 0:dc] = (da_in * att_v * (sa * (1.0 + za * (1.0 - sa)))).astype(BF16)
        dpb_ref[:, dc:2 * dc] = (dc_in * vconv * silu_c).astype(BF16)
        dgs = dc_in * gb
        dvc = dgs * silu_c
        dpb_ref[:, 4 * dc:5 * dc] = (dgs * vconv * (sc * (1.0 + zc * (1.0 - sc)))).astype(BF16)
        gcb_o[...] += jnp.sum(dvc, axis=0, keepdims=True)
        gcw_o[0:1, :] += jnp.sum(dvc * cu2, axis=0, keepdims=True)
        gcw_o[1:2, :] += jnp.sum(dvc * cu1, axis=0, keepdims=True)
        gcw_o[2:3, :] += jnp.sum(dvc * cu, axis=0, keepdims=True)
        nxt = carry[...]
        dcu = w2 * dvc + w1 * _shift_up(dvc, 1, nxt) + w0 * _shift_up(dvc, 2, nxt)
        carry[...] = dvc[0:SUBLANES, :]
        dpb_ref[:, 2 * dc:3 * dc] = (dcu * u).astype(BF16)
        dpb_ref[:, 3 * dc:4 * dc] = (dcu * gc).astype(BF16)

        @pl.when(i == n - 1)
        def _():
            for j in range(n_shards):
                gatt_o[j] = acc_att[:, j * (d // n_shards):(j + 1) * (d // n_shards)].astype(BF16)
                gconv_o[j] = acc_conv[:, j * (d // n_shards):(j + 1) * (d // n_shards)].astype(BF16)
                gout_o[j] = acc_out[j * (d // n_shards):(j + 1) * (d // n_shards), :].astype(BF16)

    rev = lambda width, col_blk: pl.BlockSpec((tm, width), lambda i: (n - 1 - i, col_blk))
    grp = lambda g: pl.BlockSpec((None, tm, dc), lambda i: (g, n - 1 - i, 0))
    grp2 = lambda g2: pl.BlockSpec((2, tm, dc), lambda i: (g2, n - 1 - i, 0))
    halo_spec = lambda g: pl.BlockSpec(
        (None, SUBLANES, dc), lambda i: (g, jnp.maximum((n - 1 - i) * (tm // SUBLANES) - 1, 0), 0))
    const = lambda shape: pl.BlockSpec(shape, lambda i: tuple(0 for _ in shape), pipeline_mode=pl.Buffered(1))
    q4 = d // n_shards
    return pl.pallas_call(
        body, name="mixer_mid", grid=(n,),
        in_specs=[rev(dc, 0), grp(3), grp(4), grp(5), grp(6), grp(7),
                  halo_spec(5), halo_spec(6), grp2(4), grp2(5), rev(d, 0), rev(d, 0),
                  const((dc, d)), const((dc, d)), const((d, d)), const((SUBLANES, dc)), const((1, dc)), const((1, d))],
        out_specs=[rev(5 * dc + 2 * d, 0), rev(dc, 0), rev(d, 0),
                   const((n_shards, dc, q4)), const((n_shards, dc, q4)), const((n_shards, q4, d)),
                   const((1, d)), const((1, d)), const((1, dc)), const((SUBLANES, dc))],
        out_shape=[jax.ShapeDtypeStruct((s, 5 * dc + 2 * d), BF16), jax.ShapeDtypeStruct((s, dc), BF16),
                   jax.ShapeDtypeStruct((s, d), F32),
                   jax.ShapeDtypeStruct((n_shards, dc, q4), BF16), jax.ShapeDtypeStruct((n_shards, dc, q4), BF16),
                   jax.ShapeDtypeStruct((n_shards, q4, d), BF16),
                   jax.ShapeDtypeStruct((1, d), F32), jax.ShapeDtypeStruct((1, d), F32),
                   jax.ShapeDtypeStruct((1, dc), F32), jax.ShapeDtypeStruct((SUBLANES, dc), F32)],
        scratch_shapes=[pltpu.VMEM((dc, d), F32), pltpu.VMEM((dc, d), F32), pltpu.VMEM((d, d), F32),
                        pltpu.VMEM((SUBLANES, dc), F32)],
        compiler_params=_params("arbitrary"),
    )(att, proj, proj, proj, proj, proj, proj, proj, proj, proj, x, tgt,
      w_att, w_conv, w_out, conv_w8, conv_b, fin_g)


def _in_proj_bwd_x(dqkv, dpb, w_in, x, dx2, g):
    s, d = x.shape
    tn = dqkv.shape[1]
    nb = dpb.shape[1] // tn
    n = s // TM_MM

    def body(*refs):
        dps, ws = refs[:nb + 1], refs[nb + 1:2 * nb + 2]
        x_ref, dx2_ref, g_ref, gx_ref, gng_ref = refs[2 * nb + 2:]
        i = pl.program_id(0)

        @pl.when(i == 0)
        def _():
            gng_ref[...] = jnp.zeros_like(gng_ref)

        dh = _nt(dps[0][...], ws[0][0])
        for j in range(1, nb + 1):
            dh = dh + _nt(dps[j][...], ws[j][0])
        xv = x_ref[...]
        r = lax.rsqrt(jnp.mean(xv * xv, axis=-1, keepdims=True) + EPS)
        xn = xv * r
        gng_ref[...] += jnp.sum(dh * xn, axis=0, keepdims=True)
        dhn = dh * g_ref[...]
        gx_ref[...] = dx2_ref[...] + r * (dhn - xn * jnp.mean(dhn * xn, axis=-1, keepdims=True))

    tile = lambda width, col_blk: pl.BlockSpec((TM_MM, width), lambda i: (i, col_blk))
    wspec = lambda blk: pl.BlockSpec((1, d, tn), lambda i: (blk, 0, 0), pipeline_mode=pl.Buffered(1))
    return pl.pallas_call(
        body, name="in_proj_bwd_x", grid=(n,),
        in_specs=[tile(tn, 0)] + [tile(tn, j) for j in range(nb)] + [wspec(j) for j in range(nb + 1)]
        + [tile(d, 0), tile(d, 0), pl.BlockSpec((1, d), lambda i: (0, 0))],
        out_specs=[tile(d, 0), pl.BlockSpec((1, d), lambda i: (0, 0))],
        out_shape=[jax.ShapeDtypeStruct((s, d), F32), jax.ShapeDtypeStruct((1, d), F32)],
        compiler_params=_params("arbitrary"),
    )(dqkv, *([dpb] * nb), *([w_in] * (nb + 1)), x, dx2, g)


def _in_proj_bwd_w(h, dqkv, dpb, order):
    s, d = h.shape
    tn = dqkv.shape[1]
    n = s // TM_BLK
    hr = d // 2
    settle = min(2, n - 1)

    def body(order_ref, h_ref, da_ref, db_ref, slots_ref, acc, sendbuf, pairbuf, chipbuf, psend, precv, send, recv, lsem):
        j, i = pl.program_id(0), pl.program_id(1)
        blk = order_ref[j]
        pos = _position()
        x, y, c = pos

        @pl.when(i == 0)
        def _():
            acc[...] = jnp.zeros_like(acc)

        @pl.when(blk == 0)
        def _():
            acc[...] += _tn(h_ref[...], da_ref[...])

        @pl.when(blk > 0)
        def _():
            acc[...] += _tn(h_ref[...], db_ref[...])

        def pair(step, half):
            return pltpu.make_async_remote_copy(
                src_ref=sendbuf.at[step, pl.ds(half * hr, hr), :], dst_ref=pairbuf.at[step],
                send_sem=psend.at[step], recv_sem=precv.at[step], device_id=(x, y, 1 - c), device_id_type=MESH)

        def ici(step):
            flip = OWNER_FLIPS[step]
            return pltpu.make_async_remote_copy(
                src_ref=chipbuf.at[step], dst_ref=slots_ref.at[flip], send_sem=send.at[step], recv_sem=recv.at[step],
                device_id=_peer(pos, 4 * (flip >> 1) + 2 * (flip & 1)), device_id_type=MESH)

        local = pltpu.make_async_copy(chipbuf.at[N_CHIPS - 1], slots_ref.at[0], lsem.at[0])

        def combine(step):
            pair(step, c).wait_recv()
            mine = sendbuf[step, pl.ds(c * hr, hr), :].astype(F32)
            chipbuf[step] = (mine + pairbuf[step].astype(F32)).astype(BF16)

        for step in range(N_CHIPS):
            @pl.when((j == step) & (i == n - 1))
            def _(step=step):
                sendbuf[step] = acc[...].astype(BF16)
                pair(step, 1 - c).start()

        for step in range(N_CHIPS - 1):
            @pl.when((j == step + 1) & (i == settle))
            def _(step=step):
                combine(step)
                ici(step).start()

        @pl.when((j == N_CHIPS - 1) & (i == n - 1))
        def _():
            combine(N_CHIPS - 1)
            local.start()
            for step in range(N_CHIPS - 1):
                ici(step).wait_recv()
            for step in range(N_CHIPS - 1):
                ici(step).wait_send()
            for step in range(N_CHIPS):
                pair(step, 1 - c).wait_send()
            local.wait()

    return pl.pallas_call(
        body, name="in_proj_bwd_w",
        grid_spec=pltpu.PrefetchScalarGridSpec(
            num_scalar_prefetch=1, grid=(N_CHIPS, n),
            in_specs=[pl.BlockSpec((TM_BLK, d), lambda j, i, order: (i, 0)),
                      pl.BlockSpec((TM_BLK, tn), lambda j, i, order: (jnp.where(order[j] == 0, i, 0), 0)),
                      pl.BlockSpec((TM_BLK, tn), lambda j, i, order: (jnp.where(order[j] == 0, 0, i),
                                                                     jnp.maximum(order[j] - 1, 0)))],
            out_specs=[ANY],
            scratch_shapes=[pltpu.VMEM((d, tn), F32), pltpu.VMEM((N_CHIPS, d, tn), BF16),
                            pltpu.VMEM((N_CHIPS, hr, tn), BF16), pltpu.VMEM((N_CHIPS, hr, tn), BF16),
                            pltpu.SemaphoreType.DMA((N_CHIPS,)), pltpu.SemaphoreType.DMA((N_CHIPS,)),
                            pltpu.SemaphoreType.DMA((N_CHIPS - 1,)), pltpu.SemaphoreType.DMA((N_CHIPS - 1,)),
                            pltpu.SemaphoreType.DMA((1,))]),
        out_shape=[jax.ShapeDtypeStruct((N_CHIPS, hr, tn), BF16)],
        compiler_params=_params("arbitrary", "arbitrary"),
    )(order, h, dqkv, dpb)[0]


LOSS_ROW = 6


def _adam_update(w, g, m, v):
    c1 = 1.0 / (1.0 - ADAM_B1 ** ADAM_STEP)
    c2 = 1.0 / (1.0 - ADAM_B2 ** ADAM_STEP)
    m2 = ADAM_B1 * m + (1.0 - ADAM_B1) * g
    v2 = ADAM_B2 * v + (1.0 - ADAM_B2) * (g * g)
    return -ADAM_LR * ((m2 * c1) / (jnp.sqrt(v2 * c2) + ADAM_EPS) + ADAM_WD * w), m2, v2


def _adamw_small(recv, params, moments_m, moments_v):
    k = recv.shape[0]
    n_par = len(params)
    cshard = params[3].shape[1]

    def body(*refs):
        r_ref = refs[0]
        ws, ms, vs = refs[1:1 + n_par], refs[1 + n_par:1 + 2 * n_par], refs[1 + 2 * n_par:1 + 3 * n_par]
        loss_ref = refs[1 + 3 * n_par]
        outs = refs[2 + 3 * n_par:]
        total = r_ref[0]
        for slot in range(1, k):
            total = total + r_ref[slot]
        loss_ref[...] = jnp.sum(total[LOSS_ROW:LOSS_ROW + 1, :], axis=1, keepdims=True)
        chip = 2 * lax.axis_index("x") + lax.axis_index("y")
        g_cw = jnp.zeros((3, cshard), F32)
        for sh in range(N_CHIPS):
            g_cw = g_cw + jnp.where(chip == sh, total[3:6, sh * cshard:(sh + 1) * cshard], 0.0)
        grads = [total[0:1, :], total[1:2, :], total[2:3, :ws[2].shape[1]], g_cw, total[8:16, :ws[4].shape[1]]]
        for p in range(n_par):
            delta, m2, v2 = _adam_update(ws[p][...], grads[p], ms[p][...], vs[p][...])
            for q, val in enumerate((grads[p], delta, m2, v2)):
                outs[4 * p + q][...] = val

    shapes = [jax.ShapeDtypeStruct((1, 1), F32)]
    for p in params:
        shapes += [jax.ShapeDtypeStruct(p.shape, F32)] * 4
    return pl.pallas_call(body, name="adamw_small", out_shape=shapes)(recv, *params, *moments_m, *moments_v)


def _adamw(w, g, m, v, name, rows_per_step):
    r, c = w.shape

    def body(w_ref, g_ref, m_ref, v_ref, d_ref, mo_ref, vo_ref):
        d_ref[...], mo_ref[...], vo_ref[...] = _adam_update(w_ref[...], g_ref[...], m_ref[...], v_ref[...])

    spec = pl.BlockSpec((rows_per_step, c), lambda i: (i, 0))
    shape = jax.ShapeDtypeStruct((r, c), F32)
    return pl.pallas_call(
        body, name=name, grid=(r // rows_per_step,),
        in_specs=[spec] * 4, out_specs=[spec] * 3, out_shape=[shape] * 3,
        compiler_params=_params("parallel"),
    )(w, g, m, v)


ANY = pl.BlockSpec(memory_space=pl.ANY)
N_CHIPS = 4
N_DEV = 8
OWNER_FLIPS = (3, 1, 2, 0)


def _position():
    return lax.axis_index("x"), lax.axis_index("y"), lax.axis_index("c")


def _gather_out_shapes(shards, kinds, cw8):
    full = [(a.shape[0], a.shape[1] * N_CHIPS) if k == "cols" else (a.shape[0] * N_CHIPS, a.shape[1])
            for a, k in zip(shards, kinds)]
    return [jax.ShapeDtypeStruct(f, BF16) for f in full] + [
        jax.ShapeDtypeStruct((N_CHIPS,) + cw8.shape, cw8.dtype)]


def _gather_sems(nw):
    return [pltpu.SemaphoreType.DMA((3, nw)), pltpu.SemaphoreType.DMA((3, nw)),
            pltpu.SemaphoreType.DMA((3, nw)), pltpu.SemaphoreType.DMA((3, nw)),
            pltpu.SemaphoreType.DMA((3,)), pltpu.SemaphoreType.DMA((3,)), pltpu.SemaphoreType.DMA((nw + 1,))]


def _gather_plan(kinds, srcs, dsts, cw, cw_all, send1, recv1, send2, recv2, ssend, srecv, lsem):
    nw = len(srcs)
    x, y, c = _position()
    mine = 2 * x + y
    chips = [(x, 1 - y), (1 - x, y), (1 - x, 1 - y)]

    def window(w, shard, half):
        r, cc = srcs[w].shape
        hr = r // 2
        if kinds[w] == "cols":
            rows = pl.ds(0, r) if half is None else pl.ds(half * hr, hr)
            return dsts[w].at[rows, pl.ds(shard * cc, cc)]
        rows = pl.ds(shard * r, r) if half is None else pl.ds(shard * r + half * hr, hr)
        return dsts[w].at[rows, :]

    def my_half(w):
        hr = srcs[w].shape[0] // 2
        return srcs[w].at[pl.ds(c * hr, hr), :]

    def local():
        return [pltpu.make_async_copy(srcs[w], window(w, mine, None), lsem.at[w]) for w in range(nw)] + [
            pltpu.make_async_copy(cw, cw_all.at[mine], lsem.at[nw])]

    def ici(k, w, shard):
        kx, ky = chips[k]
        return pltpu.make_async_remote_copy(
            src_ref=my_half(w), dst_ref=window(w, shard, c), send_sem=send1.at[k, w], recv_sem=recv1.at[k, w],
            device_id=(kx, ky, c), device_id_type=MESH)

    def d2d(k, w, shard, half):
        return pltpu.make_async_remote_copy(
            src_ref=window(w, shard, half), dst_ref=window(w, shard, half),
            send_sem=send2.at[k, w], recv_sem=recv2.at[k, w], device_id=(x, y, 1 - c), device_id_type=MESH)

    def small(k, shard):
        kx, ky = chips[k]
        return pltpu.make_async_remote_copy(
            src_ref=cw, dst_ref=cw_all.at[shard], send_sem=ssend.at[k], recv_sem=srecv.at[k],
            device_id=(kx, ky, c), device_id_type=MESH)

    def theirs(k):
        kx, ky = chips[k]
        return 2 * kx + ky

    def start():
        for cp in local():
            cp.start()
        for k in range(3):
            for w in range(nw):
                ici(k, w, mine).start()
            small(k, mine).start()

    def forward():
        for k in range(3):
            for w in range(nw):
                ici(k, w, theirs(k)).wait_recv()
                d2d(k, w, theirs(k), c).start()

    def finish():
        for k in range(3):
            for w in range(nw):
                d2d(k, w, theirs(k), 1 - c).wait_recv()
            small(k, theirs(k)).wait_recv()
        for k in range(3):
            for w in range(nw):
                ici(k, w, mine).wait_send()
                d2d(k, w, theirs(k), c).wait_send()
            small(k, mine).wait_send()
        for cp in local():
            cp.wait()

    return start, forward, finish


def _scatter_out_shapes(parts):
    return [jax.ShapeDtypeStruct((N_DEV, p.shape[1] // 2, p.shape[2]), p.dtype) for p in parts]


def _scatter_sems(nw):
    return [pltpu.SemaphoreType.DMA((N_DEV, nw)), pltpu.SemaphoreType.DMA((N_DEV, nw)), pltpu.SemaphoreType.DMA((nw,))]


def _peer(pos, k):
    x, y, c = pos
    return ((1 - x) if k & 4 else x, (1 - y) if k & 2 else y, (1 - c) if k & 1 else c)


def _scatter_plan(srcs, dsts, send, recv, lsem):
    nw = len(srcs)
    pos = _position()

    def piece(w, k):
        px, py, pc = _peer(pos, k)
        hr = srcs[w].shape[1] // 2
        return srcs[w].at[2 * px + py, pl.ds(pc * hr, hr), :]

    def remote(w, k):
        return pltpu.make_async_remote_copy(
            src_ref=piece(w, k), dst_ref=dsts[w].at[k], send_sem=send.at[k, w], recv_sem=recv.at[k, w],
            device_id=_peer(pos, k), device_id_type=MESH)

    def local(w):
        return pltpu.make_async_copy(piece(w, 0), dsts[w].at[0], lsem.at[w])

    def start():
        for w in range(nw):
            local(w).start()
        for k in range(1, N_DEV):
            for w in range(nw):
                remote(w, k).start()

    def finish():
        for k in range(1, N_DEV):
            for w in range(nw):
                remote(w, k).wait_recv()
        for k in range(1, N_DEV):
            for w in range(nw):
                remote(w, k).wait_send()
        for w in range(nw):
            local(w).wait()

    return start, finish


def _reduce_pair(slots, small):
    nw = len(slots)

    def body(*refs):
        srcs, sm = refs[:nw], refs[nw]
        dsts, sm_all = refs[nw + 1:2 * nw + 1], refs[2 * nw + 1]
        halves = refs[2 * nw + 2:3 * nw + 2]
        send, recv, ssend, srecv, lsem = refs[3 * nw + 2:]
        pos = _position()
        x, y, c = pos
        me = 4 * x + 2 * y + c

        def rows(w, half):
            hr = halves[w].shape[0]
            return dsts[w].at[pl.ds(half * hr, hr), :]

        def remote(w, half):
            return pltpu.make_async_remote_copy(
                src_ref=halves[w], dst_ref=rows(w, half), send_sem=send.at[w], recv_sem=recv.at[w],
                device_id=(x, y, 1 - c), device_id_type=MESH)

        def bcast(k, slot):
            return pltpu.make_async_remote_copy(
                src_ref=sm, dst_ref=sm_all.at[slot], send_sem=ssend.at[k], recv_sem=srecv.at[k],
                device_id=_peer(pos, k), device_id_type=MESH)

        small_copies = [bcast(k, me) for k in range(1, N_DEV)]
        own_small = pltpu.make_async_copy(sm, sm_all.at[me], lsem.at[nw])
        for cp in small_copies + [own_small]:
            cp.start()
        big = []
        for w in range(nw):
            total = srcs[w][0].astype(F32)
            for k in range(1, srcs[w].shape[0]):
                total = total + srcs[w][k].astype(F32)
            halves[w][...] = total
            big += [remote(w, c), pltpu.make_async_copy(halves[w], rows(w, c), lsem.at[w])]
            big[-2].start()
            big[-1].start()
        for w in range(nw):
            remote(w, 1 - c).wait_recv()
        for k in range(1, N_DEV):
            px, py, pc = _peer(pos, k)
            bcast(k, 4 * px + 2 * py + pc).wait_recv()
        for w in range(nw):
            big[2 * w].wait_send()
            big[2 * w + 1].wait()
        for cp in small_copies:
            cp.wait_send()
        own_small.wait()

    vmem = pl.BlockSpec(memory_space=pltpu.VMEM)
    half_shapes = [(sl.shape[1], sl.shape[2]) for sl in slots]
    return pl.pallas_call(
        body, name="reduce_pair",
        in_specs=[vmem] * (nw + 1), out_specs=[ANY] * (nw + 1),
        out_shape=[jax.ShapeDtypeStruct((2 * r, cc), F32) for r, cc in half_shapes]
        + [jax.ShapeDtypeStruct((N_DEV,) + small.shape, small.dtype)],
        scratch_shapes=[pltpu.VMEM(hs, F32) for hs in half_shapes]
        + [pltpu.SemaphoreType.DMA((nw,)), pltpu.SemaphoreType.DMA((nw,)),
           pltpu.SemaphoreType.DMA((N_DEV,)), pltpu.SemaphoreType.DMA((N_DEV,)),
           pltpu.SemaphoreType.DMA((nw + 1,))],
        compiler_params=pltpu.CompilerParams(vmem_limit_bytes=VMEM_LIMIT),
    )(*slots, small)


def _pad_to(a, rows, cols):
    return jnp.pad(a, ((0, rows - a.shape[0]), (0, cols - a.shape[1])))


def _pack_small(norm_g, fin_g, conv_b, conv_w, loss_vec, rel):
    rows = [_pad_to(norm_g, 1, SMALL_COLS), _pad_to(fin_g, 1, SMALL_COLS), _pad_to(conv_b, 1, SMALL_COLS),
            _pad_to(conv_w, 3, SMALL_COLS), _pad_to(loss_vec, 2, SMALL_COLS), _pad_to(rel, HEADS, SMALL_COLS)]
    return jnp.concatenate(rows, axis=0)


def kernel(x, norm_g, w_in, rel_bias, w_att_out, conv_w, conv_b, w_conv_out, w_out, final_norm_g, loss_target, m_norm_g, m_w_in, m_rel_bias, m_w_att_out, m_conv_w, m_conv_b, m_w_conv_out, m_w_out, m_final_norm_g, v_norm_g, v_w_in, v_rel_bias, v_w_att_out, v_conv_w, v_conv_b, v_w_conv_out, v_w_out, v_final_norm_g):
    xs, tgt = x[0], loss_target[0]
    cshard = conv_w.shape[2]
    chip = 2 * lax.axis_index("x") + lax.axis_index("y")

    shards = [w_in[0], w_att_out[0], w_conv_out[0], w_out[0]]
    cw8 = _pad_to(conv_w[0], SUBLANES, cshard)
    flips = jnp.arange(N_CHIPS, dtype=jnp.int32)
    own_first = jnp.bitwise_xor(chip, flips)
    own_last = jnp.bitwise_xor(chip, jnp.asarray(OWNER_FLIPS, jnp.int32))

    proj, h, wb_in = _in_proj_gather(xs, norm_g, shards[0], own_first)
    diag = jnp.take(rel_bias[0], _diag_rel_index(), axis=1)
    att, lse, wb_att, wb_conv, wb_out, cw_all = _attn_fwd(diag, proj, shards[1:], ["cols", "cols", "rows"], cw8)
    conv_w_full = jnp.transpose(cw_all, (1, 0, 2)).reshape(SUBLANES, N_CHIPS * cshard)
    (dpb, d_att, dx2, g_att_p, g_conv_p, g_out_p, loss_vec, g_fin, g_cb, g_cw) = _mixer_mid(
        att, proj, xs, tgt, wb_att, wb_conv, wb_out, conv_w_full, conv_b, final_norm_g[None, :])
    dqkv, g_rel, r_att, r_conv, r_out = _attn_bwd(diag, proj, d_att, att, lse, [g_att_p, g_conv_p, g_out_p])
    grad_x, g_norm = _in_proj_bwd_x(dqkv, dpb, wb_in, xs, dx2, norm_g)
    r_in = _in_proj_bwd_w(h, dqkv, dpb, own_last)

    small = _pack_small(g_norm, g_fin, g_cb, g_cw[0:3], loss_vec, g_rel)
    gw_in, gw_att, gw_conv, gw_out, r_small = _reduce_pair([r_in, r_att, r_conv, r_out], small)
    small_out = _adamw_small(
        r_small,
        [norm_g, final_norm_g[None, :], conv_b, conv_w[0], rel_bias[0]],
        [m_norm_g, m_final_norm_g[None, :], m_conv_b, m_conv_w[0], m_rel_bias[0]],
        [v_norm_g, v_final_norm_g[None, :], v_conv_b, v_conv_w[0], v_rel_bias[0]])
    loss = small_out[0][0, 0]
    small_names = ["norm_g", "final_norm_g", "conv_b", "conv_w", "rel_bias"]
    fix = {"norm_g": lambda a: a, "final_norm_g": lambda a: a[0], "conv_b": lambda a: a,
           "conv_w": lambda a: a[None], "rel_bias": lambda a: a[None]}
    small_res = {name: [fix[name](small_out[1 + 4 * p + q]) for q in range(4)] for p, name in enumerate(small_names)}

    big = {}
    for name, w, g, m, v, rows in (("w_in", w_in, gw_in, m_w_in, v_w_in, 128),
                                   ("w_att_out", w_att_out, gw_att, m_w_att_out, v_w_att_out, 256),
                                   ("w_conv_out", w_conv_out, gw_conv, m_w_conv_out, v_w_conv_out, 256),
                                   ("w_out", w_out, gw_out, m_w_out, v_w_out, 128)):
        dw, mw, vw = _adamw(w[0], g, m[0], v[0], "adamw_" + name, rows)
        big[name] = (g[None], dw[None], mw[None], vw[None])

    order = ["norm_g", "w_in", "rel_bias", "w_att_out", "conv_w", "conv_b", "w_conv_out", "w_out", "final_norm_g"]
    outs = [loss, grad_x[None]]
    for which in range(4):
        for name in order:
            outs.append(big[name][which] if name in big else small_res[name][which])
    return tuple(outs)
```

```python
import numpy as np
import jax
import jax.numpy as jnp
from jax import lax
from jax.experimental import pallas as pl
from jax.experimental.pallas import tpu as pltpu

F32 = jnp.float32
BF16 = jnp.bfloat16
MESH = pl.DeviceIdType.MESH

CHUNK = 64
N_LEFT = 8
HEADS = 8
HEAD_DIM = 64
D_ATT = HEADS * HEAD_DIM
MAX_REL = 128
N_REL = 2 * MAX_REL + 1
EPS = 1e-6
NEG_BIG = -1e30
ADAM_LR, ADAM_B1, ADAM_B2, ADAM_EPS, ADAM_WD, ADAM_STEP = 0.001, 0.9, 0.999, 1e-08, 0.01, 10

LANES = 128
SUBLANES = 8
VMEM_LIMIT = 56 * 1024 * 1024

QB = 2 * CHUNK
KW = N_LEFT * CHUNK + QB
DIAG = KW + QB
TQ = N_LEFT * CHUNK
TM_MID = 256
TM_MM = 512
TM_BLK = 1024
SMALL_ROWS, SMALL_COLS = 16, 1024


def _params(*sem):
    return pltpu.CompilerParams(dimension_semantics=sem, vmem_limit_bytes=VMEM_LIMIT)


def _nt(a, b):
    return lax.dot_general(a, b, (((1,), (1,)), ((), ())), preferred_element_type=F32)


def _tn(a, b):
    return lax.dot_general(a, b, (((0,), (0,)), ((), ())), preferred_element_type=F32)


def _nn(a, b):
    return jnp.dot(a, b, preferred_element_type=F32)


def _diag_rel_index():
    d = np.arange(DIAG)
    diff = np.where(d < KW, d, d - DIAG)
    rel = N_LEFT * CHUNK - diff
    return np.clip(rel, -MAX_REL, MAX_REL) + MAX_REL


def _build_bias(diag_ref, bias_scr):
    r = lax.broadcasted_iota(jnp.int32, (QB, KW), 0) // CHUNK
    s = lax.broadcasted_iota(jnp.int32, (QB, KW), 1) // CHUNK
    allowed = (s >= r) & (s <= r + N_LEFT)
    for h in range(HEADS):
        row = jnp.broadcast_to(diag_ref[h:h + 1, :], (QB, DIAG))
        t = pltpu.roll(row, 0, 1, stride=1, stride_axis=0)
        bias_scr[h // 2, (h % 2) * QB:(h % 2 + 1) * QB, :] = jnp.where(allowed, t[:, :KW], NEG_BIG)


def _stack_heads(a, lane_hi):
    zero = jnp.zeros_like(a)
    return jnp.concatenate([jnp.where(lane_hi, zero, a), jnp.where(lane_hi, a, zero)], axis=0)


def _in_proj_gather(x, g, shard, order):
    s, d = x.shape
    tn = shard.shape[1]
    n = s // TM_BLK
    hr = d // 2

    def body(order_ref, x_ref, g_ref, shard32_ref, proj_ref, h_ref, wfull_ref, shard_ref, hbuf, wbuf,
             send1, recv1, send2, recv2, lsem):
        del order_ref
        j, i = pl.program_id(0), pl.program_id(1)
        x, y, c = _position()
        mine = 2 * x + y
        chips = [(x, 1 - y), (1 - x, y), (1 - x, 1 - y)]

        def theirs(k):
            return 2 * chips[k][0] + chips[k][1]

        def half_rows(half):
            return pl.ds(half * hr, hr)

        def ici(k, shard_index):
            return pltpu.make_async_remote_copy(
                src_ref=shard_ref.at[half_rows(c), :], dst_ref=wfull_ref.at[shard_index, half_rows(c), :],
                send_sem=send1.at[k], recv_sem=recv1.at[k], device_id=(*chips[k], c), device_id_type=MESH)

        def d2d(k, half):
            return pltpu.make_async_remote_copy(
                src_ref=wbuf.at[k % 2, half_rows(half), :], dst_ref=wfull_ref.at[theirs(k), half_rows(half), :],
                send_sem=send2.at[k], recv_sem=recv2.at[k], device_id=(x, y, 1 - c), device_id_type=MESH)

        def load(k, half, sem):
            return pltpu.make_async_copy(wfull_ref.at[theirs(k), half_rows(half), :],
                                         wbuf.at[k % 2, half_rows(half), :], lsem.at[sem])

        own = pltpu.make_async_copy(shard_ref, wfull_ref.at[mine], lsem.at[0])

        def put_proj(block):
            for grp in range(tn // D_ATT):
                proj_ref[grp] = block[:, grp * D_ATT:(grp + 1) * D_ATT].astype(BF16)

        @pl.when((j == 0) & (i == 0))
        def _():
            shard_ref[...] = shard32_ref[...].astype(BF16)
            own.start()
            ici(0, mine).start()
            ici(1, mine).start()

        for k in range(3):
            first = max(n - 3, 0) if k == 0 else min(n // 2, n - 1)

            @pl.when((j == k) & (i == first))
            def _(k=k):
                if k == 0:
                    ici(0, mine).wait_send()
                    ici(1, mine).wait_send()
                    ici(2, mine).start()
                if k == 2:
                    d2d(0, c).wait_send()
                ici(k, theirs(k)).wait_recv()
                load(k, c, 1).start()

            @pl.when((j == k) & (i == min(first + 1, n - 1)))
            def _(k=k):
                load(k, c, 1).wait()
                d2d(k, c).start()

            @pl.when((j == k) & (i == min(first + 2, n - 1)))
            def _(k=k):
                d2d(k, 1 - c).wait_recv()
                load(k, 1 - c, 2).start()

            @pl.when((j == k + 1) & (i == 0))
            def _(k=k):
                load(k, 1 - c, 2).wait()

        @pl.when(j == 0)
        def _():
            xv = x_ref[...]
            r = lax.rsqrt(jnp.mean(xv * xv, axis=-1, keepdims=True) + EPS)
            hv = ((xv * r) * g_ref[...]).astype(BF16)
            hbuf[i] = hv
            h_ref[...] = hv
            put_proj(_nn(hv, shard_ref[...]))

        for k in range(3):
            @pl.when(j == k + 1)
            def _(k=k):
                put_proj(_nn(hbuf[i], wbuf[k % 2]))

        @pl.when((j == 3) & (i == n - 1))
        def _():
            ici(2, mine).wait_send()
            d2d(1, c).wait_send()
            d2d(2, c).wait_send()
            own.wait()

    return pl.pallas_call(
        body, name="in_proj_gather",
        grid_spec=pltpu.PrefetchScalarGridSpec(
            num_scalar_prefetch=1, grid=(N_CHIPS, n),
            in_specs=[pl.BlockSpec((TM_BLK, d), lambda j, i, order: (jnp.where(j == 0, i, n - 1), 0)),
                      pl.BlockSpec((1, d), lambda j, i, order: (0, 0)), pl.BlockSpec(memory_space=pltpu.VMEM)],
            out_specs=[pl.BlockSpec((tn // D_ATT, TM_BLK, D_ATT), lambda j, i, order: (order[j], i, 0)),
                       pl.BlockSpec((TM_BLK, d), lambda j, i, order: (jnp.where(j == 0, i, n - 1), 0)), ANY],
            scratch_shapes=[pltpu.VMEM((d, tn), BF16), pltpu.VMEM((n, TM_BLK, d), BF16), pltpu.VMEM((2, d, tn), BF16),
                            pltpu.SemaphoreType.DMA((3,)), pltpu.SemaphoreType.DMA((3,)),
                            pltpu.SemaphoreType.DMA((3,)), pltpu.SemaphoreType.DMA((3,)), pltpu.SemaphoreType.DMA((3,))]),
        out_shape=[jax.ShapeDtypeStruct((N_CHIPS * tn // D_ATT, s, D_ATT), BF16), jax.ShapeDtypeStruct((s, d), BF16),
                   jax.ShapeDtypeStruct((N_CHIPS, d, tn), BF16)],
        compiler_params=_params("arbitrary", "arbitrary"),
    )(order, x, g, shard)


def _attn_fwd(diag, proj, shards, kinds, cw8):
    s = proj.shape[1]
    n = s // TQ
    nw = len(shards)
    scale = HEAD_DIM ** -0.5

    def body(*refs):
        diag_ref, q_ref, kp_ref, kc_ref, vp_ref, vc_ref = refs[:6]
        srcs, cw = refs[6:6 + nw], refs[6 + nw]
        o_ref, lse_ref = refs[7 + nw:9 + nw]
        dsts, cw_all = refs[9 + nw:9 + 2 * nw], refs[9 + 2 * nw]
        bias_scr = refs[10 + 2 * nw]
        casts = refs[11 + 2 * nw:11 + 3 * nw]
        start, forward, finish = _gather_plan(kinds, casts, dsts, cw, cw_all, *refs[11 + 3 * nw:])
        i = pl.program_id(0)

        @pl.when(i == 0)
        def _():
            for w in range(nw):
                casts[w][...] = srcs[w][...].astype(BF16)
            start()
            _build_bias(diag_ref, bias_scr)

        @pl.when(i == n // 2)
        def _():
            forward()

        @pl.when(i == n - 1)
        def _():
            finish()

        lane_hi = lax.broadcasted_iota(jnp.int32, (QB, LANES), 1) >= HEAD_DIM

        def block(b, first_tile):
            r0, n_prev = b * QB, TQ - b * QB
            n_cur = KW - n_prev
            for p in range(HEADS // 2):
                lanes = slice(LANES * p, LANES * (p + 1))
                q2 = _stack_heads(q_ref[r0:r0 + QB, lanes] * scale, lane_hi)
                s_cur = _nt(q2, kc_ref[0:n_cur, lanes]) + bias_scr[p, :, n_prev:KW]
                if first_tile:
                    sc = s_cur
                else:
                    sc = jnp.concatenate([_nt(q2, kp_ref[r0:TQ, lanes]) + bias_scr[p, :, 0:n_prev], s_cur], axis=1)
                m = jnp.max(sc, axis=1, keepdims=True)
                pe = jnp.exp(sc - m)
                l = jnp.sum(pe, axis=1, keepdims=True)
                pb = pe.astype(BF16)
                if first_tile:
                    o2 = _nn(pb, vc_ref[0:n_cur, lanes]) / l
                else:
                    o2 = (_nn(pb[:, 0:n_prev], vp_ref[r0:TQ, lanes]) + _nn(pb[:, n_prev:KW], vc_ref[0:n_cur, lanes])) / l
                lse2 = m + jnp.log(l)
                lse_ref[r0:r0 + QB, 2 * p:2 * p + 1] = lse2[0:QB, :]
                lse_ref[r0:r0 + QB, 2 * p + 1:2 * p + 2] = lse2[QB:2 * QB, :]
                o_ref[r0:r0 + QB, lanes] = jnp.where(lane_hi, o2[QB:2 * QB, :], o2[0:QB, :]).astype(BF16)

        @pl.when(i == 0)
        def _():
            for b in range(TQ // QB):
                block(b, True)

        @pl.when(i > 0)
        def _():
            for b in range(TQ // QB):
                block(b, False)

    blk = lambda grp, prev: pl.BlockSpec(
        (None, TQ, D_ATT), (lambda i: (grp, jnp.maximum(i - 1, 0), 0)) if prev else (lambda i: (grp, i, 0)))
    vmem = pl.BlockSpec(memory_space=pltpu.VMEM)
    return pl.pallas_call(
        body, name="attn_fwd", grid=(n,),
        in_specs=[pl.BlockSpec((HEADS, DIAG), lambda i: (0, 0)),
                  blk(0, False), blk(1, True), blk(1, False), blk(2, True), blk(2, False)] + [vmem] * (nw + 1),
        out_specs=[pl.BlockSpec((TQ, D_ATT), lambda i: (i, 0)), pl.BlockSpec((TQ, HEADS), lambda i: (i, 0))]
        + [ANY] * (nw + 1),
        out_shape=[jax.ShapeDtypeStruct((s, D_ATT), BF16), jax.ShapeDtypeStruct((s, HEADS), F32)]
        + _gather_out_shapes(shards, kinds, cw8),
        scratch_shapes=[pltpu.VMEM((HEADS // 2, 2 * QB, KW), F32)] + [pltpu.VMEM(a.shape, BF16) for a in shards]
        + _gather_sems(nw),
        compiler_params=_params("arbitrary"),
    )(diag, proj, proj, proj, proj, proj, *shards, cw8)


def _attn_bwd(diag, proj, d_att, att, lse, parts):
    s = proj.shape[1]
    n = s // TQ
    npart = len(parts)
    scale = HEAD_DIM ** -0.5
    rel_pad = 3 * LANES

    def body(*refs):
        diag_ref, q_ref, kp_ref, kc_ref, vp_ref, vc_ref, do_ref, o_ref, lse_ref = refs[:9]
        part_refs = refs[9:9 + npart]
        dqkv_ref, dbias_ref = refs[9 + npart:11 + npart]
        slot_refs = refs[11 + npart:11 + 2 * npart]
        bias_scr, dbias_acc, kcat, vcat, dk_acc, dv_acc, dq_scr = refs[11 + 2 * npart:18 + 2 * npart]
        start, finish = _scatter_plan(part_refs, slot_refs, *refs[18 + 2 * npart:])
        i = pl.program_id(0)

        @pl.when(i == 0)
        def _():
            start()
            _build_bias(diag_ref, bias_scr)
            dbias_acc[...] = jnp.zeros_like(dbias_acc)
            dk_acc[...] = jnp.zeros_like(dk_acc)
            dv_acc[...] = jnp.zeros_like(dv_acc)

        @pl.when(i > 0)
        def _():
            dqkv_ref[:, 0:D_ATT] = dq_scr[...]
            dk_acc[0:TQ, :] = dk_acc[TQ:2 * TQ, :]
            dk_acc[TQ:2 * TQ, :] = jnp.zeros((TQ, D_ATT), F32)
            dv_acc[0:TQ, :] = dv_acc[TQ:2 * TQ, :]
            dv_acc[TQ:2 * TQ, :] = jnp.zeros((TQ, D_ATT), F32)

        @pl.when(i < n)
        def _():
            kcat[0:TQ, :] = kp_ref[...]
            kcat[TQ:2 * TQ, :] = kc_ref[...]
            vcat[0:TQ, :] = vp_ref[...]
            vcat[TQ:2 * TQ, :] = vc_ref[...]

        lane_hi = lax.broadcasted_iota(jnp.int32, (QB, LANES), 1) >= HEAD_DIM
        col = lax.broadcasted_iota(jnp.int32, (2 * QB, KW), 1)

        def make_block(first_tile):
            def block(b, carry):
                r0 = pl.multiple_of(b * QB, QB)
                for p in range(HEADS // 2):
                    lanes = slice(LANES * p, LANES * (p + 1))
                    q2 = _stack_heads(q_ref[pl.ds(r0, QB), lanes] * scale, lane_hi)
                    kw = kcat[pl.ds(r0, KW), lanes]
                    vw = vcat[pl.ds(r0, KW), lanes]
                    dop = do_ref[pl.ds(r0, QB), lanes]
                    do2 = _stack_heads(dop, lane_hi)
                    prod = dop.astype(F32) * o_ref[pl.ds(r0, QB), lanes].astype(F32)
                    delta2 = jnp.concatenate(
                        [jnp.sum(jnp.where(lane_hi, 0.0, prod), axis=1, keepdims=True),
                         jnp.sum(jnp.where(lane_hi, prod, 0.0), axis=1, keepdims=True)], axis=0)
                    lse2 = jnp.concatenate([lse_ref[pl.ds(r0, QB), 2 * p:2 * p + 1],
                                            lse_ref[pl.ds(r0, QB), 2 * p + 1:2 * p + 2]], axis=0)
                    sc = _nt(q2, kw) + bias_scr[p]
                    if first_tile:
                        sc = jnp.where(col >= TQ - r0, sc, NEG_BIG)
                    pr = jnp.exp(sc - lse2)
                    ds = pr * (_nt(do2, vw) - delta2)
                    dbias_acc[p] += ds
                    dsb = ds.astype(BF16)
                    dv_acc[pl.ds(r0, KW), lanes] += _tn(pr.astype(BF16), do2)
                    dk_acc[pl.ds(r0, KW), lanes] += _tn(dsb, q2)
                    dq2 = _nn(dsb, kw)
                    dq = jnp.where(lane_hi, dq2[QB:2 * QB, :], dq2[0:QB, :]) * scale
                    dq_scr[pl.ds(r0, QB), lanes] = dq.astype(BF16)
                return carry
            return block

        @pl.when(i == 0)
        def _():
            lax.fori_loop(0, TQ // QB, make_block(True), 0)

        @pl.when((i > 0) & (i < n))
        def _():
            lax.fori_loop(0, TQ // QB, make_block(False), 0)

        @pl.when(i > 0)
        def _():
            dqkv_ref[:, D_ATT:2 * D_ATT] = dk_acc[0:TQ, :].astype(BF16)
            dqkv_ref[:, 2 * D_ATT:3 * D_ATT] = dv_acc[0:TQ, :].astype(BF16)

        @pl.when(i == n)
        def _():
            d_iota = lax.broadcasted_iota(jnp.int32, (DIAG, rel_pad), 0)
            n_iota = lax.broadcasted_iota(jnp.int32, (DIAG, rel_pad), 1)
            diff = jnp.where(d_iota < KW, d_iota, d_iota - DIAG)
            idx = jnp.clip(N_LEFT * CHUNK - diff, -MAX_REL, MAX_REL) + MAX_REL
            onehot = (idx == n_iota).astype(F32)
            rows = []
            for hd in range(HEADS):
                acc = dbias_acc[hd // 2, (hd % 2) * QB:(hd % 2 + 1) * QB, :]
                a = jnp.concatenate([acc, jnp.zeros((QB, DIAG - KW), F32)], axis=1)
                g8 = a[0:SUBLANES, :]
                for blk in range(1, QB // SUBLANES):
                    g8 = g8 + pltpu.roll(a[blk * SUBLANES:(blk + 1) * SUBLANES, :], DIAG - blk * SUBLANES, 1)
                g1 = g8[0:1, :]
                for r in range(1, SUBLANES):
                    g1 = g1 + pltpu.roll(g8[r:r + 1, :], DIAG - r, 1)
                rows.append(g1)
            g = jnp.concatenate(rows, axis=0)
            dbias_ref[...] = jnp.dot(g, onehot, preferred_element_type=F32, precision=lax.Precision.HIGHEST)
            finish()

    last = n - 1
    cur = lambda grp: pl.BlockSpec((None, TQ, D_ATT), lambda i: (grp, jnp.minimum(i, last), 0))
    prev = lambda grp: pl.BlockSpec((None, TQ, D_ATT), lambda i: (grp, jnp.maximum(jnp.minimum(i, last) - 1, 0), 0))
    tile = pl.BlockSpec((TQ, D_ATT), lambda i: (jnp.minimum(i, last), 0))
    return pl.pallas_call(
        body, name="attn_bwd", grid=(n + 1,),
        in_specs=[pl.BlockSpec((HEADS, DIAG), lambda i: (0, 0)),
                  cur(0), prev(1), cur(1), prev(2), cur(2), tile, tile,
                  pl.BlockSpec((TQ, HEADS), lambda i: (jnp.minimum(i, last), 0))] + [ANY] * npart,
        out_specs=[pl.BlockSpec((TQ, 3 * D_ATT), lambda i: (jnp.maximum(i - 1, 0), 0)),
                   pl.BlockSpec((HEADS, rel_pad), lambda i: (0, 0))] + [ANY] * npart,
        out_shape=[jax.ShapeDtypeStruct((s, 3 * D_ATT), BF16), jax.ShapeDtypeStruct((HEADS, rel_pad), F32)]
        + _scatter_out_shapes(parts),
        scratch_shapes=[pltpu.VMEM((HEADS // 2, 2 * QB, KW), F32), pltpu.VMEM((HEADS // 2, 2 * QB, KW), F32),
                        pltpu.VMEM((2 * TQ, D_ATT), BF16), pltpu.VMEM((2 * TQ, D_ATT), BF16),
                        pltpu.VMEM((2 * TQ, D_ATT), F32), pltpu.VMEM((2 * TQ, D_ATT), F32),
                        pltpu.VMEM((TQ, D_ATT), BF16)] + _scatter_sems(npart),
        compiler_params=_params("arbitrary"),
    )(diag, proj, proj, proj, proj, proj, d_att, att, lse, *parts)


def _shift_down(a, k, halo):
    rolled = pltpu.roll(a, k, 0)
    row = lax.broadcasted_iota(jnp.int32, halo.shape, 0)
    first = jnp.where(row < k, pltpu.roll(halo, k, 0), rolled[0:SUBLANES, :])
    return jnp.concatenate([first, rolled[SUBLANES:, :]], axis=0)


def _shift_up(a, k, nxt):
    tm = a.shape[0]
    rolled = pltpu.roll(a, tm - k, 0)
    row = lax.broadcasted_iota(jnp.int32, nxt.shape, 0)
    last = jnp.where(row >= SUBLANES - k, pltpu.roll(nxt, SUBLANES - k, 0), rolled[tm - SUBLANES:, :])
    return jnp.concatenate([rolled[:tm - SUBLANES, :], last], axis=0)


def _sigmoid(v):
    return 0.5 * jnp.tanh(0.5 * v) + 0.5


def _mixer_mid(att, proj, x, tgt, w_att, w_conv, w_out, conv_w8, conv_b, fin_g):
    s, d = x.shape
    dc = D_ATT
    n = s // TM_MID
    tm = TM_MID
    n_shards = 4

    def body(att_ref, za_ref, gb_ref, gc_ref, u_ref, zc_ref, hgc_ref, hu_ref, gatt_ref, gconv_ref, x_ref, t_ref,
             watt_ref, wconv_ref, wout_ref, cw_ref, cb_ref, fg_ref,
             dpb_ref, do_ref, dx2_ref, gatt_o, gconv_o, gout_o, loss_o, gfn_o, gcb_o, gcw_o,
             acc_att, acc_conv, acc_out, carry):
        i = pl.program_id(0)
        tile = n - 1 - i

        @pl.when(i == 0)
        def _():
            acc_att[...] = jnp.zeros_like(acc_att)
            acc_conv[...] = jnp.zeros_like(acc_conv)
            acc_out[...] = jnp.zeros_like(acc_out)
            carry[...] = jnp.zeros_like(carry)
            loss_o[...] = jnp.zeros_like(loss_o)
            gfn_o[...] = jnp.zeros_like(gfn_o)
            gcb_o[...] = jnp.zeros_like(gcb_o)
            gcw_o[...] = jnp.zeros_like(gcw_o)

        att_v = att_ref[...].astype(F32)
        za = za_ref[...].astype(F32)
        sa = _sigmoid(za)
        silu_a = za * sa
        a_b = (att_v * silu_a).astype(BF16)

        gb = gb_ref[...].astype(F32)
        gc = gc_ref[...].astype(F32)
        u = u_ref[...].astype(F32)
        zc = zc_ref[...].astype(F32)
        cu = gc * u
        halo = jnp.where(tile > 0, hgc_ref[...].astype(F32) * hu_ref[...].astype(F32), 0.0)
        cu1 = _shift_down(cu, 1, halo)
        cu2 = _shift_down(cu, 2, halo)
        w0, w1, w2 = cw_ref[0:1, :], cw_ref[1:2, :], cw_ref[2:3, :]
        vconv = w0 * cu2 + w1 * cu1 + w2 * cu + cb_ref[...]
        sc = _sigmoid(zc)
        silu_c = zc * sc
        c_b = (gb * vconv * silu_c).astype(BF16)

        y_att = _nn(a_b, watt_ref[...])
        y_conv = _nn(c_b, wconv_ref[...])
        ga = _sigmoid(jnp.concatenate([gatt_ref[0], gatt_ref[1]], axis=1).astype(F32))
        gv = _sigmoid(jnp.concatenate([gconv_ref[0], gconv_ref[1]], axis=1).astype(F32))
        m_b = (ga * y_att + gv * y_conv).astype(BF16)
        x2 = x_ref[...] + _nn(m_b, wout_ref[...])
        r2 = lax.rsqrt(jnp.mean(x2 * x2, axis=-1, keepdims=True) + EPS)
        x2n = x2 * r2
        fg = fg_ref[...]
        err = x2n * fg - t_ref[...]
        loss_o[...] += jnp.sum(err * err, axis=0, keepdims=True) * (0.5 / d)
        dy = err * (1.0 / d)
        gfn_o[...] += jnp.sum(dy * x2n, axis=0, keepdims=True)
        dyn = dy * fg
        dx2 = r2 * (dyn - x2n * jnp.mean(dyn * x2n, axis=-1, keepdims=True))
        dx2_ref[...] = dx2
        dx2_b = dx2.astype(BF16)

        dm = _nt(dx2_b, wout_ref[...])
        acc_out[...] += _tn(m_b, dx2_b)
        dy_att = dm * ga
        dy_conv = dm * gv
        dpb_ref[:, 5 * dc:5 * dc + d] = (dy_att * y_att * (1.0 - ga)).astype(BF16)
        dpb_ref[:, 5 * dc + d:5 * dc + 2 * d] = (dy_conv * y_conv * (1.0 - gv)).astype(BF16)
        dya_b = dy_att.astype(BF16)
        dyc_b = dy_conv.astype(BF16)
        da_in = _nt(dya_b, watt_ref[...])
        acc_att[...] += _tn(a_b, dya_b)
        dc_in = _nt(dyc_b, wconv_ref[...])
        acc_conv[...] += _tn(c_b, dyc_b)

        do_ref[...] = (da_in * silu_a).astype(BF16)
        dpb_ref[:, 0:dc] = (da_in * att_v * (sa * (1.0 + za * (1.0 - sa)))).astype(BF16)
        dpb_ref[:, dc:2 * dc] = (dc_in * vconv * silu_c).astype(BF16)
        dgs = dc_in * gb
        dvc = dgs * silu_c
        dpb_ref[:, 4 * dc:5 * dc] = (dgs * vconv * (sc * (1.0 + zc * (1.0 - sc)))).astype(BF16)
        gcb_o[...] += jnp.sum(dvc, axis=0, keepdims=True)
        gcw_o[0:1, :] += jnp.sum(dvc * cu2, axis=0, keepdims=True)
        gcw_o[1:2, :] += jnp.sum(dvc * cu1, axis=0, keepdims=True)
        gcw_o[2:3, :] += jnp.sum(dvc * cu, axis=0, keepdims=True)
        nxt = carry[...]
        dcu = w2 * dvc + w1 * _shift_up(dvc, 1, nxt) + w0 * _shift_up(dvc, 2, nxt)
        carry[...] = dvc[0:SUBLANES, :]
        dpb_ref[:, 2 * dc:3 * dc] = (dcu * u).astype(BF16)
        dpb_ref[:, 3 * dc:4 * dc] = (dcu * gc).astype(BF16)

        @pl.when(i == n - 1)
        def _():
            for j in range(n_shards):
                gatt_o[j] = acc_att[:, j * (d // n_shards):(j + 1) * (d // n_shards)].astype(BF16)
                gconv_o[j] = acc_conv[:, j * (d // n_shards):(j + 1) * (d // n_shards)].astype(BF16)
                gout_o[j] = acc_out[j * (d // n_shards):(j + 1) * (d // n_shards), :].astype(BF16)

    rev = lambda width, col_blk: pl.BlockSpec((tm, width), lambda i: (n - 1 - i, col_blk))
    grp = lambda g: pl.BlockSpec((None, tm, dc), lambda i: (g, n - 1 - i, 0))
    grp2 = lambda g2: pl.BlockSpec((2, tm, dc), lambda i: (g2, n - 1 - i, 0))
    halo_spec = lambda g: pl.BlockSpec(
        (None, SUBLANES, dc), lambda i: (g, jnp.maximum((n - 1 - i) * (tm // SUBLANES) - 1, 0), 0))
    const = lambda shape: pl.BlockSpec(shape, lambda i: tuple(0 for _ in shape), pipeline_mode=pl.Buffered(1))
    q4 = d // n_shards
    return pl.pallas_call(
        body, name="mixer_mid", grid=(n,),
        in_specs=[rev(dc, 0), grp(3), grp(4), grp(5), grp(6), grp(7),
                  halo_spec(5), halo_spec(6), grp2(4), grp2(5), rev(d, 0), rev(d, 0),
                  const((dc, d)), const((dc, d)), const((d, d)), const((SUBLANES, dc)), const((1, dc)), const((1, d))],
        out_specs=[rev(5 * dc + 2 * d, 0), rev(dc, 0), rev(d, 0),
                   const((n_shards, dc, q4)), const((n_shards, dc, q4)), const((n_shards, q4, d)),
                   const((1, d)), const((1, d)), const((1, dc)), const((SUBLANES, dc))],
        out_shape=[jax.ShapeDtypeStruct((s, 5 * dc + 2 * d), BF16), jax.ShapeDtypeStruct((s, dc), BF16),
                   jax.ShapeDtypeStruct((s, d), F32),
                   jax.ShapeDtypeStruct((n_shards, dc, q4), BF16), jax.ShapeDtypeStruct((n_shards, dc, q4), BF16),
                   jax.ShapeDtypeStruct((n_shards, q4, d), BF16),
                   jax.ShapeDtypeStruct((1, d), F32), jax.ShapeDtypeStruct((1, d), F32),
                   jax.ShapeDtypeStruct((1, dc), F32), jax.ShapeDtypeStruct((SUBLANES, dc), F32)],
        scratch_shapes=[pltpu.VMEM((dc, d), F32), pltpu.VMEM((dc, d), F32), pltpu.VMEM((d, d), F32),
                        pltpu.VMEM((SUBLANES, dc), F32)],
        compiler_params=_params("arbitrary"),
    )(att, proj, proj, proj, proj, proj, proj, proj, proj, proj, x, tgt,
      w_att, w_conv, w_out, conv_w8, conv_b, fin_g)


def _in_proj_bwd_x(dqkv, dpb, w_in, x, dx2, g):
    s, d = x.shape
    tn = dqkv.shape[1]
    nb = dpb.shape[1] // tn
    n = s // TM_MM

    def body(*refs):
        dps, ws = refs[:nb + 1], refs[nb + 1:2 * nb + 2]
        x_ref, dx2_ref, g_ref, gx_ref, gng_ref = refs[2 * nb + 2:]
        i = pl.program_id(0)

        @pl.when(i == 0)
        def _():
            gng_ref[...] = jnp.zeros_like(gng_ref)

        dh = _nt(dps[0][...], ws[0][0])
        for j in range(1, nb + 1):
            dh = dh + _nt(dps[j][...], ws[j][0])
        xv = x_ref[...]
        r = lax.rsqrt(jnp.mean(xv * xv, axis=-1, keepdims=True) + EPS)
        xn = xv * r
        gng_ref[...] += jnp.sum(dh * xn, axis=0, keepdims=True)
        dhn = dh * g_ref[...]
        gx_ref[...] = dx2_ref[...] + r * (dhn - xn * jnp.mean(dhn * xn, axis=-1, keepdims=True))

    tile = lambda width, col_blk: pl.BlockSpec((TM_MM, width), lambda i: (i, col_blk))
    wspec = lambda blk: pl.BlockSpec((1, d, tn), lambda i: (blk, 0, 0), pipeline_mode=pl.Buffered(1))
    return pl.pallas_call(
        body, name="in_proj_bwd_x", grid=(n,),
        in_specs=[tile(tn, 0)] + [tile(tn, j) for j in range(nb)] + [wspec(j) for j in range(nb + 1)]
        + [tile(d, 0), tile(d, 0), pl.BlockSpec((1, d), lambda i: (0, 0))],
        out_specs=[tile(d, 0), pl.BlockSpec((1, d), lambda i: (0, 0))],
        out_shape=[jax.ShapeDtypeStruct((s, d), F32), jax.ShapeDtypeStruct((1, d), F32)],
        compiler_params=_params("arbitrary"),
    )(dqkv, *([dpb] * nb), *([w_in] * (nb + 1)), x, dx2, g)


def _in_proj_bwd_w(h, dqkv, dpb, order):
    s, d = h.shape
    tn = dqkv.shape[1]
    n = s // TM_BLK
    hr = d // 2
    settle = min(2, n - 1)

    def body(order_ref, h_ref, da_ref, db_ref, slots_ref, acc, sendbuf, pairbuf, chipbuf, psend, precv, send, recv, lsem):
        j, i = pl.program_id(0), pl.program_id(1)
        blk = order_ref[j]
        pos = _position()
        x, y, c = pos

        @pl.when(i == 0)
        def _():
            acc[...] = jnp.zeros_like(acc)

        @pl.when(blk == 0)
        def _():
            acc[...] += _tn(h_ref[...], da_ref[...])

        @pl.when(blk > 0)
        def _():
            acc[...] += _tn(h_ref[...], db_ref[...])

        def pair(step, half):
            return pltpu.make_async_remote_copy(
                src_ref=sendbuf.at[step, pl.ds(half * hr, hr), :], dst_ref=pairbuf.at[step],
                send_sem=psend.at[step], recv_sem=precv.at[step], device_id=(x, y, 1 - c), device_id_type=MESH)

        def ici(step):
            flip = OWNER_FLIPS[step]
            return pltpu.make_async_remote_copy(
                src_ref=chipbuf.at[step], dst_ref=slots_ref.at[flip], send_sem=send.at[step], recv_sem=recv.at[step],
                device_id=_peer(pos, 4 * (flip >> 1) + 2 * (flip & 1)), device_id_type=MESH)

        local = pltpu.make_async_copy(chipbuf.at[N_CHIPS - 1], slots_ref.at[0], lsem.at[0])

        def combine(step):
            pair(step, c).wait_recv()
            mine = sendbuf[step, pl.ds(c * hr, hr), :].astype(F32)
            chipbuf[step] = (mine + pairbuf[step].astype(F32)).astype(BF16)

        for step in range(N_CHIPS):
            @pl.when((j == step) & (i == n - 1))
            def _(step=step):
                sendbuf[step] = acc[...].astype(BF16)
                pair(step, 1 - c).start()

        for step in range(N_CHIPS - 1):
            @pl.when((j == step + 1) & (i == settle))
            def _(step=step):
                combine(step)
                ici(step).start()

        @pl.when((j == N_CHIPS - 1) & (i == n - 1))
        def _():
            combine(N_CHIPS - 1)
            local.start()
            for step in range(N_CHIPS - 1):
                ici(step).wait_recv()
            for step in range(N_CHIPS - 1):
                ici(step).wait_send()
            for step in range(N_CHIPS):
                pair(step, 1 - c).wait_send()
            local.wait()

    return pl.pallas_call(
        body, name="in_proj_bwd_w",
        grid_spec=pltpu.PrefetchScalarGridSpec(
            num_scalar_prefetch=1, grid=(N_CHIPS, n),
            in_specs=[pl.BlockSpec((TM_BLK, d), lambda j, i, order: (i, 0)),
                      pl.BlockSpec((TM_BLK, tn), lambda j, i, order: (jnp.where(order[j] == 0, i, 0), 0)),
                      pl.BlockSpec((TM_BLK, tn), lambda j, i, order: (jnp.where(order[j] == 0, 0, i),
                                                                     jnp.maximum(order[j] - 1, 0)))],
            out_specs=[ANY],
            scratch_shapes=[pltpu.VMEM((d, tn), F32), pltpu.VMEM((N_CHIPS, d, tn), BF16),
                            pltpu.VMEM((N_CHIPS, hr, tn), BF16), pltpu.VMEM((N_CHIPS, hr, tn), BF16),
                            pltpu.SemaphoreType.DMA((N_CHIPS,)), pltpu.SemaphoreType.DMA((N_CHIPS,)),
                            pltpu.SemaphoreType.DMA((N_CHIPS - 1,)), pltpu.SemaphoreType.DMA((N_CHIPS - 1,)),
                            pltpu.SemaphoreType.DMA((1,))]),
        out_shape=[jax.ShapeDtypeStruct((N_CHIPS, hr, tn), BF16)],
        compiler_params=_params("arbitrary", "arbitrary"),
    )(order, h, dqkv, dpb)[0]


LOSS_ROW = 6


def _adam_update(w, g, m, v):
    c1 = 1.0 / (1.0 - ADAM_B1 ** ADAM_STEP)
    c2 = 1.0 / (1.0 - ADAM_B2 ** ADAM_STEP)
    m2 = ADAM_B1 * m + (1.0 - ADAM_B1) * g
    v2 = ADAM_B2 * v + (1.0 - ADAM_B2) * (g * g)
    return -ADAM_LR * ((m2 * c1) / (jnp.sqrt(v2 * c2) + ADAM_EPS) + ADAM_WD * w), m2, v2


def _adamw_small(recv, params, moments_m, moments_v):
    k = recv.shape[0]
    n_par = len(params)
    cshard = params[3].shape[1]

    def body(*refs):
        r_ref = refs[0]
        ws, ms, vs = refs[1:1 + n_par], refs[1 + n_par:1 + 2 * n_par], refs[1 + 2 * n_par:1 + 3 * n_par]
        loss_ref = refs[1 + 3 * n_par]
        outs = refs[2 + 3 * n_par:]
        total = r_ref[0]
        for slot in range(1, k):
            total = total + r_ref[slot]
        loss_ref[...] = jnp.sum(total[LOSS_ROW:LOSS_ROW + 1, :], axis=1, keepdims=True)
        chip = 2 * lax.axis_index("x") + lax.axis_index("y")
        g_cw = jnp.zeros((3, cshard), F32)
        for sh in range(N_CHIPS):
            g_cw = g_cw + jnp.where(chip == sh, total[3:6, sh * cshard:(sh + 1) * cshard], 0.0)
        grads = [total[0:1, :], total[1:2, :], total[2:3, :ws[2].shape[1]], g_cw, total[8:16, :ws[4].shape[1]]]
        for p in range(n_par):
            delta, m2, v2 = _adam_update(ws[p][...], grads[p], ms[p][...], vs[p][...])
            for q, val in enumerate((grads[p], delta, m2, v2)):
                outs[4 * p + q][...] = val

    shapes = [jax.ShapeDtypeStruct((1, 1), F32)]
    for p in params:
        shapes += [jax.ShapeDtypeStruct(p.shape, F32)] * 4
    return pl.pallas_call(body, name="adamw_small", out_shape=shapes)(recv, *params, *moments_m, *moments_v)


def _adamw(w, g, m, v, name, rows_per_step):
    r, c = w.shape

    def body(w_ref, g_ref, m_ref, v_ref, d_ref, mo_ref, vo_ref):
        d_ref[...], mo_ref[...], vo_ref[...] = _adam_update(w_ref[...], g_ref[...], m_ref[...], v_ref[...])

    spec = pl.BlockSpec((rows_per_step, c), lambda i: (i, 0))
    shape = jax.ShapeDtypeStruct((r, c), F32)
    return pl.pallas_call(
        body, name=name, grid=(r // rows_per_step,),
        in_specs=[spec] * 4, out_specs=[spec] * 3, out_shape=[shape] * 3,
        compiler_params=_params("parallel"),
    )(w, g, m, v)


ANY = pl.BlockSpec(memory_space=pl.ANY)
N_CHIPS = 4
N_DEV = 8
OWNER_FLIPS = (3, 1, 2, 0)


def _position():
    return lax.axis_index("x"), lax.axis_index("y"), lax.axis_index("c")


def _gather_out_shapes(shards, kinds, cw8):
    full = [(a.shape[0], a.shape[1] * N_CHIPS) if k == "cols" else (a.shape[0] * N_CHIPS, a.shape[1])
            for a, k in zip(shards, kinds)]
    return [jax.ShapeDtypeStruct(f, BF16) for f in full] + [
        jax.ShapeDtypeStruct((N_CHIPS,) + cw8.shape, cw8.dtype)]


def _gather_sems(nw):
    return [pltpu.SemaphoreType.DMA((3, nw)), pltpu.SemaphoreType.DMA((3, nw)),
            pltpu.SemaphoreType.DMA((3, nw)), pltpu.SemaphoreType.DMA((3, nw)),
            pltpu.SemaphoreType.DMA((3,)), pltpu.SemaphoreType.DMA((3,)), pltpu.SemaphoreType.DMA((nw + 1,))]


def _gather_plan(kinds, srcs, dsts, cw, cw_all, send1, recv1, send2, recv2, ssend, srecv, lsem):
    nw = len(srcs)
    x, y, c = _position()
    mine = 2 * x + y
    chips = [(x, 1 - y), (1 - x, y), (1 - x, 1 - y)]

    def window(w, shard, half):
        r, cc = srcs[w].shape
        hr = r // 2
        if kinds[w] == "cols":
            rows = pl.ds(0, r) if half is None else pl.ds(half * hr, hr)
            return dsts[w].at[rows, pl.ds(shard * cc, cc)]
        rows = pl.ds(shard * r, r) if half is None else pl.ds(shard * r + half * hr, hr)
        return dsts[w].at[rows, :]

    def my_half(w):
        hr = srcs[w].shape[0] // 2
        return srcs[w].at[pl.ds(c * hr, hr), :]

    def local():
        return [pltpu.make_async_copy(srcs[w], window(w, mine, None), lsem.at[w]) for w in range(nw)] + [
            pltpu.make_async_copy(cw, cw_all.at[mine], lsem.at[nw])]

    def ici(k, w, shard):
        kx, ky = chips[k]
        return pltpu.make_async_remote_copy(
            src_ref=my_half(w), dst_ref=window(w, shard, c), send_sem=send1.at[k, w], recv_sem=recv1.at[k, w],
            device_id=(kx, ky, c), device_id_type=MESH)

    def d2d(k, w, shard, half):
        return pltpu.make_async_remote_copy(
            src_ref=window(w, shard, half), dst_ref=window(w, shard, half),
            send_sem=send2.at[k, w], recv_sem=recv2.at[k, w], device_id=(x, y, 1 - c), device_id_type=MESH)

    def small(k, shard):
        kx, ky = chips[k]
        return pltpu.make_async_remote_copy(
            src_ref=cw, dst_ref=cw_all.at[shard], send_sem=ssend.at[k], recv_sem=srecv.at[k],
            device_id=(kx, ky, c), device_id_type=MESH)

    def theirs(k):
        kx, ky = chips[k]
        return 2 * kx + ky

    def start():
        for cp in local():
            cp.start()
        for k in range(3):
            for w in range(nw):
                ici(k, w, mine).start()
            small(k, mine).start()

    def forward():
        for k in range(3):
            for w in range(nw):
                ici(k, w, theirs(k)).wait_recv()
                d2d(k, w, theirs(k), c).start()

    def finish():
        for k in range(3):
            for w in range(nw):
                d2d(k, w, theirs(k), 1 - c).wait_recv()
            small(k, theirs(k)).wait_recv()
        for k in range(3):
            for w in range(nw):
                ici(k, w, mine).wait_send()
                d2d(k, w, theirs(k), c).wait_send()
            small(k, mine).wait_send()
        for cp in local():
            cp.wait()

    return start, forward, finish


def _scatter_out_shapes(parts):
    return [jax.ShapeDtypeStruct((N_DEV, p.shape[1] // 2, p.shape[2]), p.dtype) for p in parts]


def _scatter_sems(nw):
    return [pltpu.SemaphoreType.DMA((N_DEV, nw)), pltpu.SemaphoreType.DMA((N_DEV, nw)), pltpu.SemaphoreType.DMA((nw,))]


def _peer(pos, k):
    x, y, c = pos
    return ((1 - x) if k & 4 else x, (1 - y) if k & 2 else y, (1 - c) if k & 1 else c)


def _scatter_plan(srcs, dsts, send, recv, lsem):
    nw = len(srcs)
    pos = _position()

    def piece(w, k):
        px, py, pc = _peer(pos, k)
        hr = srcs[w].shape[1] // 2
        return srcs[w].at[2 * px + py, pl.ds(pc * hr, hr), :]

    def remote(w, k):
        return pltpu.make_async_remote_copy(
            src_ref=piece(w, k), dst_ref=dsts[w].at[k], send_sem=send.at[k, w], recv_sem=recv.at[k, w],
            device_id=_peer(pos, k), device_id_type=MESH)

    def local(w):
        return pltpu.make_async_copy(piece(w, 0), dsts[w].at[0], lsem.at[w])

    def start():
        for w in range(nw):
            local(w).start()
        for k in range(1, N_DEV):
            for w in range(nw):
                remote(w, k).start()

    def finish():
        for k in range(1, N_DEV):
            for w in range(nw):
                remote(w, k).wait_recv()
        for k in range(1, N_DEV):
            for w in range(nw):
                remote(w, k).wait_send()
        for w in range(nw):
            local(w).wait()

    return start, finish


def _reduce_pair(slots, small):
    nw = len(slots)

    def body(*refs):
        srcs, sm = refs[:nw], refs[nw]
        dsts, sm_all = refs[nw + 1:2 * nw + 1], refs[2 * nw + 1]
        halves = refs[2 * nw + 2:3 * nw + 2]
        send, recv, ssend, srecv, lsem = refs[3 * nw + 2:]
        pos = _position()
        x, y, c = pos
        me = 4 * x + 2 * y + c

        def rows(w, half):
            hr = halves[w].shape[0]
            return dsts[w].at[pl.ds(half * hr, hr), :]

        def remote(w, half):
            return pltpu.make_async_remote_copy(
                src_ref=halves[w], dst_ref=rows(w, half), send_sem=send.at[w], recv_sem=recv.at[w],
                device_id=(x, y, 1 - c), device_id_type=MESH)

        def bcast(k, slot):
            return pltpu.make_async_remote_copy(
                src_ref=sm, dst_ref=sm_all.at[slot], send_sem=ssend.at[k], recv_sem=srecv.at[k],
                device_id=_peer(pos, k), device_id_type=MESH)

        small_copies = [bcast(k, me) for k in range(1, N_DEV)]
        own_small = pltpu.make_async_copy(sm, sm_all.at[me], lsem.at[nw])
        for cp in small_copies + [own_small]:
            cp.start()
        big = []
        for w in range(nw):
            total = srcs[w][0].astype(F32)
            for k in range(1, srcs[w].shape[0]):
                total = total + srcs[w][k].astype(F32)
            halves[w][...] = total
            big += [remote(w, c), pltpu.make_async_copy(halves[w], rows(w, c), lsem.at[w])]
            big[-2].start()
            big[-1].start()
        for w in range(nw):
            remote(w, 1 - c).wait_recv()
        for k in range(1, N_DEV):
            px, py, pc = _peer(pos, k)
            bcast(k, 4 * px + 2 * py + pc).wait_recv()
        for w in range(nw):
            big[2 * w].wait_send()
            big[2 * w + 1].wait()
        for cp in small_copies:
            cp.wait_send()
        own_small.wait()

    vmem = pl.BlockSpec(memory_space=pltpu.VMEM)
    half_shapes = [(sl.shape[1], sl.shape[2]) for sl in slots]
    return pl.pallas_call(
        body, name="reduce_pair",
        in_specs=[vmem] * (nw + 1), out_specs=[ANY] * (nw + 1),
        out_shape=[jax.ShapeDtypeStruct((2 * r, cc), F32) for r, cc in half_shapes]
        + [jax.ShapeDtypeStruct((N_DEV,) + small.shape, small.dtype)],
        scratch_shapes=[pltpu.VMEM(hs, F32) for hs in half_shapes]
        + [pltpu.SemaphoreType.DMA((nw,)), pltpu.SemaphoreType.DMA((nw,)),
           pltpu.SemaphoreType.DMA((N_DEV,)), pltpu.SemaphoreType.DMA((N_DEV,)),
           pltpu.SemaphoreType.DMA((nw + 1,))],
        compiler_params=pltpu.CompilerParams(vmem_limit_bytes=VMEM_LIMIT),
    )(*slots, small)


def _pad_to(a, rows, cols):
    return jnp.pad(a, ((0, rows - a.shape[0]), (0, cols - a.shape[1])))


def _pack_small(norm_g, fin_g, conv_b, conv_w, loss_vec, rel):
    rows = [_pad_to(norm_g, 1, SMALL_COLS), _pad_to(fin_g, 1, SMALL_COLS), _pad_to(conv_b, 1, SMALL_COLS),
            _pad_to(conv_w, 3, SMALL_COLS), _pad_to(loss_vec, 2, SMALL_COLS), _pad_to(rel, HEADS, SMALL_COLS)]
    return jnp.concatenate(rows, axis=0)


def kernel(x, norm_g, w_in, rel_bias, w_att_out, conv_w, conv_b, w_conv_out, w_out, final_norm_g, loss_target, m_norm_g, m_w_in, m_rel_bias, m_w_att_out, m_conv_w, m_conv_b, m_w_conv_out, m_w_out, m_final_norm_g, v_norm_g, v_w_in, v_rel_bias, v_w_att_out, v_conv_w, v_conv_b, v_w_conv_out, v_w_out, v_final_norm_g):
    xs, tgt = x[0], loss_target[0]
    cshard = conv_w.shape[2]
    chip = 2 * lax.axis_index("x") + lax.axis_index("y")

    shards = [w_in[0], w_att_out[0], w_conv_out[0], w_out[0]]
    cw8 = _pad_to(conv_w[0], SUBLANES, cshard)
    flips = jnp.arange(N_CHIPS, dtype=jnp.int32)
    own_first = jnp.bitwise_xor(chip, flips)
    own_last = jnp.bitwise_xor(chip, jnp.asarray(OWNER_FLIPS, jnp.int32))

    proj, h, wb_in = _in_proj_gather(xs, norm_g, shards[0], own_first)
    diag = jnp.take(rel_bias[0], _diag_rel_index(), axis=1)
    att, lse, wb_att, wb_conv, wb_out, cw_all = _attn_fwd(diag, proj, shards[1:], ["cols", "cols", "rows"], cw8)
    conv_w_full = jnp.transpose(cw_all, (1, 0, 2)).reshape(SUBLANES, N_CHIPS * cshard)
    (dpb, d_att, dx2, g_att_p, g_conv_p, g_out_p, loss_vec, g_fin, g_cb, g_cw) = _mixer_mid(
        att, proj, xs, tgt, wb_att, wb_conv, wb_out, conv_w_full, conv_b, final_norm_g[None, :])
    dqkv, g_rel, r_att, r_conv, r_out = _attn_bwd(diag, proj, d_att, att, lse, [g_att_p, g_conv_p, g_out_p])
    grad_x, g_norm = _in_proj_bwd_x(dqkv, dpb, wb_in, xs, dx2, norm_g)
    r_in = _in_proj_bwd_w(h, dqkv, dpb, own_last)

    small = _pack_small(g_norm, g_fin, g_cb, g_cw[0:3], loss_vec, g_rel)
    gw_in, gw_att, gw_conv, gw_out, r_small = _reduce_pair([r_in, r_att, r_conv, r_out], small)
    small_out = _adamw_small(
        r_small,
        [norm_g, final_norm_g[None, :], conv_b, conv_w[0], rel_bias[0]],
        [m_norm_g, m_final_norm_g[None, :], m_conv_b, m_conv_w[0], m_rel_bias[0]],
        [v_norm_g, v_final_norm_g[None, :], v_conv_b, v_conv_w[0], v_rel_bias[0]])
    loss = small_out[0][0, 0]
    small_names = ["norm_g", "final_norm_g", "conv_b", "conv_w", "rel_bias"]
    fix = {"norm_g": lambda a: a, "final_norm_g": lambda a: a[0], "conv_b": lambda a: a,
           "conv_w": lambda a: a[None], "rel_bias": lambda a: a[None]}
    small_res = {name: [fix[name](small_out[1 + 4 * p + q]) for q in range(4)] for p, name in enumerate(small_names)}

    big = {}
    for name, w, g, m, v, rows in (("w_in", w_in, gw_in, m_w_in, v_w_in, 128),
                                   ("w_att_out", w_att_out, gw_att, m_w_att_out, v_w_att_out, 256),
                                   ("w_conv_out", w_conv_out, gw_conv, m_w_conv_out, v_w_conv_out, 256),
                                   ("w_out", w_out, gw_out, m_w_out, v_w_out, 128)):
        dw, mw, vw = _adamw(w[0], g, m[0], v[0], "adamw_" + name, rows)
        big[name] = (g[None], dw[None], mw[None], vw[None])

    order = ["norm_g", "w_in", "rel_bias", "w_att_out", "conv_w", "conv_b", "w_conv_out", "w_out", "final_norm_g"]
    outs = [loss, grad_x[None]]
    for which in range(4):
        for name in order:
            outs.append(big[name][which] if name in big else small_res[name][which])
    return tuple(outs)
```

```python
import numpy as np
import jax
import jax.numpy as jnp
from jax import lax
from jax.experimental import pallas as pl
from jax.experimental.pallas import tpu as pltpu

F32 = jnp.float32
BF16 = jnp.bfloat16
MESH = pl.DeviceIdType.MESH

CHUNK = 64
N_LEFT = 8
HEADS = 8
HEAD_DIM = 64
D_ATT = HEADS * HEAD_DIM
MAX_REL = 128
N_REL = 2 * MAX_REL + 1
EPS = 1e-6
NEG_BIG = -1e30
ADAM_LR, ADAM_B1, ADAM_B2, ADAM_EPS, ADAM_WD, ADAM_STEP = 0.001, 0.9, 0.999, 1e-08, 0.01, 10

LANES = 128
SUBLANES = 8
VMEM_LIMIT = 56 * 1024 * 1024

QB = 2 * CHUNK
KW = N_LEFT * CHUNK + QB
DIAG = KW + QB
TQ = N_LEFT * CHUNK
TM_MID = 256
TM_MM = 512
TM_BLK = 1024
SMALL_ROWS, SMALL_COLS = 16, 1024


def _params(*sem):
    return pltpu.CompilerParams(dimension_semantics=sem, vmem_limit_bytes=VMEM_LIMIT)


def _nt(a, b):
    return lax.dot_general(a, b, (((1,), (1,)), ((), ())), preferred_element_type=F32)


def _tn(a, b):
    return lax.dot_general(a, b, (((0,), (0,)), ((), ())), preferred_element_type=F32)


def _nn(a, b):
    return jnp.dot(a, b, preferred_element_type=F32)


def _diag_rel_index():
    d = np.arange(DIAG)
    diff = np.where(d < KW, d, d - DIAG)
    rel = N_LEFT * CHUNK - diff
    return np.clip(rel, -MAX_REL, MAX_REL) + MAX_REL


def _build_bias(diag_ref, bias_scr):
    r = lax.broadcasted_iota(jnp.int32, (QB, KW), 0) // CHUNK
    s = lax.broadcasted_iota(jnp.int32, (QB, KW), 1) // CHUNK
    allowed = (s >= r) & (s <= r + N_LEFT)
    for h in range(HEADS):
        row = jnp.broadcast_to(diag_ref[h:h + 1, :], (QB, DIAG))
        t = pltpu.roll(row, 0, 1, stride=1, stride_axis=0)
        bias_scr[h // 2, (h % 2) * QB:(h % 2 + 1) * QB, :] = jnp.where(allowed, t[:, :KW], NEG_BIG)


def _stack_heads(a, lane_hi):
    zero = jnp.zeros_like(a)
    return jnp.concatenate([jnp.where(lane_hi, zero, a), jnp.where(lane_hi, a, zero)], axis=0)


def _in_proj_gather(x, g, shard, order):
    s, d = x.shape
    tn = shard.shape[1]
    n = s // TM_BLK
    hr = d // 2

    def body(order_ref, x_ref, g_ref, shard32_ref, proj_ref, h_ref, wfull_ref, shard_ref, hbuf, wbuf,
             send1, recv1, send2, recv2, lsem):
        del order_ref
        j, i = pl.program_id(0), pl.program_id(1)
        x, y, c = _position()
        mine = 2 * x + y
        chips = [(x, 1 - y), (1 - x, y), (1 - x, 1 - y)]

        def theirs(k):
            return 2 * chips[k][0] + chips[k][1]

        def half_rows(half):
            return pl.ds(half * hr, hr)

        def ici(k, shard_index):
            return pltpu.make_async_remote_copy(
                src_ref=shard_ref.at[half_rows(c), :], dst_ref=wfull_ref.at[shard_index, half_rows(c), :],
                send_sem=send1.at[k], recv_sem=recv1.at[k], device_id=(*chips[k], c), device_id_type=MESH)

        def d2d(k, half):
            return pltpu.make_async_remote_copy(
                src_ref=wbuf.at[k % 2, half_rows(half), :], dst_ref=wfull_ref.at[theirs(k), half_rows(half), :],
                send_sem=send2.at[k], recv_sem=recv2.at[k], device_id=(x, y, 1 - c), device_id_type=MESH)

        def load(k, half, sem):
            return pltpu.make_async_copy(wfull_ref.at[theirs(k), half_rows(half), :],
                                         wbuf.at[k % 2, half_rows(half), :], lsem.at[sem])

        own = pltpu.make_async_copy(shard_ref, wfull_ref.at[mine], lsem.at[0])

        def put_proj(block):
            for grp in range(tn // D_ATT):
                proj_ref[grp] = block[:, grp * D_ATT:(grp + 1) * D_ATT].astype(BF16)

        @pl.when((j == 0) & (i == 0))
        def _():
            shard_ref[...] = shard32_ref[...].astype(BF16)
            own.start()
            ici(0, mine).start()
            ici(1, mine).start()

        for k in range(3):
            first = max(n - 3, 0) if k == 0 else min(n // 2, n - 1)

            @pl.when((j == k) & (i == first))
            def _(k=k):
                if k == 0:
                    ici(0, mine).wait_send()
                    ici(1, mine).wait_send()
                    ici(2, mine).start()
                if k == 2:
                    d2d(0, c).wait_send()
                ici(k, theirs(k)).wait_recv()
                load(k, c, 1).start()

            @pl.when((j == k) & (i == min(first + 1, n - 1)))
            def _(k=k):
                load(k, c, 1).wait()
                d2d(k, c).start()

            @pl.when((j == k) & (i == min(first + 2, n - 1)))
            def _(k=k):
                d2d(k, 1 - c).wait_recv()
                load(k, 1 - c, 2).start()

            @pl.when((j == k + 1) & (i == 0))
            def _(k=k):
                load(k, 1 - c, 2).wait()

        @pl.when(j == 0)
        def _():
            xv = x_ref[...]
            r = lax.rsqrt(jnp.mean(xv * xv, axis=-1, keepdims=True) + EPS)
            hv = ((xv * r) * g_ref[...]).astype(BF16)
            hbuf[i] = hv
            h_ref[...] = hv
            put_proj(_nn(hv, shard_ref[...]))

        for k in range(3):
            @pl.when(j == k + 1)
            def _(k=k):
                put_proj(_nn(hbuf[i], wbuf[k % 2]))

        @pl.when((j == 3) & (i == n - 1))
        def _():
            ici(2, mine).wait_send()
            d2d(1, c).wait_send()
            d2d(2, c).wait_send()
            own.wait()

    return pl.pallas_call(
        body, name="in_proj_gather",
        grid_spec=pltpu.PrefetchScalarGridSpec(
            num_scalar_prefetch=1, grid=(N_CHIPS, n),
            in_specs=[pl.BlockSpec((TM_BLK, d), lambda j, i, order: (jnp.where(j == 0, i, n - 1), 0)),
                      pl.BlockSpec((1, d), lambda j, i, order: (0, 0)), pl.BlockSpec(memory_space=pltpu.VMEM)],
            out_specs=[pl.BlockSpec((tn // D_ATT, TM_BLK, D_ATT), lambda j, i, order: (order[j], i, 0)),
                       pl.BlockSpec((TM_BLK, d), lambda j, i, order: (jnp.where(j == 0, i, n - 1), 0)), ANY],
            scratch_shapes=[pltpu.VMEM((d, tn), BF16), pltpu.VMEM((n, TM_BLK, d), BF16), pltpu.VMEM((2, d, tn), BF16),
                            pltpu.SemaphoreType.DMA((3,)), pltpu.SemaphoreType.DMA((3,)),
                            pltpu.SemaphoreType.DMA((3,)), pltpu.SemaphoreType.DMA((3,)), pltpu.SemaphoreType.DMA((3,))]),
        out_shape=[jax.ShapeDtypeStruct((N_CHIPS * tn // D_ATT, s, D_ATT), BF16), jax.ShapeDtypeStruct((s, d), BF16),
                   jax.ShapeDtypeStruct((N_CHIPS, d, tn), BF16)],
        compiler_params=_params("arbitrary", "arbitrary"),
    )(order, x, g, shard)


def _attn_fwd(diag, proj, shards, kinds, cw8):
    s = proj.shape[1]
    n = s // TQ
    nw = len(shards)
    scale = HEAD_DIM ** -0.5

    def body(*refs):
        diag_ref, q_ref, kp_ref, kc_ref, vp_ref, vc_ref = refs[:6]
        srcs, cw = refs[6:6 + nw], refs[6 + nw]
        o_ref, lse_ref = refs[7 + nw:9 + nw]
        dsts, cw_all = refs[9 + nw:9 + 2 * nw], refs[9 + 2 * nw]
        bias_scr = refs[10 + 2 * nw]
        casts = refs[11 + 2 * nw:11 + 3 * nw]
        start, forward, finish = _gather_plan(kinds, casts, dsts, cw, cw_all, *refs[11 + 3 * nw:])
        i = pl.program_id(0)

        @pl.when(i == 0)
        def _():
            for w in range(nw):
                casts[w][...] = srcs[w][...].astype(BF16)
            start()
            _build_bias(diag_ref, bias_scr)

        @pl.when(i == n // 2)
        def _():
            forward()

        @pl.when(i == n - 1)
        def _():
            finish()

        lane_hi = lax.broadcasted_iota(jnp.int32, (QB, LANES), 1) >= HEAD_DIM

        def block(b, first_tile):
            r0, n_prev = b * QB, TQ - b * QB
            n_cur = KW - n_prev
            for p in range(HEADS // 2):
                lanes = slice(LANES * p, LANES * (p + 1))
                q2 = _stack_heads(q_ref[r0:r0 + QB, lanes] * scale, lane_hi)
                s_cur = _nt(q2, kc_ref[0:n_cur, lanes]) + bias_scr[p, :, n_prev:KW]
                if first_tile:
                    sc = s_cur
                else:
                    sc = jnp.concatenate([_nt(q2, kp_ref[r0:TQ, lanes]) + bias_scr[p, :, 0:n_prev], s_cur], axis=1)
                m = jnp.max(sc, axis=1, keepdims=True)
                pe = jnp.exp(sc - m)
                l = jnp.sum(pe, axis=1, keepdims=True)
                pb = pe.astype(BF16)
                if first_tile:
                    o2 = _nn(pb, vc_ref[0:n_cur, lanes]) / l
                else:
                    o2 = (_nn(pb[:, 0:n_prev], vp_ref[r0:TQ, lanes]) + _nn(pb[:, n_prev:KW], vc_ref[0:n_cur, lanes])) / l
                lse2 = m + jnp.log(l)
                lse_ref[r0:r0 + QB, 2 * p:2 * p + 1] = lse2[0:QB, :]
                lse_ref[r0:r0 + QB, 2 * p + 1:2 * p + 2] = lse2[QB:2 * QB, :]
                o_ref[r0:r0 + QB, lanes] = jnp.where(lane_hi, o2[QB:2 * QB, :], o2[0:QB, :]).astype(BF16)

        @pl.when(i == 0)
        def _():
            for b in range(TQ // QB):
                block(b, True)

        @pl.when(i > 0)
        def _():
            for b in range(TQ // QB):
                block(b, False)

    blk = lambda grp, prev: pl.BlockSpec(
        (None, TQ, D_ATT), (lambda i: (grp, jnp.maximum(i - 1, 0), 0)) if prev else (lambda i: (grp, i, 0)))
    vmem = pl.BlockSpec(memory_space=pltpu.VMEM)
    return pl.pallas_call(
        body, name="attn_fwd", grid=(n,),
        in_specs=[pl.BlockSpec((HEADS, DIAG), lambda i: (0, 0)),
                  blk(0, False), blk(1, True), blk(1, False), blk(2, True), blk(2, False)] + [vmem] * (nw + 1),
        out_specs=[pl.BlockSpec((TQ, D_ATT), lambda i: (i, 0)), pl.BlockSpec((TQ, HEADS), lambda i: (i, 0))]
        + [ANY] * (nw + 1),
        out_shape=[jax.ShapeDtypeStruct((s, D_ATT), BF16), jax.ShapeDtypeStruct((s, HEADS), F32)]
        + _gather_out_shapes(shards, kinds, cw8),
        scratch_shapes=[pltpu.VMEM((HEADS // 2, 2 * QB, KW), F32)] + [pltpu.VMEM(a.shape, BF16) for a in shards]
        + _gather_sems(nw),
        compiler_params=_params("arbitrary"),
    )(diag, proj, proj, proj, proj, proj, *shards, cw8)


def _attn_bwd(diag, proj, d_att, att, lse, parts):
    s = proj.shape[1]
    n = s // TQ
    npart = len(parts)
    scale = HEAD_DIM ** -0.5
    rel_pad = 3 * LANES

    def body(*refs):
        diag_ref, q_ref, kp_ref, kc_ref, vp_ref, vc_ref, do_ref, o_ref, lse_ref = refs[:9]
        part_refs = refs[9:9 + npart]
        dqkv_ref, dbias_ref = refs[9 + npart:11 + npart]
        slot_refs = refs[11 + npart:11 + 2 * npart]
        bias_scr, dbias_acc, dk_acc, dv_acc, dq_scr = refs[11 + 2 * npart:16 + 2 * npart]
        start, finish = _scatter_plan(part_refs, slot_refs, *refs[16 + 2 * npart:])
        i = pl.program_id(0)
        cur, prv = i % 2, 1 - i % 2

        @pl.when(i == 0)
        def _():
            start()
            _build_bias(diag_ref, bias_scr)
            dbias_acc[...] = jnp.zeros_like(dbias_acc)
            dk_acc[...] = jnp.zeros_like(dk_acc)
            dv_acc[...] = jnp.zeros_like(dv_acc)

        @pl.when(i > 0)
        def _():
            dqkv_ref[:, 0:D_ATT] = dq_scr[...]
            dk_acc[cur] = jnp.zeros((TQ, D_ATT), F32)
            dv_acc[cur] = jnp.zeros((TQ, D_ATT), F32)

        lane_hi = lax.broadcasted_iota(jnp.int32, (QB, LANES), 1) >= HEAD_DIM
        col = lax.broadcasted_iota(jnp.int32, (2 * QB, KW), 1)

        def make_block(first_tile):
            def block(b):
                r0, n_prev = b * QB, TQ - b * QB
                n_cur = KW - n_prev
                for p in range(HEADS // 2):
                    lanes = slice(LANES * p, LANES * (p + 1))
                    q2 = _stack_heads(q_ref[r0:r0 + QB, lanes] * scale, lane_hi)
                    kw = jnp.concatenate([kp_ref[r0:TQ, lanes], kc_ref[0:n_cur, lanes]], axis=0)
                    vw = jnp.concatenate([vp_ref[r0:TQ, lanes], vc_ref[0:n_cur, lanes]], axis=0)
                    dop = do_ref[r0:r0 + QB, lanes]
                    do2 = _stack_heads(dop, lane_hi)
                    prod = dop.astype(F32) * o_ref[r0:r0 + QB, lanes].astype(F32)
                    delta2 = jnp.concatenate(
                        [jnp.sum(jnp.where(lane_hi, 0.0, prod), axis=1, keepdims=True),
                         jnp.sum(jnp.where(lane_hi, prod, 0.0), axis=1, keepdims=True)], axis=0)
                    lse2 = jnp.concatenate([lse_ref[r0:r0 + QB, 2 * p:2 * p + 1],
                                            lse_ref[r0:r0 + QB, 2 * p + 1:2 * p + 2]], axis=0)
                    sc = _nt(q2, kw) + bias_scr[p]
                    if first_tile:
                        sc = jnp.where(col >= TQ - r0, sc, NEG_BIG)
                    pr = jnp.exp(sc - lse2)
                    ds = pr * (_nt(do2, vw) - delta2)
                    dbias_acc[p] += ds
                    dsb = ds.astype(BF16)
                    dv_w = _tn(pr.astype(BF16), do2)
                    dk_w = _tn(dsb, q2)
                    dv_acc[prv, r0:TQ, lanes] += dv_w[0:n_prev, :]
                    dv_acc[cur, 0:n_cur, lanes] += dv_w[n_prev:KW, :]
                    dk_acc[prv, r0:TQ, lanes] += dk_w[0:n_prev, :]
                    dk_acc[cur, 0:n_cur, lanes] += dk_w[n_prev:KW, :]
                    dq2 = _nn(dsb, kw)
                    dq = jnp.where(lane_hi, dq2[QB:2 * QB, :], dq2[0:QB, :]) * scale
                    dq_scr[r0:r0 + QB, lanes] = dq.astype(BF16)
            return block

        @pl.when(i == 0)
        def _():
            for b in range(TQ // QB):
                make_block(True)(b)

        @pl.when((i > 0) & (i < n))
        def _():
            for b in range(TQ // QB):
                make_block(False)(b)

        @pl.when(i > 0)
        def _():
            dqkv_ref[:, D_ATT:2 * D_ATT] = dk_acc[prv].astype(BF16)
            dqkv_ref[:, 2 * D_ATT:3 * D_ATT] = dv_acc[prv].astype(BF16)

        @pl.when(i == n)
        def _():
            d_iota = lax.broadcasted_iota(jnp.int32, (DIAG, rel_pad), 0)
            n_iota = lax.broadcasted_iota(jnp.int32, (DIAG, rel_pad), 1)
            diff = jnp.where(d_iota < KW, d_iota, d_iota - DIAG)
            idx = jnp.clip(N_LEFT * CHUNK - diff, -MAX_REL, MAX_REL) + MAX_REL
            onehot = (idx == n_iota).astype(F32)
            rows = []
            for hd in range(HEADS):
                acc = dbias_acc[hd // 2, (hd % 2) * QB:(hd % 2 + 1) * QB, :]
                a = jnp.concatenate([acc, jnp.zeros((QB, DIAG - KW), F32)], axis=1)
                g8 = a[0:SUBLANES, :]
                for blk in range(1, QB // SUBLANES):
                    g8 = g8 + pltpu.roll(a[blk * SUBLANES:(blk + 1) * SUBLANES, :], DIAG - blk * SUBLANES, 1)
                g1 = g8[0:1, :]
                for r in range(1, SUBLANES):
                    g1 = g1 + pltpu.roll(g8[r:r + 1, :], DIAG - r, 1)
                rows.append(g1)
            g = jnp.concatenate(rows, axis=0)
            dbias_ref[...] = jnp.dot(g, onehot, preferred_element_type=F32, precision=lax.Precision.HIGHEST)
            finish()

    last = n - 1
    cur = lambda grp: pl.BlockSpec((None, TQ, D_ATT), lambda i: (grp, jnp.minimum(i, last), 0))
    prev = lambda grp: pl.BlockSpec((None, TQ, D_ATT), lambda i: (grp, jnp.maximum(jnp.minimum(i, last) - 1, 0), 0))
    tile = pl.BlockSpec((TQ, D_ATT), lambda i: (jnp.minimum(i, last), 0))
    return pl.pallas_call(
        body, name="attn_bwd", grid=(n + 1,),
        in_specs=[pl.BlockSpec((HEADS, DIAG), lambda i: (0, 0)),
                  cur(0), prev(1), cur(1), prev(2), cur(2), tile, tile,
                  pl.BlockSpec((TQ, HEADS), lambda i: (jnp.minimum(i, last), 0))] + [ANY] * npart,
        out_specs=[pl.BlockSpec((TQ, 3 * D_ATT), lambda i: (jnp.maximum(i - 1, 0), 0)),
                   pl.BlockSpec((HEADS, rel_pad), lambda i: (0, 0))] + [ANY] * npart,
        out_shape=[jax.ShapeDtypeStruct((s, 3 * D_ATT), BF16), jax.ShapeDtypeStruct((HEADS, rel_pad), F32)]
        + _scatter_out_shapes(parts),
        scratch_shapes=[pltpu.VMEM((HEADS // 2, 2 * QB, KW), F32), pltpu.VMEM((HEADS // 2, 2 * QB, KW), F32),
                        pltpu.VMEM((2, TQ, D_ATT), F32), pltpu.VMEM((2, TQ, D_ATT), F32),
                        pltpu.VMEM((TQ, D_ATT), BF16)] + _scatter_sems(npart),
        compiler_params=_params("arbitrary"),
    )(diag, proj, proj, proj, proj, proj, d_att, att, lse, *parts)


def _shift_down(a, k, halo):
    rolled = pltpu.roll(a, k, 0)
    row = lax.broadcasted_iota(jnp.int32, halo.shape, 0)
    first = jnp.where(row < k, pltpu.roll(halo, k, 0), rolled[0:SUBLANES, :])
    return jnp.concatenate([first, rolled[SUBLANES:, :]], axis=0)


def _shift_up(a, k, nxt):
    tm = a.shape[0]
    rolled = pltpu.roll(a, tm - k, 0)
    row = lax.broadcasted_iota(jnp.int32, nxt.shape, 0)
    last = jnp.where(row >= SUBLANES - k, pltpu.roll(nxt, SUBLANES - k, 0), rolled[tm - SUBLANES:, :])
    return jnp.concatenate([rolled[:tm - SUBLANES, :], last], axis=0)


def _sigmoid(v):
    return 0.5 * jnp.tanh(0.5 * v) + 0.5


def _mixer_mid(att, proj, x, tgt, w_att, w_conv, w_out, conv_w8, conv_b, fin_g):
    s, d = x.shape
    dc = D_ATT
    n = s // TM_MID
    tm = TM_MID
    n_shards = 4

    def body(att_ref, za_ref, gb_ref, gc_ref, u_ref, zc_ref, hgc_ref, hu_ref, gatt_ref, gconv_ref, x_ref, t_ref,
             watt_ref, wconv_ref, wout_ref, cw_ref, cb_ref, fg_ref,
             dpb_ref, do_ref, dx2_ref, gatt_o, gconv_o, gout_o, loss_o, gfn_o, gcb_o, gcw_o,
             acc_att, acc_conv, acc_out, carry):
        i = pl.program_id(0)
        tile = n - 1 - i

        @pl.when(i == 0)
        def _():
            acc_att[...] = jnp.zeros_like(acc_att)
            acc_conv[...] = jnp.zeros_like(acc_conv)
            acc_out[...] = jnp.zeros_like(acc_out)
            carry[...] = jnp.zeros_like(carry)
            loss_o[...] = jnp.zeros_like(loss_o)
            gfn_o[...] = jnp.zeros_like(gfn_o)
            gcb_o[...] = jnp.zeros_like(gcb_o)
            gcw_o[...] = jnp.zeros_like(gcw_o)

        att_v = att_ref[...].astype(F32)
        za = za_ref[...].astype(F32)
        sa = _sigmoid(za)
        silu_a = za * sa
        a_b = (att_v * silu_a).astype(BF16)

        gb = gb_ref[...].astype(F32)
        gc = gc_ref[...].astype(F32)
        u = u_ref[...].astype(F32)
        zc = zc_ref[...].astype(F32)
        cu = gc * u
        halo = jnp.where(tile > 0, hgc_ref[...].astype(F32) * hu_ref[...].astype(F32), 0.0)
        cu1 = _shift_down(cu, 1, halo)
        cu2 = _shift_down(cu, 2, halo)
        w0, w1, w2 = cw_ref[0:1, :], cw_ref[1:2, :], cw_ref[2:3, :]
        vconv = w0 * cu2 + w1 * cu1 + w2 * cu + cb_ref[...]
        sc = _sigmoid(zc)
        silu_c = zc * sc
        c_b = (gb * vconv * silu_c).astype(BF16)

        y_att = _nn(a_b, watt_ref[...])
        y_conv = _nn(c_b, wconv_ref[...])
        ga = _sigmoid(jnp.concatenate([gatt_ref[0], gatt_ref[1]], axis=1).astype(F32))
        gv = _sigmoid(jnp.concatenate([gconv_ref[0], gconv_ref[1]], axis=1).astype(F32))
        m_b = (ga * y_att + gv * y_conv).astype(BF16)
        x2 = x_ref[...] + _nn(m_b, wout_ref[...])
        r2 = lax.rsqrt(jnp.mean(x2 * x2, axis=-1, keepdims=True) + EPS)
        x2n = x2 * r2
        fg = fg_ref[...]
        err = x2n * fg - t_ref[...]
        loss_o[...] += jnp.sum(err * err, axis=0, keepdims=True) * (0.5 / d)
        dy = err * (1.0 / d)
        gfn_o[...] += jnp.sum(dy * x2n, axis=0, keepdims=True)
        dyn = dy * fg
        dx2 = r2 * (dyn - x2n * jnp.mean(dyn * x2n, axis=-1, keepdims=True))
        dx2_ref[...] = dx2
        dx2_b = dx2.astype(BF16)

        dm = _nt(dx2_b, wout_ref[...])
        acc_out[...] += _tn(m_b, dx2_b)
        dy_att = dm * ga
        dy_conv = dm * gv
        dpb_ref[:, 5 * dc:5 * dc + d] = (dy_att * y_att * (1.0 - ga)).astype(BF16)
        dpb_ref[:, 5 * dc + d:5 * dc + 2 * d] = (dy_conv * y_conv * (1.0 - gv)).astype(BF16)
        dya_b = dy_att.astype(BF16)
        dyc_b = dy_conv.astype(BF16)
        da_in = _nt(dya_b, watt_ref[...])
        acc_att[...] += _tn(a_b, dya_b)
        dc_in = _nt(dyc_b, wconv_ref[...])
        acc_conv[...] += _tn(c_b, dyc_b)

        do_ref[...] = (da_in * silu_a).astype(BF16)
        dpb_ref[:, 0:dc] = (da_in * att_v * (sa * (1.0 + za * (1.0 - sa)))).astype(BF16)
        dpb_ref[:, dc:2 * dc] = (dc_in * vconv * silu_c).astype(BF16)
        dgs = dc_in * gb
        dvc = dgs * silu_c
        dpb_ref[:, 4 * dc:5 * dc] = (dgs * vconv * (sc * (1.0 + zc * (1.0 - sc)))).astype(BF16)
        gcb_o[...] += jnp.sum(dvc, axis=0, keepdims=True)
        gcw_o[0:1, :] += jnp.sum(dvc * cu2, axis=0, keepdims=True)
        gcw_o[1:2, :] += jnp.sum(dvc * cu1, axis=0, keepdims=True)
        gcw_o[2:3, :] += jnp.sum(dvc * cu, axis=0, keepdims=True)
        nxt = carry[...]
        dcu = w2 * dvc + w1 * _shift_up(dvc, 1, nxt) + w0 * _shift_up(dvc, 2, nxt)
        carry[...] = dvc[0:SUBLANES, :]
        dpb_ref[:, 2 * dc:3 * dc] = (dcu * u).astype(BF16)
        dpb_ref[:, 3 * dc:4 * dc] = (dcu * gc).astype(BF16)

        @pl.when(i == n - 1)
        def _():
            for j in range(n_shards):
                gatt_o[j] = acc_att[:, j * (d // n_shards):(j + 1) * (d // n_shards)].astype(BF16)
                gconv_o[j] = acc_conv[:, j * (d // n_shards):(j + 1) * (d // n_shards)].astype(BF16)
                gout_o[j] = acc_out[j * (d // n_shards):(j + 1) * (d // n_shards), :].astype(BF16)

    rev = lambda width, col_blk: pl.BlockSpec((tm, width), lambda i: (n - 1 - i, col_blk))
    grp = lambda g: pl.BlockSpec((None, tm, dc), lambda i: (g, n - 1 - i, 0))
    grp2 = lambda g2: pl.BlockSpec((2, tm, dc), lambda i: (g2, n - 1 - i, 0))
    halo_spec = lambda g: pl.BlockSpec(
        (None, SUBLANES, dc), lambda i: (g, jnp.maximum((n - 1 - i) * (tm // SUBLANES) - 1, 0), 0))
    const = lambda shape: pl.BlockSpec(shape, lambda i: tuple(0 for _ in shape), pipeline_mode=pl.Buffered(1))
    q4 = d // n_shards
    return pl.pallas_call(
        body, name="mixer_mid", grid=(n,),
        in_specs=[rev(dc, 0), grp(3), grp(4), grp(5), grp(6), grp(7),
                  halo_spec(5), halo_spec(6), grp2(4), grp2(5), rev(d, 0), rev(d, 0),
                  const((dc, d)), const((dc, d)), const((d, d)), const((SUBLANES, dc)), const((1, dc)), const((1, d))],
        out_specs=[rev(5 * dc + 2 * d, 0), rev(dc, 0), rev(d, 0),
                   const((n_shards, dc, q4)), const((n_shards, dc, q4)), const((n_shards, q4, d)),
                   const((1, d)), const((1, d)), const((1, dc)), const((SUBLANES, dc))],
        out_shape=[jax.ShapeDtypeStruct((s, 5 * dc + 2 * d), BF16), jax.ShapeDtypeStruct((s, dc), BF16),
                   jax.ShapeDtypeStruct((s, d), F32),
                   jax.ShapeDtypeStruct((n_shards, dc, q4), BF16), jax.ShapeDtypeStruct((n_shards, dc, q4), BF16),
                   jax.ShapeDtypeStruct((n_shards, q4, d), BF16),
                   jax.ShapeDtypeStruct((1, d), F32), jax.ShapeDtypeStruct((1, d), F32),
                   jax.ShapeDtypeStruct((1, dc), F32), jax.ShapeDtypeStruct((SUBLANES, dc), F32)],
        scratch_shapes=[pltpu.VMEM((dc, d), F32), pltpu.VMEM((dc, d), F32), pltpu.VMEM((d, d), F32),
                        pltpu.VMEM((SUBLANES, dc), F32)],
        compiler_params=_params("arbitrary"),
    )(att, proj, proj, proj, proj, proj, proj, proj, proj, proj, x, tgt,
      w_att, w_conv, w_out, conv_w8, conv_b, fin_g)


def _in_proj_bwd_x(dqkv, dpb, w_in, x, dx2, g):
    s, d = x.shape
    tn = dqkv.shape[1]
    nb = dpb.shape[1] // tn
    n = s // TM_MM

    def body(*refs):
        dps, ws = refs[:nb + 1], refs[nb + 1:2 * nb + 2]
        x_ref, dx2_ref, g_ref, gx_ref, gng_ref = refs[2 * nb + 2:]
        i = pl.program_id(0)

        @pl.when(i == 0)
        def _():
            gng_ref[...] = jnp.zeros_like(gng_ref)

        dh = _nt(dps[0][...], ws[0][0])
        for j in range(1, nb + 1):
            dh = dh + _nt(dps[j][...], ws[j][0])
        xv = x_ref[...]
        r = lax.rsqrt(jnp.mean(xv * xv, axis=-1, keepdims=True) + EPS)
        xn = xv * r
        gng_ref[...] += jnp.sum(dh * xn, axis=0, keepdims=True)
        dhn = dh * g_ref[...]
        gx_ref[...] = dx2_ref[...] + r * (dhn - xn * jnp.mean(dhn * xn, axis=-1, keepdims=True))

    tile = lambda width, col_blk: pl.BlockSpec((TM_MM, width), lambda i: (i, col_blk))
    wspec = lambda blk: pl.BlockSpec((1, d, tn), lambda i: (blk, 0, 0), pipeline_mode=pl.Buffered(1))
    return pl.pallas_call(
        body, name="in_proj_bwd_x", grid=(n,),
        in_specs=[tile(tn, 0)] + [tile(tn, j) for j in range(nb)] + [wspec(j) for j in range(nb + 1)]
        + [tile(d, 0), tile(d, 0), pl.BlockSpec((1, d), lambda i: (0, 0))],
        out_specs=[tile(d, 0), pl.BlockSpec((1, d), lambda i: (0, 0))],
        out_shape=[jax.ShapeDtypeStruct((s, d), F32), jax.ShapeDtypeStruct((1, d), F32)],
        compiler_params=_params("arbitrary"),
    )(dqkv, *([dpb] * nb), *([w_in] * (nb + 1)), x, dx2, g)


def _in_proj_bwd_w(h, dqkv, dpb, order):
    s, d = h.shape
    tn = dqkv.shape[1]
    n = s // TM_BLK
    hr = d // 2
    settle = min(2, n - 1)

    def body(order_ref, h_ref, da_ref, db_ref, slots_ref, acc, sendbuf, pairbuf, chipbuf, psend, precv, send, recv, lsem):
        j, i = pl.program_id(0), pl.program_id(1)
        blk = order_ref[j]
        pos = _position()
        x, y, c = pos

        @pl.when(i == 0)
        def _():
            acc[...] = jnp.zeros_like(acc)

        @pl.when(blk == 0)
        def _():
            acc[...] += _tn(h_ref[...], da_ref[...])

        @pl.when(blk > 0)
        def _():
            acc[...] += _tn(h_ref[...], db_ref[...])

        def pair(step, half):
            return pltpu.make_async_remote_copy(
                src_ref=sendbuf.at[step, pl.ds(half * hr, hr), :], dst_ref=pairbuf.at[step],
                send_sem=psend.at[step], recv_sem=precv.at[step], device_id=(x, y, 1 - c), device_id_type=MESH)

        def ici(step):
            flip = OWNER_FLIPS[step]
            return pltpu.make_async_remote_copy(
                src_ref=chipbuf.at[step], dst_ref=slots_ref.at[flip], send_sem=send.at[step], recv_sem=recv.at[step],
                device_id=_peer(pos, 4 * (flip >> 1) + 2 * (flip & 1)), device_id_type=MESH)

        local = pltpu.make_async_copy(chipbuf.at[N_CHIPS - 1], slots_ref.at[0], lsem.at[0])

        def combine(step):
            pair(step, c).wait_recv()
            mine = sendbuf[step, pl.ds(c * hr, hr), :].astype(F32)
            chipbuf[step] = (mine + pairbuf[step].astype(F32)).astype(BF16)

        for step in range(N_CHIPS):
            @pl.when((j == step) & (i == n - 1))
            def _(step=step):
                sendbuf[step] = acc[...].astype(BF16)
                pair(step, 1 - c).start()

        for step in range(N_CHIPS - 1):
            @pl.when((j == step + 1) & (i == settle))
            def _(step=step):
                combine(step)
                ici(step).start()

        @pl.when((j == N_CHIPS - 1) & (i == n - 1))
        def _():
            combine(N_CHIPS - 1)
            local.start()
            for step in range(N_CHIPS - 1):
                ici(step).wait_recv()
            for step in range(N_CHIPS - 1):
                ici(step).wait_send()
            for step in range(N_CHIPS):
                pair(step, 1 - c).wait_send()
            local.wait()

    return pl.pallas_call(
        body, name="in_proj_bwd_w",
        grid_spec=pltpu.PrefetchScalarGridSpec(
            num_scalar_prefetch=1, grid=(N_CHIPS, n),
            in_specs=[pl.BlockSpec((TM_BLK, d), lambda j, i, order: (i, 0)),
                      pl.BlockSpec((TM_BLK, tn), lambda j, i, order: (jnp.where(order[j] == 0, i, 0), 0)),
                      pl.BlockSpec((TM_BLK, tn), lambda j, i, order: (jnp.where(order[j] == 0, 0, i),
                                                                     jnp.maximum(order[j] - 1, 0)))],
            out_specs=[ANY],
            scratch_shapes=[pltpu.VMEM((d, tn), F32), pltpu.VMEM((N_CHIPS, d, tn), BF16),
                            pltpu.VMEM((N_CHIPS, hr, tn), BF16), pltpu.VMEM((N_CHIPS, hr, tn), BF16),
                            pltpu.SemaphoreType.DMA((N_CHIPS,)), pltpu.SemaphoreType.DMA((N_CHIPS,)),
                            pltpu.SemaphoreType.DMA((N_CHIPS - 1,)), pltpu.SemaphoreType.DMA((N_CHIPS - 1,)),
                            pltpu.SemaphoreType.DMA((1,))]),
        out_shape=[jax.ShapeDtypeStruct((N_CHIPS, hr, tn), BF16)],
        compiler_params=_params("arbitrary", "arbitrary"),
    )(order, h, dqkv, dpb)[0]


LOSS_ROW = 6


def _adam_update(w, g, m, v):
    c1 = 1.0 / (1.0 - ADAM_B1 ** ADAM_STEP)
    c2 = 1.0 / (1.0 - ADAM_B2 ** ADAM_STEP)
    m2 = ADAM_B1 * m + (1.0 - ADAM_B1) * g
    v2 = ADAM_B2 * v + (1.0 - ADAM_B2) * (g * g)
    return -ADAM_LR * ((m2 * c1) / (jnp.sqrt(v2 * c2) + ADAM_EPS) + ADAM_WD * w), m2, v2


def _adamw_small(recv, params, moments_m, moments_v):
    k = recv.shape[0]
    n_par = len(params)
    cshard = params[3].shape[1]

    def body(*refs):
        r_ref = refs[0]
        ws, ms, vs = refs[1:1 + n_par], refs[1 + n_par:1 + 2 * n_par], refs[1 + 2 * n_par:1 + 3 * n_par]
        loss_ref = refs[1 + 3 * n_par]
        outs = refs[2 + 3 * n_par:]
        total = r_ref[0]
        for slot in range(1, k):
            total = total + r_ref[slot]
        loss_ref[...] = jnp.sum(total[LOSS_ROW:LOSS_ROW + 1, :], axis=1, keepdims=True)
        chip = 2 * lax.axis_index("x") + lax.axis_index("y")
        g_cw = jnp.zeros((3, cshard), F32)
        for sh in range(N_CHIPS):
            g_cw = g_cw + jnp.where(chip == sh, total[3:6, sh * cshard:(sh + 1) * cshard], 0.0)
        grads = [total[0:1, :], total[1:2, :], total[2:3, :ws[2].shape[1]], g_cw, total[8:16, :ws[4].shape[1]]]
        for p in range(n_par):
            delta, m2, v2 = _adam_update(ws[p][...], grads[p], ms[p][...], vs[p][...])
            for q, val in enumerate((grads[p], delta, m2, v2)):
                outs[4 * p + q][...] = val

    shapes = [jax.ShapeDtypeStruct((1, 1), F32)]
    for p in params:
        shapes += [jax.ShapeDtypeStruct(p.shape, F32)] * 4
    return pl.pallas_call(body, name="adamw_small", out_shape=shapes)(recv, *params, *moments_m, *moments_v)


def _adamw(w, g, m, v, name, rows_per_step):
    r, c = w.shape

    def body(w_ref, g_ref, m_ref, v_ref, d_ref, mo_ref, vo_ref):
        d_ref[...], mo_ref[...], vo_ref[...] = _adam_update(w_ref[...], g_ref[...], m_ref[...], v_ref[...])

    spec = pl.BlockSpec((rows_per_step, c), lambda i: (i, 0))
    shape = jax.ShapeDtypeStruct((r, c), F32)
    return pl.pallas_call(
        body, name=name, grid=(r // rows_per_step,),
        in_specs=[spec] * 4, out_specs=[spec] * 3, out_shape=[shape] * 3,
        compiler_params=_params("parallel"),
    )(w, g, m, v)


ANY = pl.BlockSpec(memory_space=pl.ANY)
N_CHIPS = 4
N_DEV = 8
OWNER_FLIPS = (3, 1, 2, 0)


def _position():
    return lax.axis_index("x"), lax.axis_index("y"), lax.axis_index("c")


def _gather_out_shapes(shards, kinds, cw8):
    full = [(a.shape[0], a.shape[1] * N_CHIPS) if k == "cols" else (a.shape[0] * N_CHIPS, a.shape[1])
            for a, k in zip(shards, kinds)]
    return [jax.ShapeDtypeStruct(f, BF16) for f in full] + [
        jax.ShapeDtypeStruct((N_CHIPS,) + cw8.shape, cw8.dtype)]


def _gather_sems(nw):
    return [pltpu.SemaphoreType.DMA((3, nw)), pltpu.SemaphoreType.DMA((3, nw)),
            pltpu.SemaphoreType.DMA((3, nw)), pltpu.SemaphoreType.DMA((3, nw)),
            pltpu.SemaphoreType.DMA((3,)), pltpu.SemaphoreType.DMA((3,)), pltpu.SemaphoreType.DMA((nw + 1,))]


def _gather_plan(kinds, srcs, dsts, cw, cw_all, send1, recv1, send2, recv2, ssend, srecv, lsem):
    nw = len(srcs)
    x, y, c = _position()
    mine = 2 * x + y
    chips = [(x, 1 - y), (1 - x, y), (1 - x, 1 - y)]

    def window(w, shard, half):
        r, cc = srcs[w].shape
        hr = r // 2
        if kinds[w] == "cols":
            rows = pl.ds(0, r) if half is None else pl.ds(half * hr, hr)
            return dsts[w].at[rows, pl.ds(shard * cc, cc)]
        rows = pl.ds(shard * r, r) if half is None else pl.ds(shard * r + half * hr, hr)
        return dsts[w].at[rows, :]

    def my_half(w):
        hr = srcs[w].shape[0] // 2
        return srcs[w].at[pl.ds(c * hr, hr), :]

    def local():
        return [pltpu.make_async_copy(srcs[w], window(w, mine, None), lsem.at[w]) for w in range(nw)] + [
            pltpu.make_async_copy(cw, cw_all.at[mine], lsem.at[nw])]

    def ici(k, w, shard):
        kx, ky = chips[k]
        return pltpu.make_async_remote_copy(
            src_ref=my_half(w), dst_ref=window(w, shard, c), send_sem=send1.at[k, w], recv_sem=recv1.at[k, w],
            device_id=(kx, ky, c), device_id_type=MESH)

    def d2d(k, w, shard, half):
        return pltpu.make_async_remote_copy(
            src_ref=window(w, shard, half), dst_ref=window(w, shard, half),
            send_sem=send2.at[k, w], recv_sem=recv2.at[k, w], device_id=(x, y, 1 - c), device_id_type=MESH)

    def small(k, shard):
        kx, ky = chips[k]
        return pltpu.make_async_remote_copy(
            src_ref=cw, dst_ref=cw_all.at[shard], send_sem=ssend.at[k], recv_sem=srecv.at[k],
            device_id=(kx, ky, c), device_id_type=MESH)

    def theirs(k):
        kx, ky = chips[k]
        return 2 * kx + ky

    def start():
        for cp in local():
            cp.start()
        for k in range(3):
            for w in range(nw):
                ici(k, w, mine).start()
            small(k, mine).start()

    def forward():
        for k in range(3):
            for w in range(nw):
                ici(k, w, theirs(k)).wait_recv()
                d2d(k, w, theirs(k), c).start()

    def finish():
        for k in range(3):
            for w in range(nw):
                d2d(k, w, theirs(k), 1 - c).wait_recv()
            small(k, theirs(k)).wait_recv()
        for k in range(3):
            for w in range(nw):
                ici(k, w, mine).wait_send()
                d2d(k, w, theirs(k), c).wait_send()
            small(k, mine).wait_send()
        for cp in local():
            cp.wait()

    return start, forward, finish


def _scatter_out_shapes(parts):
    return [jax.ShapeDtypeStruct((N_DEV, p.shape[1] // 2, p.shape[2]), p.dtype) for p in parts]


def _scatter_sems(nw):
    return [pltpu.SemaphoreType.DMA((N_DEV, nw)), pltpu.SemaphoreType.DMA((N_DEV, nw)), pltpu.SemaphoreType.DMA((nw,))]


def _peer(pos, k):
    x, y, c = pos
    return ((1 - x) if k & 4 else x, (1 - y) if k & 2 else y, (1 - c) if k & 1 else c)


def _scatter_plan(srcs, dsts, send, recv, lsem):
    nw = len(srcs)
    pos = _position()

    def piece(w, k):
        px, py, pc = _peer(pos, k)
        hr = srcs[w].shape[1] // 2
        return srcs[w].at[2 * px + py, pl.ds(pc * hr, hr), :]

    def remote(w, k):
        return pltpu.make_async_remote_copy(
            src_ref=piece(w, k), dst_ref=dsts[w].at[k], send_sem=send.at[k, w], recv_sem=recv.at[k, w],
            device_id=_peer(pos, k), device_id_type=MESH)

    def local(w):
        return pltpu.make_async_copy(piece(w, 0), dsts[w].at[0], lsem.at[w])

    def start():
        for w in range(nw):
            local(w).start()
        for k in range(1, N_DEV):
            for w in range(nw):
                remote(w, k).start()

    def finish():
        for k in range(1, N_DEV):
            for w in range(nw):
                remote(w, k).wait_recv()
        for k in range(1, N_DEV):
            for w in range(nw):
                remote(w, k).wait_send()
        for w in range(nw):
            local(w).wait()

    return start, finish


def _reduce_pair(slots, small):
    nw = len(slots)

    def body(*refs):
        srcs, sm = refs[:nw], refs[nw]
        dsts, sm_all = refs[nw + 1:2 * nw + 1], refs[2 * nw + 1]
        halves = refs[2 * nw + 2:3 * nw + 2]
        send, recv, ssend, srecv, lsem = refs[3 * nw + 2:]
        pos = _position()
        x, y, c = pos
        me = 4 * x + 2 * y + c

        def rows(w, half):
            hr = halves[w].shape[0]
            return dsts[w].at[pl.ds(half * hr, hr), :]

        def remote(w, half):
            return pltpu.make_async_remote_copy(
                src_ref=halves[w], dst_ref=rows(w, half), send_sem=send.at[w], recv_sem=recv.at[w],
                device_id=(x, y, 1 - c), device_id_type=MESH)

        def bcast(k, slot):
            return pltpu.make_async_remote_copy(
                src_ref=sm, dst_ref=sm_all.at[slot], send_sem=ssend.at[k], recv_sem=srecv.at[k],
                device_id=_peer(pos, k), device_id_type=MESH)

        small_copies = [bcast(k, me) for k in range(1, N_DEV)]
        own_small = pltpu.make_async_copy(sm, sm_all.at[me], lsem.at[nw])
        for cp in small_copies + [own_small]:
            cp.start()
        big = []
        for w in range(nw):
            total = srcs[w][0].astype(F32)
            for k in range(1, srcs[w].shape[0]):
                total = total + srcs[w][k].astype(F32)
            halves[w][...] = total
            big += [remote(w, c), pltpu.make_async_copy(halves[w], rows(w, c), lsem.at[w])]
            big[-2].start()
            big[-1].start()
        for w in range(nw):
            remote(w, 1 - c).wait_recv()
        for k in range(1, N_DEV):
            px, py, pc = _peer(pos, k)
            bcast(k, 4 * px + 2 * py + pc).wait_recv()
        for w in range(nw):
            big[2 * w].wait_send()
            big[2 * w + 1].wait()
        for cp in small_copies:
            cp.wait_send()
        own_small.wait()

    vmem = pl.BlockSpec(memory_space=pltpu.VMEM)
    half_shapes = [(sl.shape[1], sl.shape[2]) for sl in slots]
    return pl.pallas_call(
        body, name="reduce_pair",
        in_specs=[vmem] * (nw + 1), out_specs=[ANY] * (nw + 1),
        out_shape=[jax.ShapeDtypeStruct((2 * r, cc), F32) for r, cc in half_shapes]
        + [jax.ShapeDtypeStruct((N_DEV,) + small.shape, small.dtype)],
        scratch_shapes=[pltpu.VMEM(hs, F32) for hs in half_shapes]
        + [pltpu.SemaphoreType.DMA((nw,)), pltpu.SemaphoreType.DMA((nw,)),
           pltpu.SemaphoreType.DMA((N_DEV,)), pltpu.SemaphoreType.DMA((N_DEV,)),
           pltpu.SemaphoreType.DMA((nw + 1,))],
        compiler_params=pltpu.CompilerParams(vmem_limit_bytes=VMEM_LIMIT),
    )(*slots, small)


def _pad_to(a, rows, cols):
    return jnp.pad(a, ((0, rows - a.shape[0]), (0, cols - a.shape[1])))


def _pack_small(norm_g, fin_g, conv_b, conv_w, loss_vec, rel):
    rows = [_pad_to(norm_g, 1, SMALL_COLS), _pad_to(fin_g, 1, SMALL_COLS), _pad_to(conv_b, 1, SMALL_COLS),
            _pad_to(conv_w, 3, SMALL_COLS), _pad_to(loss_vec, 2, SMALL_COLS), _pad_to(rel, HEADS, SMALL_COLS)]
    return jnp.concatenate(rows, axis=0)


def kernel(x, norm_g, w_in, rel_bias, w_att_out, conv_w, conv_b, w_conv_out, w_out, final_norm_g, loss_target, m_norm_g, m_w_in, m_rel_bias, m_w_att_out, m_conv_w, m_conv_b, m_w_conv_out, m_w_out, m_final_norm_g, v_norm_g, v_w_in, v_rel_bias, v_w_att_out, v_conv_w, v_conv_b, v_w_conv_out, v_w_out, v_final_norm_g):
    xs, tgt = x[0], loss_target[0]
    cshard = conv_w.shape[2]
    chip = 2 * lax.axis_index("x") + lax.axis_index("y")

    shards = [w_in[0], w_att_out[0], w_conv_out[0], w_out[0]]
    cw8 = _pad_to(conv_w[0], SUBLANES, cshard)
    flips = jnp.arange(N_CHIPS, dtype=jnp.int32)
    own_first = jnp.bitwise_xor(chip, flips)
    own_last = jnp.bitwise_xor(chip, jnp.asarray(OWNER_FLIPS, jnp.int32))

    proj, h, wb_in = _in_proj_gather(xs, norm_g, shards[0], own_first)
    diag = jnp.take(rel_bias[0], _diag_rel_index(), axis=1)
    att, lse, wb_att, wb_conv, wb_out, cw_all = _attn_fwd(diag, proj, shards[1:], ["cols", "cols", "rows"], cw8)
    conv_w_full = jnp.transpose(cw_all, (1, 0, 2)).reshape(SUBLANES, N_CHIPS * cshard)
    (dpb, d_att, dx2, g_att_p, g_conv_p, g_out_p, loss_vec, g_fin, g_cb, g_cw) = _mixer_mid(
        att, proj, xs, tgt, wb_att, wb_conv, wb_out, conv_w_full, conv_b, final_norm_g[None, :])
    dqkv, g_rel, r_att, r_conv, r_out = _attn_bwd(diag, proj, d_att, att, lse, [g_att_p, g_conv_p, g_out_p])
    grad_x, g_norm = _in_proj_bwd_x(dqkv, dpb, wb_in, xs, dx2, norm_g)
    r_in = _in_proj_bwd_w(h, dqkv, dpb, own_last)

    small = _pack_small(g_norm, g_fin, g_cb, g_cw[0:3], loss_vec, g_rel)
    gw_in, gw_att, gw_conv, gw_out, r_small = _reduce_pair([r_in, r_att, r_conv, r_out], small)
    small_out = _adamw_small(
        r_small,
        [norm_g, final_norm_g[None, :], conv_b, conv_w[0], rel_bias[0]],
        [m_norm_g, m_final_norm_g[None, :], m_conv_b, m_conv_w[0], m_rel_bias[0]],
        [v_norm_g, v_final_norm_g[None, :], v_conv_b, v_conv_w[0], v_rel_bias[0]])
    loss = small_out[0][0, 0]
    small_names = ["norm_g", "final_norm_g", "conv_b", "conv_w", "rel_bias"]
    fix = {"norm_g": lambda a: a, "final_norm_g": lambda a: a[0], "conv_b": lambda a: a,
           "conv_w": lambda a: a[None], "rel_bias": lambda a: a[None]}
    small_res = {name: [fix[name](small_out[1 + 4 * p + q]) for q in range(4)] for p, name in enumerate(small_names)}

    big = {}
    for name, w, g, m, v, rows in (("w_in", w_in, gw_in, m_w_in, v_w_in, 128),
                                   ("w_att_out", w_att_out, gw_att, m_w_att_out, v_w_att_out, 256),
                                   ("w_conv_out", w_conv_out, gw_conv, m_w_conv_out, v_w_conv_out, 256),
                                   ("w_out", w_out, gw_out, m_w_out, v_w_out, 128)):
        dw, mw, vw = _adamw(w[0], g, m[0], v[0], "adamw_" + name, rows)
        big[name] = (g[None], dw[None], mw[None], vw[None])

    order = ["norm_g", "w_in", "rel_bias", "w_att_out", "conv_w", "conv_b", "w_conv_out", "w_out", "final_norm_g"]
    outs = [loss, grad_x[None]]
    for which in range(4):
        for name in order:
            outs.append(big[name][which] if name in big else small_res[name][which])
    return tuple(outs)
```

```python
import numpy as np
import jax
import jax.numpy as jnp
from jax import lax
from jax.experimental import pallas as pl
from jax.experimental.pallas import tpu as pltpu

F32 = jnp.float32
BF16 = jnp.bfloat16
MESH = pl.DeviceIdType.MESH

CHUNK = 64
N_LEFT = 8
HEADS = 8
HEAD_DIM = 64
D_ATT = HEADS * HEAD_DIM
MAX_REL = 128
N_REL = 2 * MAX_REL + 1
EPS = 1e-6
NEG_BIG = -1e30
ADAM_LR, ADAM_B1, ADAM_B2, ADAM_EPS, ADAM_WD, ADAM_STEP = 0.001, 0.9, 0.999, 1e-08, 0.01, 10

LANES = 128
SUBLANES = 8
VMEM_LIMIT = 56 * 1024 * 1024

QB = 2 * CHUNK
KW = N_LEFT * CHUNK + QB
DIAG = KW + QB
TQ = N_LEFT * CHUNK
TM_MID = 256
TM_MM = 512
TM_BLK = 1024
SMALL_ROWS, SMALL_COLS = 16, 1024


def _params(*sem):
    return pltpu.CompilerParams(dimension_semantics=sem, vmem_limit_bytes=VMEM_LIMIT)


def _nt(a, b):
    return lax.dot_general(a, b, (((1,), (1,)), ((), ())), preferred_element_type=F32)


def _tn(a, b):
    return lax.dot_general(a, b, (((0,), (0,)), ((), ())), preferred_element_type=F32)


def _nn(a, b):
    return jnp.dot(a, b, preferred_element_type=F32)


def _diag_rel_index():
    d = np.arange(DIAG)
    diff = np.where(d < KW, d, d - DIAG)
    rel = N_LEFT * CHUNK - diff
    return np.clip(rel, -MAX_REL, MAX_REL) + MAX_REL


def _build_bias(diag_ref, bias_scr):
    r = lax.broadcasted_iota(jnp.int32, (QB, KW), 0) // CHUNK
    s = lax.broadcasted_iota(jnp.int32, (QB, KW), 1) // CHUNK
    allowed = (s >= r) & (s <= r + N_LEFT)
    for h in range(HEADS):
        row = jnp.broadcast_to(diag_ref[h:h + 1, :], (QB, DIAG))
        t = pltpu.roll(row, 0, 1, stride=1, stride_axis=0)
        bias_scr[h // 2, (h % 2) * QB:(h % 2 + 1) * QB, :] = jnp.where(allowed, t[:, :KW], NEG_BIG)


def _stack_heads(a, lane_hi):
    zero = jnp.zeros_like(a)
    return jnp.concatenate([jnp.where(lane_hi, zero, a), jnp.where(lane_hi, a, zero)], axis=0)


def _in_proj_gather(x, g, shard, order):
    s, d = x.shape
    tn = shard.shape[1]
    n = s // TM_BLK
    hr = d // 2

    def body(order_ref, x_ref, g_ref, shard32_ref, proj_ref, h_ref, wfull_ref, shard_ref, hbuf, wbuf,
             send1, recv1, send2, recv2, lsem):
        del order_ref
        j, i = pl.program_id(0), pl.program_id(1)
        x, y, c = _position()
        mine = 2 * x + y
        chips = [(x, 1 - y), (1 - x, y), (1 - x, 1 - y)]

        def theirs(k):
            return 2 * chips[k][0] + chips[k][1]

        def half_rows(half):
            return pl.ds(half * hr, hr)

        def ici(k, shard_index):
            return pltpu.make_async_remote_copy(
                src_ref=shard_ref.at[half_rows(c), :], dst_ref=wfull_ref.at[shard_index, half_rows(c), :],
                send_sem=send1.at[k], recv_sem=recv1.at[k], device_id=(*chips[k], c), device_id_type=MESH)

        def d2d(k, half):
            return pltpu.make_async_remote_copy(
                src_ref=wbuf.at[k % 2, half_rows(half), :], dst_ref=wfull_ref.at[theirs(k), half_rows(half), :],
                send_sem=send2.at[k], recv_sem=recv2.at[k], device_id=(x, y, 1 - c), device_id_type=MESH)

        def load(k, half, sem):
            return pltpu.make_async_copy(wfull_ref.at[theirs(k), half_rows(half), :],
                                         wbuf.at[k % 2, half_rows(half), :], lsem.at[sem])

        own = pltpu.make_async_copy(shard_ref, wfull_ref.at[mine], lsem.at[0])

        def put_proj(block):
            for grp in range(tn // D_ATT):
                proj_ref[grp] = block[:, grp * D_ATT:(grp + 1) * D_ATT].astype(BF16)

        @pl.when((j == 0) & (i == 0))
        def _():
            shard_ref[...] = shard32_ref[...].astype(BF16)
            own.start()
            ici(0, mine).start()
            ici(1, mine).start()

        for k in range(3):
            first = max(n - 3, 0) if k == 0 else min(n // 2, n - 1)

            @pl.when((j == k) & (i == first))
            def _(k=k):
                if k == 0:
                    ici(0, mine).wait_send()
                    ici(1, mine).wait_send()
                    ici(2, mine).start()
                if k == 2:
                    d2d(0, c).wait_send()
                ici(k, theirs(k)).wait_recv()
                load(k, c, 1).start()

            @pl.when((j == k) & (i == min(first + 1, n - 1)))
            def _(k=k):
                load(k, c, 1).wait()
                d2d(k, c).start()

            @pl.when((j == k) & (i == min(first + 2, n - 1)))
            def _(k=k):
                d2d(k, 1 - c).wait_recv()
                load(k, 1 - c, 2).start()

            @pl.when((j == k + 1) & (i == 0))
            def _(k=k):
                load(k, 1 - c, 2).wait()

        @pl.when(j == 0)
        def _():
            xv = x_ref[...]
            r = lax.rsqrt(jnp.mean(xv * xv, axis=-1, keepdims=True) + EPS)
            hv = ((xv * r) * g_ref[...]).astype(BF16)
            hbuf[i] = hv
            h_ref[...] = hv
            put_proj(_nn(hv, shard_ref[...]))

        for k in range(3):
            @pl.when(j == k + 1)
            def _(k=k):
                put_proj(_nn(hbuf[i], wbuf[k % 2]))

        @pl.when((j == 3) & (i == n - 1))
        def _():
            ici(2, mine).wait_send()
            d2d(1, c).wait_send()
            d2d(2, c).wait_send()
            own.wait()

    return pl.pallas_call(
        body, name="in_proj_gather",
        grid_spec=pltpu.PrefetchScalarGridSpec(
            num_scalar_prefetch=1, grid=(N_CHIPS, n),
            in_specs=[pl.BlockSpec((TM_BLK, d), lambda j, i, order: (jnp.where(j == 0, i, n - 1), 0)),
                      pl.BlockSpec((1, d), lambda j, i, order: (0, 0)), pl.BlockSpec(memory_space=pltpu.VMEM)],
            out_specs=[pl.BlockSpec((tn // D_ATT, TM_BLK, D_ATT), lambda j, i, order: (order[j], i, 0)),
                       pl.BlockSpec((TM_BLK, d), lambda j, i, order: (jnp.where(j == 0, i, n - 1), 0)), ANY],
            scratch_shapes=[pltpu.VMEM((d, tn), BF16), pltpu.VMEM((n, TM_BLK, d), BF16), pltpu.VMEM((2, d, tn), BF16),
                            pltpu.SemaphoreType.DMA((3,)), pltpu.SemaphoreType.DMA((3,)),
                            pltpu.SemaphoreType.DMA((3,)), pltpu.SemaphoreType.DMA((3,)), pltpu.SemaphoreType.DMA((3,))]),
        out_shape=[jax.ShapeDtypeStruct((N_CHIPS * tn // D_ATT, s, D_ATT), BF16), jax.ShapeDtypeStruct((s, d), BF16),
                   jax.ShapeDtypeStruct((N_CHIPS, d, tn), BF16)],
        compiler_params=_params("arbitrary", "arbitrary"),
    )(order, x, g, shard)


def _attn_fwd(diag, proj, shards, kinds, cw8):
    s = proj.shape[1]
    n = s // TQ
    nw = len(shards)
    scale = HEAD_DIM ** -0.5

    def body(*refs):
        diag_ref, q_ref, kp_ref, kc_ref, vp_ref, vc_ref = refs[:6]
        srcs, cw = refs[6:6 + nw], refs[6 + nw]
        o_ref, lse_ref = refs[7 + nw:9 + nw]
        dsts, cw_all = refs[9 + nw:9 + 2 * nw], refs[9 + 2 * nw]
        bias_scr = refs[10 + 2 * nw]
        casts = refs[11 + 2 * nw:11 + 3 * nw]
        start, forward, finish = _gather_plan(kinds, casts, dsts, cw, cw_all, *refs[11 + 3 * nw:])
        i = pl.program_id(0)

        @pl.when(i == 0)
        def _():
            for w in range(nw):
                casts[w][...] = srcs[w][...].astype(BF16)
            start()
            _build_bias(diag_ref, bias_scr)

        @pl.when(i == n // 2)
        def _():
            forward()

        @pl.when(i == n - 1)
        def _():
            finish()

        lane_hi = lax.broadcasted_iota(jnp.int32, (QB, LANES), 1) >= HEAD_DIM

        def block(b, first_tile):
            r0, n_prev = b * QB, TQ - b * QB
            n_cur = KW - n_prev
            pairs = range(HEADS // 2)
            lanes_of = [slice(LANES * p, LANES * (p + 1)) for p in pairs]
            scores = []
            for p in pairs:
                lanes = lanes_of[p]
                q2 = _stack_heads(q_ref[r0:r0 + QB, lanes] * scale, lane_hi)
                s_cur = _nt(q2, kc_ref[0:n_cur, lanes]) + bias_scr[p, :, n_prev:KW]
                if first_tile:
                    scores.append(s_cur)
                else:
                    scores.append(jnp.concatenate(
                        [_nt(q2, kp_ref[r0:TQ, lanes]) + bias_scr[p, :, 0:n_prev], s_cur], axis=1))
            probs = []
            for p in pairs:
                sc = scores[p]
                m = jnp.max(sc, axis=1, keepdims=True)
                pe = jnp.exp(sc - m)
                l = jnp.sum(pe, axis=1, keepdims=True)
                probs.append((pe.astype(BF16), l, m))
            for p in pairs:
                lanes = lanes_of[p]
                pb, l, m = probs[p]
                if first_tile:
                    o2 = _nn(pb, vc_ref[0:n_cur, lanes]) / l
                else:
                    o2 = (_nn(pb[:, 0:n_prev], vp_ref[r0:TQ, lanes]) + _nn(pb[:, n_prev:KW], vc_ref[0:n_cur, lanes])) / l
                lse2 = m + jnp.log(l)
                lse_ref[r0:r0 + QB, 2 * p:2 * p + 1] = lse2[0:QB, :]
                lse_ref[r0:r0 + QB, 2 * p + 1:2 * p + 2] = lse2[QB:2 * QB, :]
                o_ref[r0:r0 + QB, lanes] = jnp.where(lane_hi, o2[QB:2 * QB, :], o2[0:QB, :]).astype(BF16)

        @pl.when(i == 0)
        def _():
            for b in range(TQ // QB):
                block(b, True)

        @pl.when(i > 0)
        def _():
            for b in range(TQ // QB):
                block(b, False)

    blk = lambda grp, prev: pl.BlockSpec(
        (None, TQ, D_ATT), (lambda i: (grp, jnp.maximum(i - 1, 0), 0)) if prev else (lambda i: (grp, i, 0)))
    vmem = pl.BlockSpec(memory_space=pltpu.VMEM)
    return pl.pallas_call(
        body, name="attn_fwd", grid=(n,),
        in_specs=[pl.BlockSpec((HEADS, DIAG), lambda i: (0, 0)),
                  blk(0, False), blk(1, True), blk(1, False), blk(2, True), blk(2, False)] + [vmem] * (nw + 1),
        out_specs=[pl.BlockSpec((TQ, D_ATT), lambda i: (i, 0)), pl.BlockSpec((TQ, HEADS), lambda i: (i, 0))]
        + [ANY] * (nw + 1),
        out_shape=[jax.ShapeDtypeStruct((s, D_ATT), BF16), jax.ShapeDtypeStruct((s, HEADS), F32)]
        + _gather_out_shapes(shards, kinds, cw8),
        scratch_shapes=[pltpu.VMEM((HEADS // 2, 2 * QB, KW), F32)] + [pltpu.VMEM(a.shape, BF16) for a in shards]
        + _gather_sems(nw),
        compiler_params=_params("arbitrary"),
    )(diag, proj, proj, proj, proj, proj, *shards, cw8)


def _attn_bwd(diag, proj, d_att, att, lse, parts):
    s = proj.shape[1]
    n = s // TQ
    npart = len(parts)
    scale = HEAD_DIM ** -0.5
    rel_pad = 3 * LANES

    def body(*refs):
        diag_ref, q_ref, kp_ref, kc_ref, vp_ref, vc_ref, do_ref, o_ref, lse_ref = refs[:9]
        part_refs = refs[9:9 + npart]
        dqkv_ref, dbias_ref = refs[9 + npart:11 + npart]
        slot_refs = refs[11 + npart:11 + 2 * npart]
        bias_scr, dbias_acc, dk_acc, dv_acc, dq_scr = refs[11 + 2 * npart:16 + 2 * npart]
        start, finish = _scatter_plan(part_refs, slot_refs, *refs[16 + 2 * npart:])
        i = pl.program_id(0)
        cur, prv = i % 2, 1 - i % 2

        @pl.when(i == 0)
        def _():
            start()
            _build_bias(diag_ref, bias_scr)
            dbias_acc[...] = jnp.zeros_like(dbias_acc)
            dk_acc[...] = jnp.zeros_like(dk_acc)
            dv_acc[...] = jnp.zeros_like(dv_acc)

        @pl.when(i > 0)
        def _():
            dqkv_ref[:, 0:D_ATT] = dq_scr[...]
            dk_acc[cur] = jnp.zeros((TQ, D_ATT), F32)
            dv_acc[cur] = jnp.zeros((TQ, D_ATT), F32)

        lane_hi = lax.broadcasted_iota(jnp.int32, (QB, LANES), 1) >= HEAD_DIM
        col = lax.broadcasted_iota(jnp.int32, (2 * QB, KW), 1)

        def make_block(first_tile):
            def block(b):
                r0, n_prev = b * QB, TQ - b * QB
                n_cur = KW - n_prev
                for p in range(HEADS // 2):
                    lanes = slice(LANES * p, LANES * (p + 1))
                    q2 = _stack_heads(q_ref[r0:r0 + QB, lanes] * scale, lane_hi)
                    kw = jnp.concatenate([kp_ref[r0:TQ, lanes], kc_ref[0:n_cur, lanes]], axis=0)
                    vw = jnp.concatenate([vp_ref[r0:TQ, lanes], vc_ref[0:n_cur, lanes]], axis=0)
                    dop = do_ref[r0:r0 + QB, lanes]
                    do2 = _stack_heads(dop, lane_hi)
                    prod = dop.astype(F32) * o_ref[r0:r0 + QB, lanes].astype(F32)
                    delta2 = jnp.concatenate(
                        [jnp.sum(jnp.where(lane_hi, 0.0, prod), axis=1, keepdims=True),
                         jnp.sum(jnp.where(lane_hi, prod, 0.0), axis=1, keepdims=True)], axis=0)
                    lse2 = jnp.concatenate([lse_ref[r0:r0 + QB, 2 * p:2 * p + 1],
                                            lse_ref[r0:r0 + QB, 2 * p + 1:2 * p + 2]], axis=0)
                    sc = _nt(q2, kw) + bias_scr[p]
                    if first_tile:
                        sc = jnp.where(col >= TQ - r0, sc, NEG_BIG)
                    pr = jnp.exp(sc - lse2)
                    ds = pr * (_nt(do2, vw) - delta2)
                    dbias_acc[p] += ds
                    dsb = ds.astype(BF16)
                    dv_w = _tn(pr.astype(BF16), do2)
                    dk_w = _tn(dsb, q2)
                    dv_acc[prv, r0:TQ, lanes] += dv_w[0:n_prev, :]
                    dv_acc[cur, 0:n_cur, lanes] += dv_w[n_prev:KW, :]
                    dk_acc[prv, r0:TQ, lanes] += dk_w[0:n_prev, :]
                    dk_acc[cur, 0:n_cur, lanes] += dk_w[n_prev:KW, :]
                    dq2 = _nn(dsb, kw)
                    dq = jnp.where(lane_hi, dq2[QB:2 * QB, :], dq2[0:QB, :]) * scale
                    dq_scr[r0:r0 + QB, lanes] = dq.astype(BF16)
            return block

        @pl.when(i == 0)
        def _():
            for b in range(TQ // QB):
                make_block(True)(b)

        @pl.when((i > 0) & (i < n))
        def _():
            for b in range(TQ // QB):
                make_block(False)(b)

        @pl.when(i > 0)
        def _():
            dqkv_ref[:, D_ATT:2 * D_ATT] = dk_acc[prv].astype(BF16)
            dqkv_ref[:, 2 * D_ATT:3 * D_ATT] = dv_acc[prv].astype(BF16)

        @pl.when(i == n)
        def _():
            d_iota = lax.broadcasted_iota(jnp.int32, (DIAG, rel_pad), 0)
            n_iota = lax.broadcasted_iota(jnp.int32, (DIAG, rel_pad), 1)
            diff = jnp.where(d_iota < KW, d_iota, d_iota - DIAG)
            idx = jnp.clip(N_LEFT * CHUNK - diff, -MAX_REL, MAX_REL) + MAX_REL
            onehot = (idx == n_iota).astype(F32)
            rows = []
            for hd in range(HEADS):
                acc = dbias_acc[hd // 2, (hd % 2) * QB:(hd % 2 + 1) * QB, :]
                a = jnp.concatenate([acc, jnp.zeros((QB, DIAG - KW), F32)], axis=1)
                g8 = a[0:SUBLANES, :]
                for blk in range(1, QB // SUBLANES):
                    g8 = g8 + pltpu.roll(a[blk * SUBLANES:(blk + 1) * SUBLANES, :], DIAG - blk * SUBLANES, 1)
                g1 = g8[0:1, :]
                for r in range(1, SUBLANES):
                    g1 = g1 + pltpu.roll(g8[r:r + 1, :], DIAG - r, 1)
                rows.append(g1)
            g = jnp.concatenate(rows, axis=0)
            dbias_ref[...] = jnp.dot(g, onehot, preferred_element_type=F32, precision=lax.Precision.HIGHEST)
            finish()

    last = n - 1
    cur = lambda grp: pl.BlockSpec((None, TQ, D_ATT), lambda i: (grp, jnp.minimum(i, last), 0))
    prev = lambda grp: pl.BlockSpec((None, TQ, D_ATT), lambda i: (grp, jnp.maximum(jnp.minimum(i, last) - 1, 0), 0))
    tile = pl.BlockSpec((TQ, D_ATT), lambda i: (jnp.minimum(i, last), 0))
    return pl.pallas_call(
        body, name="attn_bwd", grid=(n + 1,),
        in_specs=[pl.BlockSpec((HEADS, DIAG), lambda i: (0, 0)),
                  cur(0), prev(1), cur(1), prev(2), cur(2), tile, tile,
                  pl.BlockSpec((TQ, HEADS), lambda i: (jnp.minimum(i, last), 0))] + [ANY] * npart,
        out_specs=[pl.BlockSpec((TQ, 3 * D_ATT), lambda i: (jnp.maximum(i - 1, 0), 0)),
                   pl.BlockSpec((HEADS, rel_pad), lambda i: (0, 0))] + [ANY] * npart,
        out_shape=[jax.ShapeDtypeStruct((s, 3 * D_ATT), BF16), jax.ShapeDtypeStruct((HEADS, rel_pad), F32)]
        + _scatter_out_shapes(parts),
        scratch_shapes=[pltpu.VMEM((HEADS // 2, 2 * QB, KW), F32), pltpu.VMEM((HEADS // 2, 2 * QB, KW), F32),
                        pltpu.VMEM((2, TQ, D_ATT), F32), pltpu.VMEM((2, TQ, D_ATT), F32),
                        pltpu.VMEM((TQ, D_ATT), BF16)] + _scatter_sems(npart),
        compiler_params=_params("arbitrary"),
    )(diag, proj, proj, proj, proj, proj, d_att, att, lse, *parts)


def _shift_down(a, k, halo):
    rolled = pltpu.roll(a, k, 0)
    row = lax.broadcasted_iota(jnp.int32, halo.shape, 0)
    first = jnp.where(row < k, pltpu.roll(halo, k, 0), rolled[0:SUBLANES, :])
    return jnp.concatenate([first, rolled[SUBLANES:, :]], axis=0)


def _shift_up(a, k, nxt):
    tm = a.shape[0]
    rolled = pltpu.roll(a, tm - k, 0)
    row = lax.broadcasted_iota(jnp.int32, nxt.shape, 0)
    last = jnp.where(row >= SUBLANES - k, pltpu.roll(nxt, SUBLANES - k, 0), rolled[tm - SUBLANES:, :])
    return jnp.concatenate([rolled[:tm - SUBLANES, :], last], axis=0)


def _sigmoid(v):
    return 0.5 * jnp.tanh(0.5 * v) + 0.5


def _mixer_mid(att, proj, x, tgt, w_att, w_conv, w_out, conv_w8, conv_b, fin_g):
    s, d = x.shape
    dc = D_ATT
    n = s // TM_MID
    tm = TM_MID
    n_shards = 4

    def body(att_ref, za_ref, gb_ref, gc_ref, u_ref, zc_ref, hgc_ref, hu_ref, gatt_ref, gconv_ref, x_ref, t_ref,
             watt_ref, wconv_ref, wout_ref, cw_ref, cb_ref, fg_ref,
             dpb_ref, do_ref, dx2_ref, gatt_o, gconv_o, gout_o, loss_o, gfn_o, gcb_o, gcw_o,
             acc_att, acc_conv, acc_out, carry):
        i = pl.program_id(0)
        tile = n - 1 - i

        @pl.when(i == 0)
        def _():
            acc_att[...] = jnp.zeros_like(acc_att)
            acc_conv[...] = jnp.zeros_like(acc_conv)
            acc_out[...] = jnp.zeros_like(acc_out)
            carry[...] = jnp.zeros_like(carry)
            loss_o[...] = jnp.zeros_like(loss_o)
            gfn_o[...] = jnp.zeros_like(gfn_o)
            gcb_o[...] = jnp.zeros_like(gcb_o)
            gcw_o[...] = jnp.zeros_like(gcw_o)

        att_v = att_ref[...].astype(F32)
        za = za_ref[...].astype(F32)
        sa = _sigmoid(za)
        silu_a = za * sa
        a_b = (att_v * silu_a).astype(BF16)

        gb = gb_ref[...].astype(F32)
        gc = gc_ref[...].astype(F32)
        u = u_ref[...].astype(F32)
        zc = zc_ref[...].astype(F32)
        cu = gc * u
        halo = jnp.where(tile > 0, hgc_ref[...].astype(F32) * hu_ref[...].astype(F32), 0.0)
        cu1 = _shift_down(cu, 1, halo)
        cu2 = _shift_down(cu, 2, halo)
        w0, w1, w2 = cw_ref[0:1, :], cw_ref[1:2, :], cw_ref[2:3, :]
        vconv = w0 * cu2 + w1 * cu1 + w2 * cu + cb_ref[...]
        sc = _sigmoid(zc)
        silu_c = zc * sc
        c_b = (gb * vconv * silu_c).astype(BF16)

        y_att = _nn(a_b, watt_ref[...])
        y_conv = _nn(c_b, wconv_ref[...])
        ga = _sigmoid(jnp.concatenate([gatt_ref[0], gatt_ref[1]], axis=1).astype(F32))
        gv = _sigmoid(jnp.concatenate([gconv_ref[0], gconv_ref[1]], axis=1).astype(F32))
        m_b = (ga * y_att + gv * y_conv).astype(BF16)
        x2 = x_ref[...] + _nn(m_b, wout_ref[...])
        r2 = lax.rsqrt(jnp.mean(x2 * x2, axis=-1, keepdims=True) + EPS)
        x2n = x2 * r2
        fg = fg_ref[...]
        err = x2n * fg - t_ref[...]
        loss_o[...] += jnp.sum(err * err, axis=0, keepdims=True) * (0.5 / d)
        dy = err * (1.0 / d)
        gfn_o[...] += jnp.sum(dy * x2n, axis=0, keepdims=True)
        dyn = dy * fg
        dx2 = r2 * (dyn - x2n * jnp.mean(dyn * x2n, axis=-1, keepdims=True))
        dx2_ref[...] = dx2
        dx2_b = dx2.astype(BF16)

        dm = _nt(dx2_b, wout_ref[...])
        acc_out[...] += _tn(m_b, dx2_b)
        dy_att = dm * ga
        dy_conv = dm * gv
        dpb_ref[:, 5 * dc:5 * dc + d] = (dy_att * y_att * (1.0 - ga)).astype(BF16)
        dpb_ref[:, 5 * dc + d:5 * dc + 2 * d] = (dy_conv * y_conv * (1.0 - gv)).astype(BF16)
        dya_b = dy_att.astype(BF16)
        dyc_b = dy_conv.astype(BF16)
        da_in = _nt(dya_b, watt_ref[...])
        acc_att[...] += _tn(a_b, dya_b)
        dc_in = _nt(dyc_b, wconv_ref[...])
        acc_conv[...] += _tn(c_b, dyc_b)

        do_ref[...] = (da_in * silu_a).astype(BF16)
        dpb_ref[:, 0:dc] = (da_in * att_v * (sa * (1.0 + za * (1.0 - sa)))).astype(BF16)
        dpb_ref[:, dc:2 * dc] = (dc_in * vconv * silu_c).astype(BF16)
        dgs = dc_in * gb
        dvc = dgs * silu_c
        dpb_ref[:, 4 * dc:5 * dc] = (dgs * vconv * (sc * (1.0 + zc * (1.0 - sc)))).astype(BF16)
        gcb_o[...] += jnp.sum(dvc, axis=0, keepdims=True)
        gcw_o[0:1, :] += jnp.sum(dvc * cu2, axis=0, keepdims=True)
        gcw_o[1:2, :] += jnp.sum(dvc * cu1, axis=0, keepdims=True)
        gcw_o[2:3, :] += jnp.sum(dvc * cu, axis=0, keepdims=True)
        nxt = carry[...]
        dcu = w2 * dvc + w1 * _shift_up(dvc, 1, nxt) + w0 * _shift_up(dvc, 2, nxt)
        carry[...] = dvc[0:SUBLANES, :]
        dpb_ref[:, 2 * dc:3 * dc] = (dcu * u).astype(BF16)
        dpb_ref[:, 3 * dc:4 * dc] = (dcu * gc).astype(BF16)

        @pl.when(i == n - 1)
        def _():
            for j in range(n_shards):
                gatt_o[j] = acc_att[:, j * (d // n_shards):(j + 1) * (d // n_shards)].astype(BF16)
                gconv_o[j] = acc_conv[:, j * (d // n_shards):(j + 1) * (d // n_shards)].astype(BF16)
                gout_o[j] = acc_out[j * (d // n_shards):(j + 1) * (d // n_shards), :].astype(BF16)

    rev = lambda width, col_blk: pl.BlockSpec((tm, width), lambda i: (n - 1 - i, col_blk))
    grp = lambda g: pl.BlockSpec((None, tm, dc), lambda i: (g, n - 1 - i, 0))
    grp2 = lambda g2: pl.BlockSpec((2, tm, dc), lambda i: (g2, n - 1 - i, 0))
    halo_spec = lambda g: pl.BlockSpec(
        (None, SUBLANES, dc), lambda i: (g, jnp.maximum((n - 1 - i) * (tm // SUBLANES) - 1, 0), 0))
    const = lambda shape: pl.BlockSpec(shape, lambda i: tuple(0 for _ in shape), pipeline_mode=pl.Buffered(1))
    q4 = d // n_shards
    return pl.pallas_call(
        body, name="mixer_mid", grid=(n,),
        in_specs=[rev(dc, 0), grp(3), grp(4), grp(5), grp(6), grp(7),
                  halo_spec(5), halo_spec(6), grp2(4), grp2(5), rev(d, 0), rev(d, 0),
                  const((dc, d)), const((dc, d)), const((d, d)), const((SUBLANES, dc)), const((1, dc)), const((1, d))],
        out_specs=[rev(5 * dc + 2 * d, 0), rev(dc, 0), rev(d, 0),
                   const((n_shards, dc, q4)), const((n_shards, dc, q4)), const((n_shards, q4, d)),
                   const((1, d)), const((1, d)), const((1, dc)), const((SUBLANES, dc))],
        out_shape=[jax.ShapeDtypeStruct((s, 5 * dc + 2 * d), BF16), jax.ShapeDtypeStruct((s, dc), BF16),
                   jax.ShapeDtypeStruct((s, d), F32),
                   jax.ShapeDtypeStruct((n_shards, dc, q4), BF16), jax.ShapeDtypeStruct((n_shards, dc, q4), BF16),
                   jax.ShapeDtypeStruct((n_shards, q4, d), BF16),
                   jax.ShapeDtypeStruct((1, d), F32), jax.ShapeDtypeStruct((1, d), F32),
                   jax.ShapeDtypeStruct((1, dc), F32), jax.ShapeDtypeStruct((SUBLANES, dc), F32)],
        scratch_shapes=[pltpu.VMEM((dc, d), F32), pltpu.VMEM((dc, d), F32), pltpu.VMEM((d, d), F32),
                        pltpu.VMEM((SUBLANES, dc), F32)],
        compiler_params=_params("arbitrary"),
    )(att, proj, proj, proj, proj, proj, proj, proj, proj, proj, x, tgt,
      w_att, w_conv, w_out, conv_w8, conv_b, fin_g)


def _in_proj_bwd_x(dqkv, dpb, w_in, x, dx2, g):
    s, d = x.shape
    tn = dqkv.shape[1]
    nb = dpb.shape[1] // tn
    n = s // TM_MM

    def body(*refs):
        dps, ws = refs[:nb + 1], refs[nb + 1:2 * nb + 2]
        x_ref, dx2_ref, g_ref, gx_ref, gng_ref = refs[2 * nb + 2:]
        i = pl.program_id(0)

        @pl.when(i == 0)
        def _():
            gng_ref[...] = jnp.zeros_like(gng_ref)

        dh = _nt(dps[0][...], ws[0][0])
        for j in range(1, nb + 1):
            dh = dh + _nt(dps[j][...], ws[j][0])
        xv = x_ref[...]
        r = lax.rsqrt(jnp.mean(xv * xv, axis=-1, keepdims=True) + EPS)
        xn = xv * r
        gng_ref[...] += jnp.sum(dh * xn, axis=0, keepdims=True)
        dhn = dh * g_ref[...]
        gx_ref[...] = dx2_ref[...] + r * (dhn - xn * jnp.mean(dhn * xn, axis=-1, keepdims=True))

    tile = lambda width, col_blk: pl.BlockSpec((TM_MM, width), lambda i: (i, col_blk))
    wspec = lambda blk: pl.BlockSpec((1, d, tn), lambda i: (blk, 0, 0), pipeline_mode=pl.Buffered(1))
    return pl.pallas_call(
        body, name="in_proj_bwd_x", grid=(n,),
        in_specs=[tile(tn, 0)] + [tile(tn, j) for j in range(nb)] + [wspec(j) for j in range(nb + 1)]
        + [tile(d, 0), tile(d, 0), pl.BlockSpec((1, d), lambda i: (0, 0))],
        out_specs=[tile(d, 0), pl.BlockSpec((1, d), lambda i: (0, 0))],
        out_shape=[jax.ShapeDtypeStruct((s, d), F32), jax.ShapeDtypeStruct((1, d), F32)],
        compiler_params=_params("arbitrary"),
    )(dqkv, *([dpb] * nb), *([w_in] * (nb + 1)), x, dx2, g)


def _in_proj_bwd_w(h, dqkv, dpb, order):
    s, d = h.shape
    tn = dqkv.shape[1]
    n = s // TM_BLK
    hr = d // 2
    settle = min(2, n - 1)

    def body(order_ref, h_ref, da_ref, db_ref, slots_ref, acc, sendbuf, pairbuf, chipbuf, psend, precv, send, recv, lsem):
        j, i = pl.program_id(0), pl.program_id(1)
        blk = order_ref[j]
        pos = _position()
        x, y, c = pos

        @pl.when(i == 0)
        def _():
            acc[...] = jnp.zeros_like(acc)

        @pl.when(blk == 0)
        def _():
            acc[...] += _tn(h_ref[...], da_ref[...])

        @pl.when(blk > 0)
        def _():
            acc[...] += _tn(h_ref[...], db_ref[...])

        def pair(step, half):
            return pltpu.make_async_remote_copy(
                src_ref=sendbuf.at[step, pl.ds(half * hr, hr), :], dst_ref=pairbuf.at[step],
                send_sem=psend.at[step], recv_sem=precv.at[step], device_id=(x, y, 1 - c), device_id_type=MESH)

        def ici(step):
            flip = OWNER_FLIPS[step]
            return pltpu.make_async_remote_copy(
                src_ref=chipbuf.at[step], dst_ref=slots_ref.at[flip], send_sem=send.at[step], recv_sem=recv.at[step],
                device_id=_peer(pos, 4 * (flip >> 1) + 2 * (flip & 1)), device_id_type=MESH)

        local = pltpu.make_async_copy(chipbuf.at[N_CHIPS - 1], slots_ref.at[0], lsem.at[0])

        def combine(step):
            pair(step, c).wait_recv()
            mine = sendbuf[step, pl.ds(c * hr, hr), :].astype(F32)
            chipbuf[step] = (mine + pairbuf[step].astype(F32)).astype(BF16)

        for step in range(N_CHIPS):
            @pl.when((j == step) & (i == n - 1))
            def _(step=step):
                sendbuf[step] = acc[...].astype(BF16)
                pair(step, 1 - c).start()

        for step in range(N_CHIPS - 1):
            @pl.when((j == step + 1) & (i == settle))
            def _(step=step):
                combine(step)
                ici(step).start()

        @pl.when((j == N_CHIPS - 1) & (i == n - 1))
        def _():
            combine(N_CHIPS - 1)
            local.start()
            for step in range(N_CHIPS - 1):
                ici(step).wait_recv()
            for step in range(N_CHIPS - 1):
                ici(step).wait_send()
            for step in range(N_CHIPS):
                pair(step, 1 - c).wait_send()
            local.wait()

    return pl.pallas_call(
        body, name="in_proj_bwd_w",
        grid_spec=pltpu.PrefetchScalarGridSpec(
            num_scalar_prefetch=1, grid=(N_CHIPS, n),
            in_specs=[pl.BlockSpec((TM_BLK, d), lambda j, i, order: (i, 0)),
                      pl.BlockSpec((TM_BLK, tn), lambda j, i, order: (jnp.where(order[j] == 0, i, 0), 0)),
                      pl.BlockSpec((TM_BLK, tn), lambda j, i, order: (jnp.where(order[j] == 0, 0, i),
                                                                     jnp.maximum(order[j] - 1, 0)))],
            out_specs=[ANY],
            scratch_shapes=[pltpu.VMEM((d, tn), F32), pltpu.VMEM((N_CHIPS, d, tn), BF16),
                            pltpu.VMEM((N_CHIPS, hr, tn), BF16), pltpu.VMEM((N_CHIPS, hr, tn), BF16),
                            pltpu.SemaphoreType.DMA((N_CHIPS,)), pltpu.SemaphoreType.DMA((N_CHIPS,)),
                            pltpu.SemaphoreType.DMA((N_CHIPS - 1,)), pltpu.SemaphoreType.DMA((N_CHIPS - 1,)),
                            pltpu.SemaphoreType.DMA((1,))]),
        out_shape=[jax.ShapeDtypeStruct((N_CHIPS, hr, tn), BF16)],
        compiler_params=_params("arbitrary", "arbitrary"),
    )(order, h, dqkv, dpb)[0]


LOSS_ROW = 6


def _adam_update(w, g, m, v):
    c1 = 1.0 / (1.0 - ADAM_B1 ** ADAM_STEP)
    c2 = 1.0 / (1.0 - ADAM_B2 ** ADAM_STEP)
    m2 = ADAM_B1 * m + (1.0 - ADAM_B1) * g
    v2 = ADAM_B2 * v + (1.0 - ADAM_B2) * (g * g)
    return -ADAM_LR * ((m2 * c1) / (jnp.sqrt(v2 * c2) + ADAM_EPS) + ADAM_WD * w), m2, v2


def _adamw_small(recv, params, moments_m, moments_v):
    k = recv.shape[0]
    n_par = len(params)
    cshard = params[3].shape[1]

    def body(*refs):
        r_ref = refs[0]
        ws, ms, vs = refs[1:1 + n_par], refs[1 + n_par:1 + 2 * n_par], refs[1 + 2 * n_par:1 + 3 * n_par]
        loss_ref = refs[1 + 3 * n_par]
        outs = refs[2 + 3 * n_par:]
        total = r_ref[0]
        for slot in range(1, k):
            total = total + r_ref[slot]
        loss_ref[...] = jnp.sum(total[LOSS_ROW:LOSS_ROW + 1, :], axis=1, keepdims=True)
        chip = 2 * lax.axis_index("x") + lax.axis_index("y")
        g_cw = jnp.zeros((3, cshard), F32)
        for sh in range(N_CHIPS):
            g_cw = g_cw + jnp.where(chip == sh, total[3:6, sh * cshard:(sh + 1) * cshard], 0.0)
        grads = [total[0:1, :], total[1:2, :], total[2:3, :ws[2].shape[1]], g_cw, total[8:16, :ws[4].shape[1]]]
        for p in range(n_par):
            delta, m2, v2 = _adam_update(ws[p][...], grads[p], ms[p][...], vs[p][...])
            for q, val in enumerate((grads[p], delta, m2, v2)):
                outs[4 * p + q][...] = val

    shapes = [jax.ShapeDtypeStruct((1, 1), F32)]
    for p in params:
        shapes += [jax.ShapeDtypeStruct(p.shape, F32)] * 4
    return pl.pallas_call(body, name="adamw_small", out_shape=shapes)(recv, *params, *moments_m, *moments_v)


def _adamw(w, g, m, v, name, rows_per_step):
    r, c = w.shape

    def body(w_ref, g_ref, m_ref, v_ref, d_ref, mo_ref, vo_ref):
        d_ref[...], mo_ref[...], vo_ref[...] = _adam_update(w_ref[...], g_ref[...], m_ref[...], v_ref[...])

    spec = pl.BlockSpec((rows_per_step, c), lambda i: (i, 0))
    shape = jax.ShapeDtypeStruct((r, c), F32)
    return pl.pallas_call(
        body, name=name, grid=(r // rows_per_step,),
        in_specs=[spec] * 4, out_specs=[spec] * 3, out_shape=[shape] * 3,
        compiler_params=_params("parallel"),
    )(w, g, m, v)


ANY = pl.BlockSpec(memory_space=pl.ANY)
N_CHIPS = 4
N_DEV = 8
OWNER_FLIPS = (3, 1, 2, 0)


def _position():
    return lax.axis_index("x"), lax.axis_index("y"), lax.axis_index("c")


def _gather_out_shapes(shards, kinds, cw8):
    full = [(a.shape[0], a.shape[1] * N_CHIPS) if k == "cols" else (a.shape[0] * N_CHIPS, a.shape[1])
            for a, k in zip(shards, kinds)]
    return [jax.ShapeDtypeStruct(f, BF16) for f in full] + [
        jax.ShapeDtypeStruct((N_CHIPS,) + cw8.shape, cw8.dtype)]


def _gather_sems(nw):
    return [pltpu.SemaphoreType.DMA((3, nw)), pltpu.SemaphoreType.DMA((3, nw)),
            pltpu.SemaphoreType.DMA((3, nw)), pltpu.SemaphoreType.DMA((3, nw)),
            pltpu.SemaphoreType.DMA((3,)), pltpu.SemaphoreType.DMA((3,)), pltpu.SemaphoreType.DMA((nw + 1,))]


def _gather_plan(kinds, srcs, dsts, cw, cw_all, send1, recv1, send2, recv2, ssend, srecv, lsem):
    nw = len(srcs)
    x, y, c = _position()
    mine = 2 * x + y
    chips = [(x, 1 - y), (1 - x, y), (1 - x, 1 - y)]

    def window(w, shard, half):
        r, cc = srcs[w].shape
        hr = r // 2
        if kinds[w] == "cols":
            rows = pl.ds(0, r) if half is None else pl.ds(half * hr, hr)
            return dsts[w].at[rows, pl.ds(shard * cc, cc)]
        rows = pl.ds(shard * r, r) if half is None else pl.ds(shard * r + half * hr, hr)
        return dsts[w].at[rows, :]

    def my_half(w):
        hr = srcs[w].shape[0] // 2
        return srcs[w].at[pl.ds(c * hr, hr), :]

    def local():
        return [pltpu.make_async_copy(srcs[w], window(w, mine, None), lsem.at[w]) for w in range(nw)] + [
            pltpu.make_async_copy(cw, cw_all.at[mine], lsem.at[nw])]

    def ici(k, w, shard):
        kx, ky = chips[k]
        return pltpu.make_async_remote_copy(
            src_ref=my_half(w), dst_ref=window(w, shard, c), send_sem=send1.at[k, w], recv_sem=recv1.at[k, w],
            device_id=(kx, ky, c), device_id_type=MESH)

    def d2d(k, w, shard, half):
        return pltpu.make_async_remote_copy(
            src_ref=window(w, shard, half), dst_ref=window(w, shard, half),
            send_sem=send2.at[k, w], recv_sem=recv2.at[k, w], device_id=(x, y, 1 - c), device_id_type=MESH)

    def small(k, shard):
        kx, ky = chips[k]
        return pltpu.make_async_remote_copy(
            src_ref=cw, dst_ref=cw_all.at[shard], send_sem=ssend.at[k], recv_sem=srecv.at[k],
            device_id=(kx, ky, c), device_id_type=MESH)

    def theirs(k):
        kx, ky = chips[k]
        return 2 * kx + ky

    def start():
        for cp in local():
            cp.start()
        for k in range(3):
            for w in range(nw):
                ici(k, w, mine).start()
            small(k, mine).start()

    def forward():
        for k in range(3):
            for w in range(nw):
                ici(k, w, theirs(k)).wait_recv()
                d2d(k, w, theirs(k), c).start()

    def finish():
        for k in range(3):
            for w in range(nw):
                d2d(k, w, theirs(k), 1 - c).wait_recv()
            small(k, theirs(k)).wait_recv()
        for k in range(3):
            for w in range(nw):
                ici(k, w, mine).wait_send()
                d2d(k, w, theirs(k), c).wait_send()
            small(k, mine).wait_send()
        for cp in local():
            cp.wait()

    return start, forward, finish


def _scatter_out_shapes(parts):
    return [jax.ShapeDtypeStruct((N_DEV, p.shape[1] // 2, p.shape[2]), p.dtype) for p in parts]


def _scatter_sems(nw):
    return [pltpu.SemaphoreType.DMA((N_DEV, nw)), pltpu.SemaphoreType.DMA((N_DEV, nw)), pltpu.SemaphoreType.DMA((nw,))]


def _peer(pos, k):
    x, y, c = pos
    return ((1 - x) if k & 4 else x, (1 - y) if k & 2 else y, (1 - c) if k & 1 else c)


def _scatter_plan(srcs, dsts, send, recv, lsem):
    nw = len(srcs)
    pos = _position()

    def piece(w, k):
        px, py, pc = _peer(pos, k)
        hr = srcs[w].shape[1] // 2
        return srcs[w].at[2 * px + py, pl.ds(pc * hr, hr), :]

    def remote(w, k):
        return pltpu.make_async_remote_copy(
            src_ref=piece(w, k), dst_ref=dsts[w].at[k], send_sem=send.at[k, w], recv_sem=recv.at[k, w],
            device_id=_peer(pos, k), device_id_type=MESH)

    def local(w):
        return pltpu.make_async_copy(piece(w, 0), dsts[w].at[0], lsem.at[w])

    def start():
        for w in range(nw):
            local(w).start()
        for k in range(1, N_DEV):
            for w in range(nw):
                remote(w, k).start()

    def finish():
        for k in range(1, N_DEV):
            for w in range(nw):
                remote(w, k).wait_recv()
        for k in range(1, N_DEV):
            for w in range(nw):
                remote(w, k).wait_send()
        for w in range(nw):
            local(w).wait()

    return start, finish


def _reduce_pair(slots, small):
    nw = len(slots)

    def body(*refs):
        srcs, sm = refs[:nw], refs[nw]
        dsts, sm_all = refs[nw + 1:2 * nw + 1], refs[2 * nw + 1]
        halves = refs[2 * nw + 2:3 * nw + 2]
        send, recv, ssend, srecv, lsem = refs[3 * nw + 2:]
        pos = _position()
        x, y, c = pos
        me = 4 * x + 2 * y + c

        def rows(w, half):
            hr = halves[w].shape[0]
            return dsts[w].at[pl.ds(half * hr, hr), :]

        def remote(w, half):
            return pltpu.make_async_remote_copy(
                src_ref=halves[w], dst_ref=rows(w, half), send_sem=send.at[w], recv_sem=recv.at[w],
                device_id=(x, y, 1 - c), device_id_type=MESH)

        def bcast(k, slot):
            return pltpu.make_async_remote_copy(
                src_ref=sm, dst_ref=sm_all.at[slot], send_sem=ssend.at[k], recv_sem=srecv.at[k],
                device_id=_peer(pos, k), device_id_type=MESH)

        small_copies = [bcast(k, me) for k in range(1, N_DEV)]
        own_small = pltpu.make_async_copy(sm, sm_all.at[me], lsem.at[nw])
        for cp in small_copies + [own_small]:
            cp.start()
        big = []
        for w in range(nw):
            total = srcs[w][0].astype(F32)
            for k in range(1, srcs[w].shape[0]):
                total = total + srcs[w][k].astype(F32)
            halves[w][...] = total
            big += [remote(w, c), pltpu.make_async_copy(halves[w], rows(w, c), lsem.at[w])]
            big[-2].start()
            big[-1].start()
        for w in range(nw):
            remote(w, 1 - c).wait_recv()
        for k in range(1, N_DEV):
            px, py, pc = _peer(pos, k)
            bcast(k, 4 * px + 2 * py + pc).wait_recv()
        for w in range(nw):
            big[2 * w].wait_send()
            big[2 * w + 1].wait()
        for cp in small_copies:
            cp.wait_send()
        own_small.wait()

    vmem = pl.BlockSpec(memory_space=pltpu.VMEM)
    half_shapes = [(sl.shape[1], sl.shape[2]) for sl in slots]
    return pl.pallas_call(
        body, name="reduce_pair",
        in_specs=[vmem] * (nw + 1), out_specs=[ANY] * (nw + 1),
        out_shape=[jax.ShapeDtypeStruct((2 * r, cc), F32) for r, cc in half_shapes]
        + [jax.ShapeDtypeStruct((N_DEV,) + small.shape, small.dtype)],
        scratch_shapes=[pltpu.VMEM(hs, F32) for hs in half_shapes]
        + [pltpu.SemaphoreType.DMA((nw,)), pltpu.SemaphoreType.DMA((nw,)),
           pltpu.SemaphoreType.DMA((N_DEV,)), pltpu.SemaphoreType.DMA((N_DEV,)),
           pltpu.SemaphoreType.DMA((nw + 1,))],
        compiler_params=pltpu.CompilerParams(vmem_limit_bytes=VMEM_LIMIT),
    )(*slots, small)


def _pad_to(a, rows, cols):
    return jnp.pad(a, ((0, rows - a.shape[0]), (0, cols - a.shape[1])))


def _pack_small(norm_g, fin_g, conv_b, conv_w, loss_vec, rel):
    rows = [_pad_to(norm_g, 1, SMALL_COLS), _pad_to(fin_g, 1, SMALL_COLS), _pad_to(conv_b, 1, SMALL_COLS),
            _pad_to(conv_w, 3, SMALL_COLS), _pad_to(loss_vec, 2, SMALL_COLS), _pad_to(rel, HEADS, SMALL_COLS)]
    return jnp.concatenate(rows, axis=0)


def kernel(x, norm_g, w_in, rel_bias, w_att_out, conv_w, conv_b, w_conv_out, w_out, final_norm_g, loss_target, m_norm_g, m_w_in, m_rel_bias, m_w_att_out, m_conv_w, m_conv_b, m_w_conv_out, m_w_out, m_final_norm_g, v_norm_g, v_w_in, v_rel_bias, v_w_att_out, v_conv_w, v_conv_b, v_w_conv_out, v_w_out, v_final_norm_g):
    xs, tgt = x[0], loss_target[0]
    cshard = conv_w.shape[2]
    chip = 2 * lax.axis_index("x") + lax.axis_index("y")

    shards = [w_in[0], w_att_out[0], w_conv_out[0], w_out[0]]
    cw8 = _pad_to(conv_w[0], SUBLANES, cshard)
    flips = jnp.arange(N_CHIPS, dtype=jnp.int32)
    own_first = jnp.bitwise_xor(chip, flips)
    own_last = jnp.bitwise_xor(chip, jnp.asarray(OWNER_FLIPS, jnp.int32))

    proj, h, wb_in = _in_proj_gather(xs, norm_g, shards[0], own_first)
    diag = jnp.take(rel_bias[0], _diag_rel_index(), axis=1)
    att, lse, wb_att, wb_conv, wb_out, cw_all = _attn_fwd(diag, proj, shards[1:], ["cols", "cols", "rows"], cw8)
    conv_w_full = jnp.transpose(cw_all, (1, 0, 2)).reshape(SUBLANES, N_CHIPS * cshard)
    (dpb, d_att, dx2, g_att_p, g_conv_p, g_out_p, loss_vec, g_fin, g_cb, g_cw) = _mixer_mid(
        att, proj, xs, tgt, wb_att, wb_conv, wb_out, conv_w_full, conv_b, final_norm_g[None, :])
    dqkv, g_rel, r_att, r_conv, r_out = _attn_bwd(diag, proj, d_att, att, lse, [g_att_p, g_conv_p, g_out_p])
    grad_x, g_norm = _in_proj_bwd_x(dqkv, dpb, wb_in, xs, dx2, norm_g)
    r_in = _in_proj_bwd_w(h, dqkv, dpb, own_last)

    small = _pack_small(g_norm, g_fin, g_cb, g_cw[0:3], loss_vec, g_rel)
    gw_in, gw_att, gw_conv, gw_out, r_small = _reduce_pair([r_in, r_att, r_conv, r_out], small)
    small_out = _adamw_small(
        r_small,
        [norm_g, final_norm_g[None, :], conv_b, conv_w[0], rel_bias[0]],
        [m_norm_g, m_final_norm_g[None, :], m_conv_b, m_conv_w[0], m_rel_bias[0]],
        [v_norm_g, v_final_norm_g[None, :], v_conv_b, v_conv_w[0], v_rel_bias[0]])
    loss = small_out[0][0, 0]
    small_names = ["norm_g", "final_norm_g", "conv_b", "conv_w", "rel_bias"]
    fix = {"norm_g": lambda a: a, "final_norm_g": lambda a: a[0], "conv_b": lambda a: a,
           "conv_w": lambda a: a[None], "rel_bias": lambda a: a[None]}
    small_res = {name: [fix[name](small_out[1 + 4 * p + q]) for q in range(4)] for p, name in enumerate(small_names)}

    big = {}
    for name, w, g, m, v, rows in (("w_in", w_in, gw_in, m_w_in, v_w_in, 128),
                                   ("w_att_out", w_att_out, gw_att, m_w_att_out, v_w_att_out, 256),
                                   ("w_conv_out", w_conv_out, gw_conv, m_w_conv_out, v_w_conv_out, 256),
                                   ("w_out", w_out, gw_out, m_w_out, v_w_out, 128)):
        dw, mw, vw = _adamw(w[0], g, m[0], v[0], "adamw_" + name, rows)
        big[name] = (g[None], dw[None], mw[None], vw[None])

    order = ["norm_g", "w_in", "rel_bias", "w_att_out", "conv_w", "conv_b", "w_conv_out", "w_out", "final_norm_g"]
    outs = [loss, grad_x[None]]
    for which in range(4):
        for name in order:
            outs.append(big[name][which] if name in big else small_res[name][which])
    return tuple(outs)
```

```python
import numpy as np
import jax
import jax.numpy as jnp
from jax import lax
from jax.experimental import pallas as pl
from jax.experimental.pallas import tpu as pltpu

F32 = jnp.float32
BF16 = jnp.bfloat16
MESH = pl.DeviceIdType.MESH

CHUNK = 64
N_LEFT = 8
HEADS = 8
HEAD_DIM = 64
D_ATT = HEADS * HEAD_DIM
MAX_REL = 128
N_REL = 2 * MAX_REL + 1
EPS = 1e-6
NEG_BIG = -1e30
ADAM_LR, ADAM_B1, ADAM_B2, ADAM_EPS, ADAM_WD, ADAM_STEP = 0.001, 0.9, 0.999, 1e-08, 0.01, 10

LANES = 128
SUBLANES = 8
VMEM_LIMIT = 56 * 1024 * 1024

QB = 2 * CHUNK
KW = N_LEFT * CHUNK + QB
DIAG = KW + QB
TQ = N_LEFT * CHUNK
TM_MID = 256
TM_MM = 512
TM_BLK = 1024
SMALL_ROWS, SMALL_COLS = 16, 1024


def _params(*sem):
    return pltpu.CompilerParams(dimension_semantics=sem, vmem_limit_bytes=VMEM_LIMIT)


def _nt(a, b):
    return lax.dot_general(a, b, (((1,), (1,)), ((), ())), preferred_element_type=F32)


def _tn(a, b):
    return lax.dot_general(a, b, (((0,), (0,)), ((), ())), preferred_element_type=F32)


def _nn(a, b):
    return jnp.dot(a, b, preferred_element_type=F32)


def _diag_rel_index():
    d = np.arange(DIAG)
    diff = np.where(d < KW, d, d - DIAG)
    rel = N_LEFT * CHUNK - diff
    return np.clip(rel, -MAX_REL, MAX_REL) + MAX_REL


def _build_bias(diag_ref, bias_scr):
    r = lax.broadcasted_iota(jnp.int32, (QB, KW), 0) // CHUNK
    s = lax.broadcasted_iota(jnp.int32, (QB, KW), 1) // CHUNK
    allowed = (s >= r) & (s <= r + N_LEFT)
    for h in range(HEADS):
        row = jnp.broadcast_to(diag_ref[h:h + 1, :], (QB, DIAG))
        t = pltpu.roll(row, 0, 1, stride=1, stride_axis=0)
        bias_scr[h // 2, (h % 2) * QB:(h % 2 + 1) * QB, :] = jnp.where(allowed, t[:, :KW], NEG_BIG)


def _stack_heads(a, lane_hi):
    zero = jnp.zeros_like(a)
    return jnp.concatenate([jnp.where(lane_hi, zero, a), jnp.where(lane_hi, a, zero)], axis=0)


def _in_proj_gather(x, g, shard, order):
    s, d = x.shape
    tn = shard.shape[1]
    n = s // TM_BLK
    hr = d // 2

    def body(order_ref, x_ref, g_ref, shard32_ref, proj_ref, h_ref, wfull_ref, shard_ref, hbuf, wbuf,
             send1, recv1, send2, recv2, lsem):
        del order_ref
        j, i = pl.program_id(0), pl.program_id(1)
        x, y, c = _position()
        mine = 2 * x + y
        chips = [(x, 1 - y), (1 - x, y), (1 - x, 1 - y)]

        def theirs(k):
            return 2 * chips[k][0] + chips[k][1]

        def half_rows(half):
            return pl.ds(half * hr, hr)

        def ici(k, shard_index):
            return pltpu.make_async_remote_copy(
                src_ref=shard_ref.at[half_rows(c), :], dst_ref=wfull_ref.at[shard_index, half_rows(c), :],
                send_sem=send1.at[k], recv_sem=recv1.at[k], device_id=(*chips[k], c), device_id_type=MESH)

        def d2d(k, half):
            return pltpu.make_async_remote_copy(
                src_ref=wbuf.at[k % 2, half_rows(half), :], dst_ref=wfull_ref.at[theirs(k), half_rows(half), :],
                send_sem=send2.at[k], recv_sem=recv2.at[k], device_id=(x, y, 1 - c), device_id_type=MESH)

        def load(k, half, sem):
            return pltpu.make_async_copy(wfull_ref.at[theirs(k), half_rows(half), :],
                                         wbuf.at[k % 2, half_rows(half), :], lsem.at[sem])

        own = pltpu.make_async_copy(shard_ref, wfull_ref.at[mine], lsem.at[0])

        def put_proj(block):
            for grp in range(tn // D_ATT):
                proj_ref[grp] = block[:, grp * D_ATT:(grp + 1) * D_ATT].astype(BF16)

        @pl.when((j == 0) & (i == 0))
        def _():
            shard_ref[...] = shard32_ref[...].astype(BF16)
            own.start()
            ici(0, mine).start()
            ici(1, mine).start()

        for k in range(3):
            first = max(n - 3, 0) if k == 0 else min(n // 2, n - 1)

            @pl.when((j == k) & (i == first))
            def _(k=k):
                if k == 0:
                    ici(0, mine).wait_send()
                    ici(1, mine).wait_send()
                    ici(2, mine).start()
                if k == 2:
                    d2d(0, c).wait_send()
                ici(k, theirs(k)).wait_recv()
                load(k, c, 1).start()

            @pl.when((j == k) & (i == min(first + 1, n - 1)))
            def _(k=k):
                load(k, c, 1).wait()
                d2d(k, c).start()

            @pl.when((j == k) & (i == min(first + 2, n - 1)))
            def _(k=k):
                d2d(k, 1 - c).wait_recv()
                load(k, 1 - c, 2).start()

            @pl.when((j == k + 1) & (i == 0))
            def _(k=k):
                load(k, 1 - c, 2).wait()

        @pl.when(j == 0)
        def _():
            xv = x_ref[...]
            r = lax.rsqrt(jnp.mean(xv * xv, axis=-1, keepdims=True) + EPS)
            hv = ((xv * r) * g_ref[...]).astype(BF16)
            hbuf[i] = hv
            h_ref[...] = hv
            put_proj(_nn(hv, shard_ref[...]))

        for k in range(3):
            @pl.when(j == k + 1)
            def _(k=k):
                put_proj(_nn(hbuf[i], wbuf[k % 2]))

        @pl.when((j == 3) & (i == n - 1))
        def _():
            ici(2, mine).wait_send()
            d2d(1, c).wait_send()
            d2d(2, c).wait_send()
            own.wait()

    return pl.pallas_call(
        body, name="in_proj_gather",
        grid_spec=pltpu.PrefetchScalarGridSpec(
            num_scalar_prefetch=1, grid=(N_CHIPS, n),
            in_specs=[pl.BlockSpec((TM_BLK, d), lambda j, i, order: (jnp.where(j == 0, i, n - 1), 0)),
                      pl.BlockSpec((1, d), lambda j, i, order: (0, 0)), pl.BlockSpec(memory_space=pltpu.VMEM)],
            out_specs=[pl.BlockSpec((tn // D_ATT, TM_BLK, D_ATT), lambda j, i, order: (order[j], i, 0)),
                       pl.BlockSpec((TM_BLK, d), lambda j, i, order: (jnp.where(j == 0, i, n - 1), 0)), ANY],
            scratch_shapes=[pltpu.VMEM((d, tn), BF16), pltpu.VMEM((n, TM_BLK, d), BF16), pltpu.VMEM((2, d, tn), BF16),
                            pltpu.SemaphoreType.DMA((3,)), pltpu.SemaphoreType.DMA((3,)),
                            pltpu.SemaphoreType.DMA((3,)), pltpu.SemaphoreType.DMA((3,)), pltpu.SemaphoreType.DMA((3,))]),
        out_shape=[jax.ShapeDtypeStruct((N_CHIPS * tn // D_ATT, s, D_ATT), BF16), jax.ShapeDtypeStruct((s, d), BF16),
                   jax.ShapeDtypeStruct((N_CHIPS, d, tn), BF16)],
        compiler_params=_params("arbitrary", "arbitrary"),
    )(order, x, g, shard)


def _attn_fwd(diag, proj, shards, kinds, cw8):
    s = proj.shape[1]
    n = s // TQ
    nw = len(shards)
    scale = HEAD_DIM ** -0.5

    def body(*refs):
        diag_ref, q_ref, kp_ref, kc_ref, vp_ref, vc_ref = refs[:6]
        srcs, cw = refs[6:6 + nw], refs[6 + nw]
        o_ref, lse_ref = refs[7 + nw:9 + nw]
        dsts, cw_all = refs[9 + nw:9 + 2 * nw], refs[9 + 2 * nw]
        bias_scr = refs[10 + 2 * nw]
        casts = refs[11 + 2 * nw:11 + 3 * nw]
        start, forward, finish = _gather_plan(kinds, casts, dsts, cw, cw_all, *refs[11 + 3 * nw:])
        i = pl.program_id(0)

        @pl.when(i == 0)
        def _():
            for w in range(nw):
                casts[w][...] = srcs[w][...].astype(BF16)
            start()
            _build_bias(diag_ref, bias_scr)

        @pl.when(i == n // 2)
        def _():
            forward()

        @pl.when(i == n - 1)
        def _():
            finish()

        lane_hi = lax.broadcasted_iota(jnp.int32, (QB, LANES), 1) >= HEAD_DIM

        def block(b, first_tile):
            r0, n_prev = b * QB, TQ - b * QB
            n_cur = KW - n_prev
            pairs = range(HEADS // 2)
            lanes_of = [slice(LANES * p, LANES * (p + 1)) for p in pairs]
            scores = []
            for p in pairs:
                lanes = lanes_of[p]
                q2 = _stack_heads(q_ref[r0:r0 + QB, lanes] * scale, lane_hi)
                s_cur = _nt(q2, kc_ref[0:n_cur, lanes]) + bias_scr[p, :, n_prev:KW]
                if first_tile:
                    scores.append(s_cur)
                else:
                    scores.append(jnp.concatenate(
                        [_nt(q2, kp_ref[r0:TQ, lanes]) + bias_scr[p, :, 0:n_prev], s_cur], axis=1))
            probs = []
            for p in pairs:
                sc = scores[p]
                m = jnp.max(sc, axis=1, keepdims=True)
                pe = jnp.exp(sc - m)
                l = jnp.sum(pe, axis=1, keepdims=True)
                probs.append((pe.astype(BF16), l, m))
            for p in pairs:
                lanes = lanes_of[p]
                pb, l, m = probs[p]
                if first_tile:
                    o2 = _nn(pb, vc_ref[0:n_cur, lanes]) / l
                else:
                    o2 = (_nn(pb[:, 0:n_prev], vp_ref[r0:TQ, lanes]) + _nn(pb[:, n_prev:KW], vc_ref[0:n_cur, lanes])) / l
                lse2 = m + jnp.log(l)
                lse_ref[r0:r0 + QB, 2 * p:2 * p + 1] = lse2[0:QB, :]
                lse_ref[r0:r0 + QB, 2 * p + 1:2 * p + 2] = lse2[QB:2 * QB, :]
                o_ref[r0:r0 + QB, lanes] = jnp.where(lane_hi, o2[QB:2 * QB, :], o2[0:QB, :]).astype(BF16)

        @pl.when(i == 0)
        def _():
            for b in range(TQ // QB):
                block(b, True)

        @pl.when(i > 0)
        def _():
            for b in range(TQ // QB):
                block(b, False)

    blk = lambda grp, prev: pl.BlockSpec(
        (None, TQ, D_ATT), (lambda i: (grp, jnp.maximum(i - 1, 0), 0)) if prev else (lambda i: (grp, i, 0)))
    vmem = pl.BlockSpec(memory_space=pltpu.VMEM)
    return pl.pallas_call(
        body, name="attn_fwd", grid=(n,),
        in_specs=[pl.BlockSpec((HEADS, DIAG), lambda i: (0, 0)),
                  blk(0, False), blk(1, True), blk(1, False), blk(2, True), blk(2, False)] + [vmem] * (nw + 1),
        out_specs=[pl.BlockSpec((TQ, D_ATT), lambda i: (i, 0)), pl.BlockSpec((TQ, HEADS), lambda i: (i, 0))]
        + [ANY] * (nw + 1),
        out_shape=[jax.ShapeDtypeStruct((s, D_ATT), BF16), jax.ShapeDtypeStruct((s, HEADS), F32)]
        + _gather_out_shapes(shards, kinds, cw8),
        scratch_shapes=[pltpu.VMEM((HEADS // 2, 2 * QB, KW), F32)] + [pltpu.VMEM(a.shape, BF16) for a in shards]
        + _gather_sems(nw),
        compiler_params=_params("arbitrary"),
    )(diag, proj, proj, proj, proj, proj, *shards, cw8)


def _attn_bwd(diag, proj, d_att, att, lse, parts):
    s = proj.shape[1]
    n = s // TQ
    npart = len(parts)
    scale = HEAD_DIM ** -0.5
    rel_pad = 3 * LANES

    def body(*refs):
        diag_ref, q_ref, kp_ref, kc_ref, vp_ref, vc_ref, do_ref, o_ref, lse_ref = refs[:9]
        part_refs = refs[9:9 + npart]
        dqkv_ref, dbias_ref = refs[9 + npart:11 + npart]
        slot_refs = refs[11 + npart:11 + 2 * npart]
        bias_scr, dbias_acc, dk_acc, dv_acc, dq_scr = refs[11 + 2 * npart:16 + 2 * npart]
        start, finish = _scatter_plan(part_refs, slot_refs, *refs[16 + 2 * npart:])
        i = pl.program_id(0)
        cur, prv = i % 2, 1 - i % 2

        @pl.when(i == 0)
        def _():
            start()
            _build_bias(diag_ref, bias_scr)
            dbias_acc[...] = jnp.zeros_like(dbias_acc)
            dk_acc[...] = jnp.zeros_like(dk_acc)
            dv_acc[...] = jnp.zeros_like(dv_acc)

        @pl.when(i > 0)
        def _():
            dqkv_ref[:, 0:D_ATT] = dq_scr[...]
            dk_acc[cur] = jnp.zeros((TQ, D_ATT), F32)
            dv_acc[cur] = jnp.zeros((TQ, D_ATT), F32)

        lane_hi = lax.broadcasted_iota(jnp.int32, (QB, LANES), 1) >= HEAD_DIM
        col = lax.broadcasted_iota(jnp.int32, (2 * QB, KW), 1)

        def make_block(first_tile):
            def block(b):
                r0, n_prev = b * QB, TQ - b * QB
                n_cur = KW - n_prev
                for p in range(HEADS // 2):
                    lanes = slice(LANES * p, LANES * (p + 1))
                    q2 = _stack_heads(q_ref[r0:r0 + QB, lanes] * scale, lane_hi)
                    kw = jnp.concatenate([kp_ref[r0:TQ, lanes], kc_ref[0:n_cur, lanes]], axis=0)
                    vw = jnp.concatenate([vp_ref[r0:TQ, lanes], vc_ref[0:n_cur, lanes]], axis=0)
                    dop = do_ref[r0:r0 + QB, lanes]
                    do2 = _stack_heads(dop, lane_hi)
                    prod = dop.astype(F32) * o_ref[r0:r0 + QB, lanes].astype(F32)
                    delta2 = jnp.concatenate(
                        [jnp.sum(jnp.where(lane_hi, 0.0, prod), axis=1, keepdims=True),
                         jnp.sum(jnp.where(lane_hi, prod, 0.0), axis=1, keepdims=True)], axis=0)
                    lse2 = jnp.concatenate([lse_ref[r0:r0 + QB, 2 * p:2 * p + 1],
                                            lse_ref[r0:r0 + QB, 2 * p + 1:2 * p + 2]], axis=0)
                    sc = _nt(q2, kw) + bias_scr[p]
                    if first_tile:
                        sc = jnp.where(col >= TQ - r0, sc, NEG_BIG)
                    pr = jnp.exp(sc - lse2)
                    ds = pr * (_nt(do2, vw) - delta2)
                    dbias_acc[p] += ds
                    dsb = ds.astype(BF16)
                    dv_w = _tn(pr.astype(BF16), do2)
                    dk_w = _tn(dsb, q2)
                    dv_acc[prv, r0:TQ, lanes] += dv_w[0:n_prev, :]
                    dv_acc[cur, 0:n_cur, lanes] += dv_w[n_prev:KW, :]
                    dk_acc[prv, r0:TQ, lanes] += dk_w[0:n_prev, :]
                    dk_acc[cur, 0:n_cur, lanes] += dk_w[n_prev:KW, :]
                    dq2 = _nn(dsb, kw)
                    dq = jnp.where(lane_hi, dq2[QB:2 * QB, :], dq2[0:QB, :]) * scale
                    dq_scr[r0:r0 + QB, lanes] = dq.astype(BF16)
            return block

        @pl.when(i == 0)
        def _():
            for b in range(TQ // QB):
                make_block(True)(b)

        @pl.when((i > 0) & (i < n))
        def _():
            for b in range(TQ // QB):
                make_block(False)(b)

        @pl.when(i > 0)
        def _():
            dqkv_ref[:, D_ATT:2 * D_ATT] = dk_acc[prv].astype(BF16)
            dqkv_ref[:, 2 * D_ATT:3 * D_ATT] = dv_acc[prv].astype(BF16)

        @pl.when(i == n)
        def _():
            d_iota = lax.broadcasted_iota(jnp.int32, (DIAG, rel_pad), 0)
            n_iota = lax.broadcasted_iota(jnp.int32, (DIAG, rel_pad), 1)
            diff = jnp.where(d_iota < KW, d_iota, d_iota - DIAG)
            idx = jnp.clip(N_LEFT * CHUNK - diff, -MAX_REL, MAX_REL) + MAX_REL
            onehot = (idx == n_iota).astype(F32)
            rows = []
            for hd in range(HEADS):
                acc = dbias_acc[hd // 2, (hd % 2) * QB:(hd % 2 + 1) * QB, :]
                a = jnp.concatenate([acc, jnp.zeros((QB, DIAG - KW), F32)], axis=1)
                g8 = a[0:SUBLANES, :]
                for blk in range(1, QB // SUBLANES):
                    g8 = g8 + pltpu.roll(a[blk * SUBLANES:(blk + 1) * SUBLANES, :], DIAG - blk * SUBLANES, 1)
                g1 = g8[0:1, :]
                for r in range(1, SUBLANES):
                    g1 = g1 + pltpu.roll(g8[r:r + 1, :], DIAG - r, 1)
                rows.append(g1)
            g = jnp.concatenate(rows, axis=0)
            dbias_ref[...] = jnp.dot(g, onehot, preferred_element_type=F32, precision=lax.Precision.HIGHEST)
            finish()

    last = n - 1
    cur = lambda grp: pl.BlockSpec((None, TQ, D_ATT), lambda i: (grp, jnp.minimum(i, last), 0))
    prev = lambda grp: pl.BlockSpec((None, TQ, D_ATT), lambda i: (grp, jnp.maximum(jnp.minimum(i, last) - 1, 0), 0))
    tile = pl.BlockSpec((TQ, D_ATT), lambda i: (jnp.minimum(i, last), 0))
    return pl.pallas_call(
        body, name="attn_bwd", grid=(n + 1,),
        in_specs=[pl.BlockSpec((HEADS, DIAG), lambda i: (0, 0)),
                  cur(0), prev(1), cur(1), prev(2), cur(2), tile, tile,
                  pl.BlockSpec((TQ, HEADS), lambda i: (jnp.minimum(i, last), 0))] + [ANY] * npart,
        out_specs=[pl.BlockSpec((TQ, 3 * D_ATT), lambda i: (jnp.maximum(i - 1, 0), 0)),
                   pl.BlockSpec((HEADS, rel_pad), lambda i: (0, 0))] + [ANY] * npart,
        out_shape=[jax.ShapeDtypeStruct((s, 3 * D_ATT), BF16), jax.ShapeDtypeStruct((HEADS, rel_pad), F32)]
        + _scatter_out_shapes(parts),
        scratch_shapes=[pltpu.VMEM((HEADS // 2, 2 * QB, KW), F32), pltpu.VMEM((HEADS // 2, 2 * QB, KW), F32),
                        pltpu.VMEM((2, TQ, D_ATT), F32), pltpu.VMEM((2, TQ, D_ATT), F32),
                        pltpu.VMEM((TQ, D_ATT), BF16)] + _scatter_sems(npart),
        compiler_params=_params("arbitrary"),
    )(diag, proj, proj, proj, proj, proj, d_att, att, lse, *parts)


def _shift_down(a, k, halo):
    rolled = pltpu.roll(a, k, 0)
    row = lax.broadcasted_iota(jnp.int32, halo.shape, 0)
    first = jnp.where(row < k, pltpu.roll(halo, k, 0), rolled[0:SUBLANES, :])
    return jnp.concatenate([first, rolled[SUBLANES:, :]], axis=0)


def _shift_up(a, k, nxt):
    tm = a.shape[0]
    rolled = pltpu.roll(a, tm - k, 0)
    row = lax.broadcasted_iota(jnp.int32, nxt.shape, 0)
    last = jnp.where(row >= SUBLANES - k, pltpu.roll(nxt, SUBLANES - k, 0), rolled[tm - SUBLANES:, :])
    return jnp.concatenate([rolled[:tm - SUBLANES, :], last], axis=0)


def _sigmoid(v):
    return 0.5 * jnp.tanh(0.5 * v) + 0.5


def _mixer_mid(att, proj, x, tgt, w_att, w_conv, w_out, w_att_t, w_conv_t, w_out_t, conv_w8, conv_b, fin_g):
    s, d = x.shape
    dc = D_ATT
    n = s // TM_MID
    tm = TM_MID
    n_shards = 4

    def body(att_ref, za_ref, gb_ref, gc_ref, u_ref, zc_ref, hgc_ref, hu_ref, gatt_ref, gconv_ref, x_ref, t_ref,
             watt_ref, wconv_ref, wout_ref, watt_t_ref, wconv_t_ref, wout_t_ref, cw_ref, cb_ref, fg_ref,
             dpb_ref, do_ref, dx2_ref, gatt_o, gconv_o, gout_o, loss_o, gfn_o, gcb_o, gcw_o,
             acc_att, acc_conv, acc_out, carry):
        i = pl.program_id(0)
        tile = n - 1 - i

        @pl.when(i == 0)
        def _():
            acc_att[...] = jnp.zeros_like(acc_att)
            acc_conv[...] = jnp.zeros_like(acc_conv)
            acc_out[...] = jnp.zeros_like(acc_out)
            carry[...] = jnp.zeros_like(carry)
            loss_o[...] = jnp.zeros_like(loss_o)
            gfn_o[...] = jnp.zeros_like(gfn_o)
            gcb_o[...] = jnp.zeros_like(gcb_o)
            gcw_o[...] = jnp.zeros_like(gcw_o)

        halves = [slice(hh * (tm // 2), (hh + 1) * (tm // 2)) for hh in range(2)]
        both = lambda fn: [fn(rows) for rows in halves]
        f32 = lambda ref, rows: ref[rows, :].astype(F32)

        gc = gc_ref[...].astype(F32)
        u = u_ref[...].astype(F32)
        cu = gc * u
        halo = jnp.where(tile > 0, hgc_ref[...].astype(F32) * hu_ref[...].astype(F32), 0.0)
        cu1 = _shift_down(cu, 1, halo)
        cu2 = _shift_down(cu, 2, halo)
        w0, w1, w2 = cw_ref[0:1, :], cw_ref[1:2, :], cw_ref[2:3, :]
        fg = fg_ref[...]

        def stage_a(rows):
            att_v, za, zc, gb = f32(att_ref, rows), f32(za_ref, rows), f32(zc_ref, rows), f32(gb_ref, rows)
            sa = _sigmoid(za)
            silu_a = za * sa
            vconv = w0 * cu2[rows, :] + w1 * cu1[rows, :] + w2 * cu[rows, :] + cb_ref[...]
            sc = _sigmoid(zc)
            silu_c = zc * sc
            return dict(att_v=att_v, za=za, zc=zc, gb=gb, sa=sa, silu_a=silu_a, vconv=vconv, sc=sc, silu_c=silu_c,
                        a_b=(att_v * silu_a).astype(BF16), c_b=(gb * vconv * silu_c).astype(BF16))

        st = both(stage_a)
        for t in st:
            t["y_att"] = _nn(t["a_b"], watt_ref[...])
            t["y_conv"] = _nn(t["c_b"], wconv_ref[...])
        for t, rows in zip(st, halves):
            gpair = lambda ref: jnp.concatenate([ref[0, rows, :], ref[1, rows, :]], axis=1).astype(F32)
            t["ga"] = _sigmoid(gpair(gatt_ref))
            t["gv"] = _sigmoid(gpair(gconv_ref))
            t["m_b"] = (t["ga"] * t["y_att"] + t["gv"] * t["y_conv"]).astype(BF16)
        for t in st:
            t["mo"] = _nn(t["m_b"], wout_ref[...])
        for t, rows in zip(st, halves):
            x2 = x_ref[rows, :] + t["mo"]
            r2 = lax.rsqrt(jnp.mean(x2 * x2, axis=-1, keepdims=True) + EPS)
            x2n = x2 * r2
            err = x2n * fg - t_ref[rows, :]
            loss_o[...] += jnp.sum(err * err, axis=0, keepdims=True) * (0.5 / d)
            dy = err * (1.0 / d)
            gfn_o[...] += jnp.sum(dy * x2n, axis=0, keepdims=True)
            dyn = dy * fg
            dx2 = r2 * (dyn - x2n * jnp.mean(dyn * x2n, axis=-1, keepdims=True))
            dx2_ref[rows, :] = dx2
            t["dx2_b"] = dx2.astype(BF16)
        for t in st:
            t["dm"] = _nn(t["dx2_b"], wout_t_ref[...])
        whole = lambda key: jnp.concatenate([st[0][key], st[1][key]], axis=0)
        acc_out[...] += _tn(whole("m_b"), whole("dx2_b"))
        for t, rows in zip(st, halves):
            dy_att = t["dm"] * t["ga"]
            dy_conv = t["dm"] * t["gv"]
            dpb_ref[rows, 5 * dc:5 * dc + d] = (dy_att * t["y_att"] * (1.0 - t["ga"])).astype(BF16)
            dpb_ref[rows, 5 * dc + d:5 * dc + 2 * d] = (dy_conv * t["y_conv"] * (1.0 - t["gv"])).astype(BF16)
            t["dya_b"] = dy_att.astype(BF16)
            t["dyc_b"] = dy_conv.astype(BF16)
        for t in st:
            t["da_in"] = _nn(t["dya_b"], watt_t_ref[...])
            t["dc_in"] = _nn(t["dyc_b"], wconv_t_ref[...])
        acc_att[...] += _tn(whole("a_b"), whole("dya_b"))
        acc_conv[...] += _tn(whole("c_b"), whole("dyc_b"))
        for t, rows in zip(st, halves):
            sa, za, sc, zc = t["sa"], t["za"], t["sc"], t["zc"]
            do_ref[rows, :] = (t["da_in"] * t["silu_a"]).astype(BF16)
            dpb_ref[rows, 0:dc] = (t["da_in"] * t["att_v"] * (sa * (1.0 + za * (1.0 - sa)))).astype(BF16)
            dpb_ref[rows, dc:2 * dc] = (t["dc_in"] * t["vconv"] * t["silu_c"]).astype(BF16)
            dgs = t["dc_in"] * t["gb"]
            t["dvc"] = dgs * t["silu_c"]
            dpb_ref[rows, 4 * dc:5 * dc] = (dgs * t["vconv"] * (sc * (1.0 + zc * (1.0 - sc)))).astype(BF16)
        dvc = whole("dvc")
        gcb_o[...] += jnp.sum(dvc, axis=0, keepdims=True)
        gcw_o[0:1, :] += jnp.sum(dvc * cu2, axis=0, keepdims=True)
        gcw_o[1:2, :] += jnp.sum(dvc * cu1, axis=0, keepdims=True)
        gcw_o[2:3, :] += jnp.sum(dvc * cu, axis=0, keepdims=True)
        nxt = carry[...]
        dcu = w2 * dvc + w1 * _shift_up(dvc, 1, nxt) + w0 * _shift_up(dvc, 2, nxt)
        carry[...] = dvc[0:SUBLANES, :]
        dpb_ref[:, 2 * dc:3 * dc] = (dcu * u).astype(BF16)
        dpb_ref[:, 3 * dc:4 * dc] = (dcu * gc).astype(BF16)

        @pl.when(i == n - 1)
        def _():
            for j in range(n_shards):
                gatt_o[j] = acc_att[:, j * (d // n_shards):(j + 1) * (d // n_shards)].astype(BF16)
                gconv_o[j] = acc_conv[:, j * (d // n_shards):(j + 1) * (d // n_shards)].astype(BF16)
                gout_o[j] = acc_out[j * (d // n_shards):(j + 1) * (d // n_shards), :].astype(BF16)

    rev = lambda width, col_blk: pl.BlockSpec((tm, width), lambda i: (n - 1 - i, col_blk))
    grp = lambda g: pl.BlockSpec((None, tm, dc), lambda i: (g, n - 1 - i, 0))
    grp2 = lambda g2: pl.BlockSpec((2, tm, dc), lambda i: (g2, n - 1 - i, 0))
    halo_spec = lambda g: pl.BlockSpec(
        (None, SUBLANES, dc), lambda i: (g, jnp.maximum((n - 1 - i) * (tm // SUBLANES) - 1, 0), 0))
    const = lambda shape: pl.BlockSpec(shape, lambda i: tuple(0 for _ in shape), pipeline_mode=pl.Buffered(1))
    q4 = d // n_shards
    return pl.pallas_call(
        body, name="mixer_mid", grid=(n,),
        in_specs=[rev(dc, 0), grp(3), grp(4), grp(5), grp(6), grp(7),
                  halo_spec(5), halo_spec(6), grp2(4), grp2(5), rev(d, 0), rev(d, 0),
                  const((dc, d)), const((dc, d)), const((d, d)), const((d, dc)), const((d, dc)), const((d, d)),
                  const((SUBLANES, dc)), const((1, dc)), const((1, d))],
        out_specs=[rev(5 * dc + 2 * d, 0), rev(dc, 0), rev(d, 0),
                   const((n_shards, dc, q4)), const((n_shards, dc, q4)), const((n_shards, q4, d)),
                   const((1, d)), const((1, d)), const((1, dc)), const((SUBLANES, dc))],
        out_shape=[jax.ShapeDtypeStruct((s, 5 * dc + 2 * d), BF16), jax.ShapeDtypeStruct((s, dc), BF16),
                   jax.ShapeDtypeStruct((s, d), F32),
                   jax.ShapeDtypeStruct((n_shards, dc, q4), BF16), jax.ShapeDtypeStruct((n_shards, dc, q4), BF16),
                   jax.ShapeDtypeStruct((n_shards, q4, d), BF16),
                   jax.ShapeDtypeStruct((1, d), F32), jax.ShapeDtypeStruct((1, d), F32),
                   jax.ShapeDtypeStruct((1, dc), F32), jax.ShapeDtypeStruct((SUBLANES, dc), F32)],
        scratch_shapes=[pltpu.VMEM((dc, d), F32), pltpu.VMEM((dc, d), F32), pltpu.VMEM((d, d), F32),
                        pltpu.VMEM((SUBLANES, dc), F32)],
        compiler_params=_params("arbitrary"),
    )(att, proj, proj, proj, proj, proj, proj, proj, proj, proj, x, tgt,
      w_att, w_conv, w_out, w_att_t, w_conv_t, w_out_t, conv_w8, conv_b, fin_g)


def _in_proj_bwd_x(dqkv, dpb, w_in, x, dx2, g):
    s, d = x.shape
    tn = dqkv.shape[1]
    nb = dpb.shape[1] // tn
    n = s // TM_MM

    def body(*refs):
        dps, ws = refs[:nb + 1], refs[nb + 1:2 * nb + 2]
        x_ref, dx2_ref, g_ref, gx_ref, gng_ref = refs[2 * nb + 2:]
        i = pl.program_id(0)

        @pl.when(i == 0)
        def _():
            gng_ref[...] = jnp.zeros_like(gng_ref)

        dh = _nt(dps[0][...], ws[0][0])
        for j in range(1, nb + 1):
            dh = dh + _nt(dps[j][...], ws[j][0])
        xv = x_ref[...]
        r = lax.rsqrt(jnp.mean(xv * xv, axis=-1, keepdims=True) + EPS)
        xn = xv * r
        gng_ref[...] += jnp.sum(dh * xn, axis=0, keepdims=True)
        dhn = dh * g_ref[...]
        gx_ref[...] = dx2_ref[...] + r * (dhn - xn * jnp.mean(dhn * xn, axis=-1, keepdims=True))

    tile = lambda width, col_blk: pl.BlockSpec((TM_MM, width), lambda i: (i, col_blk))
    wspec = lambda blk: pl.BlockSpec((1, d, tn), lambda i: (blk, 0, 0), pipeline_mode=pl.Buffered(1))
    return pl.pallas_call(
        body, name="in_proj_bwd_x", grid=(n,),
        in_specs=[tile(tn, 0)] + [tile(tn, j) for j in range(nb)] + [wspec(j) for j in range(nb + 1)]
        + [tile(d, 0), tile(d, 0), pl.BlockSpec((1, d), lambda i: (0, 0))],
        out_specs=[tile(d, 0), pl.BlockSpec((1, d), lambda i: (0, 0))],
        out_shape=[jax.ShapeDtypeStruct((s, d), F32), jax.ShapeDtypeStruct((1, d), F32)],
        compiler_params=_params("arbitrary"),
    )(dqkv, *([dpb] * nb), *([w_in] * (nb + 1)), x, dx2, g)


def _in_proj_bwd_w(h, dqkv, dpb, order):
    s, d = h.shape
    tn = dqkv.shape[1]
    n = s // TM_BLK
    hr = d // 2
    settle = min(2, n - 1)

    def body(order_ref, h_ref, da_ref, db_ref, slots_ref, acc, sendbuf, pairbuf, chipbuf, psend, precv, send, recv, lsem):
        j, i = pl.program_id(0), pl.program_id(1)
        blk = order_ref[j]
        pos = _position()
        x, y, c = pos

        @pl.when(i == 0)
        def _():
            acc[...] = jnp.zeros_like(acc)

        @pl.when(blk == 0)
        def _():
            acc[...] += _tn(h_ref[...], da_ref[...])

        @pl.when(blk > 0)
        def _():
            acc[...] += _tn(h_ref[...], db_ref[...])

        def pair(step, half):
            return pltpu.make_async_remote_copy(
                src_ref=sendbuf.at[step, pl.ds(half * hr, hr), :], dst_ref=pairbuf.at[step],
                send_sem=psend.at[step], recv_sem=precv.at[step], device_id=(x, y, 1 - c), device_id_type=MESH)

        def ici(step):
            flip = OWNER_FLIPS[step]
            return pltpu.make_async_remote_copy(
                src_ref=chipbuf.at[step], dst_ref=slots_ref.at[flip], send_sem=send.at[step], recv_sem=recv.at[step],
                device_id=_peer(pos, 4 * (flip >> 1) + 2 * (flip & 1)), device_id_type=MESH)

        local = pltpu.make_async_copy(chipbuf.at[N_CHIPS - 1], slots_ref.at[0], lsem.at[0])

        def combine(step):
            pair(step, c).wait_recv()
            mine = sendbuf[step, pl.ds(c * hr, hr), :].astype(F32)
            chipbuf[step] = (mine + pairbuf[step].astype(F32)).astype(BF16)

        for step in range(N_CHIPS):
            @pl.when((j == step) & (i == n - 1))
            def _(step=step):
                sendbuf[step] = acc[...].astype(BF16)
                pair(step, 1 - c).start()

        for step in range(N_CHIPS - 1):
            @pl.when((j == step + 1) & (i == settle))
            def _(step=step):
                combine(step)
                ici(step).start()

        @pl.when((j == N_CHIPS - 1) & (i == n - 1))
        def _():
            combine(N_CHIPS - 1)
            local.start()
            for step in range(N_CHIPS - 1):
                ici(step).wait_recv()
            for step in range(N_CHIPS - 1):
                ici(step).wait_send()
            for step in range(N_CHIPS):
                pair(step, 1 - c).wait_send()
            local.wait()

    return pl.pallas_call(
        body, name="in_proj_bwd_w",
        grid_spec=pltpu.PrefetchScalarGridSpec(
            num_scalar_prefetch=1, grid=(N_CHIPS, n),
            in_specs=[pl.BlockSpec((TM_BLK, d), lambda j, i, order: (i, 0)),
                      pl.BlockSpec((TM_BLK, tn), lambda j, i, order: (jnp.where(order[j] == 0, i, 0), 0)),
                      pl.BlockSpec((TM_BLK, tn), lambda j, i, order: (jnp.where(order[j] == 0, 0, i),
                                                                     jnp.maximum(order[j] - 1, 0)))],
            out_specs=[ANY],
            scratch_shapes=[pltpu.VMEM((d, tn), F32), pltpu.VMEM((N_CHIPS, d, tn), BF16),
                            pltpu.VMEM((N_CHIPS, hr, tn), BF16), pltpu.VMEM((N_CHIPS, hr, tn), BF16),
                            pltpu.SemaphoreType.DMA((N_CHIPS,)), pltpu.SemaphoreType.DMA((N_CHIPS,)),
                            pltpu.SemaphoreType.DMA((N_CHIPS - 1,)), pltpu.SemaphoreType.DMA((N_CHIPS - 1,)),
                            pltpu.SemaphoreType.DMA((1,))]),
        out_shape=[jax.ShapeDtypeStruct((N_CHIPS, hr, tn), BF16)],
        compiler_params=_params("arbitrary", "arbitrary"),
    )(order, h, dqkv, dpb)[0]


LOSS_ROW = 6


def _adam_update(w, g, m, v):
    c1 = 1.0 / (1.0 - ADAM_B1 ** ADAM_STEP)
    c2 = 1.0 / (1.0 - ADAM_B2 ** ADAM_STEP)
    m2 = ADAM_B1 * m + (1.0 - ADAM_B1) * g
    v2 = ADAM_B2 * v + (1.0 - ADAM_B2) * (g * g)
    return -ADAM_LR * ((m2 * c1) / (jnp.sqrt(v2 * c2) + ADAM_EPS) + ADAM_WD * w), m2, v2


def _adamw_small(recv, params, moments_m, moments_v):
    k = recv.shape[0]
    n_par = len(params)
    cshard = params[3].shape[1]

    def body(*refs):
        r_ref = refs[0]
        ws, ms, vs = refs[1:1 + n_par], refs[1 + n_par:1 + 2 * n_par], refs[1 + 2 * n_par:1 + 3 * n_par]
        loss_ref = refs[1 + 3 * n_par]
        outs = refs[2 + 3 * n_par:]
        total = r_ref[0]
        for slot in range(1, k):
            total = total + r_ref[slot]
        loss_ref[...] = jnp.sum(total[LOSS_ROW:LOSS_ROW + 1, :], axis=1, keepdims=True)
        chip = 2 * lax.axis_index("x") + lax.axis_index("y")
        g_cw = jnp.zeros((3, cshard), F32)
        for sh in range(N_CHIPS):
            g_cw = g_cw + jnp.where(chip == sh, total[3:6, sh * cshard:(sh + 1) * cshard], 0.0)
        grads = [total[0:1, :], total[1:2, :], total[2:3, :ws[2].shape[1]], g_cw, total[8:16, :ws[4].shape[1]]]
        for p in range(n_par):
            delta, m2, v2 = _adam_update(ws[p][...], grads[p], ms[p][...], vs[p][...])
            for q, val in enumerate((grads[p], delta, m2, v2)):
                outs[4 * p + q][...] = val

    shapes = [jax.ShapeDtypeStruct((1, 1), F32)]
    for p in params:
        shapes += [jax.ShapeDtypeStruct(p.shape, F32)] * 4
    return pl.pallas_call(body, name="adamw_small", out_shape=shapes)(recv, *params, *moments_m, *moments_v)


def _adamw(w, g, m, v, name, rows_per_step):
    r, c = w.shape

    def body(w_ref, g_ref, m_ref, v_ref, d_ref, mo_ref, vo_ref):
        d_ref[...], mo_ref[...], vo_ref[...] = _adam_update(w_ref[...], g_ref[...], m_ref[...], v_ref[...])

    spec = pl.BlockSpec((rows_per_step, c), lambda i: (i, 0))
    shape = jax.ShapeDtypeStruct((r, c), F32)
    return pl.pallas_call(
        body, name=name, grid=(r // rows_per_step,),
        in_specs=[spec] * 4, out_specs=[spec] * 3, out_shape=[shape] * 3,
        compiler_params=_params("parallel"),
    )(w, g, m, v)


ANY = pl.BlockSpec(memory_space=pl.ANY)
N_CHIPS = 4
N_DEV = 8
OWNER_FLIPS = (3, 1, 2, 0)


def _position():
    return lax.axis_index("x"), lax.axis_index("y"), lax.axis_index("c")


def _gather_out_shapes(shards, kinds, cw8):
    full = [(a.shape[0], a.shape[1] * N_CHIPS) if k == "cols" else (a.shape[0] * N_CHIPS, a.shape[1])
            for a, k in zip(shards, kinds)]
    return [jax.ShapeDtypeStruct(f, BF16) for f in full] + [
        jax.ShapeDtypeStruct((N_CHIPS,) + cw8.shape, cw8.dtype)]


def _gather_sems(nw):
    return [pltpu.SemaphoreType.DMA((3, nw)), pltpu.SemaphoreType.DMA((3, nw)),
            pltpu.SemaphoreType.DMA((3, nw)), pltpu.SemaphoreType.DMA((3, nw)),
            pltpu.SemaphoreType.DMA((3,)), pltpu.SemaphoreType.DMA((3,)), pltpu.SemaphoreType.DMA((nw + 1,))]


def _gather_plan(kinds, srcs, dsts, cw, cw_all, send1, recv1, send2, recv2, ssend, srecv, lsem):
    nw = len(srcs)
    x, y, c = _position()
    mine = 2 * x + y
    chips = [(x, 1 - y), (1 - x, y), (1 - x, 1 - y)]

    def window(w, shard, half):
        r, cc = srcs[w].shape
        hr = r // 2
        if kinds[w] == "cols":
            rows = pl.ds(0, r) if half is None else pl.ds(half * hr, hr)
            return dsts[w].at[rows, pl.ds(shard * cc, cc)]
        rows = pl.ds(shard * r, r) if half is None else pl.ds(shard * r + half * hr, hr)
        return dsts[w].at[rows, :]

    def my_half(w):
        hr = srcs[w].shape[0] // 2
        return srcs[w].at[pl.ds(c * hr, hr), :]

    def local():
        return [pltpu.make_async_copy(srcs[w], window(w, mine, None), lsem.at[w]) for w in range(nw)] + [
            pltpu.make_async_copy(cw, cw_all.at[mine], lsem.at[nw])]

    def ici(k, w, shard):
        kx, ky = chips[k]
        return pltpu.make_async_remote_copy(
            src_ref=my_half(w), dst_ref=window(w, shard, c), send_sem=send1.at[k, w], recv_sem=recv1.at[k, w],
            device_id=(kx, ky, c), device_id_type=MESH)

    def d2d(k, w, shard, half):
        return pltpu.make_async_remote_copy(
            src_ref=window(w, shard, half), dst_ref=window(w, shard, half),
            send_sem=send2.at[k, w], recv_sem=recv2.at[k, w], device_id=(x, y, 1 - c), device_id_type=MESH)

    def small(k, shard):
        kx, ky = chips[k]
        return pltpu.make_async_remote_copy(
            src_ref=cw, dst_ref=cw_all.at[shard], send_sem=ssend.at[k], recv_sem=srecv.at[k],
            device_id=(kx, ky, c), device_id_type=MESH)

    def theirs(k):
        kx, ky = chips[k]
        return 2 * kx + ky

    def start():
        for cp in local():
            cp.start()
        for k in range(3):
            for w in range(nw):
                ici(k, w, mine).start()
            small(k, mine).start()

    def forward():
        for k in range(3):
            for w in range(nw):
                ici(k, w, theirs(k)).wait_recv()
                d2d(k, w, theirs(k), c).start()

    def finish():
        for k in range(3):
            for w in range(nw):
                d2d(k, w, theirs(k), 1 - c).wait_recv()
            small(k, theirs(k)).wait_recv()
        for k in range(3):
            for w in range(nw):
                ici(k, w, mine).wait_send()
                d2d(k, w, theirs(k), c).wait_send()
            small(k, mine).wait_send()
        for cp in local():
            cp.wait()

    return start, forward, finish


def _scatter_out_shapes(parts):
    return [jax.ShapeDtypeStruct((N_DEV, p.shape[1] // 2, p.shape[2]), p.dtype) for p in parts]


def _scatter_sems(nw):
    return [pltpu.SemaphoreType.DMA((N_DEV, nw)), pltpu.SemaphoreType.DMA((N_DEV, nw)), pltpu.SemaphoreType.DMA((nw,))]


def _peer(pos, k):
    x, y, c = pos
    return ((1 - x) if k & 4 else x, (1 - y) if k & 2 else y, (1 - c) if k & 1 else c)


def _scatter_plan(srcs, dsts, send, recv, lsem):
    nw = len(srcs)
    pos = _position()

    def piece(w, k):
        px, py, pc = _peer(pos, k)
        hr = srcs[w].shape[1] // 2
        return srcs[w].at[2 * px + py, pl.ds(pc * hr, hr), :]

    def remote(w, k):
        return pltpu.make_async_remote_copy(
            src_ref=piece(w, k), dst_ref=dsts[w].at[k], send_sem=send.at[k, w], recv_sem=recv.at[k, w],
            device_id=_peer(pos, k), device_id_type=MESH)

    def local(w):
        return pltpu.make_async_copy(piece(w, 0), dsts[w].at[0], lsem.at[w])

    def start():
        for w in range(nw):
            local(w).start()
        for k in range(1, N_DEV):
            for w in range(nw):
                remote(w, k).start()

    def finish():
        for k in range(1, N_DEV):
            for w in range(nw):
                remote(w, k).wait_recv()
        for k in range(1, N_DEV):
            for w in range(nw):
                remote(w, k).wait_send()
        for w in range(nw):
            local(w).wait()

    return start, finish


def _reduce_pair(slots, small):
    nw = len(slots)

    def body(*refs):
        srcs, sm = refs[:nw], refs[nw]
        dsts, sm_all = refs[nw + 1:2 * nw + 1], refs[2 * nw + 1]
        halves = refs[2 * nw + 2:3 * nw + 2]
        send, recv, ssend, srecv, lsem = refs[3 * nw + 2:]
        pos = _position()
        x, y, c = pos
        me = 4 * x + 2 * y + c

        def rows(w, half):
            hr = halves[w].shape[0]
            return dsts[w].at[pl.ds(half * hr, hr), :]

        def remote(w, half):
            return pltpu.make_async_remote_copy(
                src_ref=halves[w], dst_ref=rows(w, half), send_sem=send.at[w], recv_sem=recv.at[w],
                device_id=(x, y, 1 - c), device_id_type=MESH)

        def bcast(k, slot):
            return pltpu.make_async_remote_copy(
                src_ref=sm, dst_ref=sm_all.at[slot], send_sem=ssend.at[k], recv_sem=srecv.at[k],
                device_id=_peer(pos, k), device_id_type=MESH)

        small_copies = [bcast(k, me) for k in range(1, N_DEV)]
        own_small = pltpu.make_async_copy(sm, sm_all.at[me], lsem.at[nw])
        for cp in small_copies + [own_small]:
            cp.start()
        big = []
        for w in range(nw):
            total = srcs[w][0].astype(F32)
            for k in range(1, srcs[w].shape[0]):
                total = total + srcs[w][k].astype(F32)
            halves[w][...] = total
            big += [remote(w, c), pltpu.make_async_copy(halves[w], rows(w, c), lsem.at[w])]
            big[-2].start()
            big[-1].start()
        for w in range(nw):
            remote(w, 1 - c).wait_recv()
        for k in range(1, N_DEV):
            px, py, pc = _peer(pos, k)
            bcast(k, 4 * px + 2 * py + pc).wait_recv()
        for w in range(nw):
            big[2 * w].wait_send()
            big[2 * w + 1].wait()
        for cp in small_copies:
            cp.wait_send()
        own_small.wait()

    vmem = pl.BlockSpec(memory_space=pltpu.VMEM)
    half_shapes = [(sl.shape[1], sl.shape[2]) for sl in slots]
    return pl.pallas_call(
        body, name="reduce_pair",
        in_specs=[vmem] * (nw + 1), out_specs=[ANY] * (nw + 1),
        out_shape=[jax.ShapeDtypeStruct((2 * r, cc), F32) for r, cc in half_shapes]
        + [jax.ShapeDtypeStruct((N_DEV,) + small.shape, small.dtype)],
        scratch_shapes=[pltpu.VMEM(hs, F32) for hs in half_shapes]
        + [pltpu.SemaphoreType.DMA((nw,)), pltpu.SemaphoreType.DMA((nw,)),
           pltpu.SemaphoreType.DMA((N_DEV,)), pltpu.SemaphoreType.DMA((N_DEV,)),
           pltpu.SemaphoreType.DMA((nw + 1,))],
        compiler_params=pltpu.CompilerParams(vmem_limit_bytes=VMEM_LIMIT),
    )(*slots, small)


def _pad_to(a, rows, cols):
    return jnp.pad(a, ((0, rows - a.shape[0]), (0, cols - a.shape[1])))


def _pack_small(norm_g, fin_g, conv_b, conv_w, loss_vec, rel):
    rows = [_pad_to(norm_g, 1, SMALL_COLS), _pad_to(fin_g, 1, SMALL_COLS), _pad_to(conv_b, 1, SMALL_COLS),
            _pad_to(conv_w, 3, SMALL_COLS), _pad_to(loss_vec, 2, SMALL_COLS), _pad_to(rel, HEADS, SMALL_COLS)]
    return jnp.concatenate(rows, axis=0)


def kernel(x, norm_g, w_in, rel_bias, w_att_out, conv_w, conv_b, w_conv_out, w_out, final_norm_g, loss_target, m_norm_g, m_w_in, m_rel_bias, m_w_att_out, m_conv_w, m_conv_b, m_w_conv_out, m_w_out, m_final_norm_g, v_norm_g, v_w_in, v_rel_bias, v_w_att_out, v_conv_w, v_conv_b, v_w_conv_out, v_w_out, v_final_norm_g):
    xs, tgt = x[0], loss_target[0]
    cshard = conv_w.shape[2]
    chip = 2 * lax.axis_index("x") + lax.axis_index("y")

    shards = [w_in[0], w_att_out[0], w_conv_out[0], w_out[0]]
    cw8 = _pad_to(conv_w[0], SUBLANES, cshard)
    flips = jnp.arange(N_CHIPS, dtype=jnp.int32)
    own_first = jnp.bitwise_xor(chip, flips)
    own_last = jnp.bitwise_xor(chip, jnp.asarray(OWNER_FLIPS, jnp.int32))

    proj, h, wb_in = _in_proj_gather(xs, norm_g, shards[0], own_first)
    diag = jnp.take(rel_bias[0], _diag_rel_index(), axis=1)
    att, lse, wb_att, wb_conv, wb_out, cw_all = _attn_fwd(diag, proj, shards[1:], ["cols", "cols", "rows"], cw8)
    conv_w_full = jnp.transpose(cw_all, (1, 0, 2)).reshape(SUBLANES, N_CHIPS * cshard)
    (dpb, d_att, dx2, g_att_p, g_conv_p, g_out_p, loss_vec, g_fin, g_cb, g_cw) = _mixer_mid(
        att, proj, xs, tgt, wb_att, wb_conv, wb_out, wb_att.T, wb_conv.T, wb_out.T, conv_w_full, conv_b,
        final_norm_g[None, :])
    dqkv, g_rel, r_att, r_conv, r_out = _attn_bwd(diag, proj, d_att, att, lse, [g_att_p, g_conv_p, g_out_p])
    grad_x, g_norm = _in_proj_bwd_x(dqkv, dpb, wb_in, xs, dx2, norm_g)
    r_in = _in_proj_bwd_w(h, dqkv, dpb, own_last)

    small = _pack_small(g_norm, g_fin, g_cb, g_cw[0:3], loss_vec, g_rel)
    gw_in, gw_att, gw_conv, gw_out, r_small = _reduce_pair([r_in, r_att, r_conv, r_out], small)
    small_out = _adamw_small(
        r_small,
        [norm_g, final_norm_g[None, :], conv_b, conv_w[0], rel_bias[0]],
        [m_norm_g, m_final_norm_g[None, :], m_conv_b, m_conv_w[0], m_rel_bias[0]],
        [v_norm_g, v_final_norm_g[None, :], v_conv_b, v_conv_w[0], v_rel_bias[0]])
    loss = small_out[0][0, 0]
    small_names = ["norm_g", "final_norm_g", "conv_b", "conv_w", "rel_bias"]
    fix = {"norm_g": lambda a: a, "final_norm_g": lambda a: a[0], "conv_b": lambda a: a,
           "conv_w": lambda a: a[None], "rel_bias": lambda a: a[None]}
    small_res = {name: [fix[name](small_out[1 + 4 * p + q]) for q in range(4)] for p, name in enumerate(small_names)}

    big = {}
    for name, w, g, m, v, rows in (("w_in", w_in, gw_in, m_w_in, v_w_in, 128),
                                   ("w_att_out", w_att_out, gw_att, m_w_att_out, v_w_att_out, 256),
                                   ("w_conv_out", w_conv_out, gw_conv, m_w_conv_out, v_w_conv_out, 256),
                                   ("w_out", w_out, gw_out, m_w_out, v_w_out, 128)):
        dw, mw, vw = _adamw(w[0], g, m[0], v[0], "adamw_" + name, rows)
        big[name] = (g[None], dw[None], mw[None], vw[None])

    order = ["norm_g", "w_in", "rel_bias", "w_att_out", "conv_w", "conv_b", "w_conv_out", "w_out", "final_norm_g"]
    outs = [loss, grad_x[None]]
    for which in range(4):
        for name in order:
            outs.append(big[name][which] if name in big else small_res[name][which])
    return tuple(outs)
```

```python
import numpy as np
import jax
import jax.numpy as jnp
from jax import lax
from jax.experimental import pallas as pl
from jax.experimental.pallas import tpu as pltpu

F32 = jnp.float32
BF16 = jnp.bfloat16
MESH = pl.DeviceIdType.MESH

CHUNK = 64
N_LEFT = 8
HEADS = 8
HEAD_DIM = 64
D_ATT = HEADS * HEAD_DIM
MAX_REL = 128
N_REL = 2 * MAX_REL + 1
EPS = 1e-6
NEG_BIG = -1e30
ADAM_LR, ADAM_B1, ADAM_B2, ADAM_EPS, ADAM_WD, ADAM_STEP = 0.001, 0.9, 0.999, 1e-08, 0.01, 10

LANES = 128
SUBLANES = 8
VMEM_LIMIT = 56 * 1024 * 1024

QB = 2 * CHUNK
KW = N_LEFT * CHUNK + QB
DIAG = KW + QB
TQ = N_LEFT * CHUNK
TM_MID = 256
TM_MM = 512
TM_BLK = 1024
SMALL_ROWS, SMALL_COLS = 16, 1024


def _params(*sem):
    return pltpu.CompilerParams(dimension_semantics=sem, vmem_limit_bytes=VMEM_LIMIT)


def _nt(a, b):
    return lax.dot_general(a, b, (((1,), (1,)), ((), ())), preferred_element_type=F32)


def _tn(a, b):
    return lax.dot_general(a, b, (((0,), (0,)), ((), ())), preferred_element_type=F32)


def _nn(a, b):
    return jnp.dot(a, b, preferred_element_type=F32)


def _diag_rel_index():
    d = np.arange(DIAG)
    diff = np.where(d < KW, d, d - DIAG)
    rel = N_LEFT * CHUNK - diff
    return np.clip(rel, -MAX_REL, MAX_REL) + MAX_REL


def _build_bias(diag_ref, bias_scr):
    r = lax.broadcasted_iota(jnp.int32, (QB, KW), 0) // CHUNK
    s = lax.broadcasted_iota(jnp.int32, (QB, KW), 1) // CHUNK
    allowed = (s >= r) & (s <= r + N_LEFT)
    for h in range(HEADS):
        row = jnp.broadcast_to(diag_ref[h:h + 1, :], (QB, DIAG))
        t = pltpu.roll(row, 0, 1, stride=1, stride_axis=0)
        bias_scr[h // 2, (h % 2) * QB:(h % 2 + 1) * QB, :] = jnp.where(allowed, t[:, :KW], NEG_BIG)


def _stack_heads(a, lane_hi):
    zero = jnp.zeros_like(a)
    return jnp.concatenate([jnp.where(lane_hi, zero, a), jnp.where(lane_hi, a, zero)], axis=0)


def _in_proj_gather(x, g, shard, order):
    s, d = x.shape
    tn = shard.shape[1]
    n = s // TM_BLK
    hr = d // 2

    def body(order_ref, x_ref, g_ref, shard32_ref, proj_ref, h_ref, wfull_ref, shard_ref, hbuf, wbuf,
             send1, recv1, send2, recv2, lsem):
        del order_ref
        j, i = pl.program_id(0), pl.program_id(1)
        x, y, c = _position()
        mine = 2 * x + y
        chips = [(x, 1 - y), (1 - x, y), (1 - x, 1 - y)]

        def theirs(k):
            return 2 * chips[k][0] + chips[k][1]

        def half_rows(half):
            return pl.ds(half * hr, hr)

        def ici(k, shard_index):
            return pltpu.make_async_remote_copy(
                src_ref=shard_ref.at[half_rows(c), :], dst_ref=wfull_ref.at[shard_index, half_rows(c), :],
                send_sem=send1.at[k], recv_sem=recv1.at[k], device_id=(*chips[k], c), device_id_type=MESH)

        def d2d(k, half):
            return pltpu.make_async_remote_copy(
                src_ref=wbuf.at[k % 2, half_rows(half), :], dst_ref=wfull_ref.at[theirs(k), half_rows(half), :],
                send_sem=send2.at[k], recv_sem=recv2.at[k], device_id=(x, y, 1 - c), device_id_type=MESH)

        def load(k, half, sem):
            return pltpu.make_async_copy(wfull_ref.at[theirs(k), half_rows(half), :],
                                         wbuf.at[k % 2, half_rows(half), :], lsem.at[sem])

        own = pltpu.make_async_copy(shard_ref, wfull_ref.at[mine], lsem.at[0])

        def put_proj(block):
            for grp in range(tn // D_ATT):
                proj_ref[grp] = block[:, grp * D_ATT:(grp + 1) * D_ATT].astype(BF16)

        @pl.when((j == 0) & (i == 0))
        def _():
            shard_ref[...] = shard32_ref[...].astype(BF16)
            own.start()
            ici(0, mine).start()
            ici(1, mine).start()

        for k in range(3):
            first = max(n - 3, 0) if k == 0 else min(n // 2, n - 1)

            @pl.when((j == k) & (i == first))
            def _(k=k):
                if k == 0:
                    ici(0, mine).wait_send()
                    ici(1, mine).wait_send()
                    ici(2, mine).start()
                if k == 2:
                    d2d(0, c).wait_send()
                ici(k, theirs(k)).wait_recv()
                load(k, c, 1).start()

            @pl.when((j == k) & (i == min(first + 1, n - 1)))
            def _(k=k):
                load(k, c, 1).wait()
                d2d(k, c).start()

            @pl.when((j == k) & (i == min(first + 2, n - 1)))
            def _(k=k):
                d2d(k, 1 - c).wait_recv()
                load(k, 1 - c, 2).start()

            @pl.when((j == k + 1) & (i == 0))
            def _(k=k):
                load(k, 1 - c, 2).wait()

        @pl.when(j == 0)
        def _():
            xv = x_ref[...]
            r = lax.rsqrt(jnp.mean(xv * xv, axis=-1, keepdims=True) + EPS)
            hv = ((xv * r) * g_ref[...]).astype(BF16)
            hbuf[i] = hv
            h_ref[...] = hv
            put_proj(_nn(hv, shard_ref[...]))

        for k in range(3):
            @pl.when(j == k + 1)
            def _(k=k):
                put_proj(_nn(hbuf[i], wbuf[k % 2]))

        @pl.when((j == 3) & (i == n - 1))
        def _():
            ici(2, mine).wait_send()
            d2d(1, c).wait_send()
            d2d(2, c).wait_send()
            own.wait()

    return pl.pallas_call(
        body, name="in_proj_gather",
        grid_spec=pltpu.PrefetchScalarGridSpec(
            num_scalar_prefetch=1, grid=(N_CHIPS, n),
            in_specs=[pl.BlockSpec((TM_BLK, d), lambda j, i, order: (jnp.where(j == 0, i, n - 1), 0)),
                      pl.BlockSpec((1, d), lambda j, i, order: (0, 0)), pl.BlockSpec(memory_space=pltpu.VMEM)],
            out_specs=[pl.BlockSpec((tn // D_ATT, TM_BLK, D_ATT), lambda j, i, order: (order[j], i, 0)),
                       pl.BlockSpec((TM_BLK, d), lambda j, i, order: (jnp.where(j == 0, i, n - 1), 0)), ANY],
            scratch_shapes=[pltpu.VMEM((d, tn), BF16), pltpu.VMEM((n, TM_BLK, d), BF16), pltpu.VMEM((2, d, tn), BF16),
                            pltpu.SemaphoreType.DMA((3,)), pltpu.SemaphoreType.DMA((3,)),
                            pltpu.SemaphoreType.DMA((3,)), pltpu.SemaphoreType.DMA((3,)), pltpu.SemaphoreType.DMA((3,))]),
        out_shape=[jax.ShapeDtypeStruct((N_CHIPS * tn // D_ATT, s, D_ATT), BF16), jax.ShapeDtypeStruct((s, d), BF16),
                   jax.ShapeDtypeStruct((N_CHIPS, d, tn), BF16)],
        compiler_params=_params("arbitrary", "arbitrary"),
    )(order, x, g, shard)


def _attn_fwd(diag, proj, shards, kinds, cw8):
    s = proj.shape[1]
    n = s // TQ
    nw = len(shards)
    scale = HEAD_DIM ** -0.5

    def body(*refs):
        diag_ref, q_ref, kp_ref, kc_ref, vp_ref, vc_ref = refs[:6]
        srcs, cw = refs[6:6 + nw], refs[6 + nw]
        o_ref, lse_ref = refs[7 + nw:9 + nw]
        dsts, cw_all = refs[9 + nw:9 + 2 * nw], refs[9 + 2 * nw]
        bias_scr = refs[10 + 2 * nw]
        casts = refs[11 + 2 * nw:11 + 3 * nw]
        start, forward, finish = _gather_plan(kinds, casts, dsts, cw, cw_all, *refs[11 + 3 * nw:])
        i = pl.program_id(0)

        @pl.when(i == 0)
        def _():
            for w in range(nw):
                casts[w][...] = srcs[w][...].astype(BF16)
            start()
            _build_bias(diag_ref, bias_scr)

        @pl.when(i == n // 2)
        def _():
            forward()

        @pl.when(i == n - 1)
        def _():
            finish()

        lane_hi = lax.broadcasted_iota(jnp.int32, (QB, LANES), 1) >= HEAD_DIM

        def block(b, first_tile):
            r0, n_prev = b * QB, TQ - b * QB
            n_cur = KW - n_prev
            pairs = range(HEADS // 2)
            lanes_of = [slice(LANES * p, LANES * (p + 1)) for p in pairs]
            scores = []
            for p in pairs:
                lanes = lanes_of[p]
                q2 = _stack_heads(q_ref[r0:r0 + QB, lanes] * scale, lane_hi)
                s_cur = _nt(q2, kc_ref[0:n_cur, lanes]) + bias_scr[p, :, n_prev:KW]
                if first_tile:
                    scores.append(s_cur)
                else:
                    scores.append(jnp.concatenate(
                        [_nt(q2, kp_ref[r0:TQ, lanes]) + bias_scr[p, :, 0:n_prev], s_cur], axis=1))
            probs = []
            for p in pairs:
                sc = scores[p]
                m = jnp.max(sc, axis=1, keepdims=True)
                pe = jnp.exp(sc - m)
                l = jnp.sum(pe, axis=1, keepdims=True)
                probs.append((pe.astype(BF16), l, m))
            for p in pairs:
                lanes = lanes_of[p]
                pb, l, m = probs[p]
                if first_tile:
                    o2 = _nn(pb, vc_ref[0:n_cur, lanes]) / l
                else:
                    o2 = (_nn(pb[:, 0:n_prev], vp_ref[r0:TQ, lanes]) + _nn(pb[:, n_prev:KW], vc_ref[0:n_cur, lanes])) / l
                lse2 = m + jnp.log(l)
                lse_ref[r0:r0 + QB, 2 * p:2 * p + 1] = lse2[0:QB, :]
                lse_ref[r0:r0 + QB, 2 * p + 1:2 * p + 2] = lse2[QB:2 * QB, :]
                o_ref[r0:r0 + QB, lanes] = jnp.where(lane_hi, o2[QB:2 * QB, :], o2[0:QB, :]).astype(BF16)

        @pl.when(i == 0)
        def _():
            for b in range(TQ // QB):
                block(b, True)

        @pl.when(i > 0)
        def _():
            for b in range(TQ // QB):
                block(b, False)

    blk = lambda grp, prev: pl.BlockSpec(
        (None, TQ, D_ATT), (lambda i: (grp, jnp.maximum(i - 1, 0), 0)) if prev else (lambda i: (grp, i, 0)))
    vmem = pl.BlockSpec(memory_space=pltpu.VMEM)
    return pl.pallas_call(
        body, name="attn_fwd", grid=(n,),
        in_specs=[pl.BlockSpec((HEADS, DIAG), lambda i: (0, 0)),
                  blk(0, False), blk(1, True), blk(1, False), blk(2, True), blk(2, False)] + [vmem] * (nw + 1),
        out_specs=[pl.BlockSpec((TQ, D_ATT), lambda i: (i, 0)), pl.BlockSpec((TQ, HEADS), lambda i: (i, 0))]
        + [ANY] * (nw + 1),
        out_shape=[jax.ShapeDtypeStruct((s, D_ATT), BF16), jax.ShapeDtypeStruct((s, HEADS), F32)]
        + _gather_out_shapes(shards, kinds, cw8),
        scratch_shapes=[pltpu.VMEM((HEADS // 2, 2 * QB, KW), F32)] + [pltpu.VMEM(a.shape, BF16) for a in shards]
        + _gather_sems(nw),
        compiler_params=_params("arbitrary"),
    )(diag, proj, proj, proj, proj, proj, *shards, cw8)


def _attn_bwd(diag, proj, d_att, att, lse, parts):
    s = proj.shape[1]
    n = s // TQ
    npart = len(parts)
    scale = HEAD_DIM ** -0.5
    rel_pad = 3 * LANES

    def body(*refs):
        diag_ref, q_ref, kp_ref, kc_ref, vp_ref, vc_ref, do_ref, o_ref, lse_ref = refs[:9]
        part_refs = refs[9:9 + npart]
        dqkv_ref, dbias_ref = refs[9 + npart:11 + npart]
        slot_refs = refs[11 + npart:11 + 2 * npart]
        bias_scr, dbias_acc, dk_acc, dv_acc, dq_scr = refs[11 + 2 * npart:16 + 2 * npart]
        start, finish = _scatter_plan(part_refs, slot_refs, *refs[16 + 2 * npart:])
        i = pl.program_id(0)
        cur, prv = i % 2, 1 - i % 2

        @pl.when(i == 0)
        def _():
            start()
            _build_bias(diag_ref, bias_scr)
            dbias_acc[...] = jnp.zeros_like(dbias_acc)
            dk_acc[...] = jnp.zeros_like(dk_acc)
            dv_acc[...] = jnp.zeros_like(dv_acc)

        @pl.when(i > 0)
        def _():
            dqkv_ref[:, 0:D_ATT] = dq_scr[...]
            dk_acc[cur] = jnp.zeros((D_ATT, TQ), F32)
            dv_acc[cur] = jnp.zeros((D_ATT, TQ), F32)

        lane_hi = lax.broadcasted_iota(jnp.int32, (QB, LANES), 1) >= HEAD_DIM
        col = lax.broadcasted_iota(jnp.int32, (2 * QB, KW), 1)

        def make_block(first_tile):
            def block(b):
                r0, n_prev = b * QB, TQ - b * QB
                n_cur = KW - n_prev
                for p in range(HEADS // 2):
                    lanes = slice(LANES * p, LANES * (p + 1))
                    q2 = _stack_heads(q_ref[r0:r0 + QB, lanes] * scale, lane_hi)
                    kw = jnp.concatenate([kp_ref[r0:TQ, lanes], kc_ref[0:n_cur, lanes]], axis=0)
                    vw = jnp.concatenate([vp_ref[r0:TQ, lanes], vc_ref[0:n_cur, lanes]], axis=0)
                    dop = do_ref[r0:r0 + QB, lanes]
                    do2 = _stack_heads(dop, lane_hi)
                    prod = dop.astype(F32) * o_ref[r0:r0 + QB, lanes].astype(F32)
                    delta2 = jnp.concatenate(
                        [jnp.sum(jnp.where(lane_hi, 0.0, prod), axis=1, keepdims=True),
                         jnp.sum(jnp.where(lane_hi, prod, 0.0), axis=1, keepdims=True)], axis=0)
                    lse2 = jnp.concatenate([lse_ref[r0:r0 + QB, 2 * p:2 * p + 1],
                                            lse_ref[r0:r0 + QB, 2 * p + 1:2 * p + 2]], axis=0)
                    sc = _nt(q2, kw) + bias_scr[p]
                    if first_tile:
                        sc = jnp.where(col >= TQ - r0, sc, NEG_BIG)
                    pr = jnp.exp(sc - lse2)
                    ds = pr * (_nt(do2, vw) - delta2)
                    dbias_acc[p] += ds
                    dsb = ds.astype(BF16)
                    dv_w = _tn(do2, pr.astype(BF16))
                    dk_w = _tn(q2, dsb)
                    dv_acc[prv, lanes, r0:TQ] += dv_w[:, 0:n_prev]
                    dv_acc[cur, lanes, 0:n_cur] += dv_w[:, n_prev:KW]
                    dk_acc[prv, lanes, r0:TQ] += dk_w[:, 0:n_prev]
                    dk_acc[cur, lanes, 0:n_cur] += dk_w[:, n_prev:KW]
                    dq2 = _nn(dsb, kw)
                    dq = jnp.where(lane_hi, dq2[QB:2 * QB, :], dq2[0:QB, :]) * scale
                    dq_scr[r0:r0 + QB, lanes] = dq.astype(BF16)
            return block

        @pl.when(i == 0)
        def _():
            for b in range(TQ // QB):
                make_block(True)(b)

        @pl.when((i > 0) & (i < n))
        def _():
            for b in range(TQ // QB):
                make_block(False)(b)

        @pl.when(i > 0)
        def _():
            dqkv_ref[:, D_ATT:2 * D_ATT] = dk_acc[prv].T.astype(BF16)
            dqkv_ref[:, 2 * D_ATT:3 * D_ATT] = dv_acc[prv].T.astype(BF16)

        @pl.when(i == n)
        def _():
            d_iota = lax.broadcasted_iota(jnp.int32, (DIAG, rel_pad), 0)
            n_iota = lax.broadcasted_iota(jnp.int32, (DIAG, rel_pad), 1)
            diff = jnp.where(d_iota < KW, d_iota, d_iota - DIAG)
            idx = jnp.clip(N_LEFT * CHUNK - diff, -MAX_REL, MAX_REL) + MAX_REL
            onehot = (idx == n_iota).astype(F32)
            rows = []
            for hd in range(HEADS):
                acc = dbias_acc[hd // 2, (hd % 2) * QB:(hd % 2 + 1) * QB, :]
                a = jnp.concatenate([acc, jnp.zeros((QB, DIAG - KW), F32)], axis=1)
                g8 = a[0:SUBLANES, :]
                for blk in range(1, QB // SUBLANES):
                    g8 = g8 + pltpu.roll(a[blk * SUBLANES:(blk + 1) * SUBLANES, :], DIAG - blk * SUBLANES, 1)
                g1 = g8[0:1, :]
                for r in range(1, SUBLANES):
                    g1 = g1 + pltpu.roll(g8[r:r + 1, :], DIAG - r, 1)
                rows.append(g1)
            g = jnp.concatenate(rows, axis=0)
            dbias_ref[...] = jnp.dot(g, onehot, preferred_element_type=F32, precision=lax.Precision.HIGHEST)
            finish()

    last = n - 1
    cur = lambda grp: pl.BlockSpec((None, TQ, D_ATT), lambda i: (grp, jnp.minimum(i, last), 0))
    prev = lambda grp: pl.BlockSpec((None, TQ, D_ATT), lambda i: (grp, jnp.maximum(jnp.minimum(i, last) - 1, 0), 0))
    tile = pl.BlockSpec((TQ, D_ATT), lambda i: (jnp.minimum(i, last), 0))
    return pl.pallas_call(
        body, name="attn_bwd", grid=(n + 1,),
        in_specs=[pl.BlockSpec((HEADS, DIAG), lambda i: (0, 0)),
                  cur(0), prev(1), cur(1), prev(2), cur(2), tile, tile,
                  pl.BlockSpec((TQ, HEADS), lambda i: (jnp.minimum(i, last), 0))] + [ANY] * npart,
        out_specs=[pl.BlockSpec((TQ, 3 * D_ATT), lambda i: (jnp.maximum(i - 1, 0), 0)),
                   pl.BlockSpec((HEADS, rel_pad), lambda i: (0, 0))] + [ANY] * npart,
        out_shape=[jax.ShapeDtypeStruct((s, 3 * D_ATT), BF16), jax.ShapeDtypeStruct((HEADS, rel_pad), F32)]
        + _scatter_out_shapes(parts),
        scratch_shapes=[pltpu.VMEM((HEADS // 2, 2 * QB, KW), F32), pltpu.VMEM((HEADS // 2, 2 * QB, KW), F32),
                        pltpu.VMEM((2, D_ATT, TQ), F32), pltpu.VMEM((2, D_ATT, TQ), F32),
                        pltpu.VMEM((TQ, D_ATT), BF16)] + _scatter_sems(npart),
        compiler_params=_params("arbitrary"),
    )(diag, proj, proj, proj, proj, proj, d_att, att, lse, *parts)


def _shift_down(a, k, halo):
    rolled = pltpu.roll(a, k, 0)
    row = lax.broadcasted_iota(jnp.int32, halo.shape, 0)
    first = jnp.where(row < k, pltpu.roll(halo, k, 0), rolled[0:SUBLANES, :])
    return jnp.concatenate([first, rolled[SUBLANES:, :]], axis=0)


def _shift_up(a, k, nxt):
    tm = a.shape[0]
    rolled = pltpu.roll(a, tm - k, 0)
    row = lax.broadcasted_iota(jnp.int32, nxt.shape, 0)
    last = jnp.where(row >= SUBLANES - k, pltpu.roll(nxt, SUBLANES - k, 0), rolled[tm - SUBLANES:, :])
    return jnp.concatenate([rolled[:tm - SUBLANES, :], last], axis=0)


def _sigmoid(v):
    return 0.5 * jnp.tanh(0.5 * v) + 0.5


def _mixer_mid(att, proj, x, tgt, w_att, w_conv, w_out, w_att_t, w_conv_t, w_out_t, conv_w8, conv_b, fin_g):
    s, d = x.shape
    dc = D_ATT
    n = s // TM_MID
    tm = TM_MID
    n_shards = 4

    def body(att_ref, za_ref, gb_ref, gc_ref, u_ref, zc_ref, hgc_ref, hu_ref, gatt_ref, gconv_ref, x_ref, t_ref,
             watt_ref, wconv_ref, wout_ref, watt_t_ref, wconv_t_ref, wout_t_ref, cw_ref, cb_ref, fg_ref,
             dpb_ref, do_ref, dx2_ref, gatt_o, gconv_o, gout_o, loss_o, gfn_o, gcb_o, gcw_o,
             acc_att, acc_conv, acc_out, carry):
        i = pl.program_id(0)
        tile = n - 1 - i

        @pl.when(i == 0)
        def _():
            acc_att[...] = jnp.zeros_like(acc_att)
            acc_conv[...] = jnp.zeros_like(acc_conv)
            acc_out[...] = jnp.zeros_like(acc_out)
            carry[...] = jnp.zeros_like(carry)
            loss_o[...] = jnp.zeros_like(loss_o)
            gfn_o[...] = jnp.zeros_like(gfn_o)
            gcb_o[...] = jnp.zeros_like(gcb_o)
            gcw_o[...] = jnp.zeros_like(gcw_o)

        halves = [slice(hh * (tm // 2), (hh + 1) * (tm // 2)) for hh in range(2)]
        both = lambda fn: [fn(rows) for rows in halves]
        f32 = lambda ref, rows: ref[rows, :].astype(F32)

        gc = gc_ref[...].astype(F32)
        u = u_ref[...].astype(F32)
        cu = gc * u
        halo = jnp.where(tile > 0, hgc_ref[...].astype(F32) * hu_ref[...].astype(F32), 0.0)
        cu1 = _shift_down(cu, 1, halo)
        cu2 = _shift_down(cu, 2, halo)
        w0, w1, w2 = cw_ref[0:1, :], cw_ref[1:2, :], cw_ref[2:3, :]
        fg = fg_ref[...]

        def stage_a(rows):
            att_v, za, zc, gb = f32(att_ref, rows), f32(za_ref, rows), f32(zc_ref, rows), f32(gb_ref, rows)
            sa = _sigmoid(za)
            silu_a = za * sa
            vconv = w0 * cu2[rows, :] + w1 * cu1[rows, :] + w2 * cu[rows, :] + cb_ref[...]
            sc = _sigmoid(zc)
            silu_c = zc * sc
            return dict(att_v=att_v, za=za, zc=zc, gb=gb, sa=sa, silu_a=silu_a, vconv=vconv, sc=sc, silu_c=silu_c,
                        a_b=(att_v * silu_a).astype(BF16), c_b=(gb * vconv * silu_c).astype(BF16))

        st = both(stage_a)
        for t in st:
            t["y_att"] = _nn(t["a_b"], watt_ref[...])
            t["y_conv"] = _nn(t["c_b"], wconv_ref[...])
        for t, rows in zip(st, halves):
            gpair = lambda ref: jnp.concatenate([ref[0, rows, :], ref[1, rows, :]], axis=1).astype(F32)
            t["ga"] = _sigmoid(gpair(gatt_ref))
            t["gv"] = _sigmoid(gpair(gconv_ref))
            t["m_b"] = (t["ga"] * t["y_att"] + t["gv"] * t["y_conv"]).astype(BF16)
        for t in st:
            t["mo"] = _nn(t["m_b"], wout_ref[...])
        for t, rows in zip(st, halves):
            x2 = x_ref[rows, :] + t["mo"]
            r2 = lax.rsqrt(jnp.mean(x2 * x2, axis=-1, keepdims=True) + EPS)
            x2n = x2 * r2
            err = x2n * fg - t_ref[rows, :]
            loss_o[...] += jnp.sum(err * err, axis=0, keepdims=True) * (0.5 / d)
            dy = err * (1.0 / d)
            gfn_o[...] += jnp.sum(dy * x2n, axis=0, keepdims=True)
            dyn = dy * fg
            dx2 = r2 * (dyn - x2n * jnp.mean(dyn * x2n, axis=-1, keepdims=True))
            dx2_ref[rows, :] = dx2
            t["dx2_b"] = dx2.astype(BF16)
        for t in st:
            t["dm"] = _nn(t["dx2_b"], wout_t_ref[...])
        whole = lambda key: jnp.concatenate([st[0][key], st[1][key]], axis=0)
        acc_out[...] += _tn(whole("m_b"), whole("dx2_b"))
        for t, rows in zip(st, halves):
            dy_att = t["dm"] * t["ga"]
            dy_conv = t["dm"] * t["gv"]
            dpb_ref[rows, 5 * dc:5 * dc + d] = (dy_att * t["y_att"] * (1.0 - t["ga"])).astype(BF16)
            dpb_ref[rows, 5 * dc + d:5 * dc + 2 * d] = (dy_conv * t["y_conv"] * (1.0 - t["gv"])).astype(BF16)
            t["dya_b"] = dy_att.astype(BF16)
            t["dyc_b"] = dy_conv.astype(BF16)
        for t in st:
            t["da_in"] = _nn(t["dya_b"], watt_t_ref[...])
            t["dc_in"] = _nn(t["dyc_b"], wconv_t_ref[...])
        acc_att[...] += _tn(whole("a_b"), whole("dya_b"))
        acc_conv[...] += _tn(whole("c_b"), whole("dyc_b"))
        for t, rows in zip(st, halves):
            sa, za, sc, zc = t["sa"], t["za"], t["sc"], t["zc"]
            do_ref[rows, :] = (t["da_in"] * t["silu_a"]).astype(BF16)
            dpb_ref[rows, 0:dc] = (t["da_in"] * t["att_v"] * (sa * (1.0 + za * (1.0 - sa)))).astype(BF16)
            dpb_ref[rows, dc:2 * dc] = (t["dc_in"] * t["vconv"] * t["silu_c"]).astype(BF16)
            dgs = t["dc_in"] * t["gb"]
            t["dvc"] = dgs * t["silu_c"]
            dpb_ref[rows, 4 * dc:5 * dc] = (dgs * t["vconv"] * (sc * (1.0 + zc * (1.0 - sc)))).astype(BF16)
        dvc = whole("dvc")
        gcb_o[...] += jnp.sum(dvc, axis=0, keepdims=True)
        gcw_o[0:1, :] += jnp.sum(dvc * cu2, axis=0, keepdims=True)
        gcw_o[1:2, :] += jnp.sum(dvc * cu1, axis=0, keepdims=True)
        gcw_o[2:3, :] += jnp.sum(dvc * cu, axis=0, keepdims=True)
        nxt = carry[...]
        dcu = w2 * dvc + w1 * _shift_up(dvc, 1, nxt) + w0 * _shift_up(dvc, 2, nxt)
        carry[...] = dvc[0:SUBLANES, :]
        dpb_ref[:, 2 * dc:3 * dc] = (dcu * u).astype(BF16)
        dpb_ref[:, 3 * dc:4 * dc] = (dcu * gc).astype(BF16)

        @pl.when(i == n - 1)
        def _():
            for j in range(n_shards):
                gatt_o[j] = acc_att[:, j * (d // n_shards):(j + 1) * (d // n_shards)].astype(BF16)
                gconv_o[j] = acc_conv[:, j * (d // n_shards):(j + 1) * (d // n_shards)].astype(BF16)
                gout_o[j] = acc_out[j * (d // n_shards):(j + 1) * (d // n_shards), :].astype(BF16)

    rev = lambda width, col_blk: pl.BlockSpec((tm, width), lambda i: (n - 1 - i, col_blk))
    grp = lambda g: pl.BlockSpec((None, tm, dc), lambda i: (g, n - 1 - i, 0))
    grp2 = lambda g2: pl.BlockSpec((2, tm, dc), lambda i: (g2, n - 1 - i, 0))
    halo_spec = lambda g: pl.BlockSpec(
        (None, SUBLANES, dc), lambda i: (g, jnp.maximum((n - 1 - i) * (tm // SUBLANES) - 1, 0), 0))
    const = lambda shape: pl.BlockSpec(shape, lambda i: tuple(0 for _ in shape), pipeline_mode=pl.Buffered(1))
    q4 = d // n_shards
    return pl.pallas_call(
        body, name="mixer_mid", grid=(n,),
        in_specs=[rev(dc, 0), grp(3), grp(4), grp(5), grp(6), grp(7),
                  halo_spec(5), halo_spec(6), grp2(4), grp2(5), rev(d, 0), rev(d, 0),
                  const((dc, d)), const((dc, d)), const((d, d)), const((d, dc)), const((d, dc)), const((d, d)),
                  const((SUBLANES, dc)), const((1, dc)), const((1, d))],
        out_specs=[rev(5 * dc + 2 * d, 0), rev(dc, 0), rev(d, 0),
                   const((n_shards, dc, q4)), const((n_shards, dc, q4)), const((n_shards, q4, d)),
                   const((1, d)), const((1, d)), const((1, dc)), const((SUBLANES, dc))],
        out_shape=[jax.ShapeDtypeStruct((s, 5 * dc + 2 * d), BF16), jax.ShapeDtypeStruct((s, dc), BF16),
                   jax.ShapeDtypeStruct((s, d), F32),
                   jax.ShapeDtypeStruct((n_shards, dc, q4), BF16), jax.ShapeDtypeStruct((n_shards, dc, q4), BF16),
                   jax.ShapeDtypeStruct((n_shards, q4, d), BF16),
                   jax.ShapeDtypeStruct((1, d), F32), jax.ShapeDtypeStruct((1, d), F32),
                   jax.ShapeDtypeStruct((1, dc), F32), jax.ShapeDtypeStruct((SUBLANES, dc), F32)],
        scratch_shapes=[pltpu.VMEM((dc, d), F32), pltpu.VMEM((dc, d), F32), pltpu.VMEM((d, d), F32),
                        pltpu.VMEM((SUBLANES, dc), F32)],
        compiler_params=_params("arbitrary"),
    )(att, proj, proj, proj, proj, proj, proj, proj, proj, proj, x, tgt,
      w_att, w_conv, w_out, w_att_t, w_conv_t, w_out_t, conv_w8, conv_b, fin_g)


def _in_proj_bwd_x(dqkv, dpb, w_in, x, dx2, g):
    s, d = x.shape
    tn = dqkv.shape[1]
    nb = dpb.shape[1] // tn
    n = s // TM_MM

    def body(*refs):
        dps, ws = refs[:nb + 1], refs[nb + 1:2 * nb + 2]
        x_ref, dx2_ref, g_ref, gx_ref, gng_ref = refs[2 * nb + 2:]
        i = pl.program_id(0)

        @pl.when(i == 0)
        def _():
            gng_ref[...] = jnp.zeros_like(gng_ref)

        dh = _nt(dps[0][...], ws[0][0])
        for j in range(1, nb + 1):
            dh = dh + _nt(dps[j][...], ws[j][0])
        xv = x_ref[...]
        r = lax.rsqrt(jnp.mean(xv * xv, axis=-1, keepdims=True) + EPS)
        xn = xv * r
        gng_ref[...] += jnp.sum(dh * xn, axis=0, keepdims=True)
        dhn = dh * g_ref[...]
        gx_ref[...] = dx2_ref[...] + r * (dhn - xn * jnp.mean(dhn * xn, axis=-1, keepdims=True))

    tile = lambda width, col_blk: pl.BlockSpec((TM_MM, width), lambda i: (i, col_blk))
    wspec = lambda blk: pl.BlockSpec((1, d, tn), lambda i: (blk, 0, 0), pipeline_mode=pl.Buffered(1))
    return pl.pallas_call(
        body, name="in_proj_bwd_x", grid=(n,),
        in_specs=[tile(tn, 0)] + [tile(tn, j) for j in range(nb)] + [wspec(j) for j in range(nb + 1)]
        + [tile(d, 0), tile(d, 0), pl.BlockSpec((1, d), lambda i: (0, 0))],
        out_specs=[tile(d, 0), pl.BlockSpec((1, d), lambda i: (0, 0))],
        out_shape=[jax.ShapeDtypeStruct((s, d), F32), jax.ShapeDtypeStruct((1, d), F32)],
        compiler_params=_params("arbitrary"),
    )(dqkv, *([dpb] * nb), *([w_in] * (nb + 1)), x, dx2, g)


def _in_proj_bwd_w(h, dqkv, dpb, order):
    s, d = h.shape
    tn = dqkv.shape[1]
    n = s // TM_BLK
    hr = d // 2
    settle = min(2, n - 1)

    def body(order_ref, h_ref, da_ref, db_ref, slots_ref, acc, sendbuf, pairbuf, chipbuf, psend, precv, send, recv, lsem):
        j, i = pl.program_id(0), pl.program_id(1)
        blk = order_ref[j]
        pos = _position()
        x, y, c = pos

        @pl.when(i == 0)
        def _():
            acc[...] = jnp.zeros_like(acc)

        @pl.when(blk == 0)
        def _():
            acc[...] += _tn(h_ref[...], da_ref[...])

        @pl.when(blk > 0)
        def _():
            acc[...] += _tn(h_ref[...], db_ref[...])

        def pair(step, half):
            return pltpu.make_async_remote_copy(
                src_ref=sendbuf.at[step, pl.ds(half * hr, hr), :], dst_ref=pairbuf.at[step],
                send_sem=psend.at[step], recv_sem=precv.at[step], device_id=(x, y, 1 - c), device_id_type=MESH)

        def ici(step):
            flip = OWNER_FLIPS[step]
            return pltpu.make_async_remote_copy(
                src_ref=chipbuf.at[step], dst_ref=slots_ref.at[flip], send_sem=send.at[step], recv_sem=recv.at[step],
                device_id=_peer(pos, 4 * (flip >> 1) + 2 * (flip & 1)), device_id_type=MESH)

        local = pltpu.make_async_copy(chipbuf.at[N_CHIPS - 1], slots_ref.at[0], lsem.at[0])

        def combine(step):
            pair(step, c).wait_recv()
            mine = sendbuf[step, pl.ds(c * hr, hr), :].astype(F32)
            chipbuf[step] = (mine + pairbuf[step].astype(F32)).astype(BF16)

        for step in range(N_CHIPS):
            @pl.when((j == step) & (i == n - 1))
            def _(step=step):
                sendbuf[step] = acc[...].astype(BF16)
                pair(step, 1 - c).start()

        for step in range(N_CHIPS - 1):
            @pl.when((j == step + 1) & (i == settle))
            def _(step=step):
                combine(step)
                ici(step).start()

        @pl.when((j == N_CHIPS - 1) & (i == n - 1))
        def _():
            combine(N_CHIPS - 1)
            local.start()
            for step in range(N_CHIPS - 1):
                ici(step).wait_recv()
            for step in range(N_CHIPS - 1):
                ici(step).wait_send()
            for step in range(N_CHIPS):
                pair(step, 1 - c).wait_send()
            local.wait()

    return pl.pallas_call(
        body, name="in_proj_bwd_w",
        grid_spec=pltpu.PrefetchScalarGridSpec(
            num_scalar_prefetch=1, grid=(N_CHIPS, n),
            in_specs=[pl.BlockSpec((TM_BLK, d), lambda j, i, order: (i, 0)),
                      pl.BlockSpec((TM_BLK, tn), lambda j, i, order: (jnp.where(order[j] == 0, i, 0), 0)),
                      pl.BlockSpec((TM_BLK, tn), lambda j, i, order: (jnp.where(order[j] == 0, 0, i),
                                                                     jnp.maximum(order[j] - 1, 0)))],
            out_specs=[ANY],
            scratch_shapes=[pltpu.VMEM((d, tn), F32), pltpu.VMEM((N_CHIPS, d, tn), BF16),
                            pltpu.VMEM((N_CHIPS, hr, tn), BF16), pltpu.VMEM((N_CHIPS, hr, tn), BF16),
                            pltpu.SemaphoreType.DMA((N_CHIPS,)), pltpu.SemaphoreType.DMA((N_CHIPS,)),
                            pltpu.SemaphoreType.DMA((N_CHIPS - 1,)), pltpu.SemaphoreType.DMA((N_CHIPS - 1,)),
                            pltpu.SemaphoreType.DMA((1,))]),
        out_shape=[jax.ShapeDtypeStruct((N_CHIPS, hr, tn), BF16)],
        compiler_params=_params("arbitrary", "arbitrary"),
    )(order, h, dqkv, dpb)[0]


LOSS_ROW = 6


def _adam_update(w, g, m, v):
    c1 = 1.0 / (1.0 - ADAM_B1 ** ADAM_STEP)
    c2 = 1.0 / (1.0 - ADAM_B2 ** ADAM_STEP)
    m2 = ADAM_B1 * m + (1.0 - ADAM_B1) * g
    v2 = ADAM_B2 * v + (1.0 - ADAM_B2) * (g * g)
    return -ADAM_LR * ((m2 * c1) / (jnp.sqrt(v2 * c2) + ADAM_EPS) + ADAM_WD * w), m2, v2


def _adamw_small(recv, params, moments_m, moments_v):
    k = recv.shape[0]
    n_par = len(params)
    cshard = params[3].shape[1]

    def body(*refs):
        r_ref = refs[0]
        ws, ms, vs = refs[1:1 + n_par], refs[1 + n_par:1 + 2 * n_par], refs[1 + 2 * n_par:1 + 3 * n_par]
        loss_ref = refs[1 + 3 * n_par]
        outs = refs[2 + 3 * n_par:]
        total = r_ref[0]
        for slot in range(1, k):
            total = total + r_ref[slot]
        loss_ref[...] = jnp.sum(total[LOSS_ROW:LOSS_ROW + 1, :], axis=1, keepdims=True)
        chip = 2 * lax.axis_index("x") + lax.axis_index("y")
        g_cw = jnp.zeros((3, cshard), F32)
        for sh in range(N_CHIPS):
            g_cw = g_cw + jnp.where(chip == sh, total[3:6, sh * cshard:(sh + 1) * cshard], 0.0)
        grads = [total[0:1, :], total[1:2, :], total[2:3, :ws[2].shape[1]], g_cw, total[8:16, :ws[4].shape[1]]]
        for p in range(n_par):
            delta, m2, v2 = _adam_update(ws[p][...], grads[p], ms[p][...], vs[p][...])
            for q, val in enumerate((grads[p], delta, m2, v2)):
                outs[4 * p + q][...] = val

    shapes = [jax.ShapeDtypeStruct((1, 1), F32)]
    for p in params:
        shapes += [jax.ShapeDtypeStruct(p.shape, F32)] * 4
    return pl.pallas_call(body, name="adamw_small", out_shape=shapes)(recv, *params, *moments_m, *moments_v)


def _adamw(w, g, m, v, name, rows_per_step):
    r, c = w.shape

    def body(w_ref, g_ref, m_ref, v_ref, d_ref, mo_ref, vo_ref):
        d_ref[...], mo_ref[...], vo_ref[...] = _adam_update(w_ref[...], g_ref[...], m_ref[...], v_ref[...])

    spec = pl.BlockSpec((rows_per_step, c), lambda i: (i, 0))
    shape = jax.ShapeDtypeStruct((r, c), F32)
    return pl.pallas_call(
        body, name=name, grid=(r // rows_per_step,),
        in_specs=[spec] * 4, out_specs=[spec] * 3, out_shape=[shape] * 3,
        compiler_params=_params("parallel"),
    )(w, g, m, v)


ANY = pl.BlockSpec(memory_space=pl.ANY)
N_CHIPS = 4
N_DEV = 8
OWNER_FLIPS = (3, 1, 2, 0)


def _position():
    return lax.axis_index("x"), lax.axis_index("y"), lax.axis_index("c")


def _gather_out_shapes(shards, kinds, cw8):
    full = [(a.shape[0], a.shape[1] * N_CHIPS) if k == "cols" else (a.shape[0] * N_CHIPS, a.shape[1])
            for a, k in zip(shards, kinds)]
    return [jax.ShapeDtypeStruct(f, BF16) for f in full] + [
        jax.ShapeDtypeStruct((N_CHIPS,) + cw8.shape, cw8.dtype)]


def _gather_sems(nw):
    return [pltpu.SemaphoreType.DMA((3, nw)), pltpu.SemaphoreType.DMA((3, nw)),
            pltpu.SemaphoreType.DMA((3, nw)), pltpu.SemaphoreType.DMA((3, nw)),
            pltpu.SemaphoreType.DMA((3,)), pltpu.SemaphoreType.DMA((3,)), pltpu.SemaphoreType.DMA((nw + 1,))]


def _gather_plan(kinds, srcs, dsts, cw, cw_all, send1, recv1, send2, recv2, ssend, srecv, lsem):
    nw = len(srcs)
    x, y, c = _position()
    mine = 2 * x + y
    chips = [(x, 1 - y), (1 - x, y), (1 - x, 1 - y)]

    def window(w, shard, half):
        r, cc = srcs[w].shape
        hr = r // 2
        if kinds[w] == "cols":
            rows = pl.ds(0, r) if half is None else pl.ds(half * hr, hr)
            return dsts[w].at[rows, pl.ds(shard * cc, cc)]
        rows = pl.ds(shard * r, r) if half is None else pl.ds(shard * r + half * hr, hr)
        return dsts[w].at[rows, :]

    def my_half(w):
        hr = srcs[w].shape[0] // 2
        return srcs[w].at[pl.ds(c * hr, hr), :]

    def local():
        return [pltpu.make_async_copy(srcs[w], window(w, mine, None), lsem.at[w]) for w in range(nw)] + [
            pltpu.make_async_copy(cw, cw_all.at[mine], lsem.at[nw])]

    def ici(k, w, shard):
        kx, ky = chips[k]
        return pltpu.make_async_remote_copy(
            src_ref=my_half(w), dst_ref=window(w, shard, c), send_sem=send1.at[k, w], recv_sem=recv1.at[k, w],
            device_id=(kx, ky, c), device_id_type=MESH)

    def d2d(k, w, shard, half):
        return pltpu.make_async_remote_copy(
            src_ref=window(w, shard, half), dst_ref=window(w, shard, half),
            send_sem=send2.at[k, w], recv_sem=recv2.at[k, w], device_id=(x, y, 1 - c), device_id_type=MESH)

    def small(k, shard):
        kx, ky = chips[k]
        return pltpu.make_async_remote_copy(
            src_ref=cw, dst_ref=cw_all.at[shard], send_sem=ssend.at[k], recv_sem=srecv.at[k],
            device_id=(kx, ky, c), device_id_type=MESH)

    def theirs(k):
        kx, ky = chips[k]
        return 2 * kx + ky

    def start():
        for cp in local():
            cp.start()
        for k in range(3):
            for w in range(nw):
                ici(k, w, mine).start()
            small(k, mine).start()

    def forward():
        for k in range(3):
            for w in range(nw):
                ici(k, w, theirs(k)).wait_recv()
                d2d(k, w, theirs(k), c).start()

    def finish():
        for k in range(3):
            for w in range(nw):
                d2d(k, w, theirs(k), 1 - c).wait_recv()
            small(k, theirs(k)).wait_recv()
        for k in range(3):
            for w in range(nw):
                ici(k, w, mine).wait_send()
                d2d(k, w, theirs(k), c).wait_send()
            small(k, mine).wait_send()
        for cp in local():
            cp.wait()

    return start, forward, finish


def _scatter_out_shapes(parts):
    return [jax.ShapeDtypeStruct((N_DEV, p.shape[1] // 2, p.shape[2]), p.dtype) for p in parts]


def _scatter_sems(nw):
    return [pltpu.SemaphoreType.DMA((N_DEV, nw)), pltpu.SemaphoreType.DMA((N_DEV, nw)), pltpu.SemaphoreType.DMA((nw,))]


def _peer(pos, k):
    x, y, c = pos
    return ((1 - x) if k & 4 else x, (1 - y) if k & 2 else y, (1 - c) if k & 1 else c)


def _scatter_plan(srcs, dsts, send, recv, lsem):
    nw = len(srcs)
    pos = _position()

    def piece(w, k):
        px, py, pc = _peer(pos, k)
        hr = srcs[w].shape[1] // 2
        return srcs[w].at[2 * px + py, pl.ds(pc * hr, hr), :]

    def remote(w, k):
        return pltpu.make_async_remote_copy(
            src_ref=piece(w, k), dst_ref=dsts[w].at[k], send_sem=send.at[k, w], recv_sem=recv.at[k, w],
            device_id=_peer(pos, k), device_id_type=MESH)

    def local(w):
        return pltpu.make_async_copy(piece(w, 0), dsts[w].at[0], lsem.at[w])

    def start():
        for w in range(nw):
            local(w).start()
        for k in range(1, N_DEV):
            for w in range(nw):
                remote(w, k).start()

    def finish():
        for k in range(1, N_DEV):
            for w in range(nw):
                remote(w, k).wait_recv()
        for k in range(1, N_DEV):
            for w in range(nw):
                remote(w, k).wait_send()
        for w in range(nw):
            local(w).wait()

    return start, finish


def _reduce_pair(slots, small):
    nw = len(slots)

    def body(*refs):
        srcs, sm = refs[:nw], refs[nw]
        dsts, sm_all = refs[nw + 1:2 * nw + 1], refs[2 * nw + 1]
        halves = refs[2 * nw + 2:3 * nw + 2]
        send, recv, ssend, srecv, lsem = refs[3 * nw + 2:]
        pos = _position()
        x, y, c = pos
        me = 4 * x + 2 * y + c

        def rows(w, half):
            hr = halves[w].shape[0]
            return dsts[w].at[pl.ds(half * hr, hr), :]

        def remote(w, half):
            return pltpu.make_async_remote_copy(
                src_ref=halves[w], dst_ref=rows(w, half), send_sem=send.at[w], recv_sem=recv.at[w],
                device_id=(x, y, 1 - c), device_id_type=MESH)

        def bcast(k, slot):
            return pltpu.make_async_remote_copy(
                src_ref=sm, dst_ref=sm_all.at[slot], send_sem=ssend.at[k], recv_sem=srecv.at[k],
                device_id=_peer(pos, k), device_id_type=MESH)

        small_copies = [bcast(k, me) for k in range(1, N_DEV)]
        own_small = pltpu.make_async_copy(sm, sm_all.at[me], lsem.at[nw])
        for cp in small_copies + [own_small]:
            cp.start()
        big = []
        for w in range(nw):
            total = srcs[w][0].astype(F32)
            for k in range(1, srcs[w].shape[0]):
                total = total + srcs[w][k].astype(F32)
            halves[w][...] = total
            big += [remote(w, c), pltpu.make_async_copy(halves[w], rows(w, c), lsem.at[w])]
            big[-2].start()
            big[-1].start()
        for w in range(nw):
            remote(w, 1 - c).wait_recv()
        for k in range(1, N_DEV):
            px, py, pc = _peer(pos, k)
            bcast(k, 4 * px + 2 * py + pc).wait_recv()
        for w in range(nw):
            big[2 * w].wait_send()
            big[2 * w + 1].wait()
        for cp in small_copies:
            cp.wait_send()
        own_small.wait()

    vmem = pl.BlockSpec(memory_space=pltpu.VMEM)
    half_shapes = [(sl.shape[1], sl.shape[2]) for sl in slots]
    return pl.pallas_call(
        body, name="reduce_pair",
        in_specs=[vmem] * (nw + 1), out_specs=[ANY] * (nw + 1),
        out_shape=[jax.ShapeDtypeStruct((2 * r, cc), F32) for r, cc in half_shapes]
        + [jax.ShapeDtypeStruct((N_DEV,) + small.shape, small.dtype)],
        scratch_shapes=[pltpu.VMEM(hs, F32) for hs in half_shapes]
        + [pltpu.SemaphoreType.DMA((nw,)), pltpu.SemaphoreType.DMA((nw,)),
           pltpu.SemaphoreType.DMA((N_DEV,)), pltpu.SemaphoreType.DMA((N_DEV,)),
           pltpu.SemaphoreType.DMA((nw + 1,))],
        compiler_params=pltpu.CompilerParams(vmem_limit_bytes=VMEM_LIMIT),
    )(*slots, small)


def _pad_to(a, rows, cols):
    return jnp.pad(a, ((0, rows - a.shape[0]), (0, cols - a.shape[1])))


def _pack_small(norm_g, fin_g, conv_b, conv_w, loss_vec, rel):
    rows = [_pad_to(norm_g, 1, SMALL_COLS), _pad_to(fin_g, 1, SMALL_COLS), _pad_to(conv_b, 1, SMALL_COLS),
            _pad_to(conv_w, 3, SMALL_COLS), _pad_to(loss_vec, 2, SMALL_COLS), _pad_to(rel, HEADS, SMALL_COLS)]
    return jnp.concatenate(rows, axis=0)


def kernel(x, norm_g, w_in, rel_bias, w_att_out, conv_w, conv_b, w_conv_out, w_out, final_norm_g, loss_target, m_norm_g, m_w_in, m_rel_bias, m_w_att_out, m_conv_w, m_conv_b, m_w_conv_out, m_w_out, m_final_norm_g, v_norm_g, v_w_in, v_rel_bias, v_w_att_out, v_conv_w, v_conv_b, v_w_conv_out, v_w_out, v_final_norm_g):
    xs, tgt = x[0], loss_target[0]
    cshard = conv_w.shape[2]
    chip = 2 * lax.axis_index("x") + lax.axis_index("y")

    shards = [w_in[0], w_att_out[0], w_conv_out[0], w_out[0]]
    cw8 = _pad_to(conv_w[0], SUBLANES, cshard)
    flips = jnp.arange(N_CHIPS, dtype=jnp.int32)
    own_first = jnp.bitwise_xor(chip, flips)
    own_last = jnp.bitwise_xor(chip, jnp.asarray(OWNER_FLIPS, jnp.int32))

    proj, h, wb_in = _in_proj_gather(xs, norm_g, shards[0], own_first)
    diag = jnp.take(rel_bias[0], _diag_rel_index(), axis=1)
    att, lse, wb_att, wb_conv, wb_out, cw_all = _attn_fwd(diag, proj, shards[1:], ["cols", "cols", "rows"], cw8)
    conv_w_full = jnp.transpose(cw_all, (1, 0, 2)).reshape(SUBLANES, N_CHIPS * cshard)
    (dpb, d_att, dx2, g_att_p, g_conv_p, g_out_p, loss_vec, g_fin, g_cb, g_cw) = _mixer_mid(
        att, proj, xs, tgt, wb_att, wb_conv, wb_out, wb_att.T, wb_conv.T, wb_out.T, conv_w_full, conv_b,
        final_norm_g[None, :])
    dqkv, g_rel, r_att, r_conv, r_out = _attn_bwd(diag, proj, d_att, att, lse, [g_att_p, g_conv_p, g_out_p])
    grad_x, g_norm = _in_proj_bwd_x(dqkv, dpb, wb_in, xs, dx2, norm_g)
    r_in = _in_proj_bwd_w(h, dqkv, dpb, own_last)

    small = _pack_small(g_norm, g_fin, g_cb, g_cw[0:3], loss_vec, g_rel)
    gw_in, gw_att, gw_conv, gw_out, r_small = _reduce_pair([r_in, r_att, r_conv, r_out], small)
    small_out = _adamw_small(
        r_small,
        [norm_g, final_norm_g[None, :], conv_b, conv_w[0], rel_bias[0]],
        [m_norm_g, m_final_norm_g[None, :], m_conv_b, m_conv_w[0], m_rel_bias[0]],
        [v_norm_g, v_final_norm_g[None, :], v_conv_b, v_conv_w[0], v_rel_bias[0]])
    loss = small_out[0][0, 0]
    small_names = ["norm_g", "final_norm_g", "conv_b", "conv_w", "rel_bias"]
    fix = {"norm_g": lambda a: a, "final_norm_g": lambda a: a[0], "conv_b": lambda a: a,
           "conv_w": lambda a: a[None], "rel_bias": lambda a: a[None]}
    small_res = {name: [fix[name](small_out[1 + 4 * p + q]) for q in range(4)] for p, name in enumerate(small_names)}

    big = {}
    for name, w, g, m, v, rows in (("w_in", w_in, gw_in, m_w_in, v_w_in, 128),
                                   ("w_att_out", w_att_out, gw_att, m_w_att_out, v_w_att_out, 256),
                                   ("w_conv_out", w_conv_out, gw_conv, m_w_conv_out, v_w_conv_out, 256),
                                   ("w_out", w_out, gw_out, m_w_out, v_w_out, 128)):
        dw, mw, vw = _adamw(w[0], g, m[0], v[0], "adamw_" + name, rows)
        big[name] = (g[None], dw[None], mw[None], vw[None])

    order = ["norm_g", "w_in", "rel_bias", "w_att_out", "conv_w", "conv_b", "w_conv_out", "w_out", "final_norm_g"]
    outs = [loss, grad_x[None]]
    for which in range(4):
        for name in order:
            outs.append(big[name][which] if name in big else small_res[name][which])
    return tuple(outs)
```

```python
import numpy as np
import jax
import jax.numpy as jnp
from jax import lax
from jax.experimental import pallas as pl
from jax.experimental.pallas import tpu as pltpu

F32 = jnp.float32
BF16 = jnp.bfloat16
MESH = pl.DeviceIdType.MESH

CHUNK = 64
N_LEFT = 8
HEADS = 8
HEAD_DIM = 64
D_ATT = HEADS * HEAD_DIM
MAX_REL = 128
N_REL = 2 * MAX_REL + 1
EPS = 1e-6
NEG_BIG = -1e30
ADAM_LR, ADAM_B1, ADAM_B2, ADAM_EPS, ADAM_WD, ADAM_STEP = 0.001, 0.9, 0.999, 1e-08, 0.01, 10

LANES = 128
SUBLANES = 8
VMEM_LIMIT = 56 * 1024 * 1024

QB = 2 * CHUNK
KW = N_LEFT * CHUNK + QB
DIAG = KW + QB
TQ = N_LEFT * CHUNK
TM_MID = 256
TM_MM = 512
TM_BLK = 1024
SMALL_ROWS, SMALL_COLS = 16, 1024


def _params(*sem):
    return pltpu.CompilerParams(dimension_semantics=sem, vmem_limit_bytes=VMEM_LIMIT)


def _nt(a, b):
    return lax.dot_general(a, b, (((1,), (1,)), ((), ())), preferred_element_type=F32)


def _tn(a, b):
    return lax.dot_general(a, b, (((0,), (0,)), ((), ())), preferred_element_type=F32)


def _nn(a, b):
    return jnp.dot(a, b, preferred_element_type=F32)


def _diag_rel_index():
    d = np.arange(DIAG)
    diff = np.where(d < KW, d, d - DIAG)
    rel = N_LEFT * CHUNK - diff
    return np.clip(rel, -MAX_REL, MAX_REL) + MAX_REL


def _build_bias(diag_ref, bias_scr):
    r = lax.broadcasted_iota(jnp.int32, (QB, KW), 0) // CHUNK
    s = lax.broadcasted_iota(jnp.int32, (QB, KW), 1) // CHUNK
    allowed = (s >= r) & (s <= r + N_LEFT)
    for h in range(HEADS):
        row = jnp.broadcast_to(diag_ref[h:h + 1, :], (QB, DIAG))
        t = pltpu.roll(row, 0, 1, stride=1, stride_axis=0)
        bias_scr[h // 2, (h % 2) * QB:(h % 2 + 1) * QB, :] = jnp.where(allowed, t[:, :KW], NEG_BIG)


def _stack_heads(a, lane_hi):
    zero = jnp.zeros_like(a)
    return jnp.concatenate([jnp.where(lane_hi, zero, a), jnp.where(lane_hi, a, zero)], axis=0)


def _in_proj_gather(x, g, shard, order):
    s, d = x.shape
    tn = shard.shape[1]
    n = s // TM_BLK
    hr = d // 2

    def body(order_ref, x_ref, g_ref, shard32_ref, proj_ref, h_ref, wfull_ref, shard_ref, hbuf, wbuf,
             send1, recv1, send2, recv2, lsem):
        del order_ref
        j, i = pl.program_id(0), pl.program_id(1)
        x, y, c = _position()
        mine = 2 * x + y
        chips = [(x, 1 - y), (1 - x, y), (1 - x, 1 - y)]

        def theirs(k):
            return 2 * chips[k][0] + chips[k][1]

        def half_rows(half):
            return pl.ds(half * hr, hr)

        def ici(k, shard_index):
            return pltpu.make_async_remote_copy(
                src_ref=shard_ref.at[half_rows(c), :], dst_ref=wfull_ref.at[shard_index, half_rows(c), :],
                send_sem=send1.at[k], recv_sem=recv1.at[k], device_id=(*chips[k], c), device_id_type=MESH)

        def d2d(k, half):
            return pltpu.make_async_remote_copy(
                src_ref=wbuf.at[k % 2, half_rows(half), :], dst_ref=wfull_ref.at[theirs(k), half_rows(half), :],
                send_sem=send2.at[k], recv_sem=recv2.at[k], device_id=(x, y, 1 - c), device_id_type=MESH)

        def load(k, half, sem):
            return pltpu.make_async_copy(wfull_ref.at[theirs(k), half_rows(half), :],
                                         wbuf.at[k % 2, half_rows(half), :], lsem.at[sem])

        own = pltpu.make_async_copy(shard_ref, wfull_ref.at[mine], lsem.at[0])

        def put_proj(block):
            for grp in range(tn // D_ATT):
                proj_ref[grp] = block[:, grp * D_ATT:(grp + 1) * D_ATT].astype(BF16)

        @pl.when((j == 0) & (i == 0))
        def _():
            shard_ref[...] = shard32_ref[...].astype(BF16)
            own.start()
            ici(0, mine).start()
            ici(1, mine).start()

        for k in range(3):
            first = max(n - 3, 0) if k == 0 else min(n // 2, n - 1)

            @pl.when((j == k) & (i == first))
            def _(k=k):
                if k == 0:
                    ici(0, mine).wait_send()
                    ici(1, mine).wait_send()
                    ici(2, mine).start()
                if k == 2:
                    d2d(0, c).wait_send()
                ici(k, theirs(k)).wait_recv()
                load(k, c, 1).start()

            @pl.when((j == k) & (i == min(first + 1, n - 1)))
            def _(k=k):
                load(k, c, 1).wait()
                d2d(k, c).start()

            @pl.when((j == k) & (i == min(first + 2, n - 1)))
            def _(k=k):
                d2d(k, 1 - c).wait_recv()
                load(k, 1 - c, 2).start()

            @pl.when((j == k + 1) & (i == 0))
            def _(k=k):
                load(k, 1 - c, 2).wait()

        @pl.when(j == 0)
        def _():
            xv = x_ref[...]
            r = lax.rsqrt(jnp.mean(xv * xv, axis=-1, keepdims=True) + EPS)
            hv = ((xv * r) * g_ref[...]).astype(BF16)
            hbuf[i] = hv
            h_ref[...] = hv
            put_proj(_nn(hv, shard_ref[...]))

        for k in range(3):
            @pl.when(j == k + 1)
            def _(k=k):
                put_proj(_nn(hbuf[i], wbuf[k % 2]))

        @pl.when((j == 3) & (i == n - 1))
        def _():
            ici(2, mine).wait_send()
            d2d(1, c).wait_send()
            d2d(2, c).wait_send()
            own.wait()

    return pl.pallas_call(
        body, name="in_proj_gather",
        grid_spec=pltpu.PrefetchScalarGridSpec(
            num_scalar_prefetch=1, grid=(N_CHIPS, n),
            in_specs=[pl.BlockSpec((TM_BLK, d), lambda j, i, order: (jnp.where(j == 0, i, n - 1), 0)),
                      pl.BlockSpec((1, d), lambda j, i, order: (0, 0)), pl.BlockSpec(memory_space=pltpu.VMEM)],
            out_specs=[pl.BlockSpec((tn // D_ATT, TM_BLK, D_ATT), lambda j, i, order: (order[j], i, 0)),
                       pl.BlockSpec((TM_BLK, d), lambda j, i, order: (jnp.where(j == 0, i, n - 1), 0)), ANY],
            scratch_shapes=[pltpu.VMEM((d, tn), BF16), pltpu.VMEM((n, TM_BLK, d), BF16), pltpu.VMEM((2, d, tn), BF16),
                            pltpu.SemaphoreType.DMA((3,)), pltpu.SemaphoreType.DMA((3,)),
                            pltpu.SemaphoreType.DMA((3,)), pltpu.SemaphoreType.DMA((3,)), pltpu.SemaphoreType.DMA((3,))]),
        out_shape=[jax.ShapeDtypeStruct((N_CHIPS * tn // D_ATT, s, D_ATT), BF16), jax.ShapeDtypeStruct((s, d), BF16),
                   jax.ShapeDtypeStruct((N_CHIPS, d, tn), BF16)],
        compiler_params=_params("arbitrary", "arbitrary"),
    )(order, x, g, shard)


def _attn_fwd(diag, proj, shards, kinds, cw8):
    s = proj.shape[1]
    n = s // TQ
    nw = len(shards)
    scale = HEAD_DIM ** -0.5

    def body(*refs):
        diag_ref, q_ref, kp_ref, kc_ref, vp_ref, vc_ref = refs[:6]
        srcs, cw = refs[6:6 + nw], refs[6 + nw]
        o_ref, lse_ref = refs[7 + nw:9 + nw]
        dsts, cw_all = refs[9 + nw:9 + 2 * nw], refs[9 + 2 * nw]
        bias_scr = refs[10 + 2 * nw]
        casts = refs[11 + 2 * nw:11 + 3 * nw]
        start, forward, finish = _gather_plan(kinds, casts, dsts, cw, cw_all, *refs[11 + 3 * nw:])
        i = pl.program_id(0)

        @pl.when(i == 0)
        def _():
            for w in range(nw):
                casts[w][...] = srcs[w][...].astype(BF16)
            start()
            _build_bias(diag_ref, bias_scr)

        @pl.when(i == n // 2)
        def _():
            forward()

        @pl.when(i == n - 1)
        def _():
            finish()

        lane_hi = lax.broadcasted_iota(jnp.int32, (QB, LANES), 1) >= HEAD_DIM

        def block(b, first_tile):
            r0, n_prev = b * QB, TQ - b * QB
            n_cur = KW - n_prev
            pairs = range(HEADS // 2)
            lanes_of = [slice(LANES * p, LANES * (p + 1)) for p in pairs]
            scores = []
            for p in pairs:
                lanes = lanes_of[p]
                q2 = _stack_heads(q_ref[r0:r0 + QB, lanes] * scale, lane_hi)
                s_cur = _nt(q2, kc_ref[0:n_cur, lanes]) + bias_scr[p, :, n_prev:KW]
                if first_tile:
                    scores.append(s_cur)
                else:
                    scores.append(jnp.concatenate(
                        [_nt(q2, kp_ref[r0:TQ, lanes]) + bias_scr[p, :, 0:n_prev], s_cur], axis=1))
            probs = []
            for p in pairs:
                sc = scores[p]
                m = jnp.max(sc, axis=1, keepdims=True)
                pe = jnp.exp(sc - m)
                l = jnp.sum(pe, axis=1, keepdims=True)
                probs.append((pe.astype(BF16), l, m))
            for p in pairs:
                lanes = lanes_of[p]
                pb, l, m = probs[p]
                if first_tile:
                    o2 = _nn(pb, vc_ref[0:n_cur, lanes]) / l
                else:
                    o2 = (_nn(pb[:, 0:n_prev], vp_ref[r0:TQ, lanes]) + _nn(pb[:, n_prev:KW], vc_ref[0:n_cur, lanes])) / l
                lse2 = m + jnp.log(l)
                lse_ref[r0:r0 + QB, 2 * p:2 * p + 1] = lse2[0:QB, :]
                lse_ref[r0:r0 + QB, 2 * p + 1:2 * p + 2] = lse2[QB:2 * QB, :]
                o_ref[r0:r0 + QB, lanes] = jnp.where(lane_hi, o2[QB:2 * QB, :], o2[0:QB, :]).astype(BF16)

        @pl.when(i == 0)
        def _():
            for b in range(TQ // QB):
                block(b, True)

        @pl.when(i > 0)
        def _():
            for b in range(TQ // QB):
                block(b, False)

    blk = lambda grp, prev: pl.BlockSpec(
        (None, TQ, D_ATT), (lambda i: (grp, jnp.maximum(i - 1, 0), 0)) if prev else (lambda i: (grp, i, 0)))
    vmem = pl.BlockSpec(memory_space=pltpu.VMEM)
    return pl.pallas_call(
        body, name="attn_fwd", grid=(n,),
        in_specs=[pl.BlockSpec((HEADS, DIAG), lambda i: (0, 0)),
                  blk(0, False), blk(1, True), blk(1, False), blk(2, True), blk(2, False)] + [vmem] * (nw + 1),
        out_specs=[pl.BlockSpec((TQ, D_ATT), lambda i: (i, 0)), pl.BlockSpec((TQ, HEADS), lambda i: (i, 0))]
        + [ANY] * (nw + 1),
        out_shape=[jax.ShapeDtypeStruct((s, D_ATT), BF16), jax.ShapeDtypeStruct((s, HEADS), F32)]
        + _gather_out_shapes(shards, kinds, cw8),
        scratch_shapes=[pltpu.VMEM((HEADS // 2, 2 * QB, KW), F32)] + [pltpu.VMEM(a.shape, BF16) for a in shards]
        + _gather_sems(nw),
        compiler_params=_params("arbitrary"),
    )(diag, proj, proj, proj, proj, proj, *shards, cw8)


def _attn_bwd(diag, proj, d_att, att, lse, parts):
    s = proj.shape[1]
    n = s // TQ
    npart = len(parts)
    scale = HEAD_DIM ** -0.5
    rel_pad = 3 * LANES

    def body(*refs):
        diag_ref, q_ref, kp_ref, kc_ref, vp_ref, vc_ref, do_ref, o_ref, lse_ref = refs[:9]
        part_refs = refs[9:9 + npart]
        dqkv_ref, dbias_ref = refs[9 + npart:11 + npart]
        slot_refs = refs[11 + npart:11 + 2 * npart]
        bias_scr, dbias_acc, dk_acc, dv_acc, dq_scr = refs[11 + 2 * npart:16 + 2 * npart]
        start, finish = _scatter_plan(part_refs, slot_refs, *refs[16 + 2 * npart:])
        i = pl.program_id(0)
        cur, prv = i % 2, 1 - i % 2

        @pl.when(i == 0)
        def _():
            start()
            _build_bias(diag_ref, bias_scr)
            dbias_acc[...] = jnp.zeros_like(dbias_acc)
            dk_acc[...] = jnp.zeros_like(dk_acc)
            dv_acc[...] = jnp.zeros_like(dv_acc)

        @pl.when(i > 0)
        def _():
            dqkv_ref[:, 0:D_ATT] = dq_scr[...]
            dk_acc[cur] = jnp.zeros((TQ, D_ATT), F32)
            dv_acc[cur] = jnp.zeros((TQ, D_ATT), F32)

        lane_hi = lax.broadcasted_iota(jnp.int32, (QB, LANES), 1) >= HEAD_DIM
        col = lax.broadcasted_iota(jnp.int32, (2 * QB, KW), 1)

        def make_block(first_tile):
            def block(b):
                r0, n_prev = b * QB, TQ - b * QB
                n_cur = KW - n_prev
                for p in range(HEADS // 2):
                    lanes = slice(LANES * p, LANES * (p + 1))
                    q2 = _stack_heads(q_ref[r0:r0 + QB, lanes] * scale, lane_hi)
                    kw = jnp.concatenate([kp_ref[r0:TQ, lanes], kc_ref[0:n_cur, lanes]], axis=0)
                    vw = jnp.concatenate([vp_ref[r0:TQ, lanes], vc_ref[0:n_cur, lanes]], axis=0)
                    dop = do_ref[r0:r0 + QB, lanes]
                    do2 = _stack_heads(dop, lane_hi)
                    prod = dop.astype(F32) * o_ref[r0:r0 + QB, lanes].astype(F32)
                    delta2 = jnp.concatenate(
                        [jnp.sum(jnp.where(lane_hi, 0.0, prod), axis=1, keepdims=True),
                         jnp.sum(jnp.where(lane_hi, prod, 0.0), axis=1, keepdims=True)], axis=0)
                    lse2 = jnp.concatenate([lse_ref[r0:r0 + QB, 2 * p:2 * p + 1],
                                            lse_ref[r0:r0 + QB, 2 * p + 1:2 * p + 2]], axis=0)
                    sc = _nt(q2, kw) + bias_scr[p]
                    if first_tile:
                        sc = jnp.where(col >= TQ - r0, sc, NEG_BIG)
                    pr = jnp.exp(sc - lse2)
                    ds = pr * (_nt(do2, vw) - delta2)
                    dbias_acc[p] += ds
                    dsb = ds.astype(BF16)
                    dv_w = _tn(pr.astype(BF16), do2)
                    dk_w = _tn(dsb, q2)
                    dv_acc[prv, r0:TQ, lanes] += dv_w[0:n_prev, :]
                    dv_acc[cur, 0:n_cur, lanes] += dv_w[n_prev:KW, :]
                    dk_acc[prv, r0:TQ, lanes] += dk_w[0:n_prev, :]
                    dk_acc[cur, 0:n_cur, lanes] += dk_w[n_prev:KW, :]
                    dq2 = _nn(dsb, kw)
                    dq = jnp.where(lane_hi, dq2[QB:2 * QB, :], dq2[0:QB, :]) * scale
                    dq_scr[r0:r0 + QB, lanes] = dq.astype(BF16)
            return block

        @pl.when(i == 0)
        def _():
            for b in range(TQ // QB):
                make_block(True)(b)

        @pl.when((i > 0) & (i < n))
        def _():
            for b in range(TQ // QB):
                make_block(False)(b)

        @pl.when(i > 0)
        def _():
            dqkv_ref[:, D_ATT:2 * D_ATT] = dk_acc[prv].astype(BF16)
            dqkv_ref[:, 2 * D_ATT:3 * D_ATT] = dv_acc[prv].astype(BF16)

        @pl.when(i == n)
        def _():
            d_iota = lax.broadcasted_iota(jnp.int32, (DIAG, rel_pad), 0)
            n_iota = lax.broadcasted_iota(jnp.int32, (DIAG, rel_pad), 1)
            diff = jnp.where(d_iota < KW, d_iota, d_iota - DIAG)
            idx = jnp.clip(N_LEFT * CHUNK - diff, -MAX_REL, MAX_REL) + MAX_REL
            onehot = (idx == n_iota).astype(F32)
            rows = []
            for hd in range(HEADS):
                acc = dbias_acc[hd // 2, (hd % 2) * QB:(hd % 2 + 1) * QB, :]
                a = jnp.concatenate([acc, jnp.zeros((QB, DIAG - KW), F32)], axis=1)
                g8 = a[0:SUBLANES, :]
                for blk in range(1, QB // SUBLANES):
                    g8 = g8 + pltpu.roll(a[blk * SUBLANES:(blk + 1) * SUBLANES, :], DIAG - blk * SUBLANES, 1)
                g1 = g8[0:1, :]
                for r in range(1, SUBLANES):
                    g1 = g1 + pltpu.roll(g8[r:r + 1, :], DIAG - r, 1)
                rows.append(g1)
            g = jnp.concatenate(rows, axis=0)
            dbias_ref[...] = jnp.dot(g, onehot, preferred_element_type=F32, precision=lax.Precision.HIGHEST)
            finish()

    last = n - 1
    cur = lambda grp: pl.BlockSpec((None, TQ, D_ATT), lambda i: (grp, jnp.minimum(i, last), 0))
    prev = lambda grp: pl.BlockSpec((None, TQ, D_ATT), lambda i: (grp, jnp.maximum(jnp.minimum(i, last) - 1, 0), 0))
    tile = pl.BlockSpec((TQ, D_ATT), lambda i: (jnp.minimum(i, last), 0))
    return pl.pallas_call(
        body, name="attn_bwd", grid=(n + 1,),
        in_specs=[pl.BlockSpec((HEADS, DIAG), lambda i: (0, 0)),
                  cur(0), prev(1), cur(1), prev(2), cur(2), tile, tile,
                  pl.BlockSpec((TQ, HEADS), lambda i: (jnp.minimum(i, last), 0))] + [ANY] * npart,
        out_specs=[pl.BlockSpec((TQ, 3 * D_ATT), lambda i: (jnp.maximum(i - 1, 0), 0)),
                   pl.BlockSpec((HEADS, rel_pad), lambda i: (0, 0))] + [ANY] * npart,
        out_shape=[jax.ShapeDtypeStruct((s, 3 * D_ATT), BF16), jax.ShapeDtypeStruct((HEADS, rel_pad), F32)]
        + _scatter_out_shapes(parts),
        scratch_shapes=[pltpu.VMEM((HEADS // 2, 2 * QB, KW), F32), pltpu.VMEM((HEADS // 2, 2 * QB, KW), F32),
                        pltpu.VMEM((2, TQ, D_ATT), F32), pltpu.VMEM((2, TQ, D_ATT), F32),
                        pltpu.VMEM((TQ, D_ATT), BF16)] + _scatter_sems(npart),
        compiler_params=_params("arbitrary"),
    )(diag, proj, proj, proj, proj, proj, d_att, att, lse, *parts)


def _shift_down(a, k, halo):
    rolled = pltpu.roll(a, k, 0)
    row = lax.broadcasted_iota(jnp.int32, halo.shape, 0)
    first = jnp.where(row < k, pltpu.roll(halo, k, 0), rolled[0:SUBLANES, :])
    return jnp.concatenate([first, rolled[SUBLANES:, :]], axis=0)


def _shift_up(a, k, nxt):
    tm = a.shape[0]
    rolled = pltpu.roll(a, tm - k, 0)
    row = lax.broadcasted_iota(jnp.int32, nxt.shape, 0)
    last = jnp.where(row >= SUBLANES - k, pltpu.roll(nxt, SUBLANES - k, 0), rolled[tm - SUBLANES:, :])
    return jnp.concatenate([rolled[:tm - SUBLANES, :], last], axis=0)


def _sigmoid(v):
    return 0.5 * jnp.tanh(0.5 * v) + 0.5


def _mixer_mid(att, proj, x, tgt, w_att, w_conv, w_out, w_att_t, w_conv_t, w_out_t, conv_w8, conv_b, fin_g):
    s, d = x.shape
    dc = D_ATT
    n = s // TM_MID
    tm = TM_MID
    n_shards = 4

    def body(att_ref, za_ref, gb_ref, gc_ref, u_ref, zc_ref, hgc_ref, hu_ref, gatt_ref, gconv_ref, x_ref, t_ref,
             watt_ref, wconv_ref, wout_ref, watt_t_ref, wconv_t_ref, wout_t_ref, cw_ref, cb_ref, fg_ref,
             dpb_ref, do_ref, dx2_ref, gatt_o, gconv_o, gout_o, loss_o, gfn_o, gcb_o, gcw_o,
             acc_att, acc_conv, acc_out, carry):
        i = pl.program_id(0)
        tile = n - 1 - i

        @pl.when(i == 0)
        def _():
            acc_att[...] = jnp.zeros_like(acc_att)
            acc_conv[...] = jnp.zeros_like(acc_conv)
            acc_out[...] = jnp.zeros_like(acc_out)
            carry[...] = jnp.zeros_like(carry)
            loss_o[...] = jnp.zeros_like(loss_o)
            gfn_o[...] = jnp.zeros_like(gfn_o)
            gcb_o[...] = jnp.zeros_like(gcb_o)
            gcw_o[...] = jnp.zeros_like(gcw_o)

        halves = [slice(hh * (tm // 2), (hh + 1) * (tm // 2)) for hh in range(2)]
        both = lambda fn: [fn(rows) for rows in halves]
        f32 = lambda ref, rows: ref[rows, :].astype(F32)

        gc = gc_ref[...].astype(F32)
        u = u_ref[...].astype(F32)
        cu = gc * u
        halo = jnp.where(tile > 0, hgc_ref[...].astype(F32) * hu_ref[...].astype(F32), 0.0)
        cu1 = _shift_down(cu, 1, halo)
        cu2 = _shift_down(cu, 2, halo)
        w0, w1, w2 = cw_ref[0:1, :], cw_ref[1:2, :], cw_ref[2:3, :]
        fg = fg_ref[...]

        def stage_a(rows):
            att_v, za, zc, gb = f32(att_ref, rows), f32(za_ref, rows), f32(zc_ref, rows), f32(gb_ref, rows)
            sa = _sigmoid(za)
            silu_a = za * sa
            vconv = w0 * cu2[rows, :] + w1 * cu1[rows, :] + w2 * cu[rows, :] + cb_ref[...]
            sc = _sigmoid(zc)
            silu_c = zc * sc
            return dict(att_v=att_v, za=za, zc=zc, gb=gb, sa=sa, silu_a=silu_a, vconv=vconv, sc=sc, silu_c=silu_c,
                        a_b=(att_v * silu_a).astype(BF16), c_b=(gb * vconv * silu_c).astype(BF16))

        st = both(stage_a)
        for t in st:
            t["y_att"] = _nn(t["a_b"], watt_ref[...])
            t["y_conv"] = _nn(t["c_b"], wconv_ref[...])
        for t, rows in zip(st, halves):
            gpair = lambda ref: jnp.concatenate([ref[0, rows, :], ref[1, rows, :]], axis=1).astype(F32)
            t["ga"] = _sigmoid(gpair(gatt_ref))
            t["gv"] = _sigmoid(gpair(gconv_ref))
            t["m_b"] = (t["ga"] * t["y_att"] + t["gv"] * t["y_conv"]).astype(BF16)
        for t in st:
            t["mo"] = _nn(t["m_b"], wout_ref[...])
        for t, rows in zip(st, halves):
            x2 = x_ref[rows, :] + t["mo"]
            r2 = lax.rsqrt(jnp.mean(x2 * x2, axis=-1, keepdims=True) + EPS)
            x2n = x2 * r2
            err = x2n * fg - t_ref[rows, :]
            loss_o[...] += jnp.sum(err * err, axis=0, keepdims=True) * (0.5 / d)
            dy = err * (1.0 / d)
            gfn_o[...] += jnp.sum(dy * x2n, axis=0, keepdims=True)
            dyn = dy * fg
            dx2 = r2 * (dyn - x2n * jnp.mean(dyn * x2n, axis=-1, keepdims=True))
            dx2_ref[rows, :] = dx2
            t["dx2_b"] = dx2.astype(BF16)
        for t in st:
            t["dm"] = _nn(t["dx2_b"], wout_t_ref[...])
        whole = lambda key: jnp.concatenate([st[0][key], st[1][key]], axis=0)
        acc_out[...] += _tn(whole("m_b"), whole("dx2_b"))
        for t, rows in zip(st, halves):
            dy_att = t["dm"] * t["ga"]
            dy_conv = t["dm"] * t["gv"]
            dpb_ref[rows, 5 * dc:5 * dc + d] = (dy_att * t["y_att"] * (1.0 - t["ga"])).astype(BF16)
            dpb_ref[rows, 5 * dc + d:5 * dc + 2 * d] = (dy_conv * t["y_conv"] * (1.0 - t["gv"])).astype(BF16)
            t["dya_b"] = dy_att.astype(BF16)
            t["dyc_b"] = dy_conv.astype(BF16)
        for t in st:
            t["da_in"] = _nn(t["dya_b"], watt_t_ref[...])
            t["dc_in"] = _nn(t["dyc_b"], wconv_t_ref[...])
        acc_att[...] += _tn(whole("a_b"), whole("dya_b"))
        acc_conv[...] += _tn(whole("c_b"), whole("dyc_b"))
        for t, rows in zip(st, halves):
            sa, za, sc, zc = t["sa"], t["za"], t["sc"], t["zc"]
            do_ref[rows, :] = (t["da_in"] * t["silu_a"]).astype(BF16)
            dpb_ref[rows, 0:dc] = (t["da_in"] * t["att_v"] * (sa * (1.0 + za * (1.0 - sa)))).astype(BF16)
            dpb_ref[rows, dc:2 * dc] = (t["dc_in"] * t["vconv"] * t["silu_c"]).astype(BF16)
            dgs = t["dc_in"] * t["gb"]
            t["dvc"] = dgs * t["silu_c"]
            dpb_ref[rows, 4 * dc:5 * dc] = (dgs * t["vconv"] * (sc * (1.0 + zc * (1.0 - sc)))).astype(BF16)
        dvc = whole("dvc")
        gcb_o[...] += jnp.sum(dvc, axis=0, keepdims=True)
        gcw_o[0:1, :] += jnp.sum(dvc * cu2, axis=0, keepdims=True)
        gcw_o[1:2, :] += jnp.sum(dvc * cu1, axis=0, keepdims=True)
        gcw_o[2:3, :] += jnp.sum(dvc * cu, axis=0, keepdims=True)
        nxt = carry[...]
        dcu = w2 * dvc + w1 * _shift_up(dvc, 1, nxt) + w0 * _shift_up(dvc, 2, nxt)
        carry[...] = dvc[0:SUBLANES, :]
        dpb_ref[:, 2 * dc:3 * dc] = (dcu * u).astype(BF16)
        dpb_ref[:, 3 * dc:4 * dc] = (dcu * gc).astype(BF16)

        @pl.when(i == n - 1)
        def _():
            for j in range(n_shards):
                gatt_o[j] = acc_att[:, j * (d // n_shards):(j + 1) * (d // n_shards)].astype(BF16)
                gconv_o[j] = acc_conv[:, j * (d // n_shards):(j + 1) * (d // n_shards)].astype(BF16)
                gout_o[j] = acc_out[j * (d // n_shards):(j + 1) * (d // n_shards), :].astype(BF16)

    rev = lambda width, col_blk: pl.BlockSpec((tm, width), lambda i: (n - 1 - i, col_blk))
    grp = lambda g: pl.BlockSpec((None, tm, dc), lambda i: (g, n - 1 - i, 0))
    grp2 = lambda g2: pl.BlockSpec((2, tm, dc), lambda i: (g2, n - 1 - i, 0))
    halo_spec = lambda g: pl.BlockSpec(
        (None, SUBLANES, dc), lambda i: (g, jnp.maximum((n - 1 - i) * (tm // SUBLANES) - 1, 0), 0))
    const = lambda shape: pl.BlockSpec(shape, lambda i: tuple(0 for _ in shape), pipeline_mode=pl.Buffered(1))
    q4 = d // n_shards
    return pl.pallas_call(
        body, name="mixer_mid", grid=(n,),
        in_specs=[rev(dc, 0), grp(3), grp(4), grp(5), grp(6), grp(7),
                  halo_spec(5), halo_spec(6), grp2(4), grp2(5), rev(d, 0), rev(d, 0),
                  const((dc, d)), const((dc, d)), const((d, d)), const((d, dc)), const((d, dc)), const((d, d)),
                  const((SUBLANES, dc)), const((1, dc)), const((1, d))],
        out_specs=[rev(5 * dc + 2 * d, 0), rev(dc, 0), rev(d, 0),
                   const((n_shards, dc, q4)), const((n_shards, dc, q4)), const((n_shards, q4, d)),
                   const((1, d)), const((1, d)), const((1, dc)), const((SUBLANES, dc))],
        out_shape=[jax.ShapeDtypeStruct((s, 5 * dc + 2 * d), BF16), jax.ShapeDtypeStruct((s, dc), BF16),
                   jax.ShapeDtypeStruct((s, d), F32),
                   jax.ShapeDtypeStruct((n_shards, dc, q4), BF16), jax.ShapeDtypeStruct((n_shards, dc, q4), BF16),
                   jax.ShapeDtypeStruct((n_shards, q4, d), BF16),
                   jax.ShapeDtypeStruct((1, d), F32), jax.ShapeDtypeStruct((1, d), F32),
                   jax.ShapeDtypeStruct((1, dc), F32), jax.ShapeDtypeStruct((SUBLANES, dc), F32)],
        scratch_shapes=[pltpu.VMEM((dc, d), F32), pltpu.VMEM((dc, d), F32), pltpu.VMEM((d, d), F32),
                        pltpu.VMEM((SUBLANES, dc), F32)],
        compiler_params=_params("arbitrary"),
    )(att, proj, proj, proj, proj, proj, proj, proj, proj, proj, x, tgt,
      w_att, w_conv, w_out, w_att_t, w_conv_t, w_out_t, conv_w8, conv_b, fin_g)


def _in_proj_bwd_x(dqkv, dpb, w_in, x, dx2, g):
    s, d = x.shape
    tn = dqkv.shape[1]
    nb = dpb.shape[1] // tn
    n = s // TM_MM

    def body(*refs):
        dps, ws = refs[:nb + 1], refs[nb + 1:2 * nb + 2]
        x_ref, dx2_ref, g_ref, gx_ref, gng_ref = refs[2 * nb + 2:]
        i = pl.program_id(0)

        @pl.when(i == 0)
        def _():
            gng_ref[...] = jnp.zeros_like(gng_ref)

        dh = _nt(dps[0][...], ws[0][0])
        for j in range(1, nb + 1):
            dh = dh + _nt(dps[j][...], ws[j][0])
        xv = x_ref[...]
        r = lax.rsqrt(jnp.mean(xv * xv, axis=-1, keepdims=True) + EPS)
        xn = xv * r
        gng_ref[...] += jnp.sum(dh * xn, axis=0, keepdims=True)
        dhn = dh * g_ref[...]
        gx_ref[...] = dx2_ref[...] + r * (dhn - xn * jnp.mean(dhn * xn, axis=-1, keepdims=True))

    tile = lambda width, col_blk: pl.BlockSpec((TM_MM, width), lambda i: (i, col_blk))
    wspec = lambda blk: pl.BlockSpec((1, d, tn), lambda i: (blk, 0, 0), pipeline_mode=pl.Buffered(1))
    return pl.pallas_call(
        body, name="in_proj_bwd_x", grid=(n,),
        in_specs=[tile(tn, 0)] + [tile(tn, j) for j in range(nb)] + [wspec(j) for j in range(nb + 1)]
        + [tile(d, 0), tile(d, 0), pl.BlockSpec((1, d), lambda i: (0, 0))],
        out_specs=[tile(d, 0), pl.BlockSpec((1, d), lambda i: (0, 0))],
        out_shape=[jax.ShapeDtypeStruct((s, d), F32), jax.ShapeDtypeStruct((1, d), F32)],
        compiler_params=_params("arbitrary"),
    )(dqkv, *([dpb] * nb), *([w_in] * (nb + 1)), x, dx2, g)


def _in_proj_bwd_w(h, dqkv, dpb, order):
    s, d = h.shape
    tn = dqkv.shape[1]
    n = s // TM_BLK
    hr = d // 2
    settle = min(2, n - 1)

    def body(order_ref, h_ref, da_ref, db_ref, slots_ref, acc, sendbuf, pairbuf, chipbuf, psend, precv, send, recv, lsem):
        j, i = pl.program_id(0), pl.program_id(1)
        blk = order_ref[j]
        pos = _position()
        x, y, c = pos

        @pl.when(i == 0)
        def _():
            acc[...] = jnp.zeros_like(acc)

        @pl.when(blk == 0)
        def _():
            acc[...] += _tn(h_ref[...], da_ref[...])

        @pl.when(blk > 0)
        def _():
            acc[...] += _tn(h_ref[...], db_ref[...])

        def pair(step, half):
            return pltpu.make_async_remote_copy(
                src_ref=sendbuf.at[step, pl.ds(half * hr, hr), :], dst_ref=pairbuf.at[step],
                send_sem=psend.at[step], recv_sem=precv.at[step], device_id=(x, y, 1 - c), device_id_type=MESH)

        def ici(step):
            flip = OWNER_FLIPS[step]
            return pltpu.make_async_remote_copy(
                src_ref=chipbuf.at[step], dst_ref=slots_ref.at[flip], send_sem=send.at[step], recv_sem=recv.at[step],
                device_id=_peer(pos, 4 * (flip >> 1) + 2 * (flip & 1)), device_id_type=MESH)

        local = pltpu.make_async_copy(chipbuf.at[N_CHIPS - 1], slots_ref.at[0], lsem.at[0])

        def combine(step):
            pair(step, c).wait_recv()
            mine = sendbuf[step, pl.ds(c * hr, hr), :].astype(F32)
            chipbuf[step] = (mine + pairbuf[step].astype(F32)).astype(BF16)

        for step in range(N_CHIPS):
            @pl.when((j == step) & (i == n - 1))
            def _(step=step):
                sendbuf[step] = acc[...].astype(BF16)
                pair(step, 1 - c).start()

        for step in range(N_CHIPS - 1):
            @pl.when((j == step + 1) & (i == settle))
            def _(step=step):
                combine(step)
                ici(step).start()

        @pl.when((j == N_CHIPS - 1) & (i == n - 1))
        def _():
            combine(N_CHIPS - 1)
            local.start()
            for step in range(N_CHIPS - 1):
                ici(step).wait_recv()
            for step in range(N_CHIPS - 1):
                ici(step).wait_send()
            for step in range(N_CHIPS):
                pair(step, 1 - c).wait_send()
            local.wait()

    return pl.pallas_call(
        body, name="in_proj_bwd_w",
        grid_spec=pltpu.PrefetchScalarGridSpec(
            num_scalar_prefetch=1, grid=(N_CHIPS, n),
            in_specs=[pl.BlockSpec((TM_BLK, d), lambda j, i, order: (i, 0)),
                      pl.BlockSpec((TM_BLK, tn), lambda j, i, order: (jnp.where(order[j] == 0, i, 0), 0)),
                      pl.BlockSpec((TM_BLK, tn), lambda j, i, order: (jnp.where(order[j] == 0, 0, i),
                                                                     jnp.maximum(order[j] - 1, 0)))],
            out_specs=[ANY],
            scratch_shapes=[pltpu.VMEM((d, tn), F32), pltpu.VMEM((N_CHIPS, d, tn), BF16),
                            pltpu.VMEM((N_CHIPS, hr, tn), BF16), pltpu.VMEM((N_CHIPS, hr, tn), BF16),
                            pltpu.SemaphoreType.DMA((N_CHIPS,)), pltpu.SemaphoreType.DMA((N_CHIPS,)),
                            pltpu.SemaphoreType.DMA((N_CHIPS - 1,)), pltpu.SemaphoreType.DMA((N_CHIPS - 1,)),
                            pltpu.SemaphoreType.DMA((1,))]),
        out_shape=[jax.ShapeDtypeStruct((N_CHIPS, hr, tn), BF16)],
        compiler_params=_params("arbitrary", "arbitrary"),
    )(order, h, dqkv, dpb)[0]


LOSS_ROW = 6


def _adam_update(w, g, m, v):
    c1 = 1.0 / (1.0 - ADAM_B1 ** ADAM_STEP)
    c2 = 1.0 / (1.0 - ADAM_B2 ** ADAM_STEP)
    m2 = ADAM_B1 * m + (1.0 - ADAM_B1) * g
    v2 = ADAM_B2 * v + (1.0 - ADAM_B2) * (g * g)
    return -ADAM_LR * ((m2 * c1) / (jnp.sqrt(v2 * c2) + ADAM_EPS) + ADAM_WD * w), m2, v2


def _adamw_small(recv, params, moments_m, moments_v):
    k = recv.shape[0]
    n_par = len(params)
    cshard = params[3].shape[1]

    def body(*refs):
        r_ref = refs[0]
        ws, ms, vs = refs[1:1 + n_par], refs[1 + n_par:1 + 2 * n_par], refs[1 + 2 * n_par:1 + 3 * n_par]
        loss_ref = refs[1 + 3 * n_par]
        outs = refs[2 + 3 * n_par:]
        total = r_ref[0]
        for slot in range(1, k):
            total = total + r_ref[slot]
        loss_ref[...] = jnp.sum(total[LOSS_ROW:LOSS_ROW + 1, :], axis=1, keepdims=True)
        chip = 2 * lax.axis_index("x") + lax.axis_index("y")
        g_cw = jnp.zeros((3, cshard), F32)
        for sh in range(N_CHIPS):
            g_cw = g_cw + jnp.where(chip == sh, total[3:6, sh * cshard:(sh + 1) * cshard], 0.0)
        grads = [total[0:1, :], total[1:2, :], total[2:3, :ws[2].shape[1]], g_cw, total[8:16, :ws[4].shape[1]]]
        for p in range(n_par):
            delta, m2, v2 = _adam_update(ws[p][...], grads[p], ms[p][...], vs[p][...])
            for q, val in enumerate((grads[p], delta, m2, v2)):
                outs[4 * p + q][...] = val

    shapes = [jax.ShapeDtypeStruct((1, 1), F32)]
    for p in params:
        shapes += [jax.ShapeDtypeStruct(p.shape, F32)] * 4
    return pl.pallas_call(body, name="adamw_small", out_shape=shapes)(recv, *params, *moments_m, *moments_v)


def _adamw(w, g, m, v, name, rows_per_step):
    r, c = w.shape

    def body(w_ref, g_ref, m_ref, v_ref, go_ref, d_ref, mo_ref, vo_ref):
        gv = g_ref[...]
        go_ref[...] = gv
        d_ref[...], mo_ref[...], vo_ref[...] = _adam_update(w_ref[...], gv, m_ref[...], v_ref[...])

    spec = pl.BlockSpec((rows_per_step, c), lambda i: (i, 0))
    shape = jax.ShapeDtypeStruct((r, c), F32)
    return pl.pallas_call(
        body, name=name, grid=(r // rows_per_step,),
        in_specs=[spec] * 4, out_specs=[spec] * 4, out_shape=[shape] * 4,
        compiler_params=_params("parallel"),
    )(w, g, m, v)


def _adamw_group(ws, gs, ms, vs, name):
    k = len(ws)

    def body(*refs):
        ins, outs = refs[:4 * k], refs[4 * k:]
        for j in range(k):
            gv = ins[k + j][...]
            outs[4 * j][...] = gv
            outs[4 * j + 1][...], outs[4 * j + 2][...], outs[4 * j + 3][...] = _adam_update(
                ins[j][...], gv, ins[2 * k + j][...], ins[3 * k + j][...])

    shapes = []
    for w in ws:
        shapes += [jax.ShapeDtypeStruct(w.shape, F32)] * 4
    out = pl.pallas_call(body, name=name, out_shape=shapes,
                         compiler_params=pltpu.CompilerParams(vmem_limit_bytes=VMEM_LIMIT))(*ws, *gs, *ms, *vs)
    return [tuple(out[4 * j:4 * j + 4]) for j in range(k)]


ANY = pl.BlockSpec(memory_space=pl.ANY)
N_CHIPS = 4
N_DEV = 8
OWNER_FLIPS = (3, 1, 2, 0)


def _position():
    return lax.axis_index("x"), lax.axis_index("y"), lax.axis_index("c")


def _gather_out_shapes(shards, kinds, cw8):
    full = [(a.shape[0], a.shape[1] * N_CHIPS) if k == "cols" else (a.shape[0] * N_CHIPS, a.shape[1])
            for a, k in zip(shards, kinds)]
    return [jax.ShapeDtypeStruct(f, BF16) for f in full] + [
        jax.ShapeDtypeStruct((N_CHIPS,) + cw8.shape, cw8.dtype)]


def _gather_sems(nw):
    return [pltpu.SemaphoreType.DMA((3, nw)), pltpu.SemaphoreType.DMA((3, nw)),
            pltpu.SemaphoreType.DMA((3, nw)), pltpu.SemaphoreType.DMA((3, nw)),
            pltpu.SemaphoreType.DMA((3,)), pltpu.SemaphoreType.DMA((3,)), pltpu.SemaphoreType.DMA((nw + 1,))]


def _gather_plan(kinds, srcs, dsts, cw, cw_all, send1, recv1, send2, recv2, ssend, srecv, lsem):
    nw = len(srcs)
    x, y, c = _position()
    mine = 2 * x + y
    chips = [(x, 1 - y), (1 - x, y), (1 - x, 1 - y)]

    def window(w, shard, half):
        r, cc = srcs[w].shape
        hr = r // 2
        if kinds[w] == "cols":
            rows = pl.ds(0, r) if half is None else pl.ds(half * hr, hr)
            return dsts[w].at[rows, pl.ds(shard * cc, cc)]
        rows = pl.ds(shard * r, r) if half is None else pl.ds(shard * r + half * hr, hr)
        return dsts[w].at[rows, :]

    def my_half(w):
        hr = srcs[w].shape[0] // 2
        return srcs[w].at[pl.ds(c * hr, hr), :]

    def local():
        return [pltpu.make_async_copy(srcs[w], window(w, mine, None), lsem.at[w]) for w in range(nw)] + [
            pltpu.make_async_copy(cw, cw_all.at[mine], lsem.at[nw])]

    def ici(k, w, shard):
        kx, ky = chips[k]
        return pltpu.make_async_remote_copy(
            src_ref=my_half(w), dst_ref=window(w, shard, c), send_sem=send1.at[k, w], recv_sem=recv1.at[k, w],
            device_id=(kx, ky, c), device_id_type=MESH)

    def d2d(k, w, shard, half):
        return pltpu.make_async_remote_copy(
            src_ref=window(w, shard, half), dst_ref=window(w, shard, half),
            send_sem=send2.at[k, w], recv_sem=recv2.at[k, w], device_id=(x, y, 1 - c), device_id_type=MESH)

    def small(k, shard):
        kx, ky = chips[k]
        return pltpu.make_async_remote_copy(
            src_ref=cw, dst_ref=cw_all.at[shard], send_sem=ssend.at[k], recv_sem=srecv.at[k],
            device_id=(kx, ky, c), device_id_type=MESH)

    def theirs(k):
        kx, ky = chips[k]
        return 2 * kx + ky

    def start():
        for cp in local():
            cp.start()
        for k in range(3):
            for w in range(nw):
                ici(k, w, mine).start()
            small(k, mine).start()

    def forward():
        for k in range(3):
            for w in range(nw):
                ici(k, w, theirs(k)).wait_recv()
                d2d(k, w, theirs(k), c).start()

    def finish():
        for k in range(3):
            for w in range(nw):
                d2d(k, w, theirs(k), 1 - c).wait_recv()
            small(k, theirs(k)).wait_recv()
        for k in range(3):
            for w in range(nw):
                ici(k, w, mine).wait_send()
                d2d(k, w, theirs(k), c).wait_send()
            small(k, mine).wait_send()
        for cp in local():
            cp.wait()

    return start, forward, finish


def _scatter_out_shapes(parts):
    return [jax.ShapeDtypeStruct((N_DEV, p.shape[1] // 2, p.shape[2]), p.dtype) for p in parts]


def _scatter_sems(nw):
    return [pltpu.SemaphoreType.DMA((N_DEV, nw)), pltpu.SemaphoreType.DMA((N_DEV, nw)), pltpu.SemaphoreType.DMA((nw,))]


def _peer(pos, k):
    x, y, c = pos
    return ((1 - x) if k & 4 else x, (1 - y) if k & 2 else y, (1 - c) if k & 1 else c)


def _scatter_plan(srcs, dsts, send, recv, lsem):
    nw = len(srcs)
    pos = _position()

    def piece(w, k):
        px, py, pc = _peer(pos, k)
        hr = srcs[w].shape[1] // 2
        return srcs[w].at[2 * px + py, pl.ds(pc * hr, hr), :]

    def remote(w, k):
        return pltpu.make_async_remote_copy(
            src_ref=piece(w, k), dst_ref=dsts[w].at[k], send_sem=send.at[k, w], recv_sem=recv.at[k, w],
            device_id=_peer(pos, k), device_id_type=MESH)

    def local(w):
        return pltpu.make_async_copy(piece(w, 0), dsts[w].at[0], lsem.at[w])

    def start():
        for w in range(nw):
            local(w).start()
        for k in range(1, N_DEV):
            for w in range(nw):
                remote(w, k).start()

    def finish():
        for k in range(1, N_DEV):
            for w in range(nw):
                remote(w, k).wait_recv()
        for k in range(1, N_DEV):
            for w in range(nw):
                remote(w, k).wait_send()
        for w in range(nw):
            local(w).wait()

    return start, finish


def _reduce_pair(slots, small):
    nw = len(slots)

    def body(*refs):
        srcs, sm = refs[:nw], refs[nw]
        dsts, sm_all = refs[nw + 1:2 * nw + 1], refs[2 * nw + 1]
        halves = refs[2 * nw + 2:3 * nw + 2]
        send, recv, ssend, srecv, lsem = refs[3 * nw + 2:]
        pos = _position()
        x, y, c = pos
        me = 4 * x + 2 * y + c

        def rows(w, half):
            hr = halves[w].shape[0]
            return dsts[w].at[pl.ds(half * hr, hr), :]

        def remote(w, half):
            return pltpu.make_async_remote_copy(
                src_ref=halves[w], dst_ref=rows(w, half), send_sem=send.at[w], recv_sem=recv.at[w],
                device_id=(x, y, 1 - c), device_id_type=MESH)

        def bcast(k, slot):
            return pltpu.make_async_remote_copy(
                src_ref=sm, dst_ref=sm_all.at[slot], send_sem=ssend.at[k], recv_sem=srecv.at[k],
                device_id=_peer(pos, k), device_id_type=MESH)

        small_copies = [bcast(k, me) for k in range(1, N_DEV)]
        own_small = pltpu.make_async_copy(sm, sm_all.at[me], lsem.at[nw])
        for cp in small_copies + [own_small]:
            cp.start()
        big = []
        for w in range(nw):
            total = srcs[w][0].astype(F32)
            for k in range(1, srcs[w].shape[0]):
                total = total + srcs[w][k].astype(F32)
            halves[w][...] = total
            big += [remote(w, c), pltpu.make_async_copy(halves[w], rows(w, c), lsem.at[w])]
            big[-2].start()
            big[-1].start()
        for w in range(nw):
            remote(w, 1 - c).wait_recv()
        for k in range(1, N_DEV):
            px, py, pc = _peer(pos, k)
            bcast(k, 4 * px + 2 * py + pc).wait_recv()
        for w in range(nw):
            big[2 * w].wait_send()
            big[2 * w + 1].wait()
        for cp in small_copies:
            cp.wait_send()
        own_small.wait()

    vmem = pl.BlockSpec(memory_space=pltpu.VMEM)
    half_shapes = [(sl.shape[1], sl.shape[2]) for sl in slots]
    return pl.pallas_call(
        body, name="reduce_pair",
        in_specs=[vmem] * (nw + 1), out_specs=[ANY] * (nw + 1),
        out_shape=[jax.ShapeDtypeStruct((2 * r, cc), F32) for r, cc in half_shapes]
        + [jax.ShapeDtypeStruct((N_DEV,) + small.shape, small.dtype)],
        scratch_shapes=[pltpu.VMEM(hs, F32) for hs in half_shapes]
        + [pltpu.SemaphoreType.DMA((nw,)), pltpu.SemaphoreType.DMA((nw,)),
           pltpu.SemaphoreType.DMA((N_DEV,)), pltpu.SemaphoreType.DMA((N_DEV,)),
           pltpu.SemaphoreType.DMA((nw + 1,))],
        compiler_params=pltpu.CompilerParams(vmem_limit_bytes=VMEM_LIMIT),
    )(*slots, small)


def _pad_to(a, rows, cols):
    return jnp.pad(a, ((0, rows - a.shape[0]), (0, cols - a.shape[1])))


def _pack_small(norm_g, fin_g, conv_b, conv_w, loss_vec, rel):
    rows = [_pad_to(norm_g, 1, SMALL_COLS), _pad_to(fin_g, 1, SMALL_COLS), _pad_to(conv_b, 1, SMALL_COLS),
            _pad_to(conv_w, 3, SMALL_COLS), _pad_to(loss_vec, 2, SMALL_COLS), _pad_to(rel, HEADS, SMALL_COLS)]
    return jnp.concatenate(rows, axis=0)


def kernel(x, norm_g, w_in, rel_bias, w_att_out, conv_w, conv_b, w_conv_out, w_out, final_norm_g, loss_target, m_norm_g, m_w_in, m_rel_bias, m_w_att_out, m_conv_w, m_conv_b, m_w_conv_out, m_w_out, m_final_norm_g, v_norm_g, v_w_in, v_rel_bias, v_w_att_out, v_conv_w, v_conv_b, v_w_conv_out, v_w_out, v_final_norm_g):
    xs, tgt = x[0], loss_target[0]
    cshard = conv_w.shape[2]
    chip = 2 * lax.axis_index("x") + lax.axis_index("y")

    shards = [w_in[0], w_att_out[0], w_conv_out[0], w_out[0]]
    cw8 = _pad_to(conv_w[0], SUBLANES, cshard)
    flips = jnp.arange(N_CHIPS, dtype=jnp.int32)
    own_first = jnp.bitwise_xor(chip, flips)
    own_last = jnp.bitwise_xor(chip, jnp.asarray(OWNER_FLIPS, jnp.int32))

    proj, h, wb_in = _in_proj_gather(xs, norm_g, shards[0], own_first)
    diag = jnp.take(rel_bias[0], _diag_rel_index(), axis=1)
    att, lse, wb_att, wb_conv, wb_out, cw_all = _attn_fwd(diag, proj, shards[1:], ["cols", "cols", "rows"], cw8)
    conv_w_full = jnp.transpose(cw_all, (1, 0, 2)).reshape(SUBLANES, N_CHIPS * cshard)
    (dpb, d_att, dx2, g_att_p, g_conv_p, g_out_p, loss_vec, g_fin, g_cb, g_cw) = _mixer_mid(
        att, proj, xs, tgt, wb_att, wb_conv, wb_out, wb_att.T, wb_conv.T, wb_out.T, conv_w_full, conv_b,
        final_norm_g[None, :])
    dqkv, g_rel, r_att, r_conv, r_out = _attn_bwd(diag, proj, d_att, att, lse, [g_att_p, g_conv_p, g_out_p])
    grad_x, g_norm = _in_proj_bwd_x(dqkv, dpb, wb_in, xs, dx2, norm_g)
    r_in = _in_proj_bwd_w(h, dqkv, dpb, own_last)

    small = _pack_small(g_norm, g_fin, g_cb, g_cw[0:3], loss_vec, g_rel)
    gw_in, gw_att, gw_conv, gw_out, r_small = _reduce_pair([r_in, r_att, r_conv, r_out], small)
    small_out = _adamw_small(
        r_small,
        [norm_g, final_norm_g[None, :], conv_b, conv_w[0], rel_bias[0]],
        [m_norm_g, m_final_norm_g[None, :], m_conv_b, m_conv_w[0], m_rel_bias[0]],
        [v_norm_g, v_final_norm_g[None, :], v_conv_b, v_conv_w[0], v_rel_bias[0]])
    loss = small_out[0][0, 0]
    small_names = ["norm_g", "final_norm_g", "conv_b", "conv_w", "rel_bias"]
    fix = {"norm_g": lambda a: a, "final_norm_g": lambda a: a[0], "conv_b": lambda a: a,
           "conv_w": lambda a: a[None], "rel_bias": lambda a: a[None]}
    small_res = {name: [fix[name](small_out[1 + 4 * p + q]) for q in range(4)] for p, name in enumerate(small_names)}

    big = {"w_in": _adamw(w_in[0], gw_in, m_w_in[0], v_w_in[0], "adamw_w_in", 128)}
    names = ["w_att_out", "w_conv_out", "w_out"]
    group = _adamw_group([w_att_out[0], w_conv_out[0], w_out[0]], [gw_att, gw_conv, gw_out],
                         [m_w_att_out[0], m_w_conv_out[0], m_w_out[0]],
                         [v_w_att_out[0], v_w_conv_out[0], v_w_out[0]], "adamw_small_matrices")
    big.update(zip(names, group))
    big = {name: tuple(a[None] for a in four) for name, four in big.items()}

    order = ["norm_g", "w_in", "rel_bias", "w_att_out", "conv_w", "conv_b", "w_conv_out", "w_out", "final_norm_g"]
    outs = [loss, grad_x[None]]
    for which in range(4):
        for name in order:
            outs.append(big[name][which] if name in big else small_res[name][which])
    return tuple(outs)
```

```python
import numpy as np
import jax
import jax.numpy as jnp
from jax import lax
from jax.experimental import pallas as pl
from jax.experimental.pallas import tpu as pltpu

F32 = jnp.float32
BF16 = jnp.bfloat16
MESH = pl.DeviceIdType.MESH

CHUNK = 64
N_LEFT = 8
HEADS = 8
HEAD_DIM = 64
D_ATT = HEADS * HEAD_DIM
MAX_REL = 128
N_REL = 2 * MAX_REL + 1
EPS = 1e-6
NEG_BIG = -1e30
ADAM_LR, ADAM_B1, ADAM_B2, ADAM_EPS, ADAM_WD, ADAM_STEP = 0.001, 0.9, 0.999, 1e-08, 0.01, 10

LANES = 128
SUBLANES = 8
VMEM_LIMIT = 56 * 1024 * 1024

QB = 2 * CHUNK
KW = N_LEFT * CHUNK + QB
DIAG = KW + QB
TQ = N_LEFT * CHUNK
TM_MID = 256
TM_MM = 512
TM_BLK = 1024
SMALL_ROWS, SMALL_COLS = 16, 1024


def _params(*sem):
    return pltpu.CompilerParams(dimension_semantics=sem, vmem_limit_bytes=VMEM_LIMIT)


def _nt(a, b):
    return lax.dot_general(a, b, (((1,), (1,)), ((), ())), preferred_element_type=F32)


def _tn(a, b):
    return lax.dot_general(a, b, (((0,), (0,)), ((), ())), preferred_element_type=F32)


def _nn(a, b):
    return jnp.dot(a, b, preferred_element_type=F32)


def _diag_rel_index():
    d = np.arange(DIAG)
    diff = np.where(d < KW, d, d - DIAG)
    rel = N_LEFT * CHUNK - diff
    return np.clip(rel, -MAX_REL, MAX_REL) + MAX_REL


def _build_bias(diag_ref, bias_scr):
    r = lax.broadcasted_iota(jnp.int32, (QB, KW), 0) // CHUNK
    s = lax.broadcasted_iota(jnp.int32, (QB, KW), 1) // CHUNK
    allowed = (s >= r) & (s <= r + N_LEFT)
    for h in range(HEADS):
        row = jnp.broadcast_to(diag_ref[h:h + 1, :], (QB, DIAG))
        t = pltpu.roll(row, 0, 1, stride=1, stride_axis=0)
        bias_scr[h // 2, (h % 2) * QB:(h % 2 + 1) * QB, :] = jnp.where(allowed, t[:, :KW], NEG_BIG)


def _stack_heads(a, lane_hi):
    zero = jnp.zeros_like(a)
    return jnp.concatenate([jnp.where(lane_hi, zero, a), jnp.where(lane_hi, a, zero)], axis=0)


def _in_proj_gather(x, g, shard, order):
    s, d = x.shape
    tn = shard.shape[1]
    n = s // TM_BLK
    hr = d // 2

    def body(order_ref, x_ref, g_ref, shard32_ref, proj_ref, h_ref, wfull_ref, shard_ref, hbuf, wbuf,
             send1, recv1, send2, recv2, lsem):
        del order_ref
        j, i = pl.program_id(0), pl.program_id(1)
        x, y, c = _position()
        mine = 2 * x + y
        chips = [(x, 1 - y), (1 - x, y), (1 - x, 1 - y)]

        def theirs(k):
            return 2 * chips[k][0] + chips[k][1]

        def half_rows(half):
            return pl.ds(half * hr, hr)

        def ici(k, shard_index):
            return pltpu.make_async_remote_copy(
                src_ref=shard_ref.at[half_rows(c), :], dst_ref=wfull_ref.at[shard_index, half_rows(c), :],
                send_sem=send1.at[k], recv_sem=recv1.at[k], device_id=(*chips[k], c), device_id_type=MESH)

        def d2d(k, half):
            return pltpu.make_async_remote_copy(
                src_ref=wbuf.at[k % 2, half_rows(half), :], dst_ref=wfull_ref.at[theirs(k), half_rows(half), :],
                send_sem=send2.at[k], recv_sem=recv2.at[k], device_id=(x, y, 1 - c), device_id_type=MESH)

        def load(k, half, sem):
            return pltpu.make_async_copy(wfull_ref.at[theirs(k), half_rows(half), :],
                                         wbuf.at[k % 2, half_rows(half), :], lsem.at[sem])

        own = pltpu.make_async_copy(shard_ref, wfull_ref.at[mine], lsem.at[0])

        def put_proj(block):
            for grp in range(tn // D_ATT):
                proj_ref[grp] = block[:, grp * D_ATT:(grp + 1) * D_ATT].astype(BF16)

        @pl.when((j == 0) & (i == 0))
        def _():
            shard_ref[...] = shard32_ref[...].astype(BF16)
            own.start()
            ici(0, mine).start()
            ici(1, mine).start()

        for k in range(3):
            first = max(n - 3, 0) if k == 0 else min(n // 2, n - 1)

            @pl.when((j == k) & (i == first))
            def _(k=k):
                if k == 0:
                    ici(0, mine).wait_send()
                    ici(1, mine).wait_send()
                    ici(2, mine).start()
                if k == 2:
                    d2d(0, c).wait_send()
                ici(k, theirs(k)).wait_recv()
                load(k, c, 1).start()

            @pl.when((j == k) & (i == min(first + 1, n - 1)))
            def _(k=k):
                load(k, c, 1).wait()
                d2d(k, c).start()

            @pl.when((j == k) & (i == min(first + 2, n - 1)))
            def _(k=k):
                d2d(k, 1 - c).wait_recv()
                load(k, 1 - c, 2).start()

            @pl.when((j == k + 1) & (i == 0))
            def _(k=k):
                load(k, 1 - c, 2).wait()

        @pl.when(j == 0)
        def _():
            xv = x_ref[...]
            r = lax.rsqrt(jnp.mean(xv * xv, axis=-1, keepdims=True) + EPS)
            hv = ((xv * r) * g_ref[...]).astype(BF16)
            hbuf[i] = hv
            h_ref[...] = hv
            put_proj(_nn(hv, shard_ref[...]))

        for k in range(3):
            @pl.when(j == k + 1)
            def _(k=k):
                put_proj(_nn(hbuf[i], wbuf[k % 2]))

        @pl.when((j == 3) & (i == n - 1))
        def _():
            ici(2, mine).wait_send()
            d2d(1, c).wait_send()
            d2d(2, c).wait_send()
            own.wait()

    return pl.pallas_call(
        body, name="in_proj_gather",
        grid_spec=pltpu.PrefetchScalarGridSpec(
            num_scalar_prefetch=1, grid=(N_CHIPS, n),
            in_specs=[pl.BlockSpec((TM_BLK, d), lambda j, i, order: (jnp.where(j == 0, i, n - 1), 0)),
                      pl.BlockSpec((1, d), lambda j, i, order: (0, 0)), pl.BlockSpec(memory_space=pltpu.VMEM)],
            out_specs=[pl.BlockSpec((tn // D_ATT, TM_BLK, D_ATT), lambda j, i, order: (order[j], i, 0)),
                       pl.BlockSpec((TM_BLK, d), lambda j, i, order: (jnp.where(j == 0, i, n - 1), 0)), ANY],
            scratch_shapes=[pltpu.VMEM((d, tn), BF16), pltpu.VMEM((n, TM_BLK, d), BF16), pltpu.VMEM((2, d, tn), BF16),
                            pltpu.SemaphoreType.DMA((3,)), pltpu.SemaphoreType.DMA((3,)),
                            pltpu.SemaphoreType.DMA((3,)), pltpu.SemaphoreType.DMA((3,)), pltpu.SemaphoreType.DMA((3,))]),
        out_shape=[jax.ShapeDtypeStruct((N_CHIPS * tn // D_ATT, s, D_ATT), BF16), jax.ShapeDtypeStruct((s, d), BF16),
                   jax.ShapeDtypeStruct((N_CHIPS, d, tn), BF16)],
        compiler_params=_params("arbitrary", "arbitrary"),
    )(order, x, g, shard)


def _attn_fwd(diag, proj, shards, kinds, cw8):
    s = proj.shape[1]
    n = s // TQ
    nsrc = len(shards)
    nw = 2 * nsrc
    kinds = list(kinds) + ["rows" if k == "cols" else "cols" for k in kinds]
    shapes = [a.shape for a in shards] + [a.shape[::-1] for a in shards]
    scale = HEAD_DIM ** -0.5

    def body(*refs):
        diag_ref, q_ref, kp_ref, kc_ref, vp_ref, vc_ref = refs[:6]
        srcs, cw = refs[6:6 + nsrc], refs[6 + nsrc]
        o_ref, lse_ref = refs[7 + nsrc:9 + nsrc]
        dsts, cw_all = refs[9 + nsrc:9 + nsrc + nw], refs[9 + nsrc + nw]
        bias_scr = refs[10 + nsrc + nw]
        casts = refs[11 + nsrc + nw:11 + nsrc + 2 * nw]
        start, forward, finish = _gather_plan(kinds, casts, dsts, cw, cw_all, *refs[11 + nsrc + 2 * nw:])
        i = pl.program_id(0)

        @pl.when(i == 0)
        def _():
            for w in range(nsrc):
                casts[w][...] = srcs[w][...].astype(BF16)
                casts[nsrc + w][...] = srcs[w][...].T.astype(BF16)
            start()
            _build_bias(diag_ref, bias_scr)

        @pl.when(i == n // 2)
        def _():
            forward()

        @pl.when(i == n - 1)
        def _():
            finish()

        lane_hi = lax.broadcasted_iota(jnp.int32, (QB, LANES), 1) >= HEAD_DIM

        def block(b, first_tile):
            r0, n_prev = b * QB, TQ - b * QB
            n_cur = KW - n_prev
            pairs = range(HEADS // 2)
            lanes_of = [slice(LANES * p, LANES * (p + 1)) for p in pairs]
            scores = []
            for p in pairs:
                lanes = lanes_of[p]
                q2 = _stack_heads(q_ref[r0:r0 + QB, lanes] * scale, lane_hi)
                s_cur = _nt(q2, kc_ref[0:n_cur, lanes]) + bias_scr[p, :, n_prev:KW]
                if first_tile:
                    scores.append(s_cur)
                else:
                    scores.append(jnp.concatenate(
                        [_nt(q2, kp_ref[r0:TQ, lanes]) + bias_scr[p, :, 0:n_prev], s_cur], axis=1))
            probs = []
            for p in pairs:
                sc = scores[p]
                m = jnp.max(sc, axis=1, keepdims=True)
                pe = jnp.exp(sc - m)
                l = jnp.sum(pe, axis=1, keepdims=True)
                probs.append((pe.astype(BF16), l, m))
            for p in pairs:
                lanes = lanes_of[p]
                pb, l, m = probs[p]
                if first_tile:
                    o2 = _nn(pb, vc_ref[0:n_cur, lanes]) / l
                else:
                    o2 = (_nn(pb[:, 0:n_prev], vp_ref[r0:TQ, lanes]) + _nn(pb[:, n_prev:KW], vc_ref[0:n_cur, lanes])) / l
                lse2 = m + jnp.log(l)
                lse_ref[r0:r0 + QB, 2 * p:2 * p + 1] = lse2[0:QB, :]
                lse_ref[r0:r0 + QB, 2 * p + 1:2 * p + 2] = lse2[QB:2 * QB, :]
                o_ref[r0:r0 + QB, lanes] = jnp.where(lane_hi, o2[QB:2 * QB, :], o2[0:QB, :]).astype(BF16)

        @pl.when(i == 0)
        def _():
            for b in range(TQ // QB):
                block(b, True)

        @pl.when(i > 0)
        def _():
            for b in range(TQ // QB):
                block(b, False)

    blk = lambda grp, prev: pl.BlockSpec(
        (None, TQ, D_ATT), (lambda i: (grp, jnp.maximum(i - 1, 0), 0)) if prev else (lambda i: (grp, i, 0)))
    vmem = pl.BlockSpec(memory_space=pltpu.VMEM)
    return pl.pallas_call(
        body, name="attn_fwd", grid=(n,),
        in_specs=[pl.BlockSpec((HEADS, DIAG), lambda i: (0, 0)),
                  blk(0, False), blk(1, True), blk(1, False), blk(2, True), blk(2, False)] + [vmem] * (nsrc + 1),
        out_specs=[pl.BlockSpec((TQ, D_ATT), lambda i: (i, 0)), pl.BlockSpec((TQ, HEADS), lambda i: (i, 0))]
        + [ANY] * (nw + 1),
        out_shape=[jax.ShapeDtypeStruct((s, D_ATT), BF16), jax.ShapeDtypeStruct((s, HEADS), F32)]
        + _gather_out_shapes([jax.ShapeDtypeStruct(sh, BF16) for sh in shapes], kinds, cw8),
        scratch_shapes=[pltpu.VMEM((HEADS // 2, 2 * QB, KW), F32)] + [pltpu.VMEM(sh, BF16) for sh in shapes]
        + _gather_sems(nw),
        compiler_params=_params("arbitrary"),
    )(diag, proj, proj, proj, proj, proj, *shards, cw8)


def _attn_bwd(diag, proj, d_att, att, lse, parts):
    s = proj.shape[1]
    n = s // TQ
    npart = len(parts)
    scale = HEAD_DIM ** -0.5
    rel_pad = 3 * LANES

    def body(*refs):
        diag_ref, q_ref, kp_ref, kc_ref, vp_ref, vc_ref, do_ref, o_ref, lse_ref = refs[:9]
        part_refs = refs[9:9 + npart]
        dqkv_ref, dbias_ref = refs[9 + npart:11 + npart]
        slot_refs = refs[11 + npart:11 + 2 * npart]
        bias_scr, dbias_acc, dk_acc, dv_acc, dq_scr = refs[11 + 2 * npart:16 + 2 * npart]
        start, finish = _scatter_plan(part_refs, slot_refs, *refs[16 + 2 * npart:])
        i = pl.program_id(0)
        cur, prv = i % 2, 1 - i % 2

        @pl.when(i == 0)
        def _():
            start()
            _build_bias(diag_ref, bias_scr)
            dbias_acc[...] = jnp.zeros_like(dbias_acc)
            dk_acc[...] = jnp.zeros_like(dk_acc)
            dv_acc[...] = jnp.zeros_like(dv_acc)

        @pl.when(i > 0)
        def _():
            dqkv_ref[:, 0:D_ATT] = dq_scr[...]
            dk_acc[cur] = jnp.zeros((TQ, D_ATT), F32)
            dv_acc[cur] = jnp.zeros((TQ, D_ATT), F32)

        lane_hi = lax.broadcasted_iota(jnp.int32, (QB, LANES), 1) >= HEAD_DIM
        col = lax.broadcasted_iota(jnp.int32, (2 * QB, KW), 1)

        def make_block(first_tile):
            def block(b):
                r0, n_prev = b * QB, TQ - b * QB
                n_cur = KW - n_prev
                for p in range(HEADS // 2):
                    lanes = slice(LANES * p, LANES * (p + 1))
                    q2 = _stack_heads(q_ref[r0:r0 + QB, lanes] * scale, lane_hi)
                    kw = jnp.concatenate([kp_ref[r0:TQ, lanes], kc_ref[0:n_cur, lanes]], axis=0)
                    vw = jnp.concatenate([vp_ref[r0:TQ, lanes], vc_ref[0:n_cur, lanes]], axis=0)
                    dop = do_ref[r0:r0 + QB, lanes]
                    do2 = _stack_heads(dop, lane_hi)
                    prod = dop.astype(F32) * o_ref[r0:r0 + QB, lanes].astype(F32)
                    delta2 = jnp.concatenate(
                        [jnp.sum(jnp.where(lane_hi, 0.0, prod), axis=1, keepdims=True),
                         jnp.sum(jnp.where(lane_hi, prod, 0.0), axis=1, keepdims=True)], axis=0)
                    lse2 = jnp.concatenate([lse_ref[r0:r0 + QB, 2 * p:2 * p + 1],
                                            lse_ref[r0:r0 + QB, 2 * p + 1:2 * p + 2]], axis=0)
                    sc = _nt(q2, kw) + bias_scr[p]
                    if first_tile:
                        sc = jnp.where(col >= TQ - r0, sc, NEG_BIG)
                    pr = jnp.exp(sc - lse2)
                    ds = pr * (_nt(do2, vw) - delta2)
                    dbias_acc[p] += ds
                    dsb = ds.astype(BF16)
                    dv_w = _tn(pr.astype(BF16), do2)
                    dk_w = _tn(dsb, q2)
                    dv_acc[prv, r0:TQ, lanes] += dv_w[0:n_prev, :]
                    dv_acc[cur, 0:n_cur, lanes] += dv_w[n_prev:KW, :]
                    dk_acc[prv, r0:TQ, lanes] += dk_w[0:n_prev, :]
                    dk_acc[cur, 0:n_cur, lanes] += dk_w[n_prev:KW, :]
                    dq2 = _nn(dsb, kw)
                    dq = jnp.where(lane_hi, dq2[QB:2 * QB, :], dq2[0:QB, :]) * scale
                    dq_scr[r0:r0 + QB, lanes] = dq.astype(BF16)
            return block

        @pl.when(i == 0)
        def _():
            for b in range(TQ // QB):
                make_block(True)(b)

        @pl.when((i > 0) & (i < n))
        def _():
            for b in range(TQ // QB):
                make_block(False)(b)

        @pl.when(i > 0)
        def _():
            dqkv_ref[:, D_ATT:2 * D_ATT] = dk_acc[prv].astype(BF16)
            dqkv_ref[:, 2 * D_ATT:3 * D_ATT] = dv_acc[prv].astype(BF16)

        @pl.when(i == n)
        def _():
            d_iota = lax.broadcasted_iota(jnp.int32, (DIAG, rel_pad), 0)
            n_iota = lax.broadcasted_iota(jnp.int32, (DIAG, rel_pad), 1)
            diff = jnp.where(d_iota < KW, d_iota, d_iota - DIAG)
            idx = jnp.clip(N_LEFT * CHUNK - diff, -MAX_REL, MAX_REL) + MAX_REL
            onehot = (idx == n_iota).astype(F32)
            rows = []
            for hd in range(HEADS):
                acc = dbias_acc[hd // 2, (hd % 2) * QB:(hd % 2 + 1) * QB, :]
                a = jnp.concatenate([acc, jnp.zeros((QB, DIAG - KW), F32)], axis=1)
                g8 = a[0:SUBLANES, :]
                for blk in range(1, QB // SUBLANES):
                    g8 = g8 + pltpu.roll(a[blk * SUBLANES:(blk + 1) * SUBLANES, :], DIAG - blk * SUBLANES, 1)
                g1 = g8[0:1, :]
                for r in range(1, SUBLANES):
                    g1 = g1 + pltpu.roll(g8[r:r + 1, :], DIAG - r, 1)
                rows.append(g1)
            g = jnp.concatenate(rows, axis=0)
            dbias_ref[...] = jnp.dot(g, onehot, preferred_element_type=F32, precision=lax.Precision.HIGHEST)
            finish()

    last = n - 1
    cur = lambda grp: pl.BlockSpec((None, TQ, D_ATT), lambda i: (grp, jnp.minimum(i, last), 0))
    prev = lambda grp: pl.BlockSpec((None, TQ, D_ATT), lambda i: (grp, jnp.maximum(jnp.minimum(i, last) - 1, 0), 0))
    tile = pl.BlockSpec((TQ, D_ATT), lambda i: (jnp.minimum(i, last), 0))
    return pl.pallas_call(
        body, name="attn_bwd", grid=(n + 1,),
        in_specs=[pl.BlockSpec((HEADS, DIAG), lambda i: (0, 0)),
                  cur(0), prev(1), cur(1), prev(2), cur(2), tile, tile,
                  pl.BlockSpec((TQ, HEADS), lambda i: (jnp.minimum(i, last), 0))] + [ANY] * npart,
        out_specs=[pl.BlockSpec((TQ, 3 * D_ATT), lambda i: (jnp.maximum(i - 1, 0), 0)),
                   pl.BlockSpec((HEADS, rel_pad), lambda i: (0, 0))] + [ANY] * npart,
        out_shape=[jax.ShapeDtypeStruct((s, 3 * D_ATT), BF16), jax.ShapeDtypeStruct((HEADS, rel_pad), F32)]
        + _scatter_out_shapes(parts),
        scratch_shapes=[pltpu.VMEM((HEADS // 2, 2 * QB, KW), F32), pltpu.VMEM((HEADS // 2, 2 * QB, KW), F32),
                        pltpu.VMEM((2, TQ, D_ATT), F32), pltpu.VMEM((2, TQ, D_ATT), F32),
                        pltpu.VMEM((TQ, D_ATT), BF16)] + _scatter_sems(npart),
        compiler_params=_params("arbitrary"),
    )(diag, proj, proj, proj, proj, proj, d_att, att, lse, *parts)


def _shift_down(a, k, halo):
    rolled = pltpu.roll(a, k, 0)
    row = lax.broadcasted_iota(jnp.int32, halo.shape, 0)
    first = jnp.where(row < k, pltpu.roll(halo, k, 0), rolled[0:SUBLANES, :])
    return jnp.concatenate([first, rolled[SUBLANES:, :]], axis=0)


def _shift_up(a, k, nxt):
    tm = a.shape[0]
    rolled = pltpu.roll(a, tm - k, 0)
    row = lax.broadcasted_iota(jnp.int32, nxt.shape, 0)
    last = jnp.where(row >= SUBLANES - k, pltpu.roll(nxt, SUBLANES - k, 0), rolled[tm - SUBLANES:, :])
    return jnp.concatenate([rolled[:tm - SUBLANES, :], last], axis=0)


def _sigmoid(v):
    return 0.5 * jnp.tanh(0.5 * v) + 0.5


def _mixer_mid(att, proj, x, tgt, w_att, w_conv, w_out, w_att_t, w_conv_t, w_out_t, conv_w8, conv_b, fin_g):
    s, d = x.shape
    dc = D_ATT
    n = s // TM_MID
    tm = TM_MID
    n_shards = 4

    def body(att_ref, za_ref, gb_ref, gc_ref, u_ref, zc_ref, hgc_ref, hu_ref, gatt_ref, gconv_ref, x_ref, t_ref,
             watt_ref, wconv_ref, wout_ref, watt_t_ref, wconv_t_ref, wout_t_ref, cw_ref, cb_ref, fg_ref,
             dpb_ref, do_ref, dx2_ref, gatt_o, gconv_o, gout_o, loss_o, gfn_o, gcb_o, gcw_o,
             acc_att, acc_conv, acc_out, carry):
        i = pl.program_id(0)
        tile = n - 1 - i

        @pl.when(i == 0)
        def _():
            acc_att[...] = jnp.zeros_like(acc_att)
            acc_conv[...] = jnp.zeros_like(acc_conv)
            acc_out[...] = jnp.zeros_like(acc_out)
            carry[...] = jnp.zeros_like(carry)
            loss_o[...] = jnp.zeros_like(loss_o)
            gfn_o[...] = jnp.zeros_like(gfn_o)
            gcb_o[...] = jnp.zeros_like(gcb_o)
            gcw_o[...] = jnp.zeros_like(gcw_o)

        halves = [slice(hh * (tm // 2), (hh + 1) * (tm // 2)) for hh in range(2)]
        both = lambda fn: [fn(rows) for rows in halves]
        f32 = lambda ref, rows: ref[rows, :].astype(F32)

        gc = gc_ref[...].astype(F32)
        u = u_ref[...].astype(F32)
        cu = gc * u
        halo = jnp.where(tile > 0, hgc_ref[...].astype(F32) * hu_ref[...].astype(F32), 0.0)
        cu1 = _shift_down(cu, 1, halo)
        cu2 = _shift_down(cu, 2, halo)
        w0, w1, w2 = cw_ref[0:1, :], cw_ref[1:2, :], cw_ref[2:3, :]
        fg = fg_ref[...]

        def stage_a(rows):
            att_v, za, zc, gb = f32(att_ref, rows), f32(za_ref, rows), f32(zc_ref, rows), f32(gb_ref, rows)
            sa = _sigmoid(za)
            silu_a = za * sa
            vconv = w0 * cu2[rows, :] + w1 * cu1[rows, :] + w2 * cu[rows, :] + cb_ref[...]
            sc = _sigmoid(zc)
            silu_c = zc * sc
            return dict(att_v=att_v, za=za, zc=zc, gb=gb, sa=sa, silu_a=silu_a, vconv=vconv, sc=sc, silu_c=silu_c,
                        a_b=(att_v * silu_a).astype(BF16), c_b=(gb * vconv * silu_c).astype(BF16))

        st = both(stage_a)
        for t in st:
            t["y_att"] = _nn(t["a_b"], watt_ref[...])
            t["y_conv"] = _nn(t["c_b"], wconv_ref[...])
        for t, rows in zip(st, halves):
            gpair = lambda ref: jnp.concatenate([ref[0, rows, :], ref[1, rows, :]], axis=1).astype(F32)
            t["ga"] = _sigmoid(gpair(gatt_ref))
            t["gv"] = _sigmoid(gpair(gconv_ref))
            t["m_b"] = (t["ga"] * t["y_att"] + t["gv"] * t["y_conv"]).astype(BF16)
        for t in st:
            t["mo"] = _nn(t["m_b"], wout_ref[...])
        for t, rows in zip(st, halves):
            x2 = x_ref[rows, :] + t["mo"]
            r2 = lax.rsqrt(jnp.mean(x2 * x2, axis=-1, keepdims=True) + EPS)
            x2n = x2 * r2
            err = x2n * fg - t_ref[rows, :]
            loss_o[...] += jnp.sum(err * err, axis=0, keepdims=True) * (0.5 / d)
            dy = err * (1.0 / d)
            gfn_o[...] += jnp.sum(dy * x2n, axis=0, keepdims=True)
            dyn = dy * fg
            dx2 = r2 * (dyn - x2n * jnp.mean(dyn * x2n, axis=-1, keepdims=True))
            dx2_ref[rows, :] = dx2
            t["dx2_b"] = dx2.astype(BF16)
        for t in st:
            t["dm"] = _nn(t["dx2_b"], wout_t_ref[...])
        whole = lambda key: jnp.concatenate([st[0][key], st[1][key]], axis=0)
        acc_out[...] += _tn(whole("m_b"), whole("dx2_b"))
        for t, rows in zip(st, halves):
            dy_att = t["dm"] * t["ga"]
            dy_conv = t["dm"] * t["gv"]
            dpb_ref[rows, 5 * dc:5 * dc + d] = (dy_att * t["y_att"] * (1.0 - t["ga"])).astype(BF16)
            dpb_ref[rows, 5 * dc + d:5 * dc + 2 * d] = (dy_conv * t["y_conv"] * (1.0 - t["gv"])).astype(BF16)
            t["dya_b"] = dy_att.astype(BF16)
            t["dyc_b"] = dy_conv.astype(BF16)
        for t in st:
            t["da_in"] = _nn(t["dya_b"], watt_t_ref[...])
            t["dc_in"] = _nn(t["dyc_b"], wconv_t_ref[...])
        acc_att[...] += _tn(whole("a_b"), whole("dya_b"))
        acc_conv[...] += _tn(whole("c_b"), whole("dyc_b"))
        for t, rows in zip(st, halves):
            sa, za, sc, zc = t["sa"], t["za"], t["sc"], t["zc"]
            do_ref[rows, :] = (t["da_in"] * t["silu_a"]).astype(BF16)
            dpb_ref[rows, 0:dc] = (t["da_in"] * t["att_v"] * (sa * (1.0 + za * (1.0 - sa)))).astype(BF16)
            dpb_ref[rows, dc:2 * dc] = (t["dc_in"] * t["vconv"] * t["silu_c"]).astype(BF16)
            dgs = t["dc_in"] * t["gb"]
            t["dvc"] = dgs * t["silu_c"]
            dpb_ref[rows, 4 * dc:5 * dc] = (dgs * t["vconv"] * (sc * (1.0 + zc * (1.0 - sc)))).astype(BF16)
        dvc = whole("dvc")
        gcb_o[...] += jnp.sum(dvc, axis=0, keepdims=True)
        gcw_o[0:1, :] += jnp.sum(dvc * cu2, axis=0, keepdims=True)
        gcw_o[1:2, :] += jnp.sum(dvc * cu1, axis=0, keepdims=True)
        gcw_o[2:3, :] += jnp.sum(dvc * cu, axis=0, keepdims=True)
        nxt = carry[...]
        dcu = w2 * dvc + w1 * _shift_up(dvc, 1, nxt) + w0 * _shift_up(dvc, 2, nxt)
        carry[...] = dvc[0:SUBLANES, :]
        dpb_ref[:, 2 * dc:3 * dc] = (dcu * u).astype(BF16)
        dpb_ref[:, 3 * dc:4 * dc] = (dcu * gc).astype(BF16)

        @pl.when(i == n - 1)
        def _():
            for j in range(n_shards):
                gatt_o[j] = acc_att[:, j * (d // n_shards):(j + 1) * (d // n_shards)].astype(BF16)
                gconv_o[j] = acc_conv[:, j * (d // n_shards):(j + 1) * (d // n_shards)].astype(BF16)
                gout_o[j] = acc_out[j * (d // n_shards):(j + 1) * (d // n_shards), :].astype(BF16)

    rev = lambda width, col_blk: pl.BlockSpec((tm, width), lambda i: (n - 1 - i, col_blk))
    grp = lambda g: pl.BlockSpec((None, tm, dc), lambda i: (g, n - 1 - i, 0))
    grp2 = lambda g2: pl.BlockSpec((2, tm, dc), lambda i: (g2, n - 1 - i, 0))
    halo_spec = lambda g: pl.BlockSpec(
        (None, SUBLANES, dc), lambda i: (g, jnp.maximum((n - 1 - i) * (tm // SUBLANES) - 1, 0), 0))
    const = lambda shape: pl.BlockSpec(shape, lambda i: tuple(0 for _ in shape), pipeline_mode=pl.Buffered(1))
    q4 = d // n_shards
    return pl.pallas_call(
        body, name="mixer_mid", grid=(n,),
        in_specs=[rev(dc, 0), grp(3), grp(4), grp(5), grp(6), grp(7),
                  halo_spec(5), halo_spec(6), grp2(4), grp2(5), rev(d, 0), rev(d, 0),
                  const((dc, d)), const((dc, d)), const((d, d)), const((d, dc)), const((d, dc)), const((d, d)),
                  const((SUBLANES, dc)), const((1, dc)), const((1, d))],
        out_specs=[rev(5 * dc + 2 * d, 0), rev(dc, 0), rev(d, 0),
                   const((n_shards, dc, q4)), const((n_shards, dc, q4)), const((n_shards, q4, d)),
                   const((1, d)), const((1, d)), const((1, dc)), const((SUBLANES, dc))],
        out_shape=[jax.ShapeDtypeStruct((s, 5 * dc + 2 * d), BF16), jax.ShapeDtypeStruct((s, dc), BF16),
                   jax.ShapeDtypeStruct((s, d), F32),
                   jax.ShapeDtypeStruct((n_shards, dc, q4), BF16), jax.ShapeDtypeStruct((n_shards, dc, q4), BF16),
                   jax.ShapeDtypeStruct((n_shards, q4, d), BF16),
                   jax.ShapeDtypeStruct((1, d), F32), jax.ShapeDtypeStruct((1, d), F32),
                   jax.ShapeDtypeStruct((1, dc), F32), jax.ShapeDtypeStruct((SUBLANES, dc), F32)],
        scratch_shapes=[pltpu.VMEM((dc, d), F32), pltpu.VMEM((dc, d), F32), pltpu.VMEM((d, d), F32),
                        pltpu.VMEM((SUBLANES, dc), F32)],
        compiler_params=_params("arbitrary"),
    )(att, proj, proj, proj, proj, proj, proj, proj, proj, proj, x, tgt,
      w_att, w_conv, w_out, w_att_t, w_conv_t, w_out_t, conv_w8, conv_b, fin_g)


def _in_proj_bwd_x(dqkv, dpb, w_in, x, dx2, g):
    s, d = x.shape
    tn = dqkv.shape[1]
    nb = dpb.shape[1] // tn
    n = s // TM_MM

    def body(*refs):
        dps, ws = refs[:nb + 1], refs[nb + 1:2 * nb + 2]
        x_ref, dx2_ref, g_ref, gx_ref, gng_ref = refs[2 * nb + 2:]
        i = pl.program_id(0)

        @pl.when(i == 0)
        def _():
            gng_ref[...] = jnp.zeros_like(gng_ref)

        dh = _nt(dps[0][...], ws[0][0])
        for j in range(1, nb + 1):
            dh = dh + _nt(dps[j][...], ws[j][0])
        xv = x_ref[...]
        r = lax.rsqrt(jnp.mean(xv * xv, axis=-1, keepdims=True) + EPS)
        xn = xv * r
        gng_ref[...] += jnp.sum(dh * xn, axis=0, keepdims=True)
        dhn = dh * g_ref[...]
        gx_ref[...] = dx2_ref[...] + r * (dhn - xn * jnp.mean(dhn * xn, axis=-1, keepdims=True))

    tile = lambda width, col_blk: pl.BlockSpec((TM_MM, width), lambda i: (i, col_blk))
    wspec = lambda blk: pl.BlockSpec((1, d, tn), lambda i: (blk, 0, 0), pipeline_mode=pl.Buffered(1))
    return pl.pallas_call(
        body, name="in_proj_bwd_x", grid=(n,),
        in_specs=[tile(tn, 0)] + [tile(tn, j) for j in range(nb)] + [wspec(j) for j in range(nb + 1)]
        + [tile(d, 0), tile(d, 0), pl.BlockSpec((1, d), lambda i: (0, 0))],
        out_specs=[tile(d, 0), pl.BlockSpec((1, d), lambda i: (0, 0))],
        out_shape=[jax.ShapeDtypeStruct((s, d), F32), jax.ShapeDtypeStruct((1, d), F32)],
        compiler_params=_params("arbitrary"),
    )(dqkv, *([dpb] * nb), *([w_in] * (nb + 1)), x, dx2, g)


def _in_proj_bwd_w(h, dqkv, dpb, order):
    s, d = h.shape
    tn = dqkv.shape[1]
    n = s // TM_BLK
    hr = d // 2
    settle = min(2, n - 1)

    def body(order_ref, h_ref, da_ref, db_ref, slots_ref, acc, sendbuf, pairbuf, chipbuf, psend, precv, send, recv, lsem):
        j, i = pl.program_id(0), pl.program_id(1)
        blk = order_ref[j]
        pos = _position()
        x, y, c = pos

        @pl.when(i == 0)
        def _():
            acc[...] = jnp.zeros_like(acc)

        @pl.when(blk == 0)
        def _():
            acc[...] += _tn(h_ref[...], da_ref[...])

        @pl.when(blk > 0)
        def _():
            acc[...] += _tn(h_ref[...], db_ref[...])

        def pair(step, half):
            return pltpu.make_async_remote_copy(
                src_ref=sendbuf.at[step, pl.ds(half * hr, hr), :], dst_ref=pairbuf.at[step],
                send_sem=psend.at[step], recv_sem=precv.at[step], device_id=(x, y, 1 - c), device_id_type=MESH)

        def ici(step):
            flip = OWNER_FLIPS[step]
            return pltpu.make_async_remote_copy(
                src_ref=chipbuf.at[step], dst_ref=slots_ref.at[flip], send_sem=send.at[step], recv_sem=recv.at[step],
                device_id=_peer(pos, 4 * (flip >> 1) + 2 * (flip & 1)), device_id_type=MESH)

        local = pltpu.make_async_copy(chipbuf.at[N_CHIPS - 1], slots_ref.at[0], lsem.at[0])

        def combine(step):
            pair(step, c).wait_recv()
            mine = sendbuf[step, pl.ds(c * hr, hr), :].astype(F32)
            chipbuf[step] = (mine + pairbuf[step].astype(F32)).astype(BF16)

        for step in range(N_CHIPS):
            @pl.when((j == step) & (i == n - 1))
            def _(step=step):
                sendbuf[step] = acc[...].astype(BF16)
                pair(step, 1 - c).start()

        for step in range(N_CHIPS - 1):
            @pl.when((j == step + 1) & (i == settle))
            def _(step=step):
                combine(step)
                ici(step).start()

        @pl.when((j == N_CHIPS - 1) & (i == n - 1))
        def _():
            combine(N_CHIPS - 1)
            local.start()
            for step in range(N_CHIPS - 1):
                ici(step).wait_recv()
            for step in range(N_CHIPS - 1):
                ici(step).wait_send()
            for step in range(N_CHIPS):
                pair(step, 1 - c).wait_send()
            local.wait()

    return pl.pallas_call(
        body, name="in_proj_bwd_w",
        grid_spec=pltpu.PrefetchScalarGridSpec(
            num_scalar_prefetch=1, grid=(N_CHIPS, n),
            in_specs=[pl.BlockSpec((TM_BLK, d), lambda j, i, order: (i, 0)),
                      pl.BlockSpec((TM_BLK, tn), lambda j, i, order: (jnp.where(order[j] == 0, i, 0), 0)),
                      pl.BlockSpec((TM_BLK, tn), lambda j, i, order: (jnp.where(order[j] == 0, 0, i),
                                                                     jnp.maximum(order[j] - 1, 0)))],
            out_specs=[ANY],
            scratch_shapes=[pltpu.VMEM((d, tn), F32), pltpu.VMEM((N_CHIPS, d, tn), BF16),
                            pltpu.VMEM((N_CHIPS, hr, tn), BF16), pltpu.VMEM((N_CHIPS, hr, tn), BF16),
                            pltpu.SemaphoreType.DMA((N_CHIPS,)), pltpu.SemaphoreType.DMA((N_CHIPS,)),
                            pltpu.SemaphoreType.DMA((N_CHIPS - 1,)), pltpu.SemaphoreType.DMA((N_CHIPS - 1,)),
                            pltpu.SemaphoreType.DMA((1,))]),
        out_shape=[jax.ShapeDtypeStruct((N_CHIPS, hr, tn), BF16)],
        compiler_params=_params("arbitrary", "arbitrary"),
    )(order, h, dqkv, dpb)[0]


LOSS_ROW = 6


def _adam_update(w, g, m, v):
    c1 = 1.0 / (1.0 - ADAM_B1 ** ADAM_STEP)
    c2 = 1.0 / (1.0 - ADAM_B2 ** ADAM_STEP)
    m2 = ADAM_B1 * m + (1.0 - ADAM_B1) * g
    v2 = ADAM_B2 * v + (1.0 - ADAM_B2) * (g * g)
    return -ADAM_LR * ((m2 * c1) / (jnp.sqrt(v2 * c2) + ADAM_EPS) + ADAM_WD * w), m2, v2


def _adamw_small(recv, params, moments_m, moments_v):
    k = recv.shape[0]
    n_par = len(params)
    cshard = params[3].shape[1]

    def body(*refs):
        r_ref = refs[0]
        ws, ms, vs = refs[1:1 + n_par], refs[1 + n_par:1 + 2 * n_par], refs[1 + 2 * n_par:1 + 3 * n_par]
        loss_ref = refs[1 + 3 * n_par]
        outs = refs[2 + 3 * n_par:]
        total = r_ref[0]
        for slot in range(1, k):
            total = total + r_ref[slot]
        loss_ref[...] = jnp.sum(total[LOSS_ROW:LOSS_ROW + 1, :], axis=1, keepdims=True)
        chip = 2 * lax.axis_index("x") + lax.axis_index("y")
        g_cw = jnp.zeros((3, cshard), F32)
        for sh in range(N_CHIPS):
            g_cw = g_cw + jnp.where(chip == sh, total[3:6, sh * cshard:(sh + 1) * cshard], 0.0)
        grads = [total[0:1, :], total[1:2, :], total[2:3, :ws[2].shape[1]], g_cw, total[8:16, :ws[4].shape[1]]]
        for p in range(n_par):
            delta, m2, v2 = _adam_update(ws[p][...], grads[p], ms[p][...], vs[p][...])
            for q, val in enumerate((grads[p], delta, m2, v2)):
                outs[4 * p + q][...] = val

    shapes = [jax.ShapeDtypeStruct((1, 1), F32)]
    for p in params:
        shapes += [jax.ShapeDtypeStruct(p.shape, F32)] * 4
    return pl.pallas_call(body, name="adamw_small", out_shape=shapes)(recv, *params, *moments_m, *moments_v)


def _adamw(w, g, m, v, name, rows_per_step):
    r, c = w.shape

    def body(w_ref, g_ref, m_ref, v_ref, go_ref, d_ref, mo_ref, vo_ref):
        gv = g_ref[...]
        go_ref[...] = gv
        d_ref[...], mo_ref[...], vo_ref[...] = _adam_update(w_ref[...], gv, m_ref[...], v_ref[...])

    spec = pl.BlockSpec((rows_per_step, c), lambda i: (i, 0))
    shape = jax.ShapeDtypeStruct((r, c), F32)
    return pl.pallas_call(
        body, name=name, grid=(r // rows_per_step,),
        in_specs=[spec] * 4, out_specs=[spec] * 4, out_shape=[shape] * 4,
        compiler_params=_params("parallel"),
    )(w, g, m, v)


def _adamw_group(ws, gs, ms, vs, name):
    k = len(ws)

    def body(*refs):
        ins, outs = refs[:4 * k], refs[4 * k:]
        for j in range(k):
            gv = ins[k + j][...]
            outs[4 * j][...] = gv
            outs[4 * j + 1][...], outs[4 * j + 2][...], outs[4 * j + 3][...] = _adam_update(
                ins[j][...], gv, ins[2 * k + j][...], ins[3 * k + j][...])

    shapes = []
    for w in ws:
        shapes += [jax.ShapeDtypeStruct(w.shape, F32)] * 4
    out = pl.pallas_call(body, name=name, out_shape=shapes,
                         compiler_params=pltpu.CompilerParams(vmem_limit_bytes=VMEM_LIMIT))(*ws, *gs, *ms, *vs)
    return [tuple(out[4 * j:4 * j + 4]) for j in range(k)]


ANY = pl.BlockSpec(memory_space=pl.ANY)
N_CHIPS = 4
N_DEV = 8
OWNER_FLIPS = (3, 1, 2, 0)


def _position():
    return lax.axis_index("x"), lax.axis_index("y"), lax.axis_index("c")


def _gather_out_shapes(shards, kinds, cw8):
    full = [(a.shape[0], a.shape[1] * N_CHIPS) if k == "cols" else (a.shape[0] * N_CHIPS, a.shape[1])
            for a, k in zip(shards, kinds)]
    return [jax.ShapeDtypeStruct(f, BF16) for f in full] + [
        jax.ShapeDtypeStruct((N_CHIPS,) + cw8.shape, cw8.dtype)]


def _gather_sems(nw):
    return [pltpu.SemaphoreType.DMA((3, nw)), pltpu.SemaphoreType.DMA((3, nw)),
            pltpu.SemaphoreType.DMA((3, nw)), pltpu.SemaphoreType.DMA((3, nw)),
            pltpu.SemaphoreType.DMA((3,)), pltpu.SemaphoreType.DMA((3,)), pltpu.SemaphoreType.DMA((nw + 1,))]


def _gather_plan(kinds, srcs, dsts, cw, cw_all, send1, recv1, send2, recv2, ssend, srecv, lsem):
    nw = len(srcs)
    x, y, c = _position()
    mine = 2 * x + y
    chips = [(x, 1 - y), (1 - x, y), (1 - x, 1 - y)]

    def window(w, shard, half):
        r, cc = srcs[w].shape
        hr = r // 2
        if kinds[w] == "cols":
            rows = pl.ds(0, r) if half is None else pl.ds(half * hr, hr)
            return dsts[w].at[rows, pl.ds(shard * cc, cc)]
        rows = pl.ds(shard * r, r) if half is None else pl.ds(shard * r + half * hr, hr)
        return dsts[w].at[rows, :]

    def my_half(w):
        hr = srcs[w].shape[0] // 2
        return srcs[w].at[pl.ds(c * hr, hr), :]

    def local():
        return [pltpu.make_async_copy(srcs[w], window(w, mine, None), lsem.at[w]) for w in range(nw)] + [
            pltpu.make_async_copy(cw, cw_all.at[mine], lsem.at[nw])]

    def ici(k, w, shard):
        kx, ky = chips[k]
        return pltpu.make_async_remote_copy(
            src_ref=my_half(w), dst_ref=window(w, shard, c), send_sem=send1.at[k, w], recv_sem=recv1.at[k, w],
            device_id=(kx, ky, c), device_id_type=MESH)

    def d2d(k, w, shard, half):
        return pltpu.make_async_remote_copy(
            src_ref=window(w, shard, half), dst_ref=window(w, shard, half),
            send_sem=send2.at[k, w], recv_sem=recv2.at[k, w], device_id=(x, y, 1 - c), device_id_type=MESH)

    def small(k, shard):
        kx, ky = chips[k]
        return pltpu.make_async_remote_copy(
            src_ref=cw, dst_ref=cw_all.at[shard], send_sem=ssend.at[k], recv_sem=srecv.at[k],
            device_id=(kx, ky, c), device_id_type=MESH)

    def theirs(k):
        kx, ky = chips[k]
        return 2 * kx + ky

    def start():
        for cp in local():
            cp.start()
        for k in range(3):
            for w in range(nw):
                ici(k, w, mine).start()
            small(k, mine).start()

    def forward():
        for k in range(3):
            for w in range(nw):
                ici(k, w, theirs(k)).wait_recv()
                d2d(k, w, theirs(k), c).start()

    def finish():
        for k in range(3):
            for w in range(nw):
                d2d(k, w, theirs(k), 1 - c).wait_recv()
            small(k, theirs(k)).wait_recv()
        for k in range(3):
            for w in range(nw):
                ici(k, w, mine).wait_send()
                d2d(k, w, theirs(k), c).wait_send()
            small(k, mine).wait_send()
        for cp in local():
            cp.wait()

    return start, forward, finish


def _scatter_out_shapes(parts):
    return [jax.ShapeDtypeStruct((N_DEV, p.shape[1] // 2, p.shape[2]), p.dtype) for p in parts]


def _scatter_sems(nw):
    return [pltpu.SemaphoreType.DMA((N_DEV, nw)), pltpu.SemaphoreType.DMA((N_DEV, nw)), pltpu.SemaphoreType.DMA((nw,))]


def _peer(pos, k):
    x, y, c = pos
    return ((1 - x) if k & 4 else x, (1 - y) if k & 2 else y, (1 - c) if k & 1 else c)


def _scatter_plan(srcs, dsts, send, recv, lsem):
    nw = len(srcs)
    pos = _position()

    def piece(w, k):
        px, py, pc = _peer(pos, k)
        hr = srcs[w].shape[1] // 2
        return srcs[w].at[2 * px + py, pl.ds(pc * hr, hr), :]

    def remote(w, k):
        return pltpu.make_async_remote_copy(
            src_ref=piece(w, k), dst_ref=dsts[w].at[k], send_sem=send.at[k, w], recv_sem=recv.at[k, w],
            device_id=_peer(pos, k), device_id_type=MESH)

    def local(w):
        return pltpu.make_async_copy(piece(w, 0), dsts[w].at[0], lsem.at[w])

    def start():
        for w in range(nw):
            local(w).start()
        for k in range(1, N_DEV):
            for w in range(nw):
                remote(w, k).start()

    def finish():
        for k in range(1, N_DEV):
            for w in range(nw):
                remote(w, k).wait_recv()
        for k in range(1, N_DEV):
            for w in range(nw):
                remote(w, k).wait_send()
        for w in range(nw):
            local(w).wait()

    return start, finish


def _reduce_pair(slots, small):
    nw = len(slots)

    def body(*refs):
        srcs, sm = refs[:nw], refs[nw]
        dsts, sm_all = refs[nw + 1:2 * nw + 1], refs[2 * nw + 1]
        halves = refs[2 * nw + 2:3 * nw + 2]
        send, recv, ssend, srecv, lsem = refs[3 * nw + 2:]
        pos = _position()
        x, y, c = pos
        me = 4 * x + 2 * y + c

        def rows(w, half):
            hr = halves[w].shape[0]
            return dsts[w].at[pl.ds(half * hr, hr), :]

        def remote(w, half):
            return pltpu.make_async_remote_copy(
                src_ref=halves[w], dst_ref=rows(w, half), send_sem=send.at[w], recv_sem=recv.at[w],
                device_id=(x, y, 1 - c), device_id_type=MESH)

        def bcast(k, slot):
            return pltpu.make_async_remote_copy(
                src_ref=sm, dst_ref=sm_all.at[slot], send_sem=ssend.at[k], recv_sem=srecv.at[k],
                device_id=_peer(pos, k), device_id_type=MESH)

        small_copies = [bcast(k, me) for k in range(1, N_DEV)]
        own_small = pltpu.make_async_copy(sm, sm_all.at[me], lsem.at[nw])
        for cp in small_copies + [own_small]:
            cp.start()
        big = []
        for w in range(nw):
            total = srcs[w][0].astype(F32)
            for k in range(1, srcs[w].shape[0]):
                total = total + srcs[w][k].astype(F32)
            halves[w][...] = total
            big += [remote(w, c), pltpu.make_async_copy(halves[w], rows(w, c), lsem.at[w])]
            big[-2].start()
            big[-1].start()
        for w in range(nw):
            remote(w, 1 - c).wait_recv()
        for k in range(1, N_DEV):
            px, py, pc = _peer(pos, k)
            bcast(k, 4 * px + 2 * py + pc).wait_recv()
        for w in range(nw):
            big[2 * w].wait_send()
            big[2 * w + 1].wait()
        for cp in small_copies:
            cp.wait_send()
        own_small.wait()

    vmem = pl.BlockSpec(memory_space=pltpu.VMEM)
    half_shapes = [(sl.shape[1], sl.shape[2]) for sl in slots]
    return pl.pallas_call(
        body, name="reduce_pair",
        in_specs=[vmem] * (nw + 1), out_specs=[ANY] * (nw + 1),
        out_shape=[jax.ShapeDtypeStruct((2 * r, cc), F32) for r, cc in half_shapes]
        + [jax.ShapeDtypeStruct((N_DEV,) + small.shape, small.dtype)],
        scratch_shapes=[pltpu.VMEM(hs, F32) for hs in half_shapes]
        + [pltpu.SemaphoreType.DMA((nw,)), pltpu.SemaphoreType.DMA((nw,)),
           pltpu.SemaphoreType.DMA((N_DEV,)), pltpu.SemaphoreType.DMA((N_DEV,)),
           pltpu.SemaphoreType.DMA((nw + 1,))],
        compiler_params=pltpu.CompilerParams(vmem_limit_bytes=VMEM_LIMIT),
    )(*slots, small)


def _pad_to(a, rows, cols):
    return jnp.pad(a, ((0, rows - a.shape[0]), (0, cols - a.shape[1])))


def _pack_small(norm_g, fin_g, conv_b, conv_w, loss_vec, rel):
    rows = [_pad_to(norm_g, 1, SMALL_COLS), _pad_to(fin_g, 1, SMALL_COLS), _pad_to(conv_b, 1, SMALL_COLS),
            _pad_to(conv_w, 3, SMALL_COLS), _pad_to(loss_vec, 2, SMALL_COLS), _pad_to(rel, HEADS, SMALL_COLS)]
    return jnp.concatenate(rows, axis=0)


def kernel(x, norm_g, w_in, rel_bias, w_att_out, conv_w, conv_b, w_conv_out, w_out, final_norm_g, loss_target, m_norm_g, m_w_in, m_rel_bias, m_w_att_out, m_conv_w, m_conv_b, m_w_conv_out, m_w_out, m_final_norm_g, v_norm_g, v_w_in, v_rel_bias, v_w_att_out, v_conv_w, v_conv_b, v_w_conv_out, v_w_out, v_final_norm_g):
    xs, tgt = x[0], loss_target[0]
    cshard = conv_w.shape[2]
    chip = 2 * lax.axis_index("x") + lax.axis_index("y")

    shards = [w_in[0], w_att_out[0], w_conv_out[0], w_out[0]]
    cw8 = _pad_to(conv_w[0], SUBLANES, cshard)
    flips = jnp.arange(N_CHIPS, dtype=jnp.int32)
    own_first = jnp.bitwise_xor(chip, flips)
    own_last = jnp.bitwise_xor(chip, jnp.asarray(OWNER_FLIPS, jnp.int32))

    proj, h, wb_in = _in_proj_gather(xs, norm_g, shards[0], own_first)
    diag = jnp.take(rel_bias[0], _diag_rel_index(), axis=1)
    att, lse, wb_att, wb_conv, wb_out, wt_att, wt_conv, wt_out, cw_all = _attn_fwd(
        diag, proj, shards[1:], ["cols", "cols", "rows"], cw8)
    conv_w_full = jnp.transpose(cw_all, (1, 0, 2)).reshape(SUBLANES, N_CHIPS * cshard)
    (dpb, d_att, dx2, g_att_p, g_conv_p, g_out_p, loss_vec, g_fin, g_cb, g_cw) = _mixer_mid(
        att, proj, xs, tgt, wb_att, wb_conv, wb_out, wt_att, wt_conv, wt_out, conv_w_full, conv_b,
        final_norm_g[None, :])
    dqkv, g_rel, r_att, r_conv, r_out = _attn_bwd(diag, proj, d_att, att, lse, [g_att_p, g_conv_p, g_out_p])
    grad_x, g_norm = _in_proj_bwd_x(dqkv, dpb, wb_in, xs, dx2, norm_g)
    r_in = _in_proj_bwd_w(h, dqkv, dpb, own_last)

    small = _pack_small(g_norm, g_fin, g_cb, g_cw[0:3], loss_vec, g_rel)
    gw_in, gw_att, gw_conv, gw_out, r_small = _reduce_pair([r_in, r_att, r_conv, r_out], small)
    small_out = _adamw_small(
        r_small,
        [norm_g, final_norm_g[None, :], conv_b, conv_w[0], rel_bias[0]],
        [m_norm_g, m_final_norm_g[None, :], m_conv_b, m_conv_w[0], m_rel_bias[0]],
        [v_norm_g, v_final_norm_g[None, :], v_conv_b, v_conv_w[0], v_rel_bias[0]])
    loss = small_out[0][0, 0]
    small_names = ["norm_g", "final_norm_g", "conv_b", "conv_w", "rel_bias"]
    fix = {"norm_g": lambda a: a, "final_norm_g": lambda a: a[0], "conv_b": lambda a: a,
           "conv_w": lambda a: a[None], "rel_bias": lambda a: a[None]}
    small_res = {name: [fix[name](small_out[1 + 4 * p + q]) for q in range(4)] for p, name in enumerate(small_names)}

    big = {"w_in": _adamw(w_in[0], gw_in, m_w_in[0], v_w_in[0], "adamw_w_in", 128)}
    names = ["w_att_out", "w_conv_out", "w_out"]
    group = _adamw_group([w_att_out[0], w_conv_out[0], w_out[0]], [gw_att, gw_conv, gw_out],
                         [m_w_att_out[0], m_w_conv_out[0], m_w_out[0]],
                         [v_w_att_out[0], v_w_conv_out[0], v_w_out[0]], "adamw_small_matrices")
    big.update(zip(names, group))
    big = {name: tuple(a[None] for a in four) for name, four in big.items()}

    order = ["norm_g", "w_in", "rel_bias", "w_att_out", "conv_w", "conv_b", "w_conv_out", "w_out", "final_norm_g"]
    outs = [loss, grad_x[None]]
    for which in range(4):
        for name in order:
            outs.append(big[name][which] if name in big else small_res[name][which])
    return tuple(outs)
```

```python
import numpy as np
import jax
import jax.numpy as jnp
from jax import lax
from jax.experimental import pallas as pl
from jax.experimental.pallas import tpu as pltpu

F32 = jnp.float32
BF16 = jnp.bfloat16
MESH = pl.DeviceIdType.MESH

CHUNK = 64
N_LEFT = 8
HEADS = 8
HEAD_DIM = 64
D_ATT = HEADS * HEAD_DIM
MAX_REL = 128
N_REL = 2 * MAX_REL + 1
EPS = 1e-6
NEG_BIG = -1e30
ADAM_LR, ADAM_B1, ADAM_B2, ADAM_EPS, ADAM_WD, ADAM_STEP = 0.001, 0.9, 0.999, 1e-08, 0.01, 10

LANES = 128
SUBLANES = 8
VMEM_LIMIT = 56 * 1024 * 1024

QB = 2 * CHUNK
KW = N_LEFT * CHUNK + QB
DIAG = KW + QB
TQ = N_LEFT * CHUNK
TM_MID = 256
TM_MM = 512
TM_BLK = 1024
SMALL_ROWS, SMALL_COLS = 16, 1024


def _params(*sem):
    return pltpu.CompilerParams(dimension_semantics=sem, vmem_limit_bytes=VMEM_LIMIT)


def _nt(a, b):
    return lax.dot_general(a, b, (((1,), (1,)), ((), ())), preferred_element_type=F32)


def _tn(a, b):
    return lax.dot_general(a, b, (((0,), (0,)), ((), ())), preferred_element_type=F32)


def _nn(a, b):
    return jnp.dot(a, b, preferred_element_type=F32)


def _diag_rel_index():
    d = np.arange(DIAG)
    diff = np.where(d < KW, d, d - DIAG)
    rel = N_LEFT * CHUNK - diff
    return np.clip(rel, -MAX_REL, MAX_REL) + MAX_REL


def _build_bias(diag_ref, bias_scr):
    r = lax.broadcasted_iota(jnp.int32, (QB, KW), 0) // CHUNK
    s = lax.broadcasted_iota(jnp.int32, (QB, KW), 1) // CHUNK
    allowed = (s >= r) & (s <= r + N_LEFT)
    for h in range(HEADS):
        row = jnp.broadcast_to(diag_ref[h:h + 1, :], (QB, DIAG))
        t = pltpu.roll(row, 0, 1, stride=1, stride_axis=0)
        bias_scr[h // 2, (h % 2) * QB:(h % 2 + 1) * QB, :] = jnp.where(allowed, t[:, :KW], NEG_BIG)


def _stack_heads(a, lane_hi):
    zero = jnp.zeros_like(a)
    return jnp.concatenate([jnp.where(lane_hi, zero, a), jnp.where(lane_hi, a, zero)], axis=0)


def _in_proj_gather(x, g, shard, order):
    s, d = x.shape
    tn = shard.shape[1]
    n = s // TM_BLK
    hr = d // 2

    def body(order_ref, x_ref, g_ref, shard32_ref, proj_ref, h_ref, wfull_ref, shard_ref, hbuf, wbuf,
             send1, recv1, send2, recv2, lsem):
        del order_ref
        j, i = pl.program_id(0), pl.program_id(1)
        x, y, c = _position()
        mine = 2 * x + y
        chips = [(x, 1 - y), (1 - x, y), (1 - x, 1 - y)]

        def theirs(k):
            return 2 * chips[k][0] + chips[k][1]

        def half_rows(half):
            return pl.ds(half * hr, hr)

        def landing(k, shard_index, half):
            if k < 2:
                return wbuf.at[k, half_rows(half), :]
            return wfull_ref.at[shard_index, half_rows(half), :]

        def ici(k, shard_index):
            return pltpu.make_async_remote_copy(
                src_ref=shard_ref.at[half_rows(c), :], dst_ref=landing(k, shard_index, c),
                send_sem=send1.at[k], recv_sem=recv1.at[k], device_id=(*chips[k], c), device_id_type=MESH)

        def d2d(k, half):
            return pltpu.make_async_remote_copy(
                src_ref=wbuf.at[k % 2, half_rows(half), :], dst_ref=landing(k, theirs(k), half),
                send_sem=send2.at[k], recv_sem=recv2.at[k], device_id=(x, y, 1 - c), device_id_type=MESH)

        def load(k, half, sem):
            return pltpu.make_async_copy(wfull_ref.at[theirs(k), half_rows(half), :],
                                         wbuf.at[k % 2, half_rows(half), :], lsem.at[sem])

        def keep(k):
            return pltpu.make_async_copy(wbuf.at[k], wfull_ref.at[theirs(k)], lsem.at[3 + k])

        own = pltpu.make_async_copy(shard_ref, wfull_ref.at[mine], lsem.at[0])

        def put_proj(block):
            for grp in range(tn // D_ATT):
                proj_ref[grp] = block[:, grp * D_ATT:(grp + 1) * D_ATT].astype(BF16)

        @pl.when((j == 0) & (i == 0))
        def _():
            shard_ref[...] = shard32_ref[...].astype(BF16)
            own.start()
            ici(0, mine).start()
            ici(1, mine).start()

        for k in range(2):
            first = n - 1 if k == 0 else min(n // 2, n - 1)

            @pl.when((j == k) & (i == first))
            def _(k=k):
                if k == 0:
                    ici(0, mine).wait_send()
                    ici(1, mine).wait_send()
                    ici(2, mine).start()
                ici(k, theirs(k)).wait_recv()
                d2d(k, c).start()

            then = (1, 0) if k == 0 else (1, min(first + 1, n - 1))

            @pl.when((j == then[0]) & (i == then[1]))
            def _(k=k):
                d2d(k, 1 - c).wait_recv()
                keep(k).start()

        first = min(n // 2, n - 1)

        @pl.when((j == 2) & (i == first))
        def _():
            d2d(0, c).wait_send()
            keep(0).wait()
            ici(2, theirs(2)).wait_recv()
            load(2, c, 1).start()

        @pl.when((j == 2) & (i == min(first + 1, n - 1)))
        def _():
            load(2, c, 1).wait()
            d2d(2, c).start()

        @pl.when((j == 2) & (i == min(first + 2, n - 1)))
        def _():
            d2d(2, 1 - c).wait_recv()
            load(2, 1 - c, 2).start()

        @pl.when((j == 3) & (i == 0))
        def _():
            load(2, 1 - c, 2).wait()

        @pl.when(j == 0)
        def _():
            xv = x_ref[...]
            r = lax.rsqrt(jnp.mean(xv * xv, axis=-1, keepdims=True) + EPS)
            hv = ((xv * r) * g_ref[...]).astype(BF16)
            hbuf[i] = hv
            h_ref[...] = hv
            put_proj(_nn(hv, shard_ref[...]))

        for k in range(3):
            @pl.when(j == k + 1)
            def _(k=k):
                put_proj(_nn(hbuf[i], wbuf[k % 2]))

        @pl.when((j == 3) & (i == n - 1))
        def _():
            ici(2, mine).wait_send()
            d2d(1, c).wait_send()
            d2d(2, c).wait_send()
            own.wait()
            keep(1).wait()

    return pl.pallas_call(
        body, name="in_proj_gather",
        grid_spec=pltpu.PrefetchScalarGridSpec(
            num_scalar_prefetch=1, grid=(N_CHIPS, n),
            in_specs=[pl.BlockSpec((TM_BLK, d), lambda j, i, order: (jnp.where(j == 0, i, n - 1), 0)),
                      pl.BlockSpec((1, d), lambda j, i, order: (0, 0)), pl.BlockSpec(memory_space=pltpu.VMEM)],
            out_specs=[pl.BlockSpec((tn // D_ATT, TM_BLK, D_ATT), lambda j, i, order: (order[j], i, 0)),
                       pl.BlockSpec((TM_BLK, d), lambda j, i, order: (jnp.where(j == 0, i, n - 1), 0)), ANY],
            scratch_shapes=[pltpu.VMEM((d, tn), BF16), pltpu.VMEM((n, TM_BLK, d), BF16), pltpu.VMEM((2, d, tn), BF16),
                            pltpu.SemaphoreType.DMA((3,)), pltpu.SemaphoreType.DMA((3,)),
                            pltpu.SemaphoreType.DMA((3,)), pltpu.SemaphoreType.DMA((3,)), pltpu.SemaphoreType.DMA((5,))]),
        out_shape=[jax.ShapeDtypeStruct((N_CHIPS * tn // D_ATT, s, D_ATT), BF16), jax.ShapeDtypeStruct((s, d), BF16),
                   jax.ShapeDtypeStruct((N_CHIPS, d, tn), BF16)],
        compiler_params=_params("arbitrary", "arbitrary"),
    )(order, x, g, shard)


def _attn_fwd(diag, proj, shards, kinds, cw8):
    s = proj.shape[1]
    n = s // TQ
    nw = len(shards)
    scale = HEAD_DIM ** -0.5

    def body(*refs):
        diag_ref, q_ref, kp_ref, kc_ref, vp_ref, vc_ref = refs[:6]
        srcs, cw = refs[6:6 + nw], refs[6 + nw]
        o_ref, lse_ref = refs[7 + nw:9 + nw]
        dsts, cw_all = refs[9 + nw:9 + 2 * nw], refs[9 + 2 * nw]
        bias_scr = refs[10 + 2 * nw]
        casts = refs[11 + 2 * nw:11 + 3 * nw]
        start, forward, finish = _gather_plan(kinds, casts, dsts, cw, cw_all, *refs[11 + 3 * nw:])
        i = pl.program_id(0)

        @pl.when(i == 0)
        def _():
            for w in range(nw):
                casts[w][...] = srcs[w][...].astype(BF16)
            start()
            _build_bias(diag_ref, bias_scr)

        @pl.when(i == n // 2)
        def _():
            forward()

        @pl.when(i == n - 1)
        def _():
            finish()

        lane_hi = lax.broadcasted_iota(jnp.int32, (QB, LANES), 1) >= HEAD_DIM

        def block(b, first_tile):
            r0, n_prev = b * QB, TQ - b * QB
            n_cur = KW - n_prev
            pairs = range(HEADS // 2)
            lanes_of = [slice(LANES * p, LANES * (p + 1)) for p in pairs]
            scores = []
            for p in pairs:
                lanes = lanes_of[p]
                q2 = _stack_heads(q_ref[r0:r0 + QB, lanes] * scale, lane_hi)
                s_cur = _nt(q2, kc_ref[0:n_cur, lanes]) + bias_scr[p, :, n_prev:KW]
                if first_tile:
                    scores.append(s_cur)
                else:
                    scores.append(jnp.concatenate(
                        [_nt(q2, kp_ref[r0:TQ, lanes]) + bias_scr[p, :, 0:n_prev], s_cur], axis=1))
            probs = []
            for p in pairs:
                sc = scores[p]
                m = jnp.max(sc, axis=1, keepdims=True)
                pe = jnp.exp(sc - m)
                l = jnp.sum(pe, axis=1, keepdims=True)
                probs.append((pe.astype(BF16), l, m))
            for p in pairs:
                lanes = lanes_of[p]
                pb, l, m = probs[p]
                if first_tile:
                    o2 = _nn(pb, vc_ref[0:n_cur, lanes]) / l
                else:
                    o2 = (_nn(pb[:, 0:n_prev], vp_ref[r0:TQ, lanes]) + _nn(pb[:, n_prev:KW], vc_ref[0:n_cur, lanes])) / l
                lse2 = m + jnp.log(l)
                lse_ref[r0:r0 + QB, 2 * p:2 * p + 1] = lse2[0:QB, :]
                lse_ref[r0:r0 + QB, 2 * p + 1:2 * p + 2] = lse2[QB:2 * QB, :]
                o_ref[r0:r0 + QB, lanes] = jnp.where(lane_hi, o2[QB:2 * QB, :], o2[0:QB, :]).astype(BF16)

        @pl.when(i == 0)
        def _():
            for b in range(TQ // QB):
                block(b, True)

        @pl.when(i > 0)
        def _():
            for b in range(TQ // QB):
                block(b, False)

    blk = lambda grp, prev: pl.BlockSpec(
        (None, TQ, D_ATT), (lambda i: (grp, jnp.maximum(i - 1, 0), 0)) if prev else (lambda i: (grp, i, 0)))
    vmem = pl.BlockSpec(memory_space=pltpu.VMEM)
    return pl.pallas_call(
        body, name="attn_fwd", grid=(n,),
        in_specs=[pl.BlockSpec((HEADS, DIAG), lambda i: (0, 0)),
                  blk(0, False), blk(1, True), blk(1, False), blk(2, True), blk(2, False)] + [vmem] * (nw + 1),
        out_specs=[pl.BlockSpec((TQ, D_ATT), lambda i: (i, 0)), pl.BlockSpec((TQ, HEADS), lambda i: (i, 0))]
        + [ANY] * (nw + 1),
        out_shape=[jax.ShapeDtypeStruct((s, D_ATT), BF16), jax.ShapeDtypeStruct((s, HEADS), F32)]
        + _gather_out_shapes(shards, kinds, cw8),
        scratch_shapes=[pltpu.VMEM((HEADS // 2, 2 * QB, KW), F32)] + [pltpu.VMEM(a.shape, BF16) for a in shards]
        + _gather_sems(nw),
        compiler_params=_params("arbitrary"),
    )(diag, proj, proj, proj, proj, proj, *shards, cw8)


def _attn_bwd(diag, proj, d_att, att, lse, parts):
    s = proj.shape[1]
    n = s // TQ
    npart = len(parts)
    scale = HEAD_DIM ** -0.5
    rel_pad = 3 * LANES

    def body(*refs):
        diag_ref, q_ref, kp_ref, kc_ref, vp_ref, vc_ref, do_ref, o_ref, lse_ref = refs[:9]
        part_refs = refs[9:9 + npart]
        dqkv_ref, dbias_ref = refs[9 + npart:11 + npart]
        slot_refs = refs[11 + npart:11 + 2 * npart]
        bias_scr, dbias_acc, dk_acc, dv_acc, dq_scr = refs[11 + 2 * npart:16 + 2 * npart]
        start, finish = _scatter_plan(part_refs, slot_refs, *refs[16 + 2 * npart:])
        i = pl.program_id(0)
        cur, prv = i % 2, 1 - i % 2

        @pl.when(i == 0)
        def _():
            start()
            _build_bias(diag_ref, bias_scr)
            dbias_acc[...] = jnp.zeros_like(dbias_acc)
            dk_acc[...] = jnp.zeros_like(dk_acc)
            dv_acc[...] = jnp.zeros_like(dv_acc)

        @pl.when(i > 0)
        def _():
            dqkv_ref[:, 0:D_ATT] = dq_scr[...]
            dk_acc[cur] = jnp.zeros((TQ, D_ATT), F32)
            dv_acc[cur] = jnp.zeros((TQ, D_ATT), F32)

        lane_hi = lax.broadcasted_iota(jnp.int32, (QB, LANES), 1) >= HEAD_DIM
        col = lax.broadcasted_iota(jnp.int32, (2 * QB, KW), 1)

        def make_block(first_tile):
            def block(b):
                r0, n_prev = b * QB, TQ - b * QB
                n_cur = KW - n_prev
                for p in range(HEADS // 2):
                    lanes = slice(LANES * p, LANES * (p + 1))
                    q2 = _stack_heads(q_ref[r0:r0 + QB, lanes] * scale, lane_hi)
                    kw = jnp.concatenate([kp_ref[r0:TQ, lanes], kc_ref[0:n_cur, lanes]], axis=0)
                    vw = jnp.concatenate([vp_ref[r0:TQ, lanes], vc_ref[0:n_cur, lanes]], axis=0)
                    dop = do_ref[r0:r0 + QB, lanes]
                    do2 = _stack_heads(dop, lane_hi)
                    prod = dop.astype(F32) * o_ref[r0:r0 + QB, lanes].astype(F32)
                    delta2 = jnp.concatenate(
                        [jnp.sum(jnp.where(lane_hi, 0.0, prod), axis=1, keepdims=True),
                         jnp.sum(jnp.where(lane_hi, prod, 0.0), axis=1, keepdims=True)], axis=0)
                    lse2 = jnp.concatenate([lse_ref[r0:r0 + QB, 2 * p:2 * p + 1],
                                            lse_ref[r0:r0 + QB, 2 * p + 1:2 * p + 2]], axis=0)
                    sc = _nt(q2, kw) + bias_scr[p]
                    if first_tile:
                        sc = jnp.where(col >= TQ - r0, sc, NEG_BIG)
                    pr = jnp.exp(sc - lse2)
                    ds = pr * (_nt(do2, vw) - delta2)
                    dbias_acc[p] += ds
                    dsb = ds.astype(BF16)
                    dv_w = _tn(pr.astype(BF16), do2)
                    dk_w = _tn(dsb, q2)
                    dv_acc[prv, r0:TQ, lanes] += dv_w[0:n_prev, :]
                    dv_acc[cur, 0:n_cur, lanes] += dv_w[n_prev:KW, :]
                    dk_acc[prv, r0:TQ, lanes] += dk_w[0:n_prev, :]
                    dk_acc[cur, 0:n_cur, lanes] += dk_w[n_prev:KW, :]
                    dq2 = _nn(dsb, kw)
                    dq = jnp.where(lane_hi, dq2[QB:2 * QB, :], dq2[0:QB, :]) * scale
                    dq_scr[r0:r0 + QB, lanes] = dq.astype(BF16)
            return block

        @pl.when(i == 0)
        def _():
            for b in range(TQ // QB):
                make_block(True)(b)

        @pl.when((i > 0) & (i < n))
        def _():
            for b in range(TQ // QB):
                make_block(False)(b)

        @pl.when(i > 0)
        def _():
            dqkv_ref[:, D_ATT:2 * D_ATT] = dk_acc[prv].astype(BF16)
            dqkv_ref[:, 2 * D_ATT:3 * D_ATT] = dv_acc[prv].astype(BF16)

        @pl.when(i == n)
        def _():
            d_iota = lax.broadcasted_iota(jnp.int32, (DIAG, rel_pad), 0)
            n_iota = lax.broadcasted_iota(jnp.int32, (DIAG, rel_pad), 1)
            diff = jnp.where(d_iota < KW, d_iota, d_iota - DIAG)
            idx = jnp.clip(N_LEFT * CHUNK - diff, -MAX_REL, MAX_REL) + MAX_REL
            onehot = (idx == n_iota).astype(F32)
            rows = []
            for hd in range(HEADS):
                acc = dbias_acc[hd // 2, (hd % 2) * QB:(hd % 2 + 1) * QB, :]
                a = jnp.concatenate([acc, jnp.zeros((QB, DIAG - KW), F32)], axis=1)
                g8 = a[0:SUBLANES, :]
                for blk in range(1, QB // SUBLANES):
                    g8 = g8 + pltpu.roll(a[blk * SUBLANES:(blk + 1) * SUBLANES, :], DIAG - blk * SUBLANES, 1)
                g1 = g8[0:1, :]
                for r in range(1, SUBLANES):
                    g1 = g1 + pltpu.roll(g8[r:r + 1, :], DIAG - r, 1)
                rows.append(g1)
            g = jnp.concatenate(rows, axis=0)
            dbias_ref[...] = jnp.dot(g, onehot, preferred_element_type=F32, precision=lax.Precision.HIGHEST)
            finish()

    last = n - 1
    cur = lambda grp: pl.BlockSpec((None, TQ, D_ATT), lambda i: (grp, jnp.minimum(i, last), 0))
    prev = lambda grp: pl.BlockSpec((None, TQ, D_ATT), lambda i: (grp, jnp.maximum(jnp.minimum(i, last) - 1, 0), 0))
    tile = pl.BlockSpec((TQ, D_ATT), lambda i: (jnp.minimum(i, last), 0))
    return pl.pallas_call(
        body, name="attn_bwd", grid=(n + 1,),
        in_specs=[pl.BlockSpec((HEADS, DIAG), lambda i: (0, 0)),
                  cur(0), prev(1), cur(1), prev(2), cur(2), tile, tile,
                  pl.BlockSpec((TQ, HEADS), lambda i: (jnp.minimum(i, last), 0))] + [ANY] * npart,
        out_specs=[pl.BlockSpec((TQ, 3 * D_ATT), lambda i: (jnp.maximum(i - 1, 0), 0)),
                   pl.BlockSpec((HEADS, rel_pad), lambda i: (0, 0))] + [ANY] * npart,
        out_shape=[jax.ShapeDtypeStruct((s, 3 * D_ATT), BF16), jax.ShapeDtypeStruct((HEADS, rel_pad), F32)]
        + _scatter_out_shapes(parts),
        scratch_shapes=[pltpu.VMEM((HEADS // 2, 2 * QB, KW), F32), pltpu.VMEM((HEADS // 2, 2 * QB, KW), F32),
                        pltpu.VMEM((2, TQ, D_ATT), F32), pltpu.VMEM((2, TQ, D_ATT), F32),
                        pltpu.VMEM((TQ, D_ATT), BF16)] + _scatter_sems(npart),
        compiler_params=_params("arbitrary"),
    )(diag, proj, proj, proj, proj, proj, d_att, att, lse, *parts)


def _shift_down(a, k, halo):
    rolled = pltpu.roll(a, k, 0)
    row = lax.broadcasted_iota(jnp.int32, halo.shape, 0)
    first = jnp.where(row < k, pltpu.roll(halo, k, 0), rolled[0:SUBLANES, :])
    return jnp.concatenate([first, rolled[SUBLANES:, :]], axis=0)


def _shift_up(a, k, nxt):
    tm = a.shape[0]
    rolled = pltpu.roll(a, tm - k, 0)
    row = lax.broadcasted_iota(jnp.int32, nxt.shape, 0)
    last = jnp.where(row >= SUBLANES - k, pltpu.roll(nxt, SUBLANES - k, 0), rolled[tm - SUBLANES:, :])
    return jnp.concatenate([rolled[:tm - SUBLANES, :], last], axis=0)


def _sigmoid(v):
    return 0.5 * jnp.tanh(0.5 * v) + 0.5


def _mixer_mid(att, proj, x, tgt, w_att, w_conv, w_out, w_att_t, w_conv_t, w_out_t, conv_w8, conv_b, fin_g):
    s, d = x.shape
    dc = D_ATT
    n = s // TM_MID
    tm = TM_MID
    n_shards = 4

    def body(att_ref, za_ref, gb_ref, gc_ref, u_ref, zc_ref, hgc_ref, hu_ref, gatt_ref, gconv_ref, x_ref, t_ref,
             watt_ref, wconv_ref, wout_ref, watt_t_ref, wconv_t_ref, wout_t_ref, cw_ref, cb_ref, fg_ref,
             dpb_ref, do_ref, dx2_ref, gatt_o, gconv_o, gout_o, loss_o, gfn_o, gcb_o, gcw_o,
             acc_att, acc_conv, acc_out, carry):
        i = pl.program_id(0)
        tile = n - 1 - i

        @pl.when(i == 0)
        def _():
            acc_att[...] = jnp.zeros_like(acc_att)
            acc_conv[...] = jnp.zeros_like(acc_conv)
            acc_out[...] = jnp.zeros_like(acc_out)
            carry[...] = jnp.zeros_like(carry)
            loss_o[...] = jnp.zeros_like(loss_o)
            gfn_o[...] = jnp.zeros_like(gfn_o)
            gcb_o[...] = jnp.zeros_like(gcb_o)
            gcw_o[...] = jnp.zeros_like(gcw_o)

        halves = [slice(hh * (tm // 2), (hh + 1) * (tm // 2)) for hh in range(2)]
        both = lambda fn: [fn(rows) for rows in halves]
        f32 = lambda ref, rows: ref[rows, :].astype(F32)

        gc = gc_ref[...].astype(F32)
        u = u_ref[...].astype(F32)
        cu = gc * u
        halo = jnp.where(tile > 0, hgc_ref[...].astype(F32) * hu_ref[...].astype(F32), 0.0)
        cu1 = _shift_down(cu, 1, halo)
        cu2 = _shift_down(cu, 2, halo)
        w0, w1, w2 = cw_ref[0:1, :], cw_ref[1:2, :], cw_ref[2:3, :]
        fg = fg_ref[...]

        def stage_a(rows):
            att_v, za, zc, gb = f32(att_ref, rows), f32(za_ref, rows), f32(zc_ref, rows), f32(gb_ref, rows)
            sa = _sigmoid(za)
            silu_a = za * sa
            vconv = w0 * cu2[rows, :] + w1 * cu1[rows, :] + w2 * cu[rows, :] + cb_ref[...]
            sc = _sigmoid(zc)
            silu_c = zc * sc
            return dict(att_v=att_v, za=za, zc=zc, gb=gb, sa=sa, silu_a=silu_a, vconv=vconv, sc=sc, silu_c=silu_c,
                        a_b=(att_v * silu_a).astype(BF16), c_b=(gb * vconv * silu_c).astype(BF16))

        st = both(stage_a)
        for t in st:
            t["y_att"] = _nn(t["a_b"], watt_ref[...])
            t["y_conv"] = _nn(t["c_b"], wconv_ref[...])
        for t, rows in zip(st, halves):
            gpair = lambda ref: jnp.concatenate([ref[0, rows, :], ref[1, rows, :]], axis=1).astype(F32)
            t["ga"] = _sigmoid(gpair(gatt_ref))
            t["gv"] = _sigmoid(gpair(gconv_ref))
            t["m_b"] = (t["ga"] * t["y_att"] + t["gv"] * t["y_conv"]).astype(BF16)
        for t in st:
            t["mo"] = _nn(t["m_b"], wout_ref[...])
        for t, rows in zip(st, halves):
            x2 = x_ref[rows, :] + t["mo"]
            r2 = lax.rsqrt(jnp.mean(x2 * x2, axis=-1, keepdims=True) + EPS)
            x2n = x2 * r2
            err = x2n * fg - t_ref[rows, :]
            loss_o[...] += jnp.sum(err * err, axis=0, keepdims=True) * (0.5 / d)
            dy = err * (1.0 / d)
            gfn_o[...] += jnp.sum(dy * x2n, axis=0, keepdims=True)
            dyn = dy * fg
            dx2 = r2 * (dyn - x2n * jnp.mean(dyn * x2n, axis=-1, keepdims=True))
            dx2_ref[rows, :] = dx2
            t["dx2_b"] = dx2.astype(BF16)
        for t in st:
            t["dm"] = _nn(t["dx2_b"], wout_t_ref[...])
        whole = lambda key: jnp.concatenate([st[0][key], st[1][key]], axis=0)
        acc_out[...] += _tn(whole("m_b"), whole("dx2_b"))
        for t, rows in zip(st, halves):
            dy_att = t["dm"] * t["ga"]
            dy_conv = t["dm"] * t["gv"]
            dpb_ref[rows, 5 * dc:5 * dc + d] = (dy_att * t["y_att"] * (1.0 - t["ga"])).astype(BF16)
            dpb_ref[rows, 5 * dc + d:5 * dc + 2 * d] = (dy_conv * t["y_conv"] * (1.0 - t["gv"])).astype(BF16)
            t["dya_b"] = dy_att.astype(BF16)
            t["dyc_b"] = dy_conv.astype(BF16)
        for t in st:
            t["da_in"] = _nn(t["dya_b"], watt_t_ref[...])
            t["dc_in"] = _nn(t["dyc_b"], wconv_t_ref[...])
        acc_att[...] += _tn(whole("a_b"), whole("dya_b"))
        acc_conv[...] += _tn(whole("c_b"), whole("dyc_b"))
        for t, rows in zip(st, halves):
            sa, za, sc, zc = t["sa"], t["za"], t["sc"], t["zc"]
            do_ref[rows, :] = (t["da_in"] * t["silu_a"]).astype(BF16)
            dpb_ref[rows, 0:dc] = (t["da_in"] * t["att_v"] * (sa * (1.0 + za * (1.0 - sa)))).astype(BF16)
            dpb_ref[rows, dc:2 * dc] = (t["dc_in"] * t["vconv"] * t["silu_c"]).astype(BF16)
            dgs = t["dc_in"] * t["gb"]
            t["dvc"] = dgs * t["silu_c"]
            dpb_ref[rows, 4 * dc:5 * dc] = (dgs * t["vconv"] * (sc * (1.0 + zc * (1.0 - sc)))).astype(BF16)
        dvc = whole("dvc")
        gcb_o[...] += jnp.sum(dvc, axis=0, keepdims=True)
        gcw_o[0:1, :] += jnp.sum(dvc * cu2, axis=0, keepdims=True)
        gcw_o[1:2, :] += jnp.sum(dvc * cu1, axis=0, keepdims=True)
        gcw_o[2:3, :] += jnp.sum(dvc * cu, axis=0, keepdims=True)
        nxt = carry[...]
        dcu = w2 * dvc + w1 * _shift_up(dvc, 1, nxt) + w0 * _shift_up(dvc, 2, nxt)
        carry[...] = dvc[0:SUBLANES, :]
        dpb_ref[:, 2 * dc:3 * dc] = (dcu * u).astype(BF16)
        dpb_ref[:, 3 * dc:4 * dc] = (dcu * gc).astype(BF16)

        @pl.when(i == n - 1)
        def _():
            for j in range(n_shards):
                gatt_o[j] = acc_att[:, j * (d // n_shards):(j + 1) * (d // n_shards)].astype(BF16)
                gconv_o[j] = acc_conv[:, j * (d // n_shards):(j + 1) * (d // n_shards)].astype(BF16)
                gout_o[j] = acc_out[j * (d // n_shards):(j + 1) * (d // n_shards), :].astype(BF16)

    rev = lambda width, col_blk: pl.BlockSpec((tm, width), lambda i: (n - 1 - i, col_blk))
    grp = lambda g: pl.BlockSpec((None, tm, dc), lambda i: (g, n - 1 - i, 0))
    grp2 = lambda g2: pl.BlockSpec((2, tm, dc), lambda i: (g2, n - 1 - i, 0))
    halo_spec = lambda g: pl.BlockSpec(
        (None, SUBLANES, dc), lambda i: (g, jnp.maximum((n - 1 - i) * (tm // SUBLANES) - 1, 0), 0))
    const = lambda shape: pl.BlockSpec(shape, lambda i: tuple(0 for _ in shape), pipeline_mode=pl.Buffered(1))
    q4 = d // n_shards
    return pl.pallas_call(
        body, name="mixer_mid", grid=(n,),
        in_specs=[rev(dc, 0), grp(3), grp(4), grp(5), grp(6), grp(7),
                  halo_spec(5), halo_spec(6), grp2(4), grp2(5), rev(d, 0), rev(d, 0),
                  const((dc, d)), const((dc, d)), const((d, d)), const((d, dc)), const((d, dc)), const((d, d)),
                  const((SUBLANES, dc)), const((1, dc)), const((1, d))],
        out_specs=[rev(5 * dc + 2 * d, 0), rev(dc, 0), rev(d, 0),
                   const((n_shards, dc, q4)), const((n_shards, dc, q4)), const((n_shards, q4, d)),
                   const((1, d)), const((1, d)), const((1, dc)), const((SUBLANES, dc))],
        out_shape=[jax.ShapeDtypeStruct((s, 5 * dc + 2 * d), BF16), jax.ShapeDtypeStruct((s, dc), BF16),
                   jax.ShapeDtypeStruct((s, d), F32),
                   jax.ShapeDtypeStruct((n_shards, dc, q4), BF16), jax.ShapeDtypeStruct((n_shards, dc, q4), BF16),
                   jax.ShapeDtypeStruct((n_shards, q4, d), BF16),
                   jax.ShapeDtypeStruct((1, d), F32), jax.ShapeDtypeStruct((1, d), F32),
                   jax.ShapeDtypeStruct((1, dc), F32), jax.ShapeDtypeStruct((SUBLANES, dc), F32)],
        scratch_shapes=[pltpu.VMEM((dc, d), F32), pltpu.VMEM((dc, d), F32), pltpu.VMEM((d, d), F32),
                        pltpu.VMEM((SUBLANES, dc), F32)],
        compiler_params=_params("arbitrary"),
    )(att, proj, proj, proj, proj, proj, proj, proj, proj, proj, x, tgt,
      w_att, w_conv, w_out, w_att_t, w_conv_t, w_out_t, conv_w8, conv_b, fin_g)


def _in_proj_bwd_x(dqkv, dpb, w_in, x, dx2, g):
    s, d = x.shape
    tn = dqkv.shape[1]
    nb = dpb.shape[1] // tn
    n = s // TM_MM

    def body(*refs):
        dps, ws = refs[:nb + 1], refs[nb + 1:2 * nb + 2]
        x_ref, dx2_ref, g_ref, gx_ref, gng_ref = refs[2 * nb + 2:]
        i = pl.program_id(0)

        @pl.when(i == 0)
        def _():
            gng_ref[...] = jnp.zeros_like(gng_ref)

        dh = _nt(dps[0][...], ws[0][0])
        for j in range(1, nb + 1):
            dh = dh + _nt(dps[j][...], ws[j][0])
        xv = x_ref[...]
        r = lax.rsqrt(jnp.mean(xv * xv, axis=-1, keepdims=True) + EPS)
        xn = xv * r
        gng_ref[...] += jnp.sum(dh * xn, axis=0, keepdims=True)
        dhn = dh * g_ref[...]
        gx_ref[...] = dx2_ref[...] + r * (dhn - xn * jnp.mean(dhn * xn, axis=-1, keepdims=True))

    tile = lambda width, col_blk: pl.BlockSpec((TM_MM, width), lambda i: (i, col_blk))
    wspec = lambda blk: pl.BlockSpec((1, d, tn), lambda i: (blk, 0, 0), pipeline_mode=pl.Buffered(1))
    return pl.pallas_call(
        body, name="in_proj_bwd_x", grid=(n,),
        in_specs=[tile(tn, 0)] + [tile(tn, j) for j in range(nb)] + [wspec(j) for j in range(nb + 1)]
        + [tile(d, 0), tile(d, 0), pl.BlockSpec((1, d), lambda i: (0, 0))],
        out_specs=[tile(d, 0), pl.BlockSpec((1, d), lambda i: (0, 0))],
        out_shape=[jax.ShapeDtypeStruct((s, d), F32), jax.ShapeDtypeStruct((1, d), F32)],
        compiler_params=_params("arbitrary"),
    )(dqkv, *([dpb] * nb), *([w_in] * (nb + 1)), x, dx2, g)


def _in_proj_bwd_w(h, dqkv, dpb, order):
    s, d = h.shape
    tn = dqkv.shape[1]
    n = s // TM_BLK
    hr = d // 2
    settle = min(2, n - 1)

    def body(order_ref, h_ref, da_ref, db_ref, slots_ref, acc, sendbuf, pairbuf, chipbuf, psend, precv, send, recv, lsem):
        j, i = pl.program_id(0), pl.program_id(1)
        blk = order_ref[j]
        pos = _position()
        x, y, c = pos

        @pl.when(i == 0)
        def _():
            acc[...] = jnp.zeros_like(acc)

        @pl.when(blk == 0)
        def _():
            acc[...] += _tn(h_ref[...], da_ref[...])

        @pl.when(blk > 0)
        def _():
            acc[...] += _tn(h_ref[...], db_ref[...])

        def pair(step, half):
            return pltpu.make_async_remote_copy(
                src_ref=sendbuf.at[step, pl.ds(half * hr, hr), :], dst_ref=pairbuf.at[step],
                send_sem=psend.at[step], recv_sem=precv.at[step], device_id=(x, y, 1 - c), device_id_type=MESH)

        def ici(step):
            flip = OWNER_FLIPS[step]
            return pltpu.make_async_remote_copy(
                src_ref=chipbuf.at[step], dst_ref=slots_ref.at[flip], send_sem=send.at[step], recv_sem=recv.at[step],
                device_id=_peer(pos, 4 * (flip >> 1) + 2 * (flip & 1)), device_id_type=MESH)

        local = pltpu.make_async_copy(chipbuf.at[N_CHIPS - 1], slots_ref.at[0], lsem.at[0])

        def combine(step):
            pair(step, c).wait_recv()
            mine = sendbuf[step, pl.ds(c * hr, hr), :].astype(F32)
            chipbuf[step] = (mine + pairbuf[step].astype(F32)).astype(BF16)

        for step in range(N_CHIPS):
            @pl.when((j == step) & (i == n - 1))
            def _(step=step):
                sendbuf[step] = acc[...].astype(BF16)
                pair(step, 1 - c).start()

        for step in range(N_CHIPS - 1):
            @pl.when((j == step + 1) & (i == settle))
            def _(step=step):
                combine(step)
                ici(step).start()

        @pl.when((j == N_CHIPS - 1) & (i == n - 1))
        def _():
            combine(N_CHIPS - 1)
            local.start()
            for step in range(N_CHIPS - 1):
                ici(step).wait_recv()
            for step in range(N_CHIPS - 1):
                ici(step).wait_send()
            for step in range(N_CHIPS):
                pair(step, 1 - c).wait_send()
            local.wait()

    return pl.pallas_call(
        body, name="in_proj_bwd_w",
        grid_spec=pltpu.PrefetchScalarGridSpec(
            num_scalar_prefetch=1, grid=(N_CHIPS, n),
            in_specs=[pl.BlockSpec((TM_BLK, d), lambda j, i, order: (i, 0)),
                      pl.BlockSpec((TM_BLK, tn), lambda j, i, order: (jnp.where(order[j] == 0, i, 0), 0)),
                      pl.BlockSpec((TM_BLK, tn), lambda j, i, order: (jnp.where(order[j] == 0, 0, i),
                                                                     jnp.maximum(order[j] - 1, 0)))],
            out_specs=[ANY],
            scratch_shapes=[pltpu.VMEM((d, tn), F32), pltpu.VMEM((N_CHIPS, d, tn), BF16),
                            pltpu.VMEM((N_CHIPS, hr, tn), BF16), pltpu.VMEM((N_CHIPS, hr, tn), BF16),
                            pltpu.SemaphoreType.DMA((N_CHIPS,)), pltpu.SemaphoreType.DMA((N_CHIPS,)),
                            pltpu.SemaphoreType.DMA((N_CHIPS - 1,)), pltpu.SemaphoreType.DMA((N_CHIPS - 1,)),
                            pltpu.SemaphoreType.DMA((1,))]),
        out_shape=[jax.ShapeDtypeStruct((N_CHIPS, hr, tn), BF16)],
        compiler_params=_params("arbitrary", "arbitrary"),
    )(order, h, dqkv, dpb)[0]


LOSS_ROW = 6


def _adam_update(w, g, m, v):
    c1 = 1.0 / (1.0 - ADAM_B1 ** ADAM_STEP)
    c2 = 1.0 / (1.0 - ADAM_B2 ** ADAM_STEP)
    m2 = ADAM_B1 * m + (1.0 - ADAM_B1) * g
    v2 = ADAM_B2 * v + (1.0 - ADAM_B2) * (g * g)
    return -ADAM_LR * ((m2 * c1) / (jnp.sqrt(v2 * c2) + ADAM_EPS) + ADAM_WD * w), m2, v2


def _adamw_small(recv, params, moments_m, moments_v):
    k = recv.shape[0]
    n_par = len(params)
    cshard = params[3].shape[1]

    def body(*refs):
        r_ref = refs[0]
        ws, ms, vs = refs[1:1 + n_par], refs[1 + n_par:1 + 2 * n_par], refs[1 + 2 * n_par:1 + 3 * n_par]
        loss_ref = refs[1 + 3 * n_par]
        outs = refs[2 + 3 * n_par:]
        total = r_ref[0]
        for slot in range(1, k):
            total = total + r_ref[slot]
        loss_ref[...] = jnp.sum(total[LOSS_ROW:LOSS_ROW + 1, :], axis=1, keepdims=True)
        chip = 2 * lax.axis_index("x") + lax.axis_index("y")
        g_cw = jnp.zeros((3, cshard), F32)
        for sh in range(N_CHIPS):
            g_cw = g_cw + jnp.where(chip == sh, total[3:6, sh * cshard:(sh + 1) * cshard], 0.0)
        grads = [total[0:1, :], total[1:2, :], total[2:3, :ws[2].shape[1]], g_cw, total[8:16, :ws[4].shape[1]]]
        for p in range(n_par):
            delta, m2, v2 = _adam_update(ws[p][...], grads[p], ms[p][...], vs[p][...])
            for q, val in enumerate((grads[p], delta, m2, v2)):
                outs[4 * p + q][...] = val

    shapes = [jax.ShapeDtypeStruct((1, 1), F32)]
    for p in params:
        shapes += [jax.ShapeDtypeStruct(p.shape, F32)] * 4
    return pl.pallas_call(body, name="adamw_small", out_shape=shapes)(recv, *params, *moments_m, *moments_v)


def _adamw(w, g, m, v, name, rows_per_step):
    r, c = w.shape

    def body(w_ref, g_ref, m_ref, v_ref, go_ref, d_ref, mo_ref, vo_ref):
        gv = g_ref[...]
        go_ref[...] = gv
        d_ref[...], mo_ref[...], vo_ref[...] = _adam_update(w_ref[...], gv, m_ref[...], v_ref[...])

    spec = pl.BlockSpec((rows_per_step, c), lambda i: (i, 0))
    shape = jax.ShapeDtypeStruct((r, c), F32)
    return pl.pallas_call(
        body, name=name, grid=(r // rows_per_step,),
        in_specs=[spec] * 4, out_specs=[spec] * 4, out_shape=[shape] * 4,
        compiler_params=_params("parallel"),
    )(w, g, m, v)


def _adamw_group(ws, gs, ms, vs, name):
    k = len(ws)

    def body(*refs):
        ins, outs = refs[:4 * k], refs[4 * k:]
        for j in range(k):
            gv = ins[k + j][...]
            outs[4 * j][...] = gv
            outs[4 * j + 1][...], outs[4 * j + 2][...], outs[4 * j + 3][...] = _adam_update(
                ins[j][...], gv, ins[2 * k + j][...], ins[3 * k + j][...])

    shapes = []
    for w in ws:
        shapes += [jax.ShapeDtypeStruct(w.shape, F32)] * 4
    out = pl.pallas_call(body, name=name, out_shape=shapes,
                         compiler_params=pltpu.CompilerParams(vmem_limit_bytes=VMEM_LIMIT))(*ws, *gs, *ms, *vs)
    return [tuple(out[4 * j:4 * j + 4]) for j in range(k)]


ANY = pl.BlockSpec(memory_space=pl.ANY)
N_CHIPS = 4
N_DEV = 8
OWNER_FLIPS = (3, 1, 2, 0)


def _position():
    return lax.axis_index("x"), lax.axis_index("y"), lax.axis_index("c")


def _gather_out_shapes(shards, kinds, cw8):
    full = [(a.shape[0], a.shape[1] * N_CHIPS) if k == "cols" else (a.shape[0] * N_CHIPS, a.shape[1])
            for a, k in zip(shards, kinds)]
    return [jax.ShapeDtypeStruct(f, BF16) for f in full] + [
        jax.ShapeDtypeStruct((N_CHIPS,) + cw8.shape, cw8.dtype)]


def _gather_sems(nw):
    return [pltpu.SemaphoreType.DMA((3, nw)), pltpu.SemaphoreType.DMA((3, nw)),
            pltpu.SemaphoreType.DMA((3, nw)), pltpu.SemaphoreType.DMA((3, nw)),
            pltpu.SemaphoreType.DMA((3,)), pltpu.SemaphoreType.DMA((3,)), pltpu.SemaphoreType.DMA((nw + 1,))]


def _gather_plan(kinds, srcs, dsts, cw, cw_all, send1, recv1, send2, recv2, ssend, srecv, lsem):
    nw = len(srcs)
    x, y, c = _position()
    mine = 2 * x + y
    chips = [(x, 1 - y), (1 - x, y), (1 - x, 1 - y)]

    def window(w, shard, half):
        r, cc = srcs[w].shape
        hr = r // 2
        if kinds[w] == "cols":
            rows = pl.ds(0, r) if half is None else pl.ds(half * hr, hr)
            return dsts[w].at[rows, pl.ds(shard * cc, cc)]
        rows = pl.ds(shard * r, r) if half is None else pl.ds(shard * r + half * hr, hr)
        return dsts[w].at[rows, :]

    def my_half(w):
        hr = srcs[w].shape[0] // 2
        return srcs[w].at[pl.ds(c * hr, hr), :]

    def local():
        return [pltpu.make_async_copy(srcs[w], window(w, mine, None), lsem.at[w]) for w in range(nw)] + [
            pltpu.make_async_copy(cw, cw_all.at[mine], lsem.at[nw])]

    def ici(k, w, shard):
        kx, ky = chips[k]
        return pltpu.make_async_remote_copy(
            src_ref=my_half(w), dst_ref=window(w, shard, c), send_sem=send1.at[k, w], recv_sem=recv1.at[k, w],
            device_id=(kx, ky, c), device_id_type=MESH)

    def d2d(k, w, shard, half):
        return pltpu.make_async_remote_copy(
            src_ref=window(w, shard, half), dst_ref=window(w, shard, half),
            send_sem=send2.at[k, w], recv_sem=recv2.at[k, w], device_id=(x, y, 1 - c), device_id_type=MESH)

    def small(k, shard):
        kx, ky = chips[k]
        return pltpu.make_async_remote_copy(
            src_ref=cw, dst_ref=cw_all.at[shard], send_sem=ssend.at[k], recv_sem=srecv.at[k],
            device_id=(kx, ky, c), device_id_type=MESH)

    def theirs(k):
        kx, ky = chips[k]
        return 2 * kx + ky

    def start():
        for cp in local():
            cp.start()
        for k in range(3):
            for w in range(nw):
                ici(k, w, mine).start()
            small(k, mine).start()

    def forward():
        for k in range(3):
            for w in range(nw):
                ici(k, w, theirs(k)).wait_recv()
                d2d(k, w, theirs(k), c).start()

    def finish():
        for k in range(3):
            for w in range(nw):
                d2d(k, w, theirs(k), 1 - c).wait_recv()
            small(k, theirs(k)).wait_recv()
        for k in range(3):
            for w in range(nw):
                ici(k, w, mine).wait_send()
                d2d(k, w, theirs(k), c).wait_send()
            small(k, mine).wait_send()
        for cp in local():
            cp.wait()

    return start, forward, finish


def _scatter_out_shapes(parts):
    return [jax.ShapeDtypeStruct((N_DEV, p.shape[1] // 2, p.shape[2]), p.dtype) for p in parts]


def _scatter_sems(nw):
    return [pltpu.SemaphoreType.DMA((N_DEV, nw)), pltpu.SemaphoreType.DMA((N_DEV, nw)), pltpu.SemaphoreType.DMA((nw,))]


def _peer(pos, k):
    x, y, c = pos
    return ((1 - x) if k & 4 else x, (1 - y) if k & 2 else y, (1 - c) if k & 1 else c)


def _scatter_plan(srcs, dsts, send, recv, lsem):
    nw = len(srcs)
    pos = _position()

    def piece(w, k):
        px, py, pc = _peer(pos, k)
        hr = srcs[w].shape[1] // 2
        return srcs[w].at[2 * px + py, pl.ds(pc * hr, hr), :]

    def remote(w, k):
        return pltpu.make_async_remote_copy(
            src_ref=piece(w, k), dst_ref=dsts[w].at[k], send_sem=send.at[k, w], recv_sem=recv.at[k, w],
            device_id=_peer(pos, k), device_id_type=MESH)

    def local(w):
        return pltpu.make_async_copy(piece(w, 0), dsts[w].at[0], lsem.at[w])

    def start():
        for w in range(nw):
            local(w).start()
        for k in range(1, N_DEV):
            for w in range(nw):
                remote(w, k).start()

    def finish():
        for k in range(1, N_DEV):
            for w in range(nw):
                remote(w, k).wait_recv()
        for k in range(1, N_DEV):
            for w in range(nw):
                remote(w, k).wait_send()
        for w in range(nw):
            local(w).wait()

    return start, finish


def _reduce_pair(slots, small):
    nw = len(slots)

    def body(*refs):
        srcs, sm = refs[:nw], refs[nw]
        dsts, sm_all = refs[nw + 1:2 * nw + 1], refs[2 * nw + 1]
        halves = refs[2 * nw + 2:3 * nw + 2]
        send, recv, ssend, srecv, lsem = refs[3 * nw + 2:]
        pos = _position()
        x, y, c = pos
        me = 4 * x + 2 * y + c

        def rows(w, half):
            hr = halves[w].shape[0]
            return dsts[w].at[pl.ds(half * hr, hr), :]

        def remote(w, half):
            return pltpu.make_async_remote_copy(
                src_ref=halves[w], dst_ref=rows(w, half), send_sem=send.at[w], recv_sem=recv.at[w],
                device_id=(x, y, 1 - c), device_id_type=MESH)

        def bcast(k, slot):
            return pltpu.make_async_remote_copy(
                src_ref=sm, dst_ref=sm_all.at[slot], send_sem=ssend.at[k], recv_sem=srecv.at[k],
                device_id=_peer(pos, k), device_id_type=MESH)

        small_copies = [bcast(k, me) for k in range(1, N_DEV)]
        own_small = pltpu.make_async_copy(sm, sm_all.at[me], lsem.at[nw])
        for cp in small_copies + [own_small]:
            cp.start()
        big = []
        for w in range(nw):
            total = srcs[w][0].astype(F32)
            for k in range(1, srcs[w].shape[0]):
                total = total + srcs[w][k].astype(F32)
            halves[w][...] = total
            big += [remote(w, c), pltpu.make_async_copy(halves[w], rows(w, c), lsem.at[w])]
            big[-2].start()
            big[-1].start()
        for w in range(nw):
            remote(w, 1 - c).wait_recv()
        for k in range(1, N_DEV):
            px, py, pc = _peer(pos, k)
            bcast(k, 4 * px + 2 * py + pc).wait_recv()
        for w in range(nw):
            big[2 * w].wait_send()
            big[2 * w + 1].wait()
        for cp in small_copies:
            cp.wait_send()
        own_small.wait()

    vmem = pl.BlockSpec(memory_space=pltpu.VMEM)
    half_shapes = [(sl.shape[1], sl.shape[2]) for sl in slots]
    return pl.pallas_call(
        body, name="reduce_pair",
        in_specs=[vmem] * (nw + 1), out_specs=[ANY] * (nw + 1),
        out_shape=[jax.ShapeDtypeStruct((2 * r, cc), F32) for r, cc in half_shapes]
        + [jax.ShapeDtypeStruct((N_DEV,) + small.shape, small.dtype)],
        scratch_shapes=[pltpu.VMEM(hs, F32) for hs in half_shapes]
        + [pltpu.SemaphoreType.DMA((nw,)), pltpu.SemaphoreType.DMA((nw,)),
           pltpu.SemaphoreType.DMA((N_DEV,)), pltpu.SemaphoreType.DMA((N_DEV,)),
           pltpu.SemaphoreType.DMA((nw + 1,))],
        compiler_params=pltpu.CompilerParams(vmem_limit_bytes=VMEM_LIMIT),
    )(*slots, small)


def _pad_to(a, rows, cols):
    return jnp.pad(a, ((0, rows - a.shape[0]), (0, cols - a.shape[1])))


def _pack_small(norm_g, fin_g, conv_b, conv_w, loss_vec, rel):
    rows = [_pad_to(norm_g, 1, SMALL_COLS), _pad_to(fin_g, 1, SMALL_COLS), _pad_to(conv_b, 1, SMALL_COLS),
            _pad_to(conv_w, 3, SMALL_COLS), _pad_to(loss_vec, 2, SMALL_COLS), _pad_to(rel, HEADS, SMALL_COLS)]
    return jnp.concatenate(rows, axis=0)


def kernel(x, norm_g, w_in, rel_bias, w_att_out, conv_w, conv_b, w_conv_out, w_out, final_norm_g, loss_target, m_norm_g, m_w_in, m_rel_bias, m_w_att_out, m_conv_w, m_conv_b, m_w_conv_out, m_w_out, m_final_norm_g, v_norm_g, v_w_in, v_rel_bias, v_w_att_out, v_conv_w, v_conv_b, v_w_conv_out, v_w_out, v_final_norm_g):
    xs, tgt = x[0], loss_target[0]
    cshard = conv_w.shape[2]
    chip = 2 * lax.axis_index("x") + lax.axis_index("y")

    shards = [w_in[0], w_att_out[0], w_conv_out[0], w_out[0]]
    cw8 = _pad_to(conv_w[0], SUBLANES, cshard)
    flips = jnp.arange(N_CHIPS, dtype=jnp.int32)
    own_first = jnp.bitwise_xor(chip, flips)
    own_last = jnp.bitwise_xor(chip, jnp.asarray(OWNER_FLIPS, jnp.int32))

    proj, h, wb_in = _in_proj_gather(xs, norm_g, shards[0], own_first)
    diag = jnp.take(rel_bias[0], _diag_rel_index(), axis=1)
    att, lse, wb_att, wb_conv, wb_out, cw_all = _attn_fwd(diag, proj, shards[1:], ["cols", "cols", "rows"], cw8)
    conv_w_full = jnp.transpose(cw_all, (1, 0, 2)).reshape(SUBLANES, N_CHIPS * cshard)
    (dpb, d_att, dx2, g_att_p, g_conv_p, g_out_p, loss_vec, g_fin, g_cb, g_cw) = _mixer_mid(
        att, proj, xs, tgt, wb_att, wb_conv, wb_out, wb_att.T, wb_conv.T, wb_out.T, conv_w_full, conv_b,
        final_norm_g[None, :])
    dqkv, g_rel, r_att, r_conv, r_out = _attn_bwd(diag, proj, d_att, att, lse, [g_att_p, g_conv_p, g_out_p])
    grad_x, g_norm = _in_proj_bwd_x(dqkv, dpb, wb_in, xs, dx2, norm_g)
    r_in = _in_proj_bwd_w(h, dqkv, dpb, own_last)

    small = _pack_small(g_norm, g_fin, g_cb, g_cw[0:3], loss_vec, g_rel)
    gw_in, gw_att, gw_conv, gw_out, r_small = _reduce_pair([r_in, r_att, r_conv, r_out], small)
    small_out = _adamw_small(
        r_small,
        [norm_g, final_norm_g[None, :], conv_b, conv_w[0], rel_bias[0]],
        [m_norm_g, m_final_norm_g[None, :], m_conv_b, m_conv_w[0], m_rel_bias[0]],
        [v_norm_g, v_final_norm_g[None, :], v_conv_b, v_conv_w[0], v_rel_bias[0]])
    loss = small_out[0][0, 0]
    small_names = ["norm_g", "final_norm_g", "conv_b", "conv_w", "rel_bias"]
    fix = {"norm_g": lambda a: a, "final_norm_g": lambda a: a[0], "conv_b": lambda a: a,
           "conv_w": lambda a: a[None], "rel_bias": lambda a: a[None]}
    small_res = {name: [fix[name](small_out[1 + 4 * p + q]) for q in range(4)] for p, name in enumerate(small_names)}

    big = {"w_in": _adamw(w_in[0], gw_in, m_w_in[0], v_w_in[0], "adamw_w_in", 128)}
    names = ["w_att_out", "w_conv_out", "w_out"]
    group = _adamw_group([w_att_out[0], w_conv_out[0], w_out[0]], [gw_att, gw_conv, gw_out],
                         [m_w_att_out[0], m_w_conv_out[0], m_w_out[0]],
                         [v_w_att_out[0], v_w_conv_out[0], v_w_out[0]], "adamw_small_matrices")
    big.update(zip(names, group))
    big = {name: tuple(a[None] for a in four) for name, four in big.items()}

    order = ["norm_g", "w_in", "rel_bias", "w_att_out", "conv_w", "conv_b", "w_conv_out", "w_out", "final_norm_g"]
    outs = [loss, grad_x[None]]
    for which in range(4):
        for name in order:
            outs.append(big[name][which] if name in big else small_res[name][which])
    return tuple(outs)
```

```python
import numpy as np
import jax
import jax.numpy as jnp
from jax import lax
from jax.experimental import pallas as pl
from jax.experimental.pallas import tpu as pltpu

F32 = jnp.float32
BF16 = jnp.bfloat16
MESH = pl.DeviceIdType.MESH

CHUNK = 64
N_LEFT = 8
HEADS = 8
HEAD_DIM = 64
D_ATT = HEADS * HEAD_DIM
MAX_REL = 128
N_REL = 2 * MAX_REL + 1
EPS = 1e-6
NEG_BIG = -1e30
ADAM_LR, ADAM_B1, ADAM_B2, ADAM_EPS, ADAM_WD, ADAM_STEP = 0.001, 0.9, 0.999, 1e-08, 0.01, 10

LANES = 128
SUBLANES = 8
VMEM_LIMIT = 56 * 1024 * 1024

QB = 2 * CHUNK
KW = N_LEFT * CHUNK + QB
DIAG = KW + QB
TQ = N_LEFT * CHUNK
TM_MID = 256
TM_MM = 512
TM_BLK = 1024
SMALL_ROWS, SMALL_COLS = 16, 1024


def _params(*sem):
    return pltpu.CompilerParams(dimension_semantics=sem, vmem_limit_bytes=VMEM_LIMIT)


def _nt(a, b):
    return lax.dot_general(a, b, (((1,), (1,)), ((), ())), preferred_element_type=F32)


def _tn(a, b):
    return lax.dot_general(a, b, (((0,), (0,)), ((), ())), preferred_element_type=F32)


def _nn(a, b):
    return jnp.dot(a, b, preferred_element_type=F32)


def _diag_rel_index():
    d = np.arange(DIAG)
    diff = np.where(d < KW, d, d - DIAG)
    rel = N_LEFT * CHUNK - diff
    return np.clip(rel, -MAX_REL, MAX_REL) + MAX_REL


def _build_bias(diag_ref, bias_scr):
    r = lax.broadcasted_iota(jnp.int32, (QB, KW), 0) // CHUNK
    s = lax.broadcasted_iota(jnp.int32, (QB, KW), 1) // CHUNK
    allowed = (s >= r) & (s <= r + N_LEFT)
    for h in range(HEADS):
        row = jnp.broadcast_to(diag_ref[h:h + 1, :], (QB, DIAG))
        t = pltpu.roll(row, 0, 1, stride=1, stride_axis=0)
        bias_scr[h // 2, (h % 2) * QB:(h % 2 + 1) * QB, :] = jnp.where(allowed, t[:, :KW], NEG_BIG)


def _stack_heads(a, lane_hi):
    zero = jnp.zeros_like(a)
    return jnp.concatenate([jnp.where(lane_hi, zero, a), jnp.where(lane_hi, a, zero)], axis=0)


def _in_proj_gather(x, g, shard, order):
    s, d = x.shape
    tn = shard.shape[1]
    n = s // TM_BLK
    hr = d // 2

    def body(order_ref, x_ref, g_ref, shard32_ref, proj_ref, h_ref, wfull_ref, shard_ref, hbuf, wbuf,
             send1, recv1, send2, recv2, lsem):
        del order_ref
        j, i = pl.program_id(0), pl.program_id(1)
        x, y, c = _position()
        mine = 2 * x + y
        chips = [(x, 1 - y), (1 - x, y), (1 - x, 1 - y)]

        def theirs(k):
            return 2 * chips[k][0] + chips[k][1]

        def half_rows(half):
            return pl.ds(half * hr, hr)

        def landing(k, shard_index, half):
            if k < 2:
                return wbuf.at[k, half_rows(half), :]
            return wfull_ref.at[shard_index, half_rows(half), :]

        def ici(k, shard_index):
            return pltpu.make_async_remote_copy(
                src_ref=shard_ref.at[half_rows(c), :], dst_ref=landing(k, shard_index, c),
                send_sem=send1.at[k], recv_sem=recv1.at[k], device_id=(*chips[k], c), device_id_type=MESH)

        def d2d(k, half):
            return pltpu.make_async_remote_copy(
                src_ref=wbuf.at[k % 2, half_rows(half), :], dst_ref=landing(k, theirs(k), half),
                send_sem=send2.at[k], recv_sem=recv2.at[k], device_id=(x, y, 1 - c), device_id_type=MESH)

        def load(k, half, sem):
            return pltpu.make_async_copy(wfull_ref.at[theirs(k), half_rows(half), :],
                                         wbuf.at[k % 2, half_rows(half), :], lsem.at[sem])

        def keep(k):
            return pltpu.make_async_copy(wbuf.at[k], wfull_ref.at[theirs(k)], lsem.at[3 + k])

        own = pltpu.make_async_copy(shard_ref, wfull_ref.at[mine], lsem.at[0])

        def put_proj(block):
            for grp in range(tn // D_ATT):
                proj_ref[grp] = block[:, grp * D_ATT:(grp + 1) * D_ATT].astype(BF16)

        @pl.when((j == 0) & (i == 0))
        def _():
            shard_ref[...] = shard32_ref[...].astype(BF16)
            own.start()
            ici(0, mine).start()
            ici(1, mine).start()

        for k in range(2):
            first = n - 1 if k == 0 else min(n // 2, n - 1)

            @pl.when((j == k) & (i == first))
            def _(k=k):
                if k == 0:
                    ici(0, mine).wait_send()
                    ici(1, mine).wait_send()
                    ici(2, mine).start()
                ici(k, theirs(k)).wait_recv()
                d2d(k, c).start()

            then = (1, 0) if k == 0 else (1, min(first + 1, n - 1))

            @pl.when((j == then[0]) & (i == then[1]))
            def _(k=k):
                d2d(k, 1 - c).wait_recv()
                keep(k).start()

        first = min(n // 2, n - 1)

        @pl.when((j == 2) & (i == first))
        def _():
            d2d(0, c).wait_send()
            keep(0).wait()
            ici(2, theirs(2)).wait_recv()
            load(2, c, 1).start()

        @pl.when((j == 2) & (i == min(first + 1, n - 1)))
        def _():
            load(2, c, 1).wait()
            d2d(2, c).start()

        @pl.when((j == 2) & (i == min(first + 2, n - 1)))
        def _():
            d2d(2, 1 - c).wait_recv()
            load(2, 1 - c, 2).start()

        @pl.when((j == 3) & (i == 0))
        def _():
            load(2, 1 - c, 2).wait()

        @pl.when(j == 0)
        def _():
            xv = x_ref[...]
            r = lax.rsqrt(jnp.mean(xv * xv, axis=-1, keepdims=True) + EPS)
            hv = ((xv * r) * g_ref[...]).astype(BF16)
            hbuf[i] = hv
            h_ref[...] = hv
            put_proj(_nn(hv, shard_ref[...]))

        for k in range(3):
            @pl.when(j == k + 1)
            def _(k=k):
                put_proj(_nn(hbuf[i], wbuf[k % 2]))

        @pl.when((j == 3) & (i == n - 1))
        def _():
            ici(2, mine).wait_send()
            d2d(1, c).wait_send()
            d2d(2, c).wait_send()
            own.wait()
            keep(1).wait()

    return pl.pallas_call(
        body, name="in_proj_gather",
        grid_spec=pltpu.PrefetchScalarGridSpec(
            num_scalar_prefetch=1, grid=(N_CHIPS, n),
            in_specs=[pl.BlockSpec((TM_BLK, d), lambda j, i, order: (jnp.where(j == 0, i, n - 1), 0)),
                      pl.BlockSpec((1, d), lambda j, i, order: (0, 0)), pl.BlockSpec(memory_space=pltpu.VMEM)],
            out_specs=[pl.BlockSpec((tn // D_ATT, TM_BLK, D_ATT), lambda j, i, order: (order[j], i, 0)),
                       pl.BlockSpec((TM_BLK, d), lambda j, i, order: (jnp.where(j == 0, i, n - 1), 0)), ANY],
            scratch_shapes=[pltpu.VMEM((d, tn), BF16), pltpu.VMEM((n, TM_BLK, d), BF16), pltpu.VMEM((2, d, tn), BF16),
                            pltpu.SemaphoreType.DMA((3,)), pltpu.SemaphoreType.DMA((3,)),
                            pltpu.SemaphoreType.DMA((3,)), pltpu.SemaphoreType.DMA((3,)), pltpu.SemaphoreType.DMA((5,))]),
        out_shape=[jax.ShapeDtypeStruct((N_CHIPS * tn // D_ATT, s, D_ATT), BF16), jax.ShapeDtypeStruct((s, d), BF16),
                   jax.ShapeDtypeStruct((N_CHIPS, d, tn), BF16)],
        compiler_params=_params("arbitrary", "arbitrary"),
    )(order, x, g, shard)


def _attn_fwd(diag, proj, shards, kinds, cw8):
    s = proj.shape[1]
    n = s // TQ
    nw = len(shards)
    scale = HEAD_DIM ** -0.5

    def body(*refs):
        diag_ref, q_ref, kp_ref, kc_ref, vp_ref, vc_ref = refs[:6]
        srcs, cw = refs[6:6 + nw], refs[6 + nw]
        o_ref, lse_ref = refs[7 + nw:9 + nw]
        dsts, cw_all = refs[9 + nw:9 + 2 * nw], refs[9 + 2 * nw]
        bias_scr = refs[10 + 2 * nw]
        casts = refs[11 + 2 * nw:11 + 3 * nw]
        start, forward, finish = _gather_plan(kinds, casts, dsts, cw, cw_all, *refs[11 + 3 * nw:])
        i = pl.program_id(0)

        @pl.when(i == 0)
        def _():
            for w in range(nw):
                casts[w][...] = srcs[w][...].astype(BF16)
            start()
            _build_bias(diag_ref, bias_scr)

        @pl.when(i == n // 2)
        def _():
            forward()

        @pl.when(i == n - 1)
        def _():
            finish()

        lane_hi = lax.broadcasted_iota(jnp.int32, (QB, LANES), 1) >= HEAD_DIM

        def block(b, first_tile):
            r0, n_prev = b * QB, TQ - b * QB
            n_cur = KW - n_prev
            pairs = range(HEADS // 2)
            lanes_of = [slice(LANES * p, LANES * (p + 1)) for p in pairs]
            scores = []
            for p in pairs:
                lanes = lanes_of[p]
                q2 = _stack_heads(q_ref[r0:r0 + QB, lanes] * scale, lane_hi)
                s_cur = _nt(q2, kc_ref[0:n_cur, lanes]) + bias_scr[p, :, n_prev:KW]
                if first_tile:
                    scores.append(s_cur)
                else:
                    scores.append(jnp.concatenate(
                        [_nt(q2, kp_ref[r0:TQ, lanes]) + bias_scr[p, :, 0:n_prev], s_cur], axis=1))
            probs = []
            for p in pairs:
                sc = scores[p]
                m = jnp.max(sc, axis=1, keepdims=True)
                pe = jnp.exp(sc - m)
                l = jnp.sum(pe, axis=1, keepdims=True)
                probs.append((pe.astype(BF16), l, m))
            for p in pairs:
                lanes = lanes_of[p]
                pb, l, m = probs[p]
                if first_tile:
                    o2 = _nn(pb, vc_ref[0:n_cur, lanes]) / l
                else:
                    o2 = (_nn(pb[:, 0:n_prev], vp_ref[r0:TQ, lanes]) + _nn(pb[:, n_prev:KW], vc_ref[0:n_cur, lanes])) / l
                lse2 = m + jnp.log(l)
                lse_ref[r0:r0 + QB, 2 * p:2 * p + 1] = lse2[0:QB, :]
                lse_ref[r0:r0 + QB, 2 * p + 1:2 * p + 2] = lse2[QB:2 * QB, :]
                o_ref[r0:r0 + QB, lanes] = jnp.where(lane_hi, o2[QB:2 * QB, :], o2[0:QB, :]).astype(BF16)

        @pl.when(i == 0)
        def _():
            for b in range(TQ // QB):
                block(b, True)

        @pl.when(i > 0)
        def _():
            for b in range(TQ // QB):
                block(b, False)

    blk = lambda grp, prev: pl.BlockSpec(
        (None, TQ, D_ATT), (lambda i: (grp, jnp.maximum(i - 1, 0), 0)) if prev else (lambda i: (grp, i, 0)))
    vmem = pl.BlockSpec(memory_space=pltpu.VMEM)
    return pl.pallas_call(
        body, name="attn_fwd", grid=(n,),
        in_specs=[pl.BlockSpec((HEADS, DIAG), lambda i: (0, 0)),
                  blk(0, False), blk(1, True), blk(1, False), blk(2, True), blk(2, False)] + [vmem] * (nw + 1),
        out_specs=[pl.BlockSpec((TQ, D_ATT), lambda i: (i, 0)), pl.BlockSpec((TQ, HEADS), lambda i: (i, 0))]
        + [ANY] * (nw + 1),
        out_shape=[jax.ShapeDtypeStruct((s, D_ATT), BF16), jax.ShapeDtypeStruct((s, HEADS), F32)]
        + _gather_out_shapes(shards, kinds, cw8),
        scratch_shapes=[pltpu.VMEM((HEADS // 2, 2 * QB, KW), F32)] + [pltpu.VMEM(a.shape, BF16) for a in shards]
        + _gather_sems(nw),
        compiler_params=_params("arbitrary"),
    )(diag, proj, proj, proj, proj, proj, *shards, cw8)


def _attn_bwd(diag, proj, d_att, att, lse, parts):
    s = proj.shape[1]
    n = s // TQ
    npart = len(parts)
    scale = HEAD_DIM ** -0.5
    rel_pad = 3 * LANES

    def body(*refs):
        diag_ref, q_ref, kp_ref, kc_ref, vp_ref, vc_ref, do_ref, o_ref, lse_ref = refs[:9]
        part_refs = refs[9:9 + npart]
        dqkv_ref, dbias_ref = refs[9 + npart:11 + npart]
        slot_refs = refs[11 + npart:11 + 2 * npart]
        bias_scr, dbias_acc, dk_acc, dv_acc, dq_scr = refs[11 + 2 * npart:16 + 2 * npart]
        start, finish = _scatter_plan(part_refs, slot_refs, *refs[16 + 2 * npart:])
        i = pl.program_id(0)
        cur, prv = i % 2, 1 - i % 2

        @pl.when(i == 0)
        def _():
            start()
            _build_bias(diag_ref, bias_scr)
            dbias_acc[...] = jnp.zeros_like(dbias_acc)
            dk_acc[...] = jnp.zeros_like(dk_acc)
            dv_acc[...] = jnp.zeros_like(dv_acc)

        @pl.when(i > 0)
        def _():
            dqkv_ref[:, 0:D_ATT] = dq_scr[...]
            dk_acc[cur] = jnp.zeros((TQ, D_ATT), F32)
            dv_acc[cur] = jnp.zeros((TQ, D_ATT), F32)

        lane_hi = lax.broadcasted_iota(jnp.int32, (QB, LANES), 1) >= HEAD_DIM
        col = lax.broadcasted_iota(jnp.int32, (2 * QB, KW), 1)

        def make_block(first_tile):
            def block(b):
                r0, n_prev = b * QB, TQ - b * QB
                n_cur = KW - n_prev
                for p in range(HEADS // 2):
                    lanes = slice(LANES * p, LANES * (p + 1))
                    q2 = _stack_heads(q_ref[r0:r0 + QB, lanes] * scale, lane_hi)
                    kw = jnp.concatenate([kp_ref[r0:TQ, lanes], kc_ref[0:n_cur, lanes]], axis=0)
                    vw = jnp.concatenate([vp_ref[r0:TQ, lanes], vc_ref[0:n_cur, lanes]], axis=0)
                    dop = do_ref[r0:r0 + QB, lanes]
                    do2 = _stack_heads(dop, lane_hi)
                    prod = dop.astype(F32) * o_ref[r0:r0 + QB, lanes].astype(F32)
                    delta2 = jnp.concatenate(
                        [jnp.sum(jnp.where(lane_hi, 0.0, prod), axis=1, keepdims=True),
                         jnp.sum(jnp.where(lane_hi, prod, 0.0), axis=1, keepdims=True)], axis=0)
                    lse2 = jnp.concatenate([lse_ref[r0:r0 + QB, 2 * p:2 * p + 1],
                                            lse_ref[r0:r0 + QB, 2 * p + 1:2 * p + 2]], axis=0)
                    sc = _nt(q2, kw) + bias_scr[p]
                    if first_tile:
                        sc = jnp.where(col >= TQ - r0, sc, NEG_BIG)
                    pr = jnp.exp(sc - lse2)
                    ds = pr * (_nt(do2, vw) - delta2)
                    dbias_acc[p] += ds
                    dsb = ds.astype(BF16)
                    dv_w = _tn(pr.astype(BF16), do2)
                    dk_w = _tn(dsb, q2)
                    dv_acc[prv, r0:TQ, lanes] += dv_w[0:n_prev, :]
                    dv_acc[cur, 0:n_cur, lanes] += dv_w[n_prev:KW, :]
                    dk_acc[prv, r0:TQ, lanes] += dk_w[0:n_prev, :]
                    dk_acc[cur, 0:n_cur, lanes] += dk_w[n_prev:KW, :]
                    dq2 = _nn(dsb, kw)
                    dq = jnp.where(lane_hi, dq2[QB:2 * QB, :], dq2[0:QB, :]) * scale
                    dq_scr[r0:r0 + QB, lanes] = dq.astype(BF16)
            return block

        @pl.when(i == 0)
        def _():
            for b in range(TQ // QB):
                make_block(True)(b)

        @pl.when((i > 0) & (i < n))
        def _():
            for b in range(TQ // QB):
                make_block(False)(b)

        @pl.when(i > 0)
        def _():
            dqkv_ref[:, D_ATT:2 * D_ATT] = dk_acc[prv].astype(BF16)
            dqkv_ref[:, 2 * D_ATT:3 * D_ATT] = dv_acc[prv].astype(BF16)

        @pl.when(i == n)
        def _():
            d_iota = lax.broadcasted_iota(jnp.int32, (DIAG, rel_pad), 0)
            n_iota = lax.broadcasted_iota(jnp.int32, (DIAG, rel_pad), 1)
            diff = jnp.where(d_iota < KW, d_iota, d_iota - DIAG)
            idx = jnp.clip(N_LEFT * CHUNK - diff, -MAX_REL, MAX_REL) + MAX_REL
            onehot = (idx == n_iota).astype(F32)
            rows = []
            for hd in range(HEADS):
                acc = dbias_acc[hd // 2, (hd % 2) * QB:(hd % 2 + 1) * QB, :]
                a = jnp.concatenate([acc, jnp.zeros((QB, DIAG - KW), F32)], axis=1)
                g8 = a[0:SUBLANES, :]
                for blk in range(1, QB // SUBLANES):
                    g8 = g8 + pltpu.roll(a[blk * SUBLANES:(blk + 1) * SUBLANES, :], DIAG - blk * SUBLANES, 1)
                g1 = g8[0:1, :]
                for r in range(1, SUBLANES):
                    g1 = g1 + pltpu.roll(g8[r:r + 1, :], DIAG - r, 1)
                rows.append(g1)
            g = jnp.concatenate(rows, axis=0)
            dbias_ref[...] = jnp.dot(g, onehot, preferred_element_type=F32, precision=lax.Precision.HIGHEST)
            finish()

    last = n - 1
    cur = lambda grp: pl.BlockSpec((None, TQ, D_ATT), lambda i: (grp, jnp.minimum(i, last), 0))
    prev = lambda grp: pl.BlockSpec((None, TQ, D_ATT), lambda i: (grp, jnp.maximum(jnp.minimum(i, last) - 1, 0), 0))
    tile = pl.BlockSpec((TQ, D_ATT), lambda i: (jnp.minimum(i, last), 0))
    return pl.pallas_call(
        body, name="attn_bwd", grid=(n + 1,),
        in_specs=[pl.BlockSpec((HEADS, DIAG), lambda i: (0, 0)),
                  cur(0), prev(1), cur(1), prev(2), cur(2), tile, tile,
                  pl.BlockSpec((TQ, HEADS), lambda i: (jnp.minimum(i, last), 0))] + [ANY] * npart,
        out_specs=[pl.BlockSpec((TQ, 3 * D_ATT), lambda i: (jnp.maximum(i - 1, 0), 0)),
                   pl.BlockSpec((HEADS, rel_pad), lambda i: (0, 0))] + [ANY] * npart,
        out_shape=[jax.ShapeDtypeStruct((s, 3 * D_ATT), BF16), jax.ShapeDtypeStruct((HEADS, rel_pad), F32)]
        + _scatter_out_shapes(parts),
        scratch_shapes=[pltpu.VMEM((HEADS // 2, 2 * QB, KW), F32), pltpu.VMEM((HEADS // 2, 2 * QB, KW), F32),
                        pltpu.VMEM((2, TQ, D_ATT), F32), pltpu.VMEM((2, TQ, D_ATT), F32),
                        pltpu.VMEM((TQ, D_ATT), BF16)] + _scatter_sems(npart),
        compiler_params=_params("arbitrary"),
    )(diag, proj, proj, proj, proj, proj, d_att, att, lse, *parts)


def _shift_down(a, k, halo):
    rolled = pltpu.roll(a, k, 0)
    row = lax.broadcasted_iota(jnp.int32, halo.shape, 0)
    first = jnp.where(row < k, pltpu.roll(halo, k, 0), rolled[0:SUBLANES, :])
    return jnp.concatenate([first, rolled[SUBLANES:, :]], axis=0)


def _shift_up(a, k, nxt):
    tm = a.shape[0]
    rolled = pltpu.roll(a, tm - k, 0)
    row = lax.broadcasted_iota(jnp.int32, nxt.shape, 0)
    last = jnp.where(row >= SUBLANES - k, pltpu.roll(nxt, SUBLANES - k, 0), rolled[tm - SUBLANES:, :])
    return jnp.concatenate([rolled[:tm - SUBLANES, :], last], axis=0)


def _sigmoid(v):
    return 0.5 * jnp.tanh(0.5 * v) + 0.5


def _mixer_mid(att, proj, x, tgt, w_att, w_conv, w_out, conv_w8, conv_b, fin_g):
    s, d = x.shape
    dc = D_ATT
    n = s // TM_MID
    tm = TM_MID
    n_shards = 4

    def body(att_ref, za_ref, gb_ref, gc_ref, u_ref, zc_ref, hgc_ref, hu_ref, gatt_ref, gconv_ref, x_ref, t_ref,
             watt_ref, wconv_ref, wout_ref, cw_ref, cb_ref, fg_ref,
             dpb_ref, do_ref, dx2_ref, gatt_o, gconv_o, gout_o, loss_o, gfn_o, gcb_o, gcw_o,
             acc_att, acc_conv, acc_out, carry, watt_t_ref, wconv_t_ref, wout_t_ref):
        i = pl.program_id(0)
        tile = n - 1 - i

        @pl.when(i == 0)
        def _():
            acc_att[...] = jnp.zeros_like(acc_att)
            acc_conv[...] = jnp.zeros_like(acc_conv)
            acc_out[...] = jnp.zeros_like(acc_out)
            carry[...] = jnp.zeros_like(carry)
            loss_o[...] = jnp.zeros_like(loss_o)
            gfn_o[...] = jnp.zeros_like(gfn_o)
            gcb_o[...] = jnp.zeros_like(gcb_o)
            gcw_o[...] = jnp.zeros_like(gcw_o)
            watt_t_ref[...] = watt_ref[...].T
            wconv_t_ref[...] = wconv_ref[...].T
            wout_t_ref[...] = wout_ref[...].T

        halves = [slice(hh * (tm // 2), (hh + 1) * (tm // 2)) for hh in range(2)]
        both = lambda fn: [fn(rows) for rows in halves]
        f32 = lambda ref, rows: ref[rows, :].astype(F32)

        gc = gc_ref[...].astype(F32)
        u = u_ref[...].astype(F32)
        cu = gc * u
        halo = jnp.where(tile > 0, hgc_ref[...].astype(F32) * hu_ref[...].astype(F32), 0.0)
        cu1 = _shift_down(cu, 1, halo)
        cu2 = _shift_down(cu, 2, halo)
        w0, w1, w2 = cw_ref[0:1, :], cw_ref[1:2, :], cw_ref[2:3, :]
        fg = fg_ref[...]

        def stage_a(rows):
            att_v, za, zc, gb = f32(att_ref, rows), f32(za_ref, rows), f32(zc_ref, rows), f32(gb_ref, rows)
            sa = _sigmoid(za)
            silu_a = za * sa
            vconv = w0 * cu2[rows, :] + w1 * cu1[rows, :] + w2 * cu[rows, :] + cb_ref[...]
            sc = _sigmoid(zc)
            silu_c = zc * sc
            return dict(att_v=att_v, za=za, zc=zc, gb=gb, sa=sa, silu_a=silu_a, vconv=vconv, sc=sc, silu_c=silu_c,
                        a_b=(att_v * silu_a).astype(BF16), c_b=(gb * vconv * silu_c).astype(BF16))

        st = both(stage_a)
        for t in st:
            t["y_att"] = _nn(t["a_b"], watt_ref[...])
            t["y_conv"] = _nn(t["c_b"], wconv_ref[...])
        for t, rows in zip(st, halves):
            gpair = lambda ref: jnp.concatenate([ref[0, rows, :], ref[1, rows, :]], axis=1).astype(F32)
            t["ga"] = _sigmoid(gpair(gatt_ref))
            t["gv"] = _sigmoid(gpair(gconv_ref))
            t["m_b"] = (t["ga"] * t["y_att"] + t["gv"] * t["y_conv"]).astype(BF16)
        for t in st:
            t["mo"] = _nn(t["m_b"], wout_ref[...])
        for t, rows in zip(st, halves):
            x2 = x_ref[rows, :] + t["mo"]
            r2 = lax.rsqrt(jnp.mean(x2 * x2, axis=-1, keepdims=True) + EPS)
            x2n = x2 * r2
            err = x2n * fg - t_ref[rows, :]
            loss_o[...] += jnp.sum(err * err, axis=0, keepdims=True) * (0.5 / d)
            dy = err * (1.0 / d)
            gfn_o[...] += jnp.sum(dy * x2n, axis=0, keepdims=True)
            dyn = dy * fg
            dx2 = r2 * (dyn - x2n * jnp.mean(dyn * x2n, axis=-1, keepdims=True))
            dx2_ref[rows, :] = dx2
            t["dx2_b"] = dx2.astype(BF16)
        for t in st:
            t["dm"] = _nn(t["dx2_b"], wout_t_ref[...])
        whole = lambda key: jnp.concatenate([st[0][key], st[1][key]], axis=0)
        acc_out[...] += _tn(whole("m_b"), whole("dx2_b"))
        for t, rows in zip(st, halves):
            dy_att = t["dm"] * t["ga"]
            dy_conv = t["dm"] * t["gv"]
            dpb_ref[rows, 5 * dc:5 * dc + d] = (dy_att * t["y_att"] * (1.0 - t["ga"])).astype(BF16)
            dpb_ref[rows, 5 * dc + d:5 * dc + 2 * d] = (dy_conv * t["y_conv"] * (1.0 - t["gv"])).astype(BF16)
            t["dya_b"] = dy_att.astype(BF16)
            t["dyc_b"] = dy_conv.astype(BF16)
        for t in st:
            t["da_in"] = _nn(t["dya_b"], watt_t_ref[...])
            t["dc_in"] = _nn(t["dyc_b"], wconv_t_ref[...])
        acc_att[...] += _tn(whole("a_b"), whole("dya_b"))
        acc_conv[...] += _tn(whole("c_b"), whole("dyc_b"))
        for t, rows in zip(st, halves):
            sa, za, sc, zc = t["sa"], t["za"], t["sc"], t["zc"]
            do_ref[rows, :] = (t["da_in"] * t["silu_a"]).astype(BF16)
            dpb_ref[rows, 0:dc] = (t["da_in"] * t["att_v"] * (sa * (1.0 + za * (1.0 - sa)))).astype(BF16)
            dpb_ref[rows, dc:2 * dc] = (t["dc_in"] * t["vconv"] * t["silu_c"]).astype(BF16)
            dgs = t["dc_in"] * t["gb"]
            t["dvc"] = dgs * t["silu_c"]
            dpb_ref[rows, 4 * dc:5 * dc] = (dgs * t["vconv"] * (sc * (1.0 + zc * (1.0 - sc)))).astype(BF16)
        dvc = whole("dvc")
        gcb_o[...] += jnp.sum(dvc, axis=0, keepdims=True)
        gcw_o[0:1, :] += jnp.sum(dvc * cu2, axis=0, keepdims=True)
        gcw_o[1:2, :] += jnp.sum(dvc * cu1, axis=0, keepdims=True)
        gcw_o[2:3, :] += jnp.sum(dvc * cu, axis=0, keepdims=True)
        nxt = carry[...]
        dcu = w2 * dvc + w1 * _shift_up(dvc, 1, nxt) + w0 * _shift_up(dvc, 2, nxt)
        carry[...] = dvc[0:SUBLANES, :]
        dpb_ref[:, 2 * dc:3 * dc] = (dcu * u).astype(BF16)
        dpb_ref[:, 3 * dc:4 * dc] = (dcu * gc).astype(BF16)

        @pl.when(i == n - 1)
        def _():
            for j in range(n_shards):
                gatt_o[j] = acc_att[:, j * (d // n_shards):(j + 1) * (d // n_shards)].astype(BF16)
                gconv_o[j] = acc_conv[:, j * (d // n_shards):(j + 1) * (d // n_shards)].astype(BF16)
                gout_o[j] = acc_out[j * (d // n_shards):(j + 1) * (d // n_shards), :].astype(BF16)

    rev = lambda width, col_blk: pl.BlockSpec((tm, width), lambda i: (n - 1 - i, col_blk))
    grp = lambda g: pl.BlockSpec((None, tm, dc), lambda i: (g, n - 1 - i, 0))
    grp2 = lambda g2: pl.BlockSpec((2, tm, dc), lambda i: (g2, n - 1 - i, 0))
    halo_spec = lambda g: pl.BlockSpec(
        (None, SUBLANES, dc), lambda i: (g, jnp.maximum((n - 1 - i) * (tm // SUBLANES) - 1, 0), 0))
    const = lambda shape: pl.BlockSpec(shape, lambda i: tuple(0 for _ in shape), pipeline_mode=pl.Buffered(1))
    q4 = d // n_shards
    return pl.pallas_call(
        body, name="mixer_mid", grid=(n,),
        in_specs=[rev(dc, 0), grp(3), grp(4), grp(5), grp(6), grp(7),
                  halo_spec(5), halo_spec(6), grp2(4), grp2(5), rev(d, 0), rev(d, 0),
                  const((dc, d)), const((dc, d)), const((d, d)),
                  const((SUBLANES, dc)), const((1, dc)), const((1, d))],
        out_specs=[rev(5 * dc + 2 * d, 0), rev(dc, 0), rev(d, 0),
                   const((n_shards, dc, q4)), const((n_shards, dc, q4)), const((n_shards, q4, d)),
                   const((1, d)), const((1, d)), const((1, dc)), const((SUBLANES, dc))],
        out_shape=[jax.ShapeDtypeStruct((s, 5 * dc + 2 * d), BF16), jax.ShapeDtypeStruct((s, dc), BF16),
                   jax.ShapeDtypeStruct((s, d), F32),
                   jax.ShapeDtypeStruct((n_shards, dc, q4), BF16), jax.ShapeDtypeStruct((n_shards, dc, q4), BF16),
                   jax.ShapeDtypeStruct((n_shards, q4, d), BF16),
                   jax.ShapeDtypeStruct((1, d), F32), jax.ShapeDtypeStruct((1, d), F32),
                   jax.ShapeDtypeStruct((1, dc), F32), jax.ShapeDtypeStruct((SUBLANES, dc), F32)],
        scratch_shapes=[pltpu.VMEM((dc, d), F32), pltpu.VMEM((dc, d), F32), pltpu.VMEM((d, d), F32),
                        pltpu.VMEM((SUBLANES, dc), F32),
                        pltpu.VMEM((d, dc), BF16), pltpu.VMEM((d, dc), BF16), pltpu.VMEM((d, d), BF16)],
        compiler_params=_params("arbitrary"),
    )(att, proj, proj, proj, proj, proj, proj, proj, proj, proj, x, tgt,
      w_att, w_conv, w_out, conv_w8, conv_b, fin_g)


def _in_proj_bwd_x(dqkv, dpb, w_in, x, dx2, g):
    s, d = x.shape
    tn = dqkv.shape[1]
    nb = dpb.shape[1] // tn
    n = s // TM_MM

    def body(*refs):
        dps, ws = refs[:nb + 1], refs[nb + 1:2 * nb + 2]
        x_ref, dx2_ref, g_ref, gx_ref, gng_ref = refs[2 * nb + 2:]
        i = pl.program_id(0)

        @pl.when(i == 0)
        def _():
            gng_ref[...] = jnp.zeros_like(gng_ref)

        dh = _nt(dps[0][...], ws[0][0])
        for j in range(1, nb + 1):
            dh = dh + _nt(dps[j][...], ws[j][0])
        xv = x_ref[...]
        r = lax.rsqrt(jnp.mean(xv * xv, axis=-1, keepdims=True) + EPS)
        xn = xv * r
        gng_ref[...] += jnp.sum(dh * xn, axis=0, keepdims=True)
        dhn = dh * g_ref[...]
        gx_ref[...] = dx2_ref[...] + r * (dhn - xn * jnp.mean(dhn * xn, axis=-1, keepdims=True))

    tile = lambda width, col_blk: pl.BlockSpec((TM_MM, width), lambda i: (i, col_blk))
    wspec = lambda blk: pl.BlockSpec((1, d, tn), lambda i: (blk, 0, 0), pipeline_mode=pl.Buffered(1))
    return pl.pallas_call(
        body, name="in_proj_bwd_x", grid=(n,),
        in_specs=[tile(tn, 0)] + [tile(tn, j) for j in range(nb)] + [wspec(j) for j in range(nb + 1)]
        + [tile(d, 0), tile(d, 0), pl.BlockSpec((1, d), lambda i: (0, 0))],
        out_specs=[tile(d, 0), pl.BlockSpec((1, d), lambda i: (0, 0))],
        out_shape=[jax.ShapeDtypeStruct((s, d), F32), jax.ShapeDtypeStruct((1, d), F32)],
        compiler_params=_params("arbitrary"),
    )(dqkv, *([dpb] * nb), *([w_in] * (nb + 1)), x, dx2, g)


def _in_proj_bwd_w(h, dqkv, dpb, order):
    s, d = h.shape
    tn = dqkv.shape[1]
    n = s // TM_BLK
    hr = d // 2
    settle = min(2, n - 1)

    def body(order_ref, h_ref, da_ref, db_ref, slots_ref, acc, sendbuf, pairbuf, chipbuf, psend, precv, send, recv, lsem):
        j, i = pl.program_id(0), pl.program_id(1)
        blk = order_ref[j]
        pos = _position()
        x, y, c = pos

        @pl.when(i == 0)
        def _():
            acc[...] = jnp.zeros_like(acc)

        @pl.when(blk == 0)
        def _():
            acc[...] += _tn(h_ref[...], da_ref[...])

        @pl.when(blk > 0)
        def _():
            acc[...] += _tn(h_ref[...], db_ref[...])

        def pair(step, half):
            return pltpu.make_async_remote_copy(
                src_ref=sendbuf.at[step, pl.ds(half * hr, hr), :], dst_ref=pairbuf.at[step],
                send_sem=psend.at[step], recv_sem=precv.at[step], device_id=(x, y, 1 - c), device_id_type=MESH)

        def ici(step):
            flip = OWNER_FLIPS[step]
            return pltpu.make_async_remote_copy(
                src_ref=chipbuf.at[step], dst_ref=slots_ref.at[flip], send_sem=send.at[step], recv_sem=recv.at[step],
                device_id=_peer(pos, 4 * (flip >> 1) + 2 * (flip & 1)), device_id_type=MESH)

        local = pltpu.make_async_copy(chipbuf.at[N_CHIPS - 1], slots_ref.at[0], lsem.at[0])

        def combine(step):
            pair(step, c).wait_recv()
            mine = sendbuf[step, pl.ds(c * hr, hr), :].astype(F32)
            chipbuf[step] = (mine + pairbuf[step].astype(F32)).astype(BF16)

        for step in range(N_CHIPS):
            @pl.when((j == step) & (i == n - 1))
            def _(step=step):
                sendbuf[step] = acc[...].astype(BF16)
                pair(step, 1 - c).start()

        for step in range(N_CHIPS - 1):
            @pl.when((j == step + 1) & (i == settle))
            def _(step=step):
                combine(step)
                ici(step).start()

        @pl.when((j == N_CHIPS - 1) & (i == n - 1))
        def _():
            combine(N_CHIPS - 1)
            local.start()
            for step in range(N_CHIPS - 1):
                ici(step).wait_recv()
            for step in range(N_CHIPS - 1):
                ici(step).wait_send()
            for step in range(N_CHIPS):
                pair(step, 1 - c).wait_send()
            local.wait()

    return pl.pallas_call(
        body, name="in_proj_bwd_w",
        grid_spec=pltpu.PrefetchScalarGridSpec(
            num_scalar_prefetch=1, grid=(N_CHIPS, n),
            in_specs=[pl.BlockSpec((TM_BLK, d), lambda j, i, order: (i, 0)),
                      pl.BlockSpec((TM_BLK, tn), lambda j, i, order: (jnp.where(order[j] == 0, i, 0), 0)),
                      pl.BlockSpec((TM_BLK, tn), lambda j, i, order: (jnp.where(order[j] == 0, 0, i),
                                                                     jnp.maximum(order[j] - 1, 0)))],
            out_specs=[ANY],
            scratch_shapes=[pltpu.VMEM((d, tn), F32), pltpu.VMEM((N_CHIPS, d, tn), BF16),
                            pltpu.VMEM((N_CHIPS, hr, tn), BF16), pltpu.VMEM((N_CHIPS, hr, tn), BF16),
                            pltpu.SemaphoreType.DMA((N_CHIPS,)), pltpu.SemaphoreType.DMA((N_CHIPS,)),
                            pltpu.SemaphoreType.DMA((N_CHIPS - 1,)), pltpu.SemaphoreType.DMA((N_CHIPS - 1,)),
                            pltpu.SemaphoreType.DMA((1,))]),
        out_shape=[jax.ShapeDtypeStruct((N_CHIPS, hr, tn), BF16)],
        compiler_params=_params("arbitrary", "arbitrary"),
    )(order, h, dqkv, dpb)[0]


LOSS_ROW = 6


def _adam_update(w, g, m, v):
    c1 = 1.0 / (1.0 - ADAM_B1 ** ADAM_STEP)
    c2 = 1.0 / (1.0 - ADAM_B2 ** ADAM_STEP)
    m2 = ADAM_B1 * m + (1.0 - ADAM_B1) * g
    v2 = ADAM_B2 * v + (1.0 - ADAM_B2) * (g * g)
    return -ADAM_LR * ((m2 * c1) / (jnp.sqrt(v2 * c2) + ADAM_EPS) + ADAM_WD * w), m2, v2


def _adamw_small(recv, params, moments_m, moments_v):
    k = recv.shape[0]
    n_par = len(params)
    cshard = params[3].shape[1]

    def body(*refs):
        r_ref = refs[0]
        ws, ms, vs = refs[1:1 + n_par], refs[1 + n_par:1 + 2 * n_par], refs[1 + 2 * n_par:1 + 3 * n_par]
        loss_ref = refs[1 + 3 * n_par]
        outs = refs[2 + 3 * n_par:]
        total = r_ref[0]
        for slot in range(1, k):
            total = total + r_ref[slot]
        loss_ref[...] = jnp.sum(total[LOSS_ROW:LOSS_ROW + 1, :], axis=1, keepdims=True)
        chip = 2 * lax.axis_index("x") + lax.axis_index("y")
        g_cw = jnp.zeros((3, cshard), F32)
        for sh in range(N_CHIPS):
            g_cw = g_cw + jnp.where(chip == sh, total[3:6, sh * cshard:(sh + 1) * cshard], 0.0)
        grads = [total[0:1, :], total[1:2, :], total[2:3, :ws[2].shape[1]], g_cw, total[8:16, :ws[4].shape[1]]]
        for p in range(n_par):
            delta, m2, v2 = _adam_update(ws[p][...], grads[p], ms[p][...], vs[p][...])
            for q, val in enumerate((grads[p], delta, m2, v2)):
                outs[4 * p + q][...] = val

    shapes = [jax.ShapeDtypeStruct((1, 1), F32)]
    for p in params:
        shapes += [jax.ShapeDtypeStruct(p.shape, F32)] * 4
    return pl.pallas_call(body, name="adamw_small", out_shape=shapes)(recv, *params, *moments_m, *moments_v)


def _adamw(w, g, m, v, name, rows_per_step):
    r, c = w.shape

    def body(w_ref, g_ref, m_ref, v_ref, go_ref, d_ref, mo_ref, vo_ref):
        gv = g_ref[...]
        go_ref[...] = gv
        d_ref[...], mo_ref[...], vo_ref[...] = _adam_update(w_ref[...], gv, m_ref[...], v_ref[...])

    spec = pl.BlockSpec((rows_per_step, c), lambda i: (i, 0))
    shape = jax.ShapeDtypeStruct((r, c), F32)
    return pl.pallas_call(
        body, name=name, grid=(r // rows_per_step,),
        in_specs=[spec] * 4, out_specs=[spec] * 4, out_shape=[shape] * 4,
        compiler_params=_params("parallel"),
    )(w, g, m, v)


def _adamw_group(ws, gs, ms, vs, name):
    k = len(ws)

    def body(*refs):
        ins, outs = refs[:4 * k], refs[4 * k:]
        for j in range(k):
            gv = ins[k + j][...]
            outs[4 * j][...] = gv
            outs[4 * j + 1][...], outs[4 * j + 2][...], outs[4 * j + 3][...] = _adam_update(
                ins[j][...], gv, ins[2 * k + j][...], ins[3 * k + j][...])

    shapes = []
    for w in ws:
        shapes += [jax.ShapeDtypeStruct(w.shape, F32)] * 4
    out = pl.pallas_call(body, name=name, out_shape=shapes,
                         compiler_params=pltpu.CompilerParams(vmem_limit_bytes=VMEM_LIMIT))(*ws, *gs, *ms, *vs)
    return [tuple(out[4 * j:4 * j + 4]) for j in range(k)]


ANY = pl.BlockSpec(memory_space=pl.ANY)
N_CHIPS = 4
N_DEV = 8
OWNER_FLIPS = (3, 1, 2, 0)


def _position():
    return lax.axis_index("x"), lax.axis_index("y"), lax.axis_index("c")


def _gather_out_shapes(shards, kinds, cw8):
    full = [(a.shape[0], a.shape[1] * N_CHIPS) if k == "cols" else (a.shape[0] * N_CHIPS, a.shape[1])
            for a, k in zip(shards, kinds)]
    return [jax.ShapeDtypeStruct(f, BF16) for f in full] + [
        jax.ShapeDtypeStruct((N_CHIPS,) + cw8.shape, cw8.dtype)]


def _gather_sems(nw):
    return [pltpu.SemaphoreType.DMA((3, nw)), pltpu.SemaphoreType.DMA((3, nw)),
            pltpu.SemaphoreType.DMA((3, nw)), pltpu.SemaphoreType.DMA((3, nw)),
            pltpu.SemaphoreType.DMA((3,)), pltpu.SemaphoreType.DMA((3,)), pltpu.SemaphoreType.DMA((nw + 1,))]


def _gather_plan(kinds, srcs, dsts, cw, cw_all, send1, recv1, send2, recv2, ssend, srecv, lsem):
    nw = len(srcs)
    x, y, c = _position()
    mine = 2 * x + y
    chips = [(x, 1 - y), (1 - x, y), (1 - x, 1 - y)]

    def window(w, shard, half):
        r, cc = srcs[w].shape
        hr = r // 2
        if kinds[w] == "cols":
            rows = pl.ds(0, r) if half is None else pl.ds(half * hr, hr)
            return dsts[w].at[rows, pl.ds(shard * cc, cc)]
        rows = pl.ds(shard * r, r) if half is None else pl.ds(shard * r + half * hr, hr)
        return dsts[w].at[rows, :]

    def my_half(w):
        hr = srcs[w].shape[0] // 2
        return srcs[w].at[pl.ds(c * hr, hr), :]

    def local():
        return [pltpu.make_async_copy(srcs[w], window(w, mine, None), lsem.at[w]) for w in range(nw)] + [
            pltpu.make_async_copy(cw, cw_all.at[mine], lsem.at[nw])]

    def ici(k, w, shard):
        kx, ky = chips[k]
        return pltpu.make_async_remote_copy(
            src_ref=my_half(w), dst_ref=window(w, shard, c), send_sem=send1.at[k, w], recv_sem=recv1.at[k, w],
            device_id=(kx, ky, c), device_id_type=MESH)

    def d2d(k, w, shard, half):
        return pltpu.make_async_remote_copy(
            src_ref=window(w, shard, half), dst_ref=window(w, shard, half),
            send_sem=send2.at[k, w], recv_sem=recv2.at[k, w], device_id=(x, y, 1 - c), device_id_type=MESH)

    def small(k, shard):
        kx, ky = chips[k]
        return pltpu.make_async_remote_copy(
            src_ref=cw, dst_ref=cw_all.at[shard], send_sem=ssend.at[k], recv_sem=srecv.at[k],
            device_id=(kx, ky, c), device_id_type=MESH)

    def theirs(k):
        kx, ky = chips[k]
        return 2 * kx + ky

    def start():
        for cp in local():
            cp.start()
        for k in range(3):
            for w in range(nw):
                ici(k, w, mine).start()
            small(k, mine).start()

    def forward():
        for k in range(3):
            for w in range(nw):
                ici(k, w, theirs(k)).wait_recv()
                d2d(k, w, theirs(k), c).start()

    def finish():
        for k in range(3):
            for w in range(nw):
                d2d(k, w, theirs(k), 1 - c).wait_recv()
            small(k, theirs(k)).wait_recv()
        for k in range(3):
            for w in range(nw):
                ici(k, w, mine).wait_send()
                d2d(k, w, theirs(k), c).wait_send()
            small(k, mine).wait_send()
        for cp in local():
            cp.wait()

    return start, forward, finish


def _scatter_out_shapes(parts):
    return [jax.ShapeDtypeStruct((N_DEV, p.shape[1] // 2, p.shape[2]), p.dtype) for p in parts]


def _scatter_sems(nw):
    return [pltpu.SemaphoreType.DMA((N_DEV, nw)), pltpu.SemaphoreType.DMA((N_DEV, nw)), pltpu.SemaphoreType.DMA((nw,))]


def _peer(pos, k):
    x, y, c = pos
    return ((1 - x) if k & 4 else x, (1 - y) if k & 2 else y, (1 - c) if k & 1 else c)


def _scatter_plan(srcs, dsts, send, recv, lsem):
    nw = len(srcs)
    pos = _position()

    def piece(w, k):
        px, py, pc = _peer(pos, k)
        hr = srcs[w].shape[1] // 2
        return srcs[w].at[2 * px + py, pl.ds(pc * hr, hr), :]

    def remote(w, k):
        return pltpu.make_async_remote_copy(
            src_ref=piece(w, k), dst_ref=dsts[w].at[k], send_sem=send.at[k, w], recv_sem=recv.at[k, w],
            device_id=_peer(pos, k), device_id_type=MESH)

    def local(w):
        return pltpu.make_async_copy(piece(w, 0), dsts[w].at[0], lsem.at[w])

    def start():
        for w in range(nw):
            local(w).start()
        for k in range(1, N_DEV):
            for w in range(nw):
                remote(w, k).start()

    def finish():
        for k in range(1, N_DEV):
            for w in range(nw):
                remote(w, k).wait_recv()
        for k in range(1, N_DEV):
            for w in range(nw):
                remote(w, k).wait_send()
        for w in range(nw):
            local(w).wait()

    return start, finish


def _reduce_pair(slots, small):
    nw = len(slots)

    def body(*refs):
        srcs, sm = refs[:nw], refs[nw]
        dsts, sm_all = refs[nw + 1:2 * nw + 1], refs[2 * nw + 1]
        halves = refs[2 * nw + 2:3 * nw + 2]
        send, recv, ssend, srecv, lsem = refs[3 * nw + 2:]
        pos = _position()
        x, y, c = pos
        me = 4 * x + 2 * y + c

        def rows(w, half):
            hr = halves[w].shape[0]
            return dsts[w].at[pl.ds(half * hr, hr), :]

        def remote(w, half):
            return pltpu.make_async_remote_copy(
                src_ref=halves[w], dst_ref=rows(w, half), send_sem=send.at[w], recv_sem=recv.at[w],
                device_id=(x, y, 1 - c), device_id_type=MESH)

        def bcast(k, slot):
            return pltpu.make_async_remote_copy(
                src_ref=sm, dst_ref=sm_all.at[slot], send_sem=ssend.at[k], recv_sem=srecv.at[k],
                device_id=_peer(pos, k), device_id_type=MESH)

        small_copies = [bcast(k, me) for k in range(1, N_DEV)]
        own_small = pltpu.make_async_copy(sm, sm_all.at[me], lsem.at[nw])
        for cp in small_copies + [own_small]:
            cp.start()
        big = []
        for w in range(nw):
            total = srcs[w][0].astype(F32)
            for k in range(1, srcs[w].shape[0]):
                total = total + srcs[w][k].astype(F32)
            halves[w][...] = total
            big += [remote(w, c), pltpu.make_async_copy(halves[w], rows(w, c), lsem.at[w])]
            big[-2].start()
            big[-1].start()
        for w in range(nw):
            remote(w, 1 - c).wait_recv()
        for k in range(1, N_DEV):
            px, py, pc = _peer(pos, k)
            bcast(k, 4 * px + 2 * py + pc).wait_recv()
        for w in range(nw):
            big[2 * w].wait_send()
            big[2 * w + 1].wait()
        for cp in small_copies:
            cp.wait_send()
        own_small.wait()

    vmem = pl.BlockSpec(memory_space=pltpu.VMEM)
    half_shapes = [(sl.shape[1], sl.shape[2]) for sl in slots]
    return pl.pallas_call(
        body, name="reduce_pair",
        in_specs=[vmem] * (nw + 1), out_specs=[ANY] * (nw + 1),
        out_shape=[jax.ShapeDtypeStruct((2 * r, cc), F32) for r, cc in half_shapes]
        + [jax.ShapeDtypeStruct((N_DEV,) + small.shape, small.dtype)],
        scratch_shapes=[pltpu.VMEM(hs, F32) for hs in half_shapes]
        + [pltpu.SemaphoreType.DMA((nw,)), pltpu.SemaphoreType.DMA((nw,)),
           pltpu.SemaphoreType.DMA((N_DEV,)), pltpu.SemaphoreType.DMA((N_DEV,)),
           pltpu.SemaphoreType.DMA((nw + 1,))],
        compiler_params=pltpu.CompilerParams(vmem_limit_bytes=VMEM_LIMIT),
    )(*slots, small)


def _pad_to(a, rows, cols):
    return jnp.pad(a, ((0, rows - a.shape[0]), (0, cols - a.shape[1])))


def _pack_small(norm_g, fin_g, conv_b, conv_w, loss_vec, rel):
    rows = [_pad_to(norm_g, 1, SMALL_COLS), _pad_to(fin_g, 1, SMALL_COLS), _pad_to(conv_b, 1, SMALL_COLS),
            _pad_to(conv_w, 3, SMALL_COLS), _pad_to(loss_vec, 2, SMALL_COLS), _pad_to(rel, HEADS, SMALL_COLS)]
    return jnp.concatenate(rows, axis=0)


def kernel(x, norm_g, w_in, rel_bias, w_att_out, conv_w, conv_b, w_conv_out, w_out, final_norm_g, loss_target, m_norm_g, m_w_in, m_rel_bias, m_w_att_out, m_conv_w, m_conv_b, m_w_conv_out, m_w_out, m_final_norm_g, v_norm_g, v_w_in, v_rel_bias, v_w_att_out, v_conv_w, v_conv_b, v_w_conv_out, v_w_out, v_final_norm_g):
    xs, tgt = x[0], loss_target[0]
    cshard = conv_w.shape[2]
    chip = 2 * lax.axis_index("x") + lax.axis_index("y")

    shards = [w_in[0], w_att_out[0], w_conv_out[0], w_out[0]]
    cw8 = _pad_to(conv_w[0], SUBLANES, cshard)
    flips = jnp.arange(N_CHIPS, dtype=jnp.int32)
    own_first = jnp.bitwise_xor(chip, flips)
    own_last = jnp.bitwise_xor(chip, jnp.asarray(OWNER_FLIPS, jnp.int32))

    proj, h, wb_in = _in_proj_gather(xs, norm_g, shards[0], own_first)
    diag = jnp.take(rel_bias[0], _diag_rel_index(), axis=1)
    att, lse, wb_att, wb_conv, wb_out, cw_all = _attn_fwd(diag, proj, shards[1:], ["cols", "cols", "rows"], cw8)
    conv_w_full = jnp.transpose(cw_all, (1, 0, 2)).reshape(SUBLANES, N_CHIPS * cshard)
    (dpb, d_att, dx2, g_att_p, g_conv_p, g_out_p, loss_vec, g_fin, g_cb, g_cw) = _mixer_mid(
        att, proj, xs, tgt, wb_att, wb_conv, wb_out, conv_w_full, conv_b,
        final_norm_g[None, :])
    dqkv, g_rel, r_att, r_conv, r_out = _attn_bwd(diag, proj, d_att, att, lse, [g_att_p, g_conv_p, g_out_p])
    grad_x, g_norm = _in_proj_bwd_x(dqkv, dpb, wb_in, xs, dx2, norm_g)
    r_in = _in_proj_bwd_w(h, dqkv, dpb, own_last)

    small = _pack_small(g_norm, g_fin, g_cb, g_cw[0:3], loss_vec, g_rel)
    gw_in, gw_att, gw_conv, gw_out, r_small = _reduce_pair([r_in, r_att, r_conv, r_out], small)
    small_out = _adamw_small(
        r_small,
        [norm_g, final_norm_g[None, :], conv_b, conv_w[0], rel_bias[0]],
        [m_norm_g, m_final_norm_g[None, :], m_conv_b, m_conv_w[0], m_rel_bias[0]],
        [v_norm_g, v_final_norm_g[None, :], v_conv_b, v_conv_w[0], v_rel_bias[0]])
    loss = small_out[0][0, 0]
    small_names = ["norm_g", "final_norm_g", "conv_b", "conv_w", "rel_bias"]
    fix = {"norm_g": lambda a: a, "final_norm_g": lambda a: a[0], "conv_b": lambda a: a,
           "conv_w": lambda a: a[None], "rel_bias": lambda a: a[None]}
    small_res = {name: [fix[name](small_out[1 + 4 * p + q]) for q in range(4)] for p, name in enumerate(small_names)}

    big = {"w_in": _adamw(w_in[0], gw_in, m_w_in[0], v_w_in[0], "adamw_w_in", 128)}
    names = ["w_att_out", "w_conv_out", "w_out"]
    group = _adamw_group([w_att_out[0], w_conv_out[0], w_out[0]], [gw_att, gw_conv, gw_out],
                         [m_w_att_out[0], m_w_conv_out[0], m_w_out[0]],
                         [v_w_att_out[0], v_w_conv_out[0], v_w_out[0]], "adamw_small_matrices")
    big.update(zip(names, group))
    big = {name: tuple(a[None] for a in four) for name, four in big.items()}

    order = ["norm_g", "w_in", "rel_bias", "w_att_out", "conv_w", "conv_b", "w_conv_out", "w_out", "final_norm_g"]
    outs = [loss, grad_x[None]]
    for which in range(4):
        for name in order:
            outs.append(big[name][which] if name in big else small_res[name][which])
    return tuple(outs)
```

```python
import numpy as np
import jax
import jax.numpy as jnp
from jax import lax
from jax.experimental import pallas as pl
from jax.experimental.pallas import tpu as pltpu

F32 = jnp.float32
BF16 = jnp.bfloat16
MESH = pl.DeviceIdType.MESH

CHUNK = 64
N_LEFT = 8
HEADS = 8
HEAD_DIM = 64
D_ATT = HEADS * HEAD_DIM
MAX_REL = 128
N_REL = 2 * MAX_REL + 1
EPS = 1e-6
NEG_BIG = -1e30
ADAM_LR, ADAM_B1, ADAM_B2, ADAM_EPS, ADAM_WD, ADAM_STEP = 0.001, 0.9, 0.999, 1e-08, 0.01, 10

LANES = 128
SUBLANES = 8
VMEM_LIMIT = 56 * 1024 * 1024

QB = 2 * CHUNK
KW = N_LEFT * CHUNK + QB
DIAG = KW + QB
TQ = N_LEFT * CHUNK
TM_MID = 256
TM_MM = 512
TM_BLK = 1024
SMALL_ROWS, SMALL_COLS = 16, 1024


def _params(*sem):
    return pltpu.CompilerParams(dimension_semantics=sem, vmem_limit_bytes=VMEM_LIMIT)


def _nt(a, b):
    return lax.dot_general(a, b, (((1,), (1,)), ((), ())), preferred_element_type=F32)


def _tn(a, b):
    return lax.dot_general(a, b, (((0,), (0,)), ((), ())), preferred_element_type=F32)


def _nn(a, b):
    return jnp.dot(a, b, preferred_element_type=F32)


def _diag_rel_index():
    d = np.arange(DIAG)
    diff = np.where(d < KW, d, d - DIAG)
    rel = N_LEFT * CHUNK - diff
    return np.clip(rel, -MAX_REL, MAX_REL) + MAX_REL


def _build_bias(diag_ref, bias_scr):
    r = lax.broadcasted_iota(jnp.int32, (QB, KW), 0) // CHUNK
    s = lax.broadcasted_iota(jnp.int32, (QB, KW), 1) // CHUNK
    allowed = (s >= r) & (s <= r + N_LEFT)
    for h in range(HEADS):
        row = jnp.broadcast_to(diag_ref[h:h + 1, :], (QB, DIAG))
        t = pltpu.roll(row, 0, 1, stride=1, stride_axis=0)
        bias_scr[h // 2, (h % 2) * QB:(h % 2 + 1) * QB, :] = jnp.where(allowed, t[:, :KW], NEG_BIG)


def _stack_heads(a, lane_hi):
    zero = jnp.zeros_like(a)
    return jnp.concatenate([jnp.where(lane_hi, zero, a), jnp.where(lane_hi, a, zero)], axis=0)


def _in_proj_gather(x, g, shard, order):
    s, d = x.shape
    tn = shard.shape[1]
    n = s // TM_BLK
    hr = d // 2

    def body(order_ref, x_ref, g_ref, shard32_ref, proj_ref, h_ref, wfull_ref, shard_ref, hbuf, wbuf,
             send1, recv1, send2, recv2, lsem):
        del order_ref
        j, i = pl.program_id(0), pl.program_id(1)
        x, y, c = _position()
        mine = 2 * x + y
        chips = [(x, 1 - y), (1 - x, y), (1 - x, 1 - y)]

        def theirs(k):
            return 2 * chips[k][0] + chips[k][1]

        def half_rows(half):
            return pl.ds(half * hr, hr)

        def landing(k, shard_index, half):
            if k < 2:
                return wbuf.at[k, half_rows(half), :]
            return wfull_ref.at[shard_index, half_rows(half), :]

        def ici(k, shard_index):
            return pltpu.make_async_remote_copy(
                src_ref=shard_ref.at[half_rows(c), :], dst_ref=landing(k, shard_index, c),
                send_sem=send1.at[k], recv_sem=recv1.at[k], device_id=(*chips[k], c), device_id_type=MESH)

        def d2d(k, half):
            return pltpu.make_async_remote_copy(
                src_ref=wbuf.at[k % 2, half_rows(half), :], dst_ref=landing(k, theirs(k), half),
                send_sem=send2.at[k], recv_sem=recv2.at[k], device_id=(x, y, 1 - c), device_id_type=MESH)

        def load(k, half, sem):
            return pltpu.make_async_copy(wfull_ref.at[theirs(k), half_rows(half), :],
                                         wbuf.at[k % 2, half_rows(half), :], lsem.at[sem])

        def keep(k):
            return pltpu.make_async_copy(wbuf.at[k], wfull_ref.at[theirs(k)], lsem.at[3 + k])

        own = pltpu.make_async_copy(shard_ref, wfull_ref.at[mine], lsem.at[0])

        def put_proj(block):
            for grp in range(tn // D_ATT):
                proj_ref[grp] = block[:, grp * D_ATT:(grp + 1) * D_ATT].astype(BF16)

        @pl.when((j == 0) & (i == 0))
        def _():
            shard_ref[...] = shard32_ref[...].astype(BF16)
            own.start()
            ici(0, mine).start()
            ici(1, mine).start()

        for k in range(2):
            first = n - 1 if k == 0 else min(n // 2, n - 1)

            @pl.when((j == k) & (i == first))
            def _(k=k):
                if k == 0:
                    ici(0, mine).wait_send()
                    ici(1, mine).wait_send()
                    ici(2, mine).start()
                ici(k, theirs(k)).wait_recv()
                d2d(k, c).start()

            then = (1, 0) if k == 0 else (1, min(first + 1, n - 1))

            @pl.when((j == then[0]) & (i == then[1]))
            def _(k=k):
                d2d(k, 1 - c).wait_recv()
                keep(k).start()

        first = min(n // 2, n - 1)

        @pl.when((j == 2) & (i == first))
        def _():
            d2d(0, c).wait_send()
            keep(0).wait()
            ici(2, theirs(2)).wait_recv()
            load(2, c, 1).start()

        @pl.when((j == 2) & (i == min(first + 1, n - 1)))
        def _():
            load(2, c, 1).wait()
            d2d(2, c).start()

        @pl.when((j == 2) & (i == min(first + 2, n - 1)))
        def _():
            d2d(2, 1 - c).wait_recv()
            load(2, 1 - c, 2).start()

        @pl.when((j == 3) & (i == 0))
        def _():
            load(2, 1 - c, 2).wait()

        @pl.when(j == 0)
        def _():
            xv = x_ref[...]
            r = lax.rsqrt(jnp.mean(xv * xv, axis=-1, keepdims=True) + EPS)
            hv = ((xv * r) * g_ref[...]).astype(BF16)
            hbuf[i] = hv
            h_ref[...] = hv
            put_proj(_nn(hv, shard_ref[...]))

        for k in range(3):
            @pl.when(j == k + 1)
            def _(k=k):
                put_proj(_nn(hbuf[i], wbuf[k % 2]))

        @pl.when((j == 3) & (i == n - 1))
        def _():
            ici(2, mine).wait_send()
            d2d(1, c).wait_send()
            d2d(2, c).wait_send()
            own.wait()
            keep(1).wait()

    return pl.pallas_call(
        body, name="in_proj_gather",
        grid_spec=pltpu.PrefetchScalarGridSpec(
            num_scalar_prefetch=1, grid=(N_CHIPS, n),
            in_specs=[pl.BlockSpec((TM_BLK, d), lambda j, i, order: (jnp.where(j == 0, i, n - 1), 0)),
                      pl.BlockSpec((1, d), lambda j, i, order: (0, 0)), pl.BlockSpec(memory_space=pltpu.VMEM)],
            out_specs=[pl.BlockSpec((tn // D_ATT, TM_BLK, D_ATT), lambda j, i, order: (order[j], i, 0)),
                       pl.BlockSpec((TM_BLK, d), lambda j, i, order: (jnp.where(j == 0, i, n - 1), 0)), ANY],
            scratch_shapes=[pltpu.VMEM((d, tn), BF16), pltpu.VMEM((n, TM_BLK, d), BF16), pltpu.VMEM((2, d, tn), BF16),
                            pltpu.SemaphoreType.DMA((3,)), pltpu.SemaphoreType.DMA((3,)),
                            pltpu.SemaphoreType.DMA((3,)), pltpu.SemaphoreType.DMA((3,)), pltpu.SemaphoreType.DMA((5,))]),
        out_shape=[jax.ShapeDtypeStruct((N_CHIPS * tn // D_ATT, s, D_ATT), BF16), jax.ShapeDtypeStruct((s, d), BF16),
                   jax.ShapeDtypeStruct((N_CHIPS, d, tn), BF16)],
        compiler_params=_params("arbitrary", "arbitrary"),
    )(order, x, g, shard)


def _attn_fwd(diag, proj, shards, kinds, cw8):
    s = proj.shape[1]
    n = s // TQ
    nw = len(shards)
    scale = HEAD_DIM ** -0.5

    def body(*refs):
        diag_ref, q_ref, kp_ref, kc_ref, vp_ref, vc_ref = refs[:6]
        srcs, cw = refs[6:6 + nw], refs[6 + nw]
        o_ref, lse_ref = refs[7 + nw:9 + nw]
        dsts, cw_all = refs[9 + nw:9 + 2 * nw], refs[9 + 2 * nw]
        bias_scr = refs[10 + 2 * nw]
        casts = refs[11 + 2 * nw:11 + 3 * nw]
        start, forward, finish = _gather_plan(kinds, casts, dsts, cw, cw_all, *refs[11 + 3 * nw:])
        i = pl.program_id(0)

        @pl.when(i == 0)
        def _():
            for w in range(nw):
                casts[w][...] = srcs[w][...].astype(BF16)
            start()
            _build_bias(diag_ref, bias_scr)

        @pl.when(i == n // 2)
        def _():
            forward()

        @pl.when(i == n - 1)
        def _():
            finish()

        lane_hi = lax.broadcasted_iota(jnp.int32, (QB, LANES), 1) >= HEAD_DIM

        def block(b, first_tile):
            r0, n_prev = b * QB, TQ - b * QB
            n_cur = KW - n_prev
            pairs = range(HEADS // 2)
            lanes_of = [slice(LANES * p, LANES * (p + 1)) for p in pairs]
            scores = []
            for p in pairs:
                lanes = lanes_of[p]
                q2 = _stack_heads(q_ref[r0:r0 + QB, lanes] * scale, lane_hi)
                s_cur = _nt(q2, kc_ref[0:n_cur, lanes]) + bias_scr[p, :, n_prev:KW]
                if first_tile:
                    scores.append(s_cur)
                else:
                    scores.append(jnp.concatenate(
                        [_nt(q2, kp_ref[r0:TQ, lanes]) + bias_scr[p, :, 0:n_prev], s_cur], axis=1))
            probs = []
            for p in pairs:
                sc = scores[p]
                m = jnp.max(sc, axis=1, keepdims=True)
                pe = jnp.exp(sc - m)
                l = jnp.sum(pe, axis=1, keepdims=True)
                probs.append((pe.astype(BF16), l, m))
            for p in pairs:
                lanes = lanes_of[p]
                pb, l, m = probs[p]
                if first_tile:
                    o2 = _nn(pb, vc_ref[0:n_cur, lanes]) / l
                else:
                    o2 = (_nn(pb[:, 0:n_prev], vp_ref[r0:TQ, lanes]) + _nn(pb[:, n_prev:KW], vc_ref[0:n_cur, lanes])) / l
                lse2 = m + jnp.log(l)
                lse_ref[r0:r0 + QB, 2 * p:2 * p + 1] = lse2[0:QB, :]
                lse_ref[r0:r0 + QB, 2 * p + 1:2 * p + 2] = lse2[QB:2 * QB, :]
                o_ref[r0:r0 + QB, lanes] = jnp.where(lane_hi, o2[QB:2 * QB, :], o2[0:QB, :]).astype(BF16)

        @pl.when(i == 0)
        def _():
            for b in range(TQ // QB):
                block(b, True)

        @pl.when(i > 0)
        def _():
            for b in range(TQ // QB):
                block(b, False)

    blk = lambda grp, prev: pl.BlockSpec(
        (None, TQ, D_ATT), (lambda i: (grp, jnp.maximum(i - 1, 0), 0)) if prev else (lambda i: (grp, i, 0)))
    vmem = pl.BlockSpec(memory_space=pltpu.VMEM)
    return pl.pallas_call(
        body, name="attn_fwd", grid=(n,),
        in_specs=[pl.BlockSpec((HEADS, DIAG), lambda i: (0, 0)),
                  blk(0, False), blk(1, True), blk(1, False), blk(2, True), blk(2, False)] + [vmem] * (nw + 1),
        out_specs=[pl.BlockSpec((TQ, D_ATT), lambda i: (i, 0)), pl.BlockSpec((TQ, HEADS), lambda i: (i, 0))]
        + [ANY] * (nw + 1),
        out_shape=[jax.ShapeDtypeStruct((s, D_ATT), BF16), jax.ShapeDtypeStruct((s, HEADS), F32)]
        + _gather_out_shapes(shards, kinds, cw8),
        scratch_shapes=[pltpu.VMEM((HEADS // 2, 2 * QB, KW), F32)] + [pltpu.VMEM(a.shape, BF16) for a in shards]
        + _gather_sems(nw),
        compiler_params=_params("arbitrary"),
    )(diag, proj, proj, proj, proj, proj, *shards, cw8)


def _attn_bwd(diag, proj, d_att, att, lse, parts):
    s = proj.shape[1]
    n = s // TQ
    npart = len(parts)
    scale = HEAD_DIM ** -0.5
    rel_pad = 3 * LANES

    def body(*refs):
        diag_ref, q_ref, kp_ref, kc_ref, vp_ref, vc_ref, do_ref, o_ref, lse_ref = refs[:9]
        part_refs = refs[9:9 + npart]
        dqkv_ref, dbias_ref = refs[9 + npart:11 + npart]
        slot_refs = refs[11 + npart:11 + 2 * npart]
        bias_scr, dbias_acc, dk_acc, dv_acc, dq_scr = refs[11 + 2 * npart:16 + 2 * npart]
        start, finish = _scatter_plan(part_refs, slot_refs, *refs[16 + 2 * npart:])
        i = pl.program_id(0)
        cur, prv = i % 2, 1 - i % 2

        @pl.when(i == 0)
        def _():
            start()
            _build_bias(diag_ref, bias_scr)
            dbias_acc[...] = jnp.zeros_like(dbias_acc)
            dk_acc[...] = jnp.zeros_like(dk_acc)
            dv_acc[...] = jnp.zeros_like(dv_acc)

        @pl.when(i > 0)
        def _():
            dqkv_ref[:, 0:D_ATT] = dq_scr[...]
            dk_acc[cur] = jnp.zeros((TQ, D_ATT), F32)
            dv_acc[cur] = jnp.zeros((TQ, D_ATT), F32)

        lane_hi = lax.broadcasted_iota(jnp.int32, (QB, LANES), 1) >= HEAD_DIM
        col = lax.broadcasted_iota(jnp.int32, (2 * QB, KW), 1)

        def make_block(first_tile):
            def block(b):
                r0, n_prev = b * QB, TQ - b * QB
                n_cur = KW - n_prev
                for p in range(HEADS // 2):
                    lanes = slice(LANES * p, LANES * (p + 1))
                    q2 = _stack_heads(q_ref[r0:r0 + QB, lanes] * scale, lane_hi)
                    kw = jnp.concatenate([kp_ref[r0:TQ, lanes], kc_ref[0:n_cur, lanes]], axis=0)
                    vw = jnp.concatenate([vp_ref[r0:TQ, lanes], vc_ref[0:n_cur, lanes]], axis=0)
                    dop = do_ref[r0:r0 + QB, lanes]
                    do2 = _stack_heads(dop, lane_hi)
                    prod = dop.astype(F32) * o_ref[r0:r0 + QB, lanes].astype(F32)
                    delta2 = jnp.concatenate(
                        [jnp.sum(jnp.where(lane_hi, 0.0, prod), axis=1, keepdims=True),
                         jnp.sum(jnp.where(lane_hi, prod, 0.0), axis=1, keepdims=True)], axis=0)
                    lse2 = jnp.concatenate([lse_ref[r0:r0 + QB, 2 * p:2 * p + 1],
                                            lse_ref[r0:r0 + QB, 2 * p + 1:2 * p + 2]], axis=0)
                    sc = _nt(q2, kw) + bias_scr[p]
                    if first_tile:
                        sc = jnp.where(col >= TQ - r0, sc, NEG_BIG)
                    pr = jnp.exp(sc - lse2)
                    ds = pr * (_nt(do2, vw) - delta2)
                    dbias_acc[p] += ds
                    dsb = ds.astype(BF16)
                    dv_w = _tn(pr.astype(BF16), do2)
                    dk_w = _tn(dsb, q2)
                    dv_acc[prv, r0:TQ, lanes] += dv_w[0:n_prev, :]
                    dv_acc[cur, 0:n_cur, lanes] += dv_w[n_prev:KW, :]
                    dk_acc[prv, r0:TQ, lanes] += dk_w[0:n_prev, :]
                    dk_acc[cur, 0:n_cur, lanes] += dk_w[n_prev:KW, :]
                    dq2 = _nn(dsb, kw)
                    dq = jnp.where(lane_hi, dq2[QB:2 * QB, :], dq2[0:QB, :]) * scale
                    dq_scr[r0:r0 + QB, lanes] = dq.astype(BF16)
            return block

        @pl.when(i == 0)
        def _():
            for b in range(TQ // QB):
                make_block(True)(b)

        @pl.when((i > 0) & (i < n))
        def _():
            for b in range(TQ // QB):
                make_block(False)(b)

        @pl.when(i > 0)
        def _():
            dqkv_ref[:, D_ATT:2 * D_ATT] = dk_acc[prv].astype(BF16)
            dqkv_ref[:, 2 * D_ATT:3 * D_ATT] = dv_acc[prv].astype(BF16)

        @pl.when(i == n)
        def _():
            d_iota = lax.broadcasted_iota(jnp.int32, (DIAG, rel_pad), 0)
            n_iota = lax.broadcasted_iota(jnp.int32, (DIAG, rel_pad), 1)
            diff = jnp.where(d_iota < KW, d_iota, d_iota - DIAG)
            idx = jnp.clip(N_LEFT * CHUNK - diff, -MAX_REL, MAX_REL) + MAX_REL
            onehot = (idx == n_iota).astype(F32)
            rows = []
            for hd in range(HEADS):
                acc = dbias_acc[hd // 2, (hd % 2) * QB:(hd % 2 + 1) * QB, :]
                a = jnp.concatenate([acc, jnp.zeros((QB, DIAG - KW), F32)], axis=1)
                g8 = a[0:SUBLANES, :]
                for blk in range(1, QB // SUBLANES):
                    g8 = g8 + pltpu.roll(a[blk * SUBLANES:(blk + 1) * SUBLANES, :], DIAG - blk * SUBLANES, 1)
                g1 = g8[0:1, :]
                for r in range(1, SUBLANES):
                    g1 = g1 + pltpu.roll(g8[r:r + 1, :], DIAG - r, 1)
                rows.append(g1)
            g = jnp.concatenate(rows, axis=0)
            dbias_ref[...] = jnp.dot(g, onehot, preferred_element_type=F32, precision=lax.Precision.HIGHEST)
            finish()

    last = n - 1
    cur = lambda grp: pl.BlockSpec((None, TQ, D_ATT), lambda i: (grp, jnp.minimum(i, last), 0))
    prev = lambda grp: pl.BlockSpec((None, TQ, D_ATT), lambda i: (grp, jnp.maximum(jnp.minimum(i, last) - 1, 0), 0))
    tile = pl.BlockSpec((TQ, D_ATT), lambda i: (jnp.minimum(i, last), 0))
    return pl.pallas_call(
        body, name="attn_bwd", grid=(n + 1,),
        in_specs=[pl.BlockSpec((HEADS, DIAG), lambda i: (0, 0)),
                  cur(0), prev(1), cur(1), prev(2), cur(2), tile, tile,
                  pl.BlockSpec((TQ, HEADS), lambda i: (jnp.minimum(i, last), 0))] + [ANY] * npart,
        out_specs=[pl.BlockSpec((TQ, 3 * D_ATT), lambda i: (jnp.maximum(i - 1, 0), 0)),
                   pl.BlockSpec((HEADS, rel_pad), lambda i: (0, 0))] + [ANY] * npart,
        out_shape=[jax.ShapeDtypeStruct((s, 3 * D_ATT), BF16), jax.ShapeDtypeStruct((HEADS, rel_pad), F32)]
        + _scatter_out_shapes(parts),
        scratch_shapes=[pltpu.VMEM((HEADS // 2, 2 * QB, KW), F32), pltpu.VMEM((HEADS // 2, 2 * QB, KW), F32),
                        pltpu.VMEM((2, TQ, D_ATT), F32), pltpu.VMEM((2, TQ, D_ATT), F32),
                        pltpu.VMEM((TQ, D_ATT), BF16)] + _scatter_sems(npart),
        compiler_params=_params("arbitrary"),
    )(diag, proj, proj, proj, proj, proj, d_att, att, lse, *parts)


def _shift_down(a, k, halo):
    rolled = pltpu.roll(a, k, 0)
    row = lax.broadcasted_iota(jnp.int32, halo.shape, 0)
    first = jnp.where(row < k, pltpu.roll(halo, k, 0), rolled[0:SUBLANES, :])
    return jnp.concatenate([first, rolled[SUBLANES:, :]], axis=0)


def _shift_up(a, k, nxt):
    tm = a.shape[0]
    rolled = pltpu.roll(a, tm - k, 0)
    row = lax.broadcasted_iota(jnp.int32, nxt.shape, 0)
    last = jnp.where(row >= SUBLANES - k, pltpu.roll(nxt, SUBLANES - k, 0), rolled[tm - SUBLANES:, :])
    return jnp.concatenate([rolled[:tm - SUBLANES, :], last], axis=0)


def _sigmoid(v):
    return 0.5 * jnp.tanh(0.5 * v) + 0.5


def _mixer_mid(att, proj, x, tgt, w_att, w_conv, w_out, conv_w8, conv_b, fin_g):
    s, d = x.shape
    dc = D_ATT
    n = s // TM_MID
    tm = TM_MID
    n_shards = 4

    def body(att_ref, za_ref, gb_ref, gc_ref, u_ref, zc_ref, hgc_ref, hu_ref, gatt_ref, gconv_ref, x_ref, t_ref,
             watt_ref, wconv_ref, wout_ref, cw_ref, cb_ref, fg_ref,
             dpb_ref, do_ref, dx2_ref, gatt_o, gconv_o, gout_o, loss_o, gfn_o, gcb_o, gcw_o,
             acc_att, acc_conv, acc_out, carry, watt_t_ref, wconv_t_ref, wout_t_ref):
        i = pl.program_id(0)
        tile = n - 1 - i

        @pl.when(i == 0)
        def _():
            acc_att[...] = jnp.zeros_like(acc_att)
            acc_conv[...] = jnp.zeros_like(acc_conv)
            acc_out[...] = jnp.zeros_like(acc_out)
            carry[...] = jnp.zeros_like(carry)
            loss_o[...] = jnp.zeros_like(loss_o)
            gfn_o[...] = jnp.zeros_like(gfn_o)
            gcb_o[...] = jnp.zeros_like(gcb_o)
            gcw_o[...] = jnp.zeros_like(gcw_o)
            watt_t_ref[...] = watt_ref[...].T
            wconv_t_ref[...] = wconv_ref[...].T
            wout_t_ref[...] = wout_ref[...].T

        halves = [slice(hh * (tm // 2), (hh + 1) * (tm // 2)) for hh in range(2)]
        both = lambda fn: [fn(rows) for rows in halves]
        f32 = lambda ref, rows: ref[rows, :].astype(F32)

        gc = gc_ref[...].astype(F32)
        u = u_ref[...].astype(F32)
        cu = gc * u
        halo = jnp.where(tile > 0, hgc_ref[...].astype(F32) * hu_ref[...].astype(F32), 0.0)
        cu1 = _shift_down(cu, 1, halo)
        cu2 = _shift_down(cu, 2, halo)
        w0, w1, w2 = cw_ref[0:1, :], cw_ref[1:2, :], cw_ref[2:3, :]
        fg = fg_ref[...]

        def stage_a(rows):
            att_v, za, zc, gb = f32(att_ref, rows), f32(za_ref, rows), f32(zc_ref, rows), f32(gb_ref, rows)
            sa = _sigmoid(za)
            silu_a = za * sa
            vconv = w0 * cu2[rows, :] + w1 * cu1[rows, :] + w2 * cu[rows, :] + cb_ref[...]
            sc = _sigmoid(zc)
            silu_c = zc * sc
            return dict(att_v=att_v, za=za, zc=zc, gb=gb, sa=sa, silu_a=silu_a, vconv=vconv, sc=sc, silu_c=silu_c,
                        a_b=(att_v * silu_a).astype(BF16), c_b=(gb * vconv * silu_c).astype(BF16))

        st = both(stage_a)
        for t in st:
            t["y_att"] = _nn(t["a_b"], watt_ref[...])
            t["y_conv"] = _nn(t["c_b"], wconv_ref[...])
        for t, rows in zip(st, halves):
            gpair = lambda ref: jnp.concatenate([ref[0, rows, :], ref[1, rows, :]], axis=1).astype(F32)
            t["ga"] = _sigmoid(gpair(gatt_ref))
            t["gv"] = _sigmoid(gpair(gconv_ref))
            t["m_b"] = (t["ga"] * t["y_att"] + t["gv"] * t["y_conv"]).astype(BF16)
        for t in st:
            t["mo"] = _nn(t["m_b"], wout_ref[...])
        for t, rows in zip(st, halves):
            x2 = x_ref[rows, :] + t["mo"]
            r2 = lax.rsqrt(jnp.mean(x2 * x2, axis=-1, keepdims=True) + EPS)
            x2n = x2 * r2
            err = x2n * fg - t_ref[rows, :]
            loss_o[...] += jnp.sum(err * err, axis=0, keepdims=True) * (0.5 / d)
            dy = err * (1.0 / d)
            gfn_o[...] += jnp.sum(dy * x2n, axis=0, keepdims=True)
            dyn = dy * fg
            dx2 = r2 * (dyn - x2n * jnp.mean(dyn * x2n, axis=-1, keepdims=True))
            dx2_ref[rows, :] = dx2
            t["dx2_b"] = dx2.astype(BF16)
        for t in st:
            t["dm"] = _nn(t["dx2_b"], wout_t_ref[...])
        whole = lambda key: jnp.concatenate([st[0][key], st[1][key]], axis=0)
        acc_out[...] += _tn(whole("m_b"), whole("dx2_b"))
        for t, rows in zip(st, halves):
            dy_att = t["dm"] * t["ga"]
            dy_conv = t["dm"] * t["gv"]
            dpb_ref[rows, 5 * dc:5 * dc + d] = (dy_att * t["y_att"] * (1.0 - t["ga"])).astype(BF16)
            dpb_ref[rows, 5 * dc + d:5 * dc + 2 * d] = (dy_conv * t["y_conv"] * (1.0 - t["gv"])).astype(BF16)
            t["dya_b"] = dy_att.astype(BF16)
            t["dyc_b"] = dy_conv.astype(BF16)
        for t in st:
            t["da_in"] = _nn(t["dya_b"], watt_t_ref[...])
            t["dc_in"] = _nn(t["dyc_b"], wconv_t_ref[...])
        acc_att[...] += _tn(whole("a_b"), whole("dya_b"))
        acc_conv[...] += _tn(whole("c_b"), whole("dyc_b"))
        for t, rows in zip(st, halves):
            sa, za, sc, zc = t["sa"], t["za"], t["sc"], t["zc"]
            do_ref[rows, :] = (t["da_in"] * t["silu_a"]).astype(BF16)
            dpb_ref[rows, 0:dc] = (t["da_in"] * t["att_v"] * (sa * (1.0 + za * (1.0 - sa)))).astype(BF16)
            dpb_ref[rows, dc:2 * dc] = (t["dc_in"] * t["vconv"] * t["silu_c"]).astype(BF16)
            dgs = t["dc_in"] * t["gb"]
            t["dvc"] = dgs * t["silu_c"]
            dpb_ref[rows, 4 * dc:5 * dc] = (dgs * t["vconv"] * (sc * (1.0 + zc * (1.0 - sc)))).astype(BF16)
        dvc = whole("dvc")
        gcb_o[...] += jnp.sum(dvc, axis=0, keepdims=True)
        gcw_o[0:1, :] += jnp.sum(dvc * cu2, axis=0, keepdims=True)
        gcw_o[1:2, :] += jnp.sum(dvc * cu1, axis=0, keepdims=True)
        gcw_o[2:3, :] += jnp.sum(dvc * cu, axis=0, keepdims=True)
        nxt = carry[...]
        dcu = w2 * dvc + w1 * _shift_up(dvc, 1, nxt) + w0 * _shift_up(dvc, 2, nxt)
        carry[...] = dvc[0:SUBLANES, :]
        dpb_ref[:, 2 * dc:3 * dc] = (dcu * u).astype(BF16)
        dpb_ref[:, 3 * dc:4 * dc] = (dcu * gc).astype(BF16)

        @pl.when(i == n - 1)
        def _():
            for j in range(n_shards):
                gatt_o[j] = acc_att[:, j * (d // n_shards):(j + 1) * (d // n_shards)].astype(BF16)
                gconv_o[j] = acc_conv[:, j * (d // n_shards):(j + 1) * (d // n_shards)].astype(BF16)
                gout_o[j] = acc_out[j * (d // n_shards):(j + 1) * (d // n_shards), :].astype(BF16)

    rev = lambda width, col_blk: pl.BlockSpec((tm, width), lambda i: (n - 1 - i, col_blk))
    grp = lambda g: pl.BlockSpec((None, tm, dc), lambda i: (g, n - 1 - i, 0))
    grp2 = lambda g2: pl.BlockSpec((2, tm, dc), lambda i: (g2, n - 1 - i, 0))
    halo_spec = lambda g: pl.BlockSpec(
        (None, SUBLANES, dc), lambda i: (g, jnp.maximum((n - 1 - i) * (tm // SUBLANES) - 1, 0), 0))
    const = lambda shape: pl.BlockSpec(shape, lambda i: tuple(0 for _ in shape), pipeline_mode=pl.Buffered(1))
    q4 = d // n_shards
    return pl.pallas_call(
        body, name="mixer_mid", grid=(n,),
        in_specs=[rev(dc, 0), grp(3), grp(4), grp(5), grp(6), grp(7),
                  halo_spec(5), halo_spec(6), grp2(4), grp2(5), rev(d, 0), rev(d, 0),
                  const((dc, d)), const((dc, d)), const((d, d)),
                  const((SUBLANES, dc)), const((1, dc)), const((1, d))],
        out_specs=[rev(5 * dc + 2 * d, 0), rev(dc, 0), rev(d, 0),
                   const((n_shards, dc, q4)), const((n_shards, dc, q4)), const((n_shards, q4, d)),
                   const((1, d)), const((1, d)), const((1, dc)), const((SUBLANES, dc))],
        out_shape=[jax.ShapeDtypeStruct((s, 5 * dc + 2 * d), BF16), jax.ShapeDtypeStruct((s, dc), BF16),
                   jax.ShapeDtypeStruct((s, d), F32),
                   jax.ShapeDtypeStruct((n_shards, dc, q4), BF16), jax.ShapeDtypeStruct((n_shards, dc, q4), BF16),
                   jax.ShapeDtypeStruct((n_shards, q4, d), BF16),
                   jax.ShapeDtypeStruct((1, d), F32), jax.ShapeDtypeStruct((1, d), F32),
                   jax.ShapeDtypeStruct((1, dc), F32), jax.ShapeDtypeStruct((SUBLANES, dc), F32)],
        scratch_shapes=[pltpu.VMEM((dc, d), F32), pltpu.VMEM((dc, d), F32), pltpu.VMEM((d, d), F32),
                        pltpu.VMEM((SUBLANES, dc), F32),
                        pltpu.VMEM((d, dc), BF16), pltpu.VMEM((d, dc), BF16), pltpu.VMEM((d, d), BF16)],
        compiler_params=_params("arbitrary"),
    )(att, proj, proj, proj, proj, proj, proj, proj, proj, proj, x, tgt,
      w_att, w_conv, w_out, conv_w8, conv_b, fin_g)


def _in_proj_bwd_x(dqkv, dpb, w_in, x, dx2, g, adam):
    s, d = x.shape
    tn = dqkv.shape[1]
    nb = dpb.shape[1] // tn
    n = s // TM_MM
    ar, ac = adam[0].shape
    arows = ar // n

    def body(*refs):
        dps, ws = refs[:nb + 1], refs[nb + 1:2 * nb + 2]
        x_ref, dx2_ref, g_ref, aw_ref, ag_ref, am_ref, av_ref = refs[2 * nb + 2:2 * nb + 9]
        gx_ref, gng_ref, go_ref, ad_ref, mo_ref, vo_ref = refs[2 * nb + 9:]
        i = pl.program_id(0)
        gv = ag_ref[...]
        go_ref[...] = gv
        ad_ref[...], mo_ref[...], vo_ref[...] = _adam_update(aw_ref[...], gv, am_ref[...], av_ref[...])

        @pl.when(i == 0)
        def _():
            gng_ref[...] = jnp.zeros_like(gng_ref)

        dh = _nt(dps[0][...], ws[0][0])
        for j in range(1, nb + 1):
            dh = dh + _nt(dps[j][...], ws[j][0])
        xv = x_ref[...]
        r = lax.rsqrt(jnp.mean(xv * xv, axis=-1, keepdims=True) + EPS)
        xn = xv * r
        gng_ref[...] += jnp.sum(dh * xn, axis=0, keepdims=True)
        dhn = dh * g_ref[...]
        gx_ref[...] = dx2_ref[...] + r * (dhn - xn * jnp.mean(dhn * xn, axis=-1, keepdims=True))

    tile = lambda width, col_blk: pl.BlockSpec((TM_MM, width), lambda i: (i, col_blk))
    wspec = lambda blk: pl.BlockSpec((1, d, tn), lambda i: (blk, 0, 0), pipeline_mode=pl.Buffered(1))
    aspec = pl.BlockSpec((arows, ac), lambda i: (i, 0))
    ashape = jax.ShapeDtypeStruct((ar, ac), F32)
    return pl.pallas_call(
        body, name="in_proj_bwd_x", grid=(n,),
        in_specs=[tile(tn, 0)] + [tile(tn, j) for j in range(nb)] + [wspec(j) for j in range(nb + 1)]
        + [tile(d, 0), tile(d, 0), pl.BlockSpec((1, d), lambda i: (0, 0))] + [aspec] * 4,
        out_specs=[tile(d, 0), pl.BlockSpec((1, d), lambda i: (0, 0))] + [aspec] * 4,
        out_shape=[jax.ShapeDtypeStruct((s, d), F32), jax.ShapeDtypeStruct((1, d), F32)] + [ashape] * 4,
        compiler_params=_params("arbitrary"),
    )(dqkv, *([dpb] * nb), *([w_in] * (nb + 1)), x, dx2, g, *adam)


def _in_proj_bwd_w(h, dqkv, dpb, order):
    s, d = h.shape
    tn = dqkv.shape[1]
    n = s // TM_BLK
    hr = d // 2
    settle = min(2, n - 1)

    def body(order_ref, h_ref, da_ref, db_ref, slots_ref, acc, sendbuf, pairbuf, chipbuf, psend, precv, send, recv, lsem):
        j, i = pl.program_id(0), pl.program_id(1)
        blk = order_ref[j]
        pos = _position()
        x, y, c = pos

        @pl.when(i == 0)
        def _():
            acc[...] = jnp.zeros_like(acc)

        @pl.when(blk == 0)
        def _():
            acc[...] += _tn(h_ref[...], da_ref[...])

        @pl.when(blk > 0)
        def _():
            acc[...] += _tn(h_ref[...], db_ref[...])

        def pair(step, half):
            return pltpu.make_async_remote_copy(
                src_ref=sendbuf.at[step, pl.ds(half * hr, hr), :], dst_ref=pairbuf.at[step],
                send_sem=psend.at[step], recv_sem=precv.at[step], device_id=(x, y, 1 - c), device_id_type=MESH)

        def ici(step):
            flip = OWNER_FLIPS[step]
            return pltpu.make_async_remote_copy(
                src_ref=chipbuf.at[step], dst_ref=slots_ref.at[flip], send_sem=send.at[step], recv_sem=recv.at[step],
                device_id=_peer(pos, 4 * (flip >> 1) + 2 * (flip & 1)), device_id_type=MESH)

        local = pltpu.make_async_copy(chipbuf.at[N_CHIPS - 1], slots_ref.at[0], lsem.at[0])

        def combine(step):
            pair(step, c).wait_recv()
            mine = sendbuf[step, pl.ds(c * hr, hr), :].astype(F32)
            chipbuf[step] = (mine + pairbuf[step].astype(F32)).astype(BF16)

        for step in range(N_CHIPS):
            @pl.when((j == step) & (i == n - 1))
            def _(step=step):
                sendbuf[step] = acc[...].astype(BF16)
                pair(step, 1 - c).start()

        for step in range(N_CHIPS - 1):
            @pl.when((j == step + 1) & (i == settle))
            def _(step=step):
                combine(step)
                ici(step).start()

        @pl.when((j == N_CHIPS - 1) & (i == n - 1))
        def _():
            combine(N_CHIPS - 1)
            local.start()
            for step in range(N_CHIPS - 1):
                ici(step).wait_recv()
            for step in range(N_CHIPS - 1):
                ici(step).wait_send()
            for step in range(N_CHIPS):
                pair(step, 1 - c).wait_send()
            local.wait()

    return pl.pallas_call(
        body, name="in_proj_bwd_w",
        grid_spec=pltpu.PrefetchScalarGridSpec(
            num_scalar_prefetch=1, grid=(N_CHIPS, n),
            in_specs=[pl.BlockSpec((TM_BLK, d), lambda j, i, order: (i, 0)),
                      pl.BlockSpec((TM_BLK, tn), lambda j, i, order: (jnp.where(order[j] == 0, i, 0), 0)),
                      pl.BlockSpec((TM_BLK, tn), lambda j, i, order: (jnp.where(order[j] == 0, 0, i),
                                                                     jnp.maximum(order[j] - 1, 0)))],
            out_specs=[ANY],
            scratch_shapes=[pltpu.VMEM((d, tn), F32), pltpu.VMEM((N_CHIPS, d, tn), BF16),
                            pltpu.VMEM((N_CHIPS, hr, tn), BF16), pltpu.VMEM((N_CHIPS, hr, tn), BF16),
                            pltpu.SemaphoreType.DMA((N_CHIPS,)), pltpu.SemaphoreType.DMA((N_CHIPS,)),
                            pltpu.SemaphoreType.DMA((N_CHIPS - 1,)), pltpu.SemaphoreType.DMA((N_CHIPS - 1,)),
                            pltpu.SemaphoreType.DMA((1,))]),
        out_shape=[jax.ShapeDtypeStruct((N_CHIPS, hr, tn), BF16)],
        compiler_params=_params("arbitrary", "arbitrary"),
    )(order, h, dqkv, dpb)[0]


LOSS_ROW = 6


def _adam_update(w, g, m, v):
    c1 = 1.0 / (1.0 - ADAM_B1 ** ADAM_STEP)
    c2 = 1.0 / (1.0 - ADAM_B2 ** ADAM_STEP)
    m2 = ADAM_B1 * m + (1.0 - ADAM_B1) * g
    v2 = ADAM_B2 * v + (1.0 - ADAM_B2) * (g * g)
    return -ADAM_LR * ((m2 * c1) / (jnp.sqrt(v2 * c2) + ADAM_EPS) + ADAM_WD * w), m2, v2


def _adamw_small(recv, params, moments_m, moments_v):
    k = recv.shape[0]
    n_par = len(params)
    cshard = params[3].shape[1]

    def body(*refs):
        r_ref = refs[0]
        ws, ms, vs = refs[1:1 + n_par], refs[1 + n_par:1 + 2 * n_par], refs[1 + 2 * n_par:1 + 3 * n_par]
        loss_ref = refs[1 + 3 * n_par]
        outs = refs[2 + 3 * n_par:]
        total = r_ref[0]
        for slot in range(1, k):
            total = total + r_ref[slot]
        loss_ref[...] = jnp.sum(total[LOSS_ROW:LOSS_ROW + 1, :], axis=1, keepdims=True)
        chip = 2 * lax.axis_index("x") + lax.axis_index("y")
        g_cw = jnp.zeros((3, cshard), F32)
        for sh in range(N_CHIPS):
            g_cw = g_cw + jnp.where(chip == sh, total[3:6, sh * cshard:(sh + 1) * cshard], 0.0)
        grads = [total[0:1, :], total[1:2, :], total[2:3, :ws[2].shape[1]], g_cw, total[8:16, :ws[4].shape[1]]]
        for p in range(n_par):
            delta, m2, v2 = _adam_update(ws[p][...], grads[p], ms[p][...], vs[p][...])
            for q, val in enumerate((grads[p], delta, m2, v2)):
                outs[4 * p + q][...] = val

    shapes = [jax.ShapeDtypeStruct((1, 1), F32)]
    for p in params:
        shapes += [jax.ShapeDtypeStruct(p.shape, F32)] * 4
    return pl.pallas_call(body, name="adamw_small", out_shape=shapes)(recv, *params, *moments_m, *moments_v)


def _adamw_group(ws, gs, ms, vs, name):
    k = len(ws)

    def body(*refs):
        ins, outs = refs[:4 * k], refs[4 * k:]
        for j in range(k):
            gv = ins[k + j][...]
            outs[4 * j][...] = gv
            outs[4 * j + 1][...], outs[4 * j + 2][...], outs[4 * j + 3][...] = _adam_update(
                ins[j][...], gv, ins[2 * k + j][...], ins[3 * k + j][...])

    shapes = []
    for w in ws:
        shapes += [jax.ShapeDtypeStruct(w.shape, F32)] * 4
    out = pl.pallas_call(body, name=name, out_shape=shapes,
                         compiler_params=pltpu.CompilerParams(vmem_limit_bytes=VMEM_LIMIT))(*ws, *gs, *ms, *vs)
    return [tuple(out[4 * j:4 * j + 4]) for j in range(k)]


ANY = pl.BlockSpec(memory_space=pl.ANY)
N_CHIPS = 4
N_DEV = 8
OWNER_FLIPS = (3, 1, 2, 0)


def _position():
    return lax.axis_index("x"), lax.axis_index("y"), lax.axis_index("c")


def _gather_out_shapes(shards, kinds, cw8):
    full = [(a.shape[0], a.shape[1] * N_CHIPS) if k == "cols" else (a.shape[0] * N_CHIPS, a.shape[1])
            for a, k in zip(shards, kinds)]
    return [jax.ShapeDtypeStruct(f, BF16) for f in full] + [
        jax.ShapeDtypeStruct((N_CHIPS,) + cw8.shape, cw8.dtype)]


def _gather_sems(nw):
    return [pltpu.SemaphoreType.DMA((3, nw)), pltpu.SemaphoreType.DMA((3, nw)),
            pltpu.SemaphoreType.DMA((3, nw)), pltpu.SemaphoreType.DMA((3, nw)),
            pltpu.SemaphoreType.DMA((3,)), pltpu.SemaphoreType.DMA((3,)), pltpu.SemaphoreType.DMA((nw + 1,))]


def _gather_plan(kinds, srcs, dsts, cw, cw_all, send1, recv1, send2, recv2, ssend, srecv, lsem):
    nw = len(srcs)
    x, y, c = _position()
    mine = 2 * x + y
    chips = [(x, 1 - y), (1 - x, y), (1 - x, 1 - y)]

    def window(w, shard, half):
        r, cc = srcs[w].shape
        hr = r // 2
        if kinds[w] == "cols":
            rows = pl.ds(0, r) if half is None else pl.ds(half * hr, hr)
            return dsts[w].at[rows, pl.ds(shard * cc, cc)]
        rows = pl.ds(shard * r, r) if half is None else pl.ds(shard * r + half * hr, hr)
        return dsts[w].at[rows, :]

    def my_half(w):
        hr = srcs[w].shape[0] // 2
        return srcs[w].at[pl.ds(c * hr, hr), :]

    def local():
        return [pltpu.make_async_copy(srcs[w], window(w, mine, None), lsem.at[w]) for w in range(nw)] + [
            pltpu.make_async_copy(cw, cw_all.at[mine], lsem.at[nw])]

    def ici(k, w, shard):
        kx, ky = chips[k]
        return pltpu.make_async_remote_copy(
            src_ref=my_half(w), dst_ref=window(w, shard, c), send_sem=send1.at[k, w], recv_sem=recv1.at[k, w],
            device_id=(kx, ky, c), device_id_type=MESH)

    def d2d(k, w, shard, half):
        return pltpu.make_async_remote_copy(
            src_ref=window(w, shard, half), dst_ref=window(w, shard, half),
            send_sem=send2.at[k, w], recv_sem=recv2.at[k, w], device_id=(x, y, 1 - c), device_id_type=MESH)

    def small(k, shard):
        kx, ky = chips[k]
        return pltpu.make_async_remote_copy(
            src_ref=cw, dst_ref=cw_all.at[shard], send_sem=ssend.at[k], recv_sem=srecv.at[k],
            device_id=(kx, ky, c), device_id_type=MESH)

    def theirs(k):
        kx, ky = chips[k]
        return 2 * kx + ky

    def start():
        for cp in local():
            cp.start()
        for k in range(3):
            for w in range(nw):
                ici(k, w, mine).start()
            small(k, mine).start()

    def forward():
        for k in range(3):
            for w in range(nw):
                ici(k, w, theirs(k)).wait_recv()
                d2d(k, w, theirs(k), c).start()

    def finish():
        for k in range(3):
            for w in range(nw):
                d2d(k, w, theirs(k), 1 - c).wait_recv()
            small(k, theirs(k)).wait_recv()
        for k in range(3):
            for w in range(nw):
                ici(k, w, mine).wait_send()
                d2d(k, w, theirs(k), c).wait_send()
            small(k, mine).wait_send()
        for cp in local():
            cp.wait()

    return start, forward, finish


def _scatter_out_shapes(parts):
    return [jax.ShapeDtypeStruct((N_DEV, p.shape[1] // 2, p.shape[2]), p.dtype) for p in parts]


def _scatter_sems(nw):
    return [pltpu.SemaphoreType.DMA((N_DEV, nw)), pltpu.SemaphoreType.DMA((N_DEV, nw)), pltpu.SemaphoreType.DMA((nw,))]


def _peer(pos, k):
    x, y, c = pos
    return ((1 - x) if k & 4 else x, (1 - y) if k & 2 else y, (1 - c) if k & 1 else c)


def _scatter_plan(srcs, dsts, send, recv, lsem):
    nw = len(srcs)
    pos = _position()

    def piece(w, k):
        px, py, pc = _peer(pos, k)
        hr = srcs[w].shape[1] // 2
        return srcs[w].at[2 * px + py, pl.ds(pc * hr, hr), :]

    def remote(w, k):
        return pltpu.make_async_remote_copy(
            src_ref=piece(w, k), dst_ref=dsts[w].at[k], send_sem=send.at[k, w], recv_sem=recv.at[k, w],
            device_id=_peer(pos, k), device_id_type=MESH)

    def local(w):
        return pltpu.make_async_copy(piece(w, 0), dsts[w].at[0], lsem.at[w])

    def start():
        for w in range(nw):
            local(w).start()
        for k in range(1, N_DEV):
            for w in range(nw):
                remote(w, k).start()

    def finish():
        for k in range(1, N_DEV):
            for w in range(nw):
                remote(w, k).wait_recv()
        for k in range(1, N_DEV):
            for w in range(nw):
                remote(w, k).wait_send()
        for w in range(nw):
            local(w).wait()

    return start, finish


def _gather_small(small):
    def body(sm, sm_all, ssend, srecv, lsem):
        pos = _position()
        x, y, c = pos
        me = 4 * x + 2 * y + c

        def bcast(k, slot):
            return pltpu.make_async_remote_copy(
                src_ref=sm, dst_ref=sm_all.at[slot], send_sem=ssend.at[k], recv_sem=srecv.at[k],
                device_id=_peer(pos, k), device_id_type=MESH)

        small_copies = [bcast(k, me) for k in range(1, N_DEV)]
        own_small = pltpu.make_async_copy(sm, sm_all.at[me], lsem.at[0])
        for cp in small_copies + [own_small]:
            cp.start()
        for k in range(1, N_DEV):
            px, py, pc = _peer(pos, k)
            bcast(k, 4 * px + 2 * py + pc).wait_recv()
        for cp in small_copies:
            cp.wait_send()
        own_small.wait()

    return pl.pallas_call(
        body, name="gather_small",
        in_specs=[pl.BlockSpec(memory_space=pltpu.VMEM)], out_specs=ANY,
        out_shape=jax.ShapeDtypeStruct((N_DEV,) + small.shape, small.dtype),
        scratch_shapes=[pltpu.SemaphoreType.DMA((N_DEV,)), pltpu.SemaphoreType.DMA((N_DEV,)),
                        pltpu.SemaphoreType.DMA((1,))],
        compiler_params=pltpu.CompilerParams(vmem_limit_bytes=VMEM_LIMIT),
    )(small)


def _reduce_pair(slots):
    nw = len(slots)

    def body(*refs):
        srcs = refs[:nw]
        dsts = refs[nw:2 * nw]
        halves = refs[2 * nw:3 * nw]
        send, recv, lsem = refs[3 * nw:]
        x, y, c = _position()

        def rows(w, half):
            hr = halves[w].shape[0]
            return dsts[w].at[pl.ds(half * hr, hr), :]

        def remote(w, half):
            return pltpu.make_async_remote_copy(
                src_ref=halves[w], dst_ref=rows(w, half), send_sem=send.at[w], recv_sem=recv.at[w],
                device_id=(x, y, 1 - c), device_id_type=MESH)

        big = []
        for w in range(nw):
            total = srcs[w][0].astype(F32)
            for k in range(1, srcs[w].shape[0]):
                total = total + srcs[w][k].astype(F32)
            halves[w][...] = total
            big += [remote(w, c), pltpu.make_async_copy(halves[w], rows(w, c), lsem.at[w])]
            big[-2].start()
            big[-1].start()
        for w in range(nw):
            remote(w, 1 - c).wait_recv()
        for w in range(nw):
            big[2 * w].wait_send()
            big[2 * w + 1].wait()

    vmem = pl.BlockSpec(memory_space=pltpu.VMEM)
    half_shapes = [(sl.shape[1], sl.shape[2]) for sl in slots]
    return pl.pallas_call(
        body, name="reduce_pair",
        in_specs=[vmem] * nw, out_specs=[ANY] * nw,
        out_shape=[jax.ShapeDtypeStruct((2 * r, cc), F32) for r, cc in half_shapes],
        scratch_shapes=[pltpu.VMEM(hs, F32) for hs in half_shapes]
        + [pltpu.SemaphoreType.DMA((nw,)), pltpu.SemaphoreType.DMA((nw,)), pltpu.SemaphoreType.DMA((nw,))],
        compiler_params=pltpu.CompilerParams(vmem_limit_bytes=VMEM_LIMIT),
    )(*slots)


def _pad_to(a, rows, cols):
    return jnp.pad(a, ((0, rows - a.shape[0]), (0, cols - a.shape[1])))


def _pack_small(norm_g, fin_g, conv_b, conv_w, loss_vec, rel):
    rows = [_pad_to(norm_g, 1, SMALL_COLS), _pad_to(fin_g, 1, SMALL_COLS), _pad_to(conv_b, 1, SMALL_COLS),
            _pad_to(conv_w, 3, SMALL_COLS), _pad_to(loss_vec, 2, SMALL_COLS), _pad_to(rel, HEADS, SMALL_COLS)]
    return jnp.concatenate(rows, axis=0)


def kernel(x, norm_g, w_in, rel_bias, w_att_out, conv_w, conv_b, w_conv_out, w_out, final_norm_g, loss_target, m_norm_g, m_w_in, m_rel_bias, m_w_att_out, m_conv_w, m_conv_b, m_w_conv_out, m_w_out, m_final_norm_g, v_norm_g, v_w_in, v_rel_bias, v_w_att_out, v_conv_w, v_conv_b, v_w_conv_out, v_w_out, v_final_norm_g):
    xs, tgt = x[0], loss_target[0]
    cshard = conv_w.shape[2]
    chip = 2 * lax.axis_index("x") + lax.axis_index("y")

    shards = [w_in[0], w_att_out[0], w_conv_out[0], w_out[0]]
    cw8 = _pad_to(conv_w[0], SUBLANES, cshard)
    flips = jnp.arange(N_CHIPS, dtype=jnp.int32)
    own_first = jnp.bitwise_xor(chip, flips)
    own_last = jnp.bitwise_xor(chip, jnp.asarray(OWNER_FLIPS, jnp.int32))

    proj, h, wb_in = _in_proj_gather(xs, norm_g, shards[0], own_first)
    diag = jnp.take(rel_bias[0], _diag_rel_index(), axis=1)
    att, lse, wb_att, wb_conv, wb_out, cw_all = _attn_fwd(diag, proj, shards[1:], ["cols", "cols", "rows"], cw8)
    conv_w_full = jnp.transpose(cw_all, (1, 0, 2)).reshape(SUBLANES, N_CHIPS * cshard)
    (dpb, d_att, dx2, g_att_p, g_conv_p, g_out_p, loss_vec, g_fin, g_cb, g_cw) = _mixer_mid(
        att, proj, xs, tgt, wb_att, wb_conv, wb_out, conv_w_full, conv_b,
        final_norm_g[None, :])
    dqkv, g_rel, r_att, r_conv, r_out = _attn_bwd(diag, proj, d_att, att, lse, [g_att_p, g_conv_p, g_out_p])
    r_in = _in_proj_bwd_w(h, dqkv, dpb, own_last)

    gw_in, gw_att, gw_conv, gw_out = _reduce_pair([r_in, r_att, r_conv, r_out])
    grad_x, g_norm, *adam_w_in = _in_proj_bwd_x(
        dqkv, dpb, wb_in, xs, dx2, norm_g, (w_in[0], gw_in, m_w_in[0], v_w_in[0]))
    small = _pack_small(g_norm, g_fin, g_cb, g_cw[0:3], loss_vec, g_rel)
    r_small = _gather_small(small)
    small_out = _adamw_small(
        r_small,
        [norm_g, final_norm_g[None, :], conv_b, conv_w[0], rel_bias[0]],
        [m_norm_g, m_final_norm_g[None, :], m_conv_b, m_conv_w[0], m_rel_bias[0]],
        [v_norm_g, v_final_norm_g[None, :], v_conv_b, v_conv_w[0], v_rel_bias[0]])
    loss = small_out[0][0, 0]
    small_names = ["norm_g", "final_norm_g", "conv_b", "conv_w", "rel_bias"]
    fix = {"norm_g": lambda a: a, "final_norm_g": lambda a: a[0], "conv_b": lambda a: a,
           "conv_w": lambda a: a[None], "rel_bias": lambda a: a[None]}
    small_res = {name: [fix[name](small_out[1 + 4 * p + q]) for q in range(4)] for p, name in enumerate(small_names)}

    big = {"w_in": tuple(adam_w_in)}
    names = ["w_att_out", "w_conv_out", "w_out"]
    group = _adamw_group([w_att_out[0], w_conv_out[0], w_out[0]], [gw_att, gw_conv, gw_out],
                         [m_w_att_out[0], m_w_conv_out[0], m_w_out[0]],
                         [v_w_att_out[0], v_w_conv_out[0], v_w_out[0]], "adamw_small_matrices")
    big.update(zip(names, group))
    big = {name: tuple(a[None] for a in four) for name, four in big.items()}

    order = ["norm_g", "w_in", "rel_bias", "w_att_out", "conv_w", "conv_b", "w_conv_out", "w_out", "final_norm_g"]
    outs = [loss, grad_x[None]]
    for which in range(4):
        for name in order:
            outs.append(big[name][which] if name in big else small_res[name][which])
    return tuple(outs)
```

```python
import numpy as np
import jax
import jax.numpy as jnp
from jax import lax
from jax.experimental import pallas as pl
from jax.experimental.pallas import tpu as pltpu

F32 = jnp.float32
BF16 = jnp.bfloat16
MESH = pl.DeviceIdType.MESH

CHUNK = 64
N_LEFT = 8
HEADS = 8
HEAD_DIM = 64
D_ATT = HEADS * HEAD_DIM
MAX_REL = 128
N_REL = 2 * MAX_REL + 1
EPS = 1e-6
NEG_BIG = -1e30
ADAM_LR, ADAM_B1, ADAM_B2, ADAM_EPS, ADAM_WD, ADAM_STEP = 0.001, 0.9, 0.999, 1e-08, 0.01, 10

LANES = 128
SUBLANES = 8
VMEM_LIMIT = 56 * 1024 * 1024

QB = 2 * CHUNK
KW = N_LEFT * CHUNK + QB
DIAG = KW + QB
TQ = N_LEFT * CHUNK
TM_MID = 256
ADAMW_STEPS = 8
TM_MM = 512
TM_BLK = 1024
SMALL_ROWS, SMALL_COLS = 16, 1024


def _params(*sem):
    return pltpu.CompilerParams(dimension_semantics=sem, vmem_limit_bytes=VMEM_LIMIT)


def _nt(a, b):
    return lax.dot_general(a, b, (((1,), (1,)), ((), ())), preferred_element_type=F32)


def _tn(a, b):
    return lax.dot_general(a, b, (((0,), (0,)), ((), ())), preferred_element_type=F32)


def _nn(a, b):
    return jnp.dot(a, b, preferred_element_type=F32)


def _diag_rel_index():
    d = np.arange(DIAG)
    diff = np.where(d < KW, d, d - DIAG)
    rel = N_LEFT * CHUNK - diff
    return np.clip(rel, -MAX_REL, MAX_REL) + MAX_REL


def _build_bias(diag_ref, bias_scr):
    r = lax.broadcasted_iota(jnp.int32, (QB, KW), 0) // CHUNK
    s = lax.broadcasted_iota(jnp.int32, (QB, KW), 1) // CHUNK
    allowed = (s >= r) & (s <= r + N_LEFT)
    for h in range(HEADS):
        row = jnp.broadcast_to(diag_ref[h:h + 1, :], (QB, DIAG))
        t = pltpu.roll(row, 0, 1, stride=1, stride_axis=0)
        bias_scr[h // 2, (h % 2) * QB:(h % 2 + 1) * QB, :] = jnp.where(allowed, t[:, :KW], NEG_BIG)


def _stack_heads(a, lane_hi):
    zero = jnp.zeros_like(a)
    return jnp.concatenate([jnp.where(lane_hi, zero, a), jnp.where(lane_hi, a, zero)], axis=0)


def _in_proj_gather(x, g, shard, order):
    s, d = x.shape
    tn = shard.shape[1]
    n = s // TM_BLK
    hr = d // 2

    def body(order_ref, x_ref, g_ref, shard32_ref, proj_ref, h_ref, wfull_ref, shard_ref, hbuf, wbuf,
             send1, recv1, send2, recv2, lsem):
        del order_ref
        j, i = pl.program_id(0), pl.program_id(1)
        x, y, c = _position()
        mine = 2 * x + y
        chips = [(x, 1 - y), (1 - x, y), (1 - x, 1 - y)]

        def theirs(k):
            return 2 * chips[k][0] + chips[k][1]

        def half_rows(half):
            return pl.ds(half * hr, hr)

        def landing(k, shard_index, half):
            if k < 2:
                return wbuf.at[k, half_rows(half), :]
            return wfull_ref.at[shard_index, half_rows(half), :]

        def ici(k, shard_index):
            return pltpu.make_async_remote_copy(
                src_ref=shard_ref.at[half_rows(c), :], dst_ref=landing(k, shard_index, c),
                send_sem=send1.at[k], recv_sem=recv1.at[k], device_id=(*chips[k], c), device_id_type=MESH)

        def d2d(k, half):
            return pltpu.make_async_remote_copy(
                src_ref=wbuf.at[k % 2, half_rows(half), :], dst_ref=landing(k, theirs(k), half),
                send_sem=send2.at[k], recv_sem=recv2.at[k], device_id=(x, y, 1 - c), device_id_type=MESH)

        def load(k, half, sem):
            return pltpu.make_async_copy(wfull_ref.at[theirs(k), half_rows(half), :],
                                         wbuf.at[k % 2, half_rows(half), :], lsem.at[sem])

        def keep(k):
            return pltpu.make_async_copy(wbuf.at[k], wfull_ref.at[theirs(k)], lsem.at[3 + k])

        own = pltpu.make_async_copy(shard_ref, wfull_ref.at[mine], lsem.at[0])

        def put_proj(block):
            for grp in range(tn // D_ATT):
                proj_ref[grp] = block[:, grp * D_ATT:(grp + 1) * D_ATT].astype(BF16)

        @pl.when((j == 0) & (i == 0))
        def _():
            shard_ref[...] = shard32_ref[...].astype(BF16)
            own.start()
            ici(0, mine).start()
            ici(1, mine).start()

        for k in range(2):
            first = n - 1 if k == 0 else min(n // 2, n - 1)

            @pl.when((j == k) & (i == first))
            def _(k=k):
                if k == 0:
                    ici(0, mine).wait_send()
                    ici(1, mine).wait_send()
                    ici(2, mine).start()
                ici(k, theirs(k)).wait_recv()
                d2d(k, c).start()

            then = (1, 0) if k == 0 else (1, min(first + 1, n - 1))

            @pl.when((j == then[0]) & (i == then[1]))
            def _(k=k):
                d2d(k, 1 - c).wait_recv()
                keep(k).start()

        first = min(n // 2, n - 1)

        @pl.when((j == 2) & (i == first))
        def _():
            d2d(0, c).wait_send()
            keep(0).wait()
            ici(2, theirs(2)).wait_recv()
            load(2, c, 1).start()

        @pl.when((j == 2) & (i == min(first + 1, n - 1)))
        def _():
            load(2, c, 1).wait()
            d2d(2, c).start()

        @pl.when((j == 2) & (i == min(first + 2, n - 1)))
        def _():
            d2d(2, 1 - c).wait_recv()
            load(2, 1 - c, 2).start()

        @pl.when((j == 3) & (i == 0))
        def _():
            load(2, 1 - c, 2).wait()

        @pl.when(j == 0)
        def _():
            xv = x_ref[...]
            r = lax.rsqrt(jnp.mean(xv * xv, axis=-1, keepdims=True) + EPS)
            hv = ((xv * r) * g_ref[...]).astype(BF16)
            hbuf[i] = hv
            h_ref[...] = hv
            put_proj(_nn(hv, shard_ref[...]))

        for k in range(3):
            @pl.when(j == k + 1)
            def _(k=k):
                put_proj(_nn(hbuf[i], wbuf[k % 2]))

        @pl.when((j == 3) & (i == n - 1))
        def _():
            ici(2, mine).wait_send()
            d2d(1, c).wait_send()
            d2d(2, c).wait_send()
            own.wait()
            keep(1).wait()

    return pl.pallas_call(
        body, name="in_proj_gather",
        grid_spec=pltpu.PrefetchScalarGridSpec(
            num_scalar_prefetch=1, grid=(N_CHIPS, n),
            in_specs=[pl.BlockSpec((TM_BLK, d), lambda j, i, order: (jnp.where(j == 0, i, n - 1), 0)),
                      pl.BlockSpec((1, d), lambda j, i, order: (0, 0)), pl.BlockSpec(memory_space=pltpu.VMEM)],
            out_specs=[pl.BlockSpec((tn // D_ATT, TM_BLK, D_ATT), lambda j, i, order: (order[j], i, 0)),
                       pl.BlockSpec((TM_BLK, d), lambda j, i, order: (jnp.where(j == 0, i, n - 1), 0)), ANY],
            scratch_shapes=[pltpu.VMEM((d, tn), BF16), pltpu.VMEM((n, TM_BLK, d), BF16), pltpu.VMEM((2, d, tn), BF16),
                            pltpu.SemaphoreType.DMA((3,)), pltpu.SemaphoreType.DMA((3,)),
                            pltpu.SemaphoreType.DMA((3,)), pltpu.SemaphoreType.DMA((3,)), pltpu.SemaphoreType.DMA((5,))]),
        out_shape=[jax.ShapeDtypeStruct((N_CHIPS * tn // D_ATT, s, D_ATT), BF16), jax.ShapeDtypeStruct((s, d), BF16),
                   jax.ShapeDtypeStruct((N_CHIPS, d, tn), BF16)],
        compiler_params=_params("arbitrary", "arbitrary"),
    )(order, x, g, shard)


def _attn_fwd(diag, proj, shards, kinds, cw8):
    s = proj.shape[1]
    n = s // TQ
    nw = len(shards)
    scale = HEAD_DIM ** -0.5

    def body(*refs):
        diag_ref, q_ref, kp_ref, kc_ref, vp_ref, vc_ref = refs[:6]
        srcs, cw = refs[6:6 + nw], refs[6 + nw]
        o_ref, lse_ref = refs[7 + nw:9 + nw]
        dsts, cw_all = refs[9 + nw:9 + 2 * nw], refs[9 + 2 * nw]
        bias_scr = refs[10 + 2 * nw]
        casts = refs[11 + 2 * nw:11 + 3 * nw]
        start, forward, finish = _gather_plan(kinds, casts, dsts, cw, cw_all, *refs[11 + 3 * nw:])
        i = pl.program_id(0)

        @pl.when(i == 0)
        def _():
            for w in range(nw):
                casts[w][...] = srcs[w][...].astype(BF16)
            start()
            _build_bias(diag_ref, bias_scr)

        @pl.when(i == n // 2)
        def _():
            forward()

        @pl.when(i == n - 1)
        def _():
            finish()

        lane_hi = lax.broadcasted_iota(jnp.int32, (QB, LANES), 1) >= HEAD_DIM

        def block(b, first_tile):
            r0, n_prev = b * QB, TQ - b * QB
            n_cur = KW - n_prev
            pairs = range(HEADS // 2)
            lanes_of = [slice(LANES * p, LANES * (p + 1)) for p in pairs]
            scores = []
            for p in pairs:
                lanes = lanes_of[p]
                q2 = _stack_heads(q_ref[r0:r0 + QB, lanes] * scale, lane_hi)
                s_cur = _nt(q2, kc_ref[0:n_cur, lanes]) + bias_scr[p, :, n_prev:KW]
                if first_tile:
                    scores.append(s_cur)
                else:
                    scores.append(jnp.concatenate(
                        [_nt(q2, kp_ref[r0:TQ, lanes]) + bias_scr[p, :, 0:n_prev], s_cur], axis=1))
            probs = []
            for p in pairs:
                sc = scores[p]
                m = jnp.max(sc, axis=1, keepdims=True)
                pe = jnp.exp(sc - m)
                l = jnp.sum(pe, axis=1, keepdims=True)
                probs.append((pe.astype(BF16), l, m))
            for p in pairs:
                lanes = lanes_of[p]
                pb, l, m = probs[p]
                if first_tile:
                    o2 = _nn(pb, vc_ref[0:n_cur, lanes]) / l
                else:
                    o2 = (_nn(pb[:, 0:n_prev], vp_ref[r0:TQ, lanes]) + _nn(pb[:, n_prev:KW], vc_ref[0:n_cur, lanes])) / l
                lse2 = m + jnp.log(l)
                lse_ref[r0:r0 + QB, 2 * p:2 * p + 1] = lse2[0:QB, :]
                lse_ref[r0:r0 + QB, 2 * p + 1:2 * p + 2] = lse2[QB:2 * QB, :]
                o_ref[r0:r0 + QB, lanes] = jnp.where(lane_hi, o2[QB:2 * QB, :], o2[0:QB, :]).astype(BF16)

        @pl.when(i == 0)
        def _():
            for b in range(TQ // QB):
                block(b, True)

        @pl.when(i > 0)
        def _():
            for b in range(TQ // QB):
                block(b, False)

    blk = lambda grp, prev: pl.BlockSpec(
        (None, TQ, D_ATT), (lambda i: (grp, jnp.maximum(i - 1, 0), 0)) if prev else (lambda i: (grp, i, 0)))
    vmem = pl.BlockSpec(memory_space=pltpu.VMEM)
    return pl.pallas_call(
        body, name="attn_fwd", grid=(n,),
        in_specs=[pl.BlockSpec((HEADS, DIAG), lambda i: (0, 0)),
                  blk(0, False), blk(1, True), blk(1, False), blk(2, True), blk(2, False)] + [vmem] * (nw + 1),
        out_specs=[pl.BlockSpec((TQ, D_ATT), lambda i: (i, 0)), pl.BlockSpec((TQ, HEADS), lambda i: (i, 0))]
        + [ANY] * (nw + 1),
        out_shape=[jax.ShapeDtypeStruct((s, D_ATT), BF16), jax.ShapeDtypeStruct((s, HEADS), F32)]
        + _gather_out_shapes(shards, kinds, cw8),
        scratch_shapes=[pltpu.VMEM((HEADS // 2, 2 * QB, KW), F32)] + [pltpu.VMEM(a.shape, BF16) for a in shards]
        + _gather_sems(nw),
        compiler_params=_params("arbitrary"),
    )(diag, proj, proj, proj, proj, proj, *shards, cw8)


def _attn_bwd(diag, proj, d_att, att, lse, parts):
    s = proj.shape[1]
    n = s // TQ
    npart = len(parts)
    scale = HEAD_DIM ** -0.5
    rel_pad = 3 * LANES

    def body(*refs):
        diag_ref, q_ref, kp_ref, kc_ref, vp_ref, vc_ref, do_ref, o_ref, lse_ref = refs[:9]
        part_refs = refs[9:9 + npart]
        dqkv_ref, dbias_ref = refs[9 + npart:11 + npart]
        slot_refs = refs[11 + npart:11 + 2 * npart]
        bias_scr, dbias_acc, dk_acc, dv_acc, dq_scr = refs[11 + 2 * npart:16 + 2 * npart]
        start, finish = _scatter_plan(part_refs, slot_refs, *refs[16 + 2 * npart:])
        i = pl.program_id(0)
        cur, prv = i % 2, 1 - i % 2

        @pl.when(i == 0)
        def _():
            start()
            _build_bias(diag_ref, bias_scr)
            dbias_acc[...] = jnp.zeros_like(dbias_acc)
            dk_acc[...] = jnp.zeros_like(dk_acc)
            dv_acc[...] = jnp.zeros_like(dv_acc)

        @pl.when(i > 0)
        def _():
            dqkv_ref[:, 0:D_ATT] = dq_scr[...]
            dk_acc[cur] = jnp.zeros((TQ, D_ATT), F32)
            dv_acc[cur] = jnp.zeros((TQ, D_ATT), F32)

        lane_hi = lax.broadcasted_iota(jnp.int32, (QB, LANES), 1) >= HEAD_DIM
        col = lax.broadcasted_iota(jnp.int32, (2 * QB, KW), 1)

        def make_block(first_tile):
            def block(b):
                r0, n_prev = b * QB, TQ - b * QB
                n_cur = KW - n_prev
                for p in range(HEADS // 2):
                    lanes = slice(LANES * p, LANES * (p + 1))
                    q2 = _stack_heads(q_ref[r0:r0 + QB, lanes] * scale, lane_hi)
                    kw = jnp.concatenate([kp_ref[r0:TQ, lanes], kc_ref[0:n_cur, lanes]], axis=0)
                    vw = jnp.concatenate([vp_ref[r0:TQ, lanes], vc_ref[0:n_cur, lanes]], axis=0)
                    dop = do_ref[r0:r0 + QB, lanes]
                    do2 = _stack_heads(dop, lane_hi)
                    prod = dop.astype(F32) * o_ref[r0:r0 + QB, lanes].astype(F32)
                    delta2 = jnp.concatenate(
                        [jnp.sum(jnp.where(lane_hi, 0.0, prod), axis=1, keepdims=True),
                         jnp.sum(jnp.where(lane_hi, prod, 0.0), axis=1, keepdims=True)], axis=0)
                    lse2 = jnp.concatenate([lse_ref[r0:r0 + QB, 2 * p:2 * p + 1],
                                            lse_ref[r0:r0 + QB, 2 * p + 1:2 * p + 2]], axis=0)
                    sc = _nt(q2, kw) + bias_scr[p]
                    if first_tile:
                        sc = jnp.where(col >= TQ - r0, sc, NEG_BIG)
                    pr = jnp.exp(sc - lse2)
                    ds = pr * (_nt(do2, vw) - delta2)
                    dbias_acc[p] += ds
                    dsb = ds.astype(BF16)
                    dv_w = _tn(pr.astype(BF16), do2)
                    dk_w = _tn(dsb, q2)
                    dv_acc[prv, r0:TQ, lanes] += dv_w[0:n_prev, :]
                    dv_acc[cur, 0:n_cur, lanes] += dv_w[n_prev:KW, :]
                    dk_acc[prv, r0:TQ, lanes] += dk_w[0:n_prev, :]
                    dk_acc[cur, 0:n_cur, lanes] += dk_w[n_prev:KW, :]
                    dq2 = _nn(dsb, kw)
                    dq = jnp.where(lane_hi, dq2[QB:2 * QB, :], dq2[0:QB, :]) * scale
                    dq_scr[r0:r0 + QB, lanes] = dq.astype(BF16)
            return block

        @pl.when(i == 0)
        def _():
            for b in range(TQ // QB):
                make_block(True)(b)

        @pl.when((i > 0) & (i < n))
        def _():
            for b in range(TQ // QB):
                make_block(False)(b)

        @pl.when(i > 0)
        def _():
            dqkv_ref[:, D_ATT:2 * D_ATT] = dk_acc[prv].astype(BF16)
            dqkv_ref[:, 2 * D_ATT:3 * D_ATT] = dv_acc[prv].astype(BF16)

        @pl.when(i == n)
        def _():
            d_iota = lax.broadcasted_iota(jnp.int32, (DIAG, rel_pad), 0)
            n_iota = lax.broadcasted_iota(jnp.int32, (DIAG, rel_pad), 1)
            diff = jnp.where(d_iota < KW, d_iota, d_iota - DIAG)
            idx = jnp.clip(N_LEFT * CHUNK - diff, -MAX_REL, MAX_REL) + MAX_REL
            onehot = (idx == n_iota).astype(F32)
            rows = []
            for hd in range(HEADS):
                acc = dbias_acc[hd // 2, (hd % 2) * QB:(hd % 2 + 1) * QB, :]
                a = jnp.concatenate([acc, jnp.zeros((QB, DIAG - KW), F32)], axis=1)
                g8 = a[0:SUBLANES, :]
                for blk in range(1, QB // SUBLANES):
                    g8 = g8 + pltpu.roll(a[blk * SUBLANES:(blk + 1) * SUBLANES, :], DIAG - blk * SUBLANES, 1)
                g1 = g8[0:1, :]
                for r in range(1, SUBLANES):
                    g1 = g1 + pltpu.roll(g8[r:r + 1, :], DIAG - r, 1)
                rows.append(g1)
            g = jnp.concatenate(rows, axis=0)
            dbias_ref[...] = jnp.dot(g, onehot, preferred_element_type=F32, precision=lax.Precision.HIGHEST)
            finish()

    last = n - 1
    cur = lambda grp: pl.BlockSpec((None, TQ, D_ATT), lambda i: (grp, jnp.minimum(i, last), 0))
    prev = lambda grp: pl.BlockSpec((None, TQ, D_ATT), lambda i: (grp, jnp.maximum(jnp.minimum(i, last) - 1, 0), 0))
    tile = pl.BlockSpec((TQ, D_ATT), lambda i: (jnp.minimum(i, last), 0))
    return pl.pallas_call(
        body, name="attn_bwd", grid=(n + 1,),
        in_specs=[pl.BlockSpec((HEADS, DIAG), lambda i: (0, 0)),
                  cur(0), prev(1), cur(1), prev(2), cur(2), tile, tile,
                  pl.BlockSpec((TQ, HEADS), lambda i: (jnp.minimum(i, last), 0))] + [ANY] * npart,
        out_specs=[pl.BlockSpec((TQ, 3 * D_ATT), lambda i: (jnp.maximum(i - 1, 0), 0)),
                   pl.BlockSpec((HEADS, rel_pad), lambda i: (0, 0))] + [ANY] * npart,
        out_shape=[jax.ShapeDtypeStruct((s, 3 * D_ATT), BF16), jax.ShapeDtypeStruct((HEADS, rel_pad), F32)]
        + _scatter_out_shapes(parts),
        scratch_shapes=[pltpu.VMEM((HEADS // 2, 2 * QB, KW), F32), pltpu.VMEM((HEADS // 2, 2 * QB, KW), F32),
                        pltpu.VMEM((2, TQ, D_ATT), F32), pltpu.VMEM((2, TQ, D_ATT), F32),
                        pltpu.VMEM((TQ, D_ATT), BF16)] + _scatter_sems(npart),
        compiler_params=_params("arbitrary"),
    )(diag, proj, proj, proj, proj, proj, d_att, att, lse, *parts)


def _shift_down(a, k, halo):
    rolled = pltpu.roll(a, k, 0)
    row = lax.broadcasted_iota(jnp.int32, halo.shape, 0)
    first = jnp.where(row < k, pltpu.roll(halo, k, 0), rolled[0:SUBLANES, :])
    return jnp.concatenate([first, rolled[SUBLANES:, :]], axis=0)


def _shift_up(a, k, nxt):
    tm = a.shape[0]
    rolled = pltpu.roll(a, tm - k, 0)
    row = lax.broadcasted_iota(jnp.int32, nxt.shape, 0)
    last = jnp.where(row >= SUBLANES - k, pltpu.roll(nxt, SUBLANES - k, 0), rolled[tm - SUBLANES:, :])
    return jnp.concatenate([rolled[:tm - SUBLANES, :], last], axis=0)


def _sigmoid(v):
    return 0.5 * jnp.tanh(0.5 * v) + 0.5


def _mixer_mid(att, proj, x, tgt, w_att, w_conv, w_out, conv_w8, conv_b, fin_g):
    s, d = x.shape
    dc = D_ATT
    n = s // TM_MID
    tm = TM_MID
    n_shards = 4

    def body(att_ref, za_ref, gb_ref, gc_ref, u_ref, zc_ref, hgc_ref, hu_ref, gatt_ref, gconv_ref, x_ref, t_ref,
             watt_ref, wconv_ref, wout_ref, cw_ref, cb_ref, fg_ref,
             dpb_ref, do_ref, dx2_ref, gatt_o, gconv_o, gout_o, loss_o, gfn_o, gcb_o, gcw_o,
             acc_att, acc_conv, acc_out, carry, watt_t_ref, wconv_t_ref, wout_t_ref):
        i = pl.program_id(0)
        tile = n - 1 - i

        @pl.when(i == 0)
        def _():
            acc_att[...] = jnp.zeros_like(acc_att)
            acc_conv[...] = jnp.zeros_like(acc_conv)
            acc_out[...] = jnp.zeros_like(acc_out)
            carry[...] = jnp.zeros_like(carry)
            loss_o[...] = jnp.zeros_like(loss_o)
            gfn_o[...] = jnp.zeros_like(gfn_o)
            gcb_o[...] = jnp.zeros_like(gcb_o)
            gcw_o[...] = jnp.zeros_like(gcw_o)
            watt_t_ref[...] = watt_ref[...].T
            wconv_t_ref[...] = wconv_ref[...].T
            wout_t_ref[...] = wout_ref[...].T

        halves = [slice(hh * (tm // 2), (hh + 1) * (tm // 2)) for hh in range(2)]
        both = lambda fn: [fn(rows) for rows in halves]
        f32 = lambda ref, rows: ref[rows, :].astype(F32)

        gc = gc_ref[...].astype(F32)
        u = u_ref[...].astype(F32)
        cu = gc * u
        halo = jnp.where(tile > 0, hgc_ref[...].astype(F32) * hu_ref[...].astype(F32), 0.0)
        cu1 = _shift_down(cu, 1, halo)
        cu2 = _shift_down(cu, 2, halo)
        w0, w1, w2 = cw_ref[0:1, :], cw_ref[1:2, :], cw_ref[2:3, :]
        fg = fg_ref[...]

        def stage_a(rows):
            att_v, za, zc, gb = f32(att_ref, rows), f32(za_ref, rows), f32(zc_ref, rows), f32(gb_ref, rows)
            sa = _sigmoid(za)
            silu_a = za * sa
            vconv = w0 * cu2[rows, :] + w1 * cu1[rows, :] + w2 * cu[rows, :] + cb_ref[...]
            sc = _sigmoid(zc)
            silu_c = zc * sc
            return dict(att_v=att_v, za=za, zc=zc, gb=gb, sa=sa, silu_a=silu_a, vconv=vconv, sc=sc, silu_c=silu_c,
                        a_b=(att_v * silu_a).astype(BF16), c_b=(gb * vconv * silu_c).astype(BF16))

        st = both(stage_a)
        for t in st:
            t["y_att"] = _nn(t["a_b"], watt_ref[...])
            t["y_conv"] = _nn(t["c_b"], wconv_ref[...])
        for t, rows in zip(st, halves):
            gpair = lambda ref: jnp.concatenate([ref[0, rows, :], ref[1, rows, :]], axis=1).astype(F32)
            t["ga"] = _sigmoid(gpair(gatt_ref))
            t["gv"] = _sigmoid(gpair(gconv_ref))
            t["m_b"] = (t["ga"] * t["y_att"] + t["gv"] * t["y_conv"]).astype(BF16)
        for t in st:
            t["mo"] = _nn(t["m_b"], wout_ref[...])
        for t, rows in zip(st, halves):
            x2 = x_ref[rows, :] + t["mo"]
            r2 = lax.rsqrt(jnp.mean(x2 * x2, axis=-1, keepdims=True) + EPS)
            x2n = x2 * r2
            err = x2n * fg - t_ref[rows, :]
            loss_o[...] += jnp.sum(err * err, axis=0, keepdims=True) * (0.5 / d)
            dy = err * (1.0 / d)
            gfn_o[...] += jnp.sum(dy * x2n, axis=0, keepdims=True)
            dyn = dy * fg
            dx2 = r2 * (dyn - x2n * jnp.mean(dyn * x2n, axis=-1, keepdims=True))
            dx2_ref[rows, :] = dx2
            t["dx2_b"] = dx2.astype(BF16)
        for t in st:
            t["dm"] = _nn(t["dx2_b"], wout_t_ref[...])
        whole = lambda key: jnp.concatenate([st[0][key], st[1][key]], axis=0)
        acc_out[...] += _tn(whole("m_b"), whole("dx2_b"))
        for t, rows in zip(st, halves):
            dy_att = t["dm"] * t["ga"]
            dy_conv = t["dm"] * t["gv"]
            dpb_ref[rows, 5 * dc:5 * dc + d] = (dy_att * t["y_att"] * (1.0 - t["ga"])).astype(BF16)
            dpb_ref[rows, 5 * dc + d:5 * dc + 2 * d] = (dy_conv * t["y_conv"] * (1.0 - t["gv"])).astype(BF16)
            t["dya_b"] = dy_att.astype(BF16)
            t["dyc_b"] = dy_conv.astype(BF16)
        for t in st:
            t["da_in"] = _nn(t["dya_b"], watt_t_ref[...])
            t["dc_in"] = _nn(t["dyc_b"], wconv_t_ref[...])
        acc_att[...] += _tn(whole("a_b"), whole("dya_b"))
        acc_conv[...] += _tn(whole("c_b"), whole("dyc_b"))
        for t, rows in zip(st, halves):
            sa, za, sc, zc = t["sa"], t["za"], t["sc"], t["zc"]
            do_ref[rows, :] = (t["da_in"] * t["silu_a"]).astype(BF16)
            dpb_ref[rows, 0:dc] = (t["da_in"] * t["att_v"] * (sa * (1.0 + za * (1.0 - sa)))).astype(BF16)
            dpb_ref[rows, dc:2 * dc] = (t["dc_in"] * t["vconv"] * t["silu_c"]).astype(BF16)
            dgs = t["dc_in"] * t["gb"]
            t["dvc"] = dgs * t["silu_c"]
            dpb_ref[rows, 4 * dc:5 * dc] = (dgs * t["vconv"] * (sc * (1.0 + zc * (1.0 - sc)))).astype(BF16)
        dvc = whole("dvc")
        gcb_o[...] += jnp.sum(dvc, axis=0, keepdims=True)
        gcw_o[0:1, :] += jnp.sum(dvc * cu2, axis=0, keepdims=True)
        gcw_o[1:2, :] += jnp.sum(dvc * cu1, axis=0, keepdims=True)
        gcw_o[2:3, :] += jnp.sum(dvc * cu, axis=0, keepdims=True)
        nxt = carry[...]
        dcu = w2 * dvc + w1 * _shift_up(dvc, 1, nxt) + w0 * _shift_up(dvc, 2, nxt)
        carry[...] = dvc[0:SUBLANES, :]
        dpb_ref[:, 2 * dc:3 * dc] = (dcu * u).astype(BF16)
        dpb_ref[:, 3 * dc:4 * dc] = (dcu * gc).astype(BF16)

        @pl.when(i == n - 1)
        def _():
            for j in range(n_shards):
                gatt_o[j] = acc_att[:, j * (d // n_shards):(j + 1) * (d // n_shards)].astype(BF16)
                gconv_o[j] = acc_conv[:, j * (d // n_shards):(j + 1) * (d // n_shards)].astype(BF16)
                gout_o[j] = acc_out[j * (d // n_shards):(j + 1) * (d // n_shards), :].astype(BF16)

    rev = lambda width, col_blk: pl.BlockSpec((tm, width), lambda i: (n - 1 - i, col_blk))
    grp = lambda g: pl.BlockSpec((None, tm, dc), lambda i: (g, n - 1 - i, 0))
    grp2 = lambda g2: pl.BlockSpec((2, tm, dc), lambda i: (g2, n - 1 - i, 0))
    halo_spec = lambda g: pl.BlockSpec(
        (None, SUBLANES, dc), lambda i: (g, jnp.maximum((n - 1 - i) * (tm // SUBLANES) - 1, 0), 0))
    const = lambda shape: pl.BlockSpec(shape, lambda i: tuple(0 for _ in shape), pipeline_mode=pl.Buffered(1))
    q4 = d // n_shards
    return pl.pallas_call(
        body, name="mixer_mid", grid=(n,),
        in_specs=[rev(dc, 0), grp(3), grp(4), grp(5), grp(6), grp(7),
                  halo_spec(5), halo_spec(6), grp2(4), grp2(5), rev(d, 0), rev(d, 0),
                  const((dc, d)), const((dc, d)), const((d, d)),
                  const((SUBLANES, dc)), const((1, dc)), const((1, d))],
        out_specs=[rev(5 * dc + 2 * d, 0), rev(dc, 0), rev(d, 0),
                   const((n_shards, dc, q4)), const((n_shards, dc, q4)), const((n_shards, q4, d)),
                   const((1, d)), const((1, d)), const((1, dc)), const((SUBLANES, dc))],
        out_shape=[jax.ShapeDtypeStruct((s, 5 * dc + 2 * d), BF16), jax.ShapeDtypeStruct((s, dc), BF16),
                   jax.ShapeDtypeStruct((s, d), F32),
                   jax.ShapeDtypeStruct((n_shards, dc, q4), BF16), jax.ShapeDtypeStruct((n_shards, dc, q4), BF16),
                   jax.ShapeDtypeStruct((n_shards, q4, d), BF16),
                   jax.ShapeDtypeStruct((1, d), F32), jax.ShapeDtypeStruct((1, d), F32),
                   jax.ShapeDtypeStruct((1, dc), F32), jax.ShapeDtypeStruct((SUBLANES, dc), F32)],
        scratch_shapes=[pltpu.VMEM((dc, d), F32), pltpu.VMEM((dc, d), F32), pltpu.VMEM((d, d), F32),
                        pltpu.VMEM((SUBLANES, dc), F32),
                        pltpu.VMEM((d, dc), BF16), pltpu.VMEM((d, dc), BF16), pltpu.VMEM((d, d), BF16)],
        compiler_params=_params("arbitrary"),
    )(att, proj, proj, proj, proj, proj, proj, proj, proj, proj, x, tgt,
      w_att, w_conv, w_out, conv_w8, conv_b, fin_g)


def _in_proj_bwd_x(dqkv, dpb, w_in, x, dx2, g):
    s, d = x.shape
    tn = dqkv.shape[1]
    nb = dpb.shape[1] // tn
    n = s // TM_MM

    def body(*refs):
        dps, ws = refs[:nb + 1], refs[nb + 1:2 * nb + 2]
        x_ref, dx2_ref, g_ref, gx_ref, gng_ref = refs[2 * nb + 2:]
        i = pl.program_id(0)

        @pl.when(i == 0)
        def _():
            gng_ref[...] = jnp.zeros_like(gng_ref)

        dh = _nt(dps[0][...], ws[0][0])
        for j in range(1, nb + 1):
            dh = dh + _nt(dps[j][...], ws[j][0])
        xv = x_ref[...]
        r = lax.rsqrt(jnp.mean(xv * xv, axis=-1, keepdims=True) + EPS)
        xn = xv * r
        gng_ref[...] += jnp.sum(dh * xn, axis=0, keepdims=True)
        dhn = dh * g_ref[...]
        gx_ref[...] = dx2_ref[...] + r * (dhn - xn * jnp.mean(dhn * xn, axis=-1, keepdims=True))

    tile = lambda width, col_blk: pl.BlockSpec((TM_MM, width), lambda i: (i, col_blk))
    wspec = lambda blk: pl.BlockSpec((1, d, tn), lambda i: (blk, 0, 0), pipeline_mode=pl.Buffered(1))
    return pl.pallas_call(
        body, name="in_proj_bwd_x", grid=(n,),
        in_specs=[tile(tn, 0)] + [tile(tn, j) for j in range(nb)] + [wspec(j) for j in range(nb + 1)]
        + [tile(d, 0), tile(d, 0), pl.BlockSpec((1, d), lambda i: (0, 0))],
        out_specs=[tile(d, 0), pl.BlockSpec((1, d), lambda i: (0, 0))],
        out_shape=[jax.ShapeDtypeStruct((s, d), F32), jax.ShapeDtypeStruct((1, d), F32)],
        compiler_params=_params("arbitrary"),
    )(dqkv, *([dpb] * nb), *([w_in] * (nb + 1)), x, dx2, g)


def _in_proj_bwd_w(h, dqkv, dpb, order):
    s, d = h.shape
    tn = dqkv.shape[1]
    n = s // TM_BLK
    hr = d // 2
    settle = min(2, n - 1)

    def body(order_ref, h_ref, da_ref, db_ref, slots_ref, acc, sendbuf, pairbuf, chipbuf, psend, precv, send, recv, lsem):
        j, i = pl.program_id(0), pl.program_id(1)
        blk = order_ref[j]
        pos = _position()
        x, y, c = pos

        @pl.when(i == 0)
        def _():
            acc[...] = jnp.zeros_like(acc)

        @pl.when(blk == 0)
        def _():
            acc[...] += _tn(h_ref[...], da_ref[...])

        @pl.when(blk > 0)
        def _():
            acc[...] += _tn(h_ref[...], db_ref[...])

        def pair(step, half):
            return pltpu.make_async_remote_copy(
                src_ref=sendbuf.at[step, pl.ds(half * hr, hr), :], dst_ref=pairbuf.at[step],
                send_sem=psend.at[step], recv_sem=precv.at[step], device_id=(x, y, 1 - c), device_id_type=MESH)

        def ici(step):
            flip = OWNER_FLIPS[step]
            return pltpu.make_async_remote_copy(
                src_ref=chipbuf.at[step], dst_ref=slots_ref.at[flip], send_sem=send.at[step], recv_sem=recv.at[step],
                device_id=_peer(pos, 4 * (flip >> 1) + 2 * (flip & 1)), device_id_type=MESH)

        local = pltpu.make_async_copy(chipbuf.at[N_CHIPS - 1], slots_ref.at[0], lsem.at[0])

        def combine(step):
            pair(step, c).wait_recv()
            mine = sendbuf[step, pl.ds(c * hr, hr), :].astype(F32)
            chipbuf[step] = (mine + pairbuf[step].astype(F32)).astype(BF16)

        for step in range(N_CHIPS):
            @pl.when((j == step) & (i == n - 1))
            def _(step=step):
                sendbuf[step] = acc[...].astype(BF16)
                pair(step, 1 - c).start()

        for step in range(N_CHIPS - 1):
            @pl.when((j == step + 1) & (i == settle))
            def _(step=step):
                combine(step)
                ici(step).start()

        @pl.when((j == N_CHIPS - 1) & (i == n - 1))
        def _():
            combine(N_CHIPS - 1)
            local.start()
            for step in range(N_CHIPS - 1):
                ici(step).wait_recv()
            for step in range(N_CHIPS - 1):
                ici(step).wait_send()
            for step in range(N_CHIPS):
                pair(step, 1 - c).wait_send()
            local.wait()

    return pl.pallas_call(
        body, name="in_proj_bwd_w",
        grid_spec=pltpu.PrefetchScalarGridSpec(
            num_scalar_prefetch=1, grid=(N_CHIPS, n),
            in_specs=[pl.BlockSpec((TM_BLK, d), lambda j, i, order: (i, 0)),
                      pl.BlockSpec((TM_BLK, tn), lambda j, i, order: (jnp.where(order[j] == 0, i, 0), 0)),
                      pl.BlockSpec((TM_BLK, tn), lambda j, i, order: (jnp.where(order[j] == 0, 0, i),
                                                                     jnp.maximum(order[j] - 1, 0)))],
            out_specs=[ANY],
            scratch_shapes=[pltpu.VMEM((d, tn), F32), pltpu.VMEM((N_CHIPS, d, tn), BF16),
                            pltpu.VMEM((N_CHIPS, hr, tn), BF16), pltpu.VMEM((N_CHIPS, hr, tn), BF16),
                            pltpu.SemaphoreType.DMA((N_CHIPS,)), pltpu.SemaphoreType.DMA((N_CHIPS,)),
                            pltpu.SemaphoreType.DMA((N_CHIPS - 1,)), pltpu.SemaphoreType.DMA((N_CHIPS - 1,)),
                            pltpu.SemaphoreType.DMA((1,))]),
        out_shape=[jax.ShapeDtypeStruct((N_CHIPS, hr, tn), BF16)],
        compiler_params=_params("arbitrary", "arbitrary"),
    )(order, h, dqkv, dpb)[0]


LOSS_ROW = 6


def _adam_update(w, g, m, v):
    c1 = 1.0 / (1.0 - ADAM_B1 ** ADAM_STEP)
    c2 = 1.0 / (1.0 - ADAM_B2 ** ADAM_STEP)
    m2 = ADAM_B1 * m + (1.0 - ADAM_B1) * g
    v2 = ADAM_B2 * v + (1.0 - ADAM_B2) * (g * g)
    return -ADAM_LR * ((m2 * c1) / (jnp.sqrt(v2 * c2) + ADAM_EPS) + ADAM_WD * w), m2, v2


def _adamw_small(recv, params, moments_m, moments_v):
    k = recv.shape[0]
    n_par = len(params)
    cshard = params[3].shape[1]

    def body(*refs):
        r_ref = refs[0]
        ws, ms, vs = refs[1:1 + n_par], refs[1 + n_par:1 + 2 * n_par], refs[1 + 2 * n_par:1 + 3 * n_par]
        loss_ref = refs[1 + 3 * n_par]
        outs = refs[2 + 3 * n_par:]
        total = r_ref[0]
        for slot in range(1, k):
            total = total + r_ref[slot]
        loss_ref[...] = jnp.sum(total[LOSS_ROW:LOSS_ROW + 1, :], axis=1, keepdims=True)
        chip = 2 * lax.axis_index("x") + lax.axis_index("y")
        g_cw = jnp.zeros((3, cshard), F32)
        for sh in range(N_CHIPS):
            g_cw = g_cw + jnp.where(chip == sh, total[3:6, sh * cshard:(sh + 1) * cshard], 0.0)
        grads = [total[0:1, :], total[1:2, :], total[2:3, :ws[2].shape[1]], g_cw, total[8:16, :ws[4].shape[1]]]
        for p in range(n_par):
            delta, m2, v2 = _adam_update(ws[p][...], grads[p], ms[p][...], vs[p][...])
            for q, val in enumerate((grads[p], delta, m2, v2)):
                outs[4 * p + q][...] = val

    shapes = [jax.ShapeDtypeStruct((1, 1), F32)]
    for p in params:
        shapes += [jax.ShapeDtypeStruct(p.shape, F32)] * 4
    return pl.pallas_call(body, name="adamw_small", out_shape=shapes)(recv, *params, *moments_m, *moments_v)


def _adamw_group(ws, gs, ms, vs, name, steps):
    k = len(ws)

    def body(*refs):
        ins, outs = refs[:4 * k], refs[4 * k:]
        for j in range(k):
            gv = ins[k + j][...]
            outs[4 * j][...] = gv
            outs[4 * j + 1][...], outs[4 * j + 2][...], outs[4 * j + 3][...] = _adam_update(
                ins[j][...], gv, ins[2 * k + j][...], ins[3 * k + j][...])

    specs = [pl.BlockSpec((w.shape[0] // steps, w.shape[1]), lambda i: (i, 0)) for w in ws]
    out = pl.pallas_call(
        body, name=name, grid=(steps,),
        in_specs=specs * 4, out_specs=[spec for spec in specs for _ in range(4)],
        out_shape=[jax.ShapeDtypeStruct(w.shape, F32) for w in ws for _ in range(4)],
        compiler_params=_params("parallel"),
    )(*ws, *gs, *ms, *vs)
    return [tuple(out[4 * j:4 * j + 4]) for j in range(k)]


ANY = pl.BlockSpec(memory_space=pl.ANY)
N_CHIPS = 4
N_DEV = 8
OWNER_FLIPS = (3, 1, 2, 0)


def _position():
    return lax.axis_index("x"), lax.axis_index("y"), lax.axis_index("c")


def _gather_out_shapes(shards, kinds, cw8):
    full = [(a.shape[0], a.shape[1] * N_CHIPS) if k == "cols" else (a.shape[0] * N_CHIPS, a.shape[1])
            for a, k in zip(shards, kinds)]
    return [jax.ShapeDtypeStruct(f, BF16) for f in full] + [
        jax.ShapeDtypeStruct((N_CHIPS,) + cw8.shape, cw8.dtype)]


def _gather_sems(nw):
    return [pltpu.SemaphoreType.DMA((3, nw)), pltpu.SemaphoreType.DMA((3, nw)),
            pltpu.SemaphoreType.DMA((3, nw)), pltpu.SemaphoreType.DMA((3, nw)),
            pltpu.SemaphoreType.DMA((3,)), pltpu.SemaphoreType.DMA((3,)), pltpu.SemaphoreType.DMA((nw + 1,))]


def _gather_plan(kinds, srcs, dsts, cw, cw_all, send1, recv1, send2, recv2, ssend, srecv, lsem):
    nw = len(srcs)
    x, y, c = _position()
    mine = 2 * x + y
    chips = [(x, 1 - y), (1 - x, y), (1 - x, 1 - y)]

    def window(w, shard, half):
        r, cc = srcs[w].shape
        hr = r // 2
        if kinds[w] == "cols":
            rows = pl.ds(0, r) if half is None else pl.ds(half * hr, hr)
            return dsts[w].at[rows, pl.ds(shard * cc, cc)]
        rows = pl.ds(shard * r, r) if half is None else pl.ds(shard * r + half * hr, hr)
        return dsts[w].at[rows, :]

    def my_half(w):
        hr = srcs[w].shape[0] // 2
        return srcs[w].at[pl.ds(c * hr, hr), :]

    def local():
        return [pltpu.make_async_copy(srcs[w], window(w, mine, None), lsem.at[w]) for w in range(nw)] + [
            pltpu.make_async_copy(cw, cw_all.at[mine], lsem.at[nw])]

    def ici(k, w, shard):
        kx, ky = chips[k]
        return pltpu.make_async_remote_copy(
            src_ref=my_half(w), dst_ref=window(w, shard, c), send_sem=send1.at[k, w], recv_sem=recv1.at[k, w],
            device_id=(kx, ky, c), device_id_type=MESH)

    def d2d(k, w, shard, half):
        return pltpu.make_async_remote_copy(
            src_ref=window(w, shard, half), dst_ref=window(w, shard, half),
            send_sem=send2.at[k, w], recv_sem=recv2.at[k, w], device_id=(x, y, 1 - c), device_id_type=MESH)

    def small(k, shard):
        kx, ky = chips[k]
        return pltpu.make_async_remote_copy(
            src_ref=cw, dst_ref=cw_all.at[shard], send_sem=ssend.at[k], recv_sem=srecv.at[k],
            device_id=(kx, ky, c), device_id_type=MESH)

    def theirs(k):
        kx, ky = chips[k]
        return 2 * kx + ky

    def start():
        for cp in local():
            cp.start()
        for k in range(3):
            for w in range(nw):
                ici(k, w, mine).start()
            small(k, mine).start()

    def forward():
        for k in range(3):
            for w in range(nw):
                ici(k, w, theirs(k)).wait_recv()
                d2d(k, w, theirs(k), c).start()

    def finish():
        for k in range(3):
            for w in range(nw):
                d2d(k, w, theirs(k), 1 - c).wait_recv()
            small(k, theirs(k)).wait_recv()
        for k in range(3):
            for w in range(nw):
                ici(k, w, mine).wait_send()
                d2d(k, w, theirs(k), c).wait_send()
            small(k, mine).wait_send()
        for cp in local():
            cp.wait()

    return start, forward, finish


def _scatter_out_shapes(parts):
    return [jax.ShapeDtypeStruct((N_DEV, p.shape[1] // 2, p.shape[2]), p.dtype) for p in parts]


def _scatter_sems(nw):
    return [pltpu.SemaphoreType.DMA((N_DEV, nw)), pltpu.SemaphoreType.DMA((N_DEV, nw)), pltpu.SemaphoreType.DMA((nw,))]


def _peer(pos, k):
    x, y, c = pos
    return ((1 - x) if k & 4 else x, (1 - y) if k & 2 else y, (1 - c) if k & 1 else c)


def _scatter_plan(srcs, dsts, send, recv, lsem):
    nw = len(srcs)
    pos = _position()

    def piece(w, k):
        px, py, pc = _peer(pos, k)
        hr = srcs[w].shape[1] // 2
        return srcs[w].at[2 * px + py, pl.ds(pc * hr, hr), :]

    def remote(w, k):
        return pltpu.make_async_remote_copy(
            src_ref=piece(w, k), dst_ref=dsts[w].at[k], send_sem=send.at[k, w], recv_sem=recv.at[k, w],
            device_id=_peer(pos, k), device_id_type=MESH)

    def local(w):
        return pltpu.make_async_copy(piece(w, 0), dsts[w].at[0], lsem.at[w])

    def start():
        for w in range(nw):
            local(w).start()
        for k in range(1, N_DEV):
            for w in range(nw):
                remote(w, k).start()

    def finish():
        for k in range(1, N_DEV):
            for w in range(nw):
                remote(w, k).wait_recv()
        for k in range(1, N_DEV):
            for w in range(nw):
                remote(w, k).wait_send()
        for w in range(nw):
            local(w).wait()

    return start, finish


def _reduce_pair(slots, small):
    nw = len(slots)
    ns = len(small)

    def body(*refs):
        srcs, parts = refs[:nw], refs[nw:nw + ns]
        dsts, sm_all = refs[nw + ns:2 * nw + ns], refs[2 * nw + ns]
        halves, sm = refs[2 * nw + ns + 1:3 * nw + ns + 1], refs[3 * nw + ns + 1]
        send, recv, ssend, srecv, lsem = refs[3 * nw + ns + 2:]
        pos = _position()
        x, y, c = pos
        me = 4 * x + 2 * y + c
        _pack_small(sm, *parts)

        def rows(w, half):
            hr = halves[w].shape[0]
            return dsts[w].at[pl.ds(half * hr, hr), :]

        def remote(w, half):
            return pltpu.make_async_remote_copy(
                src_ref=halves[w], dst_ref=rows(w, half), send_sem=send.at[w], recv_sem=recv.at[w],
                device_id=(x, y, 1 - c), device_id_type=MESH)

        def bcast(k, slot):
            return pltpu.make_async_remote_copy(
                src_ref=sm, dst_ref=sm_all.at[slot], send_sem=ssend.at[k], recv_sem=srecv.at[k],
                device_id=_peer(pos, k), device_id_type=MESH)

        small_copies = [bcast(k, me) for k in range(1, N_DEV)]
        own_small = pltpu.make_async_copy(sm, sm_all.at[me], lsem.at[nw])
        for cp in small_copies + [own_small]:
            cp.start()
        big = []
        for w in range(nw):
            total = srcs[w][0].astype(F32)
            for k in range(1, srcs[w].shape[0]):
                total = total + srcs[w][k].astype(F32)
            halves[w][...] = total
            big += [remote(w, c), pltpu.make_async_copy(halves[w], rows(w, c), lsem.at[w])]
            big[-2].start()
            big[-1].start()
        for w in range(nw):
            remote(w, 1 - c).wait_recv()
        for k in range(1, N_DEV):
            px, py, pc = _peer(pos, k)
            bcast(k, 4 * px + 2 * py + pc).wait_recv()
        for w in range(nw):
            big[2 * w].wait_send()
            big[2 * w + 1].wait()
        for cp in small_copies:
            cp.wait_send()
        own_small.wait()

    vmem = pl.BlockSpec(memory_space=pltpu.VMEM)
    half_shapes = [(sl.shape[1], sl.shape[2]) for sl in slots]
    return pl.pallas_call(
        body, name="reduce_pair",
        in_specs=[vmem] * (nw + ns), out_specs=[ANY] * (nw + 1),
        out_shape=[jax.ShapeDtypeStruct((2 * r, cc), F32) for r, cc in half_shapes]
        + [jax.ShapeDtypeStruct((N_DEV, SMALL_ROWS, SMALL_COLS), F32)],
        scratch_shapes=[pltpu.VMEM(hs, F32) for hs in half_shapes] + [pltpu.VMEM((SMALL_ROWS, SMALL_COLS), F32)]
        + [pltpu.SemaphoreType.DMA((nw,)), pltpu.SemaphoreType.DMA((nw,)),
           pltpu.SemaphoreType.DMA((N_DEV,)), pltpu.SemaphoreType.DMA((N_DEV,)),
           pltpu.SemaphoreType.DMA((nw + 1,))],
        compiler_params=pltpu.CompilerParams(vmem_limit_bytes=VMEM_LIMIT),
    )(*slots, *small)


def _pad_to(a, rows, cols):
    return jnp.pad(a, ((0, rows - a.shape[0]), (0, cols - a.shape[1])))


def _pack_small(sm, norm_g, fin_g, conv_b, conv_w, loss_vec, rel):
    sm[...] = jnp.zeros_like(sm)
    for row, ref in ((0, norm_g), (1, fin_g), (2, conv_b), (LOSS_ROW, loss_vec)):
        sm[row:row + 1, 0:ref.shape[1]] = ref[...]
    sm[3:6, 0:conv_w.shape[1]] = conv_w[0:3, :]
    sm[8:8 + HEADS, 0:rel.shape[1]] = rel[...]


def kernel(x, norm_g, w_in, rel_bias, w_att_out, conv_w, conv_b, w_conv_out, w_out, final_norm_g, loss_target, m_norm_g, m_w_in, m_rel_bias, m_w_att_out, m_conv_w, m_conv_b, m_w_conv_out, m_w_out, m_final_norm_g, v_norm_g, v_w_in, v_rel_bias, v_w_att_out, v_conv_w, v_conv_b, v_w_conv_out, v_w_out, v_final_norm_g):
    xs, tgt = x[0], loss_target[0]
    cshard = conv_w.shape[2]
    chip = 2 * lax.axis_index("x") + lax.axis_index("y")

    shards = [w_in[0], w_att_out[0], w_conv_out[0], w_out[0]]
    cw8 = _pad_to(conv_w[0], SUBLANES, cshard)
    flips = jnp.arange(N_CHIPS, dtype=jnp.int32)
    own_first = jnp.bitwise_xor(chip, flips)
    own_last = jnp.bitwise_xor(chip, jnp.asarray(OWNER_FLIPS, jnp.int32))

    proj, h, wb_in = _in_proj_gather(xs, norm_g, shards[0], own_first)
    diag = jnp.take(rel_bias[0], _diag_rel_index(), axis=1)
    att, lse, wb_att, wb_conv, wb_out, cw_all = _attn_fwd(diag, proj, shards[1:], ["cols", "cols", "rows"], cw8)
    conv_w_full = jnp.transpose(cw_all, (1, 0, 2)).reshape(SUBLANES, N_CHIPS * cshard)
    (dpb, d_att, dx2, g_att_p, g_conv_p, g_out_p, loss_vec, g_fin, g_cb, g_cw) = _mixer_mid(
        att, proj, xs, tgt, wb_att, wb_conv, wb_out, conv_w_full, conv_b,
        final_norm_g[None, :])
    dqkv, g_rel, r_att, r_conv, r_out = _attn_bwd(diag, proj, d_att, att, lse, [g_att_p, g_conv_p, g_out_p])
    grad_x, g_norm = _in_proj_bwd_x(dqkv, dpb, wb_in, xs, dx2, norm_g)
    r_in = _in_proj_bwd_w(h, dqkv, dpb, own_last)

    gw_in, gw_att, gw_conv, gw_out, r_small = _reduce_pair(
        [r_in, r_att, r_conv, r_out], [g_norm, g_fin, g_cb, g_cw, loss_vec, g_rel])
    small_out = _adamw_small(
        r_small,
        [norm_g, final_norm_g[None, :], conv_b, conv_w[0], rel_bias[0]],
        [m_norm_g, m_final_norm_g[None, :], m_conv_b, m_conv_w[0], m_rel_bias[0]],
        [v_norm_g, v_final_norm_g[None, :], v_conv_b, v_conv_w[0], v_rel_bias[0]])
    loss = small_out[0][0, 0]
    small_names = ["norm_g", "final_norm_g", "conv_b", "conv_w", "rel_bias"]
    fix = {"norm_g": lambda a: a, "final_norm_g": lambda a: a[0], "conv_b": lambda a: a,
           "conv_w": lambda a: a[None], "rel_bias": lambda a: a[None]}
    small_res = {name: [fix[name](small_out[1 + 4 * p + q]) for q in range(4)] for p, name in enumerate(small_names)}

    names = ["w_in", "w_att_out", "w_conv_out", "w_out"]
    group = _adamw_group([w_in[0], w_att_out[0], w_conv_out[0], w_out[0]], [gw_in, gw_att, gw_conv, gw_out],
                         [m_w_in[0], m_w_att_out[0], m_w_conv_out[0], m_w_out[0]],
                         [v_w_in[0], v_w_att_out[0], v_w_conv_out[0], v_w_out[0]], "adamw_matrices", ADAMW_STEPS)
    big = dict(zip(names, group))
    big = {name: tuple(a[None] for a in four) for name, four in big.items()}

    order = ["norm_g", "w_in", "rel_bias", "w_att_out", "conv_w", "conv_b", "w_conv_out", "w_out", "final_norm_g"]
    outs = [loss, grad_x[None]]
    for which in range(4):
        for name in order:
            outs.append(big[name][which] if name in big else small_res[name][which])
    return tuple(outs)
```

```python
import numpy as np
import jax
import jax.numpy as jnp
from jax import lax
from jax.experimental import pallas as pl
from jax.experimental.pallas import tpu as pltpu

F32 = jnp.float32
BF16 = jnp.bfloat16
MESH = pl.DeviceIdType.MESH

CHUNK = 64
N_LEFT = 8
HEADS = 8
HEAD_DIM = 64
D_ATT = HEADS * HEAD_DIM
MAX_REL = 128
N_REL = 2 * MAX_REL + 1
EPS = 1e-6
NEG_BIG = -1e30
ADAM_LR, ADAM_B1, ADAM_B2, ADAM_EPS, ADAM_WD, ADAM_STEP = 0.001, 0.9, 0.999, 1e-08, 0.01, 10

LANES = 128
SUBLANES = 8
VMEM_LIMIT = 56 * 1024 * 1024

QB = 2 * CHUNK
KW = N_LEFT * CHUNK + QB
DIAG = KW + QB
TQ = N_LEFT * CHUNK
TM_MID = 256
ADAMW_STEPS = 8
TM_MM = 512
TM_BLK = 1024
SMALL_ROWS, SMALL_COLS = 16, 1024


def _params(*sem):
    return pltpu.CompilerParams(dimension_semantics=sem, vmem_limit_bytes=VMEM_LIMIT)


def _nt(a, b):
    return lax.dot_general(a, b, (((1,), (1,)), ((), ())), preferred_element_type=F32)


def _tn(a, b):
    return lax.dot_general(a, b, (((0,), (0,)), ((), ())), preferred_element_type=F32)


def _nn(a, b):
    return jnp.dot(a, b, preferred_element_type=F32)


def _diag_rel_index():
    d = np.arange(DIAG)
    diff = np.where(d < KW, d, d - DIAG)
    rel = N_LEFT * CHUNK - diff
    return np.clip(rel, -MAX_REL, MAX_REL) + MAX_REL


def _build_bias(diag_ref, bias_scr):
    r = lax.broadcasted_iota(jnp.int32, (QB, KW), 0) // CHUNK
    s = lax.broadcasted_iota(jnp.int32, (QB, KW), 1) // CHUNK
    allowed = (s >= r) & (s <= r + N_LEFT)
    for h in range(HEADS):
        row = jnp.broadcast_to(diag_ref[h:h + 1, :], (QB, DIAG))
        t = pltpu.roll(row, 0, 1, stride=1, stride_axis=0)
        bias_scr[h // 2, (h % 2) * QB:(h % 2 + 1) * QB, :] = jnp.where(allowed, t[:, :KW], NEG_BIG)


def _stack_heads(a, lane_hi):
    zero = jnp.zeros_like(a)
    return jnp.concatenate([jnp.where(lane_hi, zero, a), jnp.where(lane_hi, a, zero)], axis=0)


def _in_proj_gather(x, g, shard, order):
    s, d = x.shape
    tn = shard.shape[1]
    n = s // TM_BLK
    hr = d // 2

    def body(order_ref, x_ref, g_ref, shard32_ref, proj_ref, h_ref, wfull_ref, shard_ref, hbuf, wbuf,
             send1, recv1, send2, recv2, lsem):
        del order_ref
        j, i = pl.program_id(0), pl.program_id(1)
        x, y, c = _position()
        mine = 2 * x + y
        chips = [(x, 1 - y), (1 - x, y), (1 - x, 1 - y)]

        def theirs(k):
            return 2 * chips[k][0] + chips[k][1]

        def half_rows(half):
            return pl.ds(half * hr, hr)

        def landing(k, shard_index, half):
            if k < 2:
                return wbuf.at[k, half_rows(half), :]
            return wfull_ref.at[shard_index, half_rows(half), :]

        def ici(k, shard_index):
            return pltpu.make_async_remote_copy(
                src_ref=shard_ref.at[half_rows(c), :], dst_ref=landing(k, shard_index, c),
                send_sem=send1.at[k], recv_sem=recv1.at[k], device_id=(*chips[k], c), device_id_type=MESH)

        def d2d(k, half):
            return pltpu.make_async_remote_copy(
                src_ref=wbuf.at[k % 2, half_rows(half), :], dst_ref=landing(k, theirs(k), half),
                send_sem=send2.at[k], recv_sem=recv2.at[k], device_id=(x, y, 1 - c), device_id_type=MESH)

        def load(k, half, sem):
            return pltpu.make_async_copy(wfull_ref.at[theirs(k), half_rows(half), :],
                                         wbuf.at[k % 2, half_rows(half), :], lsem.at[sem])

        def keep(k):
            return pltpu.make_async_copy(wbuf.at[k], wfull_ref.at[theirs(k)], lsem.at[3 + k])

        own = pltpu.make_async_copy(shard_ref, wfull_ref.at[mine], lsem.at[0])

        def put_proj(block):
            for grp in range(tn // D_ATT):
                proj_ref[grp] = block[:, grp * D_ATT:(grp + 1) * D_ATT].astype(BF16)

        @pl.when((j == 0) & (i == 0))
        def _():
            shard_ref[...] = shard32_ref[...].astype(BF16)
            own.start()
            ici(0, mine).start()
            ici(1, mine).start()

        for k in range(2):
            first = n - 1 if k == 0 else min(n // 2, n - 1)

            @pl.when((j == k) & (i == first))
            def _(k=k):
                if k == 0:
                    ici(0, mine).wait_send()
                    ici(1, mine).wait_send()
                    ici(2, mine).start()
                ici(k, theirs(k)).wait_recv()
                d2d(k, c).start()

            then = (1, 0) if k == 0 else (1, min(first + 1, n - 1))

            @pl.when((j == then[0]) & (i == then[1]))
            def _(k=k):
                d2d(k, 1 - c).wait_recv()
                keep(k).start()

        first = min(n // 2, n - 1)

        @pl.when((j == 2) & (i == first))
        def _():
            d2d(0, c).wait_send()
            keep(0).wait()
            ici(2, theirs(2)).wait_recv()
            load(2, c, 1).start()

        @pl.when((j == 2) & (i == min(first + 1, n - 1)))
        def _():
            load(2, c, 1).wait()
            d2d(2, c).start()

        @pl.when((j == 2) & (i == min(first + 2, n - 1)))
        def _():
            d2d(2, 1 - c).wait_recv()
            load(2, 1 - c, 2).start()

        @pl.when((j == 3) & (i == 0))
        def _():
            load(2, 1 - c, 2).wait()

        @pl.when(j == 0)
        def _():
            xv = x_ref[...]
            r = lax.rsqrt(jnp.mean(xv * xv, axis=-1, keepdims=True) + EPS)
            hv = ((xv * r) * g_ref[...]).astype(BF16)
            hbuf[i] = hv
            h_ref[...] = hv
            put_proj(_nn(hv, shard_ref[...]))

        for k in range(3):
            @pl.when(j == k + 1)
            def _(k=k):
                put_proj(_nn(hbuf[i], wbuf[k % 2]))

        @pl.when((j == 3) & (i == n - 1))
        def _():
            ici(2, mine).wait_send()
            d2d(1, c).wait_send()
            d2d(2, c).wait_send()
            own.wait()
            keep(1).wait()

    return pl.pallas_call(
        body, name="in_proj_gather",
        grid_spec=pltpu.PrefetchScalarGridSpec(
            num_scalar_prefetch=1, grid=(N_CHIPS, n),
            in_specs=[pl.BlockSpec((TM_BLK, d), lambda j, i, order: (jnp.where(j == 0, i, n - 1), 0)),
                      pl.BlockSpec((1, d), lambda j, i, order: (0, 0)), pl.BlockSpec(memory_space=pltpu.VMEM)],
            out_specs=[pl.BlockSpec((tn // D_ATT, TM_BLK, D_ATT), lambda j, i, order: (order[j], i, 0)),
                       pl.BlockSpec((TM_BLK, d), lambda j, i, order: (jnp.where(j == 0, i, n - 1), 0)), ANY],
            scratch_shapes=[pltpu.VMEM((d, tn), BF16), pltpu.VMEM((n, TM_BLK, d), BF16), pltpu.VMEM((2, d, tn), BF16),
                            pltpu.SemaphoreType.DMA((3,)), pltpu.SemaphoreType.DMA((3,)),
                            pltpu.SemaphoreType.DMA((3,)), pltpu.SemaphoreType.DMA((3,)), pltpu.SemaphoreType.DMA((5,))]),
        out_shape=[jax.ShapeDtypeStruct((N_CHIPS * tn // D_ATT, s, D_ATT), BF16), jax.ShapeDtypeStruct((s, d), BF16),
                   jax.ShapeDtypeStruct((N_CHIPS, d, tn), BF16)],
        compiler_params=_params("arbitrary", "arbitrary"),
    )(order, x, g, shard)


def _attn_fwd(diag, proj, shards, kinds, cw3):
    s = proj.shape[1]
    n = s // TQ
    nw = len(shards)
    scale = HEAD_DIM ** -0.5

    def body(*refs):
        diag_ref, q_ref, kp_ref, kc_ref, vp_ref, vc_ref = refs[:6]
        srcs, cw = refs[6:6 + nw], refs[6 + nw]
        o_ref, lse_ref = refs[7 + nw:9 + nw]
        dsts, cw_all = refs[9 + nw:9 + 2 * nw], refs[9 + 2 * nw]
        bias_scr = refs[10 + 2 * nw]
        casts = refs[11 + 2 * nw:11 + 3 * nw]
        start, forward, finish = _gather_plan(kinds, casts, dsts, cw, cw_all, *refs[11 + 3 * nw:])
        i = pl.program_id(0)

        @pl.when(i == 0)
        def _():
            for w in range(nw):
                casts[w][...] = srcs[w][...].astype(BF16)
            start()
            _build_bias(diag_ref, bias_scr)

        @pl.when(i == n // 2)
        def _():
            forward()

        @pl.when(i == n - 1)
        def _():
            finish()

        lane_hi = lax.broadcasted_iota(jnp.int32, (QB, LANES), 1) >= HEAD_DIM

        def block(b, first_tile):
            r0, n_prev = b * QB, TQ - b * QB
            n_cur = KW - n_prev
            pairs = range(HEADS // 2)
            lanes_of = [slice(LANES * p, LANES * (p + 1)) for p in pairs]
            scores = []
            for p in pairs:
                lanes = lanes_of[p]
                q2 = _stack_heads(q_ref[r0:r0 + QB, lanes] * scale, lane_hi)
                s_cur = _nt(q2, kc_ref[0:n_cur, lanes]) + bias_scr[p, :, n_prev:KW]
                if first_tile:
                    scores.append(s_cur)
                else:
                    scores.append(jnp.concatenate(
                        [_nt(q2, kp_ref[r0:TQ, lanes]) + bias_scr[p, :, 0:n_prev], s_cur], axis=1))
            probs = []
            for p in pairs:
                sc = scores[p]
                m = jnp.max(sc, axis=1, keepdims=True)
                pe = jnp.exp(sc - m)
                l = jnp.sum(pe, axis=1, keepdims=True)
                probs.append((pe.astype(BF16), l, m))
            for p in pairs:
                lanes = lanes_of[p]
                pb, l, m = probs[p]
                if first_tile:
                    o2 = _nn(pb, vc_ref[0:n_cur, lanes]) / l
                else:
                    o2 = (_nn(pb[:, 0:n_prev], vp_ref[r0:TQ, lanes]) + _nn(pb[:, n_prev:KW], vc_ref[0:n_cur, lanes])) / l
                lse2 = m + jnp.log(l)
                lse_ref[r0:r0 + QB, 2 * p:2 * p + 1] = lse2[0:QB, :]
                lse_ref[r0:r0 + QB, 2 * p + 1:2 * p + 2] = lse2[QB:2 * QB, :]
                o_ref[r0:r0 + QB, lanes] = jnp.where(lane_hi, o2[QB:2 * QB, :], o2[0:QB, :]).astype(BF16)

        @pl.when(i == 0)
        def _():
            for b in range(TQ // QB):
                block(b, True)

        @pl.when(i > 0)
        def _():
            for b in range(TQ // QB):
                block(b, False)

    blk = lambda grp, prev: pl.BlockSpec(
        (None, TQ, D_ATT), (lambda i: (grp, jnp.maximum(i - 1, 0), 0)) if prev else (lambda i: (grp, i, 0)))
    vmem = pl.BlockSpec(memory_space=pltpu.VMEM)
    return pl.pallas_call(
        body, name="attn_fwd", grid=(n,),
        in_specs=[pl.BlockSpec((HEADS, DIAG), lambda i: (0, 0)),
                  blk(0, False), blk(1, True), blk(1, False), blk(2, True), blk(2, False)] + [vmem] * (nw + 1),
        out_specs=[pl.BlockSpec((TQ, D_ATT), lambda i: (i, 0)), pl.BlockSpec((TQ, HEADS), lambda i: (i, 0))]
        + [ANY] * (nw + 1),
        out_shape=[jax.ShapeDtypeStruct((s, D_ATT), BF16), jax.ShapeDtypeStruct((s, HEADS), F32)]
        + _gather_out_shapes(shards, kinds, cw3),
        scratch_shapes=[pltpu.VMEM((HEADS // 2, 2 * QB, KW), F32)] + [pltpu.VMEM(a.shape, BF16) for a in shards]
        + _gather_sems(nw),
        compiler_params=_params("arbitrary"),
    )(diag, proj, proj, proj, proj, proj, *shards, cw3)


def _attn_bwd(diag, proj, d_att, att, lse, parts):
    s = proj.shape[1]
    n = s // TQ
    npart = len(parts)
    scale = HEAD_DIM ** -0.5
    rel_pad = 3 * LANES

    def body(*refs):
        diag_ref, q_ref, kp_ref, kc_ref, vp_ref, vc_ref, do_ref, o_ref, lse_ref = refs[:9]
        part_refs = refs[9:9 + npart]
        dqkv_ref, dbias_ref = refs[9 + npart:11 + npart]
        slot_refs = refs[11 + npart:11 + 2 * npart]
        bias_scr, dbias_acc, dk_acc, dv_acc, dq_scr = refs[11 + 2 * npart:16 + 2 * npart]
        start, finish = _scatter_plan(part_refs, slot_refs, *refs[16 + 2 * npart:])
        i = pl.program_id(0)
        cur, prv = i % 2, 1 - i % 2

        @pl.when(i == 0)
        def _():
            start()
            _build_bias(diag_ref, bias_scr)
            dbias_acc[...] = jnp.zeros_like(dbias_acc)
            dk_acc[...] = jnp.zeros_like(dk_acc)
            dv_acc[...] = jnp.zeros_like(dv_acc)

        @pl.when(i > 0)
        def _():
            dqkv_ref[:, 0:D_ATT] = dq_scr[...]
            dk_acc[cur] = jnp.zeros((TQ, D_ATT), F32)
            dv_acc[cur] = jnp.zeros((TQ, D_ATT), F32)

        lane_hi = lax.broadcasted_iota(jnp.int32, (QB, LANES), 1) >= HEAD_DIM
        col = lax.broadcasted_iota(jnp.int32, (2 * QB, KW), 1)

        def make_block(first_tile):
            def block(b):
                r0, n_prev = b * QB, TQ - b * QB
                n_cur = KW - n_prev
                for p in range(HEADS // 2):
                    lanes = slice(LANES * p, LANES * (p + 1))
                    q2 = _stack_heads(q_ref[r0:r0 + QB, lanes] * scale, lane_hi)
                    kw = jnp.concatenate([kp_ref[r0:TQ, lanes], kc_ref[0:n_cur, lanes]], axis=0)
                    vw = jnp.concatenate([vp_ref[r0:TQ, lanes], vc_ref[0:n_cur, lanes]], axis=0)
                    dop = do_ref[r0:r0 + QB, lanes]
                    do2 = _stack_heads(dop, lane_hi)
                    prod = dop.astype(F32) * o_ref[r0:r0 + QB, lanes].astype(F32)
                    delta2 = jnp.concatenate(
                        [jnp.sum(jnp.where(lane_hi, 0.0, prod), axis=1, keepdims=True),
                         jnp.sum(jnp.where(lane_hi, prod, 0.0), axis=1, keepdims=True)], axis=0)
                    lse2 = jnp.concatenate([lse_ref[r0:r0 + QB, 2 * p:2 * p + 1],
                                            lse_ref[r0:r0 + QB, 2 * p + 1:2 * p + 2]], axis=0)
                    sc = _nt(q2, kw) + bias_scr[p]
                    if first_tile:
                        sc = jnp.where(col >= TQ - r0, sc, NEG_BIG)
                    pr = jnp.exp(sc - lse2)
                    ds = pr * (_nt(do2, vw) - delta2)
                    dbias_acc[p] += ds
                    dsb = ds.astype(BF16)
                    dv_w = _tn(pr.astype(BF16), do2)
                    dk_w = _tn(dsb, q2)
                    dv_acc[prv, r0:TQ, lanes] += dv_w[0:n_prev, :]
                    dv_acc[cur, 0:n_cur, lanes] += dv_w[n_prev:KW, :]
                    dk_acc[prv, r0:TQ, lanes] += dk_w[0:n_prev, :]
                    dk_acc[cur, 0:n_cur, lanes] += dk_w[n_prev:KW, :]
                    dq2 = _nn(dsb, kw)
                    dq = jnp.where(lane_hi, dq2[QB:2 * QB, :], dq2[0:QB, :]) * scale
                    dq_scr[r0:r0 + QB, lanes] = dq.astype(BF16)
            return block

        @pl.when(i == 0)
        def _():
            for b in range(TQ // QB):
                make_block(True)(b)

        @pl.when((i > 0) & (i < n))
        def _():
            for b in range(TQ // QB):
                make_block(False)(b)

        @pl.when(i > 0)
        def _():
            dqkv_ref[:, D_ATT:2 * D_ATT] = dk_acc[prv].astype(BF16)
            dqkv_ref[:, 2 * D_ATT:3 * D_ATT] = dv_acc[prv].astype(BF16)

        @pl.when(i == n)
        def _():
            d_iota = lax.broadcasted_iota(jnp.int32, (DIAG, rel_pad), 0)
            n_iota = lax.broadcasted_iota(jnp.int32, (DIAG, rel_pad), 1)
            diff = jnp.where(d_iota < KW, d_iota, d_iota - DIAG)
            idx = jnp.clip(N_LEFT * CHUNK - diff, -MAX_REL, MAX_REL) + MAX_REL
            onehot = (idx == n_iota).astype(F32)
            rows = []
            for hd in range(HEADS):
                acc = dbias_acc[hd // 2, (hd % 2) * QB:(hd % 2 + 1) * QB, :]
                a = jnp.concatenate([acc, jnp.zeros((QB, DIAG - KW), F32)], axis=1)
                g8 = a[0:SUBLANES, :]
                for blk in range(1, QB // SUBLANES):
                    g8 = g8 + pltpu.roll(a[blk * SUBLANES:(blk + 1) * SUBLANES, :], DIAG - blk * SUBLANES, 1)
                g1 = g8[0:1, :]
                for r in range(1, SUBLANES):
                    g1 = g1 + pltpu.roll(g8[r:r + 1, :], DIAG - r, 1)
                rows.append(g1)
            g = jnp.concatenate(rows, axis=0)
            dbias_ref[...] = jnp.dot(g, onehot, preferred_element_type=F32, precision=lax.Precision.HIGHEST)
            finish()

    last = n - 1
    cur = lambda grp: pl.BlockSpec((None, TQ, D_ATT), lambda i: (grp, jnp.minimum(i, last), 0))
    prev = lambda grp: pl.BlockSpec((None, TQ, D_ATT), lambda i: (grp, jnp.maximum(jnp.minimum(i, last) - 1, 0), 0))
    tile = pl.BlockSpec((TQ, D_ATT), lambda i: (jnp.minimum(i, last), 0))
    return pl.pallas_call(
        body, name="attn_bwd", grid=(n + 1,),
        in_specs=[pl.BlockSpec((HEADS, DIAG), lambda i: (0, 0)),
                  cur(0), prev(1), cur(1), prev(2), cur(2), tile, tile,
                  pl.BlockSpec((TQ, HEADS), lambda i: (jnp.minimum(i, last), 0))] + [ANY] * npart,
        out_specs=[pl.BlockSpec((TQ, 3 * D_ATT), lambda i: (jnp.maximum(i - 1, 0), 0)),
                   pl.BlockSpec((HEADS, rel_pad), lambda i: (0, 0))] + [ANY] * npart,
        out_shape=[jax.ShapeDtypeStruct((s, 3 * D_ATT), BF16), jax.ShapeDtypeStruct((HEADS, rel_pad), F32)]
        + _scatter_out_shapes(parts),
        scratch_shapes=[pltpu.VMEM((HEADS // 2, 2 * QB, KW), F32), pltpu.VMEM((HEADS // 2, 2 * QB, KW), F32),
                        pltpu.VMEM((2, TQ, D_ATT), F32), pltpu.VMEM((2, TQ, D_ATT), F32),
                        pltpu.VMEM((TQ, D_ATT), BF16)] + _scatter_sems(npart),
        compiler_params=_params("arbitrary"),
    )(diag, proj, proj, proj, proj, proj, d_att, att, lse, *parts)


def _shift_down(a, k, halo):
    rolled = pltpu.roll(a, k, 0)
    row = lax.broadcasted_iota(jnp.int32, halo.shape, 0)
    first = jnp.where(row < k, pltpu.roll(halo, k, 0), rolled[0:SUBLANES, :])
    return jnp.concatenate([first, rolled[SUBLANES:, :]], axis=0)


def _shift_up(a, k, nxt):
    tm = a.shape[0]
    rolled = pltpu.roll(a, tm - k, 0)
    row = lax.broadcasted_iota(jnp.int32, nxt.shape, 0)
    last = jnp.where(row >= SUBLANES - k, pltpu.roll(nxt, SUBLANES - k, 0), rolled[tm - SUBLANES:, :])
    return jnp.concatenate([rolled[:tm - SUBLANES, :], last], axis=0)


def _sigmoid(v):
    return 0.5 * jnp.tanh(0.5 * v) + 0.5


def _mixer_mid(att, proj, x, tgt, w_att, w_conv, w_out, conv_w, conv_b, fin_g):
    s, d = x.shape
    dc = D_ATT
    n = s // TM_MID
    tm = TM_MID
    n_shards = 4

    def body(att_ref, za_ref, gb_ref, gc_ref, u_ref, zc_ref, hgc_ref, hu_ref, gatt_ref, gconv_ref, x_ref, t_ref,
             watt_ref, wconv_ref, wout_ref, cw_ref, cb_ref, fg_ref,
             dpb_ref, do_ref, dx2_ref, gatt_o, gconv_o, gout_o, loss_o, gfn_o, gcb_o, gcw_o,
             acc_att, acc_conv, acc_out, carry, watt_t_ref, wconv_t_ref, wout_t_ref):
        i = pl.program_id(0)
        tile = n - 1 - i

        @pl.when(i == 0)
        def _():
            acc_att[...] = jnp.zeros_like(acc_att)
            acc_conv[...] = jnp.zeros_like(acc_conv)
            acc_out[...] = jnp.zeros_like(acc_out)
            carry[...] = jnp.zeros_like(carry)
            loss_o[...] = jnp.zeros_like(loss_o)
            gfn_o[...] = jnp.zeros_like(gfn_o)
            gcb_o[...] = jnp.zeros_like(gcb_o)
            gcw_o[...] = jnp.zeros_like(gcw_o)
            watt_t_ref[...] = watt_ref[...].T
            wconv_t_ref[...] = wconv_ref[...].T
            wout_t_ref[...] = wout_ref[...].T

        halves = [slice(hh * (tm // 2), (hh + 1) * (tm // 2)) for hh in range(2)]
        both = lambda fn: [fn(rows) for rows in halves]
        f32 = lambda ref, rows: ref[rows, :].astype(F32)

        gc = gc_ref[...].astype(F32)
        u = u_ref[...].astype(F32)
        cu = gc * u
        halo = jnp.where(tile > 0, hgc_ref[...].astype(F32) * hu_ref[...].astype(F32), 0.0)
        cu1 = _shift_down(cu, 1, halo)
        cu2 = _shift_down(cu, 2, halo)
        w0, w1, w2 = cw_ref[0:1, :], cw_ref[1:2, :], cw_ref[2:3, :]
        fg = fg_ref[...]

        def stage_a(rows):
            att_v, za, zc, gb = f32(att_ref, rows), f32(za_ref, rows), f32(zc_ref, rows), f32(gb_ref, rows)
            sa = _sigmoid(za)
            silu_a = za * sa
            vconv = w0 * cu2[rows, :] + w1 * cu1[rows, :] + w2 * cu[rows, :] + cb_ref[...]
            sc = _sigmoid(zc)
            silu_c = zc * sc
            return dict(att_v=att_v, za=za, zc=zc, gb=gb, sa=sa, silu_a=silu_a, vconv=vconv, sc=sc, silu_c=silu_c,
                        a_b=(att_v * silu_a).astype(BF16), c_b=(gb * vconv * silu_c).astype(BF16))

        st = both(stage_a)
        for t in st:
            t["y_att"] = _nn(t["a_b"], watt_ref[...])
            t["y_conv"] = _nn(t["c_b"], wconv_ref[...])
        for t, rows in zip(st, halves):
            gpair = lambda ref: jnp.concatenate([ref[0, rows, :], ref[1, rows, :]], axis=1).astype(F32)
            t["ga"] = _sigmoid(gpair(gatt_ref))
            t["gv"] = _sigmoid(gpair(gconv_ref))
            t["m_b"] = (t["ga"] * t["y_att"] + t["gv"] * t["y_conv"]).astype(BF16)
        for t in st:
            t["mo"] = _nn(t["m_b"], wout_ref[...])
        for t, rows in zip(st, halves):
            x2 = x_ref[rows, :] + t["mo"]
            r2 = lax.rsqrt(jnp.mean(x2 * x2, axis=-1, keepdims=True) + EPS)
            x2n = x2 * r2
            err = x2n * fg - t_ref[rows, :]
            loss_o[...] += jnp.sum(err * err, axis=0, keepdims=True) * (0.5 / d)
            dy = err * (1.0 / d)
            gfn_o[...] += jnp.sum(dy * x2n, axis=0, keepdims=True)
            dyn = dy * fg
            dx2 = r2 * (dyn - x2n * jnp.mean(dyn * x2n, axis=-1, keepdims=True))
            dx2_ref[rows, :] = dx2
            t["dx2_b"] = dx2.astype(BF16)
        for t in st:
            t["dm"] = _nn(t["dx2_b"], wout_t_ref[...])
        whole = lambda key: jnp.concatenate([st[0][key], st[1][key]], axis=0)
        acc_out[...] += _tn(whole("m_b"), whole("dx2_b"))
        for t, rows in zip(st, halves):
            dy_att = t["dm"] * t["ga"]
            dy_conv = t["dm"] * t["gv"]
            dpb_ref[rows, 5 * dc:5 * dc + d] = (dy_att * t["y_att"] * (1.0 - t["ga"])).astype(BF16)
            dpb_ref[rows, 5 * dc + d:5 * dc + 2 * d] = (dy_conv * t["y_conv"] * (1.0 - t["gv"])).astype(BF16)
            t["dya_b"] = dy_att.astype(BF16)
            t["dyc_b"] = dy_conv.astype(BF16)
        for t in st:
            t["da_in"] = _nn(t["dya_b"], watt_t_ref[...])
            t["dc_in"] = _nn(t["dyc_b"], wconv_t_ref[...])
        acc_att[...] += _tn(whole("a_b"), whole("dya_b"))
        acc_conv[...] += _tn(whole("c_b"), whole("dyc_b"))
        for t, rows in zip(st, halves):
            sa, za, sc, zc = t["sa"], t["za"], t["sc"], t["zc"]
            do_ref[rows, :] = (t["da_in"] * t["silu_a"]).astype(BF16)
            dpb_ref[rows, 0:dc] = (t["da_in"] * t["att_v"] * (sa * (1.0 + za * (1.0 - sa)))).astype(BF16)
            dpb_ref[rows, dc:2 * dc] = (t["dc_in"] * t["vconv"] * t["silu_c"]).astype(BF16)
            dgs = t["dc_in"] * t["gb"]
            t["dvc"] = dgs * t["silu_c"]
            dpb_ref[rows, 4 * dc:5 * dc] = (dgs * t["vconv"] * (sc * (1.0 + zc * (1.0 - sc)))).astype(BF16)
        dvc = whole("dvc")
        gcb_o[...] += jnp.sum(dvc, axis=0, keepdims=True)
        gcw_o[0:1, :] += jnp.sum(dvc * cu2, axis=0, keepdims=True)
        gcw_o[1:2, :] += jnp.sum(dvc * cu1, axis=0, keepdims=True)
        gcw_o[2:3, :] += jnp.sum(dvc * cu, axis=0, keepdims=True)
        nxt = carry[...]
        dcu = w2 * dvc + w1 * _shift_up(dvc, 1, nxt) + w0 * _shift_up(dvc, 2, nxt)
        carry[...] = dvc[0:SUBLANES, :]
        dpb_ref[:, 2 * dc:3 * dc] = (dcu * u).astype(BF16)
        dpb_ref[:, 3 * dc:4 * dc] = (dcu * gc).astype(BF16)

        @pl.when(i == n - 1)
        def _():
            for j in range(n_shards):
                gatt_o[j] = acc_att[:, j * (d // n_shards):(j + 1) * (d // n_shards)].astype(BF16)
                gconv_o[j] = acc_conv[:, j * (d // n_shards):(j + 1) * (d // n_shards)].astype(BF16)
                gout_o[j] = acc_out[j * (d // n_shards):(j + 1) * (d // n_shards), :].astype(BF16)

    rev = lambda width, col_blk: pl.BlockSpec((tm, width), lambda i: (n - 1 - i, col_blk))
    grp = lambda g: pl.BlockSpec((None, tm, dc), lambda i: (g, n - 1 - i, 0))
    grp2 = lambda g2: pl.BlockSpec((2, tm, dc), lambda i: (g2, n - 1 - i, 0))
    halo_spec = lambda g: pl.BlockSpec(
        (None, SUBLANES, dc), lambda i: (g, jnp.maximum((n - 1 - i) * (tm // SUBLANES) - 1, 0), 0))
    const = lambda shape: pl.BlockSpec(shape, lambda i: tuple(0 for _ in shape), pipeline_mode=pl.Buffered(1))
    q4 = d // n_shards
    return pl.pallas_call(
        body, name="mixer_mid", grid=(n,),
        in_specs=[rev(dc, 0), grp(3), grp(4), grp(5), grp(6), grp(7),
                  halo_spec(5), halo_spec(6), grp2(4), grp2(5), rev(d, 0), rev(d, 0),
                  const((dc, d)), const((dc, d)), const((d, d)),
                  const(conv_w.shape), const((1, dc)), const((1, d))],
        out_specs=[rev(5 * dc + 2 * d, 0), rev(dc, 0), rev(d, 0),
                   const((n_shards, dc, q4)), const((n_shards, dc, q4)), const((n_shards, q4, d)),
                   const((1, d)), const((1, d)), const((1, dc)), const((SUBLANES, dc))],
        out_shape=[jax.ShapeDtypeStruct((s, 5 * dc + 2 * d), BF16), jax.ShapeDtypeStruct((s, dc), BF16),
                   jax.ShapeDtypeStruct((s, d), F32),
                   jax.ShapeDtypeStruct((n_shards, dc, q4), BF16), jax.ShapeDtypeStruct((n_shards, dc, q4), BF16),
                   jax.ShapeDtypeStruct((n_shards, q4, d), BF16),
                   jax.ShapeDtypeStruct((1, d), F32), jax.ShapeDtypeStruct((1, d), F32),
                   jax.ShapeDtypeStruct((1, dc), F32), jax.ShapeDtypeStruct((SUBLANES, dc), F32)],
        scratch_shapes=[pltpu.VMEM((dc, d), F32), pltpu.VMEM((dc, d), F32), pltpu.VMEM((d, d), F32),
                        pltpu.VMEM((SUBLANES, dc), F32),
                        pltpu.VMEM((d, dc), BF16), pltpu.VMEM((d, dc), BF16), pltpu.VMEM((d, d), BF16)],
        compiler_params=_params("arbitrary"),
    )(att, proj, proj, proj, proj, proj, proj, proj, proj, proj, x, tgt,
      w_att, w_conv, w_out, conv_w, conv_b, fin_g)


def _in_proj_bwd_x(dqkv, dpb, w_in, x, dx2, g):
    s, d = x.shape
    tn = dqkv.shape[1]
    nb = dpb.shape[1] // tn
    n = s // TM_MM

    def body(*refs):
        dps, ws = refs[:nb + 1], refs[nb + 1:2 * nb + 2]
        x_ref, dx2_ref, g_ref, gx_ref, gng_ref = refs[2 * nb + 2:]
        i = pl.program_id(0)

        @pl.when(i == 0)
        def _():
            gng_ref[...] = jnp.zeros_like(gng_ref)

        dh = _nt(dps[0][...], ws[0][0])
        for j in range(1, nb + 1):
            dh = dh + _nt(dps[j][...], ws[j][0])
        xv = x_ref[...]
        r = lax.rsqrt(jnp.mean(xv * xv, axis=-1, keepdims=True) + EPS)
        xn = xv * r
        gng_ref[...] += jnp.sum(dh * xn, axis=0, keepdims=True)
        dhn = dh * g_ref[...]
        gx_ref[...] = dx2_ref[...] + r * (dhn - xn * jnp.mean(dhn * xn, axis=-1, keepdims=True))

    tile = lambda width, col_blk: pl.BlockSpec((TM_MM, width), lambda i: (i, col_blk))
    wspec = lambda blk: pl.BlockSpec((1, d, tn), lambda i: (blk, 0, 0), pipeline_mode=pl.Buffered(1))
    return pl.pallas_call(
        body, name="in_proj_bwd_x", grid=(n,),
        in_specs=[tile(tn, 0)] + [tile(tn, j) for j in range(nb)] + [wspec(j) for j in range(nb + 1)]
        + [tile(d, 0), tile(d, 0), pl.BlockSpec((1, d), lambda i: (0, 0))],
        out_specs=[tile(d, 0), pl.BlockSpec((1, d), lambda i: (0, 0))],
        out_shape=[jax.ShapeDtypeStruct((s, d), F32), jax.ShapeDtypeStruct((1, d), F32)],
        compiler_params=_params("arbitrary"),
    )(dqkv, *([dpb] * nb), *([w_in] * (nb + 1)), x, dx2, g)


def _in_proj_bwd_w(h, dqkv, dpb, order):
    s, d = h.shape
    tn = dqkv.shape[1]
    n = s // TM_BLK
    hr = d // 2
    settle = min(2, n - 1)

    def body(order_ref, h_ref, da_ref, db_ref, slots_ref, acc, sendbuf, pairbuf, chipbuf, psend, precv, send, recv, lsem):
        j, i = pl.program_id(0), pl.program_id(1)
        blk = order_ref[j]
        pos = _position()
        x, y, c = pos

        @pl.when(i == 0)
        def _():
            acc[...] = jnp.zeros_like(acc)

        @pl.when(blk == 0)
        def _():
            acc[...] += _tn(h_ref[...], da_ref[...])

        @pl.when(blk > 0)
        def _():
            acc[...] += _tn(h_ref[...], db_ref[...])

        def pair(step, half):
            return pltpu.make_async_remote_copy(
                src_ref=sendbuf.at[step, pl.ds(half * hr, hr), :], dst_ref=pairbuf.at[step],
                send_sem=psend.at[step], recv_sem=precv.at[step], device_id=(x, y, 1 - c), device_id_type=MESH)

        def ici(step):
            flip = OWNER_FLIPS[step]
            return pltpu.make_async_remote_copy(
                src_ref=chipbuf.at[step], dst_ref=slots_ref.at[flip], send_sem=send.at[step], recv_sem=recv.at[step],
                device_id=_peer(pos, 4 * (flip >> 1) + 2 * (flip & 1)), device_id_type=MESH)

        local = pltpu.make_async_copy(chipbuf.at[N_CHIPS - 1], slots_ref.at[0], lsem.at[0])

        def combine(step):
            pair(step, c).wait_recv()
            mine = sendbuf[step, pl.ds(c * hr, hr), :].astype(F32)
            chipbuf[step] = (mine + pairbuf[step].astype(F32)).astype(BF16)

        for step in range(N_CHIPS):
            @pl.when((j == step) & (i == n - 1))
            def _(step=step):
                sendbuf[step] = acc[...].astype(BF16)
                pair(step, 1 - c).start()

        for step in range(N_CHIPS - 1):
            @pl.when((j == step + 1) & (i == settle))
            def _(step=step):
                combine(step)
                ici(step).start()

        @pl.when((j == N_CHIPS - 1) & (i == n - 1))
        def _():
            combine(N_CHIPS - 1)
            local.start()
            for step in range(N_CHIPS - 1):
                ici(step).wait_recv()
            for step in range(N_CHIPS - 1):
                ici(step).wait_send()
            for step in range(N_CHIPS):
                pair(step, 1 - c).wait_send()
            local.wait()

    return pl.pallas_call(
        body, name="in_proj_bwd_w",
        grid_spec=pltpu.PrefetchScalarGridSpec(
            num_scalar_prefetch=1, grid=(N_CHIPS, n),
            in_specs=[pl.BlockSpec((TM_BLK, d), lambda j, i, order: (i, 0)),
                      pl.BlockSpec((TM_BLK, tn), lambda j, i, order: (jnp.where(order[j] == 0, i, 0), 0)),
                      pl.BlockSpec((TM_BLK, tn), lambda j, i, order: (jnp.where(order[j] == 0, 0, i),
                                                                     jnp.maximum(order[j] - 1, 0)))],
            out_specs=[ANY],
            scratch_shapes=[pltpu.VMEM((d, tn), F32), pltpu.VMEM((N_CHIPS, d, tn), BF16),
                            pltpu.VMEM((N_CHIPS, hr, tn), BF16), pltpu.VMEM((N_CHIPS, hr, tn), BF16),
                            pltpu.SemaphoreType.DMA((N_CHIPS,)), pltpu.SemaphoreType.DMA((N_CHIPS,)),
                            pltpu.SemaphoreType.DMA((N_CHIPS - 1,)), pltpu.SemaphoreType.DMA((N_CHIPS - 1,)),
                            pltpu.SemaphoreType.DMA((1,))]),
        out_shape=[jax.ShapeDtypeStruct((N_CHIPS, hr, tn), BF16)],
        compiler_params=_params("arbitrary", "arbitrary"),
    )(order, h, dqkv, dpb)[0]


LOSS_ROW = 6


def _adam_update(w, g, m, v):
    c1 = 1.0 / (1.0 - ADAM_B1 ** ADAM_STEP)
    c2 = 1.0 / (1.0 - ADAM_B2 ** ADAM_STEP)
    m2 = ADAM_B1 * m + (1.0 - ADAM_B1) * g
    v2 = ADAM_B2 * v + (1.0 - ADAM_B2) * (g * g)
    return -ADAM_LR * ((m2 * c1) / (jnp.sqrt(v2 * c2) + ADAM_EPS) + ADAM_WD * w), m2, v2


def _adamw_small(recv, params, moments_m, moments_v, out_shapes):
    k = recv.shape[0]
    n_par = len(params)
    cshard = params[3].shape[1]

    def body(*refs):
        r_ref = refs[0]
        ws, ms, vs = refs[1:1 + n_par], refs[1 + n_par:1 + 2 * n_par], refs[1 + 2 * n_par:1 + 3 * n_par]
        loss_ref = refs[1 + 3 * n_par]
        outs = refs[2 + 3 * n_par:]
        total = r_ref[0]
        for slot in range(1, k):
            total = total + r_ref[slot]
        loss_ref[...] = jnp.sum(total[LOSS_ROW:LOSS_ROW + 1, :], axis=1, keepdims=True)
        chip = 2 * lax.axis_index("x") + lax.axis_index("y")
        g_cw = jnp.zeros((3, cshard), F32)
        for sh in range(N_CHIPS):
            g_cw = g_cw + jnp.where(chip == sh, total[3:6, sh * cshard:(sh + 1) * cshard], 0.0)
        grads = [total[0:1, :], total[1:2, :], total[2:3, :ws[2].shape[1]], g_cw, total[8:16, :ws[4].shape[1]]]
        for p in range(n_par):
            delta, m2, v2 = _adam_update(ws[p][...], grads[p], ms[p][...], vs[p][...])
            for q, val in enumerate((grads[p], delta, m2, v2)):
                out = outs[4 * p + q]
                out[...] = val[0] if len(out.shape) == 1 else val

    shapes = [jax.ShapeDtypeStruct((1, 1), F32)]
    for shape in out_shapes:
        shapes += [jax.ShapeDtypeStruct(shape, F32)] * 4
    return pl.pallas_call(body, name="adamw_small", out_shape=shapes)(recv, *params, *moments_m, *moments_v)


def _adamw_group(ws, gs, ms, vs, name, steps):
    k = len(ws)

    def body(*refs):
        ins, outs = refs[:4 * k], refs[4 * k:]
        for j in range(k):
            gv = ins[k + j][...]
            outs[4 * j][...] = gv
            outs[4 * j + 1][...], outs[4 * j + 2][...], outs[4 * j + 3][...] = _adam_update(
                ins[j][...], gv, ins[2 * k + j][...], ins[3 * k + j][...])

    specs = [pl.BlockSpec((w.shape[0] // steps, w.shape[1]), lambda i: (i, 0)) for w in ws]
    out = pl.pallas_call(
        body, name=name, grid=(steps,),
        in_specs=specs * 4, out_specs=[spec for spec in specs for _ in range(4)],
        out_shape=[jax.ShapeDtypeStruct(w.shape, F32) for w in ws for _ in range(4)],
        compiler_params=_params("parallel"),
    )(*ws, *gs, *ms, *vs)
    return [tuple(out[4 * j:4 * j + 4]) for j in range(k)]


ANY = pl.BlockSpec(memory_space=pl.ANY)
N_CHIPS = 4
N_DEV = 8
OWNER_FLIPS = (3, 1, 2, 0)


def _position():
    return lax.axis_index("x"), lax.axis_index("y"), lax.axis_index("c")


def _gather_out_shapes(shards, kinds, cw3):
    full = [(a.shape[0], a.shape[1] * N_CHIPS) if k == "cols" else (a.shape[0] * N_CHIPS, a.shape[1])
            for a, k in zip(shards, kinds)]
    return [jax.ShapeDtypeStruct(f, BF16) for f in full] + [
        jax.ShapeDtypeStruct((N_CHIPS,) + cw3.shape, cw3.dtype)]


def _gather_sems(nw):
    return [pltpu.SemaphoreType.DMA((3, nw)), pltpu.SemaphoreType.DMA((3, nw)),
            pltpu.SemaphoreType.DMA((3, nw)), pltpu.SemaphoreType.DMA((3, nw)),
            pltpu.SemaphoreType.DMA((3,)), pltpu.SemaphoreType.DMA((3,)), pltpu.SemaphoreType.DMA((nw + 1,))]


def _gather_plan(kinds, srcs, dsts, cw, cw_all, send1, recv1, send2, recv2, ssend, srecv, lsem):
    nw = len(srcs)
    x, y, c = _position()
    mine = 2 * x + y
    chips = [(x, 1 - y), (1 - x, y), (1 - x, 1 - y)]

    def window(w, shard, half):
        r, cc = srcs[w].shape
        hr = r // 2
        if kinds[w] == "cols":
            rows = pl.ds(0, r) if half is None else pl.ds(half * hr, hr)
            return dsts[w].at[rows, pl.ds(shard * cc, cc)]
        rows = pl.ds(shard * r, r) if half is None else pl.ds(shard * r + half * hr, hr)
        return dsts[w].at[rows, :]

    def my_half(w):
        hr = srcs[w].shape[0] // 2
        return srcs[w].at[pl.ds(c * hr, hr), :]

    def local():
        return [pltpu.make_async_copy(srcs[w], window(w, mine, None), lsem.at[w]) for w in range(nw)] + [
            pltpu.make_async_copy(cw, cw_all.at[mine], lsem.at[nw])]

    def ici(k, w, shard):
        kx, ky = chips[k]
        return pltpu.make_async_remote_copy(
            src_ref=my_half(w), dst_ref=window(w, shard, c), send_sem=send1.at[k, w], recv_sem=recv1.at[k, w],
            device_id=(kx, ky, c), device_id_type=MESH)

    def d2d(k, w, shard, half):
        return pltpu.make_async_remote_copy(
            src_ref=window(w, shard, half), dst_ref=window(w, shard, half),
            send_sem=send2.at[k, w], recv_sem=recv2.at[k, w], device_id=(x, y, 1 - c), device_id_type=MESH)

    def small(k, shard):
        kx, ky = chips[k]
        return pltpu.make_async_remote_copy(
            src_ref=cw, dst_ref=cw_all.at[shard], send_sem=ssend.at[k], recv_sem=srecv.at[k],
            device_id=(kx, ky, c), device_id_type=MESH)

    def theirs(k):
        kx, ky = chips[k]
        return 2 * kx + ky

    def start():
        for cp in local():
            cp.start()
        for k in range(3):
            for w in range(nw):
                ici(k, w, mine).start()
            small(k, mine).start()

    def forward():
        for k in range(3):
            for w in range(nw):
                ici(k, w, theirs(k)).wait_recv()
                d2d(k, w, theirs(k), c).start()

    def finish():
        for k in range(3):
            for w in range(nw):
                d2d(k, w, theirs(k), 1 - c).wait_recv()
            small(k, theirs(k)).wait_recv()
        for k in range(3):
            for w in range(nw):
                ici(k, w, mine).wait_send()
                d2d(k, w, theirs(k), c).wait_send()
            small(k, mine).wait_send()
        for cp in local():
            cp.wait()

    return start, forward, finish


def _scatter_out_shapes(parts):
    return [jax.ShapeDtypeStruct((N_DEV, p.shape[1] // 2, p.shape[2]), p.dtype) for p in parts]


def _scatter_sems(nw):
    return [pltpu.SemaphoreType.DMA((N_DEV, nw)), pltpu.SemaphoreType.DMA((N_DEV, nw)), pltpu.SemaphoreType.DMA((nw,))]


def _peer(pos, k):
    x, y, c = pos
    return ((1 - x) if k & 4 else x, (1 - y) if k & 2 else y, (1 - c) if k & 1 else c)


def _scatter_plan(srcs, dsts, send, recv, lsem):
    nw = len(srcs)
    pos = _position()

    def piece(w, k):
        px, py, pc = _peer(pos, k)
        hr = srcs[w].shape[1] // 2
        return srcs[w].at[2 * px + py, pl.ds(pc * hr, hr), :]

    def remote(w, k):
        return pltpu.make_async_remote_copy(
            src_ref=piece(w, k), dst_ref=dsts[w].at[k], send_sem=send.at[k, w], recv_sem=recv.at[k, w],
            device_id=_peer(pos, k), device_id_type=MESH)

    def local(w):
        return pltpu.make_async_copy(piece(w, 0), dsts[w].at[0], lsem.at[w])

    def start():
        for w in range(nw):
            local(w).start()
        for k in range(1, N_DEV):
            for w in range(nw):
                remote(w, k).start()

    def finish():
        for k in range(1, N_DEV):
            for w in range(nw):
                remote(w, k).wait_recv()
        for k in range(1, N_DEV):
            for w in range(nw):
                remote(w, k).wait_send()
        for w in range(nw):
            local(w).wait()

    return start, finish


def _reduce_pair(slots, small):
    nw = len(slots)
    ns = len(small)

    def body(*refs):
        srcs, parts = refs[:nw], refs[nw:nw + ns]
        dsts, sm_all = refs[nw + ns:2 * nw + ns], refs[2 * nw + ns]
        halves, sm = refs[2 * nw + ns + 1:3 * nw + ns + 1], refs[3 * nw + ns + 1]
        send, recv, ssend, srecv, lsem = refs[3 * nw + ns + 2:]
        pos = _position()
        x, y, c = pos
        me = 4 * x + 2 * y + c
        _pack_small(sm, *parts)

        def rows(w, half):
            hr = halves[w].shape[0]
            return dsts[w].at[pl.ds(half * hr, hr), :]

        def remote(w, half):
            return pltpu.make_async_remote_copy(
                src_ref=halves[w], dst_ref=rows(w, half), send_sem=send.at[w], recv_sem=recv.at[w],
                device_id=(x, y, 1 - c), device_id_type=MESH)

        def bcast(k, slot):
            return pltpu.make_async_remote_copy(
                src_ref=sm, dst_ref=sm_all.at[slot], send_sem=ssend.at[k], recv_sem=srecv.at[k],
                device_id=_peer(pos, k), device_id_type=MESH)

        small_copies = [bcast(k, me) for k in range(1, N_DEV)]
        own_small = pltpu.make_async_copy(sm, sm_all.at[me], lsem.at[nw])
        for cp in small_copies + [own_small]:
            cp.start()
        big = []
        for w in range(nw):
            total = srcs[w][0].astype(F32)
            for k in range(1, srcs[w].shape[0]):
                total = total + srcs[w][k].astype(F32)
            halves[w][...] = total
            big += [remote(w, c), pltpu.make_async_copy(halves[w], rows(w, c), lsem.at[w])]
            big[-2].start()
            big[-1].start()
        for w in range(nw):
            remote(w, 1 - c).wait_recv()
        for k in range(1, N_DEV):
            px, py, pc = _peer(pos, k)
            bcast(k, 4 * px + 2 * py + pc).wait_recv()
        for w in range(nw):
            big[2 * w].wait_send()
            big[2 * w + 1].wait()
        for cp in small_copies:
            cp.wait_send()
        own_small.wait()

    vmem = pl.BlockSpec(memory_space=pltpu.VMEM)
    half_shapes = [(sl.shape[1], sl.shape[2]) for sl in slots]
    return pl.pallas_call(
        body, name="reduce_pair",
        in_specs=[vmem] * (nw + ns), out_specs=[ANY] * (nw + 1),
        out_shape=[jax.ShapeDtypeStruct((2 * r, cc), F32) for r, cc in half_shapes]
        + [jax.ShapeDtypeStruct((N_DEV, SMALL_ROWS, SMALL_COLS), F32)],
        scratch_shapes=[pltpu.VMEM(hs, F32) for hs in half_shapes] + [pltpu.VMEM((SMALL_ROWS, SMALL_COLS), F32)]
        + [pltpu.SemaphoreType.DMA((nw,)), pltpu.SemaphoreType.DMA((nw,)),
           pltpu.SemaphoreType.DMA((N_DEV,)), pltpu.SemaphoreType.DMA((N_DEV,)),
           pltpu.SemaphoreType.DMA((nw + 1,))],
        compiler_params=pltpu.CompilerParams(vmem_limit_bytes=VMEM_LIMIT),
    )(*slots, *small)


def _pack_small(sm, norm_g, fin_g, conv_b, conv_w, loss_vec, rel):
    sm[...] = jnp.zeros_like(sm)
    for row, ref in ((0, norm_g), (1, fin_g), (2, conv_b), (LOSS_ROW, loss_vec)):
        sm[row:row + 1, 0:ref.shape[1]] = ref[...]
    sm[3:6, 0:conv_w.shape[1]] = conv_w[0:3, :]
    sm[8:8 + HEADS, 0:rel.shape[1]] = rel[...]


def kernel(x, norm_g, w_in, rel_bias, w_att_out, conv_w, conv_b, w_conv_out, w_out, final_norm_g, loss_target, m_norm_g, m_w_in, m_rel_bias, m_w_att_out, m_conv_w, m_conv_b, m_w_conv_out, m_w_out, m_final_norm_g, v_norm_g, v_w_in, v_rel_bias, v_w_att_out, v_conv_w, v_conv_b, v_w_conv_out, v_w_out, v_final_norm_g):
    xs, tgt = x[0], loss_target[0]
    cshard = conv_w.shape[2]
    chip = 2 * lax.axis_index("x") + lax.axis_index("y")

    shards = [w_in[0], w_att_out[0], w_conv_out[0], w_out[0]]
    cw3 = conv_w[0]
    flips = jnp.arange(N_CHIPS, dtype=jnp.int32)
    own_first = jnp.bitwise_xor(chip, flips)
    own_last = jnp.bitwise_xor(chip, jnp.asarray(OWNER_FLIPS, jnp.int32))

    proj, h, wb_in = _in_proj_gather(xs, norm_g, shards[0], own_first)
    diag = jnp.take(rel_bias[0], _diag_rel_index(), axis=1)
    att, lse, wb_att, wb_conv, wb_out, cw_all = _attn_fwd(diag, proj, shards[1:], ["cols", "cols", "rows"], cw3)
    conv_w_full = jnp.transpose(cw_all, (1, 0, 2)).reshape(cw3.shape[0], N_CHIPS * cshard)
    (dpb, d_att, dx2, g_att_p, g_conv_p, g_out_p, loss_vec, g_fin, g_cb, g_cw) = _mixer_mid(
        att, proj, xs, tgt, wb_att, wb_conv, wb_out, conv_w_full, conv_b,
        final_norm_g[None, :])
    dqkv, g_rel, r_att, r_conv, r_out = _attn_bwd(diag, proj, d_att, att, lse, [g_att_p, g_conv_p, g_out_p])
    grad_x, g_norm = _in_proj_bwd_x(dqkv, dpb, wb_in, xs, dx2, norm_g)
    r_in = _in_proj_bwd_w(h, dqkv, dpb, own_last)

    gw_in, gw_att, gw_conv, gw_out, r_small = _reduce_pair(
        [r_in, r_att, r_conv, r_out], [g_norm, g_fin, g_cb, g_cw, loss_vec, g_rel])
    small_out = _adamw_small(
        r_small,
        [norm_g, final_norm_g[None, :], conv_b, conv_w[0], rel_bias[0]],
        [m_norm_g, m_final_norm_g[None, :], m_conv_b, m_conv_w[0], m_rel_bias[0]],
        [v_norm_g, v_final_norm_g[None, :], v_conv_b, v_conv_w[0], v_rel_bias[0]],
        [norm_g.shape, final_norm_g.shape, conv_b.shape, conv_w[0].shape, rel_bias[0].shape])
    loss = small_out[0][0, 0]
    small_names = ["norm_g", "final_norm_g", "conv_b", "conv_w", "rel_bias"]
    fix = {"norm_g": lambda a: a, "final_norm_g": lambda a: a, "conv_b": lambda a: a,
           "conv_w": lambda a: a[None], "rel_bias": lambda a: a[None]}
    small_res = {name: [fix[name](small_out[1 + 4 * p + q]) for q in range(4)] for p, name in enumerate(small_names)}

    names = ["w_in", "w_att_out", "w_conv_out", "w_out"]
    group = _adamw_group([w_in[0], w_att_out[0], w_conv_out[0], w_out[0]], [gw_in, gw_att, gw_conv, gw_out],
                         [m_w_in[0], m_w_att_out[0], m_w_conv_out[0], m_w_out[0]],
                         [v_w_in[0], v_w_att_out[0], v_w_conv_out[0], v_w_out[0]], "adamw_matrices", ADAMW_STEPS)
    big = dict(zip(names, group))
    big = {name: tuple(a[None] for a in four) for name, four in big.items()}

    order = ["norm_g", "w_in", "rel_bias", "w_att_out", "conv_w", "conv_b", "w_conv_out", "w_out", "final_norm_g"]
    outs = [loss, grad_x[None]]
    for which in range(4):
        for name in order:
            outs.append(big[name][which] if name in big else small_res[name][which])
    return tuple(outs)
```

```python
import numpy as np
import jax
import jax.numpy as jnp
from jax import lax
from jax.experimental import pallas as pl
from jax.experimental.pallas import tpu as pltpu

F32 = jnp.float32
BF16 = jnp.bfloat16
MESH = pl.DeviceIdType.MESH

CHUNK = 64
N_LEFT = 8
HEADS = 8
HEAD_DIM = 64
D_ATT = HEADS * HEAD_DIM
MAX_REL = 128
N_REL = 2 * MAX_REL + 1
EPS = 1e-6
NEG_BIG = -1e30
ADAM_LR, ADAM_B1, ADAM_B2, ADAM_EPS, ADAM_WD, ADAM_STEP = 0.001, 0.9, 0.999, 1e-08, 0.01, 10

LANES = 128
SUBLANES = 8
VMEM_LIMIT = 56 * 1024 * 1024

QB = 2 * CHUNK
KW = N_LEFT * CHUNK + QB
DIAG = KW + QB
TQ = N_LEFT * CHUNK
TM_MID = 256
ADAMW_STEPS = 8
TM_MM = 512
TM_BLK = 1024
SMALL_ROWS, SMALL_COLS = 16, 1024


def _params(*sem):
    return pltpu.CompilerParams(dimension_semantics=sem, vmem_limit_bytes=VMEM_LIMIT)


def _nt(a, b):
    return lax.dot_general(a, b, (((1,), (1,)), ((), ())), preferred_element_type=F32)


def _tn(a, b):
    return lax.dot_general(a, b, (((0,), (0,)), ((), ())), preferred_element_type=F32)


def _nn(a, b):
    return jnp.dot(a, b, preferred_element_type=F32)


def _diag_rel_index():
    d = np.arange(DIAG)
    diff = np.where(d < KW, d, d - DIAG)
    rel = N_LEFT * CHUNK - diff
    return np.clip(rel, -MAX_REL, MAX_REL) + MAX_REL


def _build_bias(diag_ref, bias_scr):
    r = lax.broadcasted_iota(jnp.int32, (QB, KW), 0) // CHUNK
    s = lax.broadcasted_iota(jnp.int32, (QB, KW), 1) // CHUNK
    allowed = (s >= r) & (s <= r + N_LEFT)
    for h in range(HEADS):
        row = jnp.broadcast_to(diag_ref[h:h + 1, :], (QB, DIAG))
        t = pltpu.roll(row, 0, 1, stride=1, stride_axis=0)
        bias_scr[h // 2, (h % 2) * QB:(h % 2 + 1) * QB, :] = jnp.where(allowed, t[:, :KW], NEG_BIG)


def _stack_heads(a, lane_hi):
    zero = jnp.zeros_like(a)
    return jnp.concatenate([jnp.where(lane_hi, zero, a), jnp.where(lane_hi, a, zero)], axis=0)


def _in_proj_gather(x, g, shard, order):
    s, d = x.shape
    tn = shard.shape[1]
    n = s // TM_BLK
    hr = d // 2

    def body(order_ref, x_ref, g_ref, shard32_ref, proj_ref, h_ref, wfull_ref, shard_ref, hbuf, wbuf,
             send1, recv1, send2, recv2, lsem):
        del order_ref
        j, i = pl.program_id(0), pl.program_id(1)
        x, y, c = _position()
        mine = 2 * x + y
        chips = [(x, 1 - y), (1 - x, y), (1 - x, 1 - y)]

        def theirs(k):
            return 2 * chips[k][0] + chips[k][1]

        def half_rows(half):
            return pl.ds(half * hr, hr)

        def landing(k, shard_index, half):
            if k < 2:
                return wbuf.at[k, half_rows(half), :]
            return wfull_ref.at[shard_index, half_rows(half), :]

        def ici(k, shard_index):
            return pltpu.make_async_remote_copy(
                src_ref=shard_ref.at[half_rows(c), :], dst_ref=landing(k, shard_index, c),
                send_sem=send1.at[k], recv_sem=recv1.at[k], device_id=(*chips[k], c), device_id_type=MESH)

        def d2d(k, half):
            return pltpu.make_async_remote_copy(
                src_ref=wbuf.at[k % 2, half_rows(half), :], dst_ref=landing(k, theirs(k), half),
                send_sem=send2.at[k], recv_sem=recv2.at[k], device_id=(x, y, 1 - c), device_id_type=MESH)

        def load(k, half, sem):
            return pltpu.make_async_copy(wfull_ref.at[theirs(k), half_rows(half), :],
                                         wbuf.at[k % 2, half_rows(half), :], lsem.at[sem])

        def keep(k):
            return pltpu.make_async_copy(wbuf.at[k], wfull_ref.at[theirs(k)], lsem.at[3 + k])

        own = pltpu.make_async_copy(shard_ref, wfull_ref.at[mine], lsem.at[0])

        def put_proj(block):
            for grp in range(tn // D_ATT):
                proj_ref[grp] = block[:, grp * D_ATT:(grp + 1) * D_ATT].astype(BF16)

        @pl.when((j == 0) & (i == 0))
        def _():
            shard_ref[...] = shard32_ref[...].astype(BF16)
            own.start()
            ici(0, mine).start()
            ici(1, mine).start()

        for k in range(2):
            first = n - 1 if k == 0 else min(n // 2, n - 1)

            @pl.when((j == k) & (i == first))
            def _(k=k):
                if k == 0:
                    ici(0, mine).wait_send()
                    ici(1, mine).wait_send()
                    ici(2, mine).start()
                ici(k, theirs(k)).wait_recv()
                d2d(k, c).start()

            then = (1, 0) if k == 0 else (1, min(first + 1, n - 1))

            @pl.when((j == then[0]) & (i == then[1]))
            def _(k=k):
                d2d(k, 1 - c).wait_recv()
                keep(k).start()

        first = min(n // 2, n - 1)

        @pl.when((j == 2) & (i == first))
        def _():
            d2d(0, c).wait_send()
            keep(0).wait()
            ici(2, theirs(2)).wait_recv()
            load(2, c, 1).start()

        @pl.when((j == 2) & (i == min(first + 1, n - 1)))
        def _():
            load(2, c, 1).wait()
            d2d(2, c).start()

        @pl.when((j == 2) & (i == min(first + 2, n - 1)))
        def _():
            d2d(2, 1 - c).wait_recv()
            load(2, 1 - c, 2).start()

        @pl.when((j == 3) & (i == 0))
        def _():
            load(2, 1 - c, 2).wait()

        @pl.when(j == 0)
        def _():
            xv = x_ref[...]
            r = lax.rsqrt(jnp.mean(xv * xv, axis=-1, keepdims=True) + EPS)
            hv = ((xv * r) * g_ref[...]).astype(BF16)
            hbuf[i] = hv
            h_ref[...] = hv
            put_proj(_nn(hv, shard_ref[...]))

        for k in range(3):
            @pl.when(j == k + 1)
            def _(k=k):
                put_proj(_nn(hbuf[i], wbuf[k % 2]))

        @pl.when((j == 3) & (i == n - 1))
        def _():
            ici(2, mine).wait_send()
            d2d(1, c).wait_send()
            d2d(2, c).wait_send()
            own.wait()
            keep(1).wait()

    return pl.pallas_call(
        body, name="in_proj_gather",
        grid_spec=pltpu.PrefetchScalarGridSpec(
            num_scalar_prefetch=1, grid=(N_CHIPS, n),
            in_specs=[pl.BlockSpec((TM_BLK, d), lambda j, i, order: (jnp.where(j == 0, i, n - 1), 0)),
                      pl.BlockSpec((1, d), lambda j, i, order: (0, 0)), pl.BlockSpec(memory_space=pltpu.VMEM)],
            out_specs=[pl.BlockSpec((tn // D_ATT, TM_BLK, D_ATT), lambda j, i, order: (order[j], i, 0)),
                       pl.BlockSpec((TM_BLK, d), lambda j, i, order: (jnp.where(j == 0, i, n - 1), 0)), ANY],
            scratch_shapes=[pltpu.VMEM((d, tn), BF16), pltpu.VMEM((n, TM_BLK, d), BF16), pltpu.VMEM((2, d, tn), BF16),
                            pltpu.SemaphoreType.DMA((3,)), pltpu.SemaphoreType.DMA((3,)),
                            pltpu.SemaphoreType.DMA((3,)), pltpu.SemaphoreType.DMA((3,)), pltpu.SemaphoreType.DMA((5,))]),
        out_shape=[jax.ShapeDtypeStruct((N_CHIPS * tn // D_ATT, s, D_ATT), BF16), jax.ShapeDtypeStruct((s, d), BF16),
                   jax.ShapeDtypeStruct((N_CHIPS, d, tn), BF16)],
        compiler_params=_params("arbitrary", "arbitrary"),
    )(order, x, g, shard)


def _attn_fwd(diag, proj, shards, kinds, cw3):
    s = proj.shape[1]
    n = s // TQ
    nw = len(shards)
    scale = HEAD_DIM ** -0.5

    def body(*refs):
        diag_ref, q_ref, kp_ref, kc_ref, vp_ref, vc_ref = refs[:6]
        srcs, cw = refs[6:6 + nw], refs[6 + nw]
        o_ref, lse_ref = refs[7 + nw:9 + nw]
        dsts, cw_all = refs[9 + nw:9 + 2 * nw], refs[9 + 2 * nw]
        bias_scr = refs[10 + 2 * nw]
        casts = refs[11 + 2 * nw:11 + 3 * nw]
        start, forward, finish = _gather_plan(kinds, casts, dsts, cw, cw_all, *refs[11 + 3 * nw:])
        i = pl.program_id(0)

        @pl.when(i == 0)
        def _():
            for w in range(nw):
                casts[w][...] = srcs[w][...].astype(BF16)
            start()
            _build_bias(diag_ref, bias_scr)

        @pl.when(i == n // 2)
        def _():
            forward()

        @pl.when(i == n - 1)
        def _():
            finish()

        lane_hi = lax.broadcasted_iota(jnp.int32, (QB, LANES), 1) >= HEAD_DIM

        def block(b, first_tile):
            r0, n_prev = b * QB, TQ - b * QB
            n_cur = KW - n_prev
            pairs = range(HEADS // 2)
            lanes_of = [slice(LANES * p, LANES * (p + 1)) for p in pairs]
            scores = []
            for p in pairs:
                lanes = lanes_of[p]
                q2 = _stack_heads(q_ref[r0:r0 + QB, lanes] * scale, lane_hi)
                s_cur = _nt(q2, kc_ref[0:n_cur, lanes]) + bias_scr[p, :, n_prev:KW]
                if first_tile:
                    scores.append(s_cur)
                else:
                    scores.append(jnp.concatenate(
                        [_nt(q2, kp_ref[r0:TQ, lanes]) + bias_scr[p, :, 0:n_prev], s_cur], axis=1))
            probs = []
            for p in pairs:
                sc = scores[p]
                m = jnp.max(sc, axis=1, keepdims=True)
                pe = jnp.exp(sc - m)
                l = jnp.sum(pe, axis=1, keepdims=True)
                probs.append((pe.astype(BF16), l, m))
            for p in pairs:
                lanes = lanes_of[p]
                pb, l, m = probs[p]
                if first_tile:
                    o2 = _nn(pb, vc_ref[0:n_cur, lanes]) / l
                else:
                    o2 = (_nn(pb[:, 0:n_prev], vp_ref[r0:TQ, lanes]) + _nn(pb[:, n_prev:KW], vc_ref[0:n_cur, lanes])) / l
                lse2 = m + jnp.log(l)
                lse_ref[r0:r0 + QB, 2 * p:2 * p + 1] = lse2[0:QB, :]
                lse_ref[r0:r0 + QB, 2 * p + 1:2 * p + 2] = lse2[QB:2 * QB, :]
                o_ref[r0:r0 + QB, lanes] = jnp.where(lane_hi, o2[QB:2 * QB, :], o2[0:QB, :]).astype(BF16)

        @pl.when(i == 0)
        def _():
            for b in range(TQ // QB):
                block(b, True)

        @pl.when(i > 0)
        def _():
            for b in range(TQ // QB):
                block(b, False)

    blk = lambda grp, prev: pl.BlockSpec(
        (None, TQ, D_ATT), (lambda i: (grp, jnp.maximum(i - 1, 0), 0)) if prev else (lambda i: (grp, i, 0)))
    vmem = pl.BlockSpec(memory_space=pltpu.VMEM)
    return pl.pallas_call(
        body, name="attn_fwd", grid=(n,),
        in_specs=[pl.BlockSpec((HEADS, DIAG), lambda i: (0, 0)),
                  blk(0, False), blk(1, True), blk(1, False), blk(2, True), blk(2, False)] + [vmem] * (nw + 1),
        out_specs=[pl.BlockSpec((TQ, D_ATT), lambda i: (i, 0)), pl.BlockSpec((TQ, HEADS), lambda i: (i, 0))]
        + [ANY] * (nw + 1),
        out_shape=[jax.ShapeDtypeStruct((s, D_ATT), BF16), jax.ShapeDtypeStruct((s, HEADS), F32)]
        + _gather_out_shapes(shards, kinds, cw3),
        scratch_shapes=[pltpu.VMEM((HEADS // 2, 2 * QB, KW), F32)] + [pltpu.VMEM(a.shape, BF16) for a in shards]
        + _gather_sems(nw),
        compiler_params=_params("arbitrary"),
    )(diag, proj, proj, proj, proj, proj, *shards, cw3)


def _attn_bwd(diag, proj, d_att, att, lse, parts):
    s = proj.shape[1]
    n = s // TQ
    npart = len(parts)
    scale = HEAD_DIM ** -0.5
    rel_pad = 3 * LANES

    def body(*refs):
        diag_ref, q_ref, kp_ref, kc_ref, vp_ref, vc_ref, do_ref, o_ref, lse_ref = refs[:9]
        part_refs = refs[9:9 + npart]
        dqkv_ref, dbias_ref = refs[9 + npart:11 + npart]
        slot_refs = refs[11 + npart:11 + 2 * npart]
        bias_scr, dbias_acc, dk_acc, dv_acc, dq_scr = refs[11 + 2 * npart:16 + 2 * npart]
        start, finish = _scatter_plan(part_refs, slot_refs, *refs[16 + 2 * npart:])
        i = pl.program_id(0)
        cur, prv = i % 2, 1 - i % 2

        @pl.when(i == 0)
        def _():
            start()
            _build_bias(diag_ref, bias_scr)
            dbias_acc[...] = jnp.zeros_like(dbias_acc)
            dk_acc[...] = jnp.zeros_like(dk_acc)
            dv_acc[...] = jnp.zeros_like(dv_acc)

        @pl.when(i > 0)
        def _():
            dqkv_ref[:, 0:D_ATT] = dq_scr[...]
            dk_acc[cur] = jnp.zeros((TQ, D_ATT), F32)
            dv_acc[cur] = jnp.zeros((TQ, D_ATT), F32)

        lane_hi = lax.broadcasted_iota(jnp.int32, (QB, LANES), 1) >= HEAD_DIM
        col = lax.broadcasted_iota(jnp.int32, (2 * QB, KW), 1)

        def make_block(first_tile):
            def block(b):
                r0, n_prev = b * QB, TQ - b * QB
                n_cur = KW - n_prev
                for p in range(HEADS // 2):
                    lanes = slice(LANES * p, LANES * (p + 1))
                    q2 = _stack_heads(q_ref[r0:r0 + QB, lanes] * scale, lane_hi)
                    kw = jnp.concatenate([kp_ref[r0:TQ, lanes], kc_ref[0:n_cur, lanes]], axis=0)
                    vw = jnp.concatenate([vp_ref[r0:TQ, lanes], vc_ref[0:n_cur, lanes]], axis=0)
                    dop = do_ref[r0:r0 + QB, lanes]
                    do2 = _stack_heads(dop, lane_hi)
                    prod = dop.astype(F32) * o_ref[r0:r0 + QB, lanes].astype(F32)
                    delta2 = jnp.concatenate(
                        [jnp.sum(jnp.where(lane_hi, 0.0, prod), axis=1, keepdims=True),
                         jnp.sum(jnp.where(lane_hi, prod, 0.0), axis=1, keepdims=True)], axis=0)
                    lse2 = jnp.concatenate([lse_ref[r0:r0 + QB, 2 * p:2 * p + 1],
                                            lse_ref[r0:r0 + QB, 2 * p + 1:2 * p + 2]], axis=0)
                    sc = _nt(q2, kw) + bias_scr[p]
                    if first_tile:
                        sc = jnp.where(col >= TQ - r0, sc, NEG_BIG)
                    pr = jnp.exp(sc - lse2)
                    ds = pr * (_nt(do2, vw) - delta2)
                    dbias_acc[p] += ds
                    dsb = ds.astype(BF16)
                    dv_w = _tn(pr.astype(BF16), do2)
                    dk_w = _tn(dsb, q2)
                    dv_acc[prv, r0:TQ, lanes] += dv_w[0:n_prev, :]
                    dv_acc[cur, 0:n_cur, lanes] += dv_w[n_prev:KW, :]
                    dk_acc[prv, r0:TQ, lanes] += dk_w[0:n_prev, :]
                    dk_acc[cur, 0:n_cur, lanes] += dk_w[n_prev:KW, :]
                    dq2 = _nn(dsb, kw)
                    dq = jnp.where(lane_hi, dq2[QB:2 * QB, :], dq2[0:QB, :]) * scale
                    dq_scr[r0:r0 + QB, lanes] = dq.astype(BF16)
            return block

        @pl.when(i == 0)
        def _():
            for b in range(TQ // QB):
                make_block(True)(b)

        @pl.when((i > 0) & (i < n))
        def _():
            for b in range(TQ // QB):
                make_block(False)(b)

        @pl.when(i > 0)
        def _():
            dqkv_ref[:, D_ATT:2 * D_ATT] = dk_acc[prv].astype(BF16)
            dqkv_ref[:, 2 * D_ATT:3 * D_ATT] = dv_acc[prv].astype(BF16)

        @pl.when(i == n)
        def _():
            d_iota = lax.broadcasted_iota(jnp.int32, (DIAG, rel_pad), 0)
            n_iota = lax.broadcasted_iota(jnp.int32, (DIAG, rel_pad), 1)
            diff = jnp.where(d_iota < KW, d_iota, d_iota - DIAG)
            idx = jnp.clip(N_LEFT * CHUNK - diff, -MAX_REL, MAX_REL) + MAX_REL
            onehot = (idx == n_iota).astype(F32)
            rows = []
            for hd in range(HEADS):
                acc = dbias_acc[hd // 2, (hd % 2) * QB:(hd % 2 + 1) * QB, :]
                a = jnp.concatenate([acc, jnp.zeros((QB, DIAG - KW), F32)], axis=1)
                g8 = a[0:SUBLANES, :]
                for blk in range(1, QB // SUBLANES):
                    g8 = g8 + pltpu.roll(a[blk * SUBLANES:(blk + 1) * SUBLANES, :], DIAG - blk * SUBLANES, 1)
                g1 = g8[0:1, :]
                for r in range(1, SUBLANES):
                    g1 = g1 + pltpu.roll(g8[r:r + 1, :], DIAG - r, 1)
                rows.append(g1)
            g = jnp.concatenate(rows, axis=0)
            dbias_ref[...] = jnp.dot(g, onehot, preferred_element_type=F32, precision=lax.Precision.HIGHEST)
            finish()

    last = n - 1
    cur = lambda grp: pl.BlockSpec((None, TQ, D_ATT), lambda i: (grp, jnp.minimum(i, last), 0))
    prev = lambda grp: pl.BlockSpec((None, TQ, D_ATT), lambda i: (grp, jnp.maximum(jnp.minimum(i, last) - 1, 0), 0))
    tile = pl.BlockSpec((TQ, D_ATT), lambda i: (jnp.minimum(i, last), 0))
    return pl.pallas_call(
        body, name="attn_bwd", grid=(n + 1,),
        in_specs=[pl.BlockSpec((HEADS, DIAG), lambda i: (0, 0)),
                  cur(0), prev(1), cur(1), prev(2), cur(2), tile, tile,
                  pl.BlockSpec((TQ, HEADS), lambda i: (jnp.minimum(i, last), 0))] + [ANY] * npart,
        out_specs=[pl.BlockSpec((TQ, 3 * D_ATT), lambda i: (jnp.maximum(i - 1, 0), 0)),
                   pl.BlockSpec((HEADS, rel_pad), lambda i: (0, 0))] + [ANY] * npart,
        out_shape=[jax.ShapeDtypeStruct((s, 3 * D_ATT), BF16), jax.ShapeDtypeStruct((HEADS, rel_pad), F32)]
        + _scatter_out_shapes(parts),
        scratch_shapes=[pltpu.VMEM((HEADS // 2, 2 * QB, KW), F32), pltpu.VMEM((HEADS // 2, 2 * QB, KW), F32),
                        pltpu.VMEM((2, TQ, D_ATT), F32), pltpu.VMEM((2, TQ, D_ATT), F32),
                        pltpu.VMEM((TQ, D_ATT), BF16)] + _scatter_sems(npart),
        compiler_params=_params("arbitrary"),
    )(diag, proj, proj, proj, proj, proj, d_att, att, lse, *parts)


def _shift_down(a, k, halo):
    rolled = pltpu.roll(a, k, 0)
    row = lax.broadcasted_iota(jnp.int32, halo.shape, 0)
    first = jnp.where(row < k, pltpu.roll(halo, k, 0), rolled[0:SUBLANES, :])
    return jnp.concatenate([first, rolled[SUBLANES:, :]], axis=0)


def _shift_up(a, k, nxt):
    tm = a.shape[0]
    rolled = pltpu.roll(a, tm - k, 0)
    row = lax.broadcasted_iota(jnp.int32, nxt.shape, 0)
    last = jnp.where(row >= SUBLANES - k, pltpu.roll(nxt, SUBLANES - k, 0), rolled[tm - SUBLANES:, :])
    return jnp.concatenate([rolled[:tm - SUBLANES, :], last], axis=0)


def _sigmoid(v):
    return 0.5 * jnp.tanh(0.5 * v) + 0.5


def _mixer_mid(att, proj, x, tgt, w_att, w_conv, w_out, conv_w, conv_b, fin_g):
    s, d = x.shape
    dc = D_ATT
    n = s // TM_MID
    tm = TM_MID
    n_shards = 4

    def body(att_ref, za_ref, gb_ref, gc_ref, u_ref, zc_ref, hgc_ref, hu_ref, gatt_ref, gconv_ref, x_ref, t_ref,
             watt_ref, wconv_ref, wout_ref, cw_ref, cb_ref, fg_ref,
             dpb_ref, do_ref, dx2_ref, gatt_o, gconv_o, gout_o, loss_o, gfn_o, gcb_o, gcw_o,
             acc_att, acc_conv, acc_out, carry, watt_t_ref, wconv_t_ref, wout_t_ref):
        i = pl.program_id(0)
        tile = n - 1 - i

        @pl.when(i == 0)
        def _():
            acc_att[...] = jnp.zeros_like(acc_att)
            acc_conv[...] = jnp.zeros_like(acc_conv)
            acc_out[...] = jnp.zeros_like(acc_out)
            carry[...] = jnp.zeros_like(carry)
            loss_o[...] = jnp.zeros_like(loss_o)
            gfn_o[...] = jnp.zeros_like(gfn_o)
            gcb_o[...] = jnp.zeros_like(gcb_o)
            gcw_o[...] = jnp.zeros_like(gcw_o)
            watt_t_ref[...] = watt_ref[...].T
            wconv_t_ref[...] = wconv_ref[...].T
            wout_t_ref[...] = wout_ref[...].T

        halves = [slice(hh * (tm // 2), (hh + 1) * (tm // 2)) for hh in range(2)]
        both = lambda fn: [fn(rows) for rows in halves]
        f32 = lambda ref, rows: ref[rows, :].astype(F32)

        gc = gc_ref[...].astype(F32)
        u = u_ref[...].astype(F32)
        cu = gc * u
        halo = jnp.where(tile > 0, hgc_ref[...].astype(F32) * hu_ref[...].astype(F32), 0.0)
        cu1 = _shift_down(cu, 1, halo)
        cu2 = _shift_down(cu, 2, halo)
        w0, w1, w2 = cw_ref[0:1, :], cw_ref[1:2, :], cw_ref[2:3, :]
        fg = fg_ref[...]

        def stage_a(rows):
            att_v, za, zc, gb = f32(att_ref, rows), f32(za_ref, rows), f32(zc_ref, rows), f32(gb_ref, rows)
            sa = _sigmoid(za)
            silu_a = za * sa
            vconv = w0 * cu2[rows, :] + w1 * cu1[rows, :] + w2 * cu[rows, :] + cb_ref[...]
            sc = _sigmoid(zc)
            silu_c = zc * sc
            return dict(att_v=att_v, za=za, zc=zc, gb=gb, sa=sa, silu_a=silu_a, vconv=vconv, sc=sc, silu_c=silu_c,
                        a_b=(att_v * silu_a).astype(BF16), c_b=(gb * vconv * silu_c).astype(BF16))

        st = both(stage_a)
        for t in st:
            t["y_att"] = _nn(t["a_b"], watt_ref[...])
            t["y_conv"] = _nn(t["c_b"], wconv_ref[...])
        for t, rows in zip(st, halves):
            gpair = lambda ref: jnp.concatenate([ref[0, rows, :], ref[1, rows, :]], axis=1).astype(F32)
            t["ga"] = _sigmoid(gpair(gatt_ref))
            t["gv"] = _sigmoid(gpair(gconv_ref))
            t["m_b"] = (t["ga"] * t["y_att"] + t["gv"] * t["y_conv"]).astype(BF16)
        for t in st:
            t["mo"] = _nn(t["m_b"], wout_ref[...])
        for t, rows in zip(st, halves):
            x2 = x_ref[rows, :] + t["mo"]
            r2 = lax.rsqrt(jnp.mean(x2 * x2, axis=-1, keepdims=True) + EPS)
            x2n = x2 * r2
            err = x2n * fg - t_ref[rows, :]
            loss_o[...] += jnp.sum(err * err, axis=0, keepdims=True) * (0.5 / d)
            dy = err * (1.0 / d)
            gfn_o[...] += jnp.sum(dy * x2n, axis=0, keepdims=True)
            dyn = dy * fg
            dx2 = r2 * (dyn - x2n * jnp.mean(dyn * x2n, axis=-1, keepdims=True))
            dx2_ref[rows, :] = dx2
            t["dx2_b"] = dx2.astype(BF16)
        for t in st:
            t["dm"] = _nn(t["dx2_b"], wout_t_ref[...])
        whole = lambda key: jnp.concatenate([st[0][key], st[1][key]], axis=0)
        acc_out[...] += _tn(whole("m_b"), whole("dx2_b"))
        for t, rows in zip(st, halves):
            dy_att = t["dm"] * t["ga"]
            dy_conv = t["dm"] * t["gv"]
            dpb_ref[rows, 5 * dc:5 * dc + d] = (dy_att * t["y_att"] * (1.0 - t["ga"])).astype(BF16)
            dpb_ref[rows, 5 * dc + d:5 * dc + 2 * d] = (dy_conv * t["y_conv"] * (1.0 - t["gv"])).astype(BF16)
            t["dya_b"] = dy_att.astype(BF16)
            t["dyc_b"] = dy_conv.astype(BF16)
        for t in st:
            t["da_in"] = _nn(t["dya_b"], watt_t_ref[...])
            t["dc_in"] = _nn(t["dyc_b"], wconv_t_ref[...])
        acc_att[...] += _tn(whole("a_b"), whole("dya_b"))
        acc_conv[...] += _tn(whole("c_b"), whole("dyc_b"))
        for t, rows in zip(st, halves):
            sa, za, sc, zc = t["sa"], t["za"], t["sc"], t["zc"]
            do_ref[rows, :] = (t["da_in"] * t["silu_a"]).astype(BF16)
            dpb_ref[rows, 0:dc] = (t["da_in"] * t["att_v"] * (sa * (1.0 + za * (1.0 - sa)))).astype(BF16)
            dpb_ref[rows, dc:2 * dc] = (t["dc_in"] * t["vconv"] * t["silu_c"]).astype(BF16)
            dgs = t["dc_in"] * t["gb"]
            t["dvc"] = dgs * t["silu_c"]
            dpb_ref[rows, 4 * dc:5 * dc] = (dgs * t["vconv"] * (sc * (1.0 + zc * (1.0 - sc)))).astype(BF16)
        dvc = whole("dvc")
        gcb_o[...] += jnp.sum(dvc, axis=0, keepdims=True)
        gcw_o[0:1, :] += jnp.sum(dvc * cu2, axis=0, keepdims=True)
        gcw_o[1:2, :] += jnp.sum(dvc * cu1, axis=0, keepdims=True)
        gcw_o[2:3, :] += jnp.sum(dvc * cu, axis=0, keepdims=True)
        nxt = carry[...]
        dcu = w2 * dvc + w1 * _shift_up(dvc, 1, nxt) + w0 * _shift_up(dvc, 2, nxt)
        carry[...] = dvc[0:SUBLANES, :]
        dpb_ref[:, 2 * dc:3 * dc] = (dcu * u).astype(BF16)
        dpb_ref[:, 3 * dc:4 * dc] = (dcu * gc).astype(BF16)

        @pl.when(i == n - 1)
        def _():
            for j in range(n_shards):
                gatt_o[j] = acc_att[:, j * (d // n_shards):(j + 1) * (d // n_shards)].astype(BF16)
                gconv_o[j] = acc_conv[:, j * (d // n_shards):(j + 1) * (d // n_shards)].astype(BF16)
                gout_o[j] = acc_out[j * (d // n_shards):(j + 1) * (d // n_shards), :].astype(BF16)

    rev = lambda width, col_blk: pl.BlockSpec((tm, width), lambda i: (n - 1 - i, col_blk))
    grp = lambda g: pl.BlockSpec((None, tm, dc), lambda i: (g, n - 1 - i, 0))
    grp2 = lambda g2: pl.BlockSpec((2, tm, dc), lambda i: (g2, n - 1 - i, 0))
    halo_spec = lambda g: pl.BlockSpec(
        (None, SUBLANES, dc), lambda i: (g, jnp.maximum((n - 1 - i) * (tm // SUBLANES) - 1, 0), 0))
    const = lambda shape: pl.BlockSpec(shape, lambda i: tuple(0 for _ in shape), pipeline_mode=pl.Buffered(1))
    q4 = d // n_shards
    return pl.pallas_call(
        body, name="mixer_mid", grid=(n,),
        in_specs=[rev(dc, 0), grp(3), grp(4), grp(5), grp(6), grp(7),
                  halo_spec(5), halo_spec(6), grp2(4), grp2(5), rev(d, 0), rev(d, 0),
                  const((dc, d)), const((dc, d)), const((d, d)),
                  const(conv_w.shape), const((1, dc)), const((1, d))],
        out_specs=[rev(5 * dc + 2 * d, 0), rev(dc, 0), rev(d, 0),
                   const((n_shards, dc, q4)), const((n_shards, dc, q4)), const((n_shards, q4, d)),
                   const((1, d)), const((1, d)), const((1, dc)), const((SUBLANES, dc))],
        out_shape=[jax.ShapeDtypeStruct((s, 5 * dc + 2 * d), BF16), jax.ShapeDtypeStruct((s, dc), BF16),
                   jax.ShapeDtypeStruct((s, d), F32),
                   jax.ShapeDtypeStruct((n_shards, dc, q4), BF16), jax.ShapeDtypeStruct((n_shards, dc, q4), BF16),
                   jax.ShapeDtypeStruct((n_shards, q4, d), BF16),
                   jax.ShapeDtypeStruct((1, d), F32), jax.ShapeDtypeStruct((1, d), F32),
                   jax.ShapeDtypeStruct((1, dc), F32), jax.ShapeDtypeStruct((SUBLANES, dc), F32)],
        scratch_shapes=[pltpu.VMEM((dc, d), F32), pltpu.VMEM((dc, d), F32), pltpu.VMEM((d, d), F32),
                        pltpu.VMEM((SUBLANES, dc), F32),
                        pltpu.VMEM((d, dc), BF16), pltpu.VMEM((d, dc), BF16), pltpu.VMEM((d, d), BF16)],
        compiler_params=_params("arbitrary"),
    )(att, proj, proj, proj, proj, proj, proj, proj, proj, proj, x, tgt,
      w_att, w_conv, w_out, conv_w, conv_b, fin_g)


def _in_proj_bwd_x(dqkv, dpb, w_in, x, dx2, g):
    s, d = x.shape
    tn = dqkv.shape[1]
    nb = dpb.shape[1] // tn
    n = s // TM_MM

    def body(*refs):
        dps, ws = refs[:nb + 1], refs[nb + 1:2 * nb + 2]
        x_ref, dx2_ref, g_ref, gx_ref, gng_ref = refs[2 * nb + 2:]
        i = pl.program_id(0)

        @pl.when(i == 0)
        def _():
            gng_ref[...] = jnp.zeros_like(gng_ref)

        dh = _nt(dps[0][...], ws[0][0])
        for j in range(1, nb + 1):
            dh = dh + _nt(dps[j][...], ws[j][0])
        xv = x_ref[...]
        r = lax.rsqrt(jnp.mean(xv * xv, axis=-1, keepdims=True) + EPS)
        xn = xv * r
        gng_ref[...] += jnp.sum(dh * xn, axis=0, keepdims=True)
        dhn = dh * g_ref[...]
        gx_ref[...] = dx2_ref[...] + r * (dhn - xn * jnp.mean(dhn * xn, axis=-1, keepdims=True))

    tile = lambda width, col_blk: pl.BlockSpec((TM_MM, width), lambda i: (i, col_blk))
    wspec = lambda blk: pl.BlockSpec((1, d, tn), lambda i: (blk, 0, 0), pipeline_mode=pl.Buffered(1))
    return pl.pallas_call(
        body, name="in_proj_bwd_x", grid=(n,),
        in_specs=[tile(tn, 0)] + [tile(tn, j) for j in range(nb)] + [wspec(j) for j in range(nb + 1)]
        + [tile(d, 0), tile(d, 0), pl.BlockSpec((1, d), lambda i: (0, 0))],
        out_specs=[tile(d, 0), pl.BlockSpec((1, d), lambda i: (0, 0))],
        out_shape=[jax.ShapeDtypeStruct((s, d), F32), jax.ShapeDtypeStruct((1, d), F32)],
        compiler_params=_params("arbitrary"),
    )(dqkv, *([dpb] * nb), *([w_in] * (nb + 1)), x, dx2, g)


def _in_proj_bwd_w(h, dqkv, dpb, order, slots, small):
    s, d = h.shape
    tn = dqkv.shape[1]
    n = s // TM_BLK
    hr = d // 2
    settle = min(2, n - 1)
    nw, ns = len(slots), len(small)
    half_rows = [sl.shape[1] for sl in slots] + [hr]

    def body(order_ref, h_ref, da_ref, db_ref, *refs):
        srcs, parts = refs[:nw], refs[nw:nw + ns]
        slots_ref, shards, sm_all = refs[nw + ns], refs[nw + ns + 1:2 * nw + ns + 2], refs[2 * nw + ns + 2]
        acc, sendbuf, pairbuf, chipbuf = refs[2 * nw + ns + 3:2 * nw + ns + 7]
        halves, sm = refs[2 * nw + ns + 7:3 * nw + ns + 7], refs[3 * nw + ns + 7]
        psend, precv, send, recv, lsem, hsend, hrecv, ssend, srecv, hsem = refs[3 * nw + ns + 8:]
        j, i = pl.program_id(0), pl.program_id(1)
        blk = order_ref[j]
        pos = _position()
        x, y, c = pos
        me = 4 * x + 2 * y + c

        def half_sum(w):
            return halves[w] if w < nw else acc.at[pl.ds(0, hr), :]

        def half_of_shard(w, half):
            return shards[w].at[pl.ds(half * half_rows[w], half_rows[w]), :]

        def exchange(w, half):
            return pltpu.make_async_remote_copy(
                src_ref=half_sum(w), dst_ref=half_of_shard(w, half), send_sem=hsend.at[w], recv_sem=hrecv.at[w],
                device_id=(x, y, 1 - c), device_id_type=MESH)

        def keep(w):
            return pltpu.make_async_copy(half_sum(w), half_of_shard(w, c), hsem.at[w])

        def bcast(k, slot):
            return pltpu.make_async_remote_copy(
                src_ref=sm, dst_ref=sm_all.at[slot], send_sem=ssend.at[k], recv_sem=srecv.at[k],
                device_id=_peer(pos, k), device_id_type=MESH)

        own_small = pltpu.make_async_copy(sm, sm_all.at[me], hsem.at[nw + 1])

        @pl.when((j == 0) & (i == 0))
        def _():
            _pack_small(sm, *parts)
            own_small.start()
            for k in range(1, N_DEV):
                bcast(k, me).start()
            for w in range(nw):
                total = srcs[w][0].astype(F32)
                for k in range(1, srcs[w].shape[0]):
                    total = total + srcs[w][k].astype(F32)
                halves[w][...] = total
                exchange(w, c).start()
                keep(w).start()

        @pl.when(i == 0)
        def _():
            acc[...] = jnp.zeros_like(acc)

        @pl.when(blk == 0)
        def _():
            acc[...] += _tn(h_ref[...], da_ref[...])

        @pl.when(blk > 0)
        def _():
            acc[...] += _tn(h_ref[...], db_ref[...])

        def pair(step, half):
            return pltpu.make_async_remote_copy(
                src_ref=sendbuf.at[step, pl.ds(half * hr, hr), :], dst_ref=pairbuf.at[step],
                send_sem=psend.at[step], recv_sem=precv.at[step], device_id=(x, y, 1 - c), device_id_type=MESH)

        def ici(step):
            flip = OWNER_FLIPS[step]
            return pltpu.make_async_remote_copy(
                src_ref=chipbuf.at[step], dst_ref=slots_ref.at[flip], send_sem=send.at[step], recv_sem=recv.at[step],
                device_id=_peer(pos, 4 * (flip >> 1) + 2 * (flip & 1)), device_id_type=MESH)

        local = pltpu.make_async_copy(chipbuf.at[N_CHIPS - 1], slots_ref.at[0], lsem.at[0])

        def combine(step):
            pair(step, c).wait_recv()
            mine = sendbuf[step, pl.ds(c * hr, hr), :].astype(F32)
            chipbuf[step] = (mine + pairbuf[step].astype(F32)).astype(BF16)

        for step in range(N_CHIPS):
            @pl.when((j == step) & (i == n - 1))
            def _(step=step):
                sendbuf[step] = acc[...].astype(BF16)
                pair(step, 1 - c).start()

        for step in range(N_CHIPS - 1):
            @pl.when((j == step + 1) & (i == settle))
            def _(step=step):
                combine(step)
                ici(step).start()

        @pl.when((j == N_CHIPS - 1) & (i == n - 1))
        def _():
            combine(N_CHIPS - 1)
            local.start()
            staged = [pltpu.make_async_copy(slots_ref.at[f], pairbuf.at[f - 1], hsem.at[nw + 1 + f])
                      for f in range(1, N_CHIPS)]
            for step in range(N_CHIPS - 1):
                ici(step).wait_recv()
            for cp in staged:
                cp.start()
            for cp in staged:
                cp.wait()
            total = chipbuf[N_CHIPS - 1].astype(F32)
            for f in range(1, N_CHIPS):
                total = total + pairbuf[f - 1].astype(F32)
            acc[0:hr, :] = total
            exchange(nw, c).start()
            keep(nw).start()
            for w in range(nw + 1):
                exchange(w, 1 - c).wait_recv()
            for k in range(1, N_DEV):
                px, py, pc = _peer(pos, k)
                bcast(k, 4 * px + 2 * py + pc).wait_recv()
            for step in range(N_CHIPS - 1):
                ici(step).wait_send()
            for step in range(N_CHIPS):
                pair(step, 1 - c).wait_send()
            local.wait()
            for w in range(nw + 1):
                exchange(w, c).wait_send()
                keep(w).wait()
            for k in range(1, N_DEV):
                bcast(k, me).wait_send()
            own_small.wait()

    vmem = pl.BlockSpec(memory_space=pltpu.VMEM)
    shard_shapes = [(2 * sl.shape[1], sl.shape[2]) for sl in slots]
    out = pl.pallas_call(
        body, name="in_proj_bwd_w",
        grid_spec=pltpu.PrefetchScalarGridSpec(
            num_scalar_prefetch=1, grid=(N_CHIPS, n),
            in_specs=[pl.BlockSpec((TM_BLK, d), lambda j, i, order: (i, 0)),
                      pl.BlockSpec((TM_BLK, tn), lambda j, i, order: (jnp.where(order[j] == 0, i, 0), 0)),
                      pl.BlockSpec((TM_BLK, tn), lambda j, i, order: (jnp.where(order[j] == 0, 0, i),
                                                                     jnp.maximum(order[j] - 1, 0)))]
            + [vmem] * (nw + ns),
            out_specs=[ANY] * (nw + 3),
            scratch_shapes=[pltpu.VMEM((d, tn), F32), pltpu.VMEM((N_CHIPS, d, tn), BF16),
                            pltpu.VMEM((N_CHIPS, hr, tn), BF16), pltpu.VMEM((N_CHIPS, hr, tn), BF16)]
            + [pltpu.VMEM((r // 2, cc), F32) for r, cc in shard_shapes] + [pltpu.VMEM((SMALL_ROWS, SMALL_COLS), F32)]
            + [pltpu.SemaphoreType.DMA((N_CHIPS,)), pltpu.SemaphoreType.DMA((N_CHIPS,)),
               pltpu.SemaphoreType.DMA((N_CHIPS - 1,)), pltpu.SemaphoreType.DMA((N_CHIPS - 1,)),
               pltpu.SemaphoreType.DMA((1,)),
               pltpu.SemaphoreType.DMA((nw + 1,)), pltpu.SemaphoreType.DMA((nw + 1,)),
               pltpu.SemaphoreType.DMA((N_DEV,)), pltpu.SemaphoreType.DMA((N_DEV,)),
               pltpu.SemaphoreType.DMA((nw + 1 + N_CHIPS,))]),
        out_shape=[jax.ShapeDtypeStruct((N_CHIPS, hr, tn), BF16)]
        + [jax.ShapeDtypeStruct(sh, F32) for sh in shard_shapes + [(d, tn)]]
        + [jax.ShapeDtypeStruct((N_DEV, SMALL_ROWS, SMALL_COLS), F32)],
        compiler_params=_params("arbitrary", "arbitrary"),
    )(order, h, dqkv, dpb, *slots, *small)
    return [out[nw + 1]] + list(out[1:nw + 1]) + [out[nw + 2]]


LOSS_ROW = 6


def _adam_update(w, g, m, v):
    c1 = 1.0 / (1.0 - ADAM_B1 ** ADAM_STEP)
    c2 = 1.0 / (1.0 - ADAM_B2 ** ADAM_STEP)
    m2 = ADAM_B1 * m + (1.0 - ADAM_B1) * g
    v2 = ADAM_B2 * v + (1.0 - ADAM_B2) * (g * g)
    return -ADAM_LR * ((m2 * c1) / (jnp.sqrt(v2 * c2) + ADAM_EPS) + ADAM_WD * w), m2, v2


def _adamw_small(recv, params, moments_m, moments_v, out_shapes):
    k = recv.shape[0]
    n_par = len(params)
    cshard = params[3].shape[1]

    def body(*refs):
        r_ref = refs[0]
        ws, ms, vs = refs[1:1 + n_par], refs[1 + n_par:1 + 2 * n_par], refs[1 + 2 * n_par:1 + 3 * n_par]
        loss_ref = refs[1 + 3 * n_par]
        outs = refs[2 + 3 * n_par:]
        total = r_ref[0]
        for slot in range(1, k):
            total = total + r_ref[slot]
        loss_ref[...] = jnp.sum(total[LOSS_ROW:LOSS_ROW + 1, :], axis=1, keepdims=True)
        chip = 2 * lax.axis_index("x") + lax.axis_index("y")
        g_cw = jnp.zeros((3, cshard), F32)
        for sh in range(N_CHIPS):
            g_cw = g_cw + jnp.where(chip == sh, total[3:6, sh * cshard:(sh + 1) * cshard], 0.0)
        grads = [total[0:1, :], total[1:2, :], total[2:3, :ws[2].shape[1]], g_cw, total[8:16, :ws[4].shape[1]]]
        for p in range(n_par):
            delta, m2, v2 = _adam_update(ws[p][...], grads[p], ms[p][...], vs[p][...])
            for q, val in enumerate((grads[p], delta, m2, v2)):
                out = outs[4 * p + q]
                out[...] = val[0] if len(out.shape) == 1 else val

    shapes = [jax.ShapeDtypeStruct((1, 1), F32)]
    for shape in out_shapes:
        shapes += [jax.ShapeDtypeStruct(shape, F32)] * 4
    return pl.pallas_call(body, name="adamw_small", out_shape=shapes)(recv, *params, *moments_m, *moments_v)


def _adamw_group(ws, gs, ms, vs, name, steps):
    k = len(ws)

    def body(*refs):
        ins, outs = refs[:4 * k], refs[4 * k:]
        for j in range(k):
            gv = ins[k + j][...]
            outs[4 * j][...] = gv
            outs[4 * j + 1][...], outs[4 * j + 2][...], outs[4 * j + 3][...] = _adam_update(
                ins[j][...], gv, ins[2 * k + j][...], ins[3 * k + j][...])

    specs = [pl.BlockSpec((w.shape[0] // steps, w.shape[1]), lambda i: (i, 0)) for w in ws]
    out = pl.pallas_call(
        body, name=name, grid=(steps,),
        in_specs=specs * 4, out_specs=[spec for spec in specs for _ in range(4)],
        out_shape=[jax.ShapeDtypeStruct(w.shape, F32) for w in ws for _ in range(4)],
        compiler_params=_params("parallel"),
    )(*ws, *gs, *ms, *vs)
    return [tuple(out[4 * j:4 * j + 4]) for j in range(k)]


ANY = pl.BlockSpec(memory_space=pl.ANY)
N_CHIPS = 4
N_DEV = 8
OWNER_FLIPS = (3, 1, 2, 0)


def _position():
    return lax.axis_index("x"), lax.axis_index("y"), lax.axis_index("c")


def _gather_out_shapes(shards, kinds, cw3):
    full = [(a.shape[0], a.shape[1] * N_CHIPS) if k == "cols" else (a.shape[0] * N_CHIPS, a.shape[1])
            for a, k in zip(shards, kinds)]
    return [jax.ShapeDtypeStruct(f, BF16) for f in full] + [
        jax.ShapeDtypeStruct((N_CHIPS,) + cw3.shape, cw3.dtype)]


def _gather_sems(nw):
    return [pltpu.SemaphoreType.DMA((3, nw)), pltpu.SemaphoreType.DMA((3, nw)),
            pltpu.SemaphoreType.DMA((3, nw)), pltpu.SemaphoreType.DMA((3, nw)),
            pltpu.SemaphoreType.DMA((3,)), pltpu.SemaphoreType.DMA((3,)), pltpu.SemaphoreType.DMA((nw + 1,))]


def _gather_plan(kinds, srcs, dsts, cw, cw_all, send1, recv1, send2, recv2, ssend, srecv, lsem):
    nw = len(srcs)
    x, y, c = _position()
    mine = 2 * x + y
    chips = [(x, 1 - y), (1 - x, y), (1 - x, 1 - y)]

    def window(w, shard, half):
        r, cc = srcs[w].shape
        hr = r // 2
        if kinds[w] == "cols":
            rows = pl.ds(0, r) if half is None else pl.ds(half * hr, hr)
            return dsts[w].at[rows, pl.ds(shard * cc, cc)]
        rows = pl.ds(shard * r, r) if half is None else pl.ds(shard * r + half * hr, hr)
        return dsts[w].at[rows, :]

    def my_half(w):
        hr = srcs[w].shape[0] // 2
        return srcs[w].at[pl.ds(c * hr, hr), :]

    def local():
        return [pltpu.make_async_copy(srcs[w], window(w, mine, None), lsem.at[w]) for w in range(nw)] + [
            pltpu.make_async_copy(cw, cw_all.at[mine], lsem.at[nw])]

    def ici(k, w, shard):
        kx, ky = chips[k]
        return pltpu.make_async_remote_copy(
            src_ref=my_half(w), dst_ref=window(w, shard, c), send_sem=send1.at[k, w], recv_sem=recv1.at[k, w],
            device_id=(kx, ky, c), device_id_type=MESH)

    def d2d(k, w, shard, half):
        return pltpu.make_async_remote_copy(
            src_ref=window(w, shard, half), dst_ref=window(w, shard, half),
            send_sem=send2.at[k, w], recv_sem=recv2.at[k, w], device_id=(x, y, 1 - c), device_id_type=MESH)

    def small(k, shard):
        kx, ky = chips[k]
        return pltpu.make_async_remote_copy(
            src_ref=cw, dst_ref=cw_all.at[shard], send_sem=ssend.at[k], recv_sem=srecv.at[k],
            device_id=(kx, ky, c), device_id_type=MESH)

    def theirs(k):
        kx, ky = chips[k]
        return 2 * kx + ky

    def start():
        for cp in local():
            cp.start()
        for k in range(3):
            for w in range(nw):
                ici(k, w, mine).start()
            small(k, mine).start()

    def forward():
        for k in range(3):
            for w in range(nw):
                ici(k, w, theirs(k)).wait_recv()
                d2d(k, w, theirs(k), c).start()

    def finish():
        for k in range(3):
            for w in range(nw):
                d2d(k, w, theirs(k), 1 - c).wait_recv()
            small(k, theirs(k)).wait_recv()
        for k in range(3):
            for w in range(nw):
                ici(k, w, mine).wait_send()
                d2d(k, w, theirs(k), c).wait_send()
            small(k, mine).wait_send()
        for cp in local():
            cp.wait()

    return start, forward, finish


def _scatter_out_shapes(parts):
    return [jax.ShapeDtypeStruct((N_DEV, p.shape[1] // 2, p.shape[2]), p.dtype) for p in parts]


def _scatter_sems(nw):
    return [pltpu.SemaphoreType.DMA((N_DEV, nw)), pltpu.SemaphoreType.DMA((N_DEV, nw)), pltpu.SemaphoreType.DMA((nw,))]


def _peer(pos, k):
    x, y, c = pos
    return ((1 - x) if k & 4 else x, (1 - y) if k & 2 else y, (1 - c) if k & 1 else c)


def _scatter_plan(srcs, dsts, send, recv, lsem):
    nw = len(srcs)
    pos = _position()

    def piece(w, k):
        px, py, pc = _peer(pos, k)
        hr = srcs[w].shape[1] // 2
        return srcs[w].at[2 * px + py, pl.ds(pc * hr, hr), :]

    def remote(w, k):
        return pltpu.make_async_remote_copy(
            src_ref=piece(w, k), dst_ref=dsts[w].at[k], send_sem=send.at[k, w], recv_sem=recv.at[k, w],
            device_id=_peer(pos, k), device_id_type=MESH)

    def local(w):
        return pltpu.make_async_copy(piece(w, 0), dsts[w].at[0], lsem.at[w])

    def start():
        for w in range(nw):
            local(w).start()
        for k in range(1, N_DEV):
            for w in range(nw):
                remote(w, k).start()

    def finish():
        for k in range(1, N_DEV):
            for w in range(nw):
                remote(w, k).wait_recv()
        for k in range(1, N_DEV):
            for w in range(nw):
                remote(w, k).wait_send()
        for w in range(nw):
            local(w).wait()

    return start, finish


def _pack_small(sm, norm_g, fin_g, conv_b, conv_w, loss_vec, rel):
    sm[...] = jnp.zeros_like(sm)
    for row, ref in ((0, norm_g), (1, fin_g), (2, conv_b), (LOSS_ROW, loss_vec)):
        sm[row:row + 1, 0:ref.shape[1]] = ref[...]
    sm[3:6, 0:conv_w.shape[1]] = conv_w[0:3, :]
    sm[8:8 + HEADS, 0:rel.shape[1]] = rel[...]


def kernel(x, norm_g, w_in, rel_bias, w_att_out, conv_w, conv_b, w_conv_out, w_out, final_norm_g, loss_target, m_norm_g, m_w_in, m_rel_bias, m_w_att_out, m_conv_w, m_conv_b, m_w_conv_out, m_w_out, m_final_norm_g, v_norm_g, v_w_in, v_rel_bias, v_w_att_out, v_conv_w, v_conv_b, v_w_conv_out, v_w_out, v_final_norm_g):
    xs, tgt = x[0], loss_target[0]
    cshard = conv_w.shape[2]
    chip = 2 * lax.axis_index("x") + lax.axis_index("y")

    shards = [w_in[0], w_att_out[0], w_conv_out[0], w_out[0]]
    cw3 = conv_w[0]
    flips = jnp.arange(N_CHIPS, dtype=jnp.int32)
    own_first = jnp.bitwise_xor(chip, flips)
    own_last = jnp.bitwise_xor(chip, jnp.asarray(OWNER_FLIPS, jnp.int32))

    proj, h, wb_in = _in_proj_gather(xs, norm_g, shards[0], own_first)
    diag = jnp.take(rel_bias[0], _diag_rel_index(), axis=1)
    att, lse, wb_att, wb_conv, wb_out, cw_all = _attn_fwd(diag, proj, shards[1:], ["cols", "cols", "rows"], cw3)
    conv_w_full = jnp.transpose(cw_all, (1, 0, 2)).reshape(cw3.shape[0], N_CHIPS * cshard)
    (dpb, d_att, dx2, g_att_p, g_conv_p, g_out_p, loss_vec, g_fin, g_cb, g_cw) = _mixer_mid(
        att, proj, xs, tgt, wb_att, wb_conv, wb_out, conv_w_full, conv_b,
        final_norm_g[None, :])
    dqkv, g_rel, r_att, r_conv, r_out = _attn_bwd(diag, proj, d_att, att, lse, [g_att_p, g_conv_p, g_out_p])
    grad_x, g_norm = _in_proj_bwd_x(dqkv, dpb, wb_in, xs, dx2, norm_g)
    gw_in, gw_att, gw_conv, gw_out, r_small = _in_proj_bwd_w(
        h, dqkv, dpb, own_last, [r_att, r_conv, r_out], [g_norm, g_fin, g_cb, g_cw, loss_vec, g_rel])

    small_out = _adamw_small(
        r_small,
        [norm_g, final_norm_g[None, :], conv_b, conv_w[0], rel_bias[0]],
        [m_norm_g, m_final_norm_g[None, :], m_conv_b, m_conv_w[0], m_rel_bias[0]],
        [v_norm_g, v_final_norm_g[None, :], v_conv_b, v_conv_w[0], v_rel_bias[0]],
        [norm_g.shape, final_norm_g.shape, conv_b.shape, conv_w[0].shape, rel_bias[0].shape])
    loss = small_out[0][0, 0]
    small_names = ["norm_g", "final_norm_g", "conv_b", "conv_w", "rel_bias"]
    fix = {"norm_g": lambda a: a, "final_norm_g": lambda a: a, "conv_b": lambda a: a,
           "conv_w": lambda a: a[None], "rel_bias": lambda a: a[None]}
    small_res = {name: [fix[name](small_out[1 + 4 * p + q]) for q in range(4)] for p, name in enumerate(small_names)}

    names = ["w_in", "w_att_out", "w_conv_out", "w_out"]
    group = _adamw_group([w_in[0], w_att_out[0], w_conv_out[0], w_out[0]], [gw_in, gw_att, gw_conv, gw_out],
                         [m_w_in[0], m_w_att_out[0], m_w_conv_out[0], m_w_out[0]],
                         [v_w_in[0], v_w_att_out[0], v_w_conv_out[0], v_w_out[0]], "adamw_matrices", ADAMW_STEPS)
    big = dict(zip(names, group))
    big = {name: tuple(a[None] for a in four) for name, four in big.items()}

    order = ["norm_g", "w_in", "rel_bias", "w_att_out", "conv_w", "conv_b", "w_conv_out", "w_out", "final_norm_g"]
    outs = [loss, grad_x[None]]
    for which in range(4):
        for name in order:
            outs.append(big[name][which] if name in big else small_res[name][which])
    return tuple(outs)
```

```python
import numpy as np
import jax
import jax.numpy as jnp
from jax import lax
from jax.experimental import pallas as pl
from jax.experimental.pallas import tpu as pltpu

F32 = jnp.float32
BF16 = jnp.bfloat16
MESH = pl.DeviceIdType.MESH

CHUNK = 64
N_LEFT = 8
HEADS = 8
HEAD_DIM = 64
D_ATT = HEADS * HEAD_DIM
MAX_REL = 128
N_REL = 2 * MAX_REL + 1
EPS = 1e-6
NEG_BIG = -1e30
ADAM_LR, ADAM_B1, ADAM_B2, ADAM_EPS, ADAM_WD, ADAM_STEP = 0.001, 0.9, 0.999, 1e-08, 0.01, 10

LANES = 128
SUBLANES = 8
VMEM_LIMIT = 56 * 1024 * 1024

QB = 2 * CHUNK
KW = N_LEFT * CHUNK + QB
DIAG = KW + QB
TQ = N_LEFT * CHUNK
TM_MID = 256
ADAMW_STEPS = 8
TM_MM = 512
TM_BLK = 1024
SMALL_ROWS, SMALL_COLS = 16, 1024


def _params(*sem):
    return pltpu.CompilerParams(dimension_semantics=sem, vmem_limit_bytes=VMEM_LIMIT)


def _nt(a, b):
    return lax.dot_general(a, b, (((1,), (1,)), ((), ())), preferred_element_type=F32)


def _tn(a, b):
    return lax.dot_general(a, b, (((0,), (0,)), ((), ())), preferred_element_type=F32)


def _nn(a, b):
    return jnp.dot(a, b, preferred_element_type=F32)


def _diag_rel_index():
    d = np.arange(DIAG)
    diff = np.where(d < KW, d, d - DIAG)
    rel = N_LEFT * CHUNK - diff
    return np.clip(rel, -MAX_REL, MAX_REL) + MAX_REL


def _build_bias(diag_ref, bias_scr):
    r = lax.broadcasted_iota(jnp.int32, (QB, KW), 0) // CHUNK
    s = lax.broadcasted_iota(jnp.int32, (QB, KW), 1) // CHUNK
    allowed = (s >= r) & (s <= r + N_LEFT)
    for h in range(HEADS):
        row = jnp.broadcast_to(diag_ref[h:h + 1, :], (QB, DIAG))
        t = pltpu.roll(row, 0, 1, stride=1, stride_axis=0)
        bias_scr[h // 2, (h % 2) * QB:(h % 2 + 1) * QB, :] = jnp.where(allowed, t[:, :KW], NEG_BIG)


def _stack_heads(a, lane_hi):
    zero = jnp.zeros_like(a)
    return jnp.concatenate([jnp.where(lane_hi, zero, a), jnp.where(lane_hi, a, zero)], axis=0)


def _in_proj_gather(x, g, shard, order):
    s, d = x.shape
    tn = shard.shape[1]
    n = s // TM_BLK
    hr = d // 2

    def body(order_ref, x_ref, g_ref, shard32_ref, proj_ref, h_ref, wfull_ref, shard_ref, hbuf, wbuf,
             send1, recv1, send2, recv2, lsem):
        del order_ref
        j, i = pl.program_id(0), pl.program_id(1)
        x, y, c = _position()
        mine = 2 * x + y
        chips = [(x, 1 - y), (1 - x, y), (1 - x, 1 - y)]

        def theirs(k):
            return 2 * chips[k][0] + chips[k][1]

        def half_rows(half):
            return pl.ds(half * hr, hr)

        def landing(k, shard_index, half):
            if k < 2:
                return wbuf.at[k, half_rows(half), :]
            return wfull_ref.at[shard_index, half_rows(half), :]

        def ici(k, shard_index):
            return pltpu.make_async_remote_copy(
                src_ref=shard_ref.at[half_rows(c), :], dst_ref=landing(k, shard_index, c),
                send_sem=send1.at[k], recv_sem=recv1.at[k], device_id=(*chips[k], c), device_id_type=MESH)

        def d2d(k, half):
            return pltpu.make_async_remote_copy(
                src_ref=wbuf.at[k % 2, half_rows(half), :], dst_ref=landing(k, theirs(k), half),
                send_sem=send2.at[k], recv_sem=recv2.at[k], device_id=(x, y, 1 - c), device_id_type=MESH)

        def load(k, half, sem):
            return pltpu.make_async_copy(wfull_ref.at[theirs(k), half_rows(half), :],
                                         wbuf.at[k % 2, half_rows(half), :], lsem.at[sem])

        def keep(k):
            return pltpu.make_async_copy(wbuf.at[k], wfull_ref.at[theirs(k)], lsem.at[3 + k])

        own = pltpu.make_async_copy(shard_ref, wfull_ref.at[mine], lsem.at[0])

        def put_proj(block):
            for grp in range(tn // D_ATT):
                proj_ref[grp] = block[:, grp * D_ATT:(grp + 1) * D_ATT].astype(BF16)

        @pl.when((j == 0) & (i == 0))
        def _():
            shard_ref[...] = shard32_ref[...].astype(BF16)
            own.start()
            ici(0, mine).start()
            ici(1, mine).start()

        for k in range(2):
            first = n - 1 if k == 0 else min(n // 2, n - 1)

            @pl.when((j == k) & (i == first))
            def _(k=k):
                if k == 0:
                    ici(0, mine).wait_send()
                    ici(1, mine).wait_send()
                    ici(2, mine).start()
                ici(k, theirs(k)).wait_recv()
                d2d(k, c).start()

            then = (1, 0) if k == 0 else (1, min(first + 1, n - 1))

            @pl.when((j == then[0]) & (i == then[1]))
            def _(k=k):
                d2d(k, 1 - c).wait_recv()
                keep(k).start()

        first = min(n // 2, n - 1)

        @pl.when((j == 2) & (i == first))
        def _():
            d2d(0, c).wait_send()
            keep(0).wait()
            ici(2, theirs(2)).wait_recv()
            load(2, c, 1).start()

        @pl.when((j == 2) & (i == min(first + 1, n - 1)))
        def _():
            load(2, c, 1).wait()
            d2d(2, c).start()

        @pl.when((j == 2) & (i == min(first + 2, n - 1)))
        def _():
            d2d(2, 1 - c).wait_recv()
            load(2, 1 - c, 2).start()

        @pl.when((j == 3) & (i == 0))
        def _():
            load(2, 1 - c, 2).wait()

        @pl.when(j == 0)
        def _():
            xv = x_ref[...]
            r = lax.rsqrt(jnp.mean(xv * xv, axis=-1, keepdims=True) + EPS)
            hv = ((xv * r) * g_ref[...]).astype(BF16)
            hbuf[i] = hv
            h_ref[...] = hv
            put_proj(_nn(hv, shard_ref[...]))

        for k in range(3):
            @pl.when(j == k + 1)
            def _(k=k):
                put_proj(_nn(hbuf[i], wbuf[k % 2]))

        @pl.when((j == 3) & (i == n - 1))
        def _():
            ici(2, mine).wait_send()
            d2d(1, c).wait_send()
            d2d(2, c).wait_send()
            own.wait()
            keep(1).wait()

    return pl.pallas_call(
        body, name="in_proj_gather",
        grid_spec=pltpu.PrefetchScalarGridSpec(
            num_scalar_prefetch=1, grid=(N_CHIPS, n),
            in_specs=[pl.BlockSpec((TM_BLK, d), lambda j, i, order: (jnp.where(j == 0, i, n - 1), 0)),
                      pl.BlockSpec((1, d), lambda j, i, order: (0, 0)), pl.BlockSpec(memory_space=pltpu.VMEM)],
            out_specs=[pl.BlockSpec((tn // D_ATT, TM_BLK, D_ATT), lambda j, i, order: (order[j], i, 0)),
                       pl.BlockSpec((TM_BLK, d), lambda j, i, order: (jnp.where(j == 0, i, n - 1), 0)), ANY],
            scratch_shapes=[pltpu.VMEM((d, tn), BF16), pltpu.VMEM((n, TM_BLK, d), BF16), pltpu.VMEM((2, d, tn), BF16),
                            pltpu.SemaphoreType.DMA((3,)), pltpu.SemaphoreType.DMA((3,)),
                            pltpu.SemaphoreType.DMA((3,)), pltpu.SemaphoreType.DMA((3,)), pltpu.SemaphoreType.DMA((5,))]),
        out_shape=[jax.ShapeDtypeStruct((N_CHIPS * tn // D_ATT, s, D_ATT), BF16), jax.ShapeDtypeStruct((s, d), BF16),
                   jax.ShapeDtypeStruct((N_CHIPS, d, tn), BF16)],
        compiler_params=_params("arbitrary", "arbitrary"),
    )(order, x, g, shard)


def _attn_fwd(diag, proj, shards, kinds, cw3):
    s = proj.shape[1]
    n = s // TQ
    nw = len(shards)
    scale = HEAD_DIM ** -0.5

    def body(*refs):
        diag_ref, q_ref, kp_ref, kc_ref, vp_ref, vc_ref = refs[:6]
        srcs, cw = refs[6:6 + nw], refs[6 + nw]
        o_ref, lse_ref = refs[7 + nw:9 + nw]
        dsts, cw_all = refs[9 + nw:9 + 2 * nw], refs[9 + 2 * nw]
        bias_scr = refs[10 + 2 * nw]
        casts = refs[11 + 2 * nw:11 + 3 * nw]
        start, forward, finish = _gather_plan(kinds, casts, dsts, cw, cw_all, *refs[11 + 3 * nw:])
        i = pl.program_id(0)

        @pl.when(i == 0)
        def _():
            for w in range(nw):
                casts[w][...] = srcs[w][...].astype(BF16)
            start()
            _build_bias(diag_ref, bias_scr)

        @pl.when(i == n // 2)
        def _():
            forward()

        @pl.when(i == n - 1)
        def _():
            finish()

        lane_hi = lax.broadcasted_iota(jnp.int32, (QB, LANES), 1) >= HEAD_DIM

        def block(b, first_tile):
            r0, n_prev = b * QB, TQ - b * QB
            n_cur = KW - n_prev
            pairs = range(HEADS // 2)
            lanes_of = [slice(LANES * p, LANES * (p + 1)) for p in pairs]
            scores = []
            for p in pairs:
                lanes = lanes_of[p]
                q2 = _stack_heads(q_ref[r0:r0 + QB, lanes] * scale, lane_hi)
                s_cur = _nt(q2, kc_ref[0:n_cur, lanes]) + bias_scr[p, :, n_prev:KW]
                if first_tile:
                    scores.append(s_cur)
                else:
                    scores.append(jnp.concatenate(
                        [_nt(q2, kp_ref[r0:TQ, lanes]) + bias_scr[p, :, 0:n_prev], s_cur], axis=1))
            probs = []
            for p in pairs:
                sc = scores[p]
                m = jnp.max(sc, axis=1, keepdims=True)
                pe = jnp.exp(sc - m)
                l = jnp.sum(pe, axis=1, keepdims=True)
                probs.append((pe.astype(BF16), l, m))
            for p in pairs:
                lanes = lanes_of[p]
                pb, l, m = probs[p]
                if first_tile:
                    o2 = _nn(pb, vc_ref[0:n_cur, lanes]) / l
                else:
                    o2 = (_nn(pb[:, 0:n_prev], vp_ref[r0:TQ, lanes]) + _nn(pb[:, n_prev:KW], vc_ref[0:n_cur, lanes])) / l
                lse2 = m + jnp.log(l)
                lse_ref[r0:r0 + QB, 2 * p:2 * p + 1] = lse2[0:QB, :]
                lse_ref[r0:r0 + QB, 2 * p + 1:2 * p + 2] = lse2[QB:2 * QB, :]
                o_ref[r0:r0 + QB, lanes] = jnp.where(lane_hi, o2[QB:2 * QB, :], o2[0:QB, :]).astype(BF16)

        @pl.when(i == 0)
        def _():
            for b in range(TQ // QB):
                block(b, True)

        @pl.when(i > 0)
        def _():
            for b in range(TQ // QB):
                block(b, False)

    blk = lambda grp, prev: pl.BlockSpec(
        (None, TQ, D_ATT), (lambda i: (grp, jnp.maximum(i - 1, 0), 0)) if prev else (lambda i: (grp, i, 0)))
    vmem = pl.BlockSpec(memory_space=pltpu.VMEM)
    return pl.pallas_call(
        body, name="attn_fwd", grid=(n,),
        in_specs=[pl.BlockSpec((HEADS, DIAG), lambda i: (0, 0)),
                  blk(0, False), blk(1, True), blk(1, False), blk(2, True), blk(2, False)] + [vmem] * (nw + 1),
        out_specs=[pl.BlockSpec((TQ, D_ATT), lambda i: (i, 0)), pl.BlockSpec((TQ, HEADS), lambda i: (i, 0))]
        + [ANY] * (nw + 1),
        out_shape=[jax.ShapeDtypeStruct((s, D_ATT), BF16), jax.ShapeDtypeStruct((s, HEADS), F32)]
        + _gather_out_shapes(shards, kinds, cw3),
        scratch_shapes=[pltpu.VMEM((HEADS // 2, 2 * QB, KW), F32)] + [pltpu.VMEM(a.shape, BF16) for a in shards]
        + _gather_sems(nw),
        compiler_params=_params("arbitrary"),
    )(diag, proj, proj, proj, proj, proj, *shards, cw3)


def _attn_bwd(diag, proj, d_att, att, lse, parts):
    s = proj.shape[1]
    n = s // TQ
    npart = len(parts)
    scale = HEAD_DIM ** -0.5
    rel_pad = 3 * LANES

    def body(*refs):
        diag_ref, q_ref, kp_ref, kc_ref, vp_ref, vc_ref, do_ref, o_ref, lse_ref = refs[:9]
        part_refs = refs[9:9 + npart]
        dqkv_ref, dbias_ref = refs[9 + npart:11 + npart]
        slot_refs = refs[11 + npart:11 + 2 * npart]
        bias_scr, dbias_acc, dk_acc, dv_acc, dq_scr = refs[11 + 2 * npart:16 + 2 * npart]
        start, finish = _scatter_plan(part_refs, slot_refs, *refs[16 + 2 * npart:])
        i = pl.program_id(0)
        cur, prv = i % 2, 1 - i % 2

        @pl.when(i == 0)
        def _():
            start()
            _build_bias(diag_ref, bias_scr)
            dbias_acc[...] = jnp.zeros_like(dbias_acc)
            dk_acc[...] = jnp.zeros_like(dk_acc)
            dv_acc[...] = jnp.zeros_like(dv_acc)

        @pl.when(i > 0)
        def _():
            dqkv_ref[:, 0:D_ATT] = dq_scr[...]
            dk_acc[cur] = jnp.zeros((TQ, D_ATT), F32)
            dv_acc[cur] = jnp.zeros((TQ, D_ATT), F32)

        lane_hi = lax.broadcasted_iota(jnp.int32, (QB, LANES), 1) >= HEAD_DIM
        col = lax.broadcasted_iota(jnp.int32, (2 * QB, KW), 1)

        def make_block(first_tile):
            def block(b):
                r0, n_prev = b * QB, TQ - b * QB
                n_cur = KW - n_prev
                for p in range(HEADS // 2):
                    lanes = slice(LANES * p, LANES * (p + 1))
                    q2 = _stack_heads(q_ref[r0:r0 + QB, lanes] * scale, lane_hi)
                    kw = jnp.concatenate([kp_ref[r0:TQ, lanes], kc_ref[0:n_cur, lanes]], axis=0)
                    vw = jnp.concatenate([vp_ref[r0:TQ, lanes], vc_ref[0:n_cur, lanes]], axis=0)
                    dop = do_ref[r0:r0 + QB, lanes]
                    do2 = _stack_heads(dop, lane_hi)
                    prod = dop.astype(F32) * o_ref[r0:r0 + QB, lanes].astype(F32)
                    delta2 = jnp.concatenate(
                        [jnp.sum(jnp.where(lane_hi, 0.0, prod), axis=1, keepdims=True),
                         jnp.sum(jnp.where(lane_hi, prod, 0.0), axis=1, keepdims=True)], axis=0)
                    lse2 = jnp.concatenate([lse_ref[r0:r0 + QB, 2 * p:2 * p + 1],
                                            lse_ref[r0:r0 + QB, 2 * p + 1:2 * p + 2]], axis=0)
                    sc = _nt(q2, kw) + bias_scr[p]
                    if first_tile:
                        sc = jnp.where(col >= TQ - r0, sc, NEG_BIG)
                    pr = jnp.exp(sc - lse2)
                    ds = pr * (_nt(do2, vw) - delta2)
                    dbias_acc[p] += ds
                    dsb = ds.astype(BF16)
                    dv_w = _tn(pr.astype(BF16), do2)
                    dk_w = _tn(dsb, q2)
                    dv_acc[prv, r0:TQ, lanes] += dv_w[0:n_prev, :]
                    dv_acc[cur, 0:n_cur, lanes] += dv_w[n_prev:KW, :]
                    dk_acc[prv, r0:TQ, lanes] += dk_w[0:n_prev, :]
                    dk_acc[cur, 0:n_cur, lanes] += dk_w[n_prev:KW, :]
                    dq2 = _nn(dsb, kw)
                    dq = jnp.where(lane_hi, dq2[QB:2 * QB, :], dq2[0:QB, :]) * scale
                    dq_scr[r0:r0 + QB, lanes] = dq.astype(BF16)
            return block

        @pl.when(i == 0)
        def _():
            for b in range(TQ // QB):
                make_block(True)(b)

        @pl.when((i > 0) & (i < n))
        def _():
            for b in range(TQ // QB):
                make_block(False)(b)

        @pl.when(i > 0)
        def _():
            dqkv_ref[:, D_ATT:2 * D_ATT] = dk_acc[prv].astype(BF16)
            dqkv_ref[:, 2 * D_ATT:3 * D_ATT] = dv_acc[prv].astype(BF16)

        @pl.when(i == n)
        def _():
            d_iota = lax.broadcasted_iota(jnp.int32, (DIAG, rel_pad), 0)
            n_iota = lax.broadcasted_iota(jnp.int32, (DIAG, rel_pad), 1)
            diff = jnp.where(d_iota < KW, d_iota, d_iota - DIAG)
            idx = jnp.clip(N_LEFT * CHUNK - diff, -MAX_REL, MAX_REL) + MAX_REL
            onehot = (idx == n_iota).astype(F32)
            rows = []
            for hd in range(HEADS):
                acc = dbias_acc[hd // 2, (hd % 2) * QB:(hd % 2 + 1) * QB, :]
                a = jnp.concatenate([acc, jnp.zeros((QB, DIAG - KW), F32)], axis=1)
                g8 = a[0:SUBLANES, :]
                for blk in range(1, QB // SUBLANES):
                    g8 = g8 + pltpu.roll(a[blk * SUBLANES:(blk + 1) * SUBLANES, :], DIAG - blk * SUBLANES, 1)
                g1 = g8[0:1, :]
                for r in range(1, SUBLANES):
                    g1 = g1 + pltpu.roll(g8[r:r + 1, :], DIAG - r, 1)
                rows.append(g1)
            g = jnp.concatenate(rows, axis=0)
            dbias_ref[...] = jnp.dot(g, onehot, preferred_element_type=F32, precision=lax.Precision.HIGHEST)
            finish()

    last = n - 1
    cur = lambda grp: pl.BlockSpec((None, TQ, D_ATT), lambda i: (grp, jnp.minimum(i, last), 0))
    prev = lambda grp: pl.BlockSpec((None, TQ, D_ATT), lambda i: (grp, jnp.maximum(jnp.minimum(i, last) - 1, 0), 0))
    tile = pl.BlockSpec((TQ, D_ATT), lambda i: (jnp.minimum(i, last), 0))
    return pl.pallas_call(
        body, name="attn_bwd", grid=(n + 1,),
        in_specs=[pl.BlockSpec((HEADS, DIAG), lambda i: (0, 0)),
                  cur(0), prev(1), cur(1), prev(2), cur(2), tile, tile,
                  pl.BlockSpec((TQ, HEADS), lambda i: (jnp.minimum(i, last), 0))] + [ANY] * npart,
        out_specs=[pl.BlockSpec((TQ, 3 * D_ATT), lambda i: (jnp.maximum(i - 1, 0), 0)),
                   pl.BlockSpec((HEADS, rel_pad), lambda i: (0, 0))] + [ANY] * npart,
        out_shape=[jax.ShapeDtypeStruct((s, 3 * D_ATT), BF16), jax.ShapeDtypeStruct((HEADS, rel_pad), F32)]
        + _scatter_out_shapes(parts),
        scratch_shapes=[pltpu.VMEM((HEADS // 2, 2 * QB, KW), F32), pltpu.VMEM((HEADS // 2, 2 * QB, KW), F32),
                        pltpu.VMEM((2, TQ, D_ATT), F32), pltpu.VMEM((2, TQ, D_ATT), F32),
                        pltpu.VMEM((TQ, D_ATT), BF16)] + _scatter_sems(npart),
        compiler_params=_params("arbitrary"),
    )(diag, proj, proj, proj, proj, proj, d_att, att, lse, *parts)


def _shift_down(a, k, halo):
    rolled = pltpu.roll(a, k, 0)
    row = lax.broadcasted_iota(jnp.int32, halo.shape, 0)
    first = jnp.where(row < k, pltpu.roll(halo, k, 0), rolled[0:SUBLANES, :])
    return jnp.concatenate([first, rolled[SUBLANES:, :]], axis=0)


def _shift_up(a, k, nxt):
    tm = a.shape[0]
    rolled = pltpu.roll(a, tm - k, 0)
    row = lax.broadcasted_iota(jnp.int32, nxt.shape, 0)
    last = jnp.where(row >= SUBLANES - k, pltpu.roll(nxt, SUBLANES - k, 0), rolled[tm - SUBLANES:, :])
    return jnp.concatenate([rolled[:tm - SUBLANES, :], last], axis=0)


def _sigmoid(v):
    return 0.5 * jnp.tanh(0.5 * v) + 0.5


def _mixer_mid(att, proj, x, tgt, w_att, w_conv, w_out, conv_w, conv_b, fin_g):
    s, d = x.shape
    dc = D_ATT
    n = s // TM_MID
    tm = TM_MID
    n_shards = 4

    def body(att_ref, za_ref, gb_ref, gc_ref, u_ref, zc_ref, hgc_ref, hu_ref, gatt_ref, gconv_ref, x_ref, t_ref,
             watt_ref, wconv_ref, wout_ref, cw_ref, cb_ref, fg_ref,
             dpb_ref, do_ref, dx2_ref, gatt_o, gconv_o, gout_o, loss_o, gfn_o, gcb_o, gcw_o,
             acc_att, acc_conv, acc_out, carry, watt_t_ref, wconv_t_ref, wout_t_ref):
        i = pl.program_id(0)
        tile = n - 1 - i

        @pl.when(i == 0)
        def _():
            acc_att[...] = jnp.zeros_like(acc_att)
            acc_conv[...] = jnp.zeros_like(acc_conv)
            acc_out[...] = jnp.zeros_like(acc_out)
            carry[...] = jnp.zeros_like(carry)
            loss_o[...] = jnp.zeros_like(loss_o)
            gfn_o[...] = jnp.zeros_like(gfn_o)
            gcb_o[...] = jnp.zeros_like(gcb_o)
            gcw_o[...] = jnp.zeros_like(gcw_o)
            watt_t_ref[...] = watt_ref[...].T
            wconv_t_ref[...] = wconv_ref[...].T
            wout_t_ref[...] = wout_ref[...].T

        halves = [slice(hh * (tm // 2), (hh + 1) * (tm // 2)) for hh in range(2)]
        both = lambda fn: [fn(rows) for rows in halves]
        f32 = lambda ref, rows: ref[rows, :].astype(F32)

        gc = gc_ref[...].astype(F32)
        u = u_ref[...].astype(F32)
        cu = gc * u
        halo = jnp.where(tile > 0, hgc_ref[...].astype(F32) * hu_ref[...].astype(F32), 0.0)
        cu1 = _shift_down(cu, 1, halo)
        cu2 = _shift_down(cu, 2, halo)
        w0, w1, w2 = cw_ref[0:1, :], cw_ref[1:2, :], cw_ref[2:3, :]
        fg = fg_ref[...]

        def stage_a(rows):
            att_v, za, zc, gb = f32(att_ref, rows), f32(za_ref, rows), f32(zc_ref, rows), f32(gb_ref, rows)
            sa = _sigmoid(za)
            silu_a = za * sa
            vconv = w0 * cu2[rows, :] + w1 * cu1[rows, :] + w2 * cu[rows, :] + cb_ref[...]
            sc = _sigmoid(zc)
            silu_c = zc * sc
            return dict(att_v=att_v, za=za, zc=zc, gb=gb, sa=sa, silu_a=silu_a, vconv=vconv, sc=sc, silu_c=silu_c,
                        a_b=(att_v * silu_a).astype(BF16), c_b=(gb * vconv * silu_c).astype(BF16))

        st = both(stage_a)
        for t in st:
            t["y_att"] = _nn(t["a_b"], watt_ref[...])
            t["y_conv"] = _nn(t["c_b"], wconv_ref[...])
        for t, rows in zip(st, halves):
            gpair = lambda ref: jnp.concatenate([ref[0, rows, :], ref[1, rows, :]], axis=1).astype(F32)
            t["ga"] = _sigmoid(gpair(gatt_ref))
            t["gv"] = _sigmoid(gpair(gconv_ref))
            t["m_b"] = (t["ga"] * t["y_att"] + t["gv"] * t["y_conv"]).astype(BF16)
        for t in st:
            t["mo"] = _nn(t["m_b"], wout_ref[...])
        for t, rows in zip(st, halves):
            x2 = x_ref[rows, :] + t["mo"]
            r2 = lax.rsqrt(jnp.mean(x2 * x2, axis=-1, keepdims=True) + EPS)
            x2n = x2 * r2
            err = x2n * fg - t_ref[rows, :]
            loss_o[...] += jnp.sum(err * err, axis=0, keepdims=True) * (0.5 / d)
            dy = err * (1.0 / d)
            gfn_o[...] += jnp.sum(dy * x2n, axis=0, keepdims=True)
            dyn = dy * fg
            dx2 = r2 * (dyn - x2n * jnp.mean(dyn * x2n, axis=-1, keepdims=True))
            dx2_ref[rows, :] = dx2
            t["dx2_b"] = dx2.astype(BF16)
        for t in st:
            t["dm"] = _nn(t["dx2_b"], wout_t_ref[...])
        whole = lambda key: jnp.concatenate([st[0][key], st[1][key]], axis=0)
        acc_out[...] += _tn(whole("m_b"), whole("dx2_b"))
        for t, rows in zip(st, halves):
            dy_att = t["dm"] * t["ga"]
            dy_conv = t["dm"] * t["gv"]
            dpb_ref[rows, 5 * dc:5 * dc + d] = (dy_att * t["y_att"] * (1.0 - t["ga"])).astype(BF16)
            dpb_ref[rows, 5 * dc + d:5 * dc + 2 * d] = (dy_conv * t["y_conv"] * (1.0 - t["gv"])).astype(BF16)
            t["dya_b"] = dy_att.astype(BF16)
            t["dyc_b"] = dy_conv.astype(BF16)
        for t in st:
            t["da_in"] = _nn(t["dya_b"], watt_t_ref[...])
            t["dc_in"] = _nn(t["dyc_b"], wconv_t_ref[...])
        acc_att[...] += _tn(whole("a_b"), whole("dya_b"))
        acc_conv[...] += _tn(whole("c_b"), whole("dyc_b"))
        for t, rows in zip(st, halves):
            sa, za, sc, zc = t["sa"], t["za"], t["sc"], t["zc"]
            do_ref[rows, :] = (t["da_in"] * t["silu_a"]).astype(BF16)
            dpb_ref[rows, 0:dc] = (t["da_in"] * t["att_v"] * (sa * (1.0 + za * (1.0 - sa)))).astype(BF16)
            dpb_ref[rows, dc:2 * dc] = (t["dc_in"] * t["vconv"] * t["silu_c"]).astype(BF16)
            dgs = t["dc_in"] * t["gb"]
            t["dvc"] = dgs * t["silu_c"]
            dpb_ref[rows, 4 * dc:5 * dc] = (dgs * t["vconv"] * (sc * (1.0 + zc * (1.0 - sc)))).astype(BF16)
        dvc = whole("dvc")
        gcb_o[...] += jnp.sum(dvc, axis=0, keepdims=True)
        gcw_o[0:1, :] += jnp.sum(dvc * cu2, axis=0, keepdims=True)
        gcw_o[1:2, :] += jnp.sum(dvc * cu1, axis=0, keepdims=True)
        gcw_o[2:3, :] += jnp.sum(dvc * cu, axis=0, keepdims=True)
        nxt = carry[...]
        dcu = w2 * dvc + w1 * _shift_up(dvc, 1, nxt) + w0 * _shift_up(dvc, 2, nxt)
        carry[...] = dvc[0:SUBLANES, :]
        dpb_ref[:, 2 * dc:3 * dc] = (dcu * u).astype(BF16)
        dpb_ref[:, 3 * dc:4 * dc] = (dcu * gc).astype(BF16)

        @pl.when(i == n - 1)
        def _():
            for j in range(n_shards):
                gatt_o[j] = acc_att[:, j * (d // n_shards):(j + 1) * (d // n_shards)].astype(BF16)
                gconv_o[j] = acc_conv[:, j * (d // n_shards):(j + 1) * (d // n_shards)].astype(BF16)
                gout_o[j] = acc_out[j * (d // n_shards):(j + 1) * (d // n_shards), :].astype(BF16)

    rev = lambda width, col_blk: pl.BlockSpec((tm, width), lambda i: (n - 1 - i, col_blk))
    grp = lambda g: pl.BlockSpec((None, tm, dc), lambda i: (g, n - 1 - i, 0))
    grp2 = lambda g2: pl.BlockSpec((2, tm, dc), lambda i: (g2, n - 1 - i, 0))
    halo_spec = lambda g: pl.BlockSpec(
        (None, SUBLANES, dc), lambda i: (g, jnp.maximum((n - 1 - i) * (tm // SUBLANES) - 1, 0), 0))
    const = lambda shape: pl.BlockSpec(shape, lambda i: tuple(0 for _ in shape), pipeline_mode=pl.Buffered(1))
    q4 = d // n_shards
    return pl.pallas_call(
        body, name="mixer_mid", grid=(n,),
        in_specs=[rev(dc, 0), grp(3), grp(4), grp(5), grp(6), grp(7),
                  halo_spec(5), halo_spec(6), grp2(4), grp2(5), rev(d, 0), rev(d, 0),
                  const((dc, d)), const((dc, d)), const((d, d)),
                  const(conv_w.shape), const((1, dc)), const((1, d))],
        out_specs=[rev(5 * dc + 2 * d, 0), rev(dc, 0), rev(d, 0),
                   const((n_shards, dc, q4)), const((n_shards, dc, q4)), const((n_shards, q4, d)),
                   const((1, d)), const((1, d)), const((1, dc)), const((SUBLANES, dc))],
        out_shape=[jax.ShapeDtypeStruct((s, 5 * dc + 2 * d), BF16), jax.ShapeDtypeStruct((s, dc), BF16),
                   jax.ShapeDtypeStruct((s, d), F32),
                   jax.ShapeDtypeStruct((n_shards, dc, q4), BF16), jax.ShapeDtypeStruct((n_shards, dc, q4), BF16),
                   jax.ShapeDtypeStruct((n_shards, q4, d), BF16),
                   jax.ShapeDtypeStruct((1, d), F32), jax.ShapeDtypeStruct((1, d), F32),
                   jax.ShapeDtypeStruct((1, dc), F32), jax.ShapeDtypeStruct((SUBLANES, dc), F32)],
        scratch_shapes=[pltpu.VMEM((dc, d), F32), pltpu.VMEM((dc, d), F32), pltpu.VMEM((d, d), F32),
                        pltpu.VMEM((SUBLANES, dc), F32),
                        pltpu.VMEM((d, dc), BF16), pltpu.VMEM((d, dc), BF16), pltpu.VMEM((d, d), BF16)],
        compiler_params=_params("arbitrary"),
    )(att, proj, proj, proj, proj, proj, proj, proj, proj, proj, x, tgt,
      w_att, w_conv, w_out, conv_w, conv_b, fin_g)


def _in_proj_bwd_x(dqkv, dpb, w_in, x, dx2, g):
    s, d = x.shape
    tn = dqkv.shape[1]
    nb = dpb.shape[1] // tn
    n = s // TM_MM

    def body(*refs):
        dps, ws = refs[:nb + 1], refs[nb + 1:2 * nb + 2]
        x_ref, dx2_ref, g_ref, gx_ref, gng_ref = refs[2 * nb + 2:]
        i = pl.program_id(0)

        @pl.when(i == 0)
        def _():
            gng_ref[...] = jnp.zeros_like(gng_ref)

        dh = _nt(dps[0][...], ws[0][0])
        for j in range(1, nb + 1):
            dh = dh + _nt(dps[j][...], ws[j][0])
        xv = x_ref[...]
        r = lax.rsqrt(jnp.mean(xv * xv, axis=-1, keepdims=True) + EPS)
        xn = xv * r
        gng_ref[...] += jnp.sum(dh * xn, axis=0, keepdims=True)
        dhn = dh * g_ref[...]
        gx_ref[...] = dx2_ref[...] + r * (dhn - xn * jnp.mean(dhn * xn, axis=-1, keepdims=True))

    tile = lambda width, col_blk: pl.BlockSpec((TM_MM, width), lambda i: (i, col_blk))
    wspec = lambda blk: pl.BlockSpec((1, d, tn), lambda i: (blk, 0, 0), pipeline_mode=pl.Buffered(1))
    return pl.pallas_call(
        body, name="in_proj_bwd_x", grid=(n,),
        in_specs=[tile(tn, 0)] + [tile(tn, j) for j in range(nb)] + [wspec(j) for j in range(nb + 1)]
        + [tile(d, 0), tile(d, 0), pl.BlockSpec((1, d), lambda i: (0, 0))],
        out_specs=[tile(d, 0), pl.BlockSpec((1, d), lambda i: (0, 0))],
        out_shape=[jax.ShapeDtypeStruct((s, d), F32), jax.ShapeDtypeStruct((1, d), F32)],
        compiler_params=_params("arbitrary"),
    )(dqkv, *([dpb] * nb), *([w_in] * (nb + 1)), x, dx2, g)


def _in_proj_bwd_w(h, dqkv, dpb, order, slots, small):
    s, d = h.shape
    tn = dqkv.shape[1]
    n = s // TM_BLK
    hr = d // 2
    settle = min(2, n - 1)
    nw, ns = len(slots), len(small)
    half_rows = [sl.shape[1] for sl in slots] + [hr]

    def body(order_ref, h_ref, da_ref, db_ref, *refs):
        srcs, parts = refs[:nw], refs[nw:nw + ns]
        slots_ref, shards, sm_all = refs[nw + ns], refs[nw + ns + 1:2 * nw + ns + 2], refs[2 * nw + ns + 2]
        acc, sendbuf, pairbuf, chipbuf = refs[2 * nw + ns + 3:2 * nw + ns + 7]
        halves, sm = refs[2 * nw + ns + 7:3 * nw + ns + 7], refs[3 * nw + ns + 7]
        psend, precv, send, recv, lsem, hsend, hrecv, ssend, srecv, hsem = refs[3 * nw + ns + 8:]
        j, i = pl.program_id(0), pl.program_id(1)
        blk = order_ref[j]
        pos = _position()
        x, y, c = pos
        me = 4 * x + 2 * y + c

        def half_sum(w):
            return halves[w] if w < nw else acc.at[pl.ds(0, hr), :]

        def half_of_shard(w, half):
            return shards[w].at[pl.ds(half * half_rows[w], half_rows[w]), :]

        def exchange(w, half):
            return pltpu.make_async_remote_copy(
                src_ref=half_sum(w), dst_ref=half_of_shard(w, half), send_sem=hsend.at[w], recv_sem=hrecv.at[w],
                device_id=(x, y, 1 - c), device_id_type=MESH)

        def keep(w):
            return pltpu.make_async_copy(half_sum(w), half_of_shard(w, c), hsem.at[w])

        def bcast(k, slot):
            return pltpu.make_async_remote_copy(
                src_ref=sm, dst_ref=sm_all.at[slot], send_sem=ssend.at[k], recv_sem=srecv.at[k],
                device_id=_peer(pos, k), device_id_type=MESH)

        own_small = pltpu.make_async_copy(sm, sm_all.at[me], hsem.at[nw + 1])

        @pl.when((j == 0) & (i == 0))
        def _():
            _pack_small(sm, *parts)
            own_small.start()
            for k in range(1, N_DEV):
                bcast(k, me).start()
            for w in range(nw):
                total = srcs[w][0].astype(F32)
                for k in range(1, srcs[w].shape[0]):
                    total = total + srcs[w][k].astype(F32)
                halves[w][...] = total
                exchange(w, c).start()
                keep(w).start()

        @pl.when(i == 0)
        def _():
            acc[...] = jnp.zeros_like(acc)

        @pl.when(blk == 0)
        def _():
            acc[...] += _tn(h_ref[...], da_ref[...])

        @pl.when(blk > 0)
        def _():
            acc[...] += _tn(h_ref[...], db_ref[...])

        def pair(step, half):
            return pltpu.make_async_remote_copy(
                src_ref=sendbuf.at[step, pl.ds(half * hr, hr), :], dst_ref=pairbuf.at[step],
                send_sem=psend.at[step], recv_sem=precv.at[step], device_id=(x, y, 1 - c), device_id_type=MESH)

        def ici(step):
            flip = OWNER_FLIPS[step]
            return pltpu.make_async_remote_copy(
                src_ref=chipbuf.at[step], dst_ref=slots_ref.at[flip], send_sem=send.at[step], recv_sem=recv.at[step],
                device_id=_peer(pos, 4 * (flip >> 1) + 2 * (flip & 1)), device_id_type=MESH)

        local = pltpu.make_async_copy(chipbuf.at[N_CHIPS - 1], slots_ref.at[0], lsem.at[0])

        def combine(step):
            pair(step, c).wait_recv()
            mine = sendbuf[step, pl.ds(c * hr, hr), :].astype(F32)
            chipbuf[step] = (mine + pairbuf[step].astype(F32)).astype(BF16)

        for step in range(N_CHIPS):
            @pl.when((j == step) & (i == n - 1))
            def _(step=step):
                sendbuf[step] = acc[...].astype(BF16)
                pair(step, 1 - c).start()

        for step in range(N_CHIPS - 1):
            @pl.when((j == step + 1) & (i == settle))
            def _(step=step):
                combine(step)
                ici(step).start()

        @pl.when((j == N_CHIPS - 1) & (i == n - 1))
        def _():
            staged = [pltpu.make_async_copy(slots_ref.at[f], pairbuf.at[f - 1], hsem.at[nw + 1 + f])
                      for f in range(1, N_CHIPS)]
            for step in range(N_CHIPS - 1):
                ici(step).wait_recv()
            for cp in staged:
                cp.start()
            combine(N_CHIPS - 1)
            local.start()
            for cp in staged:
                cp.wait()
            total = chipbuf[N_CHIPS - 1].astype(F32)
            for f in range(1, N_CHIPS):
                total = total + pairbuf[f - 1].astype(F32)
            acc[0:hr, :] = total
            exchange(nw, c).start()
            keep(nw).start()
            for w in range(nw + 1):
                exchange(w, 1 - c).wait_recv()
            for k in range(1, N_DEV):
                px, py, pc = _peer(pos, k)
                bcast(k, 4 * px + 2 * py + pc).wait_recv()
            for step in range(N_CHIPS - 1):
                ici(step).wait_send()
            for step in range(N_CHIPS):
                pair(step, 1 - c).wait_send()
            local.wait()
            for w in range(nw + 1):
                exchange(w, c).wait_send()
                keep(w).wait()
            for k in range(1, N_DEV):
                bcast(k, me).wait_send()
            own_small.wait()

    vmem = pl.BlockSpec(memory_space=pltpu.VMEM)
    shard_shapes = [(2 * sl.shape[1], sl.shape[2]) for sl in slots]
    out = pl.pallas_call(
        body, name="in_proj_bwd_w",
        grid_spec=pltpu.PrefetchScalarGridSpec(
            num_scalar_prefetch=1, grid=(N_CHIPS, n),
            in_specs=[pl.BlockSpec((TM_BLK, d), lambda j, i, order: (i, 0)),
                      pl.BlockSpec((TM_BLK, tn), lambda j, i, order: (jnp.where(order[j] == 0, i, 0), 0)),
                      pl.BlockSpec((TM_BLK, tn), lambda j, i, order: (jnp.where(order[j] == 0, 0, i),
                                                                     jnp.maximum(order[j] - 1, 0)))]
            + [vmem] * (nw + ns),
            out_specs=[ANY] * (nw + 3),
            scratch_shapes=[pltpu.VMEM((d, tn), F32), pltpu.VMEM((N_CHIPS, d, tn), BF16),
                            pltpu.VMEM((N_CHIPS, hr, tn), BF16), pltpu.VMEM((N_CHIPS, hr, tn), BF16)]
            + [pltpu.VMEM((r // 2, cc), F32) for r, cc in shard_shapes] + [pltpu.VMEM((SMALL_ROWS, SMALL_COLS), F32)]
            + [pltpu.SemaphoreType.DMA((N_CHIPS,)), pltpu.SemaphoreType.DMA((N_CHIPS,)),
               pltpu.SemaphoreType.DMA((N_CHIPS - 1,)), pltpu.SemaphoreType.DMA((N_CHIPS - 1,)),
               pltpu.SemaphoreType.DMA((1,)),
               pltpu.SemaphoreType.DMA((nw + 1,)), pltpu.SemaphoreType.DMA((nw + 1,)),
               pltpu.SemaphoreType.DMA((N_DEV,)), pltpu.SemaphoreType.DMA((N_DEV,)),
               pltpu.SemaphoreType.DMA((nw + 1 + N_CHIPS,))]),
        out_shape=[jax.ShapeDtypeStruct((N_CHIPS, hr, tn), BF16)]
        + [jax.ShapeDtypeStruct(sh, F32) for sh in shard_shapes + [(d, tn)]]
        + [jax.ShapeDtypeStruct((N_DEV, SMALL_ROWS, SMALL_COLS), F32)],
        compiler_params=_params("arbitrary", "arbitrary"),
    )(order, h, dqkv, dpb, *slots, *small)
    return [out[nw + 1]] + list(out[1:nw + 1]) + [out[nw + 2]]


LOSS_ROW = 6


def _adam_update(w, g, m, v):
    c1 = 1.0 / (1.0 - ADAM_B1 ** ADAM_STEP)
    c2 = 1.0 / (1.0 - ADAM_B2 ** ADAM_STEP)
    m2 = ADAM_B1 * m + (1.0 - ADAM_B1) * g
    v2 = ADAM_B2 * v + (1.0 - ADAM_B2) * (g * g)
    return -ADAM_LR * ((m2 * c1) / (jnp.sqrt(v2 * c2) + ADAM_EPS) + ADAM_WD * w), m2, v2


def _adamw_small(recv, params, moments_m, moments_v, out_shapes):
    k = recv.shape[0]
    n_par = len(params)
    cshard = params[3].shape[1]

    def body(*refs):
        r_ref = refs[0]
        ws, ms, vs = refs[1:1 + n_par], refs[1 + n_par:1 + 2 * n_par], refs[1 + 2 * n_par:1 + 3 * n_par]
        loss_ref = refs[1 + 3 * n_par]
        outs = refs[2 + 3 * n_par:]
        total = r_ref[0]
        for slot in range(1, k):
            total = total + r_ref[slot]
        loss_ref[...] = jnp.sum(total[LOSS_ROW:LOSS_ROW + 1, :], axis=1, keepdims=True)
        chip = 2 * lax.axis_index("x") + lax.axis_index("y")
        g_cw = jnp.zeros((3, cshard), F32)
        for sh in range(N_CHIPS):
            g_cw = g_cw + jnp.where(chip == sh, total[3:6, sh * cshard:(sh + 1) * cshard], 0.0)
        grads = [total[0:1, :], total[1:2, :], total[2:3, :ws[2].shape[1]], g_cw, total[8:16, :ws[4].shape[1]]]
        for p in range(n_par):
            delta, m2, v2 = _adam_update(ws[p][...], grads[p], ms[p][...], vs[p][...])
            for q, val in enumerate((grads[p], delta, m2, v2)):
                out = outs[4 * p + q]
                out[...] = val[0] if len(out.shape) == 1 else val

    shapes = [jax.ShapeDtypeStruct((1, 1), F32)]
    for shape in out_shapes:
        shapes += [jax.ShapeDtypeStruct(shape, F32)] * 4
    return pl.pallas_call(body, name="adamw_small", out_shape=shapes)(recv, *params, *moments_m, *moments_v)


def _adamw_group(ws, gs, ms, vs, name, steps):
    k = len(ws)

    def body(*refs):
        ins, outs = refs[:4 * k], refs[4 * k:]
        for j in range(k):
            gv = ins[k + j][...]
            outs[4 * j][...] = gv
            outs[4 * j + 1][...], outs[4 * j + 2][...], outs[4 * j + 3][...] = _adam_update(
                ins[j][...], gv, ins[2 * k + j][...], ins[3 * k + j][...])

    specs = [pl.BlockSpec((w.shape[0] // steps, w.shape[1]), lambda i: (i, 0)) for w in ws]
    out = pl.pallas_call(
        body, name=name, grid=(steps,),
        in_specs=specs * 4, out_specs=[spec for spec in specs for _ in range(4)],
        out_shape=[jax.ShapeDtypeStruct(w.shape, F32) for w in ws for _ in range(4)],
        compiler_params=_params("parallel"),
    )(*ws, *gs, *ms, *vs)
    return [tuple(out[4 * j:4 * j + 4]) for j in range(k)]


ANY = pl.BlockSpec(memory_space=pl.ANY)
N_CHIPS = 4
N_DEV = 8
OWNER_FLIPS = (3, 1, 2, 0)


def _position():
    return lax.axis_index("x"), lax.axis_index("y"), lax.axis_index("c")


def _gather_out_shapes(shards, kinds, cw3):
    full = [(a.shape[0], a.shape[1] * N_CHIPS) if k == "cols" else (a.shape[0] * N_CHIPS, a.shape[1])
            for a, k in zip(shards, kinds)]
    return [jax.ShapeDtypeStruct(f, BF16) for f in full] + [
        jax.ShapeDtypeStruct((N_CHIPS,) + cw3.shape, cw3.dtype)]


def _gather_sems(nw):
    return [pltpu.SemaphoreType.DMA((3, nw)), pltpu.SemaphoreType.DMA((3, nw)),
            pltpu.SemaphoreType.DMA((3, nw)), pltpu.SemaphoreType.DMA((3, nw)),
            pltpu.SemaphoreType.DMA((3,)), pltpu.SemaphoreType.DMA((3,)), pltpu.SemaphoreType.DMA((nw + 1,))]


def _gather_plan(kinds, srcs, dsts, cw, cw_all, send1, recv1, send2, recv2, ssend, srecv, lsem):
    nw = len(srcs)
    x, y, c = _position()
    mine = 2 * x + y
    chips = [(x, 1 - y), (1 - x, y), (1 - x, 1 - y)]

    def window(w, shard, half):
        r, cc = srcs[w].shape
        hr = r // 2
        if kinds[w] == "cols":
            rows = pl.ds(0, r) if half is None else pl.ds(half * hr, hr)
            return dsts[w].at[rows, pl.ds(shard * cc, cc)]
        rows = pl.ds(shard * r, r) if half is None else pl.ds(shard * r + half * hr, hr)
        return dsts[w].at[rows, :]

    def my_half(w):
        hr = srcs[w].shape[0] // 2
        return srcs[w].at[pl.ds(c * hr, hr), :]

    def local():
        return [pltpu.make_async_copy(srcs[w], window(w, mine, None), lsem.at[w]) for w in range(nw)] + [
            pltpu.make_async_copy(cw, cw_all.at[mine], lsem.at[nw])]

    def ici(k, w, shard):
        kx, ky = chips[k]
        return pltpu.make_async_remote_copy(
            src_ref=my_half(w), dst_ref=window(w, shard, c), send_sem=send1.at[k, w], recv_sem=recv1.at[k, w],
            device_id=(kx, ky, c), device_id_type=MESH)

    def d2d(k, w, shard, half):
        return pltpu.make_async_remote_copy(
            src_ref=window(w, shard, half), dst_ref=window(w, shard, half),
            send_sem=send2.at[k, w], recv_sem=recv2.at[k, w], device_id=(x, y, 1 - c), device_id_type=MESH)

    def small(k, shard):
        kx, ky = chips[k]
        return pltpu.make_async_remote_copy(
            src_ref=cw, dst_ref=cw_all.at[shard], send_sem=ssend.at[k], recv_sem=srecv.at[k],
            device_id=(kx, ky, c), device_id_type=MESH)

    def theirs(k):
        kx, ky = chips[k]
        return 2 * kx + ky

    def start():
        for cp in local():
            cp.start()
        for k in range(3):
            for w in range(nw):
                ici(k, w, mine).start()
            small(k, mine).start()

    def forward():
        for k in range(3):
            for w in range(nw):
                ici(k, w, theirs(k)).wait_recv()
                d2d(k, w, theirs(k), c).start()

    def finish():
        for k in range(3):
            for w in range(nw):
                d2d(k, w, theirs(k), 1 - c).wait_recv()
            small(k, theirs(k)).wait_recv()
        for k in range(3):
            for w in range(nw):
                ici(k, w, mine).wait_send()
                d2d(k, w, theirs(k), c).wait_send()
            small(k, mine).wait_send()
        for cp in local():
            cp.wait()

    return start, forward, finish


def _scatter_out_shapes(parts):
    return [jax.ShapeDtypeStruct((N_DEV, p.shape[1] // 2, p.shape[2]), p.dtype) for p in parts]


def _scatter_sems(nw):
    return [pltpu.SemaphoreType.DMA((N_DEV, nw)), pltpu.SemaphoreType.DMA((N_DEV, nw)), pltpu.SemaphoreType.DMA((nw,))]


def _peer(pos, k):
    x, y, c = pos
    return ((1 - x) if k & 4 else x, (1 - y) if k & 2 else y, (1 - c) if k & 1 else c)


def _scatter_plan(srcs, dsts, send, recv, lsem):
    nw = len(srcs)
    pos = _position()

    def piece(w, k):
        px, py, pc = _peer(pos, k)
        hr = srcs[w].shape[1] // 2
        return srcs[w].at[2 * px + py, pl.ds(pc * hr, hr), :]

    def remote(w, k):
        return pltpu.make_async_remote_copy(
            src_ref=piece(w, k), dst_ref=dsts[w].at[k], send_sem=send.at[k, w], recv_sem=recv.at[k, w],
            device_id=_peer(pos, k), device_id_type=MESH)

    def local(w):
        return pltpu.make_async_copy(piece(w, 0), dsts[w].at[0], lsem.at[w])

    def start():
        for w in range(nw):
            local(w).start()
        for k in range(1, N_DEV):
            for w in range(nw):
                remote(w, k).start()

    def finish():
        for k in range(1, N_DEV):
            for w in range(nw):
                remote(w, k).wait_recv()
        for k in range(1, N_DEV):
            for w in range(nw):
                remote(w, k).wait_send()
        for w in range(nw):
            local(w).wait()

    return start, finish


def _pack_small(sm, norm_g, fin_g, conv_b, conv_w, loss_vec, rel):
    sm[...] = jnp.zeros_like(sm)
    for row, ref in ((0, norm_g), (1, fin_g), (2, conv_b), (LOSS_ROW, loss_vec)):
        sm[row:row + 1, 0:ref.shape[1]] = ref[...]
    sm[3:6, 0:conv_w.shape[1]] = conv_w[0:3, :]
    sm[8:8 + HEADS, 0:rel.shape[1]] = rel[...]


def kernel(x, norm_g, w_in, rel_bias, w_att_out, conv_w, conv_b, w_conv_out, w_out, final_norm_g, loss_target, m_norm_g, m_w_in, m_rel_bias, m_w_att_out, m_conv_w, m_conv_b, m_w_conv_out, m_w_out, m_final_norm_g, v_norm_g, v_w_in, v_rel_bias, v_w_att_out, v_conv_w, v_conv_b, v_w_conv_out, v_w_out, v_final_norm_g):
    xs, tgt = x[0], loss_target[0]
    cshard = conv_w.shape[2]
    chip = 2 * lax.axis_index("x") + lax.axis_index("y")

    shards = [w_in[0], w_att_out[0], w_conv_out[0], w_out[0]]
    cw3 = conv_w[0]
    flips = jnp.arange(N_CHIPS, dtype=jnp.int32)
    own_first = jnp.bitwise_xor(chip, flips)
    own_last = jnp.bitwise_xor(chip, jnp.asarray(OWNER_FLIPS, jnp.int32))

    proj, h, wb_in = _in_proj_gather(xs, norm_g, shards[0], own_first)
    diag = jnp.take(rel_bias[0], _diag_rel_index(), axis=1)
    att, lse, wb_att, wb_conv, wb_out, cw_all = _attn_fwd(diag, proj, shards[1:], ["cols", "cols", "rows"], cw3)
    conv_w_full = jnp.transpose(cw_all, (1, 0, 2)).reshape(cw3.shape[0], N_CHIPS * cshard)
    (dpb, d_att, dx2, g_att_p, g_conv_p, g_out_p, loss_vec, g_fin, g_cb, g_cw) = _mixer_mid(
        att, proj, xs, tgt, wb_att, wb_conv, wb_out, conv_w_full, conv_b,
        final_norm_g[None, :])
    dqkv, g_rel, r_att, r_conv, r_out = _attn_bwd(diag, proj, d_att, att, lse, [g_att_p, g_conv_p, g_out_p])
    grad_x, g_norm = _in_proj_bwd_x(dqkv, dpb, wb_in, xs, dx2, norm_g)
    gw_in, gw_att, gw_conv, gw_out, r_small = _in_proj_bwd_w(
        h, dqkv, dpb, own_last, [r_att, r_conv, r_out], [g_norm, g_fin, g_cb, g_cw, loss_vec, g_rel])

    small_out = _adamw_small(
        r_small,
        [norm_g, final_norm_g[None, :], conv_b, conv_w[0], rel_bias[0]],
        [m_norm_g, m_final_norm_g[None, :], m_conv_b, m_conv_w[0], m_rel_bias[0]],
        [v_norm_g, v_final_norm_g[None, :], v_conv_b, v_conv_w[0], v_rel_bias[0]],
        [norm_g.shape, final_norm_g.shape, conv_b.shape, conv_w[0].shape, rel_bias[0].shape])
    loss = small_out[0][0, 0]
    small_names = ["norm_g", "final_norm_g", "conv_b", "conv_w", "rel_bias"]
    fix = {"norm_g": lambda a: a, "final_norm_g": lambda a: a, "conv_b": lambda a: a,
           "conv_w": lambda a: a[None], "rel_bias": lambda a: a[None]}
    small_res = {name: [fix[name](small_out[1 + 4 * p + q]) for q in range(4)] for p, name in enumerate(small_names)}

    names = ["w_in", "w_att_out", "w_conv_out", "w_out"]
    group = _adamw_group([w_in[0], w_att_out[0], w_conv_out[0], w_out[0]], [gw_in, gw_att, gw_conv, gw_out],
                         [m_w_in[0], m_w_att_out[0], m_w_conv_out[0], m_w_out[0]],
                         [v_w_in[0], v_w_att_out[0], v_w_conv_out[0], v_w_out[0]], "adamw_matrices", ADAMW_STEPS)
    big = dict(zip(names, group))
    big = {name: tuple(a[None] for a in four) for name, four in big.items()}

    order = ["norm_g", "w_in", "rel_bias", "w_att_out", "conv_w", "conv_b", "w_conv_out", "w_out", "final_norm_g"]
    outs = [loss, grad_x[None]]
    for which in range(4):
        for name in order:
            outs.append(big[name][which] if name in big else small_res[name][which])
    return tuple(outs)
```

```python
import numpy as np
import jax
import jax.numpy as jnp
from jax import lax
from jax.experimental import pallas as pl
from jax.experimental.pallas import tpu as pltpu

F32 = jnp.float32
BF16 = jnp.bfloat16
MESH = pl.DeviceIdType.MESH

CHUNK = 64
N_LEFT = 8
HEADS = 8
HEAD_DIM = 64
D_ATT = HEADS * HEAD_DIM
MAX_REL = 128
N_REL = 2 * MAX_REL + 1
EPS = 1e-6
NEG_BIG = -1e30
ADAM_LR, ADAM_B1, ADAM_B2, ADAM_EPS, ADAM_WD, ADAM_STEP = 0.001, 0.9, 0.999, 1e-08, 0.01, 10

LANES = 128
SUBLANES = 8
VMEM_LIMIT = 56 * 1024 * 1024

QB = 2 * CHUNK
KW = N_LEFT * CHUNK + QB
DIAG = KW + QB
TQ = N_LEFT * CHUNK
TM_MID = 256
ADAMW_STEPS = 8
TM_MM = 512
TM_BLK = 1024
SMALL_ROWS, SMALL_COLS = 16, 1024


def _params(*sem):
    return pltpu.CompilerParams(dimension_semantics=sem, vmem_limit_bytes=VMEM_LIMIT)


def _nt(a, b):
    return lax.dot_general(a, b, (((1,), (1,)), ((), ())), preferred_element_type=F32)


def _tn(a, b):
    return lax.dot_general(a, b, (((0,), (0,)), ((), ())), preferred_element_type=F32)


def _nn(a, b):
    return jnp.dot(a, b, preferred_element_type=F32)


def _diag_rel_index():
    d = np.arange(DIAG)
    diff = np.where(d < KW, d, d - DIAG)
    rel = N_LEFT * CHUNK - diff
    return np.clip(rel, -MAX_REL, MAX_REL) + MAX_REL


def _build_bias(diag_ref, bias_scr):
    r = lax.broadcasted_iota(jnp.int32, (QB, KW), 0) // CHUNK
    s = lax.broadcasted_iota(jnp.int32, (QB, KW), 1) // CHUNK
    allowed = (s >= r) & (s <= r + N_LEFT)
    for h in range(HEADS):
        row = jnp.broadcast_to(diag_ref[h:h + 1, :], (QB, DIAG))
        t = pltpu.roll(row, 0, 1, stride=1, stride_axis=0)
        bias_scr[h // 2, (h % 2) * QB:(h % 2 + 1) * QB, :] = jnp.where(allowed, t[:, :KW], NEG_BIG)


def _stack_heads(a, lane_hi):
    zero = jnp.zeros_like(a)
    return jnp.concatenate([jnp.where(lane_hi, zero, a), jnp.where(lane_hi, a, zero)], axis=0)


def _in_proj_gather(x, g, shard, order):
    s, d = x.shape
    tn = shard.shape[1]
    n = s // TM_BLK
    hr = d // 2

    def body(order_ref, x_ref, g_ref, shard32_ref, proj_ref, h_ref, wfull_ref, shard_ref, hbuf, wbuf,
             send1, recv1, send2, recv2, lsem):
        del order_ref
        j, i = pl.program_id(0), pl.program_id(1)
        x, y, c = _position()
        mine = 2 * x + y
        chips = [(x, 1 - y), (1 - x, y), (1 - x, 1 - y)]

        def theirs(k):
            return 2 * chips[k][0] + chips[k][1]

        def half_rows(half):
            return pl.ds(half * hr, hr)

        def landing(k, shard_index, half):
            if k < 2:
                return wbuf.at[k, half_rows(half), :]
            return wfull_ref.at[shard_index, half_rows(half), :]

        def ici(k, shard_index):
            return pltpu.make_async_remote_copy(
                src_ref=shard_ref.at[half_rows(c), :], dst_ref=landing(k, shard_index, c),
                send_sem=send1.at[k], recv_sem=recv1.at[k], device_id=(*chips[k], c), device_id_type=MESH)

        def d2d(k, half):
            return pltpu.make_async_remote_copy(
                src_ref=wbuf.at[k % 2, half_rows(half), :], dst_ref=landing(k, theirs(k), half),
                send_sem=send2.at[k], recv_sem=recv2.at[k], device_id=(x, y, 1 - c), device_id_type=MESH)

        def load(k, half, sem):
            return pltpu.make_async_copy(wfull_ref.at[theirs(k), half_rows(half), :],
                                         wbuf.at[k % 2, half_rows(half), :], lsem.at[sem])

        def keep(k):
            return pltpu.make_async_copy(wbuf.at[k], wfull_ref.at[theirs(k)], lsem.at[3 + k])

        own = pltpu.make_async_copy(shard_ref, wfull_ref.at[mine], lsem.at[0])

        def put_proj(block):
            for grp in range(tn // D_ATT):
                proj_ref[grp] = block[:, grp * D_ATT:(grp + 1) * D_ATT].astype(BF16)

        @pl.when((j == 0) & (i == 0))
        def _():
            shard_ref[...] = shard32_ref[...].astype(BF16)
            own.start()
            ici(0, mine).start()
            ici(1, mine).start()

        for k in range(2):
            first = n - 1 if k == 0 else min(n // 2, n - 1)

            @pl.when((j == k) & (i == first))
            def _(k=k):
                if k == 0:
                    ici(0, mine).wait_send()
                    ici(1, mine).wait_send()
                    ici(2, mine).start()
                ici(k, theirs(k)).wait_recv()
                d2d(k, c).start()

            then = (1, 0) if k == 0 else (1, min(first + 1, n - 1))

            @pl.when((j == then[0]) & (i == then[1]))
            def _(k=k):
                d2d(k, 1 - c).wait_recv()
                keep(k).start()

        first = min(n // 2, n - 1)

        @pl.when((j == 2) & (i == first))
        def _():
            d2d(0, c).wait_send()
            keep(0).wait()
            ici(2, theirs(2)).wait_recv()
            load(2, c, 1).start()

        @pl.when((j == 2) & (i == min(first + 1, n - 1)))
        def _():
            load(2, c, 1).wait()
            d2d(2, c).start()

        @pl.when((j == 2) & (i == min(first + 2, n - 1)))
        def _():
            d2d(2, 1 - c).wait_recv()
            load(2, 1 - c, 2).start()

        @pl.when((j == 3) & (i == 0))
        def _():
            load(2, 1 - c, 2).wait()

        @pl.when(j == 0)
        def _():
            xv = x_ref[...]
            r = lax.rsqrt(jnp.mean(xv * xv, axis=-1, keepdims=True) + EPS)
            hv = ((xv * r) * g_ref[...]).astype(BF16)
            hbuf[i] = hv
            h_ref[...] = hv
            put_proj(_nn(hv, shard_ref[...]))

        for k in range(3):
            @pl.when(j == k + 1)
            def _(k=k):
                put_proj(_nn(hbuf[i], wbuf[k % 2]))

        @pl.when((j == 3) & (i == n - 1))
        def _():
            ici(2, mine).wait_send()
            d2d(1, c).wait_send()
            d2d(2, c).wait_send()
            own.wait()
            keep(1).wait()

    return pl.pallas_call(
        body, name="in_proj_gather",
        grid_spec=pltpu.PrefetchScalarGridSpec(
            num_scalar_prefetch=1, grid=(N_CHIPS, n),
            in_specs=[pl.BlockSpec((TM_BLK, d), lambda j, i, order: (jnp.where(j == 0, i, n - 1), 0)),
                      pl.BlockSpec((1, d), lambda j, i, order: (0, 0)), pl.BlockSpec(memory_space=pltpu.VMEM)],
            out_specs=[pl.BlockSpec((tn // D_ATT, TM_BLK, D_ATT), lambda j, i, order: (order[j], i, 0)),
                       pl.BlockSpec((TM_BLK, d), lambda j, i, order: (jnp.where(j == 0, i, n - 1), 0)), ANY],
            scratch_shapes=[pltpu.VMEM((d, tn), BF16), pltpu.VMEM((n, TM_BLK, d), BF16), pltpu.VMEM((2, d, tn), BF16),
                            pltpu.SemaphoreType.DMA((3,)), pltpu.SemaphoreType.DMA((3,)),
                            pltpu.SemaphoreType.DMA((3,)), pltpu.SemaphoreType.DMA((3,)), pltpu.SemaphoreType.DMA((5,))]),
        out_shape=[jax.ShapeDtypeStruct((N_CHIPS * tn // D_ATT, s, D_ATT), BF16), jax.ShapeDtypeStruct((s, d), BF16),
                   jax.ShapeDtypeStruct((N_CHIPS, d, tn), BF16)],
        compiler_params=_params("arbitrary", "arbitrary"),
    )(order, x, g, shard)


def _attn_fwd(diag, proj, shards, kinds, cw3):
    s = proj.shape[1]
    n = s // TQ
    nw = len(shards)
    scale = HEAD_DIM ** -0.5

    def body(*refs):
        diag_ref, q_ref, kp_ref, kc_ref, vp_ref, vc_ref = refs[:6]
        srcs, cw = refs[6:6 + nw], refs[6 + nw]
        o_ref, lse_ref = refs[7 + nw:9 + nw]
        dsts, cw_all = refs[9 + nw:9 + 2 * nw], refs[9 + 2 * nw]
        bias_scr = refs[10 + 2 * nw]
        casts = refs[11 + 2 * nw:11 + 3 * nw]
        start, forward, finish = _gather_plan(kinds, casts, dsts, cw, cw_all, *refs[11 + 3 * nw:])
        i = pl.program_id(0)

        @pl.when(i == 0)
        def _():
            for w in range(nw):
                casts[w][...] = srcs[w][...].astype(BF16)
            start()
            _build_bias(diag_ref, bias_scr)

        @pl.when(i == n // 2)
        def _():
            forward()

        @pl.when(i == n - 1)
        def _():
            finish()

        lane_hi = lax.broadcasted_iota(jnp.int32, (QB, LANES), 1) >= HEAD_DIM

        def block(b, first_tile):
            r0, n_prev = b * QB, TQ - b * QB
            n_cur = KW - n_prev
            pairs = range(HEADS // 2)
            lanes_of = [slice(LANES * p, LANES * (p + 1)) for p in pairs]
            scores = []
            for p in pairs:
                lanes = lanes_of[p]
                q2 = _stack_heads(q_ref[r0:r0 + QB, lanes] * scale, lane_hi)
                s_cur = _nt(q2, kc_ref[0:n_cur, lanes]) + bias_scr[p, :, n_prev:KW]
                if first_tile:
                    scores.append(s_cur)
                else:
                    scores.append(jnp.concatenate(
                        [_nt(q2, kp_ref[r0:TQ, lanes]) + bias_scr[p, :, 0:n_prev], s_cur], axis=1))
            probs = []
            for p in pairs:
                sc = scores[p]
                m = jnp.max(sc, axis=1, keepdims=True)
                pe = jnp.exp(sc - m)
                l = jnp.sum(pe, axis=1, keepdims=True)
                probs.append((pe.astype(BF16), l, m))
            for p in pairs:
                lanes = lanes_of[p]
                pb, l, m = probs[p]
                if first_tile:
                    o2 = _nn(pb, vc_ref[0:n_cur, lanes]) / l
                else:
                    o2 = (_nn(pb[:, 0:n_prev], vp_ref[r0:TQ, lanes]) + _nn(pb[:, n_prev:KW], vc_ref[0:n_cur, lanes])) / l
                lse2 = m + jnp.log(l)
                lse_ref[r0:r0 + QB, 2 * p:2 * p + 1] = lse2[0:QB, :]
                lse_ref[r0:r0 + QB, 2 * p + 1:2 * p + 2] = lse2[QB:2 * QB, :]
                o_ref[r0:r0 + QB, lanes] = jnp.where(lane_hi, o2[QB:2 * QB, :], o2[0:QB, :]).astype(BF16)

        @pl.when(i == 0)
        def _():
            for b in range(TQ // QB):
                block(b, True)

        @pl.when(i > 0)
        def _():
            for b in range(TQ // QB):
                block(b, False)

    blk = lambda grp, prev: pl.BlockSpec(
        (None, TQ, D_ATT), (lambda i: (grp, jnp.maximum(i - 1, 0), 0)) if prev else (lambda i: (grp, i, 0)))
    vmem = pl.BlockSpec(memory_space=pltpu.VMEM)
    return pl.pallas_call(
        body, name="attn_fwd", grid=(n,),
        in_specs=[pl.BlockSpec((HEADS, DIAG), lambda i: (0, 0)),
                  blk(0, False), blk(1, True), blk(1, False), blk(2, True), blk(2, False)] + [vmem] * (nw + 1),
        out_specs=[pl.BlockSpec((TQ, D_ATT), lambda i: (i, 0)), pl.BlockSpec((TQ, HEADS), lambda i: (i, 0))]
        + [ANY] * (nw + 1),
        out_shape=[jax.ShapeDtypeStruct((s, D_ATT), BF16), jax.ShapeDtypeStruct((s, HEADS), F32)]
        + _gather_out_shapes(shards, kinds, cw3),
        scratch_shapes=[pltpu.VMEM((HEADS // 2, 2 * QB, KW), F32)] + [pltpu.VMEM(a.shape, BF16) for a in shards]
        + _gather_sems(nw),
        compiler_params=_params("arbitrary"),
    )(diag, proj, proj, proj, proj, proj, *shards, cw3)


def _attn_bwd(diag, proj, d_att, att, lse, parts):
    s = proj.shape[1]
    n = s // TQ
    npart = len(parts)
    scale = HEAD_DIM ** -0.5
    rel_pad = 3 * LANES

    def body(*refs):
        diag_ref, q_ref, kp_ref, kc_ref, vp_ref, vc_ref, do_ref, o_ref, lse_ref = refs[:9]
        part_refs = refs[9:9 + npart]
        dqkv_ref, dbias_ref = refs[9 + npart:11 + npart]
        slot_refs = refs[11 + npart:11 + 2 * npart]
        bias_scr, dbias_acc, dk_acc, dv_acc, dq_scr = refs[11 + 2 * npart:16 + 2 * npart]
        start, finish = _scatter_plan(part_refs, slot_refs, *refs[16 + 2 * npart:])
        i = pl.program_id(0)
        cur, prv = i % 2, 1 - i % 2

        @pl.when(i == 0)
        def _():
            start()
            _build_bias(diag_ref, bias_scr)
            dbias_acc[...] = jnp.zeros_like(dbias_acc)
            dk_acc[...] = jnp.zeros_like(dk_acc)
            dv_acc[...] = jnp.zeros_like(dv_acc)

        @pl.when(i > 0)
        def _():
            dqkv_ref[:, 0:D_ATT] = dq_scr[...]
            dk_acc[cur] = jnp.zeros((TQ, D_ATT), F32)
            dv_acc[cur] = jnp.zeros((TQ, D_ATT), F32)

        lane_hi = lax.broadcasted_iota(jnp.int32, (QB, LANES), 1) >= HEAD_DIM
        col = lax.broadcasted_iota(jnp.int32, (2 * QB, KW), 1)

        def make_block(first_tile):
            def block(b):
                r0, n_prev = b * QB, TQ - b * QB
                n_cur = KW - n_prev
                for p in range(HEADS // 2):
                    lanes = slice(LANES * p, LANES * (p + 1))
                    q2 = _stack_heads(q_ref[r0:r0 + QB, lanes] * scale, lane_hi)
                    kw = jnp.concatenate([kp_ref[r0:TQ, lanes], kc_ref[0:n_cur, lanes]], axis=0)
                    vw = jnp.concatenate([vp_ref[r0:TQ, lanes], vc_ref[0:n_cur, lanes]], axis=0)
                    dop = do_ref[r0:r0 + QB, lanes]
                    do2 = _stack_heads(dop, lane_hi)
                    prod = dop.astype(F32) * o_ref[r0:r0 + QB, lanes].astype(F32)
                    delta2 = jnp.concatenate(
                        [jnp.sum(jnp.where(lane_hi, 0.0, prod), axis=1, keepdims=True),
                         jnp.sum(jnp.where(lane_hi, prod, 0.0), axis=1, keepdims=True)], axis=0)
                    lse2 = jnp.concatenate([lse_ref[r0:r0 + QB, 2 * p:2 * p + 1],
                                            lse_ref[r0:r0 + QB, 2 * p + 1:2 * p + 2]], axis=0)
                    sc = _nt(q2, kw) + bias_scr[p]
                    if first_tile:
                        sc = jnp.where(col >= TQ - r0, sc, NEG_BIG)
                    pr = jnp.exp(sc - lse2)
                    ds = pr * (_nt(do2, vw) - delta2)
                    dbias_acc[p] += ds
                    dsb = ds.astype(BF16)
                    dv_w = _tn(pr.astype(BF16), do2)
                    dk_w = _tn(dsb, q2)
                    dv_acc[prv, r0:TQ, lanes] += dv_w[0:n_prev, :]
                    dv_acc[cur, 0:n_cur, lanes] += dv_w[n_prev:KW, :]
                    dk_acc[prv, r0:TQ, lanes] += dk_w[0:n_prev, :]
                    dk_acc[cur, 0:n_cur, lanes] += dk_w[n_prev:KW, :]
                    dq2 = _nn(dsb, kw)
                    dq = jnp.where(lane_hi, dq2[QB:2 * QB, :], dq2[0:QB, :]) * scale
                    dq_scr[r0:r0 + QB, lanes] = dq.astype(BF16)
            return block

        @pl.when(i == 0)
        def _():
            for b in range(TQ // QB):
                make_block(True)(b)

        @pl.when((i > 0) & (i < n))
        def _():
            for b in range(TQ // QB):
                make_block(False)(b)

        @pl.when(i > 0)
        def _():
            dqkv_ref[:, D_ATT:2 * D_ATT] = dk_acc[prv].astype(BF16)
            dqkv_ref[:, 2 * D_ATT:3 * D_ATT] = dv_acc[prv].astype(BF16)

        @pl.when(i == n)
        def _():
            d_iota = lax.broadcasted_iota(jnp.int32, (DIAG, rel_pad), 0)
            n_iota = lax.broadcasted_iota(jnp.int32, (DIAG, rel_pad), 1)
            diff = jnp.where(d_iota < KW, d_iota, d_iota - DIAG)
            idx = jnp.clip(N_LEFT * CHUNK - diff, -MAX_REL, MAX_REL) + MAX_REL
            onehot = (idx == n_iota).astype(F32)
            rows = []
            for hd in range(HEADS):
                acc = dbias_acc[hd // 2, (hd % 2) * QB:(hd % 2 + 1) * QB, :]
                a = jnp.concatenate([acc, jnp.zeros((QB, DIAG - KW), F32)], axis=1)
                g8 = a[0:SUBLANES, :]
                for blk in range(1, QB // SUBLANES):
                    g8 = g8 + pltpu.roll(a[blk * SUBLANES:(blk + 1) * SUBLANES, :], DIAG - blk * SUBLANES, 1)
                g1 = g8[0:1, :]
                for r in range(1, SUBLANES):
                    g1 = g1 + pltpu.roll(g8[r:r + 1, :], DIAG - r, 1)
                rows.append(g1)
            g = jnp.concatenate(rows, axis=0)
            dbias_ref[...] = jnp.dot(g, onehot, preferred_element_type=F32, precision=lax.Precision.HIGHEST)
            finish()

    last = n - 1
    cur = lambda grp: pl.BlockSpec((None, TQ, D_ATT), lambda i: (grp, jnp.minimum(i, last), 0))
    prev = lambda grp: pl.BlockSpec((None, TQ, D_ATT), lambda i: (grp, jnp.maximum(jnp.minimum(i, last) - 1, 0), 0))
    tile = pl.BlockSpec((TQ, D_ATT), lambda i: (jnp.minimum(i, last), 0))
    return pl.pallas_call(
        body, name="attn_bwd", grid=(n + 1,),
        in_specs=[pl.BlockSpec((HEADS, DIAG), lambda i: (0, 0)),
                  cur(0), prev(1), cur(1), prev(2), cur(2), tile, tile,
                  pl.BlockSpec((TQ, HEADS), lambda i: (jnp.minimum(i, last), 0))] + [ANY] * npart,
        out_specs=[pl.BlockSpec((TQ, 3 * D_ATT), lambda i: (jnp.maximum(i - 1, 0), 0)),
                   pl.BlockSpec((HEADS, rel_pad), lambda i: (0, 0))] + [ANY] * npart,
        out_shape=[jax.ShapeDtypeStruct((s, 3 * D_ATT), BF16), jax.ShapeDtypeStruct((HEADS, rel_pad), F32)]
        + _scatter_out_shapes(parts),
        scratch_shapes=[pltpu.VMEM((HEADS // 2, 2 * QB, KW), F32), pltpu.VMEM((HEADS // 2, 2 * QB, KW), F32),
                        pltpu.VMEM((2, TQ, D_ATT), F32), pltpu.VMEM((2, TQ, D_ATT), F32),
                        pltpu.VMEM((TQ, D_ATT), BF16)] + _scatter_sems(npart),
        compiler_params=_params("arbitrary"),
    )(diag, proj, proj, proj, proj, proj, d_att, att, lse, *parts)


def _shift_down(a, k, halo):
    rolled = pltpu.roll(a, k, 0)
    row = lax.broadcasted_iota(jnp.int32, halo.shape, 0)
    first = jnp.where(row < k, pltpu.roll(halo, k, 0), rolled[0:SUBLANES, :])
    return jnp.concatenate([first, rolled[SUBLANES:, :]], axis=0)


def _shift_up(a, k, nxt):
    tm = a.shape[0]
    rolled = pltpu.roll(a, tm - k, 0)
    row = lax.broadcasted_iota(jnp.int32, nxt.shape, 0)
    last = jnp.where(row >= SUBLANES - k, pltpu.roll(nxt, SUBLANES - k, 0), rolled[tm - SUBLANES:, :])
    return jnp.concatenate([rolled[:tm - SUBLANES, :], last], axis=0)


def _sigmoid(v):
    return 0.5 * jnp.tanh(0.5 * v) + 0.5


def _mixer_mid(att, proj, x, tgt, w_att, w_conv, w_out, conv_w, conv_b, fin_g):
    s, d = x.shape
    dc = D_ATT
    n = s // TM_MID
    tm = TM_MID
    n_shards = 4

    def body(att_ref, za_ref, gb_ref, gc_ref, u_ref, zc_ref, hgc_ref, hu_ref, gatt_ref, gconv_ref, x_ref, t_ref,
             watt_ref, wconv_ref, wout_ref, cw_ref, cb_ref, fg_ref,
             dpb_ref, do_ref, dx2_ref, gatt_o, gconv_o, gout_o, loss_o, gfn_o, gcb_o, gcw_o,
             acc_att, acc_conv, acc_out, carry, watt_t_ref, wconv_t_ref, wout_t_ref):
        i = pl.program_id(0)
        tile = n - 1 - i

        @pl.when(i == 0)
        def _():
            acc_att[...] = jnp.zeros_like(acc_att)
            acc_conv[...] = jnp.zeros_like(acc_conv)
            acc_out[...] = jnp.zeros_like(acc_out)
            carry[...] = jnp.zeros_like(carry)
            loss_o[...] = jnp.zeros_like(loss_o)
            gfn_o[...] = jnp.zeros_like(gfn_o)
            gcb_o[...] = jnp.zeros_like(gcb_o)
            gcw_o[...] = jnp.zeros_like(gcw_o)
            watt_t_ref[...] = watt_ref[...].T
            wconv_t_ref[...] = wconv_ref[...].T
            wout_t_ref[...] = wout_ref[...].T

        halves = [slice(hh * (tm // 2), (hh + 1) * (tm // 2)) for hh in range(2)]
        both = lambda fn: [fn(rows) for rows in halves]
        f32 = lambda ref, rows: ref[rows, :].astype(F32)

        gc = gc_ref[...].astype(F32)
        u = u_ref[...].astype(F32)
        cu = gc * u
        halo = jnp.where(tile > 0, hgc_ref[...].astype(F32) * hu_ref[...].astype(F32), 0.0)
        cu1 = _shift_down(cu, 1, halo)
        cu2 = _shift_down(cu, 2, halo)
        w0, w1, w2 = cw_ref[0:1, :], cw_ref[1:2, :], cw_ref[2:3, :]
        fg = fg_ref[...]

        def stage_a(rows):
            att_v, za, zc, gb = f32(att_ref, rows), f32(za_ref, rows), f32(zc_ref, rows), f32(gb_ref, rows)
            sa = _sigmoid(za)
            silu_a = za * sa
            vconv = w0 * cu2[rows, :] + w1 * cu1[rows, :] + w2 * cu[rows, :] + cb_ref[...]
            sc = _sigmoid(zc)
            silu_c = zc * sc
            return dict(att_v=att_v, za=za, zc=zc, gb=gb, sa=sa, silu_a=silu_a, vconv=vconv, sc=sc, silu_c=silu_c,
                        a_b=(att_v * silu_a).astype(BF16), c_b=(gb * vconv * silu_c).astype(BF16))

        st = both(stage_a)
        for t in st:
            t["y_att"] = _nn(t["a_b"], watt_ref[...])
            t["y_conv"] = _nn(t["c_b"], wconv_ref[...])
        for t, rows in zip(st, halves):
            gpair = lambda ref: jnp.concatenate([ref[0, rows, :], ref[1, rows, :]], axis=1).astype(F32)
            t["ga"] = _sigmoid(gpair(gatt_ref))
            t["gv"] = _sigmoid(gpair(gconv_ref))
            t["m_b"] = (t["ga"] * t["y_att"] + t["gv"] * t["y_conv"]).astype(BF16)
        for t in st:
            t["mo"] = _nn(t["m_b"], wout_ref[...])
        for t, rows in zip(st, halves):
            x2 = x_ref[rows, :] + t["mo"]
            r2 = lax.rsqrt(jnp.mean(x2 * x2, axis=-1, keepdims=True) + EPS)
            x2n = x2 * r2
            err = x2n * fg - t_ref[rows, :]
            loss_o[...] += jnp.sum(err * err, axis=0, keepdims=True) * (0.5 / d)
            dy = err * (1.0 / d)
            gfn_o[...] += jnp.sum(dy * x2n, axis=0, keepdims=True)
            dyn = dy * fg
            dx2 = r2 * (dyn - x2n * jnp.mean(dyn * x2n, axis=-1, keepdims=True))
            dx2_ref[rows, :] = dx2
            t["dx2_b"] = dx2.astype(BF16)
        for t in st:
            t["dm"] = _nn(t["dx2_b"], wout_t_ref[...])
        whole = lambda key: jnp.concatenate([st[0][key], st[1][key]], axis=0)
        acc_out[...] += _tn(whole("m_b"), whole("dx2_b"))
        for t, rows in zip(st, halves):
            dy_att = t["dm"] * t["ga"]
            dy_conv = t["dm"] * t["gv"]
            dpb_ref[rows, 5 * dc:5 * dc + d] = (dy_att * t["y_att"] * (1.0 - t["ga"])).astype(BF16)
            dpb_ref[rows, 5 * dc + d:5 * dc + 2 * d] = (dy_conv * t["y_conv"] * (1.0 - t["gv"])).astype(BF16)
            t["dya_b"] = dy_att.astype(BF16)
            t["dyc_b"] = dy_conv.astype(BF16)
        for t in st:
            t["da_in"] = _nn(t["dya_b"], watt_t_ref[...])
            t["dc_in"] = _nn(t["dyc_b"], wconv_t_ref[...])
        acc_att[...] += _tn(whole("a_b"), whole("dya_b"))
        acc_conv[...] += _tn(whole("c_b"), whole("dyc_b"))
        for t, rows in zip(st, halves):
            sa, za, sc, zc = t["sa"], t["za"], t["sc"], t["zc"]
            do_ref[rows, :] = (t["da_in"] * t["silu_a"]).astype(BF16)
            dpb_ref[rows, 0:dc] = (t["da_in"] * t["att_v"] * (sa * (1.0 + za * (1.0 - sa)))).astype(BF16)
            dpb_ref[rows, dc:2 * dc] = (t["dc_in"] * t["vconv"] * t["silu_c"]).astype(BF16)
            dgs = t["dc_in"] * t["gb"]
            t["dvc"] = dgs * t["silu_c"]
            dpb_ref[rows, 4 * dc:5 * dc] = (dgs * t["vconv"] * (sc * (1.0 + zc * (1.0 - sc)))).astype(BF16)
        dvc = whole("dvc")
        gcb_o[...] += jnp.sum(dvc, axis=0, keepdims=True)
        gcw_o[0:1, :] += jnp.sum(dvc * cu2, axis=0, keepdims=True)
        gcw_o[1:2, :] += jnp.sum(dvc * cu1, axis=0, keepdims=True)
        gcw_o[2:3, :] += jnp.sum(dvc * cu, axis=0, keepdims=True)
        nxt = carry[...]
        dcu = w2 * dvc + w1 * _shift_up(dvc, 1, nxt) + w0 * _shift_up(dvc, 2, nxt)
        carry[...] = dvc[0:SUBLANES, :]
        dpb_ref[:, 2 * dc:3 * dc] = (dcu * u).astype(BF16)
        dpb_ref[:, 3 * dc:4 * dc] = (dcu * gc).astype(BF16)

        @pl.when(i == n - 1)
        def _():
            for j in range(n_shards):
                gatt_o[j] = acc_att[:, j * (d // n_shards):(j + 1) * (d // n_shards)].astype(BF16)
                gconv_o[j] = acc_conv[:, j * (d // n_shards):(j + 1) * (d // n_shards)].astype(BF16)
                gout_o[j] = acc_out[j * (d // n_shards):(j + 1) * (d // n_shards), :].astype(BF16)

    rev = lambda width, col_blk: pl.BlockSpec((tm, width), lambda i: (n - 1 - i, col_blk))
    grp = lambda g: pl.BlockSpec((None, tm, dc), lambda i: (g, n - 1 - i, 0))
    grp2 = lambda g2: pl.BlockSpec((2, tm, dc), lambda i: (g2, n - 1 - i, 0))
    halo_spec = lambda g: pl.BlockSpec(
        (None, SUBLANES, dc), lambda i: (g, jnp.maximum((n - 1 - i) * (tm // SUBLANES) - 1, 0), 0))
    const = lambda shape: pl.BlockSpec(shape, lambda i: tuple(0 for _ in shape), pipeline_mode=pl.Buffered(1))
    q4 = d // n_shards
    return pl.pallas_call(
        body, name="mixer_mid", grid=(n,),
        in_specs=[rev(dc, 0), grp(3), grp(4), grp(5), grp(6), grp(7),
                  halo_spec(5), halo_spec(6), grp2(4), grp2(5), rev(d, 0), rev(d, 0),
                  const((dc, d)), const((dc, d)), const((d, d)),
                  const(conv_w.shape), const((1, dc)), const((1, d))],
        out_specs=[rev(5 * dc + 2 * d, 0), rev(dc, 0), rev(d, 0),
                   const((n_shards, dc, q4)), const((n_shards, dc, q4)), const((n_shards, q4, d)),
                   const((1, d)), const((1, d)), const((1, dc)), const((SUBLANES, dc))],
        out_shape=[jax.ShapeDtypeStruct((s, 5 * dc + 2 * d), BF16), jax.ShapeDtypeStruct((s, dc), BF16),
                   jax.ShapeDtypeStruct((s, d), F32),
                   jax.ShapeDtypeStruct((n_shards, dc, q4), BF16), jax.ShapeDtypeStruct((n_shards, dc, q4), BF16),
                   jax.ShapeDtypeStruct((n_shards, q4, d), BF16),
                   jax.ShapeDtypeStruct((1, d), F32), jax.ShapeDtypeStruct((1, d), F32),
                   jax.ShapeDtypeStruct((1, dc), F32), jax.ShapeDtypeStruct((SUBLANES, dc), F32)],
        scratch_shapes=[pltpu.VMEM((dc, d), F32), pltpu.VMEM((dc, d), F32), pltpu.VMEM((d, d), F32),
                        pltpu.VMEM((SUBLANES, dc), F32),
                        pltpu.VMEM((d, dc), BF16), pltpu.VMEM((d, dc), BF16), pltpu.VMEM((d, d), BF16)],
        compiler_params=_params("arbitrary"),
    )(att, proj, proj, proj, proj, proj, proj, proj, proj, proj, x, tgt,
      w_att, w_conv, w_out, conv_w, conv_b, fin_g)


def _in_proj_bwd_x(dqkv, dpb, w_in, x, dx2, g):
    s, d = x.shape
    tn = dqkv.shape[1]
    nb = dpb.shape[1] // tn
    n = s // TM_MM

    def body(*refs):
        dps, ws = refs[:nb + 1], refs[nb + 1:2 * nb + 2]
        x_ref, dx2_ref, g_ref, gx_ref, gng_ref = refs[2 * nb + 2:]
        i = pl.program_id(0)

        @pl.when(i == 0)
        def _():
            gng_ref[...] = jnp.zeros_like(gng_ref)

        dh = _nt(dps[0][...], ws[0][0])
        for j in range(1, nb + 1):
            dh = dh + _nt(dps[j][...], ws[j][0])
        xv = x_ref[...]
        r = lax.rsqrt(jnp.mean(xv * xv, axis=-1, keepdims=True) + EPS)
        xn = xv * r
        gng_ref[...] += jnp.sum(dh * xn, axis=0, keepdims=True)
        dhn = dh * g_ref[...]
        gx_ref[...] = dx2_ref[...] + r * (dhn - xn * jnp.mean(dhn * xn, axis=-1, keepdims=True))

    tile = lambda width, col_blk: pl.BlockSpec((TM_MM, width), lambda i: (i, col_blk))
    wspec = lambda blk: pl.BlockSpec((1, d, tn), lambda i: (blk, 0, 0), pipeline_mode=pl.Buffered(1))
    return pl.pallas_call(
        body, name="in_proj_bwd_x", grid=(n,),
        in_specs=[tile(tn, 0)] + [tile(tn, j) for j in range(nb)] + [wspec(j) for j in range(nb + 1)]
        + [tile(d, 0), tile(d, 0), pl.BlockSpec((1, d), lambda i: (0, 0))],
        out_specs=[tile(d, 0), pl.BlockSpec((1, d), lambda i: (0, 0))],
        out_shape=[jax.ShapeDtypeStruct((s, d), F32), jax.ShapeDtypeStruct((1, d), F32)],
        compiler_params=_params("arbitrary"),
    )(dqkv, *([dpb] * nb), *([w_in] * (nb + 1)), x, dx2, g)


def _in_proj_bwd_w(h, dqkv, dpb, order, slots, small):
    s, d = h.shape
    tn = dqkv.shape[1]
    n = s // TM_BLK
    hr = d // 2
    settle = min(1, n - 1)
    nw, ns = len(slots), len(small)
    half_rows = [sl.shape[1] for sl in slots] + [hr]

    def body(order_ref, h_ref, da_ref, db_ref, *refs):
        srcs, parts = refs[:nw], refs[nw:nw + ns]
        slots_ref, shards, sm_all = refs[nw + ns], refs[nw + ns + 1:2 * nw + ns + 2], refs[2 * nw + ns + 2]
        acc, sendbuf, pairbuf, chipbuf = refs[2 * nw + ns + 3:2 * nw + ns + 7]
        halves, sm = refs[2 * nw + ns + 7:3 * nw + ns + 7], refs[3 * nw + ns + 7]
        psend, precv, send, recv, lsem, hsend, hrecv, ssend, srecv, hsem = refs[3 * nw + ns + 8:]
        j, i = pl.program_id(0), pl.program_id(1)
        blk = order_ref[j]
        pos = _position()
        x, y, c = pos
        me = 4 * x + 2 * y + c

        def half_sum(w):
            return halves[w] if w < nw else acc.at[pl.ds(0, hr), :]

        def half_of_shard(w, half):
            return shards[w].at[pl.ds(half * half_rows[w], half_rows[w]), :]

        def exchange(w, half):
            return pltpu.make_async_remote_copy(
                src_ref=half_sum(w), dst_ref=half_of_shard(w, half), send_sem=hsend.at[w], recv_sem=hrecv.at[w],
                device_id=(x, y, 1 - c), device_id_type=MESH)

        def keep(w):
            return pltpu.make_async_copy(half_sum(w), half_of_shard(w, c), hsem.at[w])

        def bcast(k, slot):
            return pltpu.make_async_remote_copy(
                src_ref=sm, dst_ref=sm_all.at[slot], send_sem=ssend.at[k], recv_sem=srecv.at[k],
                device_id=_peer(pos, k), device_id_type=MESH)

        own_small = pltpu.make_async_copy(sm, sm_all.at[me], hsem.at[nw + 1])

        @pl.when((j == 0) & (i == 0))
        def _():
            _pack_small(sm, *parts)
            own_small.start()
            for k in range(1, N_DEV):
                bcast(k, me).start()
            for w in range(nw):
                total = srcs[w][0].astype(F32)
                for k in range(1, srcs[w].shape[0]):
                    total = total + srcs[w][k].astype(F32)
                halves[w][...] = total
                exchange(w, c).start()
                keep(w).start()

        @pl.when(i == 0)
        def _():
            acc[...] = jnp.zeros_like(acc)

        @pl.when(blk == 0)
        def _():
            acc[...] += _tn(h_ref[...], da_ref[...])

        @pl.when(blk > 0)
        def _():
            acc[...] += _tn(h_ref[...], db_ref[...])

        def pair(step, half):
            return pltpu.make_async_remote_copy(
                src_ref=sendbuf.at[step, pl.ds(half * hr, hr), :], dst_ref=pairbuf.at[step],
                send_sem=psend.at[step], recv_sem=precv.at[step], device_id=(x, y, 1 - c), device_id_type=MESH)

        def ici(step):
            flip = OWNER_FLIPS[step]
            return pltpu.make_async_remote_copy(
                src_ref=chipbuf.at[step], dst_ref=slots_ref.at[flip], send_sem=send.at[step], recv_sem=recv.at[step],
                device_id=_peer(pos, 4 * (flip >> 1) + 2 * (flip & 1)), device_id_type=MESH)

        local = pltpu.make_async_copy(chipbuf.at[N_CHIPS - 1], slots_ref.at[0], lsem.at[0])

        def combine(step):
            pair(step, c).wait_recv()
            mine = sendbuf[step, pl.ds(c * hr, hr), :].astype(F32)
            chipbuf[step] = (mine + pairbuf[step].astype(F32)).astype(BF16)

        for step in range(N_CHIPS):
            @pl.when((j == step) & (i == n - 1))
            def _(step=step):
                sendbuf[step] = acc[...].astype(BF16)
                pair(step, 1 - c).start()

        for step in range(N_CHIPS - 1):
            @pl.when((j == step + 1) & (i == settle))
            def _(step=step):
                combine(step)
                ici(step).start()

        @pl.when((j == N_CHIPS - 1) & (i == n - 1))
        def _():
            combine(N_CHIPS - 1)
            local.start()
            staged = [pltpu.make_async_copy(slots_ref.at[f], pairbuf.at[f - 1], hsem.at[nw + 1 + f])
                      for f in range(1, N_CHIPS)]
            for step in range(N_CHIPS - 1):
                ici(step).wait_recv()
            for cp in staged:
                cp.start()
            for cp in staged:
                cp.wait()
            total = chipbuf[N_CHIPS - 1].astype(F32)
            for f in range(1, N_CHIPS):
                total = total + pairbuf[f - 1].astype(F32)
            acc[0:hr, :] = total
            exchange(nw, c).start()
            keep(nw).start()
            for w in range(nw + 1):
                exchange(w, 1 - c).wait_recv()
            for k in range(1, N_DEV):
                px, py, pc = _peer(pos, k)
                bcast(k, 4 * px + 2 * py + pc).wait_recv()
            for step in range(N_CHIPS - 1):
                ici(step).wait_send()
            for step in range(N_CHIPS):
                pair(step, 1 - c).wait_send()
            local.wait()
            for w in range(nw + 1):
                exchange(w, c).wait_send()
                keep(w).wait()
            for k in range(1, N_DEV):
                bcast(k, me).wait_send()
            own_small.wait()

    vmem = pl.BlockSpec(memory_space=pltpu.VMEM)
    shard_shapes = [(2 * sl.shape[1], sl.shape[2]) for sl in slots]
    out = pl.pallas_call(
        body, name="in_proj_bwd_w",
        grid_spec=pltpu.PrefetchScalarGridSpec(
            num_scalar_prefetch=1, grid=(N_CHIPS, n),
            in_specs=[pl.BlockSpec((TM_BLK, d), lambda j, i, order: (i, 0)),
                      pl.BlockSpec((TM_BLK, tn), lambda j, i, order: (jnp.where(order[j] == 0, i, 0), 0)),
                      pl.BlockSpec((TM_BLK, tn), lambda j, i, order: (jnp.where(order[j] == 0, 0, i),
                                                                     jnp.maximum(order[j] - 1, 0)))]
            + [vmem] * (nw + ns),
            out_specs=[ANY] * (nw + 3),
            scratch_shapes=[pltpu.VMEM((d, tn), F32), pltpu.VMEM((N_CHIPS, d, tn), BF16),
                            pltpu.VMEM((N_CHIPS, hr, tn), BF16), pltpu.VMEM((N_CHIPS, hr, tn), BF16)]
            + [pltpu.VMEM((r // 2, cc), F32) for r, cc in shard_shapes] + [pltpu.VMEM((SMALL_ROWS, SMALL_COLS), F32)]
            + [pltpu.SemaphoreType.DMA((N_CHIPS,)), pltpu.SemaphoreType.DMA((N_CHIPS,)),
               pltpu.SemaphoreType.DMA((N_CHIPS - 1,)), pltpu.SemaphoreType.DMA((N_CHIPS - 1,)),
               pltpu.SemaphoreType.DMA((1,)),
               pltpu.SemaphoreType.DMA((nw + 1,)), pltpu.SemaphoreType.DMA((nw + 1,)),
               pltpu.SemaphoreType.DMA((N_DEV,)), pltpu.SemaphoreType.DMA((N_DEV,)),
               pltpu.SemaphoreType.DMA((nw + 1 + N_CHIPS,))]),
        out_shape=[jax.ShapeDtypeStruct((N_CHIPS, hr, tn), BF16)]
        + [jax.ShapeDtypeStruct(sh, F32) for sh in shard_shapes + [(d, tn)]]
        + [jax.ShapeDtypeStruct((N_DEV, SMALL_ROWS, SMALL_COLS), F32)],
        compiler_params=_params("arbitrary", "arbitrary"),
    )(order, h, dqkv, dpb, *slots, *small)
    return [out[nw + 1]] + list(out[1:nw + 1]) + [out[nw + 2]]


LOSS_ROW = 6


def _adam_update(w, g, m, v):
    c1 = 1.0 / (1.0 - ADAM_B1 ** ADAM_STEP)
    c2 = 1.0 / (1.0 - ADAM_B2 ** ADAM_STEP)
    m2 = ADAM_B1 * m + (1.0 - ADAM_B1) * g
    v2 = ADAM_B2 * v + (1.0 - ADAM_B2) * (g * g)
    return -ADAM_LR * ((m2 * c1) / (jnp.sqrt(v2 * c2) + ADAM_EPS) + ADAM_WD * w), m2, v2


def _adamw_small(recv, params, moments_m, moments_v, out_shapes):
    k = recv.shape[0]
    n_par = len(params)
    cshard = params[3].shape[1]

    def body(*refs):
        r_ref = refs[0]
        ws, ms, vs = refs[1:1 + n_par], refs[1 + n_par:1 + 2 * n_par], refs[1 + 2 * n_par:1 + 3 * n_par]
        loss_ref = refs[1 + 3 * n_par]
        outs = refs[2 + 3 * n_par:]
        total = r_ref[0]
        for slot in range(1, k):
            total = total + r_ref[slot]
        loss_ref[...] = jnp.sum(total[LOSS_ROW:LOSS_ROW + 1, :], axis=1, keepdims=True)
        chip = 2 * lax.axis_index("x") + lax.axis_index("y")
        g_cw = jnp.zeros((3, cshard), F32)
        for sh in range(N_CHIPS):
            g_cw = g_cw + jnp.where(chip == sh, total[3:6, sh * cshard:(sh + 1) * cshard], 0.0)
        grads = [total[0:1, :], total[1:2, :], total[2:3, :ws[2].shape[1]], g_cw, total[8:16, :ws[4].shape[1]]]
        for p in range(n_par):
            delta, m2, v2 = _adam_update(ws[p][...], grads[p], ms[p][...], vs[p][...])
            for q, val in enumerate((grads[p], delta, m2, v2)):
                out = outs[4 * p + q]
                out[...] = val[0] if len(out.shape) == 1 else val

    shapes = [jax.ShapeDtypeStruct((1, 1), F32)]
    for shape in out_shapes:
        shapes += [jax.ShapeDtypeStruct(shape, F32)] * 4
    return pl.pallas_call(body, name="adamw_small", out_shape=shapes)(recv, *params, *moments_m, *moments_v)


def _adamw_group(ws, gs, ms, vs, name, steps):
    k = len(ws)

    def body(*refs):
        ins, outs = refs[:4 * k], refs[4 * k:]
        for j in range(k):
            gv = ins[k + j][...]
            outs[4 * j][...] = gv
            outs[4 * j + 1][...], outs[4 * j + 2][...], outs[4 * j + 3][...] = _adam_update(
                ins[j][...], gv, ins[2 * k + j][...], ins[3 * k + j][...])

    specs = [pl.BlockSpec((w.shape[0] // steps, w.shape[1]), lambda i: (i, 0)) for w in ws]
    out = pl.pallas_call(
        body, name=name, grid=(steps,),
        in_specs=specs * 4, out_specs=[spec for spec in specs for _ in range(4)],
        out_shape=[jax.ShapeDtypeStruct(w.shape, F32) for w in ws for _ in range(4)],
        compiler_params=_params("parallel"),
    )(*ws, *gs, *ms, *vs)
    return [tuple(out[4 * j:4 * j + 4]) for j in range(k)]


ANY = pl.BlockSpec(memory_space=pl.ANY)
N_CHIPS = 4
N_DEV = 8
OWNER_FLIPS = (3, 1, 2, 0)


def _position():
    return lax.axis_index("x"), lax.axis_index("y"), lax.axis_index("c")


def _gather_out_shapes(shards, kinds, cw3):
    full = [(a.shape[0], a.shape[1] * N_CHIPS) if k == "cols" else (a.shape[0] * N_CHIPS, a.shape[1])
            for a, k in zip(shards, kinds)]
    return [jax.ShapeDtypeStruct(f, BF16) for f in full] + [
        jax.ShapeDtypeStruct((N_CHIPS,) + cw3.shape, cw3.dtype)]


def _gather_sems(nw):
    return [pltpu.SemaphoreType.DMA((3, nw)), pltpu.SemaphoreType.DMA((3, nw)),
            pltpu.SemaphoreType.DMA((3, nw)), pltpu.SemaphoreType.DMA((3, nw)),
            pltpu.SemaphoreType.DMA((3,)), pltpu.SemaphoreType.DMA((3,)), pltpu.SemaphoreType.DMA((nw + 1,))]


def _gather_plan(kinds, srcs, dsts, cw, cw_all, send1, recv1, send2, recv2, ssend, srecv, lsem):
    nw = len(srcs)
    x, y, c = _position()
    mine = 2 * x + y
    chips = [(x, 1 - y), (1 - x, y), (1 - x, 1 - y)]

    def window(w, shard, half):
        r, cc = srcs[w].shape
        hr = r // 2
        if kinds[w] == "cols":
            rows = pl.ds(0, r) if half is None else pl.ds(half * hr, hr)
            return dsts[w].at[rows, pl.ds(shard * cc, cc)]
        rows = pl.ds(shard * r, r) if half is None else pl.ds(shard * r + half * hr, hr)
        return dsts[w].at[rows, :]

    def my_half(w):
        hr = srcs[w].shape[0] // 2
        return srcs[w].at[pl.ds(c * hr, hr), :]

    def local():
        return [pltpu.make_async_copy(srcs[w], window(w, mine, None), lsem.at[w]) for w in range(nw)] + [
            pltpu.make_async_copy(cw, cw_all.at[mine], lsem.at[nw])]

    def ici(k, w, shard):
        kx, ky = chips[k]
        return pltpu.make_async_remote_copy(
            src_ref=my_half(w), dst_ref=window(w, shard, c), send_sem=send1.at[k, w], recv_sem=recv1.at[k, w],
            device_id=(kx, ky, c), device_id_type=MESH)

    def d2d(k, w, shard, half):
        return pltpu.make_async_remote_copy(
            src_ref=window(w, shard, half), dst_ref=window(w, shard, half),
            send_sem=send2.at[k, w], recv_sem=recv2.at[k, w], device_id=(x, y, 1 - c), device_id_type=MESH)

    def small(k, shard):
        kx, ky = chips[k]
        return pltpu.make_async_remote_copy(
            src_ref=cw, dst_ref=cw_all.at[shard], send_sem=ssend.at[k], recv_sem=srecv.at[k],
            device_id=(kx, ky, c), device_id_type=MESH)

    def theirs(k):
        kx, ky = chips[k]
        return 2 * kx + ky

    def start():
        for cp in local():
            cp.start()
        for k in range(3):
            for w in range(nw):
                ici(k, w, mine).start()
            small(k, mine).start()

    def forward():
        for k in range(3):
            for w in range(nw):
                ici(k, w, theirs(k)).wait_recv()
                d2d(k, w, theirs(k), c).start()

    def finish():
        for k in range(3):
            for w in range(nw):
                d2d(k, w, theirs(k), 1 - c).wait_recv()
            small(k, theirs(k)).wait_recv()
        for k in range(3):
            for w in range(nw):
                ici(k, w, mine).wait_send()
                d2d(k, w, theirs(k), c).wait_send()
            small(k, mine).wait_send()
        for cp in local():
            cp.wait()

    return start, forward, finish


def _scatter_out_shapes(parts):
    return [jax.ShapeDtypeStruct((N_DEV, p.shape[1] // 2, p.shape[2]), p.dtype) for p in parts]


def _scatter_sems(nw):
    return [pltpu.SemaphoreType.DMA((N_DEV, nw)), pltpu.SemaphoreType.DMA((N_DEV, nw)), pltpu.SemaphoreType.DMA((nw,))]


def _peer(pos, k):
    x, y, c = pos
    return ((1 - x) if k & 4 else x, (1 - y) if k & 2 else y, (1 - c) if k & 1 else c)


def _scatter_plan(srcs, dsts, send, recv, lsem):
    nw = len(srcs)
    pos = _position()

    def piece(w, k):
        px, py, pc = _peer(pos, k)
        hr = srcs[w].shape[1] // 2
        return srcs[w].at[2 * px + py, pl.ds(pc * hr, hr), :]

    def remote(w, k):
        return pltpu.make_async_remote_copy(
            src_ref=piece(w, k), dst_ref=dsts[w].at[k], send_sem=send.at[k, w], recv_sem=recv.at[k, w],
            device_id=_peer(pos, k), device_id_type=MESH)

    def local(w):
        return pltpu.make_async_copy(piece(w, 0), dsts[w].at[0], lsem.at[w])

    def start():
        for w in range(nw):
            local(w).start()
        for k in range(1, N_DEV):
            for w in range(nw):
                remote(w, k).start()

    def finish():
        for k in range(1, N_DEV):
            for w in range(nw):
                remote(w, k).wait_recv()
        for k in range(1, N_DEV):
            for w in range(nw):
                remote(w, k).wait_send()
        for w in range(nw):
            local(w).wait()

    return start, finish


def _pack_small(sm, norm_g, fin_g, conv_b, conv_w, loss_vec, rel):
    sm[...] = jnp.zeros_like(sm)
    for row, ref in ((0, norm_g), (1, fin_g), (2, conv_b), (LOSS_ROW, loss_vec)):
        sm[row:row + 1, 0:ref.shape[1]] = ref[...]
    sm[3:6, 0:conv_w.shape[1]] = conv_w[0:3, :]
    sm[8:8 + HEADS, 0:rel.shape[1]] = rel[...]


def kernel(x, norm_g, w_in, rel_bias, w_att_out, conv_w, conv_b, w_conv_out, w_out, final_norm_g, loss_target, m_norm_g, m_w_in, m_rel_bias, m_w_att_out, m_conv_w, m_conv_b, m_w_conv_out, m_w_out, m_final_norm_g, v_norm_g, v_w_in, v_rel_bias, v_w_att_out, v_conv_w, v_conv_b, v_w_conv_out, v_w_out, v_final_norm_g):
    xs, tgt = x[0], loss_target[0]
    cshard = conv_w.shape[2]
    chip = 2 * lax.axis_index("x") + lax.axis_index("y")

    shards = [w_in[0], w_att_out[0], w_conv_out[0], w_out[0]]
    cw3 = conv_w[0]
    flips = jnp.arange(N_CHIPS, dtype=jnp.int32)
    own_first = jnp.bitwise_xor(chip, flips)
    own_last = jnp.bitwise_xor(chip, jnp.asarray(OWNER_FLIPS, jnp.int32))

    proj, h, wb_in = _in_proj_gather(xs, norm_g, shards[0], own_first)
    diag = jnp.take(rel_bias[0], _diag_rel_index(), axis=1)
    att, lse, wb_att, wb_conv, wb_out, cw_all = _attn_fwd(diag, proj, shards[1:], ["cols", "cols", "rows"], cw3)
    conv_w_full = jnp.transpose(cw_all, (1, 0, 2)).reshape(cw3.shape[0], N_CHIPS * cshard)
    (dpb, d_att, dx2, g_att_p, g_conv_p, g_out_p, loss_vec, g_fin, g_cb, g_cw) = _mixer_mid(
        att, proj, xs, tgt, wb_att, wb_conv, wb_out, conv_w_full, conv_b,
        final_norm_g[None, :])
    dqkv, g_rel, r_att, r_conv, r_out = _attn_bwd(diag, proj, d_att, att, lse, [g_att_p, g_conv_p, g_out_p])
    grad_x, g_norm = _in_proj_bwd_x(dqkv, dpb, wb_in, xs, dx2, norm_g)
    gw_in, gw_att, gw_conv, gw_out, r_small = _in_proj_bwd_w(
        h, dqkv, dpb, own_last, [r_att, r_conv, r_out], [g_norm, g_fin, g_cb, g_cw, loss_vec, g_rel])

    small_out = _adamw_small(
        r_small,
        [norm_g, final_norm_g[None, :], conv_b, conv_w[0], rel_bias[0]],
        [m_norm_g, m_final_norm_g[None, :], m_conv_b, m_conv_w[0], m_rel_bias[0]],
        [v_norm_g, v_final_norm_g[None, :], v_conv_b, v_conv_w[0], v_rel_bias[0]],
        [norm_g.shape, final_norm_g.shape, conv_b.shape, conv_w[0].shape, rel_bias[0].shape])
    loss = small_out[0][0, 0]
    small_names = ["norm_g", "final_norm_g", "conv_b", "conv_w", "rel_bias"]
    fix = {"norm_g": lambda a: a, "final_norm_g": lambda a: a, "conv_b": lambda a: a,
           "conv_w": lambda a: a[None], "rel_bias": lambda a: a[None]}
    small_res = {name: [fix[name](small_out[1 + 4 * p + q]) for q in range(4)] for p, name in enumerate(small_names)}

    names = ["w_in", "w_att_out", "w_conv_out", "w_out"]
    group = _adamw_group([w_in[0], w_att_out[0], w_conv_out[0], w_out[0]], [gw_in, gw_att, gw_conv, gw_out],
                         [m_w_in[0], m_w_att_out[0], m_w_conv_out[0], m_w_out[0]],
                         [v_w_in[0], v_w_att_out[0], v_w_conv_out[0], v_w_out[0]], "adamw_matrices", ADAMW_STEPS)
    big = dict(zip(names, group))
    big = {name: tuple(a[None] for a in four) for name, four in big.items()}

    order = ["norm_g", "w_in", "rel_bias", "w_att_out", "conv_w", "conv_b", "w_conv_out", "w_out", "final_norm_g"]
    outs = [loss, grad_x[None]]
    for which in range(4):
        for name in order:
            outs.append(big[name][which] if name in big else small_res[name][which])
    return tuple(outs)
```

```python
import numpy as np
import jax
import jax.numpy as jnp
from jax import lax
from jax.experimental import pallas as pl
from jax.experimental.pallas import tpu as pltpu

F32 = jnp.float32
BF16 = jnp.bfloat16
MESH = pl.DeviceIdType.MESH

CHUNK = 64
N_LEFT = 8
HEADS = 8
HEAD_DIM = 64
D_ATT = HEADS * HEAD_DIM
MAX_REL = 128
N_REL = 2 * MAX_REL + 1
EPS = 1e-6
NEG_BIG = -1e30
ADAM_LR, ADAM_B1, ADAM_B2, ADAM_EPS, ADAM_WD, ADAM_STEP = 0.001, 0.9, 0.999, 1e-08, 0.01, 10

LANES = 128
SUBLANES = 8
VMEM_LIMIT = 56 * 1024 * 1024

QB = 2 * CHUNK
KW = N_LEFT * CHUNK + QB
DIAG = KW + QB
TQ = N_LEFT * CHUNK
TM_MID = 256
ADAMW_STEPS = 8
TM_MM = 512
TM_BLK = 1024
SMALL_ROWS, SMALL_COLS = 16, 1024


def _params(*sem):
    return pltpu.CompilerParams(dimension_semantics=sem, vmem_limit_bytes=VMEM_LIMIT)


def _nt(a, b):
    return lax.dot_general(a, b, (((1,), (1,)), ((), ())), preferred_element_type=F32)


def _tn(a, b):
    return lax.dot_general(a, b, (((0,), (0,)), ((), ())), preferred_element_type=F32)


def _nn(a, b):
    return jnp.dot(a, b, preferred_element_type=F32)


def _diag_rel_index():
    d = np.arange(DIAG)
    diff = np.where(d < KW, d, d - DIAG)
    rel = N_LEFT * CHUNK - diff
    return np.clip(rel, -MAX_REL, MAX_REL) + MAX_REL


def _build_bias(diag_ref, bias_scr):
    r = lax.broadcasted_iota(jnp.int32, (QB, KW), 0) // CHUNK
    s = lax.broadcasted_iota(jnp.int32, (QB, KW), 1) // CHUNK
    allowed = (s >= r) & (s <= r + N_LEFT)
    for h in range(HEADS):
        row = jnp.broadcast_to(diag_ref[h:h + 1, :], (QB, DIAG))
        t = pltpu.roll(row, 0, 1, stride=1, stride_axis=0)
        bias_scr[h // 2, (h % 2) * QB:(h % 2 + 1) * QB, :] = jnp.where(allowed, t[:, :KW], NEG_BIG)


def _stack_heads(a, lane_hi):
    zero = jnp.zeros_like(a)
    return jnp.concatenate([jnp.where(lane_hi, zero, a), jnp.where(lane_hi, a, zero)], axis=0)


def _in_proj_gather(x, g, shard, order):
    s, d = x.shape
    tn = shard.shape[1]
    n = s // TM_BLK
    hr = d // 2

    def body(order_ref, x_ref, g_ref, shard32_ref, proj_ref, h_ref, wfull_ref, shard_ref, hbuf, wbuf, stage32,
             send1, recv1, send2, recv2, lsem):
        del order_ref
        j, i = pl.program_id(0), pl.program_id(1)
        x, y, c = _position()
        mine = 2 * x + y
        chips = [(x, 1 - y), (1 - x, y), (1 - x, 1 - y)]

        def theirs(k):
            return 2 * chips[k][0] + chips[k][1]

        def half_rows(half):
            return pl.ds(half * hr, hr)

        def landing(k, shard_index, half):
            if k < 2:
                return wbuf.at[k, half_rows(half), :]
            return wfull_ref.at[shard_index, half_rows(half), :]

        def ici(k, shard_index):
            return pltpu.make_async_remote_copy(
                src_ref=shard_ref.at[half_rows(c), :], dst_ref=landing(k, shard_index, c),
                send_sem=send1.at[k], recv_sem=recv1.at[k], device_id=(*chips[k], c), device_id_type=MESH)

        def d2d(k, half):
            return pltpu.make_async_remote_copy(
                src_ref=wbuf.at[k % 2, half_rows(half), :], dst_ref=landing(k, theirs(k), half),
                send_sem=send2.at[k], recv_sem=recv2.at[k], device_id=(x, y, 1 - c), device_id_type=MESH)

        def load(k, half, sem):
            return pltpu.make_async_copy(wfull_ref.at[theirs(k), half_rows(half), :],
                                         wbuf.at[k % 2, half_rows(half), :], lsem.at[sem])

        def keep(k):
            return pltpu.make_async_copy(wbuf.at[k], wfull_ref.at[theirs(k)], lsem.at[3 + k])

        own = pltpu.make_async_copy(shard_ref, wfull_ref.at[mine], lsem.at[0])

        def put_proj(block):
            for grp in range(tn // D_ATT):
                proj_ref[grp] = block[:, grp * D_ATT:(grp + 1) * D_ATT].astype(BF16)

        @pl.when((j == 0) & (i == 0))
        def _():
            fetch = [pltpu.make_async_copy(shard32_ref.at[half_rows(half), :], stage32.at[half_rows(half), :],
                                           lsem.at[5 + t]) for t, half in enumerate((c, 1 - c))]
            for cp in fetch:
                cp.start()
            fetch[0].wait()
            shard_ref[half_rows(c), :] = stage32[half_rows(c), :].astype(BF16)
            ici(0, mine).start()
            ici(1, mine).start()
            fetch[1].wait()
            shard_ref[half_rows(1 - c), :] = stage32[half_rows(1 - c), :].astype(BF16)
            own.start()

        for k in range(2):
            first = n - 1 if k == 0 else min(n // 2, n - 1)

            @pl.when((j == k) & (i == first))
            def _(k=k):
                ici(k, theirs(k)).wait_recv()
                d2d(k, c).start()
                if k == 0:
                    ici(0, mine).wait_send()
                    ici(1, mine).wait_send()
                    ici(2, mine).start()

            then = (1, 0) if k == 0 else (1, min(first + 1, n - 1))

            @pl.when((j == then[0]) & (i == then[1]))
            def _(k=k):
                d2d(k, 1 - c).wait_recv()
                keep(k).start()

        first = min(n // 2, n - 1)

        @pl.when((j == 2) & (i == first))
        def _():
            d2d(0, c).wait_send()
            keep(0).wait()
            ici(2, theirs(2)).wait_recv()
            load(2, c, 1).start()

        @pl.when((j == 2) & (i == min(first + 1, n - 1)))
        def _():
            load(2, c, 1).wait()
            d2d(2, c).start()

        @pl.when((j == 2) & (i == min(first + 2, n - 1)))
        def _():
            d2d(2, 1 - c).wait_recv()
            load(2, 1 - c, 2).start()

        @pl.when((j == 3) & (i == 0))
        def _():
            load(2, 1 - c, 2).wait()

        @pl.when(j == 0)
        def _():
            xv = x_ref[...]
            r = lax.rsqrt(jnp.mean(xv * xv, axis=-1, keepdims=True) + EPS)
            hv = ((xv * r) * g_ref[...]).astype(BF16)
            hbuf[i] = hv
            h_ref[...] = hv
            put_proj(_nn(hv, shard_ref[...]))

        for k in range(3):
            @pl.when(j == k + 1)
            def _(k=k):
                put_proj(_nn(hbuf[i], wbuf[k % 2]))

        @pl.when((j == 3) & (i == n - 1))
        def _():
            ici(2, mine).wait_send()
            d2d(1, c).wait_send()
            d2d(2, c).wait_send()
            own.wait()
            keep(1).wait()

    return pl.pallas_call(
        body, name="in_proj_gather",
        grid_spec=pltpu.PrefetchScalarGridSpec(
            num_scalar_prefetch=1, grid=(N_CHIPS, n),
            in_specs=[pl.BlockSpec((TM_BLK, d), lambda j, i, order: (jnp.where(j == 0, i, n - 1), 0)),
                      pl.BlockSpec((1, d), lambda j, i, order: (0, 0)), ANY],
            out_specs=[pl.BlockSpec((tn // D_ATT, TM_BLK, D_ATT), lambda j, i, order: (order[j], i, 0)),
                       pl.BlockSpec((TM_BLK, d), lambda j, i, order: (jnp.where(j == 0, i, n - 1), 0)), ANY],
            scratch_shapes=[pltpu.VMEM((d, tn), BF16), pltpu.VMEM((n, TM_BLK, d), BF16), pltpu.VMEM((2, d, tn), BF16),
                            pltpu.VMEM((d, tn), F32),
                            pltpu.SemaphoreType.DMA((3,)), pltpu.SemaphoreType.DMA((3,)),
                            pltpu.SemaphoreType.DMA((3,)), pltpu.SemaphoreType.DMA((3,)), pltpu.SemaphoreType.DMA((7,))]),
        out_shape=[jax.ShapeDtypeStruct((N_CHIPS * tn // D_ATT, s, D_ATT), BF16), jax.ShapeDtypeStruct((s, d), BF16),
                   jax.ShapeDtypeStruct((N_CHIPS, d, tn), BF16)],
        compiler_params=_params("arbitrary", "arbitrary"),
    )(order, x, g, shard)


def _attn_fwd(diag, proj, shards, kinds, cw3):
    s = proj.shape[1]
    n = s // TQ
    nw = len(shards)
    scale = HEAD_DIM ** -0.5

    def body(*refs):
        diag_ref, q_ref, kp_ref, kc_ref, vp_ref, vc_ref = refs[:6]
        srcs, cw = refs[6:6 + nw], refs[6 + nw]
        o_ref, lse_ref = refs[7 + nw:9 + nw]
        dsts, cw_all = refs[9 + nw:9 + 2 * nw], refs[9 + 2 * nw]
        bias_scr = refs[10 + 2 * nw]
        casts = refs[11 + 2 * nw:11 + 3 * nw]
        start, forward, finish = _gather_plan(kinds, casts, dsts, cw, cw_all, *refs[11 + 3 * nw:])
        i = pl.program_id(0)

        @pl.when(i == 0)
        def _():
            for w in range(nw):
                casts[w][...] = srcs[w][...].astype(BF16)
            start()
            _build_bias(diag_ref, bias_scr)

        @pl.when(i == n // 2)
        def _():
            forward()

        @pl.when(i == n - 1)
        def _():
            finish()

        lane_hi = lax.broadcasted_iota(jnp.int32, (QB, LANES), 1) >= HEAD_DIM

        def block(b, first_tile):
            r0, n_prev = b * QB, TQ - b * QB
            n_cur = KW - n_prev
            pairs = range(HEADS // 2)
            lanes_of = [slice(LANES * p, LANES * (p + 1)) for p in pairs]
            scores = []
            for p in pairs:
                lanes = lanes_of[p]
                q2 = _stack_heads(q_ref[r0:r0 + QB, lanes] * scale, lane_hi)
                s_cur = _nt(q2, kc_ref[0:n_cur, lanes]) + bias_scr[p, :, n_prev:KW]
                if first_tile:
                    scores.append(s_cur)
                else:
                    scores.append(jnp.concatenate(
                        [_nt(q2, kp_ref[r0:TQ, lanes]) + bias_scr[p, :, 0:n_prev], s_cur], axis=1))
            probs = []
            for p in pairs:
                sc = scores[p]
                m = jnp.max(sc, axis=1, keepdims=True)
                pe = jnp.exp(sc - m)
                l = jnp.sum(pe, axis=1, keepdims=True)
                probs.append((pe.astype(BF16), l, m))
            for p in pairs:
                lanes = lanes_of[p]
                pb, l, m = probs[p]
                if first_tile:
                    o2 = _nn(pb, vc_ref[0:n_cur, lanes]) / l
                else:
                    o2 = (_nn(pb[:, 0:n_prev], vp_ref[r0:TQ, lanes]) + _nn(pb[:, n_prev:KW], vc_ref[0:n_cur, lanes])) / l
                lse2 = m + jnp.log(l)
                lse_ref[r0:r0 + QB, 2 * p:2 * p + 1] = lse2[0:QB, :]
                lse_ref[r0:r0 + QB, 2 * p + 1:2 * p + 2] = lse2[QB:2 * QB, :]
                o_ref[r0:r0 + QB, lanes] = jnp.where(lane_hi, o2[QB:2 * QB, :], o2[0:QB, :]).astype(BF16)

        @pl.when(i == 0)
        def _():
            for b in range(TQ // QB):
                block(b, True)

        @pl.when(i > 0)
        def _():
            for b in range(TQ // QB):
                block(b, False)

    blk = lambda grp, prev: pl.BlockSpec(
        (None, TQ, D_ATT), (lambda i: (grp, jnp.maximum(i - 1, 0), 0)) if prev else (lambda i: (grp, i, 0)))
    vmem = pl.BlockSpec(memory_space=pltpu.VMEM)
    return pl.pallas_call(
        body, name="attn_fwd", grid=(n,),
        in_specs=[pl.BlockSpec((HEADS, DIAG), lambda i: (0, 0)),
                  blk(0, False), blk(1, True), blk(1, False), blk(2, True), blk(2, False)] + [vmem] * (nw + 1),
        out_specs=[pl.BlockSpec((TQ, D_ATT), lambda i: (i, 0)), pl.BlockSpec((TQ, HEADS), lambda i: (i, 0))]
        + [ANY] * (nw + 1),
        out_shape=[jax.ShapeDtypeStruct((s, D_ATT), BF16), jax.ShapeDtypeStruct((s, HEADS), F32)]
        + _gather_out_shapes(shards, kinds, cw3),
        scratch_shapes=[pltpu.VMEM((HEADS // 2, 2 * QB, KW), F32)] + [pltpu.VMEM(a.shape, BF16) for a in shards]
        + _gather_sems(nw),
        compiler_params=_params("arbitrary"),
    )(diag, proj, proj, proj, proj, proj, *shards, cw3)


def _attn_bwd(diag, proj, d_att, att, lse, parts):
    s = proj.shape[1]
    n = s // TQ
    npart = len(parts)
    scale = HEAD_DIM ** -0.5
    rel_pad = 3 * LANES

    def body(*refs):
        diag_ref, q_ref, kp_ref, kc_ref, vp_ref, vc_ref, do_ref, o_ref, lse_ref = refs[:9]
        part_refs = refs[9:9 + npart]
        dqkv_ref, dbias_ref = refs[9 + npart:11 + npart]
        slot_refs = refs[11 + npart:11 + 2 * npart]
        bias_scr, dbias_acc, dk_acc, dv_acc, dq_scr = refs[11 + 2 * npart:16 + 2 * npart]
        start, finish = _scatter_plan(part_refs, slot_refs, *refs[16 + 2 * npart:])
        i = pl.program_id(0)
        cur, prv = i % 2, 1 - i % 2

        @pl.when(i == 0)
        def _():
            start()
            _build_bias(diag_ref, bias_scr)
            dbias_acc[...] = jnp.zeros_like(dbias_acc)
            dk_acc[...] = jnp.zeros_like(dk_acc)
            dv_acc[...] = jnp.zeros_like(dv_acc)

        @pl.when(i > 0)
        def _():
            dqkv_ref[:, 0:D_ATT] = dq_scr[...]
            dk_acc[cur] = jnp.zeros((TQ, D_ATT), F32)
            dv_acc[cur] = jnp.zeros((TQ, D_ATT), F32)

        lane_hi = lax.broadcasted_iota(jnp.int32, (QB, LANES), 1) >= HEAD_DIM
        col = lax.broadcasted_iota(jnp.int32, (2 * QB, KW), 1)

        def make_block(first_tile):
            def block(b):
                r0, n_prev = b * QB, TQ - b * QB
                n_cur = KW - n_prev
                for p in range(HEADS // 2):
                    lanes = slice(LANES * p, LANES * (p + 1))
                    q2 = _stack_heads(q_ref[r0:r0 + QB, lanes] * scale, lane_hi)
                    kw = jnp.concatenate([kp_ref[r0:TQ, lanes], kc_ref[0:n_cur, lanes]], axis=0)
                    vw = jnp.concatenate([vp_ref[r0:TQ, lanes], vc_ref[0:n_cur, lanes]], axis=0)
                    dop = do_ref[r0:r0 + QB, lanes]
                    do2 = _stack_heads(dop, lane_hi)
                    prod = dop.astype(F32) * o_ref[r0:r0 + QB, lanes].astype(F32)
                    delta2 = jnp.concatenate(
                        [jnp.sum(jnp.where(lane_hi, 0.0, prod), axis=1, keepdims=True),
                         jnp.sum(jnp.where(lane_hi, prod, 0.0), axis=1, keepdims=True)], axis=0)
                    lse2 = jnp.concatenate([lse_ref[r0:r0 + QB, 2 * p:2 * p + 1],
                                            lse_ref[r0:r0 + QB, 2 * p + 1:2 * p + 2]], axis=0)
                    sc = _nt(q2, kw) + bias_scr[p]
                    if first_tile:
                        sc = jnp.where(col >= TQ - r0, sc, NEG_BIG)
                    pr = jnp.exp(sc - lse2)
                    ds = pr * (_nt(do2, vw) - delta2)
                    dbias_acc[p] += ds
                    dsb = ds.astype(BF16)
                    dv_w = _tn(pr.astype(BF16), do2)
                    dk_w = _tn(dsb, q2)
                    dv_acc[prv, r0:TQ, lanes] += dv_w[0:n_prev, :]
                    dv_acc[cur, 0:n_cur, lanes] += dv_w[n_prev:KW, :]
                    dk_acc[prv, r0:TQ, lanes] += dk_w[0:n_prev, :]
                    dk_acc[cur, 0:n_cur, lanes] += dk_w[n_prev:KW, :]
                    dq2 = _nn(dsb, kw)
                    dq = jnp.where(lane_hi, dq2[QB:2 * QB, :], dq2[0:QB, :]) * scale
                    dq_scr[r0:r0 + QB, lanes] = dq.astype(BF16)
            return block

        @pl.when(i == 0)
        def _():
            for b in range(TQ // QB):
                make_block(True)(b)

        @pl.when((i > 0) & (i < n))
        def _():
            for b in range(TQ // QB):
                make_block(False)(b)

        @pl.when(i > 0)
        def _():
            dqkv_ref[:, D_ATT:2 * D_ATT] = dk_acc[prv].astype(BF16)
            dqkv_ref[:, 2 * D_ATT:3 * D_ATT] = dv_acc[prv].astype(BF16)

        @pl.when(i == n)
        def _():
            d_iota = lax.broadcasted_iota(jnp.int32, (DIAG, rel_pad), 0)
            n_iota = lax.broadcasted_iota(jnp.int32, (DIAG, rel_pad), 1)
            diff = jnp.where(d_iota < KW, d_iota, d_iota - DIAG)
            idx = jnp.clip(N_LEFT * CHUNK - diff, -MAX_REL, MAX_REL) + MAX_REL
            onehot = (idx == n_iota).astype(F32)
            rows = []
            for hd in range(HEADS):
                acc = dbias_acc[hd // 2, (hd % 2) * QB:(hd % 2 + 1) * QB, :]
                a = jnp.concatenate([acc, jnp.zeros((QB, DIAG - KW), F32)], axis=1)
                g8 = a[0:SUBLANES, :]
                for blk in range(1, QB // SUBLANES):
                    g8 = g8 + pltpu.roll(a[blk * SUBLANES:(blk + 1) * SUBLANES, :], DIAG - blk * SUBLANES, 1)
                g1 = g8[0:1, :]
                for r in range(1, SUBLANES):
                    g1 = g1 + pltpu.roll(g8[r:r + 1, :], DIAG - r, 1)
                rows.append(g1)
            g = jnp.concatenate(rows, axis=0)
            dbias_ref[...] = jnp.dot(g, onehot, preferred_element_type=F32, precision=lax.Precision.HIGHEST)
            finish()

    last = n - 1
    cur = lambda grp: pl.BlockSpec((None, TQ, D_ATT), lambda i: (grp, jnp.minimum(i, last), 0))
    prev = lambda grp: pl.BlockSpec((None, TQ, D_ATT), lambda i: (grp, jnp.maximum(jnp.minimum(i, last) - 1, 0), 0))
    tile = pl.BlockSpec((TQ, D_ATT), lambda i: (jnp.minimum(i, last), 0))
    return pl.pallas_call(
        body, name="attn_bwd", grid=(n + 1,),
        in_specs=[pl.BlockSpec((HEADS, DIAG), lambda i: (0, 0)),
                  cur(0), prev(1), cur(1), prev(2), cur(2), tile, tile,
                  pl.BlockSpec((TQ, HEADS), lambda i: (jnp.minimum(i, last), 0))] + [ANY] * npart,
        out_specs=[pl.BlockSpec((TQ, 3 * D_ATT), lambda i: (jnp.maximum(i - 1, 0), 0)),
                   pl.BlockSpec((HEADS, rel_pad), lambda i: (0, 0))] + [ANY] * npart,
        out_shape=[jax.ShapeDtypeStruct((s, 3 * D_ATT), BF16), jax.ShapeDtypeStruct((HEADS, rel_pad), F32)]
        + _scatter_out_shapes(parts),
        scratch_shapes=[pltpu.VMEM((HEADS // 2, 2 * QB, KW), F32), pltpu.VMEM((HEADS // 2, 2 * QB, KW), F32),
                        pltpu.VMEM((2, TQ, D_ATT), F32), pltpu.VMEM((2, TQ, D_ATT), F32),
                        pltpu.VMEM((TQ, D_ATT), BF16)] + _scatter_sems(npart),
        compiler_params=_params("arbitrary"),
    )(diag, proj, proj, proj, proj, proj, d_att, att, lse, *parts)


def _shift_down(a, k, halo):
    rolled = pltpu.roll(a, k, 0)
    row = lax.broadcasted_iota(jnp.int32, halo.shape, 0)
    first = jnp.where(row < k, pltpu.roll(halo, k, 0), rolled[0:SUBLANES, :])
    return jnp.concatenate([first, rolled[SUBLANES:, :]], axis=0)


def _shift_up(a, k, nxt):
    tm = a.shape[0]
    rolled = pltpu.roll(a, tm - k, 0)
    row = lax.broadcasted_iota(jnp.int32, nxt.shape, 0)
    last = jnp.where(row >= SUBLANES - k, pltpu.roll(nxt, SUBLANES - k, 0), rolled[tm - SUBLANES:, :])
    return jnp.concatenate([rolled[:tm - SUBLANES, :], last], axis=0)


def _sigmoid(v):
    return 0.5 * jnp.tanh(0.5 * v) + 0.5


def _mixer_mid(att, proj, x, tgt, w_att, w_conv, w_out, conv_w, conv_b, fin_g):
    s, d = x.shape
    dc = D_ATT
    n = s // TM_MID
    tm = TM_MID
    n_shards = 4

    def body(att_ref, za_ref, gb_ref, gc_ref, u_ref, zc_ref, hgc_ref, hu_ref, gatt_ref, gconv_ref, x_ref, t_ref,
             watt_ref, wconv_ref, wout_ref, cw_ref, cb_ref, fg_ref,
             dpb_ref, do_ref, dx2_ref, gatt_o, gconv_o, gout_o, loss_o, gfn_o, gcb_o, gcw_o,
             acc_att, acc_conv, acc_out, carry, watt_t_ref, wconv_t_ref, wout_t_ref):
        i = pl.program_id(0)
        tile = n - 1 - i

        @pl.when(i == 0)
        def _():
            acc_att[...] = jnp.zeros_like(acc_att)
            acc_conv[...] = jnp.zeros_like(acc_conv)
            acc_out[...] = jnp.zeros_like(acc_out)
            carry[...] = jnp.zeros_like(carry)
            loss_o[...] = jnp.zeros_like(loss_o)
            gfn_o[...] = jnp.zeros_like(gfn_o)
            gcb_o[...] = jnp.zeros_like(gcb_o)
            gcw_o[...] = jnp.zeros_like(gcw_o)
            watt_t_ref[...] = watt_ref[...].T
            wconv_t_ref[...] = wconv_ref[...].T
            wout_t_ref[...] = wout_ref[...].T

        halves = [slice(hh * (tm // 2), (hh + 1) * (tm // 2)) for hh in range(2)]
        both = lambda fn: [fn(rows) for rows in halves]
        f32 = lambda ref, rows: ref[rows, :].astype(F32)

        gc = gc_ref[...].astype(F32)
        u = u_ref[...].astype(F32)
        cu = gc * u
        halo = jnp.where(tile > 0, hgc_ref[...].astype(F32) * hu_ref[...].astype(F32), 0.0)
        cu1 = _shift_down(cu, 1, halo)
        cu2 = _shift_down(cu, 2, halo)
        w0, w1, w2 = cw_ref[0:1, :], cw_ref[1:2, :], cw_ref[2:3, :]
        fg = fg_ref[...]

        def stage_a(rows):
            att_v, za, zc, gb = f32(att_ref, rows), f32(za_ref, rows), f32(zc_ref, rows), f32(gb_ref, rows)
            sa = _sigmoid(za)
            silu_a = za * sa
            vconv = w0 * cu2[rows, :] + w1 * cu1[rows, :] + w2 * cu[rows, :] + cb_ref[...]
            sc = _sigmoid(zc)
            silu_c = zc * sc
            return dict(att_v=att_v, za=za, zc=zc, gb=gb, sa=sa, silu_a=silu_a, vconv=vconv, sc=sc, silu_c=silu_c,
                        a_b=(att_v * silu_a).astype(BF16), c_b=(gb * vconv * silu_c).astype(BF16))

        st = both(stage_a)
        for t in st:
            t["y_att"] = _nn(t["a_b"], watt_ref[...])
            t["y_conv"] = _nn(t["c_b"], wconv_ref[...])
        for t, rows in zip(st, halves):
            gpair = lambda ref: jnp.concatenate([ref[0, rows, :], ref[1, rows, :]], axis=1).astype(F32)
            t["ga"] = _sigmoid(gpair(gatt_ref))
            t["gv"] = _sigmoid(gpair(gconv_ref))
            t["m_b"] = (t["ga"] * t["y_att"] + t["gv"] * t["y_conv"]).astype(BF16)
        for t in st:
            t["mo"] = _nn(t["m_b"], wout_ref[...])
        for t, rows in zip(st, halves):
            x2 = x_ref[rows, :] + t["mo"]
            r2 = lax.rsqrt(jnp.mean(x2 * x2, axis=-1, keepdims=True) + EPS)
            x2n = x2 * r2
            err = x2n * fg - t_ref[rows, :]
            loss_o[...] += jnp.sum(err * err, axis=0, keepdims=True) * (0.5 / d)
            dy = err * (1.0 / d)
            gfn_o[...] += jnp.sum(dy * x2n, axis=0, keepdims=True)
            dyn = dy * fg
            dx2 = r2 * (dyn - x2n * jnp.mean(dyn * x2n, axis=-1, keepdims=True))
            dx2_ref[rows, :] = dx2
            t["dx2_b"] = dx2.astype(BF16)
        for t in st:
            t["dm"] = _nn(t["dx2_b"], wout_t_ref[...])
        whole = lambda key: jnp.concatenate([st[0][key], st[1][key]], axis=0)
        acc_out[...] += _tn(whole("m_b"), whole("dx2_b"))
        for t, rows in zip(st, halves):
            dy_att = t["dm"] * t["ga"]
            dy_conv = t["dm"] * t["gv"]
            dpb_ref[rows, 5 * dc:5 * dc + d] = (dy_att * t["y_att"] * (1.0 - t["ga"])).astype(BF16)
            dpb_ref[rows, 5 * dc + d:5 * dc + 2 * d] = (dy_conv * t["y_conv"] * (1.0 - t["gv"])).astype(BF16)
            t["dya_b"] = dy_att.astype(BF16)
            t["dyc_b"] = dy_conv.astype(BF16)
        for t in st:
            t["da_in"] = _nn(t["dya_b"], watt_t_ref[...])
            t["dc_in"] = _nn(t["dyc_b"], wconv_t_ref[...])
        acc_att[...] += _tn(whole("a_b"), whole("dya_b"))
        acc_conv[...] += _tn(whole("c_b"), whole("dyc_b"))
        for t, rows in zip(st, halves):
            sa, za, sc, zc = t["sa"], t["za"], t["sc"], t["zc"]
            do_ref[rows, :] = (t["da_in"] * t["silu_a"]).astype(BF16)
            dpb_ref[rows, 0:dc] = (t["da_in"] * t["att_v"] * (sa * (1.0 + za * (1.0 - sa)))).astype(BF16)
            dpb_ref[rows, dc:2 * dc] = (t["dc_in"] * t["vconv"] * t["silu_c"]).astype(BF16)
            dgs = t["dc_in"] * t["gb"]
            t["dvc"] = dgs * t["silu_c"]
            dpb_ref[rows, 4 * dc:5 * dc] = (dgs * t["vconv"] * (sc * (1.0 + zc * (1.0 - sc)))).astype(BF16)
        dvc = whole("dvc")
        gcb_o[...] += jnp.sum(dvc, axis=0, keepdims=True)
        gcw_o[0:1, :] += jnp.sum(dvc * cu2, axis=0, keepdims=True)
        gcw_o[1:2, :] += jnp.sum(dvc * cu1, axis=0, keepdims=True)
        gcw_o[2:3, :] += jnp.sum(dvc * cu, axis=0, keepdims=True)
        nxt = carry[...]
        dcu = w2 * dvc + w1 * _shift_up(dvc, 1, nxt) + w0 * _shift_up(dvc, 2, nxt)
        carry[...] = dvc[0:SUBLANES, :]
        dpb_ref[:, 2 * dc:3 * dc] = (dcu * u).astype(BF16)
        dpb_ref[:, 3 * dc:4 * dc] = (dcu * gc).astype(BF16)

        @pl.when(i == n - 1)
        def _():
            for j in range(n_shards):
                gatt_o[j] = acc_att[:, j * (d // n_shards):(j + 1) * (d // n_shards)].astype(BF16)
                gconv_o[j] = acc_conv[:, j * (d // n_shards):(j + 1) * (d // n_shards)].astype(BF16)
                gout_o[j] = acc_out[j * (d // n_shards):(j + 1) * (d // n_shards), :].astype(BF16)

    rev = lambda width, col_blk: pl.BlockSpec((tm, width), lambda i: (n - 1 - i, col_blk))
    grp = lambda g: pl.BlockSpec((None, tm, dc), lambda i: (g, n - 1 - i, 0))
    grp2 = lambda g2: pl.BlockSpec((2, tm, dc), lambda i: (g2, n - 1 - i, 0))
    halo_spec = lambda g: pl.BlockSpec(
        (None, SUBLANES, dc), lambda i: (g, jnp.maximum((n - 1 - i) * (tm // SUBLANES) - 1, 0), 0))
    const = lambda shape: pl.BlockSpec(shape, lambda i: tuple(0 for _ in shape), pipeline_mode=pl.Buffered(1))
    q4 = d // n_shards
    return pl.pallas_call(
        body, name="mixer_mid", grid=(n,),
        in_specs=[rev(dc, 0), grp(3), grp(4), grp(5), grp(6), grp(7),
                  halo_spec(5), halo_spec(6), grp2(4), grp2(5), rev(d, 0), rev(d, 0),
                  const((dc, d)), const((dc, d)), const((d, d)),
                  const(conv_w.shape), const((1, dc)), const((1, d))],
        out_specs=[rev(5 * dc + 2 * d, 0), rev(dc, 0), rev(d, 0),
                   const((n_shards, dc, q4)), const((n_shards, dc, q4)), const((n_shards, q4, d)),
                   const((1, d)), const((1, d)), const((1, dc)), const((SUBLANES, dc))],
        out_shape=[jax.ShapeDtypeStruct((s, 5 * dc + 2 * d), BF16), jax.ShapeDtypeStruct((s, dc), BF16),
                   jax.ShapeDtypeStruct((s, d), F32),
                   jax.ShapeDtypeStruct((n_shards, dc, q4), BF16), jax.ShapeDtypeStruct((n_shards, dc, q4), BF16),
                   jax.ShapeDtypeStruct((n_shards, q4, d), BF16),
                   jax.ShapeDtypeStruct((1, d), F32), jax.ShapeDtypeStruct((1, d), F32),
                   jax.ShapeDtypeStruct((1, dc), F32), jax.ShapeDtypeStruct((SUBLANES, dc), F32)],
        scratch_shapes=[pltpu.VMEM((dc, d), F32), pltpu.VMEM((dc, d), F32), pltpu.VMEM((d, d), F32),
                        pltpu.VMEM((SUBLANES, dc), F32),
                        pltpu.VMEM((d, dc), BF16), pltpu.VMEM((d, dc), BF16), pltpu.VMEM((d, d), BF16)],
        compiler_params=_params("arbitrary"),
    )(att, proj, proj, proj, proj, proj, proj, proj, proj, proj, x, tgt,
      w_att, w_conv, w_out, conv_w, conv_b, fin_g)


def _in_proj_bwd_x(dqkv, dpb, w_in, x, dx2, g):
    s, d = x.shape
    tn = dqkv.shape[1]
    nb = dpb.shape[1] // tn
    n = s // TM_MM

    def body(*refs):
        dps, ws = refs[:nb + 1], refs[nb + 1:2 * nb + 2]
        x_ref, dx2_ref, g_ref, gx_ref, gng_ref = refs[2 * nb + 2:]
        i = pl.program_id(0)

        @pl.when(i == 0)
        def _():
            gng_ref[...] = jnp.zeros_like(gng_ref)

        dh = _nt(dps[0][...], ws[0][0])
        for j in range(1, nb + 1):
            dh = dh + _nt(dps[j][...], ws[j][0])
        xv = x_ref[...]
        r = lax.rsqrt(jnp.mean(xv * xv, axis=-1, keepdims=True) + EPS)
        xn = xv * r
        gng_ref[...] += jnp.sum(dh * xn, axis=0, keepdims=True)
        dhn = dh * g_ref[...]
        gx_ref[...] = dx2_ref[...] + r * (dhn - xn * jnp.mean(dhn * xn, axis=-1, keepdims=True))

    tile = lambda width, col_blk: pl.BlockSpec((TM_MM, width), lambda i: (i, col_blk))
    wspec = lambda blk: pl.BlockSpec((1, d, tn), lambda i: (blk, 0, 0), pipeline_mode=pl.Buffered(1))
    return pl.pallas_call(
        body, name="in_proj_bwd_x", grid=(n,),
        in_specs=[tile(tn, 0)] + [tile(tn, j) for j in range(nb)] + [wspec(j) for j in range(nb + 1)]
        + [tile(d, 0), tile(d, 0), pl.BlockSpec((1, d), lambda i: (0, 0))],
        out_specs=[tile(d, 0), pl.BlockSpec((1, d), lambda i: (0, 0))],
        out_shape=[jax.ShapeDtypeStruct((s, d), F32), jax.ShapeDtypeStruct((1, d), F32)],
        compiler_params=_params("arbitrary"),
    )(dqkv, *([dpb] * nb), *([w_in] * (nb + 1)), x, dx2, g)


def _in_proj_bwd_w(h, dqkv, dpb, order, slots, small):
    s, d = h.shape
    tn = dqkv.shape[1]
    n = s // TM_BLK
    hr = d // 2
    settle = min(1, n - 1)
    nw, ns = len(slots), len(small)
    half_rows = [sl.shape[1] for sl in slots] + [hr]

    def body(order_ref, h_ref, da_ref, db_ref, *refs):
        srcs, parts = refs[:nw], refs[nw:nw + ns]
        slots_ref, shards, sm_all = refs[nw + ns], refs[nw + ns + 1:2 * nw + ns + 2], refs[2 * nw + ns + 2]
        acc, sendbuf, pairbuf, chipbuf = refs[2 * nw + ns + 3:2 * nw + ns + 7]
        halves, sm = refs[2 * nw + ns + 7:3 * nw + ns + 7], refs[3 * nw + ns + 7]
        psend, precv, send, recv, lsem, hsend, hrecv, ssend, srecv, hsem = refs[3 * nw + ns + 8:]
        j, i = pl.program_id(0), pl.program_id(1)
        blk = order_ref[j]
        pos = _position()
        x, y, c = pos
        me = 4 * x + 2 * y + c

        def half_sum(w):
            return halves[w] if w < nw else acc.at[pl.ds(0, hr), :]

        def half_of_shard(w, half):
            return shards[w].at[pl.ds(half * half_rows[w], half_rows[w]), :]

        def exchange(w, half):
            return pltpu.make_async_remote_copy(
                src_ref=half_sum(w), dst_ref=half_of_shard(w, half), send_sem=hsend.at[w], recv_sem=hrecv.at[w],
                device_id=(x, y, 1 - c), device_id_type=MESH)

        def keep(w):
            return pltpu.make_async_copy(half_sum(w), half_of_shard(w, c), hsem.at[w])

        def bcast(k, slot):
            return pltpu.make_async_remote_copy(
                src_ref=sm, dst_ref=sm_all.at[slot], send_sem=ssend.at[k], recv_sem=srecv.at[k],
                device_id=_peer(pos, k), device_id_type=MESH)

        own_small = pltpu.make_async_copy(sm, sm_all.at[me], hsem.at[nw + 1])

        @pl.when((j == 0) & (i == 0))
        def _():
            _pack_small(sm, *parts)
            own_small.start()
            for k in range(1, N_DEV):
                bcast(k, me).start()
            for w in range(nw):
                total = srcs[w][0].astype(F32)
                for k in range(1, srcs[w].shape[0]):
                    total = total + srcs[w][k].astype(F32)
                halves[w][...] = total
                exchange(w, c).start()
                keep(w).start()

        @pl.when(i == 0)
        def _():
            acc[...] = jnp.zeros_like(acc)

        @pl.when(blk == 0)
        def _():
            acc[...] += _tn(h_ref[...], da_ref[...])

        @pl.when(blk > 0)
        def _():
            acc[...] += _tn(h_ref[...], db_ref[...])

        def pair(step, half):
            return pltpu.make_async_remote_copy(
                src_ref=sendbuf.at[step, pl.ds(half * hr, hr), :], dst_ref=pairbuf.at[step],
                send_sem=psend.at[step], recv_sem=precv.at[step], device_id=(x, y, 1 - c), device_id_type=MESH)

        def ici(step):
            flip = OWNER_FLIPS[step]
            return pltpu.make_async_remote_copy(
                src_ref=chipbuf.at[step], dst_ref=slots_ref.at[flip], send_sem=send.at[step], recv_sem=recv.at[step],
                device_id=_peer(pos, 4 * (flip >> 1) + 2 * (flip & 1)), device_id_type=MESH)

        local = pltpu.make_async_copy(chipbuf.at[N_CHIPS - 1], slots_ref.at[0], lsem.at[0])

        def combine(step):
            pair(step, c).wait_recv()
            mine = sendbuf[step, pl.ds(c * hr, hr), :].astype(F32)
            chipbuf[step] = (mine + pairbuf[step].astype(F32)).astype(BF16)

        for step in range(N_CHIPS):
            @pl.when((j == step) & (i == n - 1))
            def _(step=step):
                sendbuf[step] = acc[...].astype(BF16)
                pair(step, 1 - c).start()

        for step in range(N_CHIPS - 1):
            @pl.when((j == step + 1) & (i == settle))
            def _(step=step):
                combine(step)
                ici(step).start()

        @pl.when((j == N_CHIPS - 1) & (i == n - 1))
        def _():
            combine(N_CHIPS - 1)
            local.start()
            staged = [pltpu.make_async_copy(slots_ref.at[f], pairbuf.at[f - 1], hsem.at[nw + 1 + f])
                      for f in range(1, N_CHIPS)]
            for step in range(N_CHIPS - 1):
                ici(step).wait_recv()
            for cp in staged:
                cp.start()
            for cp in staged:
                cp.wait()
            total = chipbuf[N_CHIPS - 1].astype(F32)
            for f in range(1, N_CHIPS):
                total = total + pairbuf[f - 1].astype(F32)
            acc[0:hr, :] = total
            exchange(nw, c).start()
            keep(nw).start()
            for w in range(nw + 1):
                exchange(w, 1 - c).wait_recv()
            for k in range(1, N_DEV):
                px, py, pc = _peer(pos, k)
                bcast(k, 4 * px + 2 * py + pc).wait_recv()
            for step in range(N_CHIPS - 1):
                ici(step).wait_send()
            for step in range(N_CHIPS):
                pair(step, 1 - c).wait_send()
            local.wait()
            for w in range(nw + 1):
                exchange(w, c).wait_send()
                keep(w).wait()
            for k in range(1, N_DEV):
                bcast(k, me).wait_send()
            own_small.wait()

    vmem = pl.BlockSpec(memory_space=pltpu.VMEM)
    shard_shapes = [(2 * sl.shape[1], sl.shape[2]) for sl in slots]
    out = pl.pallas_call(
        body, name="in_proj_bwd_w",
        grid_spec=pltpu.PrefetchScalarGridSpec(
            num_scalar_prefetch=1, grid=(N_CHIPS, n),
            in_specs=[pl.BlockSpec((TM_BLK, d), lambda j, i, order: (i, 0)),
                      pl.BlockSpec((TM_BLK, tn), lambda j, i, order: (jnp.where(order[j] == 0, i, 0), 0)),
                      pl.BlockSpec((TM_BLK, tn), lambda j, i, order: (jnp.where(order[j] == 0, 0, i),
                                                                     jnp.maximum(order[j] - 1, 0)))]
            + [vmem] * (nw + ns),
            out_specs=[ANY] * (nw + 3),
            scratch_shapes=[pltpu.VMEM((d, tn), F32), pltpu.VMEM((N_CHIPS, d, tn), BF16),
                            pltpu.VMEM((N_CHIPS, hr, tn), BF16), pltpu.VMEM((N_CHIPS, hr, tn), BF16)]
            + [pltpu.VMEM((r // 2, cc), F32) for r, cc in shard_shapes] + [pltpu.VMEM((SMALL_ROWS, SMALL_COLS), F32)]
            + [pltpu.SemaphoreType.DMA((N_CHIPS,)), pltpu.SemaphoreType.DMA((N_CHIPS,)),
               pltpu.SemaphoreType.DMA((N_CHIPS - 1,)), pltpu.SemaphoreType.DMA((N_CHIPS - 1,)),
               pltpu.SemaphoreType.DMA((1,)),
               pltpu.SemaphoreType.DMA((nw + 1,)), pltpu.SemaphoreType.DMA((nw + 1,)),
               pltpu.SemaphoreType.DMA((N_DEV,)), pltpu.SemaphoreType.DMA((N_DEV,)),
               pltpu.SemaphoreType.DMA((nw + 1 + N_CHIPS,))]),
        out_shape=[jax.ShapeDtypeStruct((N_CHIPS, hr, tn), BF16)]
        + [jax.ShapeDtypeStruct(sh, F32) for sh in shard_shapes + [(d, tn)]]
        + [jax.ShapeDtypeStruct((N_DEV, SMALL_ROWS, SMALL_COLS), F32)],
        compiler_params=_params("arbitrary", "arbitrary"),
    )(order, h, dqkv, dpb, *slots, *small)
    return [out[nw + 1]] + list(out[1:nw + 1]) + [out[nw + 2]]


LOSS_ROW = 6


def _adam_update(w, g, m, v):
    c1 = 1.0 / (1.0 - ADAM_B1 ** ADAM_STEP)
    c2 = 1.0 / (1.0 - ADAM_B2 ** ADAM_STEP)
    m2 = ADAM_B1 * m + (1.0 - ADAM_B1) * g
    v2 = ADAM_B2 * v + (1.0 - ADAM_B2) * (g * g)
    return -ADAM_LR * ((m2 * c1) / (jnp.sqrt(v2 * c2) + ADAM_EPS) + ADAM_WD * w), m2, v2


def _adamw_small(recv, params, moments_m, moments_v, out_shapes):
    k = recv.shape[0]
    n_par = len(params)
    cshard = params[3].shape[1]

    def body(*refs):
        r_ref = refs[0]
        ws, ms, vs = refs[1:1 + n_par], refs[1 + n_par:1 + 2 * n_par], refs[1 + 2 * n_par:1 + 3 * n_par]
        loss_ref = refs[1 + 3 * n_par]
        outs = refs[2 + 3 * n_par:]
        total = r_ref[0]
        for slot in range(1, k):
            total = total + r_ref[slot]
        loss_ref[...] = jnp.sum(total[LOSS_ROW:LOSS_ROW + 1, :], axis=1, keepdims=True)
        chip = 2 * lax.axis_index("x") + lax.axis_index("y")
        g_cw = jnp.zeros((3, cshard), F32)
        for sh in range(N_CHIPS):
            g_cw = g_cw + jnp.where(chip == sh, total[3:6, sh * cshard:(sh + 1) * cshard], 0.0)
        grads = [total[0:1, :], total[1:2, :], total[2:3, :ws[2].shape[1]], g_cw, total[8:16, :ws[4].shape[1]]]
        for p in range(n_par):
            delta, m2, v2 = _adam_update(ws[p][...], grads[p], ms[p][...], vs[p][...])
            for q, val in enumerate((grads[p], delta, m2, v2)):
                out = outs[4 * p + q]
                out[...] = val[0] if len(out.shape) == 1 else val

    shapes = [jax.ShapeDtypeStruct((1, 1), F32)]
    for shape in out_shapes:
        shapes += [jax.ShapeDtypeStruct(shape, F32)] * 4
    return pl.pallas_call(body, name="adamw_small", out_shape=shapes)(recv, *params, *moments_m, *moments_v)


def _adamw_group(ws, gs, ms, vs, name, steps):
    k = len(ws)

    def body(*refs):
        ins, outs = refs[:4 * k], refs[4 * k:]
        for j in range(k):
            gv = ins[k + j][...]
            outs[4 * j][...] = gv
            outs[4 * j + 1][...], outs[4 * j + 2][...], outs[4 * j + 3][...] = _adam_update(
                ins[j][...], gv, ins[2 * k + j][...], ins[3 * k + j][...])

    specs = [pl.BlockSpec((w.shape[0] // steps, w.shape[1]), lambda i: (i, 0)) for w in ws]
    out = pl.pallas_call(
        body, name=name, grid=(steps,),
        in_specs=specs * 4, out_specs=[spec for spec in specs for _ in range(4)],
        out_shape=[jax.ShapeDtypeStruct(w.shape, F32) for w in ws for _ in range(4)],
        compiler_params=_params("parallel"),
    )(*ws, *gs, *ms, *vs)
    return [tuple(out[4 * j:4 * j + 4]) for j in range(k)]


ANY = pl.BlockSpec(memory_space=pl.ANY)
N_CHIPS = 4
N_DEV = 8
OWNER_FLIPS = (3, 1, 2, 0)


def _position():
    return lax.axis_index("x"), lax.axis_index("y"), lax.axis_index("c")


def _gather_out_shapes(shards, kinds, cw3):
    full = [(a.shape[0], a.shape[1] * N_CHIPS) if k == "cols" else (a.shape[0] * N_CHIPS, a.shape[1])
            for a, k in zip(shards, kinds)]
    return [jax.ShapeDtypeStruct(f, BF16) for f in full] + [
        jax.ShapeDtypeStruct((N_CHIPS,) + cw3.shape, cw3.dtype)]


def _gather_sems(nw):
    return [pltpu.SemaphoreType.DMA((3, nw)), pltpu.SemaphoreType.DMA((3, nw)),
            pltpu.SemaphoreType.DMA((3, nw)), pltpu.SemaphoreType.DMA((3, nw)),
            pltpu.SemaphoreType.DMA((3,)), pltpu.SemaphoreType.DMA((3,)), pltpu.SemaphoreType.DMA((nw + 1,))]


def _gather_plan(kinds, srcs, dsts, cw, cw_all, send1, recv1, send2, recv2, ssend, srecv, lsem):
    nw = len(srcs)
    x, y, c = _position()
    mine = 2 * x + y
    chips = [(x, 1 - y), (1 - x, y), (1 - x, 1 - y)]

    def window(w, shard, half):
        r, cc = srcs[w].shape
        hr = r // 2
        if kinds[w] == "cols":
            rows = pl.ds(0, r) if half is None else pl.ds(half * hr, hr)
            return dsts[w].at[rows, pl.ds(shard * cc, cc)]
        rows = pl.ds(shard * r, r) if half is None else pl.ds(shard * r + half * hr, hr)
        return dsts[w].at[rows, :]

    def my_half(w):
        hr = srcs[w].shape[0] // 2
        return srcs[w].at[pl.ds(c * hr, hr), :]

    def local():
        return [pltpu.make_async_copy(srcs[w], window(w, mine, None), lsem.at[w]) for w in range(nw)] + [
            pltpu.make_async_copy(cw, cw_all.at[mine], lsem.at[nw])]

    def ici(k, w, shard):
        kx, ky = chips[k]
        return pltpu.make_async_remote_copy(
            src_ref=my_half(w), dst_ref=window(w, shard, c), send_sem=send1.at[k, w], recv_sem=recv1.at[k, w],
            device_id=(kx, ky, c), device_id_type=MESH)

    def d2d(k, w, shard, half):
        return pltpu.make_async_remote_copy(
            src_ref=window(w, shard, half), dst_ref=window(w, shard, half),
            send_sem=send2.at[k, w], recv_sem=recv2.at[k, w], device_id=(x, y, 1 - c), device_id_type=MESH)

    def small(k, shard):
        kx, ky = chips[k]
        return pltpu.make_async_remote_copy(
            src_ref=cw, dst_ref=cw_all.at[shard], send_sem=ssend.at[k], recv_sem=srecv.at[k],
            device_id=(kx, ky, c), device_id_type=MESH)

    def theirs(k):
        kx, ky = chips[k]
        return 2 * kx + ky

    def start():
        for cp in local():
            cp.start()
        for k in range(3):
            for w in range(nw):
                ici(k, w, mine).start()
            small(k, mine).start()

    def forward():
        for k in range(3):
            for w in range(nw):
                ici(k, w, theirs(k)).wait_recv()
                d2d(k, w, theirs(k), c).start()

    def finish():
        for k in range(3):
            for w in range(nw):
                d2d(k, w, theirs(k), 1 - c).wait_recv()
            small(k, theirs(k)).wait_recv()
        for k in range(3):
            for w in range(nw):
                ici(k, w, mine).wait_send()
                d2d(k, w, theirs(k), c).wait_send()
            small(k, mine).wait_send()
        for cp in local():
            cp.wait()

    return start, forward, finish


def _scatter_out_shapes(parts):
    return [jax.ShapeDtypeStruct((N_DEV, p.shape[1] // 2, p.shape[2]), p.dtype) for p in parts]


def _scatter_sems(nw):
    return [pltpu.SemaphoreType.DMA((N_DEV, nw)), pltpu.SemaphoreType.DMA((N_DEV, nw)), pltpu.SemaphoreType.DMA((nw,))]


def _peer(pos, k):
    x, y, c = pos
    return ((1 - x) if k & 4 else x, (1 - y) if k & 2 else y, (1 - c) if k & 1 else c)


def _scatter_plan(srcs, dsts, send, recv, lsem):
    nw = len(srcs)
    pos = _position()

    def piece(w, k):
        px, py, pc = _peer(pos, k)
        hr = srcs[w].shape[1] // 2
        return srcs[w].at[2 * px + py, pl.ds(pc * hr, hr), :]

    def remote(w, k):
        return pltpu.make_async_remote_copy(
            src_ref=piece(w, k), dst_ref=dsts[w].at[k], send_sem=send.at[k, w], recv_sem=recv.at[k, w],
            device_id=_peer(pos, k), device_id_type=MESH)

    def local(w):
        return pltpu.make_async_copy(piece(w, 0), dsts[w].at[0], lsem.at[w])

    def start():
        for w in range(nw):
            local(w).start()
        for k in range(1, N_DEV):
            for w in range(nw):
                remote(w, k).start()

    def finish():
        for k in range(1, N_DEV):
            for w in range(nw):
                remote(w, k).wait_recv()
        for k in range(1, N_DEV):
            for w in range(nw):
                remote(w, k).wait_send()
        for w in range(nw):
            local(w).wait()

    return start, finish


def _pack_small(sm, norm_g, fin_g, conv_b, conv_w, loss_vec, rel):
    sm[...] = jnp.zeros_like(sm)
    for row, ref in ((0, norm_g), (1, fin_g), (2, conv_b), (LOSS_ROW, loss_vec)):
        sm[row:row + 1, 0:ref.shape[1]] = ref[...]
    sm[3:6, 0:conv_w.shape[1]] = conv_w[0:3, :]
    sm[8:8 + HEADS, 0:rel.shape[1]] = rel[...]


def kernel(x, norm_g, w_in, rel_bias, w_att_out, conv_w, conv_b, w_conv_out, w_out, final_norm_g, loss_target, m_norm_g, m_w_in, m_rel_bias, m_w_att_out, m_conv_w, m_conv_b, m_w_conv_out, m_w_out, m_final_norm_g, v_norm_g, v_w_in, v_rel_bias, v_w_att_out, v_conv_w, v_conv_b, v_w_conv_out, v_w_out, v_final_norm_g):
    xs, tgt = x[0], loss_target[0]
    cshard = conv_w.shape[2]
    chip = 2 * lax.axis_index("x") + lax.axis_index("y")

    shards = [w_in[0], w_att_out[0], w_conv_out[0], w_out[0]]
    cw3 = conv_w[0]
    flips = jnp.arange(N_CHIPS, dtype=jnp.int32)
    own_first = jnp.bitwise_xor(chip, flips)
    own_last = jnp.bitwise_xor(chip, jnp.asarray(OWNER_FLIPS, jnp.int32))

    proj, h, wb_in = _in_proj_gather(xs, norm_g, shards[0], own_first)
    diag = jnp.take(rel_bias[0], _diag_rel_index(), axis=1)
    att, lse, wb_att, wb_conv, wb_out, cw_all = _attn_fwd(diag, proj, shards[1:], ["cols", "cols", "rows"], cw3)
    conv_w_full = jnp.transpose(cw_all, (1, 0, 2)).reshape(cw3.shape[0], N_CHIPS * cshard)
    (dpb, d_att, dx2, g_att_p, g_conv_p, g_out_p, loss_vec, g_fin, g_cb, g_cw) = _mixer_mid(
        att, proj, xs, tgt, wb_att, wb_conv, wb_out, conv_w_full, conv_b,
        final_norm_g[None, :])
    dqkv, g_rel, r_att, r_conv, r_out = _attn_bwd(diag, proj, d_att, att, lse, [g_att_p, g_conv_p, g_out_p])
    grad_x, g_norm = _in_proj_bwd_x(dqkv, dpb, wb_in, xs, dx2, norm_g)
    gw_in, gw_att, gw_conv, gw_out, r_small = _in_proj_bwd_w(
        h, dqkv, dpb, own_last, [r_att, r_conv, r_out], [g_norm, g_fin, g_cb, g_cw, loss_vec, g_rel])

    small_out = _adamw_small(
        r_small,
        [norm_g, final_norm_g[None, :], conv_b, conv_w[0], rel_bias[0]],
        [m_norm_g, m_final_norm_g[None, :], m_conv_b, m_conv_w[0], m_rel_bias[0]],
        [v_norm_g, v_final_norm_g[None, :], v_conv_b, v_conv_w[0], v_rel_bias[0]],
        [norm_g.shape, final_norm_g.shape, conv_b.shape, conv_w[0].shape, rel_bias[0].shape])
    loss = small_out[0][0, 0]
    small_names = ["norm_g", "final_norm_g", "conv_b", "conv_w", "rel_bias"]
    fix = {"norm_g": lambda a: a, "final_norm_g": lambda a: a, "conv_b": lambda a: a,
           "conv_w": lambda a: a[None], "rel_bias": lambda a: a[None]}
    small_res = {name: [fix[name](small_out[1 + 4 * p + q]) for q in range(4)] for p, name in enumerate(small_names)}

    names = ["w_in", "w_att_out", "w_conv_out", "w_out"]
    group = _adamw_group([w_in[0], w_att_out[0], w_conv_out[0], w_out[0]], [gw_in, gw_att, gw_conv, gw_out],
                         [m_w_in[0], m_w_att_out[0], m_w_conv_out[0], m_w_out[0]],
                         [v_w_in[0], v_w_att_out[0], v_w_conv_out[0], v_w_out[0]], "adamw_matrices", ADAMW_STEPS)
    big = dict(zip(names, group))
    big = {name: tuple(a[None] for a in four) for name, four in big.items()}

    order = ["norm_g", "w_in", "rel_bias", "w_att_out", "conv_w", "conv_b", "w_conv_out", "w_out", "final_norm_g"]
    outs = [loss, grad_x[None]]
    for which in range(4):
        for name in order:
            outs.append(big[name][which] if name in big else small_res[name][which])
    return tuple(outs)
```

```python
import numpy as np
import jax
import jax.numpy as jnp
from jax import lax
from jax.experimental import pallas as pl
from jax.experimental.pallas import tpu as pltpu

F32 = jnp.float32
BF16 = jnp.bfloat16
MESH = pl.DeviceIdType.MESH

CHUNK = 64
N_LEFT = 8
HEADS = 8
HEAD_DIM = 64
D_ATT = HEADS * HEAD_DIM
MAX_REL = 128
N_REL = 2 * MAX_REL + 1
EPS = 1e-6
NEG_BIG = -1e30
ADAM_LR, ADAM_B1, ADAM_B2, ADAM_EPS, ADAM_WD, ADAM_STEP = 0.001, 0.9, 0.999, 1e-08, 0.01, 10

LANES = 128
SUBLANES = 8
VMEM_LIMIT = 56 * 1024 * 1024

QB = 2 * CHUNK
KW = N_LEFT * CHUNK + QB
DIAG = KW + QB
TQ = N_LEFT * CHUNK
TM_MID = 256
ADAMW_STEPS = 8
TM_MM = 512
TM_BLK = 1024
SMALL_ROWS, SMALL_COLS = 16, 1024


def _params(*sem):
    return pltpu.CompilerParams(dimension_semantics=sem, vmem_limit_bytes=VMEM_LIMIT)


def _nt(a, b):
    return lax.dot_general(a, b, (((1,), (1,)), ((), ())), preferred_element_type=F32)


def _tn(a, b):
    return lax.dot_general(a, b, (((0,), (0,)), ((), ())), preferred_element_type=F32)


def _nn(a, b):
    return jnp.dot(a, b, preferred_element_type=F32)


def _diag_rel_index():
    d = np.arange(DIAG)
    diff = np.where(d < KW, d, d - DIAG)
    rel = N_LEFT * CHUNK - diff
    return np.clip(rel, -MAX_REL, MAX_REL) + MAX_REL


def _build_bias(diag_ref, bias_scr):
    r = lax.broadcasted_iota(jnp.int32, (QB, KW), 0) // CHUNK
    s = lax.broadcasted_iota(jnp.int32, (QB, KW), 1) // CHUNK
    allowed = (s >= r) & (s <= r + N_LEFT)
    for h in range(HEADS):
        row = jnp.broadcast_to(diag_ref[h:h + 1, :], (QB, DIAG))
        t = pltpu.roll(row, 0, 1, stride=1, stride_axis=0)
        bias_scr[h // 2, (h % 2) * QB:(h % 2 + 1) * QB, :] = jnp.where(allowed, t[:, :KW], NEG_BIG)


def _stack_heads(a, lane_hi):
    zero = jnp.zeros_like(a)
    return jnp.concatenate([jnp.where(lane_hi, zero, a), jnp.where(lane_hi, a, zero)], axis=0)


def _in_proj_gather(x, g, shard, order):
    s, d = x.shape
    tn = shard.shape[1]
    n = s // TM_BLK
    hr = d // 2

    def body(order_ref, x_ref, g_ref, shard32_ref, proj_ref, h_ref, wfull_ref, shard_ref, hbuf, wbuf,
             send1, recv1, send2, recv2, lsem):
        del order_ref
        j, i = pl.program_id(0), pl.program_id(1)
        x, y, c = _position()
        mine = 2 * x + y
        chips = [(x, 1 - y), (1 - x, y), (1 - x, 1 - y)]

        def theirs(k):
            return 2 * chips[k][0] + chips[k][1]

        def half_rows(half):
            return pl.ds(half * hr, hr)

        def landing(k, shard_index, half):
            if k < 2:
                return wbuf.at[k, half_rows(half), :]
            return wfull_ref.at[shard_index, half_rows(half), :]

        def ici(k, shard_index):
            return pltpu.make_async_remote_copy(
                src_ref=shard_ref.at[half_rows(c), :], dst_ref=landing(k, shard_index, c),
                send_sem=send1.at[k], recv_sem=recv1.at[k], device_id=(*chips[k], c), device_id_type=MESH)

        def d2d(k, half):
            return pltpu.make_async_remote_copy(
                src_ref=wbuf.at[k % 2, half_rows(half), :], dst_ref=landing(k, theirs(k), half),
                send_sem=send2.at[k], recv_sem=recv2.at[k], device_id=(x, y, 1 - c), device_id_type=MESH)

        def load(k, half, sem):
            return pltpu.make_async_copy(wfull_ref.at[theirs(k), half_rows(half), :],
                                         wbuf.at[k % 2, half_rows(half), :], lsem.at[sem])

        def keep(k):
            return pltpu.make_async_copy(wbuf.at[k], wfull_ref.at[theirs(k)], lsem.at[3 + k])

        own = pltpu.make_async_copy(shard_ref, wfull_ref.at[mine], lsem.at[0])

        def put_proj(block):
            for grp in range(tn // D_ATT):
                proj_ref[grp] = block[:, grp * D_ATT:(grp + 1) * D_ATT].astype(BF16)

        @pl.when((j == 0) & (i == 0))
        def _():
            shard_ref[...] = shard32_ref[...].astype(BF16)
            own.start()
            ici(0, mine).start()
            ici(1, mine).start()

        for k in range(2):
            first = n - 1 if k == 0 else min(n // 2, n - 1)

            @pl.when((j == k) & (i == first))
            def _(k=k):
                if k == 0:
                    ici(0, mine).wait_send()
                    ici(1, mine).wait_send()
                    ici(2, mine).start()
                ici(k, theirs(k)).wait_recv()
                d2d(k, c).start()

            then = (1, 0) if k == 0 else (1, min(first + 1, n - 1))

            @pl.when((j == then[0]) & (i == then[1]))
            def _(k=k):
                d2d(k, 1 - c).wait_recv()
                keep(k).start()

        first = min(n // 2, n - 1)

        @pl.when((j == 2) & (i == first))
        def _():
            d2d(0, c).wait_send()
            keep(0).wait()
            ici(2, theirs(2)).wait_recv()
            load(2, c, 1).start()

        @pl.when((j == 2) & (i == min(first + 1, n - 1)))
        def _():
            load(2, c, 1).wait()
            d2d(2, c).start()

        @pl.when((j == 2) & (i == min(first + 2, n - 1)))
        def _():
            d2d(2, 1 - c).wait_recv()
            load(2, 1 - c, 2).start()

        @pl.when((j == 3) & (i == 0))
        def _():
            load(2, 1 - c, 2).wait()

        @pl.when(j == 0)
        def _():
            xv = x_ref[...]
            r = lax.rsqrt(jnp.mean(xv * xv, axis=-1, keepdims=True) + EPS)
            hv = ((xv * r) * g_ref[...]).astype(BF16)
            hbuf[i] = hv
            h_ref[...] = hv
            put_proj(_nn(hv, shard_ref[...]))

        for k in range(3):
            @pl.when(j == k + 1)
            def _(k=k):
                put_proj(_nn(hbuf[i], wbuf[k % 2]))

        @pl.when((j == 3) & (i == n - 1))
        def _():
            ici(2, mine).wait_send()
            d2d(1, c).wait_send()
            d2d(2, c).wait_send()
            own.wait()
            keep(1).wait()

    return pl.pallas_call(
        body, name="in_proj_gather",
        grid_spec=pltpu.PrefetchScalarGridSpec(
            num_scalar_prefetch=1, grid=(N_CHIPS, n),
            in_specs=[pl.BlockSpec((TM_BLK, d), lambda j, i, order: (jnp.where(j == 0, i, n - 1), 0)),
                      pl.BlockSpec((1, d), lambda j, i, order: (0, 0)), pl.BlockSpec(memory_space=pltpu.VMEM)],
            out_specs=[pl.BlockSpec((tn // D_ATT, TM_BLK, D_ATT), lambda j, i, order: (order[j], i, 0)),
                       pl.BlockSpec((TM_BLK, d), lambda j, i, order: (jnp.where(j == 0, i, n - 1), 0)), ANY],
            scratch_shapes=[pltpu.VMEM((d, tn), BF16), pltpu.VMEM((n, TM_BLK, d), BF16), pltpu.VMEM((2, d, tn), BF16),
                            pltpu.SemaphoreType.DMA((3,)), pltpu.SemaphoreType.DMA((3,)),
                            pltpu.SemaphoreType.DMA((3,)), pltpu.SemaphoreType.DMA((3,)), pltpu.SemaphoreType.DMA((5,))]),
        out_shape=[jax.ShapeDtypeStruct((N_CHIPS * tn // D_ATT, s, D_ATT), BF16), jax.ShapeDtypeStruct((s, d), BF16),
                   jax.ShapeDtypeStruct((N_CHIPS, d, tn), BF16)],
        compiler_params=_params("arbitrary", "arbitrary"),
    )(order, x, g, shard)


def _attn_fwd(diag, proj, shards, kinds, cw3):
    s = proj.shape[1]
    n = s // TQ
    nw = len(shards)
    scale = HEAD_DIM ** -0.5

    def body(*refs):
        diag_ref, q_ref, kp_ref, kc_ref, vp_ref, vc_ref = refs[:6]
        srcs, cw = refs[6:6 + nw], refs[6 + nw]
        o_ref, lse_ref = refs[7 + nw:9 + nw]
        dsts, cw_all = refs[9 + nw:9 + 2 * nw], refs[9 + 2 * nw]
        bias_scr = refs[10 + 2 * nw]
        casts = refs[11 + 2 * nw:11 + 3 * nw]
        start, forward, finish = _gather_plan(kinds, casts, dsts, cw, cw_all, *refs[11 + 3 * nw:])
        i = pl.program_id(0)

        @pl.when(i == 0)
        def _():
            for w in range(nw):
                casts[w][...] = srcs[w][...].astype(BF16)
            start()
            _build_bias(diag_ref, bias_scr)

        @pl.when(i == n // 2)
        def _():
            forward()

        @pl.when(i == n - 1)
        def _():
            finish()

        lane_hi = lax.broadcasted_iota(jnp.int32, (QB, LANES), 1) >= HEAD_DIM

        def block(b, first_tile):
            r0, n_prev = b * QB, TQ - b * QB
            n_cur = KW - n_prev
            pairs = range(HEADS // 2)
            lanes_of = [slice(LANES * p, LANES * (p + 1)) for p in pairs]
            scores = []
            for p in pairs:
                lanes = lanes_of[p]
                q2 = _stack_heads(q_ref[r0:r0 + QB, lanes] * scale, lane_hi)
                s_cur = _nt(q2, kc_ref[0:n_cur, lanes]) + bias_scr[p, :, n_prev:KW]
                if first_tile:
                    scores.append(s_cur)
                else:
                    scores.append(jnp.concatenate(
                        [_nt(q2, kp_ref[r0:TQ, lanes]) + bias_scr[p, :, 0:n_prev], s_cur], axis=1))
            probs = []
            for p in pairs:
                sc = scores[p]
                m = jnp.max(sc, axis=1, keepdims=True)
                pe = jnp.exp(sc - m)
                l = jnp.sum(pe, axis=1, keepdims=True)
                probs.append((pe.astype(BF16), l, m))
            for p in pairs:
                lanes = lanes_of[p]
                pb, l, m = probs[p]
                if first_tile:
                    o2 = _nn(pb, vc_ref[0:n_cur, lanes]) / l
                else:
                    o2 = (_nn(pb[:, 0:n_prev], vp_ref[r0:TQ, lanes]) + _nn(pb[:, n_prev:KW], vc_ref[0:n_cur, lanes])) / l
                lse2 = m + jnp.log(l)
                lse_ref[r0:r0 + QB, 2 * p:2 * p + 1] = lse2[0:QB, :]
                lse_ref[r0:r0 + QB, 2 * p + 1:2 * p + 2] = lse2[QB:2 * QB, :]
                o_ref[r0:r0 + QB, lanes] = jnp.where(lane_hi, o2[QB:2 * QB, :], o2[0:QB, :]).astype(BF16)

        @pl.when(i == 0)
        def _():
            for b in range(TQ // QB):
                block(b, True)

        @pl.when(i > 0)
        def _():
            for b in range(TQ // QB):
                block(b, False)

    blk = lambda grp, prev: pl.BlockSpec(
        (None, TQ, D_ATT), (lambda i: (grp, jnp.maximum(i - 1, 0), 0)) if prev else (lambda i: (grp, i, 0)))
    vmem = pl.BlockSpec(memory_space=pltpu.VMEM)
    return pl.pallas_call(
        body, name="attn_fwd", grid=(n,),
        in_specs=[pl.BlockSpec((HEADS, DIAG), lambda i: (0, 0)),
                  blk(0, False), blk(1, True), blk(1, False), blk(2, True), blk(2, False)] + [vmem] * (nw + 1),
        out_specs=[pl.BlockSpec((TQ, D_ATT), lambda i: (i, 0)), pl.BlockSpec((TQ, HEADS), lambda i: (i, 0))]
        + [ANY] * (nw + 1),
        out_shape=[jax.ShapeDtypeStruct((s, D_ATT), BF16), jax.ShapeDtypeStruct((s, HEADS), F32)]
        + _gather_out_shapes(shards, kinds, cw3),
        scratch_shapes=[pltpu.VMEM((HEADS // 2, 2 * QB, KW), F32)] + [pltpu.VMEM(a.shape, BF16) for a in shards]
        + _gather_sems(nw),
        compiler_params=_params("arbitrary"),
    )(diag, proj, proj, proj, proj, proj, *shards, cw3)


def _attn_bwd(diag, proj, d_att, att, lse, parts):
    s = proj.shape[1]
    n = s // TQ
    npart = len(parts)
    scale = HEAD_DIM ** -0.5
    rel_pad = 3 * LANES

    def body(*refs):
        diag_ref, q_ref, kp_ref, kc_ref, vp_ref, vc_ref, do_ref, o_ref, lse_ref = refs[:9]
        part_refs = refs[9:9 + npart]
        dqkv_ref, dbias_ref = refs[9 + npart:11 + npart]
        slot_refs = refs[11 + npart:11 + 2 * npart]
        bias_scr, dbias_acc, dk_acc, dv_acc, dq_scr = refs[11 + 2 * npart:16 + 2 * npart]
        start, finish = _scatter_plan(part_refs, slot_refs, *refs[16 + 2 * npart:])
        i = pl.program_id(0)
        cur, prv = i % 2, 1 - i % 2

        @pl.when(i == 0)
        def _():
            start()
            _build_bias(diag_ref, bias_scr)
            dbias_acc[...] = jnp.zeros_like(dbias_acc)
            dk_acc[...] = jnp.zeros_like(dk_acc)
            dv_acc[...] = jnp.zeros_like(dv_acc)

        @pl.when(i > 0)
        def _():
            dqkv_ref[:, 0:D_ATT] = dq_scr[...]
            dk_acc[cur] = jnp.zeros((TQ, D_ATT), F32)
            dv_acc[cur] = jnp.zeros((TQ, D_ATT), F32)

        lane_hi = lax.broadcasted_iota(jnp.int32, (QB, LANES), 1) >= HEAD_DIM
        col = lax.broadcasted_iota(jnp.int32, (2 * QB, KW), 1)

        def make_block(first_tile):
            def block(b):
                r0, n_prev = b * QB, TQ - b * QB
                n_cur = KW - n_prev
                for p in range(HEADS // 2):
                    lanes = slice(LANES * p, LANES * (p + 1))
                    q2 = _stack_heads(q_ref[r0:r0 + QB, lanes] * scale, lane_hi)
                    kw = jnp.concatenate([kp_ref[r0:TQ, lanes], kc_ref[0:n_cur, lanes]], axis=0)
                    vw = jnp.concatenate([vp_ref[r0:TQ, lanes], vc_ref[0:n_cur, lanes]], axis=0)
                    dop = do_ref[r0:r0 + QB, lanes]
                    do2 = _stack_heads(dop, lane_hi)
                    prod = dop.astype(F32) * o_ref[r0:r0 + QB, lanes].astype(F32)
                    delta2 = jnp.concatenate(
                        [jnp.sum(jnp.where(lane_hi, 0.0, prod), axis=1, keepdims=True),
                         jnp.sum(jnp.where(lane_hi, prod, 0.0), axis=1, keepdims=True)], axis=0)
                    lse2 = jnp.concatenate([lse_ref[r0:r0 + QB, 2 * p:2 * p + 1],
                                            lse_ref[r0:r0 + QB, 2 * p + 1:2 * p + 2]], axis=0)
                    sc = _nt(q2, kw) + bias_scr[p]
                    if first_tile:
                        sc = jnp.where(col >= TQ - r0, sc, NEG_BIG)
                    pr = jnp.exp(sc - lse2)
                    ds = pr * (_nt(do2, vw) - delta2)
                    dbias_acc[p] += ds
                    dsb = ds.astype(BF16)
                    dv_w = _tn(pr.astype(BF16), do2)
                    dk_w = _tn(dsb, q2)
                    dv_acc[prv, r0:TQ, lanes] += dv_w[0:n_prev, :]
                    dv_acc[cur, 0:n_cur, lanes] += dv_w[n_prev:KW, :]
                    dk_acc[prv, r0:TQ, lanes] += dk_w[0:n_prev, :]
                    dk_acc[cur, 0:n_cur, lanes] += dk_w[n_prev:KW, :]
                    dq2 = _nn(dsb, kw)
                    dq = jnp.where(lane_hi, dq2[QB:2 * QB, :], dq2[0:QB, :]) * scale
                    dq_scr[r0:r0 + QB, lanes] = dq.astype(BF16)
            return block

        @pl.when(i == 0)
        def _():
            for b in range(TQ // QB):
                make_block(True)(b)

        @pl.when((i > 0) & (i < n))
        def _():
            for b in range(TQ // QB):
                make_block(False)(b)

        @pl.when(i > 0)
        def _():
            dqkv_ref[:, D_ATT:2 * D_ATT] = dk_acc[prv].astype(BF16)
            dqkv_ref[:, 2 * D_ATT:3 * D_ATT] = dv_acc[prv].astype(BF16)

        @pl.when(i == n)
        def _():
            d_iota = lax.broadcasted_iota(jnp.int32, (DIAG, rel_pad), 0)
            n_iota = lax.broadcasted_iota(jnp.int32, (DIAG, rel_pad), 1)
            diff = jnp.where(d_iota < KW, d_iota, d_iota - DIAG)
            idx = jnp.clip(N_LEFT * CHUNK - diff, -MAX_REL, MAX_REL) + MAX_REL
            onehot = (idx == n_iota).astype(F32)
            rows = []
            for hd in range(HEADS):
                acc = dbias_acc[hd // 2, (hd % 2) * QB:(hd % 2 + 1) * QB, :]
                a = jnp.concatenate([acc, jnp.zeros((QB, DIAG - KW), F32)], axis=1)
                g8 = a[0:SUBLANES, :]
                for blk in range(1, QB // SUBLANES):
                    g8 = g8 + pltpu.roll(a[blk * SUBLANES:(blk + 1) * SUBLANES, :], DIAG - blk * SUBLANES, 1)
                g1 = g8[0:1, :]
                for r in range(1, SUBLANES):
                    g1 = g1 + pltpu.roll(g8[r:r + 1, :], DIAG - r, 1)
                rows.append(g1)
            g = jnp.concatenate(rows, axis=0)
            dbias_ref[...] = jnp.dot(g, onehot, preferred_element_type=F32, precision=lax.Precision.HIGHEST)
            finish()

    last = n - 1
    cur = lambda grp: pl.BlockSpec((None, TQ, D_ATT), lambda i: (grp, jnp.minimum(i, last), 0))
    prev = lambda grp: pl.BlockSpec((None, TQ, D_ATT), lambda i: (grp, jnp.maximum(jnp.minimum(i, last) - 1, 0), 0))
    tile = pl.BlockSpec((TQ, D_ATT), lambda i: (jnp.minimum(i, last), 0))
    return pl.pallas_call(
        body, name="attn_bwd", grid=(n + 1,),
        in_specs=[pl.BlockSpec((HEADS, DIAG), lambda i: (0, 0)),
                  cur(0), prev(1), cur(1), prev(2), cur(2), tile, tile,
                  pl.BlockSpec((TQ, HEADS), lambda i: (jnp.minimum(i, last), 0))] + [ANY] * npart,
        out_specs=[pl.BlockSpec((TQ, 3 * D_ATT), lambda i: (jnp.maximum(i - 1, 0), 0)),
                   pl.BlockSpec((HEADS, rel_pad), lambda i: (0, 0))] + [ANY] * npart,
        out_shape=[jax.ShapeDtypeStruct((s, 3 * D_ATT), BF16), jax.ShapeDtypeStruct((HEADS, rel_pad), F32)]
        + _scatter_out_shapes(parts),
        scratch_shapes=[pltpu.VMEM((HEADS // 2, 2 * QB, KW), F32), pltpu.VMEM((HEADS // 2, 2 * QB, KW), F32),
                        pltpu.VMEM((2, TQ, D_ATT), F32), pltpu.VMEM((2, TQ, D_ATT), F32),
                        pltpu.VMEM((TQ, D_ATT), BF16)] + _scatter_sems(npart),
        compiler_params=_params("arbitrary"),
    )(diag, proj, proj, proj, proj, proj, d_att, att, lse, *parts)


def _shift_down(a, k, halo):
    rolled = pltpu.roll(a, k, 0)
    row = lax.broadcasted_iota(jnp.int32, halo.shape, 0)
    first = jnp.where(row < k, pltpu.roll(halo, k, 0), rolled[0:SUBLANES, :])
    return jnp.concatenate([first, rolled[SUBLANES:, :]], axis=0)


def _shift_up(a, k, nxt):
    tm = a.shape[0]
    rolled = pltpu.roll(a, tm - k, 0)
    row = lax.broadcasted_iota(jnp.int32, nxt.shape, 0)
    last = jnp.where(row >= SUBLANES - k, pltpu.roll(nxt, SUBLANES - k, 0), rolled[tm - SUBLANES:, :])
    return jnp.concatenate([rolled[:tm - SUBLANES, :], last], axis=0)


def _sigmoid(v):
    return 0.5 * jnp.tanh(0.5 * v) + 0.5


def _mixer_mid(att, proj, x, tgt, w_att, w_conv, w_out, conv_w, conv_b, fin_g):
    s, d = x.shape
    dc = D_ATT
    n = s // TM_MID
    tm = TM_MID
    n_shards = 4

    def body(att_ref, za_ref, gb_ref, gc_ref, u_ref, zc_ref, hgc_ref, hu_ref, gatt_ref, gconv_ref, x_ref, t_ref,
             watt_ref, wconv_ref, wout_ref, cw_ref, cb_ref, fg_ref,
             dpb_ref, do_ref, dx2_ref, gatt_o, gconv_o, gout_o, loss_o, gfn_o, gcb_o, gcw_o,
             acc_att, acc_conv, acc_out, carry, watt_t_ref, wconv_t_ref, wout_t_ref):
        i = pl.program_id(0)
        tile = n - 1 - i

        @pl.when(i == 0)
        def _():
            acc_att[...] = jnp.zeros_like(acc_att)
            acc_conv[...] = jnp.zeros_like(acc_conv)
            acc_out[...] = jnp.zeros_like(acc_out)
            carry[...] = jnp.zeros_like(carry)
            loss_o[...] = jnp.zeros_like(loss_o)
            gfn_o[...] = jnp.zeros_like(gfn_o)
            gcb_o[...] = jnp.zeros_like(gcb_o)
            gcw_o[...] = jnp.zeros_like(gcw_o)
            watt_t_ref[...] = watt_ref[...].T
            wconv_t_ref[...] = wconv_ref[...].T
            wout_t_ref[...] = wout_ref[...].T

        halves = [slice(hh * (tm // 2), (hh + 1) * (tm // 2)) for hh in range(2)]
        both = lambda fn: [fn(rows) for rows in halves]
        f32 = lambda ref, rows: ref[rows, :].astype(F32)

        gc = gc_ref[...].astype(F32)
        u = u_ref[...].astype(F32)
        cu = gc * u
        halo = jnp.where(tile > 0, hgc_ref[...].astype(F32) * hu_ref[...].astype(F32), 0.0)
        cu1 = _shift_down(cu, 1, halo)
        cu2 = _shift_down(cu, 2, halo)
        w0, w1, w2 = cw_ref[0:1, :], cw_ref[1:2, :], cw_ref[2:3, :]
        fg = fg_ref[...]

        def stage_a(rows):
            att_v, za, zc, gb = f32(att_ref, rows), f32(za_ref, rows), f32(zc_ref, rows), f32(gb_ref, rows)
            sa = _sigmoid(za)
            silu_a = za * sa
            vconv = w0 * cu2[rows, :] + w1 * cu1[rows, :] + w2 * cu[rows, :] + cb_ref[...]
            sc = _sigmoid(zc)
            silu_c = zc * sc
            return dict(att_v=att_v, za=za, zc=zc, gb=gb, sa=sa, silu_a=silu_a, vconv=vconv, sc=sc, silu_c=silu_c,
                        a_b=(att_v * silu_a).astype(BF16), c_b=(gb * vconv * silu_c).astype(BF16))

        st = both(stage_a)
        for t in st:
            t["y_att"] = _nn(t["a_b"], watt_ref[...])
            t["y_conv"] = _nn(t["c_b"], wconv_ref[...])
        for t, rows in zip(st, halves):
            gpair = lambda ref: jnp.concatenate([ref[0, rows, :], ref[1, rows, :]], axis=1).astype(F32)
            t["ga"] = _sigmoid(gpair(gatt_ref))
            t["gv"] = _sigmoid(gpair(gconv_ref))
            t["m_b"] = (t["ga"] * t["y_att"] + t["gv"] * t["y_conv"]).astype(BF16)
        for t in st:
            t["mo"] = _nn(t["m_b"], wout_ref[...])
        for t, rows in zip(st, halves):
            x2 = x_ref[rows, :] + t["mo"]
            r2 = lax.rsqrt(jnp.mean(x2 * x2, axis=-1, keepdims=True) + EPS)
            x2n = x2 * r2
            err = x2n * fg - t_ref[rows, :]
            loss_o[...] += jnp.sum(err * err, axis=0, keepdims=True) * (0.5 / d)
            dy = err * (1.0 / d)
            gfn_o[...] += jnp.sum(dy * x2n, axis=0, keepdims=True)
            dyn = dy * fg
            dx2 = r2 * (dyn - x2n * jnp.mean(dyn * x2n, axis=-1, keepdims=True))
            dx2_ref[rows, :] = dx2
            t["dx2_b"] = dx2.astype(BF16)
        for t in st:
            t["dm"] = _nn(t["dx2_b"], wout_t_ref[...])
        whole = lambda key: jnp.concatenate([st[0][key], st[1][key]], axis=0)
        acc_out[...] += _tn(whole("m_b"), whole("dx2_b"))
        for t, rows in zip(st, halves):
            dy_att = t["dm"] * t["ga"]
            dy_conv = t["dm"] * t["gv"]
            dpb_ref[rows, 5 * dc:5 * dc + d] = (dy_att * t["y_att"] * (1.0 - t["ga"])).astype(BF16)
            dpb_ref[rows, 5 * dc + d:5 * dc + 2 * d] = (dy_conv * t["y_conv"] * (1.0 - t["gv"])).astype(BF16)
            t["dya_b"] = dy_att.astype(BF16)
            t["dyc_b"] = dy_conv.astype(BF16)
        for t in st:
            t["da_in"] = _nn(t["dya_b"], watt_t_ref[...])
            t["dc_in"] = _nn(t["dyc_b"], wconv_t_ref[...])
        acc_att[...] += _tn(whole("a_b"), whole("dya_b"))
        acc_conv[...] += _tn(whole("c_b"), whole("dyc_b"))
        for t, rows in zip(st, halves):
            sa, za, sc, zc = t["sa"], t["za"], t["sc"], t["zc"]
            do_ref[rows, :] = (t["da_in"] * t["silu_a"]).astype(BF16)
            dpb_ref[rows, 0:dc] = (t["da_in"] * t["att_v"] * (sa * (1.0 + za * (1.0 - sa)))).astype(BF16)
            dpb_ref[rows, dc:2 * dc] = (t["dc_in"] * t["vconv"] * t["silu_c"]).astype(BF16)
            dgs = t["dc_in"] * t["gb"]
            t["dvc"] = dgs * t["silu_c"]
            dpb_ref[rows, 4 * dc:5 * dc] = (dgs * t["vconv"] * (sc * (1.0 + zc * (1.0 - sc)))).astype(BF16)
        dvc = whole("dvc")
        gcb_o[...] += jnp.sum(dvc, axis=0, keepdims=True)
        gcw_o[0:1, :] += jnp.sum(dvc * cu2, axis=0, keepdims=True)
        gcw_o[1:2, :] += jnp.sum(dvc * cu1, axis=0, keepdims=True)
        gcw_o[2:3, :] += jnp.sum(dvc * cu, axis=0, keepdims=True)
        nxt = carry[...]
        dcu = w2 * dvc + w1 * _shift_up(dvc, 1, nxt) + w0 * _shift_up(dvc, 2, nxt)
        carry[...] = dvc[0:SUBLANES, :]
        dpb_ref[:, 2 * dc:3 * dc] = (dcu * u).astype(BF16)
        dpb_ref[:, 3 * dc:4 * dc] = (dcu * gc).astype(BF16)

        @pl.when(i == n - 1)
        def _():
            for j in range(n_shards):
                gatt_o[j] = acc_att[:, j * (d // n_shards):(j + 1) * (d // n_shards)].astype(BF16)
                gconv_o[j] = acc_conv[:, j * (d // n_shards):(j + 1) * (d // n_shards)].astype(BF16)
                gout_o[j] = acc_out[j * (d // n_shards):(j + 1) * (d // n_shards), :].astype(BF16)

    rev = lambda width, col_blk: pl.BlockSpec((tm, width), lambda i: (n - 1 - i, col_blk))
    grp = lambda g: pl.BlockSpec((None, tm, dc), lambda i: (g, n - 1 - i, 0))
    grp2 = lambda g2: pl.BlockSpec((2, tm, dc), lambda i: (g2, n - 1 - i, 0))
    halo_spec = lambda g: pl.BlockSpec(
        (None, SUBLANES, dc), lambda i: (g, jnp.maximum((n - 1 - i) * (tm // SUBLANES) - 1, 0), 0))
    const = lambda shape: pl.BlockSpec(shape, lambda i: tuple(0 for _ in shape), pipeline_mode=pl.Buffered(1))
    q4 = d // n_shards
    return pl.pallas_call(
        body, name="mixer_mid", grid=(n,),
        in_specs=[rev(dc, 0), grp(3), grp(4), grp(5), grp(6), grp(7),
                  halo_spec(5), halo_spec(6), grp2(4), grp2(5), rev(d, 0), rev(d, 0),
                  const((dc, d)), const((dc, d)), const((d, d)),
                  const(conv_w.shape), const((1, dc)), const((1, d))],
        out_specs=[rev(5 * dc + 2 * d, 0), rev(dc, 0), rev(d, 0),
                   const((n_shards, dc, q4)), const((n_shards, dc, q4)), const((n_shards, q4, d)),
                   const((1, d)), const((1, d)), const((1, dc)), const((SUBLANES, dc))],
        out_shape=[jax.ShapeDtypeStruct((s, 5 * dc + 2 * d), BF16), jax.ShapeDtypeStruct((s, dc), BF16),
                   jax.ShapeDtypeStruct((s, d), F32),
                   jax.ShapeDtypeStruct((n_shards, dc, q4), BF16), jax.ShapeDtypeStruct((n_shards, dc, q4), BF16),
                   jax.ShapeDtypeStruct((n_shards, q4, d), BF16),
                   jax.ShapeDtypeStruct((1, d), F32), jax.ShapeDtypeStruct((1, d), F32),
                   jax.ShapeDtypeStruct((1, dc), F32), jax.ShapeDtypeStruct((SUBLANES, dc), F32)],
        scratch_shapes=[pltpu.VMEM((dc, d), F32), pltpu.VMEM((dc, d), F32), pltpu.VMEM((d, d), F32),
                        pltpu.VMEM((SUBLANES, dc), F32),
                        pltpu.VMEM((d, dc), BF16), pltpu.VMEM((d, dc), BF16), pltpu.VMEM((d, d), BF16)],
        compiler_params=_params("arbitrary"),
    )(att, proj, proj, proj, proj, proj, proj, proj, proj, proj, x, tgt,
      w_att, w_conv, w_out, conv_w, conv_b, fin_g)


def _in_proj_bwd_x(dqkv, dpb, w_in, x, dx2, g):
    s, d = x.shape
    tn = dqkv.shape[1]
    nb = dpb.shape[1] // tn
    n = s // TM_MM

    def body(*refs):
        dps, ws = refs[:nb + 1], refs[nb + 1:2 * nb + 2]
        x_ref, dx2_ref, g_ref, gx_ref, gng_ref = refs[2 * nb + 2:]
        i = pl.program_id(0)

        @pl.when(i == 0)
        def _():
            gng_ref[...] = jnp.zeros_like(gng_ref)

        dh = _nt(dps[0][...], ws[0][0])
        for j in range(1, nb + 1):
            dh = dh + _nt(dps[j][...], ws[j][0])
        xv = x_ref[...]
        r = lax.rsqrt(jnp.mean(xv * xv, axis=-1, keepdims=True) + EPS)
        xn = xv * r
        gng_ref[...] += jnp.sum(dh * xn, axis=0, keepdims=True)
        dhn = dh * g_ref[...]
        gx_ref[...] = dx2_ref[...] + r * (dhn - xn * jnp.mean(dhn * xn, axis=-1, keepdims=True))

    tile = lambda width, col_blk: pl.BlockSpec((TM_MM, width), lambda i: (i, col_blk))
    wspec = lambda blk: pl.BlockSpec((1, d, tn), lambda i: (blk, 0, 0), pipeline_mode=pl.Buffered(1))
    return pl.pallas_call(
        body, name="in_proj_bwd_x", grid=(n,),
        in_specs=[tile(tn, 0)] + [tile(tn, j) for j in range(nb)] + [wspec(j) for j in range(nb + 1)]
        + [tile(d, 0), tile(d, 0), pl.BlockSpec((1, d), lambda i: (0, 0))],
        out_specs=[tile(d, 0), pl.BlockSpec((1, d), lambda i: (0, 0))],
        out_shape=[jax.ShapeDtypeStruct((s, d), F32), jax.ShapeDtypeStruct((1, d), F32)],
        compiler_params=_params("arbitrary"),
    )(dqkv, *([dpb] * nb), *([w_in] * (nb + 1)), x, dx2, g)


def _in_proj_bwd_w(h, dqkv, dpb, order, slots, small):
    s, d = h.shape
    tn = dqkv.shape[1]
    n = s // TM_BLK
    hr = d // 2
    settle = min(1, n - 1)
    nw, ns = len(slots), len(small)
    half_rows = [sl.shape[1] for sl in slots] + [hr]

    def body(order_ref, h_ref, da_ref, db_ref, *refs):
        srcs, parts = refs[:nw], refs[nw:nw + ns]
        slots_ref, shards, sm_all = refs[nw + ns], refs[nw + ns + 1:2 * nw + ns + 2], refs[2 * nw + ns + 2]
        acc, sendbuf, pairbuf, chipbuf = refs[2 * nw + ns + 3:2 * nw + ns + 7]
        halves, sm = refs[2 * nw + ns + 7:3 * nw + ns + 7], refs[3 * nw + ns + 7]
        psend, precv, send, recv, lsem, hsend, hrecv, ssend, srecv, hsem = refs[3 * nw + ns + 8:]
        j, i = pl.program_id(0), pl.program_id(1)
        blk = order_ref[j]
        pos = _position()
        x, y, c = pos
        me = 4 * x + 2 * y + c

        def half_sum(w):
            return halves[w] if w < nw else acc.at[pl.ds(0, hr), :]

        def half_of_shard(w, half):
            return shards[w].at[pl.ds(half * half_rows[w], half_rows[w]), :]

        def exchange(w, half):
            return pltpu.make_async_remote_copy(
                src_ref=half_sum(w), dst_ref=half_of_shard(w, half), send_sem=hsend.at[w], recv_sem=hrecv.at[w],
                device_id=(x, y, 1 - c), device_id_type=MESH)

        def keep(w):
            return pltpu.make_async_copy(half_sum(w), half_of_shard(w, c), hsem.at[w])

        def bcast(k, slot):
            return pltpu.make_async_remote_copy(
                src_ref=sm, dst_ref=sm_all.at[slot], send_sem=ssend.at[k], recv_sem=srecv.at[k],
                device_id=_peer(pos, k), device_id_type=MESH)

        own_small = pltpu.make_async_copy(sm, sm_all.at[me], hsem.at[nw + 1])

        @pl.when((j == 0) & (i == 0))
        def _():
            _pack_small(sm, *parts)
            own_small.start()
            for k in range(1, N_DEV):
                bcast(k, me).start()
            for w in range(nw):
                total = srcs[w][0].astype(F32)
                for k in range(1, srcs[w].shape[0]):
                    total = total + srcs[w][k].astype(F32)
                halves[w][...] = total
                exchange(w, c).start()
                keep(w).start()

        for first_tile in (True, False):
            for from_qkv in (True, False):
                @pl.when(((i == 0) if first_tile else (i > 0)) & ((blk == 0) if from_qkv else (blk > 0)))
                def _(first_tile=first_tile, from_qkv=from_qkv):
                    prod = _tn(h_ref[...], (da_ref if from_qkv else db_ref)[...])
                    if first_tile:
                        acc[...] = prod
                    else:
                        acc[...] += prod

        def pair(step, half):
            return pltpu.make_async_remote_copy(
                src_ref=sendbuf.at[step, pl.ds(half * hr, hr), :], dst_ref=pairbuf.at[step],
                send_sem=psend.at[step], recv_sem=precv.at[step], device_id=(x, y, 1 - c), device_id_type=MESH)

        def ici(step):
            flip = OWNER_FLIPS[step]
            return pltpu.make_async_remote_copy(
                src_ref=chipbuf.at[step], dst_ref=slots_ref.at[flip], send_sem=send.at[step], recv_sem=recv.at[step],
                device_id=_peer(pos, 4 * (flip >> 1) + 2 * (flip & 1)), device_id_type=MESH)

        local = pltpu.make_async_copy(chipbuf.at[N_CHIPS - 1], slots_ref.at[0], lsem.at[0])

        def combine(step):
            pair(step, c).wait_recv()
            mine = sendbuf[step, pl.ds(c * hr, hr), :].astype(F32)
            chipbuf[step] = (mine + pairbuf[step].astype(F32)).astype(BF16)

        for step in range(N_CHIPS):
            @pl.when((j == step) & (i == n - 1))
            def _(step=step):
                sendbuf[step] = acc[...].astype(BF16)
                pair(step, 1 - c).start()

        for step in range(N_CHIPS - 1):
            @pl.when((j == step + 1) & (i == settle))
            def _(step=step):
                combine(step)
                ici(step).start()

        @pl.when((j == N_CHIPS - 1) & (i == n - 1))
        def _():
            combine(N_CHIPS - 1)
            local.start()
            staged = [pltpu.make_async_copy(slots_ref.at[f], pairbuf.at[f - 1], hsem.at[nw + 1 + f])
                      for f in range(1, N_CHIPS)]
            for step in range(N_CHIPS - 1):
                ici(step).wait_recv()
            for cp in staged:
                cp.start()
            for cp in staged:
                cp.wait()
            total = chipbuf[N_CHIPS - 1].astype(F32)
            for f in range(1, N_CHIPS):
                total = total + pairbuf[f - 1].astype(F32)
            acc[0:hr, :] = total
            exchange(nw, c).start()
            keep(nw).start()
            for w in range(nw + 1):
                exchange(w, 1 - c).wait_recv()
            for k in range(1, N_DEV):
                px, py, pc = _peer(pos, k)
                bcast(k, 4 * px + 2 * py + pc).wait_recv()
            for step in range(N_CHIPS - 1):
                ici(step).wait_send()
            for step in range(N_CHIPS):
                pair(step, 1 - c).wait_send()
            local.wait()
            for w in range(nw + 1):
                exchange(w, c).wait_send()
                keep(w).wait()
            for k in range(1, N_DEV):
                bcast(k, me).wait_send()
            own_small.wait()

    vmem = pl.BlockSpec(memory_space=pltpu.VMEM)
    shard_shapes = [(2 * sl.shape[1], sl.shape[2]) for sl in slots]
    out = pl.pallas_call(
        body, name="in_proj_bwd_w",
        grid_spec=pltpu.PrefetchScalarGridSpec(
            num_scalar_prefetch=1, grid=(N_CHIPS, n),
            in_specs=[pl.BlockSpec((TM_BLK, d), lambda j, i, order: (i, 0)),
                      pl.BlockSpec((TM_BLK, tn), lambda j, i, order: (jnp.where(order[j] == 0, i, 0), 0)),
                      pl.BlockSpec((TM_BLK, tn), lambda j, i, order: (jnp.where(order[j] == 0, 0, i),
                                                                     jnp.maximum(order[j] - 1, 0)))]
            + [vmem] * (nw + ns),
            out_specs=[ANY] * (nw + 3),
            scratch_shapes=[pltpu.VMEM((d, tn), F32), pltpu.VMEM((N_CHIPS, d, tn), BF16),
                            pltpu.VMEM((N_CHIPS, hr, tn), BF16), pltpu.VMEM((N_CHIPS, hr, tn), BF16)]
            + [pltpu.VMEM((r // 2, cc), F32) for r, cc in shard_shapes] + [pltpu.VMEM((SMALL_ROWS, SMALL_COLS), F32)]
            + [pltpu.SemaphoreType.DMA((N_CHIPS,)), pltpu.SemaphoreType.DMA((N_CHIPS,)),
               pltpu.SemaphoreType.DMA((N_CHIPS - 1,)), pltpu.SemaphoreType.DMA((N_CHIPS - 1,)),
               pltpu.SemaphoreType.DMA((1,)),
               pltpu.SemaphoreType.DMA((nw + 1,)), pltpu.SemaphoreType.DMA((nw + 1,)),
               pltpu.SemaphoreType.DMA((N_DEV,)), pltpu.SemaphoreType.DMA((N_DEV,)),
               pltpu.SemaphoreType.DMA((nw + 1 + N_CHIPS,))]),
        out_shape=[jax.ShapeDtypeStruct((N_CHIPS, hr, tn), BF16)]
        + [jax.ShapeDtypeStruct(sh, F32) for sh in shard_shapes + [(d, tn)]]
        + [jax.ShapeDtypeStruct((N_DEV, SMALL_ROWS, SMALL_COLS), F32)],
        compiler_params=_params("arbitrary", "arbitrary"),
    )(order, h, dqkv, dpb, *slots, *small)
    return [out[nw + 1]] + list(out[1:nw + 1]) + [out[nw + 2]]


LOSS_ROW = 6


def _adam_update(w, g, m, v):
    c1 = 1.0 / (1.0 - ADAM_B1 ** ADAM_STEP)
    c2 = 1.0 / (1.0 - ADAM_B2 ** ADAM_STEP)
    m2 = ADAM_B1 * m + (1.0 - ADAM_B1) * g
    v2 = ADAM_B2 * v + (1.0 - ADAM_B2) * (g * g)
    return -ADAM_LR * ((m2 * c1) / (jnp.sqrt(v2 * c2) + ADAM_EPS) + ADAM_WD * w), m2, v2


def _adamw_small(recv, params, moments_m, moments_v, out_shapes):
    k = recv.shape[0]
    n_par = len(params)
    cshard = params[3].shape[1]

    def body(*refs):
        r_ref = refs[0]
        ws, ms, vs = refs[1:1 + n_par], refs[1 + n_par:1 + 2 * n_par], refs[1 + 2 * n_par:1 + 3 * n_par]
        loss_ref = refs[1 + 3 * n_par]
        outs = refs[2 + 3 * n_par:]
        total = r_ref[0]
        for slot in range(1, k):
            total = total + r_ref[slot]
        loss_ref[...] = jnp.sum(total[LOSS_ROW:LOSS_ROW + 1, :], axis=1, keepdims=True)
        chip = 2 * lax.axis_index("x") + lax.axis_index("y")
        g_cw = jnp.zeros((3, cshard), F32)
        for sh in range(N_CHIPS):
            g_cw = g_cw + jnp.where(chip == sh, total[3:6, sh * cshard:(sh + 1) * cshard], 0.0)
        grads = [total[0:1, :], total[1:2, :], total[2:3, :ws[2].shape[1]], g_cw, total[8:16, :ws[4].shape[1]]]
        for p in range(n_par):
            delta, m2, v2 = _adam_update(ws[p][...], grads[p], ms[p][...], vs[p][...])
            for q, val in enumerate((grads[p], delta, m2, v2)):
                out = outs[4 * p + q]
                out[...] = val[0] if len(out.shape) == 1 else val

    shapes = [jax.ShapeDtypeStruct((1, 1), F32)]
    for shape in out_shapes:
        shapes += [jax.ShapeDtypeStruct(shape, F32)] * 4
    return pl.pallas_call(body, name="adamw_small", out_shape=shapes)(recv, *params, *moments_m, *moments_v)


def _adamw_group(ws, gs, ms, vs, name, steps):
    k = len(ws)

    def body(*refs):
        ins, outs = refs[:4 * k], refs[4 * k:]
        for j in range(k):
            gv = ins[k + j][...]
            outs[4 * j][...] = gv
            outs[4 * j + 1][...], outs[4 * j + 2][...], outs[4 * j + 3][...] = _adam_update(
                ins[j][...], gv, ins[2 * k + j][...], ins[3 * k + j][...])

    specs = [pl.BlockSpec((w.shape[0] // steps, w.shape[1]), lambda i: (i, 0)) for w in ws]
    out = pl.pallas_call(
        body, name=name, grid=(steps,),
        in_specs=specs * 4, out_specs=[spec for spec in specs for _ in range(4)],
        out_shape=[jax.ShapeDtypeStruct(w.shape, F32) for w in ws for _ in range(4)],
        compiler_params=_params("parallel"),
    )(*ws, *gs, *ms, *vs)
    return [tuple(out[4 * j:4 * j + 4]) for j in range(k)]


ANY = pl.BlockSpec(memory_space=pl.ANY)
N_CHIPS = 4
N_DEV = 8
OWNER_FLIPS = (3, 1, 2, 0)


def _position():
    return lax.axis_index("x"), lax.axis_index("y"), lax.axis_index("c")


def _gather_out_shapes(shards, kinds, cw3):
    full = [(a.shape[0], a.shape[1] * N_CHIPS) if k == "cols" else (a.shape[0] * N_CHIPS, a.shape[1])
            for a, k in zip(shards, kinds)]
    return [jax.ShapeDtypeStruct(f, BF16) for f in full] + [
        jax.ShapeDtypeStruct((N_CHIPS,) + cw3.shape, cw3.dtype)]


def _gather_sems(nw):
    return [pltpu.SemaphoreType.DMA((3, nw)), pltpu.SemaphoreType.DMA((3, nw)),
            pltpu.SemaphoreType.DMA((3, nw)), pltpu.SemaphoreType.DMA((3, nw)),
            pltpu.SemaphoreType.DMA((3,)), pltpu.SemaphoreType.DMA((3,)), pltpu.SemaphoreType.DMA((nw + 1,))]


def _gather_plan(kinds, srcs, dsts, cw, cw_all, send1, recv1, send2, recv2, ssend, srecv, lsem):
    nw = len(srcs)
    x, y, c = _position()
    mine = 2 * x + y
    chips = [(x, 1 - y), (1 - x, y), (1 - x, 1 - y)]

    def window(w, shard, half):
        r, cc = srcs[w].shape
        hr = r // 2
        if kinds[w] == "cols":
            rows = pl.ds(0, r) if half is None else pl.ds(half * hr, hr)
            return dsts[w].at[rows, pl.ds(shard * cc, cc)]
        rows = pl.ds(shard * r, r) if half is None else pl.ds(shard * r + half * hr, hr)
        return dsts[w].at[rows, :]

    def my_half(w):
        hr = srcs[w].shape[0] // 2
        return srcs[w].at[pl.ds(c * hr, hr), :]

    def local():
        return [pltpu.make_async_copy(srcs[w], window(w, mine, None), lsem.at[w]) for w in range(nw)] + [
            pltpu.make_async_copy(cw, cw_all.at[mine], lsem.at[nw])]

    def ici(k, w, shard):
        kx, ky = chips[k]
        return pltpu.make_async_remote_copy(
            src_ref=my_half(w), dst_ref=window(w, shard, c), send_sem=send1.at[k, w], recv_sem=recv1.at[k, w],
            device_id=(kx, ky, c), device_id_type=MESH)

    def d2d(k, w, shard, half):
        return pltpu.make_async_remote_copy(
            src_ref=window(w, shard, half), dst_ref=window(w, shard, half),
            send_sem=send2.at[k, w], recv_sem=recv2.at[k, w], device_id=(x, y, 1 - c), device_id_type=MESH)

    def small(k, shard):
        kx, ky = chips[k]
        return pltpu.make_async_remote_copy(
            src_ref=cw, dst_ref=cw_all.at[shard], send_sem=ssend.at[k], recv_sem=srecv.at[k],
            device_id=(kx, ky, c), device_id_type=MESH)

    def theirs(k):
        kx, ky = chips[k]
        return 2 * kx + ky

    def start():
        for cp in local():
            cp.start()
        for k in range(3):
            for w in range(nw):
                ici(k, w, mine).start()
            small(k, mine).start()

    def forward():
        for k in range(3):
            for w in range(nw):
                ici(k, w, theirs(k)).wait_recv()
                d2d(k, w, theirs(k), c).start()

    def finish():
        for k in range(3):
            for w in range(nw):
                d2d(k, w, theirs(k), 1 - c).wait_recv()
            small(k, theirs(k)).wait_recv()
        for k in range(3):
            for w in range(nw):
                ici(k, w, mine).wait_send()
                d2d(k, w, theirs(k), c).wait_send()
            small(k, mine).wait_send()
        for cp in local():
            cp.wait()

    return start, forward, finish


def _scatter_out_shapes(parts):
    return [jax.ShapeDtypeStruct((N_DEV, p.shape[1] // 2, p.shape[2]), p.dtype) for p in parts]


def _scatter_sems(nw):
    return [pltpu.SemaphoreType.DMA((N_DEV, nw)), pltpu.SemaphoreType.DMA((N_DEV, nw)), pltpu.SemaphoreType.DMA((nw,))]


def _peer(pos, k):
    x, y, c = pos
    return ((1 - x) if k & 4 else x, (1 - y) if k & 2 else y, (1 - c) if k & 1 else c)


def _scatter_plan(srcs, dsts, send, recv, lsem):
    nw = len(srcs)
    pos = _position()

    def piece(w, k):
        px, py, pc = _peer(pos, k)
        hr = srcs[w].shape[1] // 2
        return srcs[w].at[2 * px + py, pl.ds(pc * hr, hr), :]

    def remote(w, k):
        return pltpu.make_async_remote_copy(
            src_ref=piece(w, k), dst_ref=dsts[w].at[k], send_sem=send.at[k, w], recv_sem=recv.at[k, w],
            device_id=_peer(pos, k), device_id_type=MESH)

    def local(w):
        return pltpu.make_async_copy(piece(w, 0), dsts[w].at[0], lsem.at[w])

    def start():
        for w in range(nw):
            local(w).start()
        for k in range(1, N_DEV):
            for w in range(nw):
                remote(w, k).start()

    def finish():
        for k in range(1, N_DEV):
            for w in range(nw):
                remote(w, k).wait_recv()
        for k in range(1, N_DEV):
            for w in range(nw):
                remote(w, k).wait_send()
        for w in range(nw):
            local(w).wait()

    return start, finish


def _pack_small(sm, norm_g, fin_g, conv_b, conv_w, loss_vec, rel):
    sm[...] = jnp.zeros_like(sm)
    for row, ref in ((0, norm_g), (1, fin_g), (2, conv_b), (LOSS_ROW, loss_vec)):
        sm[row:row + 1, 0:ref.shape[1]] = ref[...]
    sm[3:6, 0:conv_w.shape[1]] = conv_w[0:3, :]
    sm[8:8 + HEADS, 0:rel.shape[1]] = rel[...]


def kernel(x, norm_g, w_in, rel_bias, w_att_out, conv_w, conv_b, w_conv_out, w_out, final_norm_g, loss_target, m_norm_g, m_w_in, m_rel_bias, m_w_att_out, m_conv_w, m_conv_b, m_w_conv_out, m_w_out, m_final_norm_g, v_norm_g, v_w_in, v_rel_bias, v_w_att_out, v_conv_w, v_conv_b, v_w_conv_out, v_w_out, v_final_norm_g):
    xs, tgt = x[0], loss_target[0]
    cshard = conv_w.shape[2]
    chip = 2 * lax.axis_index("x") + lax.axis_index("y")

    shards = [w_in[0], w_att_out[0], w_conv_out[0], w_out[0]]
    cw3 = conv_w[0]
    flips = jnp.arange(N_CHIPS, dtype=jnp.int32)
    own_first = jnp.bitwise_xor(chip, flips)
    own_last = jnp.bitwise_xor(chip, jnp.asarray(OWNER_FLIPS, jnp.int32))

    proj, h, wb_in = _in_proj_gather(xs, norm_g, shards[0], own_first)
    diag = jnp.take(rel_bias[0], _diag_rel_index(), axis=1)
    att, lse, wb_att, wb_conv, wb_out, cw_all = _attn_fwd(diag, proj, shards[1:], ["cols", "cols", "rows"], cw3)
    conv_w_full = jnp.transpose(cw_all, (1, 0, 2)).reshape(cw3.shape[0], N_CHIPS * cshard)
    (dpb, d_att, dx2, g_att_p, g_conv_p, g_out_p, loss_vec, g_fin, g_cb, g_cw) = _mixer_mid(
        att, proj, xs, tgt, wb_att, wb_conv, wb_out, conv_w_full, conv_b,
        final_norm_g[None, :])
    dqkv, g_rel, r_att, r_conv, r_out = _attn_bwd(diag, proj, d_att, att, lse, [g_att_p, g_conv_p, g_out_p])
    grad_x, g_norm = _in_proj_bwd_x(dqkv, dpb, wb_in, xs, dx2, norm_g)
    gw_in, gw_att, gw_conv, gw_out, r_small = _in_proj_bwd_w(
        h, dqkv, dpb, own_last, [r_att, r_conv, r_out], [g_norm, g_fin, g_cb, g_cw, loss_vec, g_rel])

    small_out = _adamw_small(
        r_small,
        [norm_g, final_norm_g[None, :], conv_b, conv_w[0], rel_bias[0]],
        [m_norm_g, m_final_norm_g[None, :], m_conv_b, m_conv_w[0], m_rel_bias[0]],
        [v_norm_g, v_final_norm_g[None, :], v_conv_b, v_conv_w[0], v_rel_bias[0]],
        [norm_g.shape, final_norm_g.shape, conv_b.shape, conv_w[0].shape, rel_bias[0].shape])
    loss = small_out[0][0, 0]
    small_names = ["norm_g", "final_norm_g", "conv_b", "conv_w", "rel_bias"]
    fix = {"norm_g": lambda a: a, "final_norm_g": lambda a: a, "conv_b": lambda a: a,
           "conv_w": lambda a: a[None], "rel_bias": lambda a: a[None]}
    small_res = {name: [fix[name](small_out[1 + 4 * p + q]) for q in range(4)] for p, name in enumerate(small_names)}

    names = ["w_in", "w_att_out", "w_conv_out", "w_out"]
    group = _adamw_group([w_in[0], w_att_out[0], w_conv_out[0], w_out[0]], [gw_in, gw_att, gw_conv, gw_out],
                         [m_w_in[0], m_w_att_out[0], m_w_conv_out[0], m_w_out[0]],
                         [v_w_in[0], v_w_att_out[0], v_w_conv_out[0], v_w_out[0]], "adamw_matrices", ADAMW_STEPS)
    big = dict(zip(names, group))
    big = {name: tuple(a[None] for a in four) for name, four in big.items()}

    order = ["norm_g", "w_in", "rel_bias", "w_att_out", "conv_w", "conv_b", "w_conv_out", "w_out", "final_norm_g"]
    outs = [loss, grad_x[None]]
    for which in range(4):
        for name in order:
            outs.append(big[name][which] if name in big else small_res[name][which])
    return tuple(outs)
```

```python
import numpy as np
import jax
import jax.numpy as jnp
from jax import lax
from jax.experimental import pallas as pl
from jax.experimental.pallas import tpu as pltpu

F32 = jnp.float32
BF16 = jnp.bfloat16
MESH = pl.DeviceIdType.MESH

CHUNK = 64
N_LEFT = 8
HEADS = 8
HEAD_DIM = 64
D_ATT = HEADS * HEAD_DIM
MAX_REL = 128
N_REL = 2 * MAX_REL + 1
EPS = 1e-6
NEG_BIG = -1e30
ADAM_LR, ADAM_B1, ADAM_B2, ADAM_EPS, ADAM_WD, ADAM_STEP = 0.001, 0.9, 0.999, 1e-08, 0.01, 10

LANES = 128
SUBLANES = 8
VMEM_LIMIT = 56 * 1024 * 1024

QB = 2 * CHUNK
KW = N_LEFT * CHUNK + QB
DIAG = KW + QB
TQ = N_LEFT * CHUNK
TM_MID = 256
ADAMW_STEPS = 8
TM_MM = 512
TM_BLK = 1024
SMALL_ROWS, SMALL_COLS = 16, 1024


def _params(*sem):
    return pltpu.CompilerParams(dimension_semantics=sem, vmem_limit_bytes=VMEM_LIMIT)


def _nt(a, b):
    return lax.dot_general(a, b, (((1,), (1,)), ((), ())), preferred_element_type=F32)


def _tn(a, b):
    return lax.dot_general(a, b, (((0,), (0,)), ((), ())), preferred_element_type=F32)


def _nn(a, b):
    return jnp.dot(a, b, preferred_element_type=F32)


def _diag_rel_index():
    d = np.arange(DIAG)
    diff = np.where(d < KW, d, d - DIAG)
    rel = N_LEFT * CHUNK - diff
    return np.clip(rel, -MAX_REL, MAX_REL) + MAX_REL


def _build_bias(diag_ref, bias_scr):
    r = lax.broadcasted_iota(jnp.int32, (QB, KW), 0) // CHUNK
    s = lax.broadcasted_iota(jnp.int32, (QB, KW), 1) // CHUNK
    allowed = (s >= r) & (s <= r + N_LEFT)
    for h in range(HEADS):
        row = jnp.broadcast_to(diag_ref[h:h + 1, :], (QB, DIAG))
        t = pltpu.roll(row, 0, 1, stride=1, stride_axis=0)
        bias_scr[h // 2, (h % 2) * QB:(h % 2 + 1) * QB, :] = jnp.where(allowed, t[:, :KW], NEG_BIG)


def _stack_heads(a, lane_hi):
    zero = jnp.zeros_like(a)
    return jnp.concatenate([jnp.where(lane_hi, zero, a), jnp.where(lane_hi, a, zero)], axis=0)


def _in_proj_gather(x, g, shard, order):
    s, d = x.shape
    tn = shard.shape[1]
    n = s // TM_BLK
    hr = d // 2

    def body(order_ref, x_ref, g_ref, shard32_ref, proj_ref, h_ref, wfull_ref, shard_ref, hbuf, wbuf,
             send1, recv1, send2, recv2, lsem):
        del order_ref
        j, i = pl.program_id(0), pl.program_id(1)
        x, y, c = _position()
        mine = 2 * x + y
        chips = [(x, 1 - y), (1 - x, y), (1 - x, 1 - y)]

        def theirs(k):
            return 2 * chips[k][0] + chips[k][1]

        def half_rows(half):
            return pl.ds(half * hr, hr)

        def landing(k, shard_index, half):
            if k < 2:
                return wbuf.at[k, half_rows(half), :]
            return wfull_ref.at[shard_index, half_rows(half), :]

        def ici(k, shard_index):
            return pltpu.make_async_remote_copy(
                src_ref=shard_ref.at[half_rows(c), :], dst_ref=landing(k, shard_index, c),
                send_sem=send1.at[k], recv_sem=recv1.at[k], device_id=(*chips[k], c), device_id_type=MESH)

        def d2d(k, half):
            return pltpu.make_async_remote_copy(
                src_ref=wbuf.at[k % 2, half_rows(half), :], dst_ref=landing(k, theirs(k), half),
                send_sem=send2.at[k], recv_sem=recv2.at[k], device_id=(x, y, 1 - c), device_id_type=MESH)

        def load(k, half, sem):
            return pltpu.make_async_copy(wfull_ref.at[theirs(k), half_rows(half), :],
                                         wbuf.at[k % 2, half_rows(half), :], lsem.at[sem])

        def keep(k):
            return pltpu.make_async_copy(wbuf.at[k], wfull_ref.at[theirs(k)], lsem.at[3 + k])

        own = pltpu.make_async_copy(shard_ref, wfull_ref.at[mine], lsem.at[0])

        def put_proj(block):
            for grp in range(tn // D_ATT):
                proj_ref[grp] = block[:, grp * D_ATT:(grp + 1) * D_ATT].astype(BF16)

        @pl.when((j == 0) & (i == 0))
        def _():
            shard_ref[...] = shard32_ref[...].astype(BF16)
            own.start()
            ici(0, mine).start()
            ici(1, mine).start()

        for k in range(2):
            first = n - 1 if k == 0 else min(n // 2, n - 1)

            @pl.when((j == k) & (i == first))
            def _(k=k):
                if k == 0:
                    ici(0, mine).wait_send()
                    ici(1, mine).wait_send()
                    ici(2, mine).start()
                ici(k, theirs(k)).wait_recv()
                d2d(k, c).start()

            then = (1, 0) if k == 0 else (1, min(first + 1, n - 1))

            @pl.when((j == then[0]) & (i == then[1]))
            def _(k=k):
                d2d(k, 1 - c).wait_recv()
                keep(k).start()

        first = min(n // 2, n - 1)

        @pl.when((j == 2) & (i == first))
        def _():
            d2d(0, c).wait_send()
            keep(0).wait()
            ici(2, theirs(2)).wait_recv()
            load(2, c, 1).start()

        @pl.when((j == 2) & (i == min(first + 1, n - 1)))
        def _():
            load(2, c, 1).wait()
            d2d(2, c).start()

        @pl.when((j == 2) & (i == min(first + 2, n - 1)))
        def _():
            d2d(2, 1 - c).wait_recv()
            load(2, 1 - c, 2).start()

        @pl.when((j == 3) & (i == 0))
        def _():
            load(2, 1 - c, 2).wait()

        @pl.when(j == 0)
        def _():
            xv = x_ref[...]
            r = lax.rsqrt(jnp.mean(xv * xv, axis=-1, keepdims=True) + EPS)
            hv = ((xv * r) * g_ref[...]).astype(BF16)
            hbuf[i] = hv
            h_ref[...] = hv
            put_proj(_nn(hv, shard_ref[...]))

        for k in range(3):
            @pl.when(j == k + 1)
            def _(k=k):
                put_proj(_nn(hbuf[i], wbuf[k % 2]))

        @pl.when((j == 3) & (i == n - 1))
        def _():
            ici(2, mine).wait_send()
            d2d(1, c).wait_send()
            d2d(2, c).wait_send()
            own.wait()
            keep(1).wait()

    return pl.pallas_call(
        body, name="in_proj_gather",
        grid_spec=pltpu.PrefetchScalarGridSpec(
            num_scalar_prefetch=1, grid=(N_CHIPS, n),
            in_specs=[pl.BlockSpec((TM_BLK, d), lambda j, i, order: (jnp.where(j == 0, i, n - 1), 0)),
                      pl.BlockSpec((1, d), lambda j, i, order: (0, 0)), pl.BlockSpec(memory_space=pltpu.VMEM)],
            out_specs=[pl.BlockSpec((tn // D_ATT, TM_BLK, D_ATT), lambda j, i, order: (order[j], i, 0)),
                       pl.BlockSpec((TM_BLK, d), lambda j, i, order: (jnp.where(j == 0, i, n - 1), 0)), ANY],
            scratch_shapes=[pltpu.VMEM((d, tn), BF16), pltpu.VMEM((n, TM_BLK, d), BF16), pltpu.VMEM((2, d, tn), BF16),
                            pltpu.SemaphoreType.DMA((3,)), pltpu.SemaphoreType.DMA((3,)),
                            pltpu.SemaphoreType.DMA((3,)), pltpu.SemaphoreType.DMA((3,)), pltpu.SemaphoreType.DMA((5,))]),
        out_shape=[jax.ShapeDtypeStruct((N_CHIPS * tn // D_ATT, s, D_ATT), BF16), jax.ShapeDtypeStruct((s, d), BF16),
                   jax.ShapeDtypeStruct((N_CHIPS, d, tn), BF16)],
        compiler_params=_params("arbitrary", "arbitrary"),
    )(order, x, g, shard)


def _attn_fwd(diag, proj, shards, kinds, cw3):
    s = proj.shape[1]
    n = s // TQ
    nw = len(shards)
    scale = HEAD_DIM ** -0.5

    def body(*refs):
        diag_ref, q_ref, kp_ref, kc_ref, vp_ref, vc_ref = refs[:6]
        srcs, cw = refs[6:6 + nw], refs[6 + nw]
        o_ref, lse_ref = refs[7 + nw:9 + nw]
        dsts, cw_all = refs[9 + nw:9 + 2 * nw], refs[9 + 2 * nw]
        bias_scr = refs[10 + 2 * nw]
        casts = refs[11 + 2 * nw:11 + 3 * nw]
        start, forward, finish = _gather_plan(kinds, casts, dsts, cw, cw_all, *refs[11 + 3 * nw:])
        i = pl.program_id(0)

        @pl.when(i == 0)
        def _():
            for w in range(nw):
                casts[w][...] = srcs[w][...].astype(BF16)
            start()
            _build_bias(diag_ref, bias_scr)

        @pl.when(i == n // 2)
        def _():
            forward()

        @pl.when(i == n - 1)
        def _():
            finish()

        lane_hi = lax.broadcasted_iota(jnp.int32, (QB, LANES), 1) >= HEAD_DIM

        def block(b, first_tile):
            r0, n_prev = b * QB, TQ - b * QB
            n_cur = KW - n_prev
            pairs = range(HEADS // 2)
            lanes_of = [slice(LANES * p, LANES * (p + 1)) for p in pairs]
            scores = []
            for p in pairs:
                lanes = lanes_of[p]
                q2 = _stack_heads(q_ref[r0:r0 + QB, lanes] * scale, lane_hi)
                s_cur = _nt(q2, kc_ref[0:n_cur, lanes]) + bias_scr[p, :, n_prev:KW]
                if first_tile:
                    scores.append(s_cur)
                else:
                    scores.append(jnp.concatenate(
                        [_nt(q2, kp_ref[r0:TQ, lanes]) + bias_scr[p, :, 0:n_prev], s_cur], axis=1))
            probs = []
            for p in pairs:
                sc = scores[p]
                m = jnp.max(sc, axis=1, keepdims=True)
                pe = jnp.exp(sc - m)
                l = jnp.sum(pe, axis=1, keepdims=True)
                probs.append((pe.astype(BF16), l, m))
            for p in pairs:
                lanes = lanes_of[p]
                pb, l, m = probs[p]
                if first_tile:
                    o2 = _nn(pb, vc_ref[0:n_cur, lanes]) / l
                else:
                    o2 = (_nn(pb[:, 0:n_prev], vp_ref[r0:TQ, lanes]) + _nn(pb[:, n_prev:KW], vc_ref[0:n_cur, lanes])) / l
                lse2 = m + jnp.log(l)
                lse_ref[r0:r0 + QB, 2 * p:2 * p + 1] = lse2[0:QB, :]
                lse_ref[r0:r0 + QB, 2 * p + 1:2 * p + 2] = lse2[QB:2 * QB, :]
                o_ref[r0:r0 + QB, lanes] = jnp.where(lane_hi, o2[QB:2 * QB, :], o2[0:QB, :]).astype(BF16)

        @pl.when(i == 0)
        def _():
            for b in range(TQ // QB):
                block(b, True)

        @pl.when(i > 0)
        def _():
            for b in range(TQ // QB):
                block(b, False)

    blk = lambda grp, prev: pl.BlockSpec(
        (None, TQ, D_ATT), (lambda i: (grp, jnp.maximum(i - 1, 0), 0)) if prev else (lambda i: (grp, i, 0)))
    vmem = pl.BlockSpec(memory_space=pltpu.VMEM)
    return pl.pallas_call(
        body, name="attn_fwd", grid=(n,),
        in_specs=[pl.BlockSpec((HEADS, DIAG), lambda i: (0, 0)),
                  blk(0, False), blk(1, True), blk(1, False), blk(2, True), blk(2, False)] + [vmem] * (nw + 1),
        out_specs=[pl.BlockSpec((TQ, D_ATT), lambda i: (i, 0)), pl.BlockSpec((TQ, HEADS), lambda i: (i, 0))]
        + [ANY] * (nw + 1),
        out_shape=[jax.ShapeDtypeStruct((s, D_ATT), BF16), jax.ShapeDtypeStruct((s, HEADS), F32)]
        + _gather_out_shapes(shards, kinds, cw3),
        scratch_shapes=[pltpu.VMEM((HEADS // 2, 2 * QB, KW), F32)] + [pltpu.VMEM(a.shape, BF16) for a in shards]
        + _gather_sems(nw),
        compiler_params=_params("arbitrary"),
    )(diag, proj, proj, proj, proj, proj, *shards, cw3)


def _attn_bwd(diag, proj, d_att, att, lse, parts):
    s = proj.shape[1]
    n = s // TQ
    npart = len(parts)
    scale = HEAD_DIM ** -0.5
    rel_pad = 3 * LANES

    def body(*refs):
        diag_ref, q_ref, kp_ref, kc_ref, vp_ref, vc_ref, do_ref, o_ref, lse_ref = refs[:9]
        part_refs = refs[9:9 + npart]
        dqkv_ref, dbias_ref = refs[9 + npart:11 + npart]
        slot_refs = refs[11 + npart:11 + 2 * npart]
        bias_scr, dbias_acc, dk_acc, dv_acc, dq_scr = refs[11 + 2 * npart:16 + 2 * npart]
        start, finish = _scatter_plan(part_refs, slot_refs, *refs[16 + 2 * npart:])
        i = pl.program_id(0)
        cur, prv = i % 2, 1 - i % 2

        @pl.when(i == 0)
        def _():
            start()
            _build_bias(diag_ref, bias_scr)
            dbias_acc[...] = jnp.zeros_like(dbias_acc)
            dk_acc[...] = jnp.zeros_like(dk_acc)
            dv_acc[...] = jnp.zeros_like(dv_acc)

        @pl.when(i > 0)
        def _():
            dqkv_ref[:, 0:D_ATT] = dq_scr[...]
            dk_acc[cur] = jnp.zeros((TQ, D_ATT), F32)
            dv_acc[cur] = jnp.zeros((TQ, D_ATT), F32)

        lane_hi = lax.broadcasted_iota(jnp.int32, (QB, LANES), 1) >= HEAD_DIM
        col = lax.broadcasted_iota(jnp.int32, (2 * QB, KW), 1)

        def make_block(first_tile):
            def block(b):
                r0, n_prev = b * QB, TQ - b * QB
                n_cur = KW - n_prev
                for p in range(HEADS // 2):
                    lanes = slice(LANES * p, LANES * (p + 1))
                    q2 = _stack_heads(q_ref[r0:r0 + QB, lanes] * scale, lane_hi)
                    kw = jnp.concatenate([kp_ref[r0:TQ, lanes], kc_ref[0:n_cur, lanes]], axis=0)
                    vw = jnp.concatenate([vp_ref[r0:TQ, lanes], vc_ref[0:n_cur, lanes]], axis=0)
                    dop = do_ref[r0:r0 + QB, lanes]
                    do2 = _stack_heads(dop, lane_hi)
                    prod = dop.astype(F32) * o_ref[r0:r0 + QB, lanes].astype(F32)
                    delta2 = jnp.concatenate(
                        [jnp.sum(jnp.where(lane_hi, 0.0, prod), axis=1, keepdims=True),
                         jnp.sum(jnp.where(lane_hi, prod, 0.0), axis=1, keepdims=True)], axis=0)
                    lse2 = jnp.concatenate([lse_ref[r0:r0 + QB, 2 * p:2 * p + 1],
                                            lse_ref[r0:r0 + QB, 2 * p + 1:2 * p + 2]], axis=0)
                    sc = _nt(q2, kw) + bias_scr[p]
                    if first_tile:
                        sc = jnp.where(col >= TQ - r0, sc, NEG_BIG)
                    pr = jnp.exp(sc - lse2)
                    ds = pr * (_nt(do2, vw) - delta2)
                    dbias_acc[p] += ds
                    dsb = ds.astype(BF16)
                    dv_w = _tn(pr.astype(BF16), do2)
                    dk_w = _tn(dsb, q2)
                    dv_acc[prv, r0:TQ, lanes] += dv_w[0:n_prev, :]
                    dv_acc[cur, 0:n_cur, lanes] += dv_w[n_prev:KW, :]
                    dk_acc[prv, r0:TQ, lanes] += dk_w[0:n_prev, :]
                    dk_acc[cur, 0:n_cur, lanes] += dk_w[n_prev:KW, :]
                    dq2 = _nn(dsb, kw)
                    dq = jnp.where(lane_hi, dq2[QB:2 * QB, :], dq2[0:QB, :]) * scale
                    dq_scr[r0:r0 + QB, lanes] = dq.astype(BF16)
            return block

        @pl.when(i == 0)
        def _():
            for b in range(TQ // QB):
                make_block(True)(b)

        @pl.when((i > 0) & (i < n))
        def _():
            for b in range(TQ // QB):
                make_block(False)(b)

        @pl.when(i > 0)
        def _():
            dqkv_ref[:, D_ATT:2 * D_ATT] = dk_acc[prv].astype(BF16)
            dqkv_ref[:, 2 * D_ATT:3 * D_ATT] = dv_acc[prv].astype(BF16)

        @pl.when(i == n)
        def _():
            d_iota = lax.broadcasted_iota(jnp.int32, (DIAG, rel_pad), 0)
            n_iota = lax.broadcasted_iota(jnp.int32, (DIAG, rel_pad), 1)
            diff = jnp.where(d_iota < KW, d_iota, d_iota - DIAG)
            idx = jnp.clip(N_LEFT * CHUNK - diff, -MAX_REL, MAX_REL) + MAX_REL
            onehot = (idx == n_iota).astype(F32)
            rows = []
            for hd in range(HEADS):
                acc = dbias_acc[hd // 2, (hd % 2) * QB:(hd % 2 + 1) * QB, :]
                a = jnp.concatenate([acc, jnp.zeros((QB, DIAG - KW), F32)], axis=1)
                g8 = a[0:SUBLANES, :]
                for blk in range(1, QB // SUBLANES):
                    g8 = g8 + pltpu.roll(a[blk * SUBLANES:(blk + 1) * SUBLANES, :], DIAG - blk * SUBLANES, 1)
                g1 = g8[0:1, :]
                for r in range(1, SUBLANES):
                    g1 = g1 + pltpu.roll(g8[r:r + 1, :], DIAG - r, 1)
                rows.append(g1)
            g = jnp.concatenate(rows, axis=0)
            dbias_ref[...] = jnp.dot(g, onehot, preferred_element_type=F32, precision=lax.Precision.HIGHEST)
            finish()

    last = n - 1
    cur = lambda grp: pl.BlockSpec((None, TQ, D_ATT), lambda i: (grp, jnp.minimum(i, last), 0))
    prev = lambda grp: pl.BlockSpec((None, TQ, D_ATT), lambda i: (grp, jnp.maximum(jnp.minimum(i, last) - 1, 0), 0))
    tile = pl.BlockSpec((TQ, D_ATT), lambda i: (jnp.minimum(i, last), 0))
    return pl.pallas_call(
        body, name="attn_bwd", grid=(n + 1,),
        in_specs=[pl.BlockSpec((HEADS, DIAG), lambda i: (0, 0)),
                  cur(0), prev(1), cur(1), prev(2), cur(2), tile, tile,
                  pl.BlockSpec((TQ, HEADS), lambda i: (jnp.minimum(i, last), 0))] + [ANY] * npart,
        out_specs=[pl.BlockSpec((TQ, 3 * D_ATT), lambda i: (jnp.maximum(i - 1, 0), 0)),
                   pl.BlockSpec((HEADS, rel_pad), lambda i: (0, 0))] + [ANY] * npart,
        out_shape=[jax.ShapeDtypeStruct((s, 3 * D_ATT), BF16), jax.ShapeDtypeStruct((HEADS, rel_pad), F32)]
        + _scatter_out_shapes(parts),
        scratch_shapes=[pltpu.VMEM((HEADS // 2, 2 * QB, KW), F32), pltpu.VMEM((HEADS // 2, 2 * QB, KW), F32),
                        pltpu.VMEM((2, TQ, D_ATT), F32), pltpu.VMEM((2, TQ, D_ATT), F32),
                        pltpu.VMEM((TQ, D_ATT), BF16)] + _scatter_sems(npart),
        compiler_params=_params("arbitrary"),
    )(diag, proj, proj, proj, proj, proj, d_att, att, lse, *parts)


def _shift_down(a, k, halo):
    rolled = pltpu.roll(a, k, 0)
    row = lax.broadcasted_iota(jnp.int32, halo.shape, 0)
    first = jnp.where(row < k, pltpu.roll(halo, k, 0), rolled[0:SUBLANES, :])
    return jnp.concatenate([first, rolled[SUBLANES:, :]], axis=0)


def _shift_up(a, k, nxt):
    tm = a.shape[0]
    rolled = pltpu.roll(a, tm - k, 0)
    row = lax.broadcasted_iota(jnp.int32, nxt.shape, 0)
    last = jnp.where(row >= SUBLANES - k, pltpu.roll(nxt, SUBLANES - k, 0), rolled[tm - SUBLANES:, :])
    return jnp.concatenate([rolled[:tm - SUBLANES, :], last], axis=0)


def _sigmoid(v):
    return 0.5 * jnp.tanh(0.5 * v) + 0.5


def _mixer_mid(att, proj, x, tgt, w_att, w_conv, w_out, conv_w, conv_b, fin_g):
    s, d = x.shape
    dc = D_ATT
    n = s // TM_MID
    tm = TM_MID
    n_shards = 4

    def body(att_ref, za_ref, gb_ref, gc_ref, u_ref, zc_ref, hgc_ref, hu_ref, gatt_ref, gconv_ref, x_ref, t_ref,
             watt_ref, wconv_ref, wout_ref, cw_ref, cb_ref, fg_ref,
             dpb_ref, do_ref, dx2_ref, gatt_o, gconv_o, gout_o, loss_o, gfn_o, gcb_o, gcw_o,
             acc_att, acc_conv, acc_out, carry, watt_t_ref, wconv_t_ref, wout_t_ref):
        i = pl.program_id(0)
        tile = n - 1 - i

        @pl.when(i == 0)
        def _():
            acc_att[...] = jnp.zeros_like(acc_att)
            acc_conv[...] = jnp.zeros_like(acc_conv)
            acc_out[...] = jnp.zeros_like(acc_out)
            carry[...] = jnp.zeros_like(carry)
            loss_o[...] = jnp.zeros_like(loss_o)
            gfn_o[...] = jnp.zeros_like(gfn_o)
            gcb_o[...] = jnp.zeros_like(gcb_o)
            gcw_o[...] = jnp.zeros_like(gcw_o)
            watt_t_ref[...] = watt_ref[...].T
            wconv_t_ref[...] = wconv_ref[...].T
            wout_t_ref[...] = wout_ref[...].T

        halves = [slice(hh * (tm // 2), (hh + 1) * (tm // 2)) for hh in range(2)]
        both = lambda fn: [fn(rows) for rows in halves]
        f32 = lambda ref, rows: ref[rows, :].astype(F32)

        gc = gc_ref[...].astype(F32)
        u = u_ref[...].astype(F32)
        cu = gc * u
        halo = jnp.where(tile > 0, hgc_ref[...].astype(F32) * hu_ref[...].astype(F32), 0.0)
        cu1 = _shift_down(cu, 1, halo)
        cu2 = _shift_down(cu, 2, halo)
        w0, w1, w2 = cw_ref[0:1, :], cw_ref[1:2, :], cw_ref[2:3, :]
        fg = fg_ref[...]

        def stage_a(rows):
            att_v, za, zc, gb = f32(att_ref, rows), f32(za_ref, rows), f32(zc_ref, rows), f32(gb_ref, rows)
            sa = _sigmoid(za)
            silu_a = za * sa
            vconv = w0 * cu2[rows, :] + w1 * cu1[rows, :] + w2 * cu[rows, :] + cb_ref[...]
            sc = _sigmoid(zc)
            silu_c = zc * sc
            return dict(att_v=att_v, za=za, zc=zc, gb=gb, sa=sa, silu_a=silu_a, vconv=vconv, sc=sc, silu_c=silu_c,
                        a_b=(att_v * silu_a).astype(BF16), c_b=(gb * vconv * silu_c).astype(BF16))

        st = both(stage_a)
        for t in st:
            t["y_att"] = _nn(t["a_b"], watt_ref[...])
            t["y_conv"] = _nn(t["c_b"], wconv_ref[...])
        for t, rows in zip(st, halves):
            gpair = lambda ref: jnp.concatenate([ref[0, rows, :], ref[1, rows, :]], axis=1).astype(F32)
            t["ga"] = _sigmoid(gpair(gatt_ref))
            t["gv"] = _sigmoid(gpair(gconv_ref))
            t["m_b"] = (t["ga"] * t["y_att"] + t["gv"] * t["y_conv"]).astype(BF16)
        for t in st:
            t["mo"] = _nn(t["m_b"], wout_ref[...])
        for t, rows in zip(st, halves):
            x2 = x_ref[rows, :] + t["mo"]
            r2 = lax.rsqrt(jnp.mean(x2 * x2, axis=-1, keepdims=True) + EPS)
            x2n = x2 * r2
            err = x2n * fg - t_ref[rows, :]
            loss_o[...] += jnp.sum(err * err, axis=0, keepdims=True) * (0.5 / d)
            dy = err * (1.0 / d)
            gfn_o[...] += jnp.sum(dy * x2n, axis=0, keepdims=True)
            dyn = dy * fg
            dx2 = r2 * (dyn - x2n * jnp.mean(dyn * x2n, axis=-1, keepdims=True))
            dx2_ref[rows, :] = dx2
            t["dx2_b"] = dx2.astype(BF16)
        for t in st:
            t["dm"] = _nn(t["dx2_b"], wout_t_ref[...])
        whole = lambda key: jnp.concatenate([st[0][key], st[1][key]], axis=0)
        acc_out[...] += _tn(whole("m_b"), whole("dx2_b"))
        for t, rows in zip(st, halves):
            dy_att = t["dm"] * t["ga"]
            dy_conv = t["dm"] * t["gv"]
            dpb_ref[rows, 5 * dc:5 * dc + d] = (dy_att * t["y_att"] * (1.0 - t["ga"])).astype(BF16)
            dpb_ref[rows, 5 * dc + d:5 * dc + 2 * d] = (dy_conv * t["y_conv"] * (1.0 - t["gv"])).astype(BF16)
            t["dya_b"] = dy_att.astype(BF16)
            t["dyc_b"] = dy_conv.astype(BF16)
        for t in st:
            t["da_in"] = _nn(t["dya_b"], watt_t_ref[...])
            t["dc_in"] = _nn(t["dyc_b"], wconv_t_ref[...])
        acc_att[...] += _tn(whole("a_b"), whole("dya_b"))
        acc_conv[...] += _tn(whole("c_b"), whole("dyc_b"))
        for t, rows in zip(st, halves):
            sa, za, sc, zc = t["sa"], t["za"], t["sc"], t["zc"]
            do_ref[rows, :] = (t["da_in"] * t["silu_a"]).astype(BF16)
            dpb_ref[rows, 0:dc] = (t["da_in"] * t["att_v"] * (sa * (1.0 + za * (1.0 - sa)))).astype(BF16)
            dpb_ref[rows, dc:2 * dc] = (t["dc_in"] * t["vconv"] * t["silu_c"]).astype(BF16)
            dgs = t["dc_in"] * t["gb"]
            t["dvc"] = dgs * t["silu_c"]
            dpb_ref[rows, 4 * dc:5 * dc] = (dgs * t["vconv"] * (sc * (1.0 + zc * (1.0 - sc)))).astype(BF16)
        dvc = whole("dvc")
        gcb_o[...] += jnp.sum(dvc, axis=0, keepdims=True)
        gcw_o[0:1, :] += jnp.sum(dvc * cu2, axis=0, keepdims=True)
        gcw_o[1:2, :] += jnp.sum(dvc * cu1, axis=0, keepdims=True)
        gcw_o[2:3, :] += jnp.sum(dvc * cu, axis=0, keepdims=True)
        nxt = carry[...]
        dcu = w2 * dvc + w1 * _shift_up(dvc, 1, nxt) + w0 * _shift_up(dvc, 2, nxt)
        carry[...] = dvc[0:SUBLANES, :]
        dpb_ref[:, 2 * dc:3 * dc] = (dcu * u).astype(BF16)
        dpb_ref[:, 3 * dc:4 * dc] = (dcu * gc).astype(BF16)

        @pl.when(i == n - 1)
        def _():
            for j in range(n_shards):
                gatt_o[j] = acc_att[:, j * (d // n_shards):(j + 1) * (d // n_shards)].astype(BF16)
                gconv_o[j] = acc_conv[:, j * (d // n_shards):(j + 1) * (d // n_shards)].astype(BF16)
                gout_o[j] = acc_out[j * (d // n_shards):(j + 1) * (d // n_shards), :].astype(BF16)

    rev = lambda width, col_blk: pl.BlockSpec((tm, width), lambda i: (n - 1 - i, col_blk))
    grp = lambda g: pl.BlockSpec((None, tm, dc), lambda i: (g, n - 1 - i, 0))
    grp2 = lambda g2: pl.BlockSpec((2, tm, dc), lambda i: (g2, n - 1 - i, 0))
    halo_spec = lambda g: pl.BlockSpec(
        (None, SUBLANES, dc), lambda i: (g, jnp.maximum((n - 1 - i) * (tm // SUBLANES) - 1, 0), 0))
    const = lambda shape: pl.BlockSpec(shape, lambda i: tuple(0 for _ in shape), pipeline_mode=pl.Buffered(1))
    q4 = d // n_shards
    return pl.pallas_call(
        body, name="mixer_mid", grid=(n,),
        in_specs=[rev(dc, 0), grp(3), grp(4), grp(5), grp(6), grp(7),
                  halo_spec(5), halo_spec(6), grp2(4), grp2(5), rev(d, 0), rev(d, 0),
                  const((dc, d)), const((dc, d)), const((d, d)),
                  const(conv_w.shape), const((1, dc)), const((1, d))],
        out_specs=[rev(5 * dc + 2 * d, 0), rev(dc, 0), rev(d, 0),
                   const((n_shards, dc, q4)), const((n_shards, dc, q4)), const((n_shards, q4, d)),
                   const((1, d)), const((1, d)), const((1, dc)), const((SUBLANES, dc))],
        out_shape=[jax.ShapeDtypeStruct((s, 5 * dc + 2 * d), BF16), jax.ShapeDtypeStruct((s, dc), BF16),
                   jax.ShapeDtypeStruct((s, d), F32),
                   jax.ShapeDtypeStruct((n_shards, dc, q4), BF16), jax.ShapeDtypeStruct((n_shards, dc, q4), BF16),
                   jax.ShapeDtypeStruct((n_shards, q4, d), BF16),
                   jax.ShapeDtypeStruct((1, d), F32), jax.ShapeDtypeStruct((1, d), F32),
                   jax.ShapeDtypeStruct((1, dc), F32), jax.ShapeDtypeStruct((SUBLANES, dc), F32)],
        scratch_shapes=[pltpu.VMEM((dc, d), F32), pltpu.VMEM((dc, d), F32), pltpu.VMEM((d, d), F32),
                        pltpu.VMEM((SUBLANES, dc), F32),
                        pltpu.VMEM((d, dc), BF16), pltpu.VMEM((d, dc), BF16), pltpu.VMEM((d, d), BF16)],
        compiler_params=_params("arbitrary"),
    )(att, proj, proj, proj, proj, proj, proj, proj, proj, proj, x, tgt,
      w_att, w_conv, w_out, conv_w, conv_b, fin_g)


def _in_proj_bwd_x(dqkv, dpb, w_in, x, dx2, g):
    s, d = x.shape
    tn = dqkv.shape[1]
    nb = dpb.shape[1] // tn
    n = s // TM_MM

    def body(*refs):
        dps, ws = refs[:nb + 1], refs[nb + 1:2 * nb + 2]
        x_ref, dx2_ref, g_ref, gx_ref, gng_ref = refs[2 * nb + 2:]
        i = pl.program_id(0)

        @pl.when(i == 0)
        def _():
            gng_ref[...] = jnp.zeros_like(gng_ref)

        dh = _nt(dps[0][...], ws[0][0])
        for j in range(1, nb + 1):
            dh = dh + _nt(dps[j][...], ws[j][0])
        xv = x_ref[...]
        r = lax.rsqrt(jnp.mean(xv * xv, axis=-1, keepdims=True) + EPS)
        xn = xv * r
        gng_ref[...] += jnp.sum(dh * xn, axis=0, keepdims=True)
        dhn = dh * g_ref[...]
        gx_ref[...] = dx2_ref[...] + r * (dhn - xn * jnp.mean(dhn * xn, axis=-1, keepdims=True))

    tile = lambda width, col_blk: pl.BlockSpec((TM_MM, width), lambda i: (i, col_blk))
    wspec = lambda blk: pl.BlockSpec((1, d, tn), lambda i: (blk, 0, 0), pipeline_mode=pl.Buffered(1))
    return pl.pallas_call(
        body, name="in_proj_bwd_x", grid=(n,),
        in_specs=[tile(tn, 0)] + [tile(tn, j) for j in range(nb)] + [wspec(j) for j in range(nb + 1)]
        + [tile(d, 0), tile(d, 0), pl.BlockSpec((1, d), lambda i: (0, 0))],
        out_specs=[tile(d, 0), pl.BlockSpec((1, d), lambda i: (0, 0))],
        out_shape=[jax.ShapeDtypeStruct((s, d), F32), jax.ShapeDtypeStruct((1, d), F32)],
        compiler_params=_params("arbitrary"),
    )(dqkv, *([dpb] * nb), *([w_in] * (nb + 1)), x, dx2, g)


def _in_proj_bwd_w(h, dqkv, dpb, order, slots, small):
    s, d = h.shape
    tn = dqkv.shape[1]
    n = s // TM_BLK
    hr = d // 2
    settle = min(1, n - 1)
    nw, ns = len(slots), len(small)
    half_rows = [sl.shape[1] for sl in slots] + [hr]

    def body(order_ref, h_ref, da_ref, db_ref, *refs):
        srcs, parts = refs[:nw], refs[nw:nw + ns]
        slots_ref, shards, sm_all = refs[nw + ns], refs[nw + ns + 1:2 * nw + ns + 2], refs[2 * nw + ns + 2]
        acc, sendbuf, pairbuf, chipbuf = refs[2 * nw + ns + 3:2 * nw + ns + 7]
        halves, sm = refs[2 * nw + ns + 7:3 * nw + ns + 7], refs[3 * nw + ns + 7]
        psend, precv, send, recv, lsem, hsend, hrecv, ssend, srecv, hsem = refs[3 * nw + ns + 8:]
        j, i = pl.program_id(0), pl.program_id(1)
        blk = order_ref[j]
        pos = _position()
        x, y, c = pos
        me = 4 * x + 2 * y + c

        def half_sum(w):
            return halves[w] if w < nw else acc.at[pl.ds(0, hr), :]

        def half_of_shard(w, half):
            return shards[w].at[pl.ds(half * half_rows[w], half_rows[w]), :]

        def exchange(w, half):
            return pltpu.make_async_remote_copy(
                src_ref=half_sum(w), dst_ref=half_of_shard(w, half), send_sem=hsend.at[w], recv_sem=hrecv.at[w],
                device_id=(x, y, 1 - c), device_id_type=MESH)

        def keep(w):
            return pltpu.make_async_copy(half_sum(w), half_of_shard(w, c), hsem.at[w])

        def bcast(k, slot):
            return pltpu.make_async_remote_copy(
                src_ref=sm, dst_ref=sm_all.at[slot], send_sem=ssend.at[k], recv_sem=srecv.at[k],
                device_id=_peer(pos, k), device_id_type=MESH)

        own_small = pltpu.make_async_copy(sm, sm_all.at[me], hsem.at[nw + 1])

        @pl.when((j == 0) & (i == 0))
        def _():
            _pack_small(sm, *parts)
            own_small.start()
            for k in range(1, N_DEV):
                bcast(k, me).start()
            for w in range(nw):
                total = srcs[w][0].astype(F32)
                for k in range(1, srcs[w].shape[0]):
                    total = total + srcs[w][k].astype(F32)
                halves[w][...] = total
                exchange(w, c).start()
                keep(w).start()

        @pl.when(i == 0)
        def _():
            acc[...] = jnp.zeros_like(acc)

        @pl.when(blk == 0)
        def _():
            acc[...] += _tn(h_ref[...], da_ref[...])

        @pl.when(blk > 0)
        def _():
            acc[...] += _tn(h_ref[...], db_ref[...])

        def pair(step, half):
            return pltpu.make_async_remote_copy(
                src_ref=sendbuf.at[step, pl.ds(half * hr, hr), :], dst_ref=pairbuf.at[step],
                send_sem=psend.at[step], recv_sem=precv.at[step], device_id=(x, y, 1 - c), device_id_type=MESH)

        def ici(step):
            flip = OWNER_FLIPS[step]
            return pltpu.make_async_remote_copy(
                src_ref=chipbuf.at[step], dst_ref=slots_ref.at[flip], send_sem=send.at[step], recv_sem=recv.at[step],
                device_id=_peer(pos, 4 * (flip >> 1) + 2 * (flip & 1)), device_id_type=MESH)

        local = pltpu.make_async_copy(chipbuf.at[N_CHIPS - 1], slots_ref.at[0], lsem.at[0])

        def combine(step):
            pair(step, c).wait_recv()
            mine = sendbuf[step, pl.ds(c * hr, hr), :].astype(F32)
            chipbuf[step] = (mine + pairbuf[step].astype(F32)).astype(BF16)

        for step in range(N_CHIPS):
            @pl.when((j == step) & (i == n - 1))
            def _(step=step):
                sendbuf[step] = acc[...].astype(BF16)
                pair(step, 1 - c).start()

        for step in range(N_CHIPS - 1):
            @pl.when((j == step + 1) & (i == settle))
            def _(step=step):
                combine(step)
                ici(step).start()

        @pl.when((j == N_CHIPS - 1) & (i == n - 1))
        def _():
            combine(N_CHIPS - 1)
            local.start()
            staged = [pltpu.make_async_copy(slots_ref.at[f], pairbuf.at[f - 1], hsem.at[nw + 1 + f])
                      for f in range(1, N_CHIPS)]
            for step in range(N_CHIPS - 1):
                ici(step).wait_recv()
            for cp in staged:
                cp.start()
            for cp in staged:
                cp.wait()
            total = chipbuf[N_CHIPS - 1].astype(F32)
            for f in range(1, N_CHIPS):
                total = total + pairbuf[f - 1].astype(F32)
            acc[0:hr, :] = total
            exchange(nw, c).start()
            keep(nw).start()
            for w in range(nw + 1):
                exchange(w, 1 - c).wait_recv()
            for k in range(1, N_DEV):
                px, py, pc = _peer(pos, k)
                bcast(k, 4 * px + 2 * py + pc).wait_recv()
            for step in range(N_CHIPS - 1):
                ici(step).wait_send()
            for step in range(N_CHIPS):
                pair(step, 1 - c).wait_send()
            local.wait()
            for w in range(nw + 1):
                exchange(w, c).wait_send()
                keep(w).wait()
            for k in range(1, N_DEV):
                bcast(k, me).wait_send()
            own_small.wait()

    vmem = pl.BlockSpec(memory_space=pltpu.VMEM)
    shard_shapes = [(2 * sl.shape[1], sl.shape[2]) for sl in slots]
    out = pl.pallas_call(
        body, name="in_proj_bwd_w",
        grid_spec=pltpu.PrefetchScalarGridSpec(
            num_scalar_prefetch=1, grid=(N_CHIPS, n),
            in_specs=[pl.BlockSpec((TM_BLK, d), lambda j, i, order: (i, 0)),
                      pl.BlockSpec((TM_BLK, tn), lambda j, i, order: (jnp.where(order[j] == 0, i, 0), 0)),
                      pl.BlockSpec((TM_BLK, tn), lambda j, i, order: (jnp.where(order[j] == 0, 0, i),
                                                                     jnp.maximum(order[j] - 1, 0)))]
            + [vmem] * (nw + ns),
            out_specs=[ANY] * (nw + 3),
            scratch_shapes=[pltpu.VMEM((d, tn), F32), pltpu.VMEM((N_CHIPS, d, tn), BF16),
                            pltpu.VMEM((N_CHIPS, hr, tn), BF16), pltpu.VMEM((N_CHIPS, hr, tn), BF16)]
            + [pltpu.VMEM((r // 2, cc), F32) for r, cc in shard_shapes] + [pltpu.VMEM((SMALL_ROWS, SMALL_COLS), F32)]
            + [pltpu.SemaphoreType.DMA((N_CHIPS,)), pltpu.SemaphoreType.DMA((N_CHIPS,)),
               pltpu.SemaphoreType.DMA((N_CHIPS - 1,)), pltpu.SemaphoreType.DMA((N_CHIPS - 1,)),
               pltpu.SemaphoreType.DMA((1,)),
               pltpu.SemaphoreType.DMA((nw + 1,)), pltpu.SemaphoreType.DMA((nw + 1,)),
               pltpu.SemaphoreType.DMA((N_DEV,)), pltpu.SemaphoreType.DMA((N_DEV,)),
               pltpu.SemaphoreType.DMA((nw + 1 + N_CHIPS,))]),
        out_shape=[jax.ShapeDtypeStruct((N_CHIPS, hr, tn), BF16)]
        + [jax.ShapeDtypeStruct(sh, F32) for sh in shard_shapes + [(d, tn)]]
        + [jax.ShapeDtypeStruct((N_DEV, SMALL_ROWS, SMALL_COLS), F32)],
        compiler_params=_params("arbitrary", "arbitrary"),
    )(order, h, dqkv, dpb, *slots, *small)
    return [out[nw + 1]] + list(out[1:nw + 1]) + [out[nw + 2]]


LOSS_ROW = 6


def _adam_update(w, g, m, v):
    c1 = 1.0 / (1.0 - ADAM_B1 ** ADAM_STEP)
    c2 = 1.0 / (1.0 - ADAM_B2 ** ADAM_STEP)
    m2 = ADAM_B1 * m + (1.0 - ADAM_B1) * g
    v2 = ADAM_B2 * v + (1.0 - ADAM_B2) * (g * g)
    return -ADAM_LR * ((m2 * c1) / (jnp.sqrt(v2 * c2) + ADAM_EPS) + ADAM_WD * w), m2, v2


def _adamw_small(recv, params, moments_m, moments_v, out_shapes):
    k = recv.shape[0]
    n_par = len(params)
    cshard = params[3].shape[1]

    def body(*refs):
        r_ref = refs[0]
        ws, ms, vs = refs[1:1 + n_par], refs[1 + n_par:1 + 2 * n_par], refs[1 + 2 * n_par:1 + 3 * n_par]
        loss_ref = refs[1 + 3 * n_par]
        outs = refs[2 + 3 * n_par:]
        total = r_ref[0]
        for slot in range(1, k):
            total = total + r_ref[slot]
        loss_ref[...] = jnp.sum(total[LOSS_ROW:LOSS_ROW + 1, :], axis=1, keepdims=True)
        chip = 2 * lax.axis_index("x") + lax.axis_index("y")
        g_cw = jnp.zeros((3, cshard), F32)
        for sh in range(N_CHIPS):
            g_cw = g_cw + jnp.where(chip == sh, total[3:6, sh * cshard:(sh + 1) * cshard], 0.0)
        grads = [total[0:1, :], total[1:2, :], total[2:3, :ws[2].shape[1]], g_cw, total[8:16, :ws[4].shape[1]]]
        for p in range(n_par):
            delta, m2, v2 = _adam_update(ws[p][...], grads[p], ms[p][...], vs[p][...])
            for q, val in enumerate((grads[p], delta, m2, v2)):
                out = outs[4 * p + q]
                out[...] = val[0] if len(out.shape) == 1 else val

    shapes = [jax.ShapeDtypeStruct((1, 1), F32)]
    for shape in out_shapes:
        shapes += [jax.ShapeDtypeStruct(shape, F32)] * 4
    return pl.pallas_call(body, name="adamw_small", out_shape=shapes)(recv, *params, *moments_m, *moments_v)


def _adamw_group(ws, gs, ms, vs, name, steps):
    k = len(ws)

    def body(*refs):
        ins, outs = refs[:4 * k], refs[4 * k:]
        for j in range(k):
            gv = ins[k + j][...]
            outs[4 * j][...] = gv
            outs[4 * j + 1][...], outs[4 * j + 2][...], outs[4 * j + 3][...] = _adam_update(
                ins[j][...], gv, ins[2 * k + j][...], ins[3 * k + j][...])

    specs = [pl.BlockSpec((w.shape[0] // steps, w.shape[1]), lambda i: (i, 0)) for w in ws]
    out = pl.pallas_call(
        body, name=name, grid=(steps,),
        in_specs=specs * 4, out_specs=[spec for spec in specs for _ in range(4)],
        out_shape=[jax.ShapeDtypeStruct(w.shape, F32) for w in ws for _ in range(4)],
        compiler_params=_params("parallel"),
    )(*ws, *gs, *ms, *vs)
    return [tuple(out[4 * j:4 * j + 4]) for j in range(k)]


ANY = pl.BlockSpec(memory_space=pl.ANY)
N_CHIPS = 4
N_DEV = 8
OWNER_FLIPS = (3, 2, 1, 0)


def _position():
    return lax.axis_index("x"), lax.axis_index("y"), lax.axis_index("c")


def _gather_out_shapes(shards, kinds, cw3):
    full = [(a.shape[0], a.shape[1] * N_CHIPS) if k == "cols" else (a.shape[0] * N_CHIPS, a.shape[1])
            for a, k in zip(shards, kinds)]
    return [jax.ShapeDtypeStruct(f, BF16) for f in full] + [
        jax.ShapeDtypeStruct((N_CHIPS,) + cw3.shape, cw3.dtype)]


def _gather_sems(nw):
    return [pltpu.SemaphoreType.DMA((3, nw)), pltpu.SemaphoreType.DMA((3, nw)),
            pltpu.SemaphoreType.DMA((3, nw)), pltpu.SemaphoreType.DMA((3, nw)),
            pltpu.SemaphoreType.DMA((3,)), pltpu.SemaphoreType.DMA((3,)), pltpu.SemaphoreType.DMA((nw + 1,))]


def _gather_plan(kinds, srcs, dsts, cw, cw_all, send1, recv1, send2, recv2, ssend, srecv, lsem):
    nw = len(srcs)
    x, y, c = _position()
    mine = 2 * x + y
    chips = [(x, 1 - y), (1 - x, y), (1 - x, 1 - y)]

    def window(w, shard, half):
        r, cc = srcs[w].shape
        hr = r // 2
        if kinds[w] == "cols":
            rows = pl.ds(0, r) if half is None else pl.ds(half * hr, hr)
            return dsts[w].at[rows, pl.ds(shard * cc, cc)]
        rows = pl.ds(shard * r, r) if half is None else pl.ds(shard * r + half * hr, hr)
        return dsts[w].at[rows, :]

    def my_half(w):
        hr = srcs[w].shape[0] // 2
        return srcs[w].at[pl.ds(c * hr, hr), :]

    def local():
        return [pltpu.make_async_copy(srcs[w], window(w, mine, None), lsem.at[w]) for w in range(nw)] + [
            pltpu.make_async_copy(cw, cw_all.at[mine], lsem.at[nw])]

    def ici(k, w, shard):
        kx, ky = chips[k]
        return pltpu.make_async_remote_copy(
            src_ref=my_half(w), dst_ref=window(w, shard, c), send_sem=send1.at[k, w], recv_sem=recv1.at[k, w],
            device_id=(kx, ky, c), device_id_type=MESH)

    def d2d(k, w, shard, half):
        return pltpu.make_async_remote_copy(
            src_ref=window(w, shard, half), dst_ref=window(w, shard, half),
            send_sem=send2.at[k, w], recv_sem=recv2.at[k, w], device_id=(x, y, 1 - c), device_id_type=MESH)

    def small(k, shard):
        kx, ky = chips[k]
        return pltpu.make_async_remote_copy(
            src_ref=cw, dst_ref=cw_all.at[shard], send_sem=ssend.at[k], recv_sem=srecv.at[k],
            device_id=(kx, ky, c), device_id_type=MESH)

    def theirs(k):
        kx, ky = chips[k]
        return 2 * kx + ky

    def start():
        for cp in local():
            cp.start()
        for k in range(3):
            for w in range(nw):
                ici(k, w, mine).start()
            small(k, mine).start()

    def forward():
        for k in range(3):
            for w in range(nw):
                ici(k, w, theirs(k)).wait_recv()
                d2d(k, w, theirs(k), c).start()

    def finish():
        for k in range(3):
            for w in range(nw):
                d2d(k, w, theirs(k), 1 - c).wait_recv()
            small(k, theirs(k)).wait_recv()
        for k in range(3):
            for w in range(nw):
                ici(k, w, mine).wait_send()
                d2d(k, w, theirs(k), c).wait_send()
            small(k, mine).wait_send()
        for cp in local():
            cp.wait()

    return start, forward, finish


def _scatter_out_shapes(parts):
    return [jax.ShapeDtypeStruct((N_DEV, p.shape[1] // 2, p.shape[2]), p.dtype) for p in parts]


def _scatter_sems(nw):
    return [pltpu.SemaphoreType.DMA((N_DEV, nw)), pltpu.SemaphoreType.DMA((N_DEV, nw)), pltpu.SemaphoreType.DMA((nw,))]


def _peer(pos, k):
    x, y, c = pos
    return ((1 - x) if k & 4 else x, (1 - y) if k & 2 else y, (1 - c) if k & 1 else c)


def _scatter_plan(srcs, dsts, send, recv, lsem):
    nw = len(srcs)
    pos = _position()

    def piece(w, k):
        px, py, pc = _peer(pos, k)
        hr = srcs[w].shape[1] // 2
        return srcs[w].at[2 * px + py, pl.ds(pc * hr, hr), :]

    def remote(w, k):
        return pltpu.make_async_remote_copy(
            src_ref=piece(w, k), dst_ref=dsts[w].at[k], send_sem=send.at[k, w], recv_sem=recv.at[k, w],
            device_id=_peer(pos, k), device_id_type=MESH)

    def local(w):
        return pltpu.make_async_copy(piece(w, 0), dsts[w].at[0], lsem.at[w])

    def start():
        for w in range(nw):
            local(w).start()
        for k in range(1, N_DEV):
            for w in range(nw):
                remote(w, k).start()

    def finish():
        for k in range(1, N_DEV):
            for w in range(nw):
                remote(w, k).wait_recv()
        for k in range(1, N_DEV):
            for w in range(nw):
                remote(w, k).wait_send()
        for w in range(nw):
            local(w).wait()

    return start, finish


def _pack_small(sm, norm_g, fin_g, conv_b, conv_w, loss_vec, rel):
    sm[...] = jnp.zeros_like(sm)
    for row, ref in ((0, norm_g), (1, fin_g), (2, conv_b), (LOSS_ROW, loss_vec)):
        sm[row:row + 1, 0:ref.shape[1]] = ref[...]
    sm[3:6, 0:conv_w.shape[1]] = conv_w[0:3, :]
    sm[8:8 + HEADS, 0:rel.shape[1]] = rel[...]


def kernel(x, norm_g, w_in, rel_bias, w_att_out, conv_w, conv_b, w_conv_out, w_out, final_norm_g, loss_target, m_norm_g, m_w_in, m_rel_bias, m_w_att_out, m_conv_w, m_conv_b, m_w_conv_out, m_w_out, m_final_norm_g, v_norm_g, v_w_in, v_rel_bias, v_w_att_out, v_conv_w, v_conv_b, v_w_conv_out, v_w_out, v_final_norm_g):
    xs, tgt = x[0], loss_target[0]
    cshard = conv_w.shape[2]
    chip = 2 * lax.axis_index("x") + lax.axis_index("y")

    shards = [w_in[0], w_att_out[0], w_conv_out[0], w_out[0]]
    cw3 = conv_w[0]
    flips = jnp.arange(N_CHIPS, dtype=jnp.int32)
    own_first = jnp.bitwise_xor(chip, flips)
    own_last = jnp.bitwise_xor(chip, jnp.asarray(OWNER_FLIPS, jnp.int32))

    proj, h, wb_in = _in_proj_gather(xs, norm_g, shards[0], own_first)
    diag = jnp.take(rel_bias[0], _diag_rel_index(), axis=1)
    att, lse, wb_att, wb_conv, wb_out, cw_all = _attn_fwd(diag, proj, shards[1:], ["cols", "cols", "rows"], cw3)
    conv_w_full = jnp.transpose(cw_all, (1, 0, 2)).reshape(cw3.shape[0], N_CHIPS * cshard)
    (dpb, d_att, dx2, g_att_p, g_conv_p, g_out_p, loss_vec, g_fin, g_cb, g_cw) = _mixer_mid(
        att, proj, xs, tgt, wb_att, wb_conv, wb_out, conv_w_full, conv_b,
        final_norm_g[None, :])
    dqkv, g_rel, r_att, r_conv, r_out = _attn_bwd(diag, proj, d_att, att, lse, [g_att_p, g_conv_p, g_out_p])
    grad_x, g_norm = _in_proj_bwd_x(dqkv, dpb, wb_in, xs, dx2, norm_g)
    gw_in, gw_att, gw_conv, gw_out, r_small = _in_proj_bwd_w(
        h, dqkv, dpb, own_last, [r_att, r_conv, r_out], [g_norm, g_fin, g_cb, g_cw, loss_vec, g_rel])

    small_out = _adamw_small(
        r_small,
        [norm_g, final_norm_g[None, :], conv_b, conv_w[0], rel_bias[0]],
        [m_norm_g, m_final_norm_g[None, :], m_conv_b, m_conv_w[0], m_rel_bias[0]],
        [v_norm_g, v_final_norm_g[None, :], v_conv_b, v_conv_w[0], v_rel_bias[0]],
        [norm_g.shape, final_norm_g.shape, conv_b.shape, conv_w[0].shape, rel_bias[0].shape])
    loss = small_out[0][0, 0]
    small_names = ["norm_g", "final_norm_g", "conv_b", "conv_w", "rel_bias"]
    fix = {"norm_g": lambda a: a, "final_norm_g": lambda a: a, "conv_b": lambda a: a,
           "conv_w": lambda a: a[None], "rel_bias": lambda a: a[None]}
    small_res = {name: [fix[name](small_out[1 + 4 * p + q]) for q in range(4)] for p, name in enumerate(small_names)}

    names = ["w_in", "w_att_out", "w_conv_out", "w_out"]
    group = _adamw_group([w_in[0], w_att_out[0], w_conv_out[0], w_out[0]], [gw_in, gw_att, gw_conv, gw_out],
                         [m_w_in[0], m_w_att_out[0], m_w_conv_out[0], m_w_out[0]],
                         [v_w_in[0], v_w_att_out[0], v_w_conv_out[0], v_w_out[0]], "adamw_matrices", ADAMW_STEPS)
    big = dict(zip(names, group))
    big = {name: tuple(a[None] for a in four) for name, four in big.items()}

    order = ["norm_g", "w_in", "rel_bias", "w_att_out", "conv_w", "conv_b", "w_conv_out", "w_out", "final_norm_g"]
    outs = [loss, grad_x[None]]
    for which in range(4):
        for name in order:
            outs.append(big[name][which] if name in big else small_res[name][which])
    return tuple(outs)
```
